```python
import jax, jax.numpy as jnp
from jax import lax
import numpy as np

D_MODEL = 1024
BATCH = 8
SEQ = 4096
DEPTH = 1

CHUNK = 64
N_HEADS_SB = 8
HEAD_DIM_SB = 64
D_SB = N_HEADS_SB * HEAD_DIM_SB
N_GROUPS_SGU = 8
GROUP_DIM_SGU = 64
D_SGU = N_GROUPS_SGU * GROUP_DIM_SGU
SGU_CHUNK = 128
Q_BLOCK = 128
EPS = 1e-6
IN_WIDTHS = (D_SB, D_SB, D_SB, D_SB, D_SGU, D_SGU, D_SGU, D_MODEL, D_MODEL)
D_IN = 4 * D_SB + 3 * D_SGU + 2 * D_MODEL

kernel_name = 'stickbreak_sgu_gated_hybrid'


def rmsnorm(x, g):
    xf = x.astype(jnp.float32)
    y = xf * lax.rsqrt(jnp.mean(xf * xf, axis=-1, keepdims=True) + EPS)
    return (y * g.astype(jnp.float32)).astype(x.dtype)


def split_points():
    pts, acc = [], 0
    for w in IN_WIDTHS[:-1]:
        acc += w
        pts.append(acc)
    return pts


def stick_breaking_attention(q, k, v):
    b, s, h, dh = q.shape
    scale = dh ** -0.5
    qf = q.astype(jnp.float32).transpose(0, 2, 1, 3) * scale
    kf = k.astype(jnp.float32).transpose(0, 2, 1, 3)
    vf = v.astype(jnp.float32).transpose(0, 2, 1, 3)
    outs = []
    for start in range(0, s, Q_BLOCK):
        end = start + Q_BLOCK
        qb = qf[:, :, start:end]
        kb = kf[:, :, :end]
        vb = vf[:, :, :end]
        z = jnp.einsum('bhqd,bhkd->bhqk', qb, kb)
        t_idx = start + jnp.arange(Q_BLOCK)[:, None]
        s_idx = jnp.arange(end)[None, :]
        before = s_idx < t_idx
        log_keep = jnp.where(before, jax.nn.log_sigmoid(-z), 0.0)
        log_stick = lax.cumsum(log_keep, axis=3, reverse=True) - log_keep
        log_w = jax.nn.log_sigmoid(z) + log_stick
        w = jnp.where(before, jnp.exp(log_w), 0.0)
        outs.append(jnp.einsum('bhqk,bhkd->bhqd', w, vb))
    o = jnp.concatenate(outs, axis=2)
    return o.transpose(0, 2, 1, 3).astype(q.dtype)


def spatial_gating(u, v, ln_g, ln_b, w_s, b_s):
    b, s, g, c = v.shape
    vf = v.astype(jnp.float32)
    mu = jnp.mean(vf, axis=-1, keepdims=True)
    var = jnp.mean(jnp.square(vf - mu), axis=-1, keepdims=True)
    vn = (vf - mu) * lax.rsqrt(var + EPS) * ln_g.astype(jnp.float32) + ln_b.astype(jnp.float32)
    vn = vn.reshape(b, s // SGU_CHUNK, SGU_CHUNK, g, c)
    pos = jnp.arange(SGU_CHUNK)
    mask = (pos[None, :] // CHUNK) <= (pos[:, None] // CHUNK)
    w = jnp.where(mask[None], w_s.astype(jnp.float32), 0.0)
    mixed = jnp.einsum('gts,bnsgc->bntgc', w, vn) + b_s.astype(jnp.float32).T[None, None, :, :, None]
    return u * mixed.reshape(b, s, g, c).astype(u.dtype)


def _fwd_setup_inputs(seed: int = 0) -> dict:
    key = jax.random.key(seed)
    ks = jax.random.split(key, 13)
    f32 = jnp.float32
    x = jax.random.normal(ks[0], (BATCH, SEQ, D_MODEL), f32)
    norm_g = 1.0 + 0.1 * jax.random.normal(ks[1], (DEPTH, D_MODEL), f32)
    w_in = jax.random.normal(ks[2], (DEPTH, D_MODEL, D_IN), f32) * D_MODEL ** -0.5
    sgu_ln_g = 1.0 + 0.1 * jax.random.normal(ks[3], (DEPTH, N_GROUPS_SGU, GROUP_DIM_SGU), f32)
    sgu_ln_b = 0.1 * jax.random.normal(ks[4], (DEPTH, N_GROUPS_SGU, GROUP_DIM_SGU), f32)
    w_spatial = jax.random.normal(ks[5], (DEPTH, N_GROUPS_SGU, SGU_CHUNK, SGU_CHUNK), f32) * SGU_CHUNK ** -0.5
    b_spatial = 1.0 + 0.1 * jax.random.normal(ks[6], (DEPTH, N_GROUPS_SGU, SGU_CHUNK), f32)
    w_up_a = jax.random.normal(ks[7], (DEPTH, D_SB, D_MODEL), f32) * D_SB ** -0.5
    w_up_b = jax.random.normal(ks[8], (DEPTH, D_SGU, D_MODEL), f32) * D_SGU ** -0.5
    w_out = jax.random.normal(ks[9], (DEPTH, D_MODEL, D_MODEL), f32) * D_MODEL ** -0.5
    final_norm_g = 1.0 + 0.1 * jax.random.normal(ks[10], (D_MODEL,), f32)
    return {'x': x, 'norm_g': norm_g, 'w_in': w_in, 'sgu_ln_g': sgu_ln_g, 'sgu_ln_b': sgu_ln_b,
            'w_spatial': w_spatial, 'b_spatial': b_spatial, 'w_up_a': w_up_a, 'w_up_b': w_up_b,
            'w_out': w_out, 'final_norm_g': final_norm_g}


def _fwd_reference(x, norm_g, w_in, sgu_ln_g, sgu_ln_b, w_spatial, b_spatial, w_up_a, w_up_b, w_out, final_norm_g):
    b, s, _ = x.shape
    pts = split_points()
    for l in range(DEPTH):
        h = rmsnorm(x, norm_g[l])
        proj = jnp.einsum('bsd,de->bse', h, w_in[l])
        q, k, v, z_a, u_b, v_b, z_b, g_a, g_b = jnp.split(proj, pts, axis=-1)
        y_a = stick_breaking_attention(
            q.reshape(b, s, N_HEADS_SB, HEAD_DIM_SB),
            k.reshape(b, s, N_HEADS_SB, HEAD_DIM_SB),
            v.reshape(b, s, N_HEADS_SB, HEAD_DIM_SB)).reshape(b, s, D_SB) * jax.nn.silu(z_a)
        y_b = spatial_gating(
            jax.nn.gelu(u_b).reshape(b, s, N_GROUPS_SGU, GROUP_DIM_SGU),
            jax.nn.gelu(v_b).reshape(b, s, N_GROUPS_SGU, GROUP_DIM_SGU),
            sgu_ln_g[l], sgu_ln_b[l], w_spatial[l], b_spatial[l]).reshape(b, s, D_SGU) * jax.nn.silu(z_b)
        p_a = jnp.einsum('bse,ed->bsd', y_a, w_up_a[l])
        p_b = jnp.einsum('bse,ed->bsd', y_b, w_up_b[l])
        merged = jax.nn.sigmoid(g_a) * p_a + jax.nn.sigmoid(g_b) * p_b
        x = x + jnp.einsum('bsd,de->bse', merged, w_out[l])
    return rmsnorm(x, final_norm_g)


import jax as _jax
import jax.numpy as _jnp

TWIN_FORMAT = 'train_step'
FWD_PARAMS = ['x', 'norm_g', 'w_in', 'sgu_ln_g', 'sgu_ln_b', 'w_spatial', 'b_spatial', 'w_up_a', 'w_up_b', 'w_out', 'final_norm_g']
TWIN_WEIGHTS = ['norm_g', 'w_in', 'sgu_ln_g', 'sgu_ln_b', 'w_spatial', 'b_spatial', 'w_up_a', 'w_up_b', 'w_out', 'final_norm_g']
TWIN_DIFF_INPUT = 'x'
TWIN_INPUTS = ['x', 'norm_g', 'w_in', 'sgu_ln_g', 'sgu_ln_b', 'w_spatial', 'b_spatial', 'w_up_a', 'w_up_b', 'w_out', 'final_norm_g', 'loss_target', 'm_norm_g', 'm_w_in', 'm_sgu_ln_g', 'm_sgu_ln_b', 'm_w_spatial', 'm_b_spatial', 'm_w_up_a', 'm_w_up_b', 'm_w_out', 'm_final_norm_g', 'v_norm_g', 'v_w_in', 'v_sgu_ln_g', 'v_sgu_ln_b', 'v_w_spatial', 'v_b_spatial', 'v_w_up_a', 'v_w_up_b', 'v_w_out', 'v_final_norm_g']
TWIN_OUTPUTS = ['loss', 'grad_x', 'grad_norm_g', 'grad_w_in', 'grad_sgu_ln_g', 'grad_sgu_ln_b', 'grad_w_spatial', 'grad_b_spatial', 'grad_w_up_a', 'grad_w_up_b', 'grad_w_out', 'grad_final_norm_g', 'delta_norm_g', 'delta_w_in', 'delta_sgu_ln_g', 'delta_sgu_ln_b', 'delta_w_spatial', 'delta_b_spatial', 'delta_w_up_a', 'delta_w_up_b', 'delta_w_out', 'delta_final_norm_g', 'new_m_norm_g', 'new_m_w_in', 'new_m_sgu_ln_g', 'new_m_sgu_ln_b', 'new_m_w_spatial', 'new_m_b_spatial', 'new_m_w_up_a', 'new_m_w_up_b', 'new_m_w_out', 'new_m_final_norm_g', 'new_v_norm_g', 'new_v_w_in', 'new_v_sgu_ln_g', 'new_v_sgu_ln_b', 'new_v_w_spatial', 'new_v_b_spatial', 'new_v_w_up_a', 'new_v_w_up_b', 'new_v_w_out', 'new_v_final_norm_g']
TWIN_LEAF_KINDS = {'loss': 'loss', 'grad_x': 'grad_x', 'grad_norm_g': 'grad_w', 'grad_w_in': 'grad_w', 'grad_sgu_ln_g': 'grad_w', 'grad_sgu_ln_b': 'grad_w', 'grad_w_spatial': 'grad_w', 'grad_b_spatial': 'grad_w', 'grad_w_up_a': 'grad_w', 'grad_w_up_b': 'grad_w', 'grad_w_out': 'grad_w', 'grad_final_norm_g': 'grad_w', 'delta_norm_g': 'delta_w', 'delta_w_in': 'delta_w', 'delta_sgu_ln_g': 'delta_w', 'delta_sgu_ln_b': 'delta_w', 'delta_w_spatial': 'delta_w', 'delta_b_spatial': 'delta_w', 'delta_w_up_a': 'delta_w', 'delta_w_up_b': 'delta_w', 'delta_w_out': 'delta_w', 'delta_final_norm_g': 'delta_w', 'new_m_norm_g': 'new_m', 'new_m_w_in': 'new_m', 'new_m_sgu_ln_g': 'new_m', 'new_m_sgu_ln_b': 'new_m', 'new_m_w_spatial': 'new_m', 'new_m_b_spatial': 'new_m', 'new_m_w_up_a': 'new_m', 'new_m_w_up_b': 'new_m', 'new_m_w_out': 'new_m', 'new_m_final_norm_g': 'new_m', 'new_v_norm_g': 'new_v', 'new_v_w_in': 'new_v', 'new_v_sgu_ln_g': 'new_v', 'new_v_sgu_ln_b': 'new_v', 'new_v_w_spatial': 'new_v', 'new_v_b_spatial': 'new_v', 'new_v_w_up_a': 'new_v', 'new_v_w_up_b': 'new_v', 'new_v_w_out': 'new_v', 'new_v_final_norm_g': 'new_v'}


def _forward(args):
    return _fwd_reference(*[args[k] for k in FWD_PARAMS])


def _output_shape():
    out = _jax.eval_shape(lambda: _forward(_fwd_setup_inputs(0)))
    return out.shape, out.dtype

N_MICROBATCH = 1
ADAM_LR = 0.001
ADAM_B1 = 0.9
ADAM_B2 = 0.999
ADAM_EPS = 1e-08
ADAM_WD = 0.01
ADAM_STEP = 10
PER_EXAMPLE_BATCH_AXIS = {'x': 0, 'loss_target': 0}
SHARED_INPUTS = []
_WEIGHT_DTYPES = {'norm_g': _jnp.float32, 'w_in': _jnp.float32, 'sgu_ln_g': _jnp.float32, 'sgu_ln_b': _jnp.float32, 'w_spatial': _jnp.float32, 'b_spatial': _jnp.float32, 'w_up_a': _jnp.float32, 'w_up_b': _jnp.float32, 'w_out': _jnp.float32, 'final_norm_g': _jnp.float32}
MOMENT_SCALE = {'norm_g': 1.038361e-01, 'w_in': 4.324731e-02, 'sgu_ln_g': 4.954246e-02, 'sgu_ln_b': 4.535404e-02, 'w_spatial': 3.136713e-02, 'b_spatial': 3.977120e-02, 'w_up_a': 3.661656e-02, 'w_up_b': 4.915944e-02, 'w_out': 6.196143e-02, 'final_norm_g': 3.218584e+01}


def _to_microbatches(a, axis):
    t = _jnp.moveaxis(a, axis, 0)
    t = t.reshape((N_MICROBATCH, t.shape[0] // N_MICROBATCH) + t.shape[1:])
    return _jnp.moveaxis(t, 1, axis + 1)


def setup_inputs(seed: int = 0) -> dict:
    inp = _fwd_setup_inputs(seed)
    key = _jax.random.fold_in(_jax.random.key(seed), 7919)
    shape, _ = _output_shape()
    out = dict(inp)
    out["loss_target"] = _jax.random.normal(_jax.random.fold_in(key, 0), shape, _jnp.float32)
    for i, name in enumerate(TWIN_WEIGHTS):
        w = inp[name].astype(_jnp.float32)
        if MOMENT_SCALE is None:
            s = _jnp.sqrt(_jnp.mean(_jnp.square(w)) + 1e-30)
        else:
            s = MOMENT_SCALE[name]
        km, kv = _jax.random.split(_jax.random.fold_in(key, i + 1))
        out[name] = w
        out["m_" + name] = s * _jax.random.normal(km, w.shape, _jnp.float32)
        out["v_" + name] = (s * s) * _jax.random.uniform(kv, w.shape, _jnp.float32, 0.5, 1.5)
    if N_MICROBATCH > 1:
        for name, axis in PER_EXAMPLE_BATCH_AXIS.items():
            out[name] = _to_microbatches(out[name], axis)
    return {'x': out['x'], 'norm_g': out['norm_g'], 'w_in': out['w_in'], 'sgu_ln_g': out['sgu_ln_g'], 'sgu_ln_b': out['sgu_ln_b'], 'w_spatial': out['w_spatial'], 'b_spatial': out['b_spatial'], 'w_up_a': out['w_up_a'], 'w_up_b': out['w_up_b'], 'w_out': out['w_out'], 'final_norm_g': out['final_norm_g'], 'loss_target': out['loss_target'], 'm_norm_g': out['m_norm_g'], 'm_w_in': out['m_w_in'], 'm_sgu_ln_g': out['m_sgu_ln_g'], 'm_sgu_ln_b': out['m_sgu_ln_b'], 'm_w_spatial': out['m_w_spatial'], 'm_b_spatial': out['m_b_spatial'], 'm_w_up_a': out['m_w_up_a'], 'm_w_up_b': out['m_w_up_b'], 'm_w_out': out['m_w_out'], 'm_final_norm_g': out['m_final_norm_g'], 'v_norm_g': out['v_norm_g'], 'v_w_in': out['v_w_in'], 'v_sgu_ln_g': out['v_sgu_ln_g'], 'v_sgu_ln_b': out['v_sgu_ln_b'], 'v_w_spatial': out['v_w_spatial'], 'v_b_spatial': out['v_b_spatial'], 'v_w_up_a': out['v_w_up_a'], 'v_w_up_b': out['v_w_up_b'], 'v_w_out': out['v_w_out'], 'v_final_norm_g': out['v_final_norm_g']}


def _loss(weights, diff, rest, loss_target):
    with _jax.named_scope("forward"):
        args = {**rest, TWIN_DIFF_INPUT: diff, **{k: w.astype(_WEIGHT_DTYPES[k]) for k, w in weights.items()}}
        y = _forward(args)
    with _jax.named_scope("loss_head"):
        err = _jnp.square(y.astype(_jnp.float32) - loss_target)
        return 0.5 * _jnp.sum(_jnp.mean(err, axis=-1)) if err.ndim else 0.5 * err


def _adamw(w, g, m, v):
    m = ADAM_B1 * m + (1.0 - ADAM_B1) * g
    v = ADAM_B2 * v + (1.0 - ADAM_B2) * _jnp.square(g)
    m_hat = m / (1.0 - ADAM_B1 ** ADAM_STEP)
    v_hat = v / (1.0 - ADAM_B2 ** ADAM_STEP)
    delta = -ADAM_LR * (m_hat / (_jnp.sqrt(v_hat) + ADAM_EPS) + ADAM_WD * w)
    return delta, m, v


def reference(x, norm_g, w_in, sgu_ln_g, sgu_ln_b, w_spatial, b_spatial, w_up_a, w_up_b, w_out, final_norm_g, loss_target, m_norm_g, m_w_in, m_sgu_ln_g, m_sgu_ln_b, m_w_spatial, m_b_spatial, m_w_up_a, m_w_up_b, m_w_out, m_final_norm_g, v_norm_g, v_w_in, v_sgu_ln_g, v_sgu_ln_b, v_w_spatial, v_b_spatial, v_w_up_a, v_w_up_b, v_w_out, v_final_norm_g):
    given = dict(x=x, norm_g=norm_g, w_in=w_in, sgu_ln_g=sgu_ln_g, sgu_ln_b=sgu_ln_b, w_spatial=w_spatial, b_spatial=b_spatial, w_up_a=w_up_a, w_up_b=w_up_b, w_out=w_out, final_norm_g=final_norm_g, loss_target=loss_target, m_norm_g=m_norm_g, m_w_in=m_w_in, m_sgu_ln_g=m_sgu_ln_g, m_sgu_ln_b=m_sgu_ln_b, m_w_spatial=m_w_spatial, m_b_spatial=m_b_spatial, m_w_up_a=m_w_up_a, m_w_up_b=m_w_up_b, m_w_out=m_w_out, m_final_norm_g=m_final_norm_g, v_norm_g=v_norm_g, v_w_in=v_w_in, v_sgu_ln_g=v_sgu_ln_g, v_sgu_ln_b=v_sgu_ln_b, v_w_spatial=v_w_spatial, v_b_spatial=v_b_spatial, v_w_up_a=v_w_up_a, v_w_up_b=v_w_up_b, v_w_out=v_w_out, v_final_norm_g=v_final_norm_g)
    weights = {n: given[n] for n in TWIN_WEIGHTS}
    shared = {n: given[n] for n in SHARED_INPUTS}
    per_example = {n: given[n] for n in ['x']}
    grad_fn = _jax.value_and_grad(_loss, argnums=(0, 1))

    def one_microbatch(ex, loss_target):
        ex = dict(ex)
        diff = ex.pop(TWIN_DIFF_INPUT)
        return grad_fn(weights, diff, {**shared, **ex}, loss_target)

    if N_MICROBATCH == 1:
        loss, (grad_w, grad_x) = one_microbatch(per_example, given["loss_target"])
    else:
        def body(carry, xs):
            loss_sum, grad_sum = carry
            l_k, (gw_k, gx_k) = one_microbatch(xs[0], xs[1])
            with _jax.named_scope("update"):
                return (loss_sum + l_k, _jax.tree.map(_jnp.add, grad_sum, gw_k)), gx_k

        init = (_jnp.zeros((), _jnp.float32), _jax.tree.map(_jnp.zeros_like, weights))
        (loss, grad_w), grad_x = _jax.lax.scan(body, init, (per_example, given["loss_target"]))
    with _jax.named_scope("update"):
        delta_w, new_m, new_v = {}, {}, {}
        for n in TWIN_WEIGHTS:
            delta_w[n], new_m[n], new_v[n] = _adamw(weights[n], grad_w[n], given["m_" + n], given["v_" + n])
    return (loss, grad_x, *[grad_w[n] for n in TWIN_WEIGHTS], *[delta_w[n] for n in TWIN_WEIGHTS],
            *[new_m[n] for n in TWIN_WEIGHTS], *[new_v[n] for n in TWIN_WEIGHTS])
```

```python
import functools
import math

import jax
import jax.numpy as jnp
from jax import lax
from jax.experimental import pallas as pl
from jax.experimental.pallas import tpu as pltpu

F32 = jnp.float32
BF16 = jnp.bfloat16

D_MODEL = 1024
N_HEADS = 8
HEAD_DIM = 64
D_BRANCH = 512
D_IN = 4 * D_BRANCH + 3 * D_BRANCH + 2 * D_MODEL
N_GROUPS = 8
GROUP_DIM = 64
SGU_CHUNK = 128
SGU_SUBCHUNK = 64
GROUP_SHIFT = 6
EPS = 1e-6
LANES = 128
ATTN_BLOCK = 256
N_CHIPS = 4
N_DEV = 8
MESH = pl.DeviceIdType.MESH

ADAM_LR = 0.001
ADAM_B1 = 0.9
ADAM_B2 = 0.999
ADAM_EPS = 1e-08
ADAM_WD = 0.01
ADAM_STEP = 10

COL_Q, COL_K, COL_V, COL_ZA, COL_UB, COL_VB, COL_ZB, COL_GA, COL_GB = 0, 1, 2, 3, 4, 5, 6, 7, 9

VMEM_LIMIT = 56 * 1024 * 1024

SMALL_ROWS = 1088
SMALL_PIECE = SMALL_ROWS // N_DEV


def _cparams(sem=None):
    return pltpu.CompilerParams(dimension_semantics=sem, vmem_limit_bytes=VMEM_LIMIT)


def _aligned(v, m):
    return v if isinstance(v, int) else pl.multiple_of(v, m)


def _sigmoid(x):
    return 1.0 / (1.0 + jnp.exp(-x))


def _gelu_and_grad(x):
    k = math.sqrt(2.0 / math.pi)
    x2 = x * x
    inner = k * (x + 0.044715 * x * x2)
    th = jnp.tanh(inner)
    g = 0.5 * x * (1.0 + th)
    dg = 0.5 * (1.0 + th) + 0.5 * x * (1.0 - th * th) * (k * (1.0 + 3.0 * 0.044715 * x2))
    return g, dg


def _split_dot(a, b_bf16, passes):
    out = None
    rem = a
    for _ in range(passes):
        part = rem.astype(BF16)
        d = jnp.dot(part, b_bf16, preferred_element_type=F32)
        out = d if out is None else out + d
        rem = rem - part.astype(F32)
    return out


def _dot_nt(a, b):
    return lax.dot_general(a, b, (((1,), (1,)), ((), ())), preferred_element_type=F32)


def _dot_tn(a, b):
    return lax.dot_general(a, b, (((0,), (0,)), ((), ())), preferred_element_type=F32)


def _in_proj(x, norm_g, w_in):
    s = x.shape[0]
    tm, tn = min(512, s), 512

    def body(x_ref, g_ref, w_ref, proj_ref, ht_ref, h_scr):
        @pl.when(pl.program_id(1) == 0)
        def _():
            xf = x_ref[...]
            r = lax.rsqrt(jnp.mean(xf * xf, axis=-1, keepdims=True) + EPS)
            h = xf * r * g_ref[...]
            h_scr[...] = h.astype(BF16)
            ht_ref[...] = h.T.astype(BF16)

        proj_ref[...] = jnp.dot(h_scr[...], w_ref[...], preferred_element_type=F32).astype(BF16)

    return pl.pallas_call(
        body, name="in_proj", grid=(s // tm, D_IN // tn),
        in_specs=[pl.BlockSpec((tm, D_MODEL), lambda i, j: (i, 0)),
                  pl.BlockSpec((1, D_MODEL), lambda i, j: (0, 0)),
                  pl.BlockSpec((D_MODEL, tn), lambda i, j: (0, j))],
        out_specs=[pl.BlockSpec((tm, tn), lambda i, j: (i, j)),
                   pl.BlockSpec((D_MODEL, tm), lambda i, j: (0, i))],
        out_shape=[jax.ShapeDtypeStruct((s, D_IN), BF16), jax.ShapeDtypeStruct((D_MODEL, s), BF16)],
        scratch_shapes=[pltpu.VMEM((tm, D_MODEL), BF16)],
        compiler_params=_cparams(("parallel", "arbitrary")),
    )(x, norm_g, w_in)


def _softplus_parts(z):
    p = jnp.exp(-jnp.abs(z))
    sp = jnp.maximum(z, 0.0) + jnp.log(1.0 + p)
    return p, sp


def _attn_fwd(proj, blk):
    s = proj.shape[0]
    nq = s // blk
    scale = HEAD_DIM ** -0.5

    def body(q_ref, k_ref, v_ref, za_ref, o_ref, ya_ref, rs_ref, acc_ref, r_ref):
        i = pl.program_id(1)
        lane = lax.broadcasted_iota(jnp.int32, (blk, LANES), 1)
        lo_half = lane < HEAD_DIM
        q = q_ref[...] * jnp.asarray(scale, BF16)
        qm = (jnp.where(lo_half, q, jnp.zeros_like(q)), jnp.where(lo_half, jnp.zeros_like(q), q))
        row = lax.broadcasted_iota(jnp.int32, (blk, blk), 0)
        col = lax.broadcasted_iota(jnp.int32, (blk, blk), 1)
        trev = (row >= col).astype(BF16)
        before = col < row
        acc_ref[...] = jnp.zeros_like(acc_ref)
        r_ref[...] = jnp.zeros_like(r_ref)
        rs_ref[...] = jnp.zeros_like(rs_ref)

        def block(j, diag):
            ks = pl.multiple_of(j * blk, blk)
            kj = k_ref[pl.ds(ks, blk), :]
            vj = v_ref[pl.ds(ks, blk), :]
            for h in range(2):
                z = _dot_nt(qm[h], kj)
                _, sp = _softplus_parts(z)
                lk = -sp
                if diag:
                    lk = jnp.where(before, lk, 0.0)
                cin = _split_dot(lk, trev, 2)
                r = r_ref[h]
                w = jnp.exp(z + cin + r)
                if diag:
                    w = jnp.where(before, w, 0.0)
                acc_ref[h] += jnp.dot(w.astype(BF16), vj, preferred_element_type=F32)
                rs_ref[...] = jnp.where(lane == j + HEAD_DIM * h, r, rs_ref[...])
                r_ref[h] = r + cin[:, 0:1]

        block(i, True)

        def loop_body(jj, carry):
            block(i - 1 - jj, False)
            return carry

        lax.fori_loop(0, i, loop_body, 0)
        o = jnp.where(lo_half, acc_ref[0], acc_ref[1])
        o_ref[...] = o.astype(BF16)
        za = za_ref[...].astype(F32)
        ya_ref[...] = (o * (za * _sigmoid(za))).astype(BF16)

    return pl.pallas_call(
        body, name="attn_fwd", grid=(N_HEADS // 2, nq),
        in_specs=[pl.BlockSpec((blk, LANES), lambda p, i: (i, 4 * COL_Q + p)),
                  pl.BlockSpec((s, LANES), lambda p, i: (0, 4 * COL_K + p)),
                  pl.BlockSpec((s, LANES), lambda p, i: (0, 4 * COL_V + p)),
                  pl.BlockSpec((blk, LANES), lambda p, i: (i, 4 * COL_ZA + p))],
        out_specs=[pl.BlockSpec((blk, LANES), lambda p, i: (i, p)),
                   pl.BlockSpec((blk, LANES), lambda p, i: (i, p)),
                   pl.BlockSpec((None, blk, LANES), lambda p, i: (p, i, 0))],
        out_shape=[jax.ShapeDtypeStruct((s, D_BRANCH), BF16), jax.ShapeDtypeStruct((s, D_BRANCH), BF16),
                   jax.ShapeDtypeStruct((N_HEADS // 2, s, LANES), F32)],
        scratch_shapes=[pltpu.VMEM((2, blk, LANES), F32), pltpu.VMEM((2, blk, 1), F32)],
        compiler_params=_cparams(("parallel", "parallel")),
    )(proj, proj, proj, proj)


def _attn_bwd(proj, do, rsave, blk):
    s = proj.shape[0]
    nq = s // blk
    scale = HEAD_DIM ** -0.5

    def body(q_ref, k_ref, v_ref, do_ref, rs_ref, dq_ref, dk_ref, dv_ref, dk_acc, dv_acc, dq_acc, e_ref):
        i = pl.program_id(1)
        lane = lax.broadcasted_iota(jnp.int32, (blk, LANES), 1)
        lo_half = lane < HEAD_DIM
        q = q_ref[...] * jnp.asarray(scale, BF16)
        dout = do_ref[...]
        zero = jnp.zeros_like(q)
        qm = (jnp.where(lo_half, q, zero), jnp.where(lo_half, zero, q))
        dom = (jnp.where(lo_half, dout, zero), jnp.where(lo_half, zero, dout))
        row = lax.broadcasted_iota(jnp.int32, (blk, blk), 0)
        col = lax.broadcasted_iota(jnp.int32, (blk, blk), 1)
        trev = (row >= col).astype(BF16)
        tfwd = (row <= col).astype(BF16)
        before = col < row

        @pl.when(i == 0)
        def _():
            dk_acc[...] = jnp.zeros_like(dk_acc)
            dv_acc[...] = jnp.zeros_like(dv_acc)

        dq_acc[...] = jnp.zeros_like(dq_acc)
        e_ref[...] = jnp.zeros_like(e_ref)
        rs = rs_ref[...]

        def block(j, diag):
            ks = pl.multiple_of(j * blk, blk)
            kj = k_ref[pl.ds(ks, blk), :]
            vj = v_ref[pl.ds(ks, blk), :]
            for h in range(2):
                z = _dot_nt(qm[h], kj)
                p, sp = _softplus_parts(z)
                lk = -sp
                if diag:
                    lk = jnp.where(before, lk, 0.0)
                cin = _split_dot(lk, trev, 2)
                r = jnp.sum(jnp.where(lane == j + HEAD_DIM * h, rs, 0.0), axis=-1, keepdims=True)
                w = jnp.exp(z + cin + r)
                if diag:
                    w = jnp.where(before, w, 0.0)
                dw = _dot_nt(dom[h], vj)
                e = dw * w
                eincl = _split_dot(e, tfwd, 2) + e_ref[h]
                beta = jnp.where(z >= 0.0, 1.0, p) / (1.0 + p)
                dz = e - beta * eincl
                if diag:
                    dz = jnp.where(before, dz, 0.0)
                e_ref[h] = eincl[:, blk - 1:blk]
                dzb = dz.astype(BF16)
                wb = w.astype(BF16)
                dq_acc[h] += jnp.dot(dzb, kj, preferred_element_type=F32)
                dk_acc[pl.ds(ks, blk), :] += _dot_tn(dzb, qm[h])
                dv_acc[pl.ds(ks, blk), :] += _dot_tn(wb, dom[h])

        def loop_body(j, carry):
            block(j, False)
            return carry

        lax.fori_loop(0, i, loop_body, 0)
        block(i, True)
        dq = jnp.where(lo_half, dq_acc[0], dq_acc[1]) * scale
        dq_ref[...] = dq.astype(BF16)

        @pl.when(i == nq - 1)
        def _():
            dk_ref[...] = dk_acc[...].astype(BF16)
            dv_ref[...] = dv_acc[...].astype(BF16)

    return pl.pallas_call(
        body, name="attn_bwd", grid=(N_HEADS // 2, nq),
        in_specs=[pl.BlockSpec((blk, LANES), lambda p, i: (i, 4 * COL_Q + p)),
                  pl.BlockSpec((s, LANES), lambda p, i: (0, 4 * COL_K + p)),
                  pl.BlockSpec((s, LANES), lambda p, i: (0, 4 * COL_V + p)),
                  pl.BlockSpec((blk, LANES), lambda p, i: (i, p)),
                  pl.BlockSpec((None, blk, LANES), lambda p, i: (p, i, 0))],
        out_specs=[pl.BlockSpec((blk, LANES), lambda p, i: (i, p)),
                   pl.BlockSpec((s, LANES), lambda p, i: (0, p)),
                   pl.BlockSpec((s, LANES), lambda p, i: (0, p))],
        out_shape=[jax.ShapeDtypeStruct((s, D_BRANCH), BF16)] * 3,
        scratch_shapes=[pltpu.VMEM((s, LANES), F32), pltpu.VMEM((s, LANES), F32),
                        pltpu.VMEM((2, blk, LANES), F32), pltpu.VMEM((2, blk, 1), F32)],
        compiler_params=_cparams(("parallel", "arbitrary")),
    )(proj, proj, proj, do, rsave)


def _group_avg_matrix():
    a = lax.broadcasted_iota(jnp.int32, (LANES, LANES), 0) >> GROUP_SHIFT
    b = lax.broadcasted_iota(jnp.int32, (LANES, LANES), 1) >> GROUP_SHIFT
    return jnp.where(a == b, 1.0 / GROUP_DIM, 0.0).astype(BF16)


def _group_mean(a, avg):
    parts = [_split_dot(a[:, LANES * k:LANES * (k + 1)], avg, 3) for k in range(D_BRANCH // LANES)]
    return jnp.concatenate(parts, axis=1)


def _sgu_forward_parts(ub, vb, ln_g, ln_b, avg):
    ug, dug = _gelu_and_grad(ub)
    vg, dvg = _gelu_and_grad(vb)
    mu = _group_mean(vg, avg)
    d = vg - mu
    var = _group_mean(d * d, avg)
    rstd = lax.rsqrt(var + EPS)
    vhat = d * rstd
    vn = vhat * ln_g + ln_b
    return ug, dug, dvg, rstd, vhat, vn


def _sgu_mix(w_ref, src_bf16, n_chunks):
    lane = lax.broadcasted_iota(jnp.int32, (SGU_CHUNK, LANES), 1)
    lo_half = lane < GROUP_DIM
    rows = []
    for n in range(n_chunks):
        slabs = []
        for a in range(D_BRANCH // LANES):
            blk = src_bf16[SGU_CHUNK * n:SGU_CHUNK * (n + 1), LANES * a:LANES * (a + 1)]
            zero = jnp.zeros_like(blk)
            m0 = jnp.dot(w_ref[2 * a], jnp.where(lo_half, blk, zero), preferred_element_type=F32)
            m1 = jnp.dot(w_ref[2 * a + 1], jnp.where(lo_half, zero, blk), preferred_element_type=F32)
            slabs.append(m0 + m1)
        rows.append(jnp.concatenate(slabs, axis=1))
    return jnp.concatenate(rows, axis=0)


def _sgu_fwd(proj, ln_g, ln_b, w_mask, bias_full):
    s = proj.shape[0]
    tm = min(512, s)
    n_chunks = tm // SGU_CHUNK

    def body(ub_ref, vb_ref, zb_ref, g_ref, b_ref, w_ref, bias_ref, yb_ref):
        avg = _group_avg_matrix()
        ug, _, _, _, _, vn = _sgu_forward_parts(ub_ref[...].astype(F32), vb_ref[...].astype(F32),
                                                g_ref[...], b_ref[...], avg)
        mixed = _sgu_mix(w_ref, vn.astype(BF16), n_chunks) + jnp.concatenate([bias_ref[...]] * n_chunks, axis=0)
        zb = zb_ref[...].astype(F32)
        yb_ref[...] = (ug * mixed * (zb * _sigmoid(zb))).astype(BF16)

    col = lambda c: pl.BlockSpec((tm, D_BRANCH), lambda i: (i, c))
    full = lambda shape: pl.BlockSpec(shape, lambda i: (0,) * len(shape))
    return pl.pallas_call(
        body, name="sgu_fwd", grid=(s // tm,),
        in_specs=[col(COL_UB), col(COL_VB), col(COL_ZB), full((1, D_BRANCH)), full((1, D_BRANCH)),
                  full((N_GROUPS, SGU_CHUNK, SGU_CHUNK)), full((SGU_CHUNK, D_BRANCH))],
        out_specs=pl.BlockSpec((tm, D_BRANCH), lambda i: (i, 0)),
        out_shape=jax.ShapeDtypeStruct((s, D_BRANCH), BF16),
        compiler_params=_cparams(("parallel",)),
    )(proj, proj, proj, ln_g, ln_b, w_mask, bias_full)


def _sgu_bwd(proj, dyb, ln_g, ln_b, w_mask, w_mask_t, bias_full):
    s = proj.shape[0]
    tm = min(512, s)
    n_chunks = tm // SGU_CHUNK
    n_steps = s // tm

    def body(ub_ref, vb_ref, zb_ref, dyb_ref, g_ref, b_ref, w_ref, wt_ref, bias_ref,
             dsgu_ref, dw_ref, db_ref, dg_ref, dbeta_ref, dmix_acc):
        i = pl.program_id(0)

        @pl.when(i == 0)
        def _():
            dw_ref[...] = jnp.zeros_like(dw_ref)
            dg_ref[...] = jnp.zeros_like(dg_ref)
            dbeta_ref[...] = jnp.zeros_like(dbeta_ref)
            dmix_acc[...] = jnp.zeros_like(dmix_acc)

        avg = _group_avg_matrix()
        ln_gv = g_ref[...]
        ug, dug, dvg, rstd, vhat, vn = _sgu_forward_parts(ub_ref[...].astype(F32), vb_ref[...].astype(F32),
                                                          ln_gv, b_ref[...], avg)
        vnb = vn.astype(BF16)
        mixed = _sgu_mix(w_ref, vnb, n_chunks) + jnp.concatenate([bias_ref[...]] * n_chunks, axis=0)
        zb = zb_ref[...].astype(F32)
        sg = _sigmoid(zb)
        sz = zb * sg
        dsz = sg * (1.0 + zb * (1.0 - sg))
        dy = dyb_ref[...].astype(F32)
        dmixed = dy * ug * sz
        du = dy * mixed * sz * dug
        dzb = dy * ug * mixed * dsz
        dmb = dmixed.astype(BF16)
        dvn = _sgu_mix(wt_ref, dmb, n_chunks)

        lane = lax.broadcasted_iota(jnp.int32, (SGU_CHUNK, LANES), 1)
        lo_half = lane < GROUP_DIM
        dm_sum = None
        for n in range(n_chunks):
            rows = slice(SGU_CHUNK * n, SGU_CHUNK * (n + 1))
            dm_sum = dmixed[rows] if dm_sum is None else dm_sum + dmixed[rows]
            for a in range(D_BRANCH // LANES):
                cols = slice(LANES * a, LANES * (a + 1))
                dblk = dmb[rows, cols]
                vblk = vnb[rows, cols]
                zero = jnp.zeros_like(dblk)
                dw_ref[2 * a] += _dot_nt(jnp.where(lo_half, dblk, zero), vblk)
                dw_ref[2 * a + 1] += _dot_nt(jnp.where(lo_half, zero, dblk), vblk)
        dmix_acc[...] += dm_sum

        dg_ref[...] += jnp.sum(dvn * vhat, axis=0, keepdims=True)
        dbeta_ref[...] += jnp.sum(dvn, axis=0, keepdims=True)
        dvh = dvn * ln_gv
        m1 = _group_mean(dvh, avg)
        m2 = _group_mean(dvh * vhat, avg)
        dv = rstd * (dvh - m1 - vhat * m2) * dvg
        dsgu_ref[:, 0:D_BRANCH] = du.astype(BF16)
        dsgu_ref[:, D_BRANCH:2 * D_BRANCH] = dv.astype(BF16)
        dsgu_ref[:, 2 * D_BRANCH:3 * D_BRANCH] = dzb.astype(BF16)

        @pl.when(i == n_steps - 1)
        def _():
            pos = lax.broadcasted_iota(jnp.int32, (SGU_CHUNK, SGU_CHUNK), 0) >> GROUP_SHIFT
            src = lax.broadcasted_iota(jnp.int32, (SGU_CHUNK, SGU_CHUNK), 1) >> GROUP_SHIFT
            keep = src <= pos
            for g in range(N_GROUPS):
                dw_ref[g] = jnp.where(keep, dw_ref[g], 0.0)
            grp = lax.broadcasted_iota(jnp.int32, (D_BRANCH, LANES), 0) >> GROUP_SHIFT
            sel = (grp == lax.broadcasted_iota(jnp.int32, (D_BRANCH, LANES), 1)).astype(BF16)
            db_ref[...] = _split_dot(dmix_acc[...], sel, 3)

    col = lambda c: pl.BlockSpec((tm, D_BRANCH), lambda i: (i, c))
    full = lambda shape: pl.BlockSpec(shape, lambda i: (0,) * len(shape))
    return pl.pallas_call(
        body, name="sgu_bwd", grid=(n_steps,),
        in_specs=[col(COL_UB), col(COL_VB), col(COL_ZB), pl.BlockSpec((tm, D_BRANCH), lambda i: (i, 0)),
                  full((1, D_BRANCH)), full((1, D_BRANCH)),
                  full((N_GROUPS, SGU_CHUNK, SGU_CHUNK)), full((N_GROUPS, SGU_CHUNK, SGU_CHUNK)),
                  full((SGU_CHUNK, D_BRANCH))],
        out_specs=[pl.BlockSpec((tm, 3 * D_BRANCH), lambda i: (i, 0)),
                   full((N_GROUPS, SGU_CHUNK, SGU_CHUNK)), full((SGU_CHUNK, LANES)),
                   full((1, D_BRANCH)), full((1, D_BRANCH))],
        out_shape=[jax.ShapeDtypeStruct((s, 3 * D_BRANCH), BF16),
                   jax.ShapeDtypeStruct((N_GROUPS, SGU_CHUNK, SGU_CHUNK), F32),
                   jax.ShapeDtypeStruct((SGU_CHUNK, LANES), F32),
                   jax.ShapeDtypeStruct((1, D_BRANCH), F32), jax.ShapeDtypeStruct((1, D_BRANCH), F32)],
        scratch_shapes=[pltpu.VMEM((SGU_CHUNK, D_BRANCH), F32)],
        compiler_params=_cparams(("arbitrary",)),
    )(proj, proj, proj, dyb, ln_g, ln_b, w_mask, w_mask_t, bias_full)


def _mid(proj, ya, yb, o, x, target, final_g, w_up_a, w_up_b, w_out):
    s = x.shape[0]
    tm = min(256, s)
    n_steps = s // tm
    half = D_MODEL // 2

    def body(ya_ref, yb_ref, o_ref, za_ref, ga0_ref, ga1_ref, gb0_ref, gb1_ref, x_ref, t_ref, gf_ref,
             wa_ref, wb_ref, wo_ref,
             dzg_ref, do_ref, dyb_ref, dx2_ref, gwo_ref, gwa_ref, gwb_ref, loss_ref, dgf_ref,
             acc_o, acc_a, acc_b):
        i = pl.program_id(0)

        @pl.when(i == 0)
        def _():
            acc_o[...] = jnp.zeros_like(acc_o)
            acc_a[...] = jnp.zeros_like(acc_a)
            acc_b[...] = jnp.zeros_like(acc_b)
            loss_ref[...] = jnp.zeros_like(loss_ref)
            dgf_ref[...] = jnp.zeros_like(dgf_ref)

        ya_v = ya_ref[...]
        yb_v = yb_ref[...]
        pa = jnp.dot(ya_v, wa_ref[...], preferred_element_type=F32)
        pb = jnp.dot(yb_v, wb_ref[...], preferred_element_type=F32)
        sa = _sigmoid(jnp.concatenate([ga0_ref[...], ga1_ref[...]], axis=1).astype(F32))
        sb = _sigmoid(jnp.concatenate([gb0_ref[...], gb1_ref[...]], axis=1).astype(F32))
        merged = (sa * pa + sb * pb).astype(BF16)
        x2 = x_ref[...] + jnp.dot(merged, wo_ref[...], preferred_element_type=F32)
        r2 = lax.rsqrt(jnp.mean(x2 * x2, axis=-1, keepdims=True) + EPS)
        xh = x2 * r2
        gf = gf_ref[...]
        diff = xh * gf - t_ref[...]
        loss_ref[...] += 0.5 * jnp.sum(jnp.mean(diff * diff, axis=-1, keepdims=True))
        dy = diff * (1.0 / D_MODEL)
        dgf_ref[...] += jnp.sum(dy * xh, axis=0, keepdims=True)
        dyg = dy * gf
        dx2 = r2 * (dyg - xh * jnp.mean(dyg * xh, axis=-1, keepdims=True))
        dx2_ref[...] = dx2
        dx2b = dx2.astype(BF16)
        dmerged = _dot_nt(dx2b, wo_ref[...])
        acc_o[...] += _dot_tn(merged, dx2b)
        dpa = dmerged * sa
        dpb = dmerged * sb
        dzg_ref[:, D_BRANCH:D_BRANCH + D_MODEL] = (dpa * pa * (1.0 - sa)).astype(BF16)
        dzg_ref[:, D_BRANCH + D_MODEL:D_BRANCH + 2 * D_MODEL] = (dpb * pb * (1.0 - sb)).astype(BF16)
        dpab = dpa.astype(BF16)
        dpbb = dpb.astype(BF16)
        acc_a[...] += _dot_tn(ya_v, dpab)
        acc_b[...] += _dot_tn(yb_v, dpbb)
        dya = _dot_nt(dpab, wa_ref[...])
        dyb_ref[...] = _dot_nt(dpbb, wb_ref[...]).astype(BF16)
        za = za_ref[...].astype(F32)
        sg = _sigmoid(za)
        do_ref[...] = (dya * (za * sg)).astype(BF16)
        dzg_ref[:, 0:D_BRANCH] = (dya * o_ref[...].astype(F32) * (sg * (1.0 + za * (1.0 - sg)))).astype(BF16)

        @pl.when(i == n_steps - 1)
        def _():
            gwo_ref[...] = acc_o[...].astype(BF16)
            gwa_ref[...] = acc_a[...].astype(BF16)
            gwb_ref[...] = acc_b[...].astype(BF16)

    tok = lambda w: pl.BlockSpec((tm, w), lambda i: (i, 0))
    col = lambda c: pl.BlockSpec((tm, half), lambda i: (i, c))
    full = lambda shape: pl.BlockSpec(shape, lambda i: (0,) * len(shape))
    return pl.pallas_call(
        body, name="mid", grid=(n_steps,),
        in_specs=[tok(D_BRANCH), tok(D_BRANCH), tok(D_BRANCH), col(COL_ZA), col(COL_GA), col(COL_GA + 1),
                  col(COL_GB), col(COL_GB + 1), tok(D_MODEL), tok(D_MODEL), full((1, D_MODEL)),
                  full((D_BRANCH, D_MODEL)), full((D_BRANCH, D_MODEL)), full((D_MODEL, D_MODEL))],
        out_specs=[tok(D_BRANCH + 2 * D_MODEL), tok(D_BRANCH), tok(D_BRANCH), tok(D_MODEL),
                   full((D_MODEL, D_MODEL)), full((D_BRANCH, D_MODEL)), full((D_BRANCH, D_MODEL)),
                   full((8, LANES)), full((1, D_MODEL))],
        out_shape=[jax.ShapeDtypeStruct((s, D_BRANCH + 2 * D_MODEL), BF16),
                   jax.ShapeDtypeStruct((s, D_BRANCH), BF16), jax.ShapeDtypeStruct((s, D_BRANCH), BF16),
                   jax.ShapeDtypeStruct((s, D_MODEL), F32),
                   jax.ShapeDtypeStruct((D_MODEL, D_MODEL), BF16),
                   jax.ShapeDtypeStruct((D_BRANCH, D_MODEL), BF16), jax.ShapeDtypeStruct((D_BRANCH, D_MODEL), BF16),
                   jax.ShapeDtypeStruct((8, LANES), F32), jax.ShapeDtypeStruct((1, D_MODEL), F32)],
        scratch_shapes=[pltpu.VMEM((D_MODEL, D_MODEL), F32), pltpu.VMEM((D_BRANCH, D_MODEL), F32),
                        pltpu.VMEM((D_BRANCH, D_MODEL), F32)],
        compiler_params=_cparams(("arbitrary",)),
    )(ya, yb, o, proj, proj, proj, proj, proj, x, target, final_g, w_up_a, w_up_b, w_out)


def _dwin_piece(ht, piece, tile_of, prev):
    s = ht.shape[1]
    n_tiles = piece.shape[1] // D_BRANCH

    def body(ht_ref, p_ref, *rest):
        out_ref = rest[-1]
        out_ref[...] = jnp.dot(ht_ref[...], p_ref[...], preferred_element_type=F32).astype(BF16)

    in_specs = [pl.BlockSpec((D_MODEL, s), lambda j: (0, 0)), pl.BlockSpec((s, D_BRANCH), lambda j: (0, j))]
    args = [ht, piece]
    aliases = {}
    if prev is not None:
        in_specs.append(pl.BlockSpec(memory_space=pl.ANY))
        args.append(prev)
        aliases = {2: 0}
    return pl.pallas_call(
        body, name="dwin_piece", grid=(n_tiles,),
        in_specs=in_specs,
        out_specs=pl.BlockSpec((D_MODEL, D_BRANCH), lambda j: (0, tile_of(j))),
        out_shape=jax.ShapeDtypeStruct((D_MODEL, D_IN), BF16),
        input_output_aliases=aliases,
        compiler_params=_cparams(("parallel",)),
    )(*args)


def _dh_dx(pieces, w_in, x, norm_g, dx2):
    s = x.shape[0]
    tm = min(256, s)
    arrays = []
    for arr, _, _, _ in pieces:
        if not any(arr is a for a in arrays):
            arrays.append(arr)
    n_arr = len(arrays)
    plan = [([k for k, a in enumerate(arrays) if a is arr][0], wcol, off, width) for arr, wcol, off, width in pieces]

    def body(*refs):
        p_refs = refs[:n_arr]
        w_ref, x_ref, g_ref, dx2_ref, dx_ref, dg_ref = refs[n_arr:]

        @pl.when(pl.program_id(0) == 0)
        def _():
            dg_ref[...] = jnp.zeros_like(dg_ref)

        dh = None
        for k, wcol, off, width in plan:
            d = _dot_nt(p_refs[k][:, off:off + width], w_ref[:, wcol:wcol + width])
            dh = d if dh is None else dh + d
        xf = x_ref[...]
        r = lax.rsqrt(jnp.mean(xf * xf, axis=-1, keepdims=True) + EPS)
        xh = xf * r
        dg_ref[...] += jnp.sum(dh * xh, axis=0, keepdims=True)
        dhg = dh * g_ref[...]
        dx_ref[...] = r * (dhg - xh * jnp.mean(dhg * xh, axis=-1, keepdims=True)) + dx2_ref[...]

    tok = lambda w: pl.BlockSpec((tm, w), lambda i: (i, 0))
    full = lambda shape: pl.BlockSpec(shape, lambda i: (0,) * len(shape))
    return pl.pallas_call(
        body, name="dh_dx", grid=(s // tm,),
        in_specs=[tok(a.shape[1]) for a in arrays] + [full((D_MODEL, D_IN)), tok(D_MODEL), full((1, D_MODEL)),
                                                      tok(D_MODEL)],
        out_specs=[tok(D_MODEL), full((1, D_MODEL))],
        out_shape=[jax.ShapeDtypeStruct((s, D_MODEL), F32), jax.ShapeDtypeStruct((1, D_MODEL), F32)],
        compiler_params=_cparams(("arbitrary",)),
    )(*arrays, w_in, x, norm_g, dx2)


def _adamw(w, g, m, v):
    rows, cols = w.shape
    tr = max(t for t in range(8, 257, 8) if rows % t == 0)
    c1 =1.0 - ADAM_B1 ** ADAM_STEP
    c2 = 1.0 - ADAM_B2 ** ADAM_STEP

    def body(w_ref, g_ref, m_ref, v_ref, d_ref, nm_ref, nv_ref):
        gv = g_ref[...]
        nm = ADAM_B1 * m_ref[...] + (1.0 - ADAM_B1) * gv
        nv = ADAM_B2 * v_ref[...] + (1.0 - ADAM_B2) * (gv * gv)
        d_ref[...] = -ADAM_LR * ((nm / c1) / (jnp.sqrt(nv / c2) + ADAM_EPS) + ADAM_WD * w_ref[...])
        nm_ref[...] = nm
        nv_ref[...] = nv

    spec = pl.BlockSpec((tr, cols), lambda i: (i, 0))
    return pl.pallas_call(
        body, name="adamw", grid=(rows // tr,),
        in_specs=[spec] * 4, out_specs=[spec] * 3,
        out_shape=[jax.ShapeDtypeStruct((rows, cols), F32)] * 3,
        compiler_params=_cparams(("parallel",)),
    )(w, g, m, v)


def _place():
    x, y, c = lax.axis_index("x"), lax.axis_index("y"), lax.axis_index("c")
    return x, y, c


def _gather_weights(w_in, w_up_a, w_up_b, w_out):
    shards = (w_in, w_up_a, w_up_b, w_out)
    n_arr = len(shards)
    col_sharded = (True, True, True, False)
    full_shapes = ((D_MODEL, D_IN), (D_BRANCH, D_MODEL), (D_BRANCH, D_MODEL), (D_MODEL, D_MODEL))

    def body(*refs):
        src = refs[:n_arr]
        out = refs[n_arr:2 * n_arr]
        stage = refs[2 * n_arr:3 * n_arr]
        cast = refs[3 * n_arr:4 * n_arr]
        send_sems, recv_sems, local_sems = refs[4 * n_arr:]
        x, y, c = _place()
        chip = 2 * x + y
        sibling = (x, y, 1 - c)
        others = [(1 - x, y), (x, 1 - y), (1 - x, 1 - y)]

        def region(a, chip_idx, half):
            r, w = shards[a].shape
            hr = r // 2
            if col_sharded[a]:
                return out[a].at[pl.ds(_aligned(half * hr, 16), hr), pl.ds(_aligned(chip_idx * w, LANES), w)]
            return out[a].at[pl.ds(_aligned(chip_idx * r + half * hr, 16), hr), :]

        loads = [pltpu.make_async_copy(src[a], stage[a], local_sems.at[a]) for a in range(n_arr)]
        for cp in loads:
            cp.start()
        for a in range(n_arr):
            loads[a].wait()
            cast[a][...] = stage[a][...].astype(BF16)
        stores = []
        for a in range(n_arr):
            hr = shards[a].shape[0] // 2
            for half in range(2):
                cp = pltpu.make_async_copy(cast[a].at[pl.ds(half * hr, hr), :], region(a, chip, half),
                                           local_sems.at[n_arr + 2 * a + half])
                cp.start()
                stores.append(cp)

        def remote(k, a, chip_idx, half, to, from_vmem):
            hr = shards[a].shape[0] // 2
            s_ref = cast[a].at[pl.ds(_aligned(half * hr, 16), hr), :] if from_vmem else region(a, chip_idx, half)
            return pltpu.make_async_remote_copy(src_ref=s_ref, dst_ref=region(a, chip_idx, half),
                                                send_sem=send_sems.at[k], recv_sem=recv_sems.at[k],
                                                device_id=to, device_id_type=MESH)

        first = []
        for j, (ox, oy) in enumerate(others):
            for a in range(n_arr):
                cp = remote(n_arr * j + a, a, chip, c, (ox, oy, c), True)
                cp.start()
                first.append(cp)
        passed = []
        for j, (ox, oy) in enumerate(others):
            ochip = 2 * ox + oy
            for a in range(n_arr):
                k = n_arr * j + a
                remote(k, a, ochip, c, sibling, False).wait_recv()
                cp = remote(3 * n_arr + k, a, ochip, c, sibling, False)
                cp.start()
                passed.append(cp)
        for j, (ox, oy) in enumerate(others):
            ochip = 2 * ox + oy
            for a in range(n_arr):
                remote(3 * n_arr + n_arr * j + a, a, ochip, 1 - c, sibling, False).wait_recv()
        for cp in first + passed:
            cp.wait_send()
        for cp in stores:
            cp.wait()

    any_spec = pl.BlockSpec(memory_space=pl.ANY)
    return pl.pallas_call(
        body, name="gather_weights",
        in_specs=[any_spec] * n_arr, out_specs=[any_spec] * n_arr,
        out_shape=[jax.ShapeDtypeStruct(sh, BF16) for sh in full_shapes],
        scratch_shapes=[pltpu.VMEM(a.shape, F32) for a in shards] + [pltpu.VMEM(a.shape, BF16) for a in shards]
        + [pltpu.SemaphoreType.DMA((6 * n_arr,)), pltpu.SemaphoreType.DMA((6 * n_arr,)),
           pltpu.SemaphoreType.DMA((3 * n_arr,))],
        compiler_params=pltpu.CompilerParams(vmem_limit_bytes=VMEM_LIMIT, has_side_effects=True),
    )(*shards)


def _reduce_grads(g_in, g_up_a, g_up_b, g_out, g_small):
    big = (g_in, g_up_a, g_up_b, g_out)
    n_big = len(big)
    col_sharded = (True, True, True, False)
    piece_shapes = []
    for a, arr in enumerate(big):
        r, w = arr.shape
        piece_shapes.append((r // 2, w // N_CHIPS) if col_sharded[a] else (r // (2 * N_CHIPS), w))
    shard_shapes = [(2 * r, w) for r, w in piece_shapes]
    n_arr = n_big + 1

    def body(*refs):
        src = refs[:n_arr]
        out = refs[n_arr:2 * n_arr]
        slots = refs[2 * n_arr:3 * n_arr]
        sums = refs[3 * n_arr:4 * n_arr]
        send1, recv1, send2, recv2, local_sems = refs[4 * n_arr:]
        x, y, c = _place()
        me = 4 * x + 2 * y + c

        def piece_of(a, dev):
            chip_idx, half = dev // 2, dev % 2
            if a == n_big:
                return src[a].at[dev]
            r, w = piece_shapes[a]
            if col_sharded[a]:
                return src[a].at[pl.ds(pl.multiple_of(half * r, 16), r), pl.ds(pl.multiple_of(chip_idx * w, LANES), w)]
            return src[a].at[pl.ds(pl.multiple_of(dev * r, 16), r), :]

        def dev_id(dev):
            return (dev // 4, (dev // 2) % 2, dev % 2)

        own = [pltpu.make_async_copy(piece_of(a, me), slots[a].at[me], local_sems.at[a]) for a in range(n_arr)]
        for cp in own:
            cp.start()
        sends = []
        for d in range(1, N_DEV):
            peer = (me + d) % N_DEV
            for a in range(n_arr):
                cp = pltpu.make_async_remote_copy(
                    src_ref=piece_of(a, peer), dst_ref=slots[a].at[me],
                    send_sem=send1.at[n_arr * peer + a], recv_sem=recv1.at[n_arr * me + a],
                    device_id=dev_id(peer), device_id_type=MESH)
                cp.start()
                sends.append(cp)
        for cp in own:
            cp.wait()
        for d in range(1, N_DEV):
            peer = (me + d) % N_DEV
            for a in range(n_arr):
                pltpu.make_async_remote_copy(
                    src_ref=piece_of(a, peer), dst_ref=slots[a].at[peer],
                    send_sem=send1.at[n_arr * peer + a], recv_sem=recv1.at[n_arr * peer + a],
                    device_id=dev_id(peer), device_id_type=MESH).wait_recv()
        for a in range(n_arr):
            rows = slots[a].shape[1]
            step = 64 if rows % 64 == 0 else 8

            def add_rows(t, carry, a=a, step=step):
                r0 = pl.multiple_of(t * step, step)
                total = slots[a][0, pl.ds(r0, step), :].astype(F32)
                for dev in range(1, N_DEV):
                    total = total + slots[a][dev, pl.ds(r0, step), :].astype(F32)
                sums[a][pl.ds(r0, step), :] = total
                return carry

            lax.fori_loop(0, rows // step, add_rows, 0)
        shares = []
        keeps = []
        for a in range(n_big):
            r, w = piece_shapes[a]
            dst = out[a].at[pl.ds(pl.multiple_of(c * r, 8), r), :]
            cp = pltpu.make_async_remote_copy(src_ref=sums[a], dst_ref=dst, send_sem=send2.at[a], recv_sem=recv2.at[a],
                                              device_id=(x, y, 1 - c), device_id_type=MESH)
            cp.start()
            shares.append(cp)
            kp = pltpu.make_async_copy(sums[a], dst, local_sems.at[n_arr + a])
            kp.start()
            keeps.append(kp)
        kp = pltpu.make_async_copy(sums[n_big], out[n_big].at[me], local_sems.at[n_arr + n_big])
        kp.start()
        keeps.append(kp)
        for d in range(1, N_DEV):
            peer = (me + d) % N_DEV
            cp = pltpu.make_async_remote_copy(src_ref=sums[n_big], dst_ref=out[n_big].at[me],
                                              send_sem=send2.at[n_big + peer], recv_sem=recv2.at[n_big + me],
                                              device_id=dev_id(peer), device_id_type=MESH)
            cp.start()
            shares.append(cp)
        for a in range(n_big):
            r, w = piece_shapes[a]
            other = out[a].at[pl.ds(pl.multiple_of((1 - c) * r, 8), r), :]
            pltpu.make_async_remote_copy(src_ref=sums[a], dst_ref=other, send_sem=send2.at[a], recv_sem=recv2.at[a],
                                         device_id=(x, y, 1 - c), device_id_type=MESH).wait_recv()
        for d in range(1, N_DEV):
            peer = (me + d) % N_DEV
            pltpu.make_async_remote_copy(src_ref=sums[n_big], dst_ref=out[n_big].at[peer],
                                         send_sem=send2.at[n_big + peer], recv_sem=recv2.at[n_big + peer],
                                         device_id=dev_id(peer), device_id_type=MESH).wait_recv()
        for cp in sends + shares:
            cp.wait_send()
        for kp in keeps:
            kp.wait()

    any_spec = pl.BlockSpec(memory_space=pl.ANY)
    small_piece = (SMALL_PIECE, LANES)
    return pl.pallas_call(
        body, name="reduce_grads",
        in_specs=[any_spec] * n_arr, out_specs=[any_spec] * n_arr,
        out_shape=[jax.ShapeDtypeStruct(sh, F32) for sh in shard_shapes]
        + [jax.ShapeDtypeStruct((N_DEV,) + small_piece, F32)],
        scratch_shapes=[pltpu.VMEM((N_DEV,) + sh, BF16) for sh in piece_shapes]
        + [pltpu.VMEM((N_DEV,) + small_piece, F32)]
        + [pltpu.VMEM(sh, F32) for sh in piece_shapes] + [pltpu.VMEM(small_piece, F32)]
        + [pltpu.SemaphoreType.DMA((N_DEV * n_arr,)), pltpu.SemaphoreType.DMA((N_DEV * n_arr,)),
           pltpu.SemaphoreType.DMA((n_big + N_DEV,)), pltpu.SemaphoreType.DMA((n_big + N_DEV,)),
           pltpu.SemaphoreType.DMA((2 * n_arr,))],
        compiler_params=pltpu.CompilerParams(vmem_limit_bytes=VMEM_LIMIT, has_side_effects=True),
    )(*big, g_small)


_SMALL_PARTS = (("norm_g", 8), ("sgu_ln_g", 8), ("sgu_ln_b", 8), ("w_spatial", 1024), ("b_spatial", 8),
                ("final_norm_g", 8))


def _pack_small(parts):
    rows = []
    used = 0
    for name, n_rows in _SMALL_PARTS:
        a = parts[name].reshape(-1, LANES).astype(F32)
        a = jnp.pad(a, ((0, n_rows - a.shape[0]), (0, 0)))
        rows.append(a)
        used += n_rows
    rows.append(jnp.zeros((SMALL_ROWS - used, LANES), F32))
    return jnp.concatenate(rows, axis=0)


def _unpack_small(packed, shapes):
    out = {}
    r0 = 0
    for name, n_rows in _SMALL_PARTS:
        n = math.prod(shapes[name])
        out[name] = packed[r0:r0 + n // LANES].reshape(shapes[name])
        r0 += n_rows
    return out


def _local_step(x, target, norm_g, w_in, sgu_ln_g, sgu_ln_b, w_spatial, b_spatial, w_up_a, w_up_b, w_out,
                final_norm_g, blk):
    pos = jnp.arange(SGU_CHUNK)
    keep = (pos[None, :] // SGU_SUBCHUNK) <= (pos[:, None] // SGU_SUBCHUNK)
    w_mask = jnp.where(keep[None], w_spatial, 0.0).astype(BF16)
    w_mask_t = jnp.swapaxes(w_mask, 1, 2)
    bias_full = jnp.repeat(b_spatial.T, GROUP_DIM, axis=1)
    ln_g = sgu_ln_g.reshape(1, D_BRANCH)
    ln_b = sgu_ln_b.reshape(1, D_BRANCH)
    final_g = final_norm_g.reshape(1, D_MODEL)

    proj, ht = _in_proj(x, norm_g, w_in)
    o, ya, rsave = _attn_fwd(proj, blk)
    yb = _sgu_fwd(proj, ln_g, ln_b, w_mask, bias_full)
    dzg, do, dyb, dx2, g_out, g_up_a, g_up_b, loss_acc, d_final = _mid(
        proj, ya, yb, o, x, target, final_g, w_up_a, w_up_b, w_out)
    dsgu, d_wsp, d_bsp, d_lng, d_lnb = _sgu_bwd(proj, dyb, ln_g, ln_b, w_mask, w_mask_t, bias_full)
    dq, dk, dv = _attn_bwd(proj, do, rsave, blk)

    g_in = _dwin_piece(ht, dzg, lambda j: jnp.where(j == 0, COL_ZA, COL_GA - 1 + j), None)
    g_in = _dwin_piece(ht, dsgu, lambda j: COL_UB + j, g_in)
    g_in = _dwin_piece(ht, dq, lambda j: COL_Q + j, g_in)
    g_in = _dwin_piece(ht, dk, lambda j: COL_K + j, g_in)
    g_in = _dwin_piece(ht, dv, lambda j: COL_V + j, g_in)
    pieces = [(dq, COL_Q * D_BRANCH, 0, D_BRANCH), (dk, COL_K * D_BRANCH, 0, D_BRANCH),
              (dv, COL_V * D_BRANCH, 0, D_BRANCH), (dzg, COL_ZA * D_BRANCH, 0, D_BRANCH),
              (dsgu, COL_UB * D_BRANCH, 0, 3 * D_BRANCH), (dzg, COL_GA * D_BRANCH, D_BRANCH, 2 * D_MODEL)]
    dx, d_norm = _dh_dx(pieces, w_in, x, norm_g, dx2)
    small = {"norm_g": d_norm, "sgu_ln_g": d_lng, "sgu_ln_b": d_lnb, "w_spatial": d_wsp,
             "b_spatial": d_bsp[:, :N_GROUPS].T, "final_norm_g": d_final}
    return loss_acc[0, 0], dx, g_in, g_up_a, g_up_b, g_out, small


def kernel(x, norm_g, w_in, sgu_ln_g, sgu_ln_b, w_spatial, b_spatial, w_up_a, w_up_b, w_out, final_norm_g, loss_target, m_norm_g, m_w_in, m_sgu_ln_g, m_sgu_ln_b, m_w_spatial, m_b_spatial, m_w_up_a, m_w_up_b, m_w_out, m_final_norm_g, v_norm_g, v_w_in, v_sgu_ln_g, v_sgu_ln_b, v_w_spatial, v_b_spatial, v_w_up_a, v_w_up_b, v_w_out, v_final_norm_g):
    big_names = ("w_in", "w_up_a", "w_up_b", "w_out")
    small_names = tuple(n for n, _ in _SMALL_PARTS)
    names = ("norm_g", "w_in", "sgu_ln_g", "sgu_ln_b", "w_spatial", "b_spatial", "w_up_a", "w_up_b", "w_out",
             "final_norm_g")
    w = dict(norm_g=norm_g, w_in=w_in, sgu_ln_g=sgu_ln_g, sgu_ln_b=sgu_ln_b, w_spatial=w_spatial,
             b_spatial=b_spatial, w_up_a=w_up_a, w_up_b=w_up_b, w_out=w_out, final_norm_g=final_norm_g)
    m = dict(norm_g=m_norm_g, w_in=m_w_in, sgu_ln_g=m_sgu_ln_g, sgu_ln_b=m_sgu_ln_b, w_spatial=m_w_spatial,
             b_spatial=m_b_spatial, w_up_a=m_w_up_a, w_up_b=m_w_up_b, w_out=m_w_out, final_norm_g=m_final_norm_g)
    v = dict(norm_g=v_norm_g, w_in=v_w_in, sgu_ln_g=v_sgu_ln_g, sgu_ln_b=v_sgu_ln_b, w_spatial=v_w_spatial,
             b_spatial=v_b_spatial, w_up_a=v_w_up_a, w_up_b=v_w_up_b, w_out=v_w_out, final_norm_g=v_final_norm_g)
    shapes = {n: w[n].shape for n in names}
    flat2d = lambda a: a.reshape(a.shape[-2:])

    full = _gather_weights(*[flat2d(w[n]) for n in big_names])
    loss, dx, g_in, g_up_a, g_up_b, g_out, small = _local_step(
        x[0], loss_target[0], norm_g, full[0], sgu_ln_g[0], sgu_ln_b[0], w_spatial[0], b_spatial[0],
        full[1], full[2], full[3], final_norm_g, ATTN_BLOCK)
    packed = _pack_small(small).reshape(N_DEV, SMALL_PIECE, LANES)
    red = _reduce_grads(g_in, g_up_a, g_up_b, g_out, packed)

    grads, deltas, new_m, new_v = {}, {}, {}, {}
    for n, g in zip(big_names, red[:4]):
        d, nm, nv = _adamw(flat2d(w[n]), g, flat2d(m[n]), flat2d(v[n]))
        grads[n], deltas[n], new_m[n], new_v[n] = (a.reshape(shapes[n]) for a in (g, d, nm, nv))
    g_small = red[4].reshape(SMALL_ROWS, LANES)
    d, nm, nv = _adamw(_pack_small({n: w[n] for n in small_names}), g_small,
                       _pack_small({n: m[n] for n in small_names}), _pack_small({n: v[n] for n in small_names}))
    for src, dst in ((g_small, grads), (d, deltas), (nm, new_m), (nv, new_v)):
        dst.update(_unpack_small(src, shapes))

    loss = lax.psum(loss, ("x", "y", "c"))
    return (loss, dx[None], *[grads[n] for n in names], *[deltas[n] for n in names],
            *[new_m[n] for n in names], *[new_v[n] for n in names])
```

```python
import functools
import math

import jax
import jax.numpy as jnp
from jax import lax
from jax.experimental import pallas as pl
from jax.experimental.pallas import tpu as pltpu

F32 = jnp.float32
BF16 = jnp.bfloat16

D_MODEL = 1024
N_HEADS = 8
HEAD_DIM = 64
D_BRANCH = 512
D_IN = 4 * D_BRANCH + 3 * D_BRANCH + 2 * D_MODEL
N_GROUPS = 8
GROUP_DIM = 64
SGU_CHUNK = 128
SGU_SUBCHUNK = 64
GROUP_SHIFT = 6
EPS = 1e-6
LANES = 128
ATTN_BLOCK = 256
SCAN_PASSES = 1
ATTN_PAIRS = 2
N_CHIPS = 4
N_DEV = 8
MESH = pl.DeviceIdType.MESH

ADAM_LR = 0.001
ADAM_B1 = 0.9
ADAM_B2 = 0.999
ADAM_EPS = 1e-08
ADAM_WD = 0.01
ADAM_STEP = 10

COL_Q, COL_K, COL_V, COL_ZA, COL_UB, COL_VB, COL_ZB, COL_GA, COL_GB = 0, 1, 2, 3, 4, 5, 6, 7, 9

VMEM_LIMIT = 56 * 1024 * 1024

SMALL_ROWS = 1088
SMALL_PIECE = SMALL_ROWS // N_DEV


def _cparams(sem=None):
    return pltpu.CompilerParams(dimension_semantics=sem, vmem_limit_bytes=VMEM_LIMIT)


def _aligned(v, m):
    return v if isinstance(v, int) else pl.multiple_of(v, m)


def _sigmoid(x):
    return 1.0 / (1.0 + jnp.exp(-x))


def _gelu_and_grad(x):
    k = math.sqrt(2.0 / math.pi)
    x2 = x * x
    inner = k * (x + 0.044715 * x * x2)
    th = jnp.tanh(inner)
    g = 0.5 * x * (1.0 + th)
    dg = 0.5 * (1.0 + th) + 0.5 * x * (1.0 - th * th) * (k * (1.0 + 3.0 * 0.044715 * x2))
    return g, dg


def _split_dot(a, b_bf16, passes):
    out = None
    rem = a
    for _ in range(passes):
        part = rem.astype(BF16)
        d = jnp.dot(part, b_bf16, preferred_element_type=F32)
        out = d if out is None else out + d
        rem = rem - part.astype(F32)
    return out


def _dot_nt(a, b):
    return lax.dot_general(a, b, (((1,), (1,)), ((), ())), preferred_element_type=F32)


def _dot_tn(a, b):
    return lax.dot_general(a, b, (((0,), (0,)), ((), ())), preferred_element_type=F32)


def _in_proj(x, norm_g, w_in):
    s = x.shape[0]
    tm, tn = min(512, s), 512

    def body(x_ref, g_ref, w_ref, proj_ref, ht_ref, h_scr):
        @pl.when(pl.program_id(1) == 0)
        def _():
            xf = x_ref[...]
            r = lax.rsqrt(jnp.mean(xf * xf, axis=-1, keepdims=True) + EPS)
            h = xf * r * g_ref[...]
            h_scr[...] = h.astype(BF16)
            ht_ref[...] = h.T.astype(BF16)

        proj_ref[...] = jnp.dot(h_scr[...], w_ref[...], preferred_element_type=F32).astype(BF16)

    return pl.pallas_call(
        body, name="in_proj", grid=(s // tm, D_IN // tn),
        in_specs=[pl.BlockSpec((tm, D_MODEL), lambda i, j: (i, 0)),
                  pl.BlockSpec((1, D_MODEL), lambda i, j: (0, 0)),
                  pl.BlockSpec((D_MODEL, tn), lambda i, j: (0, j))],
        out_specs=[pl.BlockSpec((tm, tn), lambda i, j: (i, j)),
                   pl.BlockSpec((D_MODEL, tm), lambda i, j: (0, i))],
        out_shape=[jax.ShapeDtypeStruct((s, D_IN), BF16), jax.ShapeDtypeStruct((D_MODEL, s), BF16)],
        scratch_shapes=[pltpu.VMEM((tm, D_MODEL), BF16)],
        compiler_params=_cparams(("parallel", "arbitrary")),
    )(x, norm_g, w_in)


def _neg_softplus_parts(z):
    p = jnp.exp(-jnp.abs(z))
    return p, jnp.maximum(z, 0.0) + jnp.log(1.0 + p)


def _split_cat(a, passes):
    parts = []
    rem = a
    for k in range(passes):
        part = rem.astype(BF16)
        parts.append(part)
        if k + 1 < passes:
            rem = rem - part.astype(F32)
    return parts[0] if passes == 1 else jnp.concatenate(parts, axis=1)


def _tri(blk, upper, sign):
    row = lax.broadcasted_iota(jnp.int32, (blk, blk), 0)
    col = lax.broadcasted_iota(jnp.int32, (blk, blk), 1)
    keep = (row <= col) if upper else (row >= col)
    t = jnp.where(keep, sign, 0.0).astype(BF16)
    return t if SCAN_PASSES == 1 else jnp.concatenate([t] * SCAN_PASSES, axis=0)


def _attn_fwd(proj, blk, npairs):
    s = proj.shape[0]
    nq = s // blk
    scale = HEAD_DIM ** -0.5
    heads = tuple(range(2 * npairs))
    width = LANES * npairs

    def body(q_ref, k_ref, v_ref, za_ref, o_ref, ya_ref, rs_ref, acc_ref, r_ref, z_ref):
        i = pl.program_id(1)
        lane = lax.broadcasted_iota(jnp.int32, (blk, LANES), 1)
        lo_half = lane < HEAD_DIM
        qm = []
        for pr in range(npairs):
            q = q_ref[:, LANES * pr:LANES * (pr + 1)] * jnp.asarray(scale, BF16)
            zero = jnp.zeros_like(q)
            qm += [jnp.where(lo_half, q, zero), jnp.where(lo_half, zero, q)]
        row = lax.broadcasted_iota(jnp.int32, (blk, blk), 0)
        col = lax.broadcasted_iota(jnp.int32, (blk, blk), 1)
        before = col < row
        tneg = _tri(blk, False, -1.0)
        acc_ref[...] = jnp.zeros_like(acc_ref)
        r_ref[...] = jnp.zeros_like(r_ref)
        rs_ref[...] = jnp.zeros_like(rs_ref)

        def scores(j):
            ks = pl.multiple_of(j * blk, blk)
            return [_dot_nt(qm[h], k_ref[pl.ds(ks, blk), LANES * (h // 2):LANES * (h // 2 + 1)]) for h in heads]

        for h, zh in enumerate(scores(i)):
            z_ref[h] = zh

        def block(j, diag):
            ks = pl.multiple_of(j * blk, blk)
            vj = [v_ref[pl.ds(ks, blk), LANES * pr:LANES * (pr + 1)] for pr in range(npairs)]
            z = [z_ref[h] for h in heads]
            sp = [_neg_softplus_parts(z[h])[1] for h in heads]
            if diag:
                sp = [jnp.where(before, sp[h], 0.0) for h in heads]
            cin = [jnp.dot(_split_cat(sp[h], SCAN_PASSES), tneg, preferred_element_type=F32) for h in heads]
            for h, zh in enumerate(scores(jnp.maximum(j - 1, 0))):
                z_ref[h] = zh
            w = [jnp.exp(z[h] + cin[h]) for h in heads]
            if diag:
                w = [jnp.where(before, w[h], 0.0) for h in heads]
            pv = [jnp.dot(w[h].astype(BF16), vj[h // 2], preferred_element_type=F32) for h in heads]
            r = [r_ref[h] for h in heads]
            for h in heads:
                acc_ref[h] += pv[h] * jnp.exp(r[h])
                r_ref[h] = r[h] + cin[h][:, 0:1]
            for pr in range(npairs):
                rs_ref[pr] = jnp.where(lane == j, r[2 * pr], jnp.where(lane == j + HEAD_DIM, r[2 * pr + 1], rs_ref[pr]))

        block(i, True)

        def loop_body(jj, carry):
            block(i - 1 - jj, False)
            return carry

        lax.fori_loop(0, i, loop_body, 0)
        for pr in range(npairs):
            cols = slice(LANES * pr, LANES * (pr + 1))
            o = jnp.where(lo_half, acc_ref[2 * pr], acc_ref[2 * pr + 1])
            o_ref[:, cols] = o.astype(BF16)
            za = za_ref[:, cols].astype(F32)
            ya_ref[:, cols] = (o * (za * _sigmoid(za))).astype(BF16)

    n_steps = N_HEADS // (2 * npairs)
    return pl.pallas_call(
        body, name="attn_fwd", grid=(n_steps, nq),
        in_specs=[pl.BlockSpec((blk, width), lambda p, i: (i, n_steps * COL_Q + p)),
                  pl.BlockSpec((s, width), lambda p, i: (0, n_steps * COL_K + p)),
                  pl.BlockSpec((s, width), lambda p, i: (0, n_steps * COL_V + p)),
                  pl.BlockSpec((blk, width), lambda p, i: (i, n_steps * COL_ZA + p))],
        out_specs=[pl.BlockSpec((blk, width), lambda p, i: (i, p)),
                   pl.BlockSpec((blk, width), lambda p, i: (i, p)),
                   pl.BlockSpec((npairs, blk, LANES), lambda p, i: (p, i, 0))],
        out_shape=[jax.ShapeDtypeStruct((s, D_BRANCH), BF16), jax.ShapeDtypeStruct((s, D_BRANCH), BF16),
                   jax.ShapeDtypeStruct((N_HEADS // 2, s, LANES), F32)],
        scratch_shapes=[pltpu.VMEM((2 * npairs, blk, LANES), F32), pltpu.VMEM((2 * npairs, blk, 1), F32),
                        pltpu.VMEM((2 * npairs, blk, blk), F32)],
        compiler_params=_cparams(("parallel", "parallel")),
    )(proj, proj, proj, proj)


def _attn_bwd(proj, do, rsave, blk, npairs):
    s = proj.shape[0]
    nq = s // blk
    scale = HEAD_DIM ** -0.5
    heads = tuple(range(2 * npairs))
    width = LANES * npairs

    def body(q_ref, k_ref, v_ref, do_ref, rs_ref, dq_ref, dk_ref, dv_ref, dk_acc, dv_acc, dq_acc, e_ref):
        i = pl.program_id(1)
        lane = lax.broadcasted_iota(jnp.int32, (blk, LANES), 1)
        lo_half = lane < HEAD_DIM
        qm, dom = [], []
        for pr in range(npairs):
            cols = slice(LANES * pr, LANES * (pr + 1))
            q = q_ref[:, cols] * jnp.asarray(scale, BF16)
            zero = jnp.zeros_like(q)
            qm += [jnp.where(lo_half, q, zero), jnp.where(lo_half, zero, q)]
            dout = do_ref[:, cols].astype(F32)
            dom += [jnp.where(lo_half, dout, 0.0), jnp.where(lo_half, 0.0, dout)]
        row = lax.broadcasted_iota(jnp.int32, (blk, blk), 0)
        col = lax.broadcasted_iota(jnp.int32, (blk, blk), 1)
        before = col < row
        tneg = _tri(blk, False, -1.0)
        tfwd = _tri(blk, True, 1.0)

        @pl.when(i == 0)
        def _():
            dk_acc[...] = jnp.zeros_like(dk_acc)
            dv_acc[...] = jnp.zeros_like(dv_acc)

        dq_acc[...] = jnp.zeros_like(dq_acc)
        e_ref[...] = jnp.zeros_like(e_ref)

        def block(j, diag):
            ks = pl.multiple_of(j * blk, blk)
            kj = [k_ref[pl.ds(ks, blk), LANES * pr:LANES * (pr + 1)] for pr in range(npairs)]
            vj = [v_ref[pl.ds(ks, blk), LANES * pr:LANES * (pr + 1)] for pr in range(npairs)]
            z = [_dot_nt(qm[h], kj[h // 2]) for h in heads]
            er = [jnp.exp(jnp.sum(jnp.where(lane == j + HEAD_DIM * (h % 2), rs_ref[h // 2], 0.0), axis=-1,
                                  keepdims=True)) for h in heads]
            dos = [(dom[h] * er[h]).astype(BF16) for h in heads]
            dw = [_dot_nt(dos[h], vj[h // 2]) for h in heads]
            psp = [_neg_softplus_parts(z[h]) for h in heads]
            sp = [psp[h][1] for h in heads]
            if diag:
                sp = [jnp.where(before, sp[h], 0.0) for h in heads]
            cin = [jnp.dot(_split_cat(sp[h], SCAN_PASSES), tneg, preferred_element_type=F32) for h in heads]
            w = [jnp.exp(z[h] + cin[h]) for h in heads]
            if diag:
                w = [jnp.where(before, w[h], 0.0) for h in heads]
            e =[dw[h] * w[h] for h in heads]
            eincl = [jnp.dot(_split_cat(e[h], SCAN_PASSES), tfwd, preferred_element_type=F32) + e_ref[h]
                     for h in heads]
            dz = []
            for h in heads:
                p = psp[h][0]
                beta = jnp.where(z[h] >= 0.0, 1.0, p) / (1.0 + p)
                d = e[h] - beta * eincl[h]
                dz.append((jnp.where(before, d, 0.0) if diag else d).astype(BF16))
            wb = [w[h].astype(BF16) for h in heads]
            for h in heads:
                e_ref[h] = eincl[h][:, blk - 1:blk]
                dq_acc[h] += jnp.dot(dz[h], kj[h // 2], preferred_element_type=F32)
            for pr in range(npairs):
                cols = slice(LANES * pr, LANES * (pr + 1))
                h0, h1 = 2 * pr, 2 * pr + 1
                dk_acc[pl.ds(ks, blk), cols] += _dot_tn(dz[h0], qm[h0]) + _dot_tn(dz[h1], qm[h1])
                dv_acc[pl.ds(ks, blk), cols] += _dot_tn(wb[h0], dos[h0]) + _dot_tn(wb[h1], dos[h1])

        def loop_body(j, carry):
            block(j, False)
            return carry

        lax.fori_loop(0, i, loop_body, 0)
        block(i, True)
        for pr in range(npairs):
            dq = jnp.where(lo_half, dq_acc[2 * pr], dq_acc[2 * pr + 1]) * scale
            dq_ref[:, LANES * pr:LANES * (pr + 1)] = dq.astype(BF16)

        @pl.when(i == nq - 1)
        def _():
            dk_ref[...] = dk_acc[...].astype(BF16)
            dv_ref[...] = dv_acc[...].astype(BF16)

    n_steps = N_HEADS // (2 * npairs)
    return pl.pallas_call(
        body, name="attn_bwd", grid=(n_steps, nq),
        in_specs=[pl.BlockSpec((blk, width), lambda p, i: (i, n_steps * COL_Q + p)),
                  pl.BlockSpec((s, width), lambda p, i: (0, n_steps * COL_K + p)),
                  pl.BlockSpec((s, width), lambda p, i: (0, n_steps * COL_V + p)),
                  pl.BlockSpec((blk, width), lambda p, i: (i, p)),
                  pl.BlockSpec((npairs, blk, LANES), lambda p, i: (p, i, 0))],
        out_specs=[pl.BlockSpec((blk, width), lambda p, i: (i, p)),
                   pl.BlockSpec((s, width), lambda p, i: (0, p)),
                   pl.BlockSpec((s, width), lambda p, i: (0, p))],
        out_shape=[jax.ShapeDtypeStruct((s, D_BRANCH), BF16)] * 3,
        scratch_shapes=[pltpu.VMEM((s, width), F32), pltpu.VMEM((s, width), F32),
                        pltpu.VMEM((2 * npairs, blk, LANES), F32), pltpu.VMEM((2 * npairs, blk, 1), F32)],
        compiler_params=_cparams(("parallel", "arbitrary")),
    )(proj, proj, proj, do, rsave)


def _group_avg_matrix():
    a = lax.broadcasted_iota(jnp.int32, (LANES, LANES), 0) >> GROUP_SHIFT
    b = lax.broadcasted_iota(jnp.int32, (LANES, LANES), 1) >> GROUP_SHIFT
    return jnp.where(a == b, 1.0 / GROUP_DIM, 0.0).astype(BF16)


def _group_mean(a, avg):
    parts = [_split_dot(a[:, LANES * k:LANES * (k + 1)], avg, 3) for k in range(D_BRANCH // LANES)]
    return jnp.concatenate(parts, axis=1)


def _sgu_forward_parts(ub, vb, ln_g, ln_b, avg):
    ug, dug = _gelu_and_grad(ub)
    vg, dvg = _gelu_and_grad(vb)
    mu = _group_mean(vg, avg)
    d = vg - mu
    var = _group_mean(d * d, avg)
    rstd = lax.rsqrt(var + EPS)
    vhat = d * rstd
    vn = vhat * ln_g + ln_b
    return ug, dug, dvg, rstd, vhat, vn


def _sgu_mix(w_ref, src_bf16, n_chunks):
    lane = lax.broadcasted_iota(jnp.int32, (SGU_CHUNK, LANES), 1)
    lo_half = lane < GROUP_DIM
    rows = []
    for n in range(n_chunks):
        slabs = []
        for a in range(D_BRANCH // LANES):
            blk = src_bf16[SGU_CHUNK * n:SGU_CHUNK * (n + 1), LANES * a:LANES * (a + 1)]
            zero = jnp.zeros_like(blk)
            m0 = jnp.dot(w_ref[2 * a], jnp.where(lo_half, blk, zero), preferred_element_type=F32)
            m1 = jnp.dot(w_ref[2 * a + 1], jnp.where(lo_half, zero, blk), preferred_element_type=F32)
            slabs.append(m0 + m1)
        rows.append(jnp.concatenate(slabs, axis=1))
    return jnp.concatenate(rows, axis=0)


def _sgu_fwd(proj, ln_g, ln_b, w_mask, bias_full):
    s = proj.shape[0]
    tm = min(512, s)
    n_chunks = tm // SGU_CHUNK

    def body(ub_ref, vb_ref, zb_ref, g_ref, b_ref, w_ref, bias_ref, yb_ref):
        avg = _group_avg_matrix()
        ug, _, _, _, _, vn = _sgu_forward_parts(ub_ref[...].astype(F32), vb_ref[...].astype(F32),
                                                g_ref[...], b_ref[...], avg)
        mixed = _sgu_mix(w_ref, vn.astype(BF16), n_chunks) + jnp.concatenate([bias_ref[...]] * n_chunks, axis=0)
        zb = zb_ref[...].astype(F32)
        yb_ref[...] = (ug * mixed * (zb * _sigmoid(zb))).astype(BF16)

    col = lambda c: pl.BlockSpec((tm, D_BRANCH), lambda i: (i, c))
    full = lambda shape: pl.BlockSpec(shape, lambda i: (0,) * len(shape))
    return pl.pallas_call(
        body, name="sgu_fwd", grid=(s // tm,),
        in_specs=[col(COL_UB), col(COL_VB), col(COL_ZB), full((1, D_BRANCH)), full((1, D_BRANCH)),
                  full((N_GROUPS, SGU_CHUNK, SGU_CHUNK)), full((SGU_CHUNK, D_BRANCH))],
        out_specs=pl.BlockSpec((tm, D_BRANCH), lambda i: (i, 0)),
        out_shape=jax.ShapeDtypeStruct((s, D_BRANCH), BF16),
        compiler_params=_cparams(("parallel",)),
    )(proj, proj, proj, ln_g, ln_b, w_mask, bias_full)


def _sgu_bwd(proj, dyb, ln_g, ln_b, w_mask, w_mask_t, bias_full):
    s = proj.shape[0]
    tm = min(512, s)
    n_chunks = tm // SGU_CHUNK
    n_steps = s // tm

    def body(ub_ref, vb_ref, zb_ref, dyb_ref, g_ref, b_ref, w_ref, wt_ref, bias_ref,
             dsgu_ref, dw_ref, db_ref, dg_ref, dbeta_ref, dmix_acc):
        i = pl.program_id(0)

        @pl.when(i == 0)
        def _():
            dw_ref[...] = jnp.zeros_like(dw_ref)
            dg_ref[...] = jnp.zeros_like(dg_ref)
            dbeta_ref[...] = jnp.zeros_like(dbeta_ref)
            dmix_acc[...] = jnp.zeros_like(dmix_acc)

        avg = _group_avg_matrix()
        ln_gv = g_ref[...]
        ug, dug, dvg, rstd, vhat, vn = _sgu_forward_parts(ub_ref[...].astype(F32), vb_ref[...].astype(F32),
                                                          ln_gv, b_ref[...], avg)
        vnb = vn.astype(BF16)
        mixed = _sgu_mix(w_ref, vnb, n_chunks) + jnp.concatenate([bias_ref[...]] * n_chunks, axis=0)
        zb = zb_ref[...].astype(F32)
        sg = _sigmoid(zb)
        sz = zb * sg
        dsz = sg * (1.0 + zb * (1.0 - sg))
        dy = dyb_ref[...].astype(F32)
        dmixed = dy * ug * sz
        du = dy * mixed * sz * dug
        dzb = dy * ug * mixed * dsz
        dmb = dmixed.astype(BF16)
        dvn = _sgu_mix(wt_ref, dmb, n_chunks)

        lane = lax.broadcasted_iota(jnp.int32, (SGU_CHUNK, LANES), 1)
        lo_half = lane < GROUP_DIM
        dm_sum = None
        for n in range(n_chunks):
            rows = slice(SGU_CHUNK * n, SGU_CHUNK * (n + 1))
            dm_sum = dmixed[rows] if dm_sum is None else dm_sum + dmixed[rows]
            for a in range(D_BRANCH // LANES):
                cols = slice(LANES * a, LANES * (a + 1))
                dblk = dmb[rows, cols]
                vblk = vnb[rows, cols]
                zero = jnp.zeros_like(dblk)
                dw_ref[2 * a] += _dot_nt(jnp.where(lo_half, dblk, zero), vblk)
                dw_ref[2 * a + 1] += _dot_nt(jnp.where(lo_half, zero, dblk), vblk)
        dmix_acc[...] += dm_sum

        dg_ref[...] += jnp.sum(dvn * vhat, axis=0, keepdims=True)
        dbeta_ref[...] += jnp.sum(dvn, axis=0, keepdims=True)
        dvh = dvn * ln_gv
        m1 = _group_mean(dvh, avg)
        m2 = _group_mean(dvh * vhat, avg)
        dv = rstd * (dvh - m1 - vhat * m2) * dvg
        dsgu_ref[:, 0:D_BRANCH] = du.astype(BF16)
        dsgu_ref[:, D_BRANCH:2 * D_BRANCH] = dv.astype(BF16)
        dsgu_ref[:, 2 * D_BRANCH:3 * D_BRANCH] = dzb.astype(BF16)

        @pl.when(i == n_steps - 1)
        def _():
            pos = lax.broadcasted_iota(jnp.int32, (SGU_CHUNK, SGU_CHUNK), 0) >> GROUP_SHIFT
            src = lax.broadcasted_iota(jnp.int32, (SGU_CHUNK, SGU_CHUNK), 1) >> GROUP_SHIFT
            keep = src <= pos
            for g in range(N_GROUPS):
                dw_ref[g] = jnp.where(keep, dw_ref[g], 0.0)
            grp = lax.broadcasted_iota(jnp.int32, (D_BRANCH, LANES), 0) >> GROUP_SHIFT
            sel = (grp == lax.broadcasted_iota(jnp.int32, (D_BRANCH, LANES), 1)).astype(BF16)
            db_ref[...] = _split_dot(dmix_acc[...], sel, 3)

    col = lambda c: pl.BlockSpec((tm, D_BRANCH), lambda i: (i, c))
    full = lambda shape: pl.BlockSpec(shape, lambda i: (0,) * len(shape))
    return pl.pallas_call(
        body, name="sgu_bwd", grid=(n_steps,),
        in_specs=[col(COL_UB), col(COL_VB), col(COL_ZB), pl.BlockSpec((tm, D_BRANCH), lambda i: (i, 0)),
                  full((1, D_BRANCH)), full((1, D_BRANCH)),
                  full((N_GROUPS, SGU_CHUNK, SGU_CHUNK)), full((N_GROUPS, SGU_CHUNK, SGU_CHUNK)),
                  full((SGU_CHUNK, D_BRANCH))],
        out_specs=[pl.BlockSpec((tm, 3 * D_BRANCH), lambda i: (i, 0)),
                   full((N_GROUPS, SGU_CHUNK, SGU_CHUNK)), full((SGU_CHUNK, LANES)),
                   full((1, D_BRANCH)), full((1, D_BRANCH))],
        out_shape=[jax.ShapeDtypeStruct((s, 3 * D_BRANCH), BF16),
                   jax.ShapeDtypeStruct((N_GROUPS, SGU_CHUNK, SGU_CHUNK), F32),
                   jax.ShapeDtypeStruct((SGU_CHUNK, LANES), F32),
                   jax.ShapeDtypeStruct((1, D_BRANCH), F32), jax.ShapeDtypeStruct((1, D_BRANCH), F32)],
        scratch_shapes=[pltpu.VMEM((SGU_CHUNK, D_BRANCH), F32)],
        compiler_params=_cparams(("arbitrary",)),
    )(proj, proj, proj, dyb, ln_g, ln_b, w_mask, w_mask_t, bias_full)


def _mid(proj, ya, yb, o, x, target, final_g, w_up_a, w_up_b, w_out):
    s = x.shape[0]
    tm = min(256, s)
    n_steps = s // tm
    half = D_MODEL // 2

    def body(ya_ref, yb_ref, o_ref, za_ref, ga0_ref, ga1_ref, gb0_ref, gb1_ref, x_ref, t_ref, gf_ref,
             wa_ref, wb_ref, wo_ref,
             dzg_ref, do_ref, dyb_ref, dx2_ref, gwo_ref, gwa_ref, gwb_ref, loss_ref, dgf_ref,
             acc_o, acc_a, acc_b):
        i = pl.program_id(0)

        @pl.when(i == 0)
        def _():
            acc_o[...] = jnp.zeros_like(acc_o)
            acc_a[...] = jnp.zeros_like(acc_a)
            acc_b[...] = jnp.zeros_like(acc_b)
            loss_ref[...] = jnp.zeros_like(loss_ref)
            dgf_ref[...] = jnp.zeros_like(dgf_ref)

        ya_v = ya_ref[...]
        yb_v = yb_ref[...]
        pa = jnp.dot(ya_v, wa_ref[...], preferred_element_type=F32)
        pb = jnp.dot(yb_v, wb_ref[...], preferred_element_type=F32)
        sa = _sigmoid(jnp.concatenate([ga0_ref[...], ga1_ref[...]], axis=1).astype(F32))
        sb = _sigmoid(jnp.concatenate([gb0_ref[...], gb1_ref[...]], axis=1).astype(F32))
        merged = (sa * pa + sb * pb).astype(BF16)
        x2 = x_ref[...] + jnp.dot(merged, wo_ref[...], preferred_element_type=F32)
        r2 = lax.rsqrt(jnp.mean(x2 * x2, axis=-1, keepdims=True) + EPS)
        xh = x2 * r2
        gf = gf_ref[...]
        diff = xh * gf - t_ref[...]
        loss_ref[...] += 0.5 * jnp.sum(jnp.mean(diff * diff, axis=-1, keepdims=True))
        dy = diff * (1.0 / D_MODEL)
        dgf_ref[...] += jnp.sum(dy * xh, axis=0, keepdims=True)
        dyg = dy * gf
        dx2 = r2 * (dyg - xh * jnp.mean(dyg * xh, axis=-1, keepdims=True))
        dx2_ref[...] = dx2
        dx2b = dx2.astype(BF16)
        dmerged = _dot_nt(dx2b, wo_ref[...])
        acc_o[...] += _dot_tn(merged, dx2b)
        dpa = dmerged * sa
        dpb = dmerged * sb
        dzg_ref[:, D_BRANCH:D_BRANCH + D_MODEL] = (dpa * pa * (1.0 - sa)).astype(BF16)
        dzg_ref[:, D_BRANCH + D_MODEL:D_BRANCH + 2 * D_MODEL] = (dpb * pb * (1.0 - sb)).astype(BF16)
        dpab = dpa.astype(BF16)
        dpbb = dpb.astype(BF16)
        acc_a[...] += _dot_tn(ya_v, dpab)
        acc_b[...] += _dot_tn(yb_v, dpbb)
        dya = _dot_nt(dpab, wa_ref[...])
        dyb_ref[...] = _dot_nt(dpbb, wb_ref[...]).astype(BF16)
        za = za_ref[...].astype(F32)
        sg = _sigmoid(za)
        do_ref[...] = (dya * (za * sg)).astype(BF16)
        dzg_ref[:, 0:D_BRANCH] = (dya * o_ref[...].astype(F32) * (sg * (1.0 + za * (1.0 - sg)))).astype(BF16)

        @pl.when(i == n_steps - 1)
        def _():
            gwo_ref[...] = acc_o[...].astype(BF16)
            gwa_ref[...] = acc_a[...].astype(BF16)
            gwb_ref[...] = acc_b[...].astype(BF16)

    tok = lambda w: pl.BlockSpec((tm, w), lambda i: (i, 0))
    col = lambda c: pl.BlockSpec((tm, half), lambda i: (i, c))
    full = lambda shape: pl.BlockSpec(shape, lambda i: (0,) * len(shape))
    return pl.pallas_call(
        body, name="mid", grid=(n_steps,),
        in_specs=[tok(D_BRANCH), tok(D_BRANCH), tok(D_BRANCH), col(COL_ZA), col(COL_GA), col(COL_GA + 1),
                  col(COL_GB), col(COL_GB + 1), tok(D_MODEL), tok(D_MODEL), full((1, D_MODEL)),
                  full((D_BRANCH, D_MODEL)), full((D_BRANCH, D_MODEL)), full((D_MODEL, D_MODEL))],
        out_specs=[tok(D_BRANCH + 2 * D_MODEL), tok(D_BRANCH), tok(D_BRANCH), tok(D_MODEL),
                   full((D_MODEL, D_MODEL)), full((D_BRANCH, D_MODEL)), full((D_BRANCH, D_MODEL)),
                   full((8, LANES)), full((1, D_MODEL))],
        out_shape=[jax.ShapeDtypeStruct((s, D_BRANCH + 2 * D_MODEL), BF16),
                   jax.ShapeDtypeStruct((s, D_BRANCH), BF16), jax.ShapeDtypeStruct((s, D_BRANCH), BF16),
                   jax.ShapeDtypeStruct((s, D_MODEL), F32),
                   jax.ShapeDtypeStruct((D_MODEL, D_MODEL), BF16),
                   jax.ShapeDtypeStruct((D_BRANCH, D_MODEL), BF16), jax.ShapeDtypeStruct((D_BRANCH, D_MODEL), BF16),
                   jax.ShapeDtypeStruct((8, LANES), F32), jax.ShapeDtypeStruct((1, D_MODEL), F32)],
        scratch_shapes=[pltpu.VMEM((D_MODEL, D_MODEL), F32), pltpu.VMEM((D_BRANCH, D_MODEL), F32),
                        pltpu.VMEM((D_BRANCH, D_MODEL), F32)],
        compiler_params=_cparams(("arbitrary",)),
    )(ya, yb, o, proj, proj, proj, proj, proj, x, target, final_g, w_up_a, w_up_b, w_out)


def _dwin_piece(ht, piece, tile_of, prev):
    s = ht.shape[1]
    n_tiles = piece.shape[1] // D_BRANCH

    def body(ht_ref, p_ref, *rest):
        out_ref = rest[-1]
        out_ref[...] = jnp.dot(ht_ref[...], p_ref[...], preferred_element_type=F32).astype(BF16)

    in_specs = [pl.BlockSpec((D_MODEL, s), lambda j: (0, 0)), pl.BlockSpec((s, D_BRANCH), lambda j: (0, j))]
    args = [ht, piece]
    aliases = {}
    if prev is not None:
        in_specs.append(pl.BlockSpec(memory_space=pl.ANY))
        args.append(prev)
        aliases = {2: 0}
    return pl.pallas_call(
        body, name="dwin_piece", grid=(n_tiles,),
        in_specs=in_specs,
        out_specs=pl.BlockSpec((D_MODEL, D_BRANCH), lambda j: (0, tile_of(j))),
        out_shape=jax.ShapeDtypeStruct((D_MODEL, D_IN), BF16),
        input_output_aliases=aliases,
        compiler_params=_cparams(("parallel",)),
    )(*args)


def _dh_dx(pieces, w_in, x, norm_g, dx2):
    s = x.shape[0]
    tm = min(256, s)
    arrays = []
    for arr, _, _, _ in pieces:
        if not any(arr is a for a in arrays):
            arrays.append(arr)
    n_arr = len(arrays)
    plan = [([k for k, a in enumerate(arrays) if a is arr][0], wcol, off, width) for arr, wcol, off, width in pieces]

    def body(*refs):
        p_refs = refs[:n_arr]
        w_ref, x_ref, g_ref, dx2_ref, dx_ref, dg_ref = refs[n_arr:]

        @pl.when(pl.program_id(0) == 0)
        def _():
            dg_ref[...] = jnp.zeros_like(dg_ref)

        dh = None
        for k, wcol, off, width in plan:
            d = _dot_nt(p_refs[k][:, off:off + width], w_ref[:, wcol:wcol + width])
            dh = d if dh is None else dh + d
        xf = x_ref[...]
        r = lax.rsqrt(jnp.mean(xf * xf, axis=-1, keepdims=True) + EPS)
        xh = xf * r
        dg_ref[...] += jnp.sum(dh * xh, axis=0, keepdims=True)
        dhg = dh * g_ref[...]
        dx_ref[...] = r * (dhg - xh * jnp.mean(dhg * xh, axis=-1, keepdims=True)) + dx2_ref[...]

    tok = lambda w: pl.BlockSpec((tm, w), lambda i: (i, 0))
    full = lambda shape: pl.BlockSpec(shape, lambda i: (0,) * len(shape))
    return pl.pallas_call(
        body, name="dh_dx", grid=(s // tm,),
        in_specs=[tok(a.shape[1]) for a in arrays] + [full((D_MODEL, D_IN)), tok(D_MODEL), full((1, D_MODEL)),
                                                      tok(D_MODEL)],
        out_specs=[tok(D_MODEL), full((1, D_MODEL))],
        out_shape=[jax.ShapeDtypeStruct((s, D_MODEL), F32), jax.ShapeDtypeStruct((1, D_MODEL), F32)],
        compiler_params=_cparams(("arbitrary",)),
    )(*arrays, w_in, x, norm_g, dx2)


def _adamw(w, g, m, v):
    rows, cols = w.shape
    tr = max(t for t in range(8, 257, 8) if rows % t == 0)
    c1 =1.0 - ADAM_B1 ** ADAM_STEP
    c2 = 1.0 - ADAM_B2 ** ADAM_STEP

    def body(w_ref, g_ref, m_ref, v_ref, d_ref, nm_ref, nv_ref):
        gv = g_ref[...]
        nm = ADAM_B1 * m_ref[...] + (1.0 - ADAM_B1) * gv
        nv = ADAM_B2 * v_ref[...] + (1.0 - ADAM_B2) * (gv * gv)
        d_ref[...] = -ADAM_LR * ((nm / c1) / (jnp.sqrt(nv / c2) + ADAM_EPS) + ADAM_WD * w_ref[...])
        nm_ref[...] = nm
        nv_ref[...] = nv

    spec = pl.BlockSpec((tr, cols), lambda i: (i, 0))
    return pl.pallas_call(
        body, name="adamw", grid=(rows // tr,),
        in_specs=[spec] * 4, out_specs=[spec] * 3,
        out_shape=[jax.ShapeDtypeStruct((rows, cols), F32)] * 3,
        compiler_params=_cparams(("parallel",)),
    )(w, g, m, v)


def _place():
    x, y, c = lax.axis_index("x"), lax.axis_index("y"), lax.axis_index("c")
    return x, y, c


def _gather_weights(w_in, w_up_a, w_up_b, w_out):
    shards = (w_in, w_up_a, w_up_b, w_out)
    n_arr = len(shards)
    col_sharded = (True, True, True, False)
    full_shapes = ((D_MODEL, D_IN), (D_BRANCH, D_MODEL), (D_BRANCH, D_MODEL), (D_MODEL, D_MODEL))

    def body(*refs):
        src = refs[:n_arr]
        out = refs[n_arr:2 * n_arr]
        stage = refs[2 * n_arr:3 * n_arr]
        cast = refs[3 * n_arr:4 * n_arr]
        send_sems, recv_sems, local_sems = refs[4 * n_arr:]
        x, y, c = _place()
        chip = 2 * x + y
        sibling = (x, y, 1 - c)
        others = [(1 - x, y), (x, 1 - y), (1 - x, 1 - y)]

        def region(a, chip_idx, half):
            r, w = shards[a].shape
            hr = r // 2
            if col_sharded[a]:
                return out[a].at[pl.ds(_aligned(half * hr, 16), hr), pl.ds(_aligned(chip_idx * w, LANES), w)]
            return out[a].at[pl.ds(_aligned(chip_idx * r + half * hr, 16), hr), :]

        loads = [pltpu.make_async_copy(src[a], stage[a], local_sems.at[a]) for a in range(n_arr)]
        for cp in loads:
            cp.start()
        for a in range(n_arr):
            loads[a].wait()
            cast[a][...] = stage[a][...].astype(BF16)
        stores = []
        for a in range(n_arr):
            hr = shards[a].shape[0] // 2
            for half in range(2):
                cp = pltpu.make_async_copy(cast[a].at[pl.ds(half * hr, hr), :], region(a, chip, half),
                                           local_sems.at[n_arr + 2 * a + half])
                cp.start()
                stores.append(cp)

        def remote(k, a, chip_idx, half, to, from_vmem):
            hr = shards[a].shape[0] // 2
            s_ref = cast[a].at[pl.ds(_aligned(half * hr, 16), hr), :] if from_vmem else region(a, chip_idx, half)
            return pltpu.make_async_remote_copy(src_ref=s_ref, dst_ref=region(a, chip_idx, half),
                                                send_sem=send_sems.at[k], recv_sem=recv_sems.at[k],
                                                device_id=to, device_id_type=MESH)

        first = []
        for j, (ox, oy) in enumerate(others):
            for a in range(n_arr):
                cp = remote(n_arr * j + a, a, chip, c, (ox, oy, c), True)
                cp.start()
                first.append(cp)
        passed = []
        for j, (ox, oy) in enumerate(others):
            ochip = 2 * ox + oy
            for a in range(n_arr):
                k = n_arr * j + a
                remote(k, a, ochip, c, sibling, False).wait_recv()
                cp = remote(3 * n_arr + k, a, ochip, c, sibling, False)
                cp.start()
                passed.append(cp)
        for j, (ox, oy) in enumerate(others):
            ochip = 2 * ox + oy
            for a in range(n_arr):
                remote(3 * n_arr + n_arr * j + a, a, ochip, 1 - c, sibling, False).wait_recv()
        for cp in first + passed:
            cp.wait_send()
        for cp in stores:
            cp.wait()

    any_spec = pl.BlockSpec(memory_space=pl.ANY)
    return pl.pallas_call(
        body, name="gather_weights",
        in_specs=[any_spec] * n_arr, out_specs=[any_spec] * n_arr,
        out_shape=[jax.ShapeDtypeStruct(sh, BF16) for sh in full_shapes],
        scratch_shapes=[pltpu.VMEM(a.shape, F32) for a in shards] + [pltpu.VMEM(a.shape, BF16) for a in shards]
        + [pltpu.SemaphoreType.DMA((6 * n_arr,)), pltpu.SemaphoreType.DMA((6 * n_arr,)),
           pltpu.SemaphoreType.DMA((3 * n_arr,))],
        compiler_params=pltpu.CompilerParams(vmem_limit_bytes=VMEM_LIMIT, has_side_effects=True),
    )(*shards)


def _reduce_grads(g_in, g_up_a, g_up_b, g_out, g_small):
    big = (g_in, g_up_a, g_up_b, g_out)
    n_big = len(big)
    col_sharded = (True, True, True, False)
    piece_shapes = []
    for a, arr in enumerate(big):
        r, w = arr.shape
        piece_shapes.append((r // 2, w // N_CHIPS) if col_sharded[a] else (r // (2 * N_CHIPS), w))
    shard_shapes = [(2 * r, w) for r, w in piece_shapes]
    n_arr = n_big + 1

    def body(*refs):
        src = refs[:n_arr]
        out = refs[n_arr:2 * n_arr]
        slots = refs[2 * n_arr:3 * n_arr]
        sums = refs[3 * n_arr:4 * n_arr]
        send1, recv1, send2, recv2, local_sems = refs[4 * n_arr:]
        x, y, c = _place()
        me = 4 * x + 2 * y + c

        def piece_of(a, dev):
            chip_idx, half = dev // 2, dev % 2
            if a == n_big:
                return src[a].at[dev]
            r, w = piece_shapes[a]
            if col_sharded[a]:
                return src[a].at[pl.ds(pl.multiple_of(half * r, 16), r), pl.ds(pl.multiple_of(chip_idx * w, LANES), w)]
            return src[a].at[pl.ds(pl.multiple_of(dev * r, 16), r), :]

        def dev_id(dev):
            return (dev // 4, (dev // 2) % 2, dev % 2)

        own = [pltpu.make_async_copy(piece_of(a, me), slots[a].at[me], local_sems.at[a]) for a in range(n_arr)]
        for cp in own:
            cp.start()
        sends = []
        for d in range(1, N_DEV):
            peer = (me + d) % N_DEV
            for a in range(n_arr):
                cp = pltpu.make_async_remote_copy(
                    src_ref=piece_of(a, peer), dst_ref=slots[a].at[me],
                    send_sem=send1.at[n_arr * peer + a], recv_sem=recv1.at[n_arr * me + a],
                    device_id=dev_id(peer), device_id_type=MESH)
                cp.start()
                sends.append(cp)
        for cp in own:
            cp.wait()
        for d in range(1, N_DEV):
            peer = (me + d) % N_DEV
            for a in range(n_arr):
                pltpu.make_async_remote_copy(
                    src_ref=piece_of(a, peer), dst_ref=slots[a].at[peer],
                    send_sem=send1.at[n_arr * peer + a], recv_sem=recv1.at[n_arr * peer + a],
                    device_id=dev_id(peer), device_id_type=MESH).wait_recv()
        for a in range(n_arr):
            rows = slots[a].shape[1]
            step = 64 if rows % 64 == 0 else 8

            def add_rows(t, carry, a=a, step=step):
                r0 = pl.multiple_of(t * step, step)
                total = slots[a][0, pl.ds(r0, step), :].astype(F32)
                for dev in range(1, N_DEV):
                    total = total + slots[a][dev, pl.ds(r0, step), :].astype(F32)
                sums[a][pl.ds(r0, step), :] = total
                return carry

            lax.fori_loop(0, rows // step, add_rows, 0)
        shares = []
        keeps = []
        for a in range(n_big):
            r, w = piece_shapes[a]
            dst = out[a].at[pl.ds(pl.multiple_of(c * r, 8), r), :]
            cp = pltpu.make_async_remote_copy(src_ref=sums[a], dst_ref=dst, send_sem=send2.at[a], recv_sem=recv2.at[a],
                                              device_id=(x, y, 1 - c), device_id_type=MESH)
            cp.start()
            shares.append(cp)
            kp = pltpu.make_async_copy(sums[a], dst, local_sems.at[n_arr + a])
            kp.start()
            keeps.append(kp)
        kp = pltpu.make_async_copy(sums[n_big], out[n_big].at[me], local_sems.at[n_arr + n_big])
        kp.start()
        keeps.append(kp)
        for d in range(1, N_DEV):
            peer = (me + d) % N_DEV
            cp = pltpu.make_async_remote_copy(src_ref=sums[n_big], dst_ref=out[n_big].at[me],
                                              send_sem=send2.at[n_big + peer], recv_sem=recv2.at[n_big + me],
                                              device_id=dev_id(peer), device_id_type=MESH)
            cp.start()
            shares.append(cp)
        for a in range(n_big):
            r, w = piece_shapes[a]
            other = out[a].at[pl.ds(pl.multiple_of((1 - c) * r, 8), r), :]
            pltpu.make_async_remote_copy(src_ref=sums[a], dst_ref=other, send_sem=send2.at[a], recv_sem=recv2.at[a],
                                         device_id=(x, y, 1 - c), device_id_type=MESH).wait_recv()
        for d in range(1, N_DEV):
            peer = (me + d) % N_DEV
            pltpu.make_async_remote_copy(src_ref=sums[n_big], dst_ref=out[n_big].at[peer],
                                         send_sem=send2.at[n_big + peer], recv_sem=recv2.at[n_big + peer],
                                         device_id=dev_id(peer), device_id_type=MESH).wait_recv()
        for cp in sends + shares:
            cp.wait_send()
        for kp in keeps:
            kp.wait()

    any_spec = pl.BlockSpec(memory_space=pl.ANY)
    small_piece = (SMALL_PIECE, LANES)
    return pl.pallas_call(
        body, name="reduce_grads",
        in_specs=[any_spec] * n_arr, out_specs=[any_spec] * n_arr,
        out_shape=[jax.ShapeDtypeStruct(sh, F32) for sh in shard_shapes]
        + [jax.ShapeDtypeStruct((N_DEV,) + small_piece, F32)],
        scratch_shapes=[pltpu.VMEM((N_DEV,) + sh, BF16) for sh in piece_shapes]
        + [pltpu.VMEM((N_DEV,) + small_piece, F32)]
        + [pltpu.VMEM(sh, F32) for sh in piece_shapes] + [pltpu.VMEM(small_piece, F32)]
        + [pltpu.SemaphoreType.DMA((N_DEV * n_arr,)), pltpu.SemaphoreType.DMA((N_DEV * n_arr,)),
           pltpu.SemaphoreType.DMA((n_big + N_DEV,)), pltpu.SemaphoreType.DMA((n_big + N_DEV,)),
           pltpu.SemaphoreType.DMA((2 * n_arr,))],
        compiler_params=pltpu.CompilerParams(vmem_limit_bytes=VMEM_LIMIT, has_side_effects=True),
    )(*big, g_small)


_SMALL_PARTS = (("norm_g", 8), ("sgu_ln_g", 8), ("sgu_ln_b", 8), ("w_spatial", 1024), ("b_spatial", 8),
                ("final_norm_g", 8))


def _pack_small(parts):
    rows = []
    used = 0
    for name, n_rows in _SMALL_PARTS:
        a = parts[name].reshape(-1, LANES).astype(F32)
        a = jnp.pad(a, ((0, n_rows - a.shape[0]), (0, 0)))
        rows.append(a)
        used += n_rows
    rows.append(jnp.zeros((SMALL_ROWS - used, LANES), F32))
    return jnp.concatenate(rows, axis=0)


def _unpack_small(packed, shapes):
    out = {}
    r0 = 0
    for name, n_rows in _SMALL_PARTS:
        n = math.prod(shapes[name])
        out[name] = packed[r0:r0 + n // LANES].reshape(shapes[name])
        r0 += n_rows
    return out


def _local_step(x, target, norm_g, w_in, sgu_ln_g, sgu_ln_b, w_spatial, b_spatial, w_up_a, w_up_b, w_out,
                final_norm_g, blk):
    pos = jnp.arange(SGU_CHUNK)
    keep = (pos[None, :] // SGU_SUBCHUNK) <= (pos[:, None] // SGU_SUBCHUNK)
    w_mask = jnp.where(keep[None], w_spatial, 0.0).astype(BF16)
    w_mask_t = jnp.swapaxes(w_mask, 1, 2)
    bias_full = jnp.repeat(b_spatial.T, GROUP_DIM, axis=1)
    ln_g = sgu_ln_g.reshape(1, D_BRANCH)
    ln_b = sgu_ln_b.reshape(1, D_BRANCH)
    final_g = final_norm_g.reshape(1, D_MODEL)

    proj, ht = _in_proj(x, norm_g, w_in)
    o, ya, rsave = _attn_fwd(proj, blk, ATTN_PAIRS)
    yb = _sgu_fwd(proj, ln_g, ln_b, w_mask, bias_full)
    dzg, do, dyb, dx2, g_out, g_up_a, g_up_b, loss_acc, d_final = _mid(
        proj, ya, yb, o, x, target, final_g, w_up_a, w_up_b, w_out)
    dsgu, d_wsp, d_bsp, d_lng, d_lnb = _sgu_bwd(proj, dyb, ln_g, ln_b, w_mask, w_mask_t, bias_full)
    dq, dk, dv = _attn_bwd(proj, do, rsave, blk, ATTN_PAIRS)

    g_in = _dwin_piece(ht, dzg, lambda j: jnp.where(j == 0, COL_ZA, COL_GA - 1 + j), None)
    g_in = _dwin_piece(ht, dsgu, lambda j: COL_UB + j, g_in)
    g_in = _dwin_piece(ht, dq, lambda j: COL_Q + j, g_in)
    g_in = _dwin_piece(ht, dk, lambda j: COL_K + j, g_in)
    g_in = _dwin_piece(ht, dv, lambda j: COL_V + j, g_in)
    pieces = [(dq, COL_Q * D_BRANCH, 0, D_BRANCH), (dk, COL_K * D_BRANCH, 0, D_BRANCH),
              (dv, COL_V * D_BRANCH, 0, D_BRANCH), (dzg, COL_ZA * D_BRANCH, 0, D_BRANCH),
              (dsgu, COL_UB * D_BRANCH, 0, 3 * D_BRANCH), (dzg, COL_GA * D_BRANCH, D_BRANCH, 2 * D_MODEL)]
    dx, d_norm = _dh_dx(pieces, w_in, x, norm_g, dx2)
    small = {"norm_g": d_norm, "sgu_ln_g": d_lng, "sgu_ln_b": d_lnb, "w_spatial": d_wsp,
             "b_spatial": d_bsp[:, :N_GROUPS].T, "final_norm_g": d_final}
    return loss_acc[0, 0], dx, g_in, g_up_a, g_up_b, g_out, small


def kernel(x, norm_g, w_in, sgu_ln_g, sgu_ln_b, w_spatial, b_spatial, w_up_a, w_up_b, w_out, final_norm_g, loss_target, m_norm_g, m_w_in, m_sgu_ln_g, m_sgu_ln_b, m_w_spatial, m_b_spatial, m_w_up_a, m_w_up_b, m_w_out, m_final_norm_g, v_norm_g, v_w_in, v_sgu_ln_g, v_sgu_ln_b, v_w_spatial, v_b_spatial, v_w_up_a, v_w_up_b, v_w_out, v_final_norm_g):
    big_names = ("w_in", "w_up_a", "w_up_b", "w_out")
    small_names = tuple(n for n, _ in _SMALL_PARTS)
    names = ("norm_g", "w_in", "sgu_ln_g", "sgu_ln_b", "w_spatial", "b_spatial", "w_up_a", "w_up_b", "w_out",
             "final_norm_g")
    w = dict(norm_g=norm_g, w_in=w_in, sgu_ln_g=sgu_ln_g, sgu_ln_b=sgu_ln_b, w_spatial=w_spatial,
             b_spatial=b_spatial, w_up_a=w_up_a, w_up_b=w_up_b, w_out=w_out, final_norm_g=final_norm_g)
    m = dict(norm_g=m_norm_g, w_in=m_w_in, sgu_ln_g=m_sgu_ln_g, sgu_ln_b=m_sgu_ln_b, w_spatial=m_w_spatial,
             b_spatial=m_b_spatial, w_up_a=m_w_up_a, w_up_b=m_w_up_b, w_out=m_w_out, final_norm_g=m_final_norm_g)
    v = dict(norm_g=v_norm_g, w_in=v_w_in, sgu_ln_g=v_sgu_ln_g, sgu_ln_b=v_sgu_ln_b, w_spatial=v_w_spatial,
             b_spatial=v_b_spatial, w_up_a=v_w_up_a, w_up_b=v_w_up_b, w_out=v_w_out, final_norm_g=v_final_norm_g)
    shapes = {n: w[n].shape for n in names}
    flat2d = lambda a: a.reshape(a.shape[-2:])

    full = _gather_weights(*[flat2d(w[n]) for n in big_names])
    loss, dx, g_in, g_up_a, g_up_b, g_out, small = _local_step(
        x[0], loss_target[0], norm_g, full[0], sgu_ln_g[0], sgu_ln_b[0], w_spatial[0], b_spatial[0],
        full[1], full[2], full[3], final_norm_g, ATTN_BLOCK)
    packed = _pack_small(small).reshape(N_DEV, SMALL_PIECE, LANES)
    red = _reduce_grads(g_in, g_up_a, g_up_b, g_out, packed)

    grads, deltas, new_m, new_v = {}, {}, {}, {}
    for n, g in zip(big_names, red[:4]):
        d, nm, nv = _adamw(flat2d(w[n]), g, flat2d(m[n]), flat2d(v[n]))
        grads[n], deltas[n], new_m[n], new_v[n] = (a.reshape(shapes[n]) for a in (g, d, nm, nv))
    g_small = red[4].reshape(SMALL_ROWS, LANES)
    d, nm, nv = _adamw(_pack_small({n: w[n] for n in small_names}), g_small,
                       _pack_small({n: m[n] for n in small_names}), _pack_small({n: v[n] for n in small_names}))
    for src, dst in ((g_small, grads), (d, deltas), (nm, new_m), (nv, new_v)):
        dst.update(_unpack_small(src, shapes))

    loss = lax.psum(loss, ("x", "y", "c"))
    return (loss, dx[None], *[grads[n] for n in names], *[deltas[n] for n in names],
            *[new_m[n] for n in names], *[new_v[n] for n in names])
```

```python
import functools
import math

import jax
import jax.numpy as jnp
from jax import lax
from jax.experimental import pallas as pl
from jax.experimental.pallas import tpu as pltpu

F32 = jnp.float32
BF16 = jnp.bfloat16

D_MODEL = 1024
N_HEADS = 8
HEAD_DIM = 64
D_BRANCH = 512
D_IN = 4 * D_BRANCH + 3 * D_BRANCH + 2 * D_MODEL
N_GROUPS = 8
GROUP_DIM = 64
SGU_CHUNK = 128
SGU_SUBCHUNK = 64
GROUP_SHIFT = 6
EPS = 1e-6
LANES = 128
ATTN_BLOCK = 256
SCAN_PASSES = 1
ATTN_PAIRS = 2
N_CHIPS = 4
N_DEV = 8
MESH = pl.DeviceIdType.MESH

ADAM_LR = 0.001
ADAM_B1 = 0.9
ADAM_B2 = 0.999
ADAM_EPS = 1e-08
ADAM_WD = 0.01
ADAM_STEP = 10

COL_Q, COL_K, COL_V, COL_ZA, COL_UB, COL_VB, COL_ZB, COL_GA, COL_GB = 0, 1, 2, 3, 4, 5, 6, 7, 9

VMEM_LIMIT = 56 * 1024 * 1024

SMALL_ROWS = 1088
SMALL_PIECE = SMALL_ROWS // N_DEV


def _cparams(sem=None):
    return pltpu.CompilerParams(dimension_semantics=sem, vmem_limit_bytes=VMEM_LIMIT)


def _aligned(v, m):
    return v if isinstance(v, int) else pl.multiple_of(v, m)


def _sigmoid(x):
    return 1.0 / (1.0 + jnp.exp(-x))


def _gelu_and_grad(x):
    k = math.sqrt(2.0 / math.pi)
    x2 = x * x
    inner = k * (x + 0.044715 * x * x2)
    th = jnp.tanh(inner)
    g = 0.5 * x * (1.0 + th)
    dg = 0.5 * (1.0 + th) + 0.5 * x * (1.0 - th * th) * (k * (1.0 + 3.0 * 0.044715 * x2))
    return g, dg


def _split_dot(a, b_bf16, passes):
    out = None
    rem = a
    for _ in range(passes):
        part = rem.astype(BF16)
        d = jnp.dot(part, b_bf16, preferred_element_type=F32)
        out = d if out is None else out + d
        rem = rem - part.astype(F32)
    return out


def _dot_nt(a, b):
    return lax.dot_general(a, b, (((1,), (1,)), ((), ())), preferred_element_type=F32)


def _dot_tn(a, b):
    return lax.dot_general(a, b, (((0,), (0,)), ((), ())), preferred_element_type=F32)


def _place():
    x, y, c = lax.axis_index("x"), lax.axis_index("y"), lax.axis_index("c")
    return x, y, c


def _in_proj_gather(x, norm_g, w_in, w_up_a, w_up_b, w_out):
    s = x.shape[0]
    tm = min(512, s)
    nt = s // tm
    shards = (w_in, w_up_a, w_up_b, w_out)
    n_arr = len(shards)
    col_sharded = (True, True, True, False)
    full_shapes = ((D_MODEL, D_IN), (D_BRANCH, D_MODEL), (D_BRANCH, D_MODEL), (D_MODEL, D_MODEL))
    w_shard = w_in.shape[1]
    half_rows = D_MODEL // 2
    stage_rows = 256

    def body(order_ref, x_ref, g_ref, *refs):
        src = refs[:n_arr]
        proj_ref, ht_ref = refs[n_arr:n_arr + 2]
        out = refs[n_arr + 2:2 * n_arr + 2]
        wsc, h_scr, stage = refs[2 * n_arr + 2:2 * n_arr + 5]
        small_stage = refs[2 * n_arr + 5:2 * n_arr + 8]
        small_cast = refs[2 * n_arr + 8:2 * n_arr + 11]
        send_sems, recv_sems, local_sems = refs[2 * n_arr + 11:]
        k = pl.program_id(0)
        i = pl.program_id(1)
        x_, y_, c = _place()
        chip = 2 * x_ + y_
        sibling = (x_, y_, 1 - c)
        others = [(x_, 1 - y_), (1 - x_, y_), (1 - x_, 1 - y_)]

        def region(a, chip_idx, half):
            if a == 0:
                return wsc.at[chip_idx, pl.ds(_aligned(half * half_rows, 16), half_rows), :]
            r, w = shards[a].shape
            hr = r // 2
            if col_sharded[a]:
                return out[a].at[pl.ds(_aligned(half * hr, 16), hr), pl.ds(_aligned(chip_idx * w, LANES), w)]
            return out[a].at[pl.ds(_aligned(chip_idx * r + half * hr, 16), hr), :]

        def remote(kk, a, chip_idx, half, to, own):
            s_ref = region(a, chip_idx, half)
            if own and a > 0:
                hr = shards[a].shape[0] // 2
                s_ref = small_cast[a - 1].at[pl.ds(_aligned(half * hr, 16), hr), :]
            return pltpu.make_async_remote_copy(src_ref=s_ref, dst_ref=region(a, chip_idx, half),
                                                send_sem=send_sems.at[kk], recv_sem=recv_sems.at[kk],
                                                device_id=to, device_id_type=MESH)

        def keep_whole(kk, chip_idx):
            return pltpu.make_async_copy(wsc.at[chip_idx],
                                         out[0].at[:, pl.ds(_aligned(chip_idx * w_shard, LANES), w_shard)],
                                         local_sems.at[kk])

        def small_stores():
            cps = []
            for a in range(1, n_arr):
                hr = shards[a].shape[0] // 2
                for half in range(2):
                    cps.append(pltpu.make_async_copy(small_cast[a - 1].at[pl.ds(half * hr, hr), :],
                                                     region(a, chip, half), local_sems.at[4 + 2 * (a - 1) + half]))
            return cps

        def arrive_and_pass(j):
            ochip = chip ^ j
            for a in range(n_arr):
                kk = n_arr * (j - 1) + a
                remote(kk, a, ochip, c, sibling, False).wait_recv()
                remote(3 * n_arr + kk, a, ochip, c, sibling, False).start()

        def from_sibling(j, a):
            remote(3 * n_arr + n_arr * (j - 1) + a, a, chip ^ j, 1 - c, sibling, False).wait_recv()

        @pl.when((k == 0) & (i == 0))
        def _():
            for r0 in range(0, D_MODEL, stage_rows):
                pltpu.sync_copy(src[0].at[pl.ds(r0, stage_rows), :], stage)
                wsc[chip, pl.ds(r0, stage_rows), :] = stage[...].astype(BF16)
            for a in range(1, n_arr):
                pltpu.sync_copy(src[a], small_stage[a - 1])
                small_cast[a - 1][...] = small_stage[a - 1][...].astype(BF16)
            for j in (1, 2, 3):
                for a in range(n_arr):
                    remote(n_arr * (j - 1) + a, a, chip, c, (*others[j - 1], c), True).start()
            keep_whole(0, chip).start()
            for cp in small_stores():
                cp.start()

        @pl.when((k == 1) & (i == 0))
        def _():
            arrive_and_pass(1)
            arrive_and_pass(2)
            from_sibling(1, 0)
            keep_whole(1, chip ^ 1).start()

        @pl.when((k == 2) & (i == 0))
        def _():
            from_sibling(2, 0)
            keep_whole(2, chip ^ 2).start()
            arrive_and_pass(3)

        @pl.when((k == 3) & (i == 0))
        def _():
            from_sibling(3, 0)
            keep_whole(3, chip ^ 3).start()

        @pl.when(k == 0)
        def _():
            xf = x_ref[...]
            r = lax.rsqrt(jnp.mean(xf * xf, axis=-1, keepdims=True) + EPS)
            h = xf * r * g_ref[...]
            h_scr[i] = h.astype(BF16)
            ht_ref[...] = h.T.astype(BF16)

        proj_ref[...] = jnp.dot(h_scr[i], wsc[order_ref[k]], preferred_element_type=F32).astype(BF16)

        @pl.when((k == 3) & (i == nt - 1))
        def _():
            for j in (1, 2, 3):
                for a in range(1, n_arr):
                    from_sibling(j, a)
            for j in (1, 2, 3):
                for a in range(n_arr):
                    kk = n_arr * (j - 1) + a
                    remote(kk, a, chip, c, (*others[j - 1], c), True).wait_send()
                    remote(3 * n_arr + kk, a, chip ^ j, c, sibling, False).wait_send()
            for kk in range(4):
                keep_whole(kk, chip ^ kk).wait()
            for cp in small_stores():
                cp.wait()

    any_spec = pl.BlockSpec(memory_space=pl.ANY)
    tile = lambda kk, ii: jnp.where(kk == 0, ii, nt - 1)
    grid_spec = pltpu.PrefetchScalarGridSpec(
        num_scalar_prefetch=1, grid=(N_CHIPS, nt),
        in_specs=[pl.BlockSpec((tm, D_MODEL), lambda kk, ii, order: (tile(kk, ii), 0)),
                  pl.BlockSpec((1, D_MODEL), lambda kk, ii, order: (0, 0))] + [any_spec] * n_arr,
        out_specs=[pl.BlockSpec((tm, w_shard), lambda kk, ii, order: (ii, order[kk])),
                   pl.BlockSpec((D_MODEL, tm), lambda kk, ii, order: (0, tile(kk, ii)))] + [any_spec] * n_arr,
        scratch_shapes=[pltpu.VMEM((N_CHIPS, D_MODEL, w_shard), BF16), pltpu.VMEM((nt, tm, D_MODEL), BF16),
                        pltpu.VMEM((stage_rows, w_shard), F32)]
        + [pltpu.VMEM(a.shape, F32) for a in shards[1:]] + [pltpu.VMEM(a.shape, BF16) for a in shards[1:]]
        + [pltpu.SemaphoreType.DMA((6 * n_arr,)), pltpu.SemaphoreType.DMA((6 * n_arr,)),
           pltpu.SemaphoreType.DMA((4 + 2 * (n_arr - 1),))])
    x_, y_, _ = _place()
    order = (2 * x_ + y_) ^ jnp.arange(N_CHIPS, dtype=jnp.int32)
    return pl.pallas_call(
        body, name="in_proj_gather", grid_spec=grid_spec,
        out_shape=[jax.ShapeDtypeStruct((s, D_IN), BF16), jax.ShapeDtypeStruct((D_MODEL, s), BF16)]
        + [jax.ShapeDtypeStruct(sh, BF16) for sh in full_shapes],
        compiler_params=pltpu.CompilerParams(dimension_semantics=("arbitrary", "arbitrary"),
                                             vmem_limit_bytes=VMEM_LIMIT, has_side_effects=True),
    )(order, x, norm_g, *shards)


def _neg_softplus_parts(z):
    p = jnp.exp(-jnp.abs(z))
    return p, jnp.maximum(z, 0.0) + jnp.log(1.0 + p)


def _split_cat(a, passes):
    parts = []
    rem = a
    for k in range(passes):
        part = rem.astype(BF16)
        parts.append(part)
        if k + 1 < passes:
            rem = rem - part.astype(F32)
    return parts[0] if passes == 1 else jnp.concatenate(parts, axis=1)


def _tri(blk, upper, sign):
    row = lax.broadcasted_iota(jnp.int32, (blk, blk), 0)
    col = lax.broadcasted_iota(jnp.int32, (blk, blk), 1)
    keep = (row <= col) if upper else (row >= col)
    t = jnp.where(keep, sign, 0.0).astype(BF16)
    return t if SCAN_PASSES == 1 else jnp.concatenate([t] * SCAN_PASSES, axis=0)


def _attn_fwd(proj, blk, npairs):
    s = proj.shape[0]
    nq = s // blk
    scale = HEAD_DIM ** -0.5
    heads = tuple(range(2 * npairs))
    width = LANES * npairs

    def body(q_ref, k_ref, v_ref, za_ref, o_ref, ya_ref, rs_ref, acc_ref, r_ref, z_ref):
        i = pl.program_id(1)
        lane = lax.broadcasted_iota(jnp.int32, (blk, LANES), 1)
        lo_half = lane < HEAD_DIM
        qm = []
        for pr in range(npairs):
            q = q_ref[:, LANES * pr:LANES * (pr + 1)] * jnp.asarray(scale, BF16)
            zero = jnp.zeros_like(q)
            qm += [jnp.where(lo_half, q, zero), jnp.where(lo_half, zero, q)]
        row = lax.broadcasted_iota(jnp.int32, (blk, blk), 0)
        col = lax.broadcasted_iota(jnp.int32, (blk, blk), 1)
        before = col < row
        tneg = _tri(blk, False, -1.0)
        acc_ref[...] = jnp.zeros_like(acc_ref)
        r_ref[...] = jnp.zeros_like(r_ref)
        rs_ref[...] = jnp.zeros_like(rs_ref)

        def scores(j):
            ks = pl.multiple_of(j * blk, blk)
            return [_dot_nt(qm[h], k_ref[pl.ds(ks, blk), LANES * (h // 2):LANES * (h // 2 + 1)]) for h in heads]

        for h, zh in enumerate(scores(i)):
            z_ref[h] = zh

        def block(j, diag):
            ks = pl.multiple_of(j * blk, blk)
            vj = [v_ref[pl.ds(ks, blk), LANES * pr:LANES * (pr + 1)] for pr in range(npairs)]
            z = [z_ref[h] for h in heads]
            sp = [_neg_softplus_parts(z[h])[1] for h in heads]
            if diag:
                sp = [jnp.where(before, sp[h], 0.0) for h in heads]
            cin = [jnp.dot(_split_cat(sp[h], SCAN_PASSES), tneg, preferred_element_type=F32) for h in heads]
            for h, zh in enumerate(scores(jnp.maximum(j - 1, 0))):
                z_ref[h] = zh
            w = [jnp.exp(z[h] + cin[h]) for h in heads]
            if diag:
                w = [jnp.where(before, w[h], 0.0) for h in heads]
            pv = [jnp.dot(w[h].astype(BF16), vj[h // 2], preferred_element_type=F32) for h in heads]
            r = [r_ref[h] for h in heads]
            for h in heads:
                acc_ref[h] += pv[h] * jnp.exp(r[h])
                r_ref[h] = r[h] + cin[h][:, 0:1]
            for pr in range(npairs):
                rs_ref[pr] = jnp.where(lane == j, r[2 * pr], jnp.where(lane == j + HEAD_DIM, r[2 * pr + 1], rs_ref[pr]))

        block(i, True)

        def loop_body(jj, carry):
            block(i - 1 - jj, False)
            return carry

        lax.fori_loop(0, i, loop_body, 0)
        for pr in range(npairs):
            cols = slice(LANES * pr, LANES * (pr + 1))
            o = jnp.where(lo_half, acc_ref[2 * pr], acc_ref[2 * pr + 1])
            o_ref[:, cols] = o.astype(BF16)
            za = za_ref[:, cols].astype(F32)
            ya_ref[:, cols] = (o * (za * _sigmoid(za))).astype(BF16)

    n_steps = N_HEADS // (2 * npairs)
    return pl.pallas_call(
        body, name="attn_fwd", grid=(n_steps, nq),
        in_specs=[pl.BlockSpec((blk, width), lambda p, i: (i, n_steps * COL_Q + p)),
                  pl.BlockSpec((s, width), lambda p, i: (0, n_steps * COL_K + p)),
                  pl.BlockSpec((s, width), lambda p, i: (0, n_steps * COL_V + p)),
                  pl.BlockSpec((blk, width), lambda p, i: (i, n_steps * COL_ZA + p))],
        out_specs=[pl.BlockSpec((blk, width), lambda p, i: (i, p)),
                   pl.BlockSpec((blk, width), lambda p, i: (i, p)),
                   pl.BlockSpec((npairs, blk, LANES), lambda p, i: (p, i, 0))],
        out_shape=[jax.ShapeDtypeStruct((s, D_BRANCH), BF16), jax.ShapeDtypeStruct((s, D_BRANCH), BF16),
                   jax.ShapeDtypeStruct((N_HEADS // 2, s, LANES), F32)],
        scratch_shapes=[pltpu.VMEM((2 * npairs, blk, LANES), F32), pltpu.VMEM((2 * npairs, blk, 1), F32),
                        pltpu.VMEM((2 * npairs, blk, blk), F32)],
        compiler_params=_cparams(("parallel", "parallel")),
    )(proj, proj, proj, proj)


_GRAD_COL_SHARDED = (True, True, True, False)
_GRAD_FULL_SHAPES = ((D_MODEL, D_IN), (D_BRANCH, D_MODEL), (D_BRANCH, D_MODEL), (D_MODEL, D_MODEL))
_GRAD_PIECE_SHAPES = tuple((r // 2, w // N_CHIPS) if cs else (r // (2 * N_CHIPS), w)
                           for (r, w), cs in zip(_GRAD_FULL_SHAPES, _GRAD_COL_SHARDED))
_EARLY_IN_DEVS = (4, 5, 6, 7)


def _grad_piece(ref, a, dev):
    r, w = _GRAD_PIECE_SHAPES[a]
    if _GRAD_COL_SHARDED[a]:
        return ref.at[pl.ds((dev % 2) * r, r), pl.ds((dev // 2) * w, w)]
    return ref.at[pl.ds(dev * r, r), :]


def _dev_id(dev):
    return (dev // 4, (dev // 2) % 2, dev % 2)


def _early_arrays(dev):
    return (0, 1, 2, 3) if dev in _EARLY_IN_DEVS else (1, 2, 3)


def _attn_bwd(proj, do, rsave, blk, npairs, grads):
    s = proj.shape[0]
    nq = s // blk
    scale = HEAD_DIM ** -0.5
    heads = tuple(range(2 * npairs))
    width = LANES * npairs

    n_steps = N_HEADS // (2 * npairs)
    n_g = len(grads)

    def body(q_ref, k_ref, v_ref, do_ref, rs_ref, *refs):
        g_src = refs[:n_g]
        dq_ref, dk_ref, dv_ref = refs[n_g:n_g + 3]
        g_slots = refs[n_g + 3:2 * n_g + 3]
        dk_acc, dv_acc, dq_acc, e_ref, send_sems, recv_sems = refs[2 * n_g + 3:]
        i = pl.program_id(1)
        x_, y_, c_ = lax.axis_index("x"), lax.axis_index("y"), lax.axis_index("c")
        me = 4 * x_ + 2 * y_ + c_

        def early(a, dst_dev, src_dev):
            return pltpu.make_async_remote_copy(
                src_ref=_grad_piece(g_src[a], a, dst_dev), dst_ref=g_slots[a].at[src_dev],
                send_sem=send_sems.at[n_g * dst_dev + a], recv_sem=recv_sems.at[n_g * src_dev + a],
                device_id=_dev_id(dst_dev), device_id_type=MESH)

        @pl.when((pl.program_id(0) == 0) & (i == 0))
        def _():
            for dev in range(N_DEV):
                @pl.when(me != dev)
                def _():
                    for a in _early_arrays(dev):
                        early(a, dev, me).start()

        lane = lax.broadcasted_iota(jnp.int32, (blk, LANES), 1)
        lo_half = lane < HEAD_DIM
        qm, dom = [], []
        for pr in range(npairs):
            cols = slice(LANES * pr, LANES * (pr + 1))
            q = q_ref[:, cols] * jnp.asarray(scale, BF16)
            zero = jnp.zeros_like(q)
            qm += [jnp.where(lo_half, q, zero), jnp.where(lo_half, zero, q)]
            dout = do_ref[:, cols].astype(F32)
            dom += [jnp.where(lo_half, dout, 0.0), jnp.where(lo_half, 0.0, dout)]
        row = lax.broadcasted_iota(jnp.int32, (blk, blk), 0)
        col = lax.broadcasted_iota(jnp.int32, (blk, blk), 1)
        before = col < row
        tneg = _tri(blk, False, -1.0)
        tfwd = _tri(blk, True, 1.0)

        @pl.when(i == 0)
        def _():
            dk_acc[...] = jnp.zeros_like(dk_acc)
            dv_acc[...] = jnp.zeros_like(dv_acc)

        dq_acc[...] = jnp.zeros_like(dq_acc)
        e_ref[...] = jnp.zeros_like(e_ref)

        def block(j, diag):
            ks = pl.multiple_of(j * blk, blk)
            kj = [k_ref[pl.ds(ks, blk), LANES * pr:LANES * (pr + 1)] for pr in range(npairs)]
            vj = [v_ref[pl.ds(ks, blk), LANES * pr:LANES * (pr + 1)] for pr in range(npairs)]
            z = [_dot_nt(qm[h], kj[h // 2]) for h in heads]
            er = [jnp.exp(jnp.sum(jnp.where(lane == j + HEAD_DIM * (h % 2), rs_ref[h // 2], 0.0), axis=-1,
                                  keepdims=True)) for h in heads]
            dos = [(dom[h] * er[h]).astype(BF16) for h in heads]
            dw = [_dot_nt(dos[h], vj[h // 2]) for h in heads]
            psp = [_neg_softplus_parts(z[h]) for h in heads]
            sp = [psp[h][1] for h in heads]
            if diag:
                sp = [jnp.where(before, sp[h], 0.0) for h in heads]
            cin = [jnp.dot(_split_cat(sp[h], SCAN_PASSES), tneg, preferred_element_type=F32) for h in heads]
            w = [jnp.exp(z[h] + cin[h]) for h in heads]
            if diag:
                w = [jnp.where(before, w[h], 0.0) for h in heads]
            e =[dw[h] * w[h] for h in heads]
            eincl = [jnp.dot(_split_cat(e[h], SCAN_PASSES), tfwd, preferred_element_type=F32) + e_ref[h]
                     for h in heads]
            dz = []
            for h in heads:
                p = psp[h][0]
                beta = jnp.where(z[h] >= 0.0, 1.0, p) / (1.0 + p)
                d = e[h] - beta * eincl[h]
                dz.append((jnp.where(before, d, 0.0) if diag else d).astype(BF16))
            wb = [w[h].astype(BF16) for h in heads]
            for h in heads:
                e_ref[h] = eincl[h][:, blk - 1:blk]
                dq_acc[h] += jnp.dot(dz[h], kj[h // 2], preferred_element_type=F32)
            for pr in range(npairs):
                cols = slice(LANES * pr, LANES * (pr + 1))
                h0, h1 = 2 * pr, 2 * pr + 1
                dk_acc[pl.ds(ks, blk), cols] += _dot_tn(dz[h0], qm[h0]) + _dot_tn(dz[h1], qm[h1])
                dv_acc[pl.ds(ks, blk), cols] += _dot_tn(wb[h0], dos[h0]) + _dot_tn(wb[h1], dos[h1])

        def loop_body(j, carry):
            block(j, False)
            return carry

        lax.fori_loop(0, i, loop_body, 0)
        block(i, True)
        for pr in range(npairs):
            dq = jnp.where(lo_half, dq_acc[2 * pr], dq_acc[2 * pr + 1]) * scale
            dq_ref[:, LANES * pr:LANES * (pr + 1)] = dq.astype(BF16)

        @pl.when(i == nq - 1)
        def _():
            dk_ref[...] = dk_acc[...].astype(BF16)
            dv_ref[...] = dv_acc[...].astype(BF16)

        @pl.when((pl.program_id(0) == n_steps - 1) & (i == nq - 1))
        def _():
            for dev in range(N_DEV):
                @pl.when(me != dev)
                def _():
                    for a in (1, 2, 3):
                        early(a, dev, dev).wait_recv()
                    for a in _early_arrays(dev):
                        early(a, dev, me).wait_send()

                @pl.when((me != dev) & (me >= _EARLY_IN_DEVS[0]))
                def _():
                    early(0, dev, dev).wait_recv()

    any_spec = pl.BlockSpec(memory_space=pl.ANY)
    return pl.pallas_call(
        body, name="attn_bwd", grid=(n_steps, nq),
        in_specs=[pl.BlockSpec((blk, width), lambda p, i: (i, n_steps * COL_Q + p)),
                  pl.BlockSpec((s, width), lambda p, i: (0, n_steps * COL_K + p)),
                  pl.BlockSpec((s, width), lambda p, i: (0, n_steps * COL_V + p)),
                  pl.BlockSpec((blk, width), lambda p, i: (i, p)),
                  pl.BlockSpec((npairs, blk, LANES), lambda p, i: (p, i, 0))] + [any_spec] * n_g,
        out_specs=[pl.BlockSpec((blk, width), lambda p, i: (i, p)),
                   pl.BlockSpec((s, width), lambda p, i: (0, p)),
                   pl.BlockSpec((s, width), lambda p, i: (0, p))] + [any_spec] * n_g,
        out_shape=[jax.ShapeDtypeStruct((s, D_BRANCH), BF16)] * 3
        + [jax.ShapeDtypeStruct((N_DEV,) + sh, BF16) for sh in _GRAD_PIECE_SHAPES],
        scratch_shapes=[pltpu.VMEM((s, width), F32), pltpu.VMEM((s, width), F32),
                        pltpu.VMEM((2 * npairs, blk, LANES), F32), pltpu.VMEM((2 * npairs, blk, 1), F32),
                        pltpu.SemaphoreType.DMA((N_DEV * n_g,)), pltpu.SemaphoreType.DMA((N_DEV * n_g,))],
        compiler_params=pltpu.CompilerParams(dimension_semantics=("arbitrary", "arbitrary"),
                                             vmem_limit_bytes=VMEM_LIMIT, has_side_effects=True),
    )(proj, proj, proj, do, rsave, *grads)


def _group_avg_matrix():
    a = lax.broadcasted_iota(jnp.int32, (LANES, LANES), 0) >> GROUP_SHIFT
    b = lax.broadcasted_iota(jnp.int32, (LANES, LANES), 1) >> GROUP_SHIFT
    return jnp.where(a == b, 1.0 / GROUP_DIM, 0.0).astype(BF16)


def _group_mean(a, avg):
    parts = [_split_dot(a[:, LANES * k:LANES * (k + 1)], avg, 3) for k in range(D_BRANCH // LANES)]
    return jnp.concatenate(parts, axis=1)


def _sgu_forward_parts(ub, vb, ln_g, ln_b, avg):
    ug, dug = _gelu_and_grad(ub)
    vg, dvg = _gelu_and_grad(vb)
    mu = _group_mean(vg, avg)
    d = vg - mu
    var = _group_mean(d * d, avg)
    rstd = lax.rsqrt(var + EPS)
    vhat = d * rstd
    vn = vhat * ln_g + ln_b
    return ug, dug, dvg, rstd, vhat, vn


def _sgu_mix(w_ref, src_bf16, n_chunks):
    lane = lax.broadcasted_iota(jnp.int32, (SGU_CHUNK, LANES), 1)
    lo_half = lane < GROUP_DIM
    rows = []
    for n in range(n_chunks):
        slabs = []
        for a in range(D_BRANCH // LANES):
            blk = src_bf16[SGU_CHUNK * n:SGU_CHUNK * (n + 1), LANES * a:LANES * (a + 1)]
            zero = jnp.zeros_like(blk)
            m0 = jnp.dot(w_ref[2 * a], jnp.where(lo_half, blk, zero), preferred_element_type=F32)
            m1 = jnp.dot(w_ref[2 * a + 1], jnp.where(lo_half, zero, blk), preferred_element_type=F32)
            slabs.append(m0 + m1)
        rows.append(jnp.concatenate(slabs, axis=1))
    return jnp.concatenate(rows, axis=0)


def _sgu_fwd(proj, ln_g, ln_b, w_mask, bias_full):
    s = proj.shape[0]
    tm = min(512, s)
    n_chunks = tm // SGU_CHUNK

    def body(ub_ref, vb_ref, zb_ref, g_ref, b_ref, w_ref, bias_ref, yb_ref):
        avg = _group_avg_matrix()
        ug, _, _, _, _, vn = _sgu_forward_parts(ub_ref[...].astype(F32), vb_ref[...].astype(F32),
                                                g_ref[...], b_ref[...], avg)
        mixed = _sgu_mix(w_ref, vn.astype(BF16), n_chunks) + jnp.concatenate([bias_ref[...]] * n_chunks, axis=0)
        zb = zb_ref[...].astype(F32)
        yb_ref[...] = (ug * mixed * (zb * _sigmoid(zb))).astype(BF16)

    col = lambda c: pl.BlockSpec((tm, D_BRANCH), lambda i: (i, c))
    full = lambda shape: pl.BlockSpec(shape, lambda i: (0,) * len(shape))
    return pl.pallas_call(
        body, name="sgu_fwd", grid=(s // tm,),
        in_specs=[col(COL_UB), col(COL_VB), col(COL_ZB), full((1, D_BRANCH)), full((1, D_BRANCH)),
                  full((N_GROUPS, SGU_CHUNK, SGU_CHUNK)), full((SGU_CHUNK, D_BRANCH))],
        out_specs=pl.BlockSpec((tm, D_BRANCH), lambda i: (i, 0)),
        out_shape=jax.ShapeDtypeStruct((s, D_BRANCH), BF16),
        compiler_params=_cparams(("parallel",)),
    )(proj, proj, proj, ln_g, ln_b, w_mask, bias_full)


def _sgu_bwd(proj, dyb, ln_g, ln_b, w_mask, w_mask_t, bias_full):
    s = proj.shape[0]
    tm = min(512, s)
    n_chunks = tm // SGU_CHUNK
    n_steps = s // tm

    def body(ub_ref, vb_ref, zb_ref, dyb_ref, g_ref, b_ref, w_ref, wt_ref, bias_ref,
             dsgu_ref, dw_ref, db_ref, dg_ref, dbeta_ref, dmix_acc):
        i = pl.program_id(0)

        @pl.when(i == 0)
        def _():
            dw_ref[...] = jnp.zeros_like(dw_ref)
            dg_ref[...] = jnp.zeros_like(dg_ref)
            dbeta_ref[...] = jnp.zeros_like(dbeta_ref)
            dmix_acc[...] = jnp.zeros_like(dmix_acc)

        avg = _group_avg_matrix()
        ln_gv = g_ref[...]
        ug, dug, dvg, rstd, vhat, vn = _sgu_forward_parts(ub_ref[...].astype(F32), vb_ref[...].astype(F32),
                                                          ln_gv, b_ref[...], avg)
        vnb = vn.astype(BF16)
        mixed = _sgu_mix(w_ref, vnb, n_chunks) + jnp.concatenate([bias_ref[...]] * n_chunks, axis=0)
        zb = zb_ref[...].astype(F32)
        sg = _sigmoid(zb)
        sz = zb * sg
        dsz = sg * (1.0 + zb * (1.0 - sg))
        dy = dyb_ref[...].astype(F32)
        dmixed = dy * ug * sz
        du = dy * mixed * sz * dug
        dzb = dy * ug * mixed * dsz
        dmb = dmixed.astype(BF16)
        dvn = _sgu_mix(wt_ref, dmb, n_chunks)

        lane = lax.broadcasted_iota(jnp.int32, (SGU_CHUNK, LANES), 1)
        lo_half = lane < GROUP_DIM
        dm_sum = None
        for n in range(n_chunks):
            rows = slice(SGU_CHUNK * n, SGU_CHUNK * (n + 1))
            dm_sum = dmixed[rows] if dm_sum is None else dm_sum + dmixed[rows]
            for a in range(D_BRANCH // LANES):
                cols = slice(LANES * a, LANES * (a + 1))
                dblk = dmb[rows, cols]
                vblk = vnb[rows, cols]
                zero = jnp.zeros_like(dblk)
                dw_ref[2 * a] += _dot_nt(jnp.where(lo_half, dblk, zero), vblk)
                dw_ref[2 * a + 1] += _dot_nt(jnp.where(lo_half, zero, dblk), vblk)
        dmix_acc[...] += dm_sum

        dg_ref[...] += jnp.sum(dvn * vhat, axis=0, keepdims=True)
        dbeta_ref[...] += jnp.sum(dvn, axis=0, keepdims=True)
        dvh = dvn * ln_gv
        m1 = _group_mean(dvh, avg)
        m2 = _group_mean(dvh * vhat, avg)
        dv = rstd * (dvh - m1 - vhat * m2) * dvg
        dsgu_ref[:, 0:D_BRANCH] = du.astype(BF16)
        dsgu_ref[:, D_BRANCH:2 * D_BRANCH] = dv.astype(BF16)
        dsgu_ref[:, 2 * D_BRANCH:3 * D_BRANCH] = dzb.astype(BF16)

        @pl.when(i == n_steps - 1)
        def _():
            pos = lax.broadcasted_iota(jnp.int32, (SGU_CHUNK, SGU_CHUNK), 0) >> GROUP_SHIFT
            src = lax.broadcasted_iota(jnp.int32, (SGU_CHUNK, SGU_CHUNK), 1) >> GROUP_SHIFT
            keep = src <= pos
            for g in range(N_GROUPS):
                dw_ref[g] = jnp.where(keep, dw_ref[g], 0.0)
            grp = lax.broadcasted_iota(jnp.int32, (D_BRANCH, LANES), 0) >> GROUP_SHIFT
            sel = (grp == lax.broadcasted_iota(jnp.int32, (D_BRANCH, LANES), 1)).astype(BF16)
            db_ref[...] = _split_dot(dmix_acc[...], sel, 3)

    col = lambda c: pl.BlockSpec((tm, D_BRANCH), lambda i: (i, c))
    full = lambda shape: pl.BlockSpec(shape, lambda i: (0,) * len(shape))
    return pl.pallas_call(
        body, name="sgu_bwd", grid=(n_steps,),
        in_specs=[col(COL_UB), col(COL_VB), col(COL_ZB), pl.BlockSpec((tm, D_BRANCH), lambda i: (i, 0)),
                  full((1, D_BRANCH)), full((1, D_BRANCH)),
                  full((N_GROUPS, SGU_CHUNK, SGU_CHUNK)), full((N_GROUPS, SGU_CHUNK, SGU_CHUNK)),
                  full((SGU_CHUNK, D_BRANCH))],
        out_specs=[pl.BlockSpec((tm, 3 * D_BRANCH), lambda i: (i, 0)),
                   full((N_GROUPS, SGU_CHUNK, SGU_CHUNK)), full((SGU_CHUNK, LANES)),
                   full((1, D_BRANCH)), full((1, D_BRANCH))],
        out_shape=[jax.ShapeDtypeStruct((s, 3 * D_BRANCH), BF16),
                   jax.ShapeDtypeStruct((N_GROUPS, SGU_CHUNK, SGU_CHUNK), F32),
                   jax.ShapeDtypeStruct((SGU_CHUNK, LANES), F32),
                   jax.ShapeDtypeStruct((1, D_BRANCH), F32), jax.ShapeDtypeStruct((1, D_BRANCH), F32)],
        scratch_shapes=[pltpu.VMEM((SGU_CHUNK, D_BRANCH), F32)],
        compiler_params=_cparams(("arbitrary",)),
    )(proj, proj, proj, dyb, ln_g, ln_b, w_mask, w_mask_t, bias_full)


def _mid(proj, ya, yb, o, x, target, final_g, w_up_a, w_up_b, w_out):
    s = x.shape[0]
    tm = min(256, s)
    n_steps = s // tm
    half = D_MODEL // 2

    def body(ya_ref, yb_ref, o_ref, za_ref, ga0_ref, ga1_ref, gb0_ref, gb1_ref, x_ref, t_ref, gf_ref,
             wa_ref, wb_ref, wo_ref,
             dzg_ref, do_ref, dyb_ref, dx2_ref, gwo_ref, gwa_ref, gwb_ref, loss_ref, dgf_ref,
             acc_o, acc_a, acc_b):
        i = pl.program_id(0)

        @pl.when(i == 0)
        def _():
            acc_o[...] = jnp.zeros_like(acc_o)
            acc_a[...] = jnp.zeros_like(acc_a)
            acc_b[...] = jnp.zeros_like(acc_b)
            loss_ref[...] = jnp.zeros_like(loss_ref)
            dgf_ref[...] = jnp.zeros_like(dgf_ref)

        ya_v = ya_ref[...]
        yb_v = yb_ref[...]
        pa = jnp.dot(ya_v, wa_ref[...], preferred_element_type=F32)
        pb = jnp.dot(yb_v, wb_ref[...], preferred_element_type=F32)
        sa = _sigmoid(jnp.concatenate([ga0_ref[...], ga1_ref[...]], axis=1).astype(F32))
        sb = _sigmoid(jnp.concatenate([gb0_ref[...], gb1_ref[...]], axis=1).astype(F32))
        merged = (sa * pa + sb * pb).astype(BF16)
        x2 = x_ref[...] + jnp.dot(merged, wo_ref[...], preferred_element_type=F32)
        r2 = lax.rsqrt(jnp.mean(x2 * x2, axis=-1, keepdims=True) + EPS)
        xh = x2 * r2
        gf = gf_ref[...]
        diff = xh * gf - t_ref[...]
        loss_ref[...] += 0.5 * jnp.sum(jnp.mean(diff * diff, axis=-1, keepdims=True))
        dy = diff * (1.0 / D_MODEL)
        dgf_ref[...] += jnp.sum(dy * xh, axis=0, keepdims=True)
        dyg = dy * gf
        dx2 = r2 * (dyg - xh * jnp.mean(dyg * xh, axis=-1, keepdims=True))
        dx2_ref[...] = dx2
        dx2b = dx2.astype(BF16)
        dmerged = _dot_nt(dx2b, wo_ref[...])
        acc_o[...] += _dot_tn(merged, dx2b)
        dpa = dmerged * sa
        dpb = dmerged * sb
        dzg_ref[:, D_BRANCH:D_BRANCH + D_MODEL] = (dpa * pa * (1.0 - sa)).astype(BF16)
        dzg_ref[:, D_BRANCH + D_MODEL:D_BRANCH + 2 * D_MODEL] = (dpb * pb * (1.0 - sb)).astype(BF16)
        dpab = dpa.astype(BF16)
        dpbb = dpb.astype(BF16)
        acc_a[...] += _dot_tn(ya_v, dpab)
        acc_b[...] += _dot_tn(yb_v, dpbb)
        dya = _dot_nt(dpab, wa_ref[...])
        dyb_ref[...] = _dot_nt(dpbb, wb_ref[...]).astype(BF16)
        za = za_ref[...].astype(F32)
        sg = _sigmoid(za)
        do_ref[...] = (dya * (za * sg)).astype(BF16)
        dzg_ref[:, 0:D_BRANCH] = (dya * o_ref[...].astype(F32) * (sg * (1.0 + za * (1.0 - sg)))).astype(BF16)

        @pl.when(i == n_steps - 1)
        def _():
            gwo_ref[...] = acc_o[...].astype(BF16)
            gwa_ref[...] = acc_a[...].astype(BF16)
            gwb_ref[...] = acc_b[...].astype(BF16)

    tok = lambda w: pl.BlockSpec((tm, w), lambda i: (i, 0))
    col = lambda c: pl.BlockSpec((tm, half), lambda i: (i, c))
    full = lambda shape: pl.BlockSpec(shape, lambda i: (0,) * len(shape))
    return pl.pallas_call(
        body, name="mid", grid=(n_steps,),
        in_specs=[tok(D_BRANCH), tok(D_BRANCH), tok(D_BRANCH), col(COL_ZA), col(COL_GA), col(COL_GA + 1),
                  col(COL_GB), col(COL_GB + 1), tok(D_MODEL), tok(D_MODEL), full((1, D_MODEL)),
                  full((D_BRANCH, D_MODEL)), full((D_BRANCH, D_MODEL)), full((D_MODEL, D_MODEL))],
        out_specs=[tok(D_BRANCH + 2 * D_MODEL), tok(D_BRANCH), tok(D_BRANCH), tok(D_MODEL),
                   full((D_MODEL, D_MODEL)), full((D_BRANCH, D_MODEL)), full((D_BRANCH, D_MODEL)),
                   full((8, LANES)), full((1, D_MODEL))],
        out_shape=[jax.ShapeDtypeStruct((s, D_BRANCH + 2 * D_MODEL), BF16),
                   jax.ShapeDtypeStruct((s, D_BRANCH), BF16), jax.ShapeDtypeStruct((s, D_BRANCH), BF16),
                   jax.ShapeDtypeStruct((s, D_MODEL), F32),
                   jax.ShapeDtypeStruct((D_MODEL, D_MODEL), BF16),
                   jax.ShapeDtypeStruct((D_BRANCH, D_MODEL), BF16), jax.ShapeDtypeStruct((D_BRANCH, D_MODEL), BF16),
                   jax.ShapeDtypeStruct((8, LANES), F32), jax.ShapeDtypeStruct((1, D_MODEL), F32)],
        scratch_shapes=[pltpu.VMEM((D_MODEL, D_MODEL), F32), pltpu.VMEM((D_BRANCH, D_MODEL), F32),
                        pltpu.VMEM((D_BRANCH, D_MODEL), F32)],
        compiler_params=_cparams(("arbitrary",)),
    )(ya, yb, o, proj, proj, proj, proj, proj, x, target, final_g, w_up_a, w_up_b, w_out)


def _dwin_piece(ht, piece, tile_of, prev):
    s = ht.shape[1]
    n_tiles = piece.shape[1] // D_BRANCH

    def body(ht_ref, p_ref, *rest):
        out_ref = rest[-1]
        out_ref[...] = jnp.dot(ht_ref[...], p_ref[...], preferred_element_type=F32).astype(BF16)

    in_specs = [pl.BlockSpec((D_MODEL, s), lambda j: (0, 0)), pl.BlockSpec((s, D_BRANCH), lambda j: (0, j))]
    args = [ht, piece]
    aliases = {}
    if prev is not None:
        in_specs.append(pl.BlockSpec(memory_space=pl.ANY))
        args.append(prev)
        aliases = {2: 0}
    return pl.pallas_call(
        body, name="dwin_piece", grid=(n_tiles,),
        in_specs=in_specs,
        out_specs=pl.BlockSpec((D_MODEL, D_BRANCH), lambda j: (0, tile_of(j))),
        out_shape=jax.ShapeDtypeStruct((D_MODEL, D_IN), BF16),
        input_output_aliases=aliases,
        compiler_params=_cparams(("parallel",)),
    )(*args)


def _dh_dx(pieces, w_in, x, norm_g, dx2):
    s = x.shape[0]
    tm = min(256, s)
    arrays = []
    for arr, _, _, _ in pieces:
        if not any(arr is a for a in arrays):
            arrays.append(arr)
    n_arr = len(arrays)
    plan = [([k for k, a in enumerate(arrays) if a is arr][0], wcol, off, width) for arr, wcol, off, width in pieces]

    def body(*refs):
        p_refs = refs[:n_arr]
        w_ref, x_ref, g_ref, dx2_ref, dx_ref, dg_ref = refs[n_arr:]

        @pl.when(pl.program_id(0) == 0)
        def _():
            dg_ref[...] = jnp.zeros_like(dg_ref)

        dh = None
        for k, wcol, off, width in plan:
            d = _dot_nt(p_refs[k][:, off:off + width], w_ref[:, wcol:wcol + width])
            dh = d if dh is None else dh + d
        xf = x_ref[...]
        r = lax.rsqrt(jnp.mean(xf * xf, axis=-1, keepdims=True) + EPS)
        xh = xf * r
        dg_ref[...] += jnp.sum(dh * xh, axis=0, keepdims=True)
        dhg = dh * g_ref[...]
        dx_ref[...] = r * (dhg - xh * jnp.mean(dhg * xh, axis=-1, keepdims=True)) + dx2_ref[...]

    tok = lambda w: pl.BlockSpec((tm, w), lambda i: (i, 0))
    full = lambda shape: pl.BlockSpec(shape, lambda i: (0,) * len(shape))
    return pl.pallas_call(
        body, name="dh_dx", grid=(s // tm,),
        in_specs=[tok(a.shape[1]) for a in arrays] + [full((D_MODEL, D_IN)), tok(D_MODEL), full((1, D_MODEL)),
                                                      tok(D_MODEL)],
        out_specs=[tok(D_MODEL), full((1, D_MODEL))],
        out_shape=[jax.ShapeDtypeStruct((s, D_MODEL), F32), jax.ShapeDtypeStruct((1, D_MODEL), F32)],
        compiler_params=_cparams(("arbitrary",)),
    )(*arrays, w_in, x, norm_g, dx2)


def _adamw(w, g, m, v):
    rows, cols = w.shape
    tr = max(t for t in range(8, 257, 8) if rows % t == 0)
    c1 =1.0 - ADAM_B1 ** ADAM_STEP
    c2 = 1.0 - ADAM_B2 ** ADAM_STEP

    def body(w_ref, g_ref, m_ref, v_ref, d_ref, nm_ref, nv_ref):
        gv = g_ref[...]
        nm = ADAM_B1 * m_ref[...] + (1.0 - ADAM_B1) * gv
        nv = ADAM_B2 * v_ref[...] + (1.0 - ADAM_B2) * (gv * gv)
        d_ref[...] = -ADAM_LR * ((nm / c1) / (jnp.sqrt(nv / c2) + ADAM_EPS) + ADAM_WD * w_ref[...])
        nm_ref[...] = nm
        nv_ref[...] = nv

    spec = pl.BlockSpec((tr, cols), lambda i: (i, 0))
    return pl.pallas_call(
        body, name="adamw", grid=(rows // tr,),
        in_specs=[spec] * 4, out_specs=[spec] * 3,
        out_shape=[jax.ShapeDtypeStruct((rows, cols), F32)] * 3,
        compiler_params=_cparams(("parallel",)),
    )(w, g, m, v)


def _place():
    x, y, c = lax.axis_index("x"), lax.axis_index("y"), lax.axis_index("c")
    return x, y, c


def _gather_weights(w_in, w_up_a, w_up_b, w_out):
    shards = (w_in, w_up_a, w_up_b, w_out)
    n_arr = len(shards)
    col_sharded = (True, True, True, False)
    full_shapes = ((D_MODEL, D_IN), (D_BRANCH, D_MODEL), (D_BRANCH, D_MODEL), (D_MODEL, D_MODEL))

    def body(*refs):
        src = refs[:n_arr]
        out = refs[n_arr:2 * n_arr]
        stage = refs[2 * n_arr:3 * n_arr]
        cast = refs[3 * n_arr:4 * n_arr]
        send_sems, recv_sems, local_sems = refs[4 * n_arr:]
        x, y, c = _place()
        chip = 2 * x + y
        sibling = (x, y, 1 - c)
        others = [(1 - x, y), (x, 1 - y), (1 - x, 1 - y)]

        def region(a, chip_idx, half):
            r, w = shards[a].shape
            hr = r // 2
            if col_sharded[a]:
                return out[a].at[pl.ds(_aligned(half * hr, 16), hr), pl.ds(_aligned(chip_idx * w, LANES), w)]
            return out[a].at[pl.ds(_aligned(chip_idx * r + half * hr, 16), hr), :]

        loads = [pltpu.make_async_copy(src[a], stage[a], local_sems.at[a]) for a in range(n_arr)]
        for cp in loads:
            cp.start()
        for a in range(n_arr):
            loads[a].wait()
            cast[a][...] = stage[a][...].astype(BF16)
        stores = []
        for a in range(n_arr):
            hr = shards[a].shape[0] // 2
            for half in range(2):
                cp = pltpu.make_async_copy(cast[a].at[pl.ds(half * hr, hr), :], region(a, chip, half),
                                           local_sems.at[n_arr + 2 * a + half])
                cp.start()
                stores.append(cp)

        def remote(k, a, chip_idx, half, to, from_vmem):
            hr = shards[a].shape[0] // 2
            s_ref = cast[a].at[pl.ds(_aligned(half * hr, 16), hr), :] if from_vmem else region(a, chip_idx, half)
            return pltpu.make_async_remote_copy(src_ref=s_ref, dst_ref=region(a, chip_idx, half),
                                                send_sem=send_sems.at[k], recv_sem=recv_sems.at[k],
                                                device_id=to, device_id_type=MESH)

        first = []
        for j, (ox, oy) in enumerate(others):
            for a in range(n_arr):
                cp = remote(n_arr * j + a, a, chip, c, (ox, oy, c), True)
                cp.start()
                first.append(cp)
        passed = []
        for j, (ox, oy) in enumerate(others):
            ochip = 2 * ox + oy
            for a in range(n_arr):
                k = n_arr * j + a
                remote(k, a, ochip, c, sibling, False).wait_recv()
                cp = remote(3 * n_arr + k, a, ochip, c, sibling, False)
                cp.start()
                passed.append(cp)
        for j, (ox, oy) in enumerate(others):
            ochip = 2 * ox + oy
            for a in range(n_arr):
                remote(3 * n_arr + n_arr * j + a, a, ochip, 1 - c, sibling, False).wait_recv()
        for cp in first + passed:
            cp.wait_send()
        for cp in stores:
            cp.wait()

    any_spec = pl.BlockSpec(memory_space=pl.ANY)
    return pl.pallas_call(
        body, name="gather_weights",
        in_specs=[any_spec] * n_arr, out_specs=[any_spec] * n_arr,
        out_shape=[jax.ShapeDtypeStruct(sh, BF16) for sh in full_shapes],
        scratch_shapes=[pltpu.VMEM(a.shape, F32) for a in shards] + [pltpu.VMEM(a.shape, BF16) for a in shards]
        + [pltpu.SemaphoreType.DMA((6 * n_arr,)), pltpu.SemaphoreType.DMA((6 * n_arr,)),
           pltpu.SemaphoreType.DMA((3 * n_arr,))],
        compiler_params=pltpu.CompilerParams(vmem_limit_bytes=VMEM_LIMIT, has_side_effects=True),
    )(*shards)


def _reduce_grads_tail(grads, g_small, early_slots):
    n_big = len(grads)
    n_arr = n_big + 1
    shard_shapes = [(2 * r, w) for r, w in _GRAD_PIECE_SHAPES]
    small_piece = (SMALL_PIECE, LANES)
    late_in_devs = tuple(d for d in range(N_DEV) if d not in _EARLY_IN_DEVS)

    def body(*refs):
        src = refs[:n_arr]
        early = refs[n_arr:n_arr + n_big]
        out = refs[n_arr + n_big:2 * n_arr + n_big]
        slots = refs[2 * n_arr + n_big:3 * n_arr + n_big]
        sums = refs[3 * n_arr + n_big:4 * n_arr + n_big]
        send1, recv1, send2, recv2, local_sems = refs[4 * n_arr + n_big:]
        x, y, c = _place()
        me = 4 * x + 2 * y + c

        def piece_of(a, dev):
            return src[a].at[dev] if a == n_big else _grad_piece(src[a], a, dev)

        def late(a, dst_dev, src_dev):
            return pltpu.make_async_remote_copy(
                src_ref=piece_of(a, dst_dev), dst_ref=slots[a].at[src_dev],
                send_sem=send1.at[n_arr * dst_dev + a], recv_sem=recv1.at[n_arr * src_dev + a],
                device_id=_dev_id(dst_dev), device_id_type=MESH)

        def late_arrays(dev):
            return (0, n_big) if dev in late_in_devs else (n_big,)

        def load(a, dev):
            return pltpu.make_async_copy(early[a].at[dev], slots[a].at[dev], local_sems.at[n_arr * dev + a])

        def own(a, dev):
            return pltpu.make_async_copy(piece_of(a, dev), slots[a].at[dev], local_sems.at[n_arr * dev + a])

        def early_here(a, dev_is_early):
            return a != 0 or dev_is_early

        for dev in range(N_DEV):
            @pl.when(me == dev)
            def _():
                for a in range(n_arr):
                    own(a, dev).start()
                for peer in range(N_DEV):
                    if peer != dev:
                        for a in late_arrays(peer):
                            late(a, peer, dev).start()
                        for a in range(n_big):
                            if early_here(a, dev in _EARLY_IN_DEVS):
                                load(a, peer).start()
                for a in range(n_arr):
                    own(a, dev).wait()
                for peer in range(N_DEV):
                    if peer != dev:
                        for a in late_arrays(dev):
                            late(a, dev, peer).wait_recv()
                        for a in range(n_big):
                            if early_here(a, dev in _EARLY_IN_DEVS):
                                load(a, peer).wait()

        for a in range(n_arr):
            rows = slots[a].shape[1]
            step = 64 if rows % 64 == 0 else 8

            def add_rows(t, carry, a=a, step=step):
                r0 = pl.multiple_of(t * step, step)
                total = slots[a][0, pl.ds(r0, step), :].astype(F32)
                for dev in range(1, N_DEV):
                    total = total + slots[a][dev, pl.ds(r0, step), :].astype(F32)
                sums[a][pl.ds(r0, step), :] = total
                return carry

            lax.fori_loop(0, rows // step, add_rows, 0)

        shares = []
        keeps = []
        for a in range(n_big):
            r, w = _GRAD_PIECE_SHAPES[a]
            dst = out[a].at[pl.ds(pl.multiple_of(c * r, 8), r), :]
            cp = pltpu.make_async_remote_copy(src_ref=sums[a], dst_ref=dst, send_sem=send2.at[a], recv_sem=recv2.at[a],
                                              device_id=(x, y, 1 - c), device_id_type=MESH)
            cp.start()
            shares.append(cp)
            kp = pltpu.make_async_copy(sums[a], dst, local_sems.at[N_DEV * n_arr + a])
            kp.start()
            keeps.append(kp)
        kp = pltpu.make_async_copy(sums[n_big], out[n_big].at[me], local_sems.at[N_DEV * n_arr + n_big])
        kp.start()
        keeps.append(kp)

        def small_share(dst_dev, src_dev):
            return pltpu.make_async_remote_copy(src_ref=sums[n_big], dst_ref=out[n_big].at[src_dev],
                                                send_sem=send2.at[n_big + dst_dev], recv_sem=recv2.at[n_big + src_dev],
                                                device_id=_dev_id(dst_dev), device_id_type=MESH)

        for dev in range(N_DEV):
            @pl.when(me != dev)
            def _():
                small_share(dev, me).start()
        for a in range(n_big):
            r, w = _GRAD_PIECE_SHAPES[a]
            other = out[a].at[pl.ds(pl.multiple_of((1 - c) * r, 8), r), :]
            pltpu.make_async_remote_copy(src_ref=sums[a], dst_ref=other, send_sem=send2.at[a], recv_sem=recv2.at[a],
                                         device_id=(x, y, 1 - c), device_id_type=MESH).wait_recv()
        for dev in range(N_DEV):
            @pl.when(me != dev)
            def _():
                small_share(dev, dev).wait_recv()
                small_share(dev, me).wait_send()
                for a in late_arrays(dev):
                    late(a, dev, me).wait_send()
        for cp in shares:
            cp.wait_send()
        for kp in keeps:
            kp.wait()

    any_spec = pl.BlockSpec(memory_space=pl.ANY)
    return pl.pallas_call(
        body, name="reduce_grads_tail",
        in_specs=[any_spec] * (n_arr + n_big), out_specs=[any_spec] * n_arr,
        out_shape=[jax.ShapeDtypeStruct(sh, F32) for sh in shard_shapes]
        + [jax.ShapeDtypeStruct((N_DEV,) + small_piece, F32)],
        scratch_shapes=[pltpu.VMEM((N_DEV,) + sh, BF16) for sh in _GRAD_PIECE_SHAPES]
        + [pltpu.VMEM((N_DEV,) + small_piece, F32)]
        + [pltpu.VMEM(sh, F32) for sh in _GRAD_PIECE_SHAPES] + [pltpu.VMEM(small_piece, F32)]
        + [pltpu.SemaphoreType.DMA((N_DEV * n_arr,)), pltpu.SemaphoreType.DMA((N_DEV * n_arr,)),
           pltpu.SemaphoreType.DMA((n_big + N_DEV,)), pltpu.SemaphoreType.DMA((n_big + N_DEV,)),
           pltpu.SemaphoreType.DMA((N_DEV * n_arr + n_arr,))],
        compiler_params=pltpu.CompilerParams(vmem_limit_bytes=VMEM_LIMIT, has_side_effects=True),
    )(*grads, g_small, *early_slots)


def _reduce_grads(g_in, g_up_a, g_up_b, g_out, g_small):
    big = (g_in, g_up_a, g_up_b, g_out)
    n_big = len(big)
    col_sharded = (True, True, True, False)
    piece_shapes = []
    for a, arr in enumerate(big):
        r, w = arr.shape
        piece_shapes.append((r // 2, w // N_CHIPS) if col_sharded[a] else (r // (2 * N_CHIPS), w))
    shard_shapes = [(2 * r, w) for r, w in piece_shapes]
    n_arr = n_big + 1

    def body(*refs):
        src = refs[:n_arr]
        out = refs[n_arr:2 * n_arr]
        slots = refs[2 * n_arr:3 * n_arr]
        sums = refs[3 * n_arr:4 * n_arr]
        send1, recv1, send2, recv2, local_sems = refs[4 * n_arr:]
        x, y, c = _place()
        me = 4 * x + 2 * y + c

        def piece_of(a, dev):
            chip_idx, half = dev // 2, dev % 2
            if a == n_big:
                return src[a].at[dev]
            r, w = piece_shapes[a]
            if col_sharded[a]:
                return src[a].at[pl.ds(pl.multiple_of(half * r, 16), r), pl.ds(pl.multiple_of(chip_idx * w, LANES), w)]
            return src[a].at[pl.ds(pl.multiple_of(dev * r, 16), r), :]

        def dev_id(dev):
            return (dev // 4, (dev // 2) % 2, dev % 2)

        own = [pltpu.make_async_copy(piece_of(a, me), slots[a].at[me], local_sems.at[a]) for a in range(n_arr)]
        for cp in own:
            cp.start()
        sends = []
        for d in range(1, N_DEV):
            peer = (me + d) % N_DEV
            for a in range(n_arr):
                cp = pltpu.make_async_remote_copy(
                    src_ref=piece_of(a, peer), dst_ref=slots[a].at[me],
                    send_sem=send1.at[n_arr * peer + a], recv_sem=recv1.at[n_arr * me + a],
                    device_id=dev_id(peer), device_id_type=MESH)
                cp.start()
                sends.append(cp)
        for cp in own:
            cp.wait()
        for d in range(1, N_DEV):
            peer = (me + d) % N_DEV
            for a in range(n_arr):
                pltpu.make_async_remote_copy(
                    src_ref=piece_of(a, peer), dst_ref=slots[a].at[peer],
                    send_sem=send1.at[n_arr * peer + a], recv_sem=recv1.at[n_arr * peer + a],
                    device_id=dev_id(peer), device_id_type=MESH).wait_recv()
        for a in range(n_arr):
            rows = slots[a].shape[1]
            step = 64 if rows % 64 == 0 else 8

            def add_rows(t, carry, a=a, step=step):
                r0 = pl.multiple_of(t * step, step)
                total = slots[a][0, pl.ds(r0, step), :].astype(F32)
                for dev in range(1, N_DEV):
                    total = total + slots[a][dev, pl.ds(r0, step), :].astype(F32)
                sums[a][pl.ds(r0, step), :] = total
                return carry

            lax.fori_loop(0, rows // step, add_rows, 0)
        shares = []
        keeps = []
        for a in range(n_big):
            r, w = piece_shapes[a]
            dst = out[a].at[pl.ds(pl.multiple_of(c * r, 8), r), :]
            cp = pltpu.make_async_remote_copy(src_ref=sums[a], dst_ref=dst, send_sem=send2.at[a], recv_sem=recv2.at[a],
                                              device_id=(x, y, 1 - c), device_id_type=MESH)
            cp.start()
            shares.append(cp)
            kp = pltpu.make_async_copy(sums[a], dst, local_sems.at[n_arr + a])
            kp.start()
            keeps.append(kp)
        kp = pltpu.make_async_copy(sums[n_big], out[n_big].at[me], local_sems.at[n_arr + n_big])
        kp.start()
        keeps.append(kp)
        for d in range(1, N_DEV):
            peer = (me + d) % N_DEV
            cp = pltpu.make_async_remote_copy(src_ref=sums[n_big], dst_ref=out[n_big].at[me],
                                              send_sem=send2.at[n_big + peer], recv_sem=recv2.at[n_big + me],
                                              device_id=dev_id(peer), device_id_type=MESH)
            cp.start()
            shares.append(cp)
        for a in range(n_big):
            r, w = piece_shapes[a]
            other = out[a].at[pl.ds(pl.multiple_of((1 - c) * r, 8), r), :]
            pltpu.make_async_remote_copy(src_ref=sums[a], dst_ref=other, send_sem=send2.at[a], recv_sem=recv2.at[a],
                                         device_id=(x, y, 1 - c), device_id_type=MESH).wait_recv()
        for d in range(1, N_DEV):
            peer = (me + d) % N_DEV
            pltpu.make_async_remote_copy(src_ref=sums[n_big], dst_ref=out[n_big].at[peer],
                                         send_sem=send2.at[n_big + peer], recv_sem=recv2.at[n_big + peer],
                                         device_id=dev_id(peer), device_id_type=MESH).wait_recv()
        for cp in sends + shares:
            cp.wait_send()
        for kp in keeps:
            kp.wait()

    any_spec = pl.BlockSpec(memory_space=pl.ANY)
    small_piece = (SMALL_PIECE, LANES)
    return pl.pallas_call(
        body, name="reduce_grads",
        in_specs=[any_spec] * n_arr, out_specs=[any_spec] * n_arr,
        out_shape=[jax.ShapeDtypeStruct(sh, F32) for sh in shard_shapes]
        + [jax.ShapeDtypeStruct((N_DEV,) + small_piece, F32)],
        scratch_shapes=[pltpu.VMEM((N_DEV,) + sh, BF16) for sh in piece_shapes]
        + [pltpu.VMEM((N_DEV,) + small_piece, F32)]
        + [pltpu.VMEM(sh, F32) for sh in piece_shapes] + [pltpu.VMEM(small_piece, F32)]
        + [pltpu.SemaphoreType.DMA((N_DEV * n_arr,)), pltpu.SemaphoreType.DMA((N_DEV * n_arr,)),
           pltpu.SemaphoreType.DMA((n_big + N_DEV,)), pltpu.SemaphoreType.DMA((n_big + N_DEV,)),
           pltpu.SemaphoreType.DMA((2 * n_arr,))],
        compiler_params=pltpu.CompilerParams(vmem_limit_bytes=VMEM_LIMIT, has_side_effects=True),
    )(*big, g_small)


_SMALL_PARTS = (("norm_g", 8), ("sgu_ln_g", 8), ("sgu_ln_b", 8), ("w_spatial", 1024), ("b_spatial", 8),
                ("final_norm_g", 8))


def _pack_small(parts):
    rows = []
    used = 0
    for name, n_rows in _SMALL_PARTS:
        a = parts[name].reshape(-1, LANES).astype(F32)
        a = jnp.pad(a, ((0, n_rows - a.shape[0]), (0, 0)))
        rows.append(a)
        used += n_rows
    rows.append(jnp.zeros((SMALL_ROWS - used, LANES), F32))
    return jnp.concatenate(rows, axis=0)


def _unpack_small(packed, shapes):
    out = {}
    r0 = 0
    for name, n_rows in _SMALL_PARTS:
        n = math.prod(shapes[name])
        out[name] = packed[r0:r0 + n // LANES].reshape(shapes[name])
        r0 += n_rows
    return out


def _local_step(proj, ht, x, target, norm_g, w_in, sgu_ln_g, sgu_ln_b, w_spatial, b_spatial, w_up_a, w_up_b, w_out,
                final_norm_g, blk):
    pos = jnp.arange(SGU_CHUNK)
    keep = (pos[None, :] // SGU_SUBCHUNK) <= (pos[:, None] // SGU_SUBCHUNK)
    w_mask = jnp.where(keep[None], w_spatial, 0.0).astype(BF16)
    w_mask_t = jnp.swapaxes(w_mask, 1, 2)
    bias_full = jnp.repeat(b_spatial.T, GROUP_DIM, axis=1)
    ln_g = sgu_ln_g.reshape(1, D_BRANCH)
    ln_b = sgu_ln_b.reshape(1, D_BRANCH)
    final_g = final_norm_g.reshape(1, D_MODEL)

    o, ya, rsave = _attn_fwd(proj, blk, ATTN_PAIRS)
    yb = _sgu_fwd(proj, ln_g, ln_b, w_mask, bias_full)
    dzg, do, dyb, dx2, g_out, g_up_a, g_up_b, loss_acc, d_final = _mid(
        proj, ya, yb, o, x, target, final_g, w_up_a, w_up_b, w_out)
    dsgu, d_wsp, d_bsp, d_lng, d_lnb = _sgu_bwd(proj, dyb, ln_g, ln_b, w_mask, w_mask_t, bias_full)
    g_in = _dwin_piece(ht, dzg, lambda j: jnp.where(j == 0, COL_ZA, COL_GA - 1 + j), None)
    g_in = _dwin_piece(ht, dsgu, lambda j: COL_UB + j, g_in)
    dq, dk, dv, *early_slots = _attn_bwd(proj, do, rsave, blk, ATTN_PAIRS, (g_in, g_up_a, g_up_b, g_out))
    g_in = _dwin_piece(ht, dq, lambda j: COL_Q + j, g_in)
    g_in = _dwin_piece(ht, dk, lambda j: COL_K + j, g_in)
    g_in = _dwin_piece(ht, dv, lambda j: COL_V + j, g_in)
    pieces = [(dq, COL_Q * D_BRANCH, 0, D_BRANCH), (dk, COL_K * D_BRANCH, 0, D_BRANCH),
              (dv, COL_V * D_BRANCH, 0, D_BRANCH), (dzg, COL_ZA * D_BRANCH, 0, D_BRANCH),
              (dsgu, COL_UB * D_BRANCH, 0, 3 * D_BRANCH), (dzg, COL_GA * D_BRANCH, D_BRANCH, 2 * D_MODEL)]
    dx, d_norm = _dh_dx(pieces, w_in, x, norm_g, dx2)
    small = {"norm_g": d_norm, "sgu_ln_g": d_lng, "sgu_ln_b": d_lnb, "w_spatial": d_wsp,
             "b_spatial": d_bsp[:, :N_GROUPS].T, "final_norm_g": d_final}
    return loss_acc[0, 0], dx, (g_in, g_up_a, g_up_b, g_out), small, early_slots


def kernel(x, norm_g, w_in, sgu_ln_g, sgu_ln_b, w_spatial, b_spatial, w_up_a, w_up_b, w_out, final_norm_g, loss_target, m_norm_g, m_w_in, m_sgu_ln_g, m_sgu_ln_b, m_w_spatial, m_b_spatial, m_w_up_a, m_w_up_b, m_w_out, m_final_norm_g, v_norm_g, v_w_in, v_sgu_ln_g, v_sgu_ln_b, v_w_spatial, v_b_spatial, v_w_up_a, v_w_up_b, v_w_out, v_final_norm_g):
    big_names = ("w_in", "w_up_a", "w_up_b", "w_out")
    small_names = tuple(n for n, _ in _SMALL_PARTS)
    names = ("norm_g", "w_in", "sgu_ln_g", "sgu_ln_b", "w_spatial", "b_spatial", "w_up_a", "w_up_b", "w_out",
             "final_norm_g")
    w = dict(norm_g=norm_g, w_in=w_in, sgu_ln_g=sgu_ln_g, sgu_ln_b=sgu_ln_b, w_spatial=w_spatial,
             b_spatial=b_spatial, w_up_a=w_up_a, w_up_b=w_up_b, w_out=w_out, final_norm_g=final_norm_g)
    m = dict(norm_g=m_norm_g, w_in=m_w_in, sgu_ln_g=m_sgu_ln_g, sgu_ln_b=m_sgu_ln_b, w_spatial=m_w_spatial,
             b_spatial=m_b_spatial, w_up_a=m_w_up_a, w_up_b=m_w_up_b, w_out=m_w_out, final_norm_g=m_final_norm_g)
    v = dict(norm_g=v_norm_g, w_in=v_w_in, sgu_ln_g=v_sgu_ln_g, sgu_ln_b=v_sgu_ln_b, w_spatial=v_w_spatial,
             b_spatial=v_b_spatial, w_up_a=v_w_up_a, w_up_b=v_w_up_b, w_out=v_w_out, final_norm_g=v_final_norm_g)
    shapes = {n: w[n].shape for n in names}
    flat2d = lambda a: a.reshape(a.shape[-2:])

    proj, ht, *full = _in_proj_gather(x[0], norm_g, *[flat2d(w[n]) for n in big_names])
    loss, dx, big_grads, small, early_slots = _local_step(
        proj, ht, x[0], loss_target[0], norm_g, full[0], sgu_ln_g[0], sgu_ln_b[0], w_spatial[0], b_spatial[0],
        full[1], full[2], full[3], final_norm_g, ATTN_BLOCK)
    packed = _pack_small(small).reshape(N_DEV, SMALL_PIECE, LANES)
    red = _reduce_grads_tail(big_grads, packed, early_slots)

    grads, deltas, new_m, new_v = {}, {}, {}, {}
    for n, g in zip(big_names, red[:4]):
        d, nm, nv = _adamw(flat2d(w[n]), g, flat2d(m[n]), flat2d(v[n]))
        grads[n], deltas[n], new_m[n], new_v[n] = (a.reshape(shapes[n]) for a in (g, d, nm, nv))
    g_small = red[4].reshape(SMALL_ROWS, LANES)
    d, nm, nv = _adamw(_pack_small({n: w[n] for n in small_names}), g_small,
                       _pack_small({n: m[n] for n in small_names}), _pack_small({n: v[n] for n in small_names}))
    for src, dst in ((g_small, grads), (d, deltas), (nm, new_m), (nv, new_v)):
        dst.update(_unpack_small(src, shapes))

    loss = lax.psum(loss, ("x", "y", "c"))
    return (loss, dx[None], *[grads[n] for n in names], *[deltas[n] for n in names],
            *[new_m[n] for n in names], *[new_v[n] for n in names])
```

```python
import functools
import math

import jax
import jax.numpy as jnp
from jax import lax
from jax.experimental import pallas as pl
from jax.experimental.pallas import tpu as pltpu

F32 = jnp.float32
BF16 = jnp.bfloat16

D_MODEL = 1024
N_HEADS = 8
HEAD_DIM = 64
D_BRANCH = 512
D_IN = 4 * D_BRANCH + 3 * D_BRANCH + 2 * D_MODEL
N_GROUPS = 8
GROUP_DIM = 64
SGU_CHUNK = 128
SGU_SUBCHUNK = 64
GROUP_SHIFT = 6
EPS = 1e-6
LANES = 128
ATTN_BLOCK = 256
SCAN_PASSES = 1
ATTN_PAIRS = 2
N_CHIPS = 4
N_DEV = 8
MESH = pl.DeviceIdType.MESH

ADAM_LR = 0.001
ADAM_B1 = 0.9
ADAM_B2 = 0.999
ADAM_EPS = 1e-08
ADAM_WD = 0.01
ADAM_STEP = 10

COL_Q, COL_K, COL_V, COL_ZA, COL_UB, COL_VB, COL_ZB, COL_GA, COL_GB = 0, 1, 2, 3, 4, 5, 6, 7, 9

VMEM_LIMIT = 56 * 1024 * 1024

SMALL_ROWS = 1088
SMALL_PIECE = SMALL_ROWS // N_DEV


def _cparams(sem=None):
    return pltpu.CompilerParams(dimension_semantics=sem, vmem_limit_bytes=VMEM_LIMIT)


def _aligned(v, m):
    return v if isinstance(v, int) else pl.multiple_of(v, m)


def _sigmoid(x):
    return 1.0 / (1.0 + jnp.exp(-x))


def _gelu_and_grad(x):
    k = math.sqrt(2.0 / math.pi)
    x2 = x * x
    inner = k * (x + 0.044715 * x * x2)
    th = jnp.tanh(inner)
    g = 0.5 * x * (1.0 + th)
    dg = 0.5 * (1.0 + th) + 0.5 * x * (1.0 - th * th) * (k * (1.0 + 3.0 * 0.044715 * x2))
    return g, dg


def _split_dot(a, b_bf16, passes):
    out = None
    rem = a
    for _ in range(passes):
        part = rem.astype(BF16)
        d = jnp.dot(part, b_bf16, preferred_element_type=F32)
        out = d if out is None else out + d
        rem = rem - part.astype(F32)
    return out


def _dot_nt(a, b):
    return lax.dot_general(a, b, (((1,), (1,)), ((), ())), preferred_element_type=F32)


def _dot_tn(a, b):
    return lax.dot_general(a, b, (((0,), (0,)), ((), ())), preferred_element_type=F32)


def _place():
    x, y, c = lax.axis_index("x"), lax.axis_index("y"), lax.axis_index("c")
    return x, y, c


def _in_proj_gather(x, norm_g, w_in, w_up_a, w_up_b, w_out):
    s = x.shape[0]
    tm = min(512, s)
    nt = s // tm
    shards = (w_in, w_up_a, w_up_b, w_out)
    n_arr = len(shards)
    col_sharded = (True, True, True, False)
    full_shapes = ((D_MODEL, D_IN), (D_BRANCH, D_MODEL), (D_BRANCH, D_MODEL), (D_MODEL, D_MODEL))
    w_shard = w_in.shape[1]
    half_rows = D_MODEL // 2
    stage_rows = 256

    def body(order_ref, x_ref, g_ref, *refs):
        src = refs[:n_arr]
        proj_ref, ht_ref = refs[n_arr:n_arr + 2]
        out = refs[n_arr + 2:2 * n_arr + 2]
        wsc, h_scr, stage = refs[2 * n_arr + 2:2 * n_arr + 5]
        small_stage = refs[2 * n_arr + 5:2 * n_arr + 8]
        small_cast = refs[2 * n_arr + 8:2 * n_arr + 11]
        send_sems, recv_sems, local_sems = refs[2 * n_arr + 11:]
        k = pl.program_id(0)
        i = pl.program_id(1)
        x_, y_, c = _place()
        chip = 2 * x_ + y_
        sibling = (x_, y_, 1 - c)
        others = [(x_, 1 - y_), (1 - x_, y_), (1 - x_, 1 - y_)]

        def region(a, chip_idx, half):
            if a == 0:
                return wsc.at[chip_idx, pl.ds(_aligned(half * half_rows, 16), half_rows), :]
            r, w = shards[a].shape
            hr = r // 2
            if col_sharded[a]:
                return out[a].at[pl.ds(_aligned(half * hr, 16), hr), pl.ds(_aligned(chip_idx * w, LANES), w)]
            return out[a].at[pl.ds(_aligned(chip_idx * r + half * hr, 16), hr), :]

        def remote(kk, a, chip_idx, half, to, own):
            s_ref = region(a, chip_idx, half)
            if own and a > 0:
                hr = shards[a].shape[0] // 2
                s_ref = small_cast[a - 1].at[pl.ds(_aligned(half * hr, 16), hr), :]
            return pltpu.make_async_remote_copy(src_ref=s_ref, dst_ref=region(a, chip_idx, half),
                                                send_sem=send_sems.at[kk], recv_sem=recv_sems.at[kk],
                                                device_id=to, device_id_type=MESH)

        def keep_whole(kk, chip_idx):
            return pltpu.make_async_copy(wsc.at[chip_idx],
                                         out[0].at[:, pl.ds(_aligned(chip_idx * w_shard, LANES), w_shard)],
                                         local_sems.at[kk])

        def small_stores():
            cps = []
            for a in range(1, n_arr):
                hr = shards[a].shape[0] // 2
                for half in range(2):
                    cps.append(pltpu.make_async_copy(small_cast[a - 1].at[pl.ds(half * hr, hr), :],
                                                     region(a, chip, half), local_sems.at[4 + 2 * (a - 1) + half]))
            return cps

        def arrive_and_pass(j):
            ochip = chip ^ j
            for a in range(n_arr):
                kk = n_arr * (j - 1) + a
                remote(kk, a, ochip, c, sibling, False).wait_recv()
                remote(3 * n_arr + kk, a, ochip, c, sibling, False).start()

        def from_sibling(j, a):
            remote(3 * n_arr + n_arr * (j - 1) + a, a, chip ^ j, 1 - c, sibling, False).wait_recv()

        @pl.when((k == 0) & (i == 0))
        def _():
            for r0 in range(0, D_MODEL, stage_rows):
                pltpu.sync_copy(src[0].at[pl.ds(r0, stage_rows), :], stage)
                wsc[chip, pl.ds(r0, stage_rows), :] = stage[...].astype(BF16)
            for a in range(1, n_arr):
                pltpu.sync_copy(src[a], small_stage[a - 1])
                small_cast[a - 1][...] = small_stage[a - 1][...].astype(BF16)
            for j in (1, 2, 3):
                for a in range(n_arr):
                    remote(n_arr * (j - 1) + a, a, chip, c, (*others[j - 1], c), True).start()
            keep_whole(0, chip).start()
            for cp in small_stores():
                cp.start()

        @pl.when((k == 1) & (i == 0))
        def _():
            arrive_and_pass(1)
            arrive_and_pass(2)
            from_sibling(1, 0)
            keep_whole(1, chip ^ 1).start()

        @pl.when((k == 2) & (i == 0))
        def _():
            from_sibling(2, 0)
            keep_whole(2, chip ^ 2).start()
            arrive_and_pass(3)

        @pl.when((k == 3) & (i == 0))
        def _():
            from_sibling(3, 0)
            keep_whole(3, chip ^ 3).start()

        @pl.when(k == 0)
        def _():
            xf = x_ref[...]
            r = lax.rsqrt(jnp.mean(xf * xf, axis=-1, keepdims=True) + EPS)
            h = xf * r * g_ref[...]
            h_scr[i] = h.astype(BF16)
            ht_ref[...] = h.T.astype(BF16)

        proj_ref[...] = jnp.dot(h_scr[i], wsc[order_ref[k]], preferred_element_type=F32).astype(BF16)

        @pl.when((k == 3) & (i == nt - 1))
        def _():
            for j in (1, 2, 3):
                for a in range(1, n_arr):
                    from_sibling(j, a)
            for j in (1, 2, 3):
                for a in range(n_arr):
                    kk = n_arr * (j - 1) + a
                    remote(kk, a, chip, c, (*others[j - 1], c), True).wait_send()
                    remote(3 * n_arr + kk, a, chip ^ j, c, sibling, False).wait_send()
            for kk in range(4):
                keep_whole(kk, chip ^ kk).wait()
            for cp in small_stores():
                cp.wait()

    any_spec = pl.BlockSpec(memory_space=pl.ANY)
    tile = lambda kk, ii: jnp.where(kk == 0, ii, nt - 1)
    grid_spec = pltpu.PrefetchScalarGridSpec(
        num_scalar_prefetch=1, grid=(N_CHIPS, nt),
        in_specs=[pl.BlockSpec((tm, D_MODEL), lambda kk, ii, order: (tile(kk, ii), 0)),
                  pl.BlockSpec((1, D_MODEL), lambda kk, ii, order: (0, 0))] + [any_spec] * n_arr,
        out_specs=[pl.BlockSpec((tm, w_shard), lambda kk, ii, order: (ii, order[kk])),
                   pl.BlockSpec((D_MODEL, tm), lambda kk, ii, order: (0, tile(kk, ii)))] + [any_spec] * n_arr,
        scratch_shapes=[pltpu.VMEM((N_CHIPS, D_MODEL, w_shard), BF16), pltpu.VMEM((nt, tm, D_MODEL), BF16),
                        pltpu.VMEM((stage_rows, w_shard), F32)]
        + [pltpu.VMEM(a.shape, F32) for a in shards[1:]] + [pltpu.VMEM(a.shape, BF16) for a in shards[1:]]
        + [pltpu.SemaphoreType.DMA((6 * n_arr,)), pltpu.SemaphoreType.DMA((6 * n_arr,)),
           pltpu.SemaphoreType.DMA((4 + 2 * (n_arr - 1),))])
    x_, y_, _ = _place()
    order = (2 * x_ + y_) ^ jnp.arange(N_CHIPS, dtype=jnp.int32)
    return pl.pallas_call(
        body, name="in_proj_gather", grid_spec=grid_spec,
        out_shape=[jax.ShapeDtypeStruct((s, D_IN), BF16), jax.ShapeDtypeStruct((D_MODEL, s), BF16)]
        + [jax.ShapeDtypeStruct(sh, BF16) for sh in full_shapes],
        compiler_params=pltpu.CompilerParams(dimension_semantics=("arbitrary", "arbitrary"),
                                             vmem_limit_bytes=VMEM_LIMIT, has_side_effects=True),
    )(order, x, norm_g, *shards)


def _neg_softplus_parts(z):
    zb = z.astype(BF16)
    p = jnp.exp(-jnp.abs(zb))
    return p, jnp.maximum(zb, jnp.zeros_like(zb)) + jnp.log(1.0 + p)


def _split_cat(a, passes):
    parts = []
    rem = a
    for k in range(passes):
        part = rem.astype(BF16)
        parts.append(part)
        if k + 1 < passes:
            rem = rem - part.astype(F32)
    return parts[0] if passes == 1 else jnp.concatenate(parts, axis=1)


def _tri(blk, upper, sign):
    row = lax.broadcasted_iota(jnp.int32, (blk, blk), 0)
    col = lax.broadcasted_iota(jnp.int32, (blk, blk), 1)
    keep = (row <= col) if upper else (row >= col)
    t = jnp.where(keep, sign, 0.0).astype(BF16)
    return t if SCAN_PASSES == 1 else jnp.concatenate([t] * SCAN_PASSES, axis=0)


def _attn_fwd(proj, blk, npairs):
    s = proj.shape[0]
    nq = s // blk
    scale = HEAD_DIM ** -0.5
    heads = tuple(range(2 * npairs))
    width = LANES * npairs

    def body(q_ref, k_ref, v_ref, za_ref, o_ref, ya_ref, rs_ref, acc_ref, r_ref, z_ref):
        i = pl.program_id(1)
        lane = lax.broadcasted_iota(jnp.int32, (blk, LANES), 1)
        lo_half = lane < HEAD_DIM
        qm = []
        for pr in range(npairs):
            q = q_ref[:, LANES * pr:LANES * (pr + 1)] * jnp.asarray(scale, BF16)
            zero = jnp.zeros_like(q)
            qm += [jnp.where(lo_half, q, zero), jnp.where(lo_half, zero, q)]
        row = lax.broadcasted_iota(jnp.int32, (blk, blk), 0)
        col = lax.broadcasted_iota(jnp.int32, (blk, blk), 1)
        before = col < row
        tneg = _tri(blk, False, -1.0)
        acc_ref[...] = jnp.zeros_like(acc_ref)
        r_ref[...] = jnp.zeros_like(r_ref)
        rs_ref[...] = jnp.zeros_like(rs_ref)

        def scores(j):
            ks = pl.multiple_of(j * blk, blk)
            return [_dot_nt(qm[h], k_ref[pl.ds(ks, blk), LANES * (h // 2):LANES * (h // 2 + 1)]) for h in heads]

        for h, zh in enumerate(scores(i)):
            z_ref[h] = zh

        def block(j, diag):
            ks = pl.multiple_of(j * blk, blk)
            vj = [v_ref[pl.ds(ks, blk), LANES * pr:LANES * (pr + 1)] for pr in range(npairs)]
            z = [z_ref[h] for h in heads]
            sp = [_neg_softplus_parts(z[h])[1] for h in heads]
            if diag:
                sp = [jnp.where(before, sp[h], 0.0) for h in heads]
            cin = [jnp.dot(_split_cat(sp[h], SCAN_PASSES), tneg, preferred_element_type=F32) for h in heads]
            for h, zh in enumerate(scores(jnp.maximum(j - 1, 0))):
                z_ref[h] = zh
            w = [jnp.exp(z[h] + cin[h]) for h in heads]
            if diag:
                w = [jnp.where(before, w[h], 0.0) for h in heads]
            pv = [jnp.dot(w[h].astype(BF16), vj[h // 2], preferred_element_type=F32) for h in heads]
            r = [r_ref[h] for h in heads]
            for h in heads:
                acc_ref[h] += pv[h] * jnp.exp(r[h])
                r_ref[h] = r[h] + cin[h][:, 0:1]
            for pr in range(npairs):
                rs_ref[pr] = jnp.where(lane == j, r[2 * pr], jnp.where(lane == j + HEAD_DIM, r[2 * pr + 1], rs_ref[pr]))

        block(i, True)

        def loop_body(jj, carry):
            block(i - 1 - jj, False)
            return carry

        lax.fori_loop(0, i, loop_body, 0)
        for pr in range(npairs):
            cols = slice(LANES * pr, LANES * (pr + 1))
            o = jnp.where(lo_half, acc_ref[2 * pr], acc_ref[2 * pr + 1])
            o_ref[:, cols] = o.astype(BF16)
            za = za_ref[:, cols].astype(F32)
            ya_ref[:, cols] = (o * (za * _sigmoid(za))).astype(BF16)

    n_steps = N_HEADS // (2 * npairs)
    return pl.pallas_call(
        body, name="attn_fwd", grid=(n_steps, nq),
        in_specs=[pl.BlockSpec((blk, width), lambda p, i: (i, n_steps * COL_Q + p)),
                  pl.BlockSpec((s, width), lambda p, i: (0, n_steps * COL_K + p)),
                  pl.BlockSpec((s, width), lambda p, i: (0, n_steps * COL_V + p)),
                  pl.BlockSpec((blk, width), lambda p, i: (i, n_steps * COL_ZA + p))],
        out_specs=[pl.BlockSpec((blk, width), lambda p, i: (i, p)),
                   pl.BlockSpec((blk, width), lambda p, i: (i, p)),
                   pl.BlockSpec((npairs, blk, LANES), lambda p, i: (p, i, 0))],
        out_shape=[jax.ShapeDtypeStruct((s, D_BRANCH), BF16), jax.ShapeDtypeStruct((s, D_BRANCH), BF16),
                   jax.ShapeDtypeStruct((N_HEADS // 2, s, LANES), F32)],
        scratch_shapes=[pltpu.VMEM((2 * npairs, blk, LANES), F32), pltpu.VMEM((2 * npairs, blk, 1), F32),
                        pltpu.VMEM((2 * npairs, blk, blk), F32)],
        compiler_params=_cparams(("parallel", "parallel")),
    )(proj, proj, proj, proj)


_GRAD_COL_SHARDED = (True, True, True, False)
_GRAD_FULL_SHAPES = ((D_MODEL, D_IN), (D_BRANCH, D_MODEL), (D_BRANCH, D_MODEL), (D_MODEL, D_MODEL))
_GRAD_PIECE_SHAPES = tuple((r // 2, w // N_CHIPS) if cs else (r // (2 * N_CHIPS), w)
                           for (r, w), cs in zip(_GRAD_FULL_SHAPES, _GRAD_COL_SHARDED))
_EARLY_IN_DEVS = (4, 5, 6, 7)
_LATE_IN_DEVS = (0, 1, 2, 3)


def _grad_piece(ref, a, dev):
    r, w = _GRAD_PIECE_SHAPES[a]
    if _GRAD_COL_SHARDED[a]:
        return ref.at[pl.ds((dev % 2) * r, r), pl.ds((dev // 2) * w, w)]
    return ref.at[pl.ds(dev * r, r), :]


def _dev_id(dev):
    return (dev // 4, (dev // 2) % 2, dev % 2)


def _early_arrays(dev):
    return (0, 1, 2, 3) if dev in _EARLY_IN_DEVS else (1, 2, 3)


def _attn_bwd(proj, do, rsave, blk, npairs, grads):
    s = proj.shape[0]
    nq = s // blk
    scale = HEAD_DIM ** -0.5
    heads = tuple(range(2 * npairs))
    width = LANES * npairs

    n_steps = N_HEADS // (2 * npairs)
    n_g = len(grads)

    def body(q_ref, k_ref, v_ref, do_ref, rs_ref, *refs):
        g_src = refs[:n_g]
        dq_ref, dk_ref, dv_ref = refs[n_g:n_g + 3]
        g_slots = refs[n_g + 3:2 * n_g + 3]
        dk_acc, dv_acc, dq_acc, e_ref, qmt_ref, domt_ref, rst_ref, send_sems, recv_sems = refs[2 * n_g + 3:]
        i = pl.program_id(1)
        x_, y_, c_ = lax.axis_index("x"), lax.axis_index("y"), lax.axis_index("c")
        me = 4 * x_ + 2 * y_ + c_

        def early(a, dst_dev, src_dev):
            return pltpu.make_async_remote_copy(
                src_ref=_grad_piece(g_src[a], a, dst_dev), dst_ref=g_slots[a].at[src_dev],
                send_sem=send_sems.at[n_g * dst_dev + a], recv_sem=recv_sems.at[n_g * src_dev + a],
                device_id=_dev_id(dst_dev), device_id_type=MESH)

        @pl.when((pl.program_id(0) == 0) & (i == 0))
        def _():
            for dev in range(N_DEV):
                @pl.when(me != dev)
                def _():
                    for a in _early_arrays(dev):
                        early(a, dev, me).start()

        lane = lax.broadcasted_iota(jnp.int32, (blk, LANES), 1)
        lo_half = lane < HEAD_DIM
        qm, dom = [], []
        for pr in range(npairs):
            cols = slice(LANES * pr, LANES * (pr + 1))
            q = q_ref[:, cols] * jnp.asarray(scale, BF16)
            zero = jnp.zeros_like(q)
            qm += [jnp.where(lo_half, q, zero), jnp.where(lo_half, zero, q)]
            dout = do_ref[:, cols].astype(F32)
            dom += [jnp.where(lo_half, dout, 0.0), jnp.where(lo_half, 0.0, dout)]
        row = lax.broadcasted_iota(jnp.int32, (blk, blk), 0)
        col = lax.broadcasted_iota(jnp.int32, (blk, blk), 1)
        before = col < row
        tneg = _tri(blk, False, -1.0)
        tfwd = _tri(blk, True, 1.0)

        @pl.when(i == 0)
        def _():
            dk_acc[...] = jnp.zeros_like(dk_acc)
            dv_acc[...] = jnp.zeros_like(dv_acc)

        dq_acc[...] = jnp.zeros_like(dq_acc)
        e_ref[...] = jnp.zeros_like(e_ref)
        for h in heads:
            qmt_ref[h] = qm[h].astype(F32).T.astype(BF16)
            domt_ref[h] = dom[h].T
        for pr in range(npairs):
            rst_ref[pr] = rs_ref[pr].T

        def block(j, diag):
            ks = pl.multiple_of(j * blk, blk)
            kj = [k_ref[pl.ds(ks, blk), LANES * pr:LANES * (pr + 1)] for pr in range(npairs)]
            vj = [v_ref[pl.ds(ks, blk), LANES * pr:LANES * (pr + 1)] for pr in range(npairs)]
            z = [_dot_nt(qm[h], kj[h // 2]) for h in heads]
            dost = [(domt_ref[h] * jnp.exp(rst_ref[h // 2, pl.ds(j + HEAD_DIM * (h % 2), 1), :])).astype(BF16)
                    for h in heads]
            er = [jnp.exp(jnp.sum(jnp.where(lane == j + HEAD_DIM * (h % 2), rs_ref[h // 2], 0.0), axis=-1,
                                  keepdims=True)) for h in heads]
            dos = [(dom[h] * er[h]).astype(BF16) for h in heads]
            dw = [_dot_nt(dos[h], vj[h // 2]) for h in heads]
            psp = [_neg_softplus_parts(z[h]) for h in heads]
            sp = [psp[h][1] for h in heads]
            if diag:
                sp = [jnp.where(before, sp[h], 0.0) for h in heads]
            cin = [jnp.dot(_split_cat(sp[h], SCAN_PASSES), tneg, preferred_element_type=F32) for h in heads]
            w = [jnp.exp(z[h] + cin[h]) for h in heads]
            if diag:
                w = [jnp.where(before, w[h], 0.0) for h in heads]
            e =[dw[h] * w[h] for h in heads]
            eincl = [jnp.dot(_split_cat(e[h], SCAN_PASSES), tfwd, preferred_element_type=F32) + e_ref[h]
                     for h in heads]
            dz = []
            for h in heads:
                p = psp[h][0]
                beta = jnp.where(z[h] >= 0.0, 1.0, p) / (1.0 + p)
                d = e[h] - beta * eincl[h]
                dz.append((jnp.where(before, d, 0.0) if diag else d).astype(BF16))
            wb = [w[h].astype(BF16) for h in heads]
            for h in heads:
                e_ref[h] = eincl[h][:, blk - 1:blk]
                dq_acc[h] += jnp.dot(dz[h], kj[h // 2], preferred_element_type=F32)
            for pr in range(npairs):
                rows = slice(LANES * pr, LANES * (pr + 1))
                h0, h1 = 2 * pr, 2 * pr + 1
                dk_acc[rows, pl.ds(ks, blk)] += (jnp.dot(qmt_ref[h0], dz[h0], preferred_element_type=F32)
                                                 + jnp.dot(qmt_ref[h1], dz[h1], preferred_element_type=F32))
                dv_acc[rows, pl.ds(ks, blk)] += (jnp.dot(dost[h0], wb[h0], preferred_element_type=F32)
                                                 + jnp.dot(dost[h1], wb[h1], preferred_element_type=F32))

        def loop_body(j, carry):
            block(j, False)
            return carry

        lax.fori_loop(0, i, loop_body, 0)
        block(i, True)
        for pr in range(npairs):
            dq = jnp.where(lo_half, dq_acc[2 * pr], dq_acc[2 * pr + 1]) * scale
            dq_ref[:, LANES * pr:LANES * (pr + 1)] = dq.astype(BF16)

        @pl.when(i == nq - 1)
        def _():
            dk_ref[...] = dk_acc[...].T.astype(BF16)
            dv_ref[...] = dv_acc[...].T.astype(BF16)

        @pl.when((pl.program_id(0) == n_steps - 1) & (i == nq - 1))
        def _():
            for dev in range(N_DEV):
                @pl.when(me != dev)
                def _():
                    for a in (1, 2, 3):
                        early(a, dev, dev).wait_recv()
                    for a in _early_arrays(dev):
                        early(a, dev, me).wait_send()

                @pl.when((me != dev) & (me >= _EARLY_IN_DEVS[0]))
                def _():
                    early(0, dev, dev).wait_recv()

    any_spec = pl.BlockSpec(memory_space=pl.ANY)
    return pl.pallas_call(
        body, name="attn_bwd", grid=(n_steps, nq),
        in_specs=[pl.BlockSpec((blk, width), lambda p, i: (i, n_steps * COL_Q + p)),
                  pl.BlockSpec((s, width), lambda p, i: (0, n_steps * COL_K + p)),
                  pl.BlockSpec((s, width), lambda p, i: (0, n_steps * COL_V + p)),
                  pl.BlockSpec((blk, width), lambda p, i: (i, p)),
                  pl.BlockSpec((npairs, blk, LANES), lambda p, i: (p, i, 0))] + [any_spec] * n_g,
        out_specs=[pl.BlockSpec((blk, width), lambda p, i: (i, p)),
                   pl.BlockSpec((s, width), lambda p, i: (0, p)),
                   pl.BlockSpec((s, width), lambda p, i: (0, p))] + [any_spec] * n_g,
        out_shape=[jax.ShapeDtypeStruct((s, D_BRANCH), BF16)] * 3
        + [jax.ShapeDtypeStruct((N_DEV,) + sh, BF16) for sh in _GRAD_PIECE_SHAPES],
        scratch_shapes=[pltpu.VMEM((width, s), F32), pltpu.VMEM((width, s), F32),
                        pltpu.VMEM((2 * npairs, blk, LANES), F32), pltpu.VMEM((2 * npairs, blk, 1), F32),
                        pltpu.VMEM((2 * npairs, LANES, blk), BF16), pltpu.VMEM((2 * npairs, LANES, blk), F32),
                        pltpu.VMEM((npairs, LANES, blk), F32),
                        pltpu.SemaphoreType.DMA((N_DEV * n_g,)), pltpu.SemaphoreType.DMA((N_DEV * n_g,))],
        compiler_params=pltpu.CompilerParams(dimension_semantics=("arbitrary", "arbitrary"),
                                             vmem_limit_bytes=VMEM_LIMIT, has_side_effects=True),
    )(proj, proj, proj, do, rsave, *grads)


def _group_avg_matrix():
    a = lax.broadcasted_iota(jnp.int32, (LANES, LANES), 0) >> GROUP_SHIFT
    b = lax.broadcasted_iota(jnp.int32, (LANES, LANES), 1) >> GROUP_SHIFT
    return jnp.where(a == b, 1.0 / GROUP_DIM, 0.0).astype(BF16)


def _group_mean(a, avg):
    parts = [_split_dot(a[:, LANES * k:LANES * (k + 1)], avg, 3) for k in range(D_BRANCH // LANES)]
    return jnp.concatenate(parts, axis=1)


def _sgu_forward_parts(ub, vb, ln_g, ln_b, avg):
    ug, dug = _gelu_and_grad(ub)
    vg, dvg = _gelu_and_grad(vb)
    mu = _group_mean(vg, avg)
    d = vg - mu
    var = _group_mean(d * d, avg)
    rstd = lax.rsqrt(var + EPS)
    vhat = d * rstd
    vn = vhat * ln_g + ln_b
    return ug, dug, dvg, rstd, vhat, vn


def _sgu_mix(w_ref, src_bf16, n_chunks):
    lane = lax.broadcasted_iota(jnp.int32, (SGU_CHUNK, LANES), 1)
    lo_half = lane < GROUP_DIM
    rows = []
    for n in range(n_chunks):
        slabs = []
        for a in range(D_BRANCH // LANES):
            blk = src_bf16[SGU_CHUNK * n:SGU_CHUNK * (n + 1), LANES * a:LANES * (a + 1)]
            zero = jnp.zeros_like(blk)
            m0 = jnp.dot(w_ref[2 * a], jnp.where(lo_half, blk, zero), preferred_element_type=F32)
            m1 = jnp.dot(w_ref[2 * a + 1], jnp.where(lo_half, zero, blk), preferred_element_type=F32)
            slabs.append(m0 + m1)
        rows.append(jnp.concatenate(slabs, axis=1))
    return jnp.concatenate(rows, axis=0)


def _sgu_fwd(proj, ln_g, ln_b, w_mask, bias_full):
    s = proj.shape[0]
    tm = min(512, s)
    n_chunks = tm // SGU_CHUNK

    def body(ub_ref, vb_ref, zb_ref, g_ref, b_ref, w_ref, bias_ref, yb_ref):
        avg = _group_avg_matrix()
        ug, _, _, _, _, vn = _sgu_forward_parts(ub_ref[...].astype(F32), vb_ref[...].astype(F32),
                                                g_ref[...], b_ref[...], avg)
        mixed = _sgu_mix(w_ref, vn.astype(BF16), n_chunks) + jnp.concatenate([bias_ref[...]] * n_chunks, axis=0)
        zb = zb_ref[...].astype(F32)
        yb_ref[...] = (ug * mixed * (zb * _sigmoid(zb))).astype(BF16)

    col = lambda c: pl.BlockSpec((tm, D_BRANCH), lambda i: (i, c))
    full = lambda shape: pl.BlockSpec(shape, lambda i: (0,) * len(shape))
    return pl.pallas_call(
        body, name="sgu_fwd", grid=(s // tm,),
        in_specs=[col(COL_UB), col(COL_VB), col(COL_ZB), full((1, D_BRANCH)), full((1, D_BRANCH)),
                  full((N_GROUPS, SGU_CHUNK, SGU_CHUNK)), full((SGU_CHUNK, D_BRANCH))],
        out_specs=pl.BlockSpec((tm, D_BRANCH), lambda i: (i, 0)),
        out_shape=jax.ShapeDtypeStruct((s, D_BRANCH), BF16),
        compiler_params=_cparams(("parallel",)),
    )(proj, proj, proj, ln_g, ln_b, w_mask, bias_full)


def _sgu_bwd(proj, dyb, ln_g, ln_b, w_mask, w_mask_t, bias_full):
    s = proj.shape[0]
    tm = min(512, s)
    n_chunks = tm // SGU_CHUNK
    n_steps = s // tm

    def body(ub_ref, vb_ref, zb_ref, dyb_ref, g_ref, b_ref, w_ref, wt_ref, bias_ref,
             dsgu_ref, dw_ref, db_ref, dg_ref, dbeta_ref, dmix_acc):
        i = pl.program_id(0)

        @pl.when(i == 0)
        def _():
            dw_ref[...] = jnp.zeros_like(dw_ref)
            dg_ref[...] = jnp.zeros_like(dg_ref)
            dbeta_ref[...] = jnp.zeros_like(dbeta_ref)
            dmix_acc[...] = jnp.zeros_like(dmix_acc)

        avg = _group_avg_matrix()
        ln_gv = g_ref[...]
        ug, dug, dvg, rstd, vhat, vn = _sgu_forward_parts(ub_ref[...].astype(F32), vb_ref[...].astype(F32),
                                                          ln_gv, b_ref[...], avg)
        vnb = vn.astype(BF16)
        mixed = _sgu_mix(w_ref, vnb, n_chunks) + jnp.concatenate([bias_ref[...]] * n_chunks, axis=0)
        zb = zb_ref[...].astype(F32)
        sg = _sigmoid(zb)
        sz = zb * sg
        dsz = sg * (1.0 + zb * (1.0 - sg))
        dy = dyb_ref[...].astype(F32)
        dmixed = dy * ug * sz
        du = dy * mixed * sz * dug
        dzb = dy * ug * mixed * dsz
        dmb = dmixed.astype(BF16)
        dvn = _sgu_mix(wt_ref, dmb, n_chunks)

        lane = lax.broadcasted_iota(jnp.int32, (SGU_CHUNK, LANES), 1)
        lo_half = lane < GROUP_DIM
        dm_sum = None
        for n in range(n_chunks):
            rows = slice(SGU_CHUNK * n, SGU_CHUNK * (n + 1))
            dm_sum = dmixed[rows] if dm_sum is None else dm_sum + dmixed[rows]
            for a in range(D_BRANCH // LANES):
                cols = slice(LANES * a, LANES * (a + 1))
                dblk = dmb[rows, cols]
                vblk = vnb[rows, cols]
                zero = jnp.zeros_like(dblk)
                dw_ref[2 * a] += _dot_nt(jnp.where(lo_half, dblk, zero), vblk)
                dw_ref[2 * a + 1] += _dot_nt(jnp.where(lo_half, zero, dblk), vblk)
        dmix_acc[...] += dm_sum

        dg_ref[...] += jnp.sum(dvn * vhat, axis=0, keepdims=True)
        dbeta_ref[...] += jnp.sum(dvn, axis=0, keepdims=True)
        dvh = dvn * ln_gv
        m1 = _group_mean(dvh, avg)
        m2 = _group_mean(dvh * vhat, avg)
        dv = rstd * (dvh - m1 - vhat * m2) * dvg
        dsgu_ref[:, 0:D_BRANCH] = du.astype(BF16)
        dsgu_ref[:, D_BRANCH:2 * D_BRANCH] = dv.astype(BF16)
        dsgu_ref[:, 2 * D_BRANCH:3 * D_BRANCH] = dzb.astype(BF16)

        @pl.when(i == n_steps - 1)
        def _():
            pos = lax.broadcasted_iota(jnp.int32, (SGU_CHUNK, SGU_CHUNK), 0) >> GROUP_SHIFT
            src = lax.broadcasted_iota(jnp.int32, (SGU_CHUNK, SGU_CHUNK), 1) >> GROUP_SHIFT
            keep = src <= pos
            for g in range(N_GROUPS):
                dw_ref[g] = jnp.where(keep, dw_ref[g], 0.0)
            grp = lax.broadcasted_iota(jnp.int32, (D_BRANCH, LANES), 0) >> GROUP_SHIFT
            sel = (grp == lax.broadcasted_iota(jnp.int32, (D_BRANCH, LANES), 1)).astype(BF16)
            db_ref[...] = _split_dot(dmix_acc[...], sel, 3)

    col = lambda c: pl.BlockSpec((tm, D_BRANCH), lambda i: (i, c))
    full = lambda shape: pl.BlockSpec(shape, lambda i: (0,) * len(shape))
    return pl.pallas_call(
        body, name="sgu_bwd", grid=(n_steps,),
        in_specs=[col(COL_UB), col(COL_VB), col(COL_ZB), pl.BlockSpec((tm, D_BRANCH), lambda i: (i, 0)),
                  full((1, D_BRANCH)), full((1, D_BRANCH)),
                  full((N_GROUPS, SGU_CHUNK, SGU_CHUNK)), full((N_GROUPS, SGU_CHUNK, SGU_CHUNK)),
                  full((SGU_CHUNK, D_BRANCH))],
        out_specs=[pl.BlockSpec((tm, 3 * D_BRANCH), lambda i: (i, 0)),
                   full((N_GROUPS, SGU_CHUNK, SGU_CHUNK)), full((SGU_CHUNK, LANES)),
                   full((1, D_BRANCH)), full((1, D_BRANCH))],
        out_shape=[jax.ShapeDtypeStruct((s, 3 * D_BRANCH), BF16),
                   jax.ShapeDtypeStruct((N_GROUPS, SGU_CHUNK, SGU_CHUNK), F32),
                   jax.ShapeDtypeStruct((SGU_CHUNK, LANES), F32),
                   jax.ShapeDtypeStruct((1, D_BRANCH), F32), jax.ShapeDtypeStruct((1, D_BRANCH), F32)],
        scratch_shapes=[pltpu.VMEM((SGU_CHUNK, D_BRANCH), F32)],
        compiler_params=_cparams(("arbitrary",)),
    )(proj, proj, proj, dyb, ln_g, ln_b, w_mask, w_mask_t, bias_full)


def _mid(proj, ya, yb, o, x, target, final_g, w_up_a, w_up_b, w_out):
    s = x.shape[0]
    tm = min(256, s)
    n_steps = s // tm
    half = D_MODEL // 2

    def body(ya_ref, yb_ref, o_ref, za_ref, ga0_ref, ga1_ref, gb0_ref, gb1_ref, x_ref, t_ref, gf_ref,
             wa_ref, wb_ref, wo_ref,
             dzg_ref, do_ref, dyb_ref, dx2_ref, gwo_ref, gwa_ref, gwb_ref, loss_ref, dgf_ref,
             acc_o, acc_a, acc_b):
        i = pl.program_id(0)

        @pl.when(i == 0)
        def _():
            acc_o[...] = jnp.zeros_like(acc_o)
            acc_a[...] = jnp.zeros_like(acc_a)
            acc_b[...] = jnp.zeros_like(acc_b)
            loss_ref[...] = jnp.zeros_like(loss_ref)
            dgf_ref[...] = jnp.zeros_like(dgf_ref)

        ya_v = ya_ref[...]
        yb_v = yb_ref[...]
        pa = jnp.dot(ya_v, wa_ref[...], preferred_element_type=F32)
        pb = jnp.dot(yb_v, wb_ref[...], preferred_element_type=F32)
        sa = _sigmoid(jnp.concatenate([ga0_ref[...], ga1_ref[...]], axis=1).astype(F32))
        sb = _sigmoid(jnp.concatenate([gb0_ref[...], gb1_ref[...]], axis=1).astype(F32))
        merged = (sa * pa + sb * pb).astype(BF16)
        x2 = x_ref[...] + jnp.dot(merged, wo_ref[...], preferred_element_type=F32)
        r2 = lax.rsqrt(jnp.mean(x2 * x2, axis=-1, keepdims=True) + EPS)
        xh = x2 * r2
        gf = gf_ref[...]
        diff = xh * gf - t_ref[...]
        loss_ref[...] += 0.5 * jnp.sum(jnp.mean(diff * diff, axis=-1, keepdims=True))
        dy = diff * (1.0 / D_MODEL)
        dgf_ref[...] += jnp.sum(dy * xh, axis=0, keepdims=True)
        dyg = dy * gf
        dx2 = r2 * (dyg - xh * jnp.mean(dyg * xh, axis=-1, keepdims=True))
        dx2_ref[...] = dx2
        dx2b = dx2.astype(BF16)
        dmerged = _dot_nt(dx2b, wo_ref[...])
        acc_o[...] += _dot_tn(merged, dx2b)
        dpa = dmerged * sa
        dpb = dmerged * sb
        dzg_ref[:, D_BRANCH:D_BRANCH + D_MODEL] = (dpa * pa * (1.0 - sa)).astype(BF16)
        dzg_ref[:, D_BRANCH + D_MODEL:D_BRANCH + 2 * D_MODEL] = (dpb * pb * (1.0 - sb)).astype(BF16)
        dpab = dpa.astype(BF16)
        dpbb = dpb.astype(BF16)
        acc_a[...] += _dot_tn(ya_v, dpab)
        acc_b[...] += _dot_tn(yb_v, dpbb)
        dya = _dot_nt(dpab, wa_ref[...])
        dyb_ref[...] = _dot_nt(dpbb, wb_ref[...]).astype(BF16)
        za = za_ref[...].astype(F32)
        sg = _sigmoid(za)
        do_ref[...] = (dya * (za * sg)).astype(BF16)
        dzg_ref[:, 0:D_BRANCH] = (dya * o_ref[...].astype(F32) * (sg * (1.0 + za * (1.0 - sg)))).astype(BF16)

        @pl.when(i == n_steps - 1)
        def _():
            gwo_ref[...] = acc_o[...].astype(BF16)
            gwa_ref[...] = acc_a[...].astype(BF16)
            gwb_ref[...] = acc_b[...].astype(BF16)

    tok = lambda w: pl.BlockSpec((tm, w), lambda i: (i, 0))
    col = lambda c: pl.BlockSpec((tm, half), lambda i: (i, c))
    full = lambda shape: pl.BlockSpec(shape, lambda i: (0,) * len(shape))
    return pl.pallas_call(
        body, name="mid", grid=(n_steps,),
        in_specs=[tok(D_BRANCH), tok(D_BRANCH), tok(D_BRANCH), col(COL_ZA), col(COL_GA), col(COL_GA + 1),
                  col(COL_GB), col(COL_GB + 1), tok(D_MODEL), tok(D_MODEL), full((1, D_MODEL)),
                  full((D_BRANCH, D_MODEL)), full((D_BRANCH, D_MODEL)), full((D_MODEL, D_MODEL))],
        out_specs=[tok(D_BRANCH + 2 * D_MODEL), tok(D_BRANCH), tok(D_BRANCH), tok(D_MODEL),
                   full((D_MODEL, D_MODEL)), full((D_BRANCH, D_MODEL)), full((D_BRANCH, D_MODEL)),
                   full((8, LANES)), full((1, D_MODEL))],
        out_shape=[jax.ShapeDtypeStruct((s, D_BRANCH + 2 * D_MODEL), BF16),
                   jax.ShapeDtypeStruct((s, D_BRANCH), BF16), jax.ShapeDtypeStruct((s, D_BRANCH), BF16),
                   jax.ShapeDtypeStruct((s, D_MODEL), F32),
                   jax.ShapeDtypeStruct((D_MODEL, D_MODEL), BF16),
                   jax.ShapeDtypeStruct((D_BRANCH, D_MODEL), BF16), jax.ShapeDtypeStruct((D_BRANCH, D_MODEL), BF16),
                   jax.ShapeDtypeStruct((8, LANES), F32), jax.ShapeDtypeStruct((1, D_MODEL), F32)],
        scratch_shapes=[pltpu.VMEM((D_MODEL, D_MODEL), F32), pltpu.VMEM((D_BRANCH, D_MODEL), F32),
                        pltpu.VMEM((D_BRANCH, D_MODEL), F32)],
        compiler_params=_cparams(("arbitrary",)),
    )(ya, yb, o, proj, proj, proj, proj, proj, x, target, final_g, w_up_a, w_up_b, w_out)


def _dwin_piece(ht, piece, tile_of, prev):
    s = ht.shape[1]
    n_tiles = piece.shape[1] // D_BRANCH

    def body(ht_ref, p_ref, *rest):
        out_ref = rest[-1]
        out_ref[...] = jnp.dot(ht_ref[...], p_ref[...], preferred_element_type=F32).astype(BF16)

    in_specs = [pl.BlockSpec((D_MODEL, s), lambda j: (0, 0)), pl.BlockSpec((s, D_BRANCH), lambda j: (0, j))]
    args = [ht, piece]
    aliases = {}
    if prev is not None:
        in_specs.append(pl.BlockSpec(memory_space=pl.ANY))
        args.append(prev)
        aliases = {2: 0}
    return pl.pallas_call(
        body, name="dwin_piece", grid=(n_tiles,),
        in_specs=in_specs,
        out_specs=pl.BlockSpec((D_MODEL, D_BRANCH), lambda j: (0, tile_of(j))),
        out_shape=jax.ShapeDtypeStruct((D_MODEL, D_IN), BF16),
        input_output_aliases=aliases,
        compiler_params=_cparams(("parallel",)),
    )(*args)


def _dh_dx(pieces, w_in, x, norm_g, dx2, g_in):
    s = x.shape[0]
    tm = min(256, s)
    n_steps = s // tm
    arrays = []
    for arr, _, _, _ in pieces:
        if not any(arr is a for a in arrays):
            arrays.append(arr)
    n_arr = len(arrays)
    plan = [([k for k, a in enumerate(arrays) if a is arr][0], wcol, off, width) for arr, wcol, off, width in pieces]

    def body(*refs):
        p_refs = refs[:n_arr]
        w_ref, x_ref, g_ref, dx2_ref, gin_ref, dx_ref, dg_ref, late_ref, send_sems, recv_sems = refs[n_arr:]
        x_, y_, c_ = lax.axis_index("x"), lax.axis_index("y"), lax.axis_index("c")
        me = 4 * x_ + 2 * y_ + c_

        def late(dst_dev, src_dev):
            return pltpu.make_async_remote_copy(
                src_ref=_grad_piece(gin_ref, 0, dst_dev), dst_ref=late_ref.at[src_dev],
                send_sem=send_sems.at[dst_dev], recv_sem=recv_sems.at[src_dev],
                device_id=_dev_id(dst_dev), device_id_type=MESH)

        @pl.when(pl.program_id(0) == 0)
        def _():
            dg_ref[...] = jnp.zeros_like(dg_ref)
            for dev in _LATE_IN_DEVS:
                @pl.when(me != dev)
                def _():
                    late(dev, me).start()

        dh = None
        for k, wcol, off, width in plan:
            d = _dot_nt(p_refs[k][:, off:off + width], w_ref[:, wcol:wcol + width])
            dh = d if dh is None else dh + d
        xf = x_ref[...]
        r = lax.rsqrt(jnp.mean(xf * xf, axis=-1, keepdims=True) + EPS)
        xh = xf * r
        dg_ref[...] += jnp.sum(dh * xh, axis=0, keepdims=True)
        dhg = dh * g_ref[...]
        dx_ref[...] = r * (dhg - xh * jnp.mean(dhg * xh, axis=-1, keepdims=True)) + dx2_ref[...]

        @pl.when(pl.program_id(0) == n_steps - 1)
        def _():
            for dev in range(N_DEV):
                @pl.when((me != dev) & (me <= _LATE_IN_DEVS[-1]))
                def _():
                    late(dev, dev).wait_recv()
            for dev in _LATE_IN_DEVS:
                @pl.when(me != dev)
                def _():
                    late(dev, me).wait_send()

    tok = lambda w: pl.BlockSpec((tm, w), lambda i: (i, 0))
    full = lambda shape: pl.BlockSpec(shape, lambda i: (0,) * len(shape))
    any_spec = pl.BlockSpec(memory_space=pl.ANY)
    return pl.pallas_call(
        body, name="dh_dx", grid=(n_steps,),
        in_specs=[tok(a.shape[1]) for a in arrays] + [full((D_MODEL, D_IN)), tok(D_MODEL), full((1, D_MODEL)),
                                                      tok(D_MODEL), any_spec],
        out_specs=[tok(D_MODEL), full((1, D_MODEL)), any_spec],
        out_shape=[jax.ShapeDtypeStruct((s, D_MODEL), F32), jax.ShapeDtypeStruct((1, D_MODEL), F32),
                   jax.ShapeDtypeStruct((N_DEV,) + _GRAD_PIECE_SHAPES[0], BF16)],
        scratch_shapes=[pltpu.SemaphoreType.DMA((N_DEV,)), pltpu.SemaphoreType.DMA((N_DEV,))],
        compiler_params=pltpu.CompilerParams(dimension_semantics=("arbitrary",), vmem_limit_bytes=VMEM_LIMIT,
                                             has_side_effects=True),
    )(*arrays, w_in, x, norm_g, dx2, g_in)


def _adamw(w, g, m, v):
    rows, cols = w.shape
    tr = max(t for t in range(8, 257, 8) if rows % t == 0)
    c1 =1.0 - ADAM_B1 ** ADAM_STEP
    c2 = 1.0 - ADAM_B2 ** ADAM_STEP

    def body(w_ref, g_ref, m_ref, v_ref, g_out_ref, d_ref, nm_ref, nv_ref):
        gv = g_ref[...]
        g_out_ref[...] = gv
        nm = ADAM_B1 * m_ref[...] + (1.0 - ADAM_B1) * gv
        nv = ADAM_B2 * v_ref[...] + (1.0 - ADAM_B2) * (gv * gv)
        d_ref[...] = -ADAM_LR * ((nm / c1) / (jnp.sqrt(nv / c2) + ADAM_EPS) + ADAM_WD * w_ref[...])
        nm_ref[...] = nm
        nv_ref[...] = nv

    spec = pl.BlockSpec((tr, cols), lambda i: (i, 0))
    return pl.pallas_call(
        body, name="adamw", grid=(rows // tr,),
        in_specs=[spec] * 4, out_specs=[spec] * 4,
        out_shape=[jax.ShapeDtypeStruct((rows, cols), F32)] * 4,
        compiler_params=_cparams(("parallel",)),
    )(w, g, m, v)


def _place():
    x, y, c = lax.axis_index("x"), lax.axis_index("y"), lax.axis_index("c")
    return x, y, c


def _gather_weights(w_in, w_up_a, w_up_b, w_out):
    shards = (w_in, w_up_a, w_up_b, w_out)
    n_arr = len(shards)
    col_sharded = (True, True, True, False)
    full_shapes = ((D_MODEL, D_IN), (D_BRANCH, D_MODEL), (D_BRANCH, D_MODEL), (D_MODEL, D_MODEL))

    def body(*refs):
        src = refs[:n_arr]
        out = refs[n_arr:2 * n_arr]
        stage = refs[2 * n_arr:3 * n_arr]
        cast = refs[3 * n_arr:4 * n_arr]
        send_sems, recv_sems, local_sems = refs[4 * n_arr:]
        x, y, c = _place()
        chip = 2 * x + y
        sibling = (x, y, 1 - c)
        others = [(1 - x, y), (x, 1 - y), (1 - x, 1 - y)]

        def region(a, chip_idx, half):
            r, w = shards[a].shape
            hr = r // 2
            if col_sharded[a]:
                return out[a].at[pl.ds(_aligned(half * hr, 16), hr), pl.ds(_aligned(chip_idx * w, LANES), w)]
            return out[a].at[pl.ds(_aligned(chip_idx * r + half * hr, 16), hr), :]

        loads = [pltpu.make_async_copy(src[a], stage[a], local_sems.at[a]) for a in range(n_arr)]
        for cp in loads:
            cp.start()
        for a in range(n_arr):
            loads[a].wait()
            cast[a][...] = stage[a][...].astype(BF16)
        stores = []
        for a in range(n_arr):
            hr = shards[a].shape[0] // 2
            for half in range(2):
                cp = pltpu.make_async_copy(cast[a].at[pl.ds(half * hr, hr), :], region(a, chip, half),
                                           local_sems.at[n_arr + 2 * a + half])
                cp.start()
                stores.append(cp)

        def remote(k, a, chip_idx, half, to, from_vmem):
            hr = shards[a].shape[0] // 2
            s_ref = cast[a].at[pl.ds(_aligned(half * hr, 16), hr), :] if from_vmem else region(a, chip_idx, half)
            return pltpu.make_async_remote_copy(src_ref=s_ref, dst_ref=region(a, chip_idx, half),
                                                send_sem=send_sems.at[k], recv_sem=recv_sems.at[k],
                                                device_id=to, device_id_type=MESH)

        first = []
        for j, (ox, oy) in enumerate(others):
            for a in range(n_arr):
                cp = remote(n_arr * j + a, a, chip, c, (ox, oy, c), True)
                cp.start()
                first.append(cp)
        passed = []
        for j, (ox, oy) in enumerate(others):
            ochip = 2 * ox + oy
            for a in range(n_arr):
                k = n_arr * j + a
                remote(k, a, ochip, c, sibling, False).wait_recv()
                cp = remote(3 * n_arr + k, a, ochip, c, sibling, False)
                cp.start()
                passed.append(cp)
        for j, (ox, oy) in enumerate(others):
            ochip = 2 * ox + oy
            for a in range(n_arr):
                remote(3 * n_arr + n_arr * j + a, a, ochip, 1 - c, sibling, False).wait_recv()
        for cp in first + passed:
            cp.wait_send()
        for cp in stores:
            cp.wait()

    any_spec = pl.BlockSpec(memory_space=pl.ANY)
    return pl.pallas_call(
        body, name="gather_weights",
        in_specs=[any_spec] * n_arr, out_specs=[any_spec] * n_arr,
        out_shape=[jax.ShapeDtypeStruct(sh, BF16) for sh in full_shapes],
        scratch_shapes=[pltpu.VMEM(a.shape, F32) for a in shards] + [pltpu.VMEM(a.shape, BF16) for a in shards]
        + [pltpu.SemaphoreType.DMA((6 * n_arr,)), pltpu.SemaphoreType.DMA((6 * n_arr,)),
           pltpu.SemaphoreType.DMA((3 * n_arr,))],
        compiler_params=pltpu.CompilerParams(vmem_limit_bytes=VMEM_LIMIT, has_side_effects=True),
    )(*shards)


def _reduce_grads_tail(grads, g_small, early_slots, late_in_slots):
    n_big = len(grads)
    n_arr = n_big + 1
    shard_shapes = [(2 * r, w) for r, w in _GRAD_PIECE_SHAPES]
    small_piece = (SMALL_PIECE, LANES)

    def body(*refs):
        src = refs[:n_arr]
        early = refs[n_arr:n_arr + n_big]
        late_in = refs[n_arr + n_big]
        n_in = n_arr + n_big + 1
        out = refs[n_in:n_in + n_arr]
        slots = refs[n_in + n_arr:n_in + 2 * n_arr]
        sums = refs[n_in + 2 * n_arr:n_in + 3 * n_arr]
        send1, recv1, send2, recv2, local_sems = refs[n_in + 3 * n_arr:]
        x, y, c = _place()
        me = 4 * x + 2 * y + c

        def piece_of(a, dev):
            return src[a].at[dev] if a == n_big else _grad_piece(src[a], a, dev)

        def late(a, dst_dev, src_dev):
            return pltpu.make_async_remote_copy(
                src_ref=piece_of(a, dst_dev), dst_ref=slots[a].at[src_dev],
                send_sem=send1.at[n_arr * dst_dev + a], recv_sem=recv1.at[n_arr * src_dev + a],
                device_id=_dev_id(dst_dev), device_id_type=MESH)

        def late_arrays(dev):
            return (n_big,)

        def load(a, dev, received):
            return pltpu.make_async_copy(received.at[dev], slots[a].at[dev], local_sems.at[n_arr * dev + a])

        def own(a, dev):
            return pltpu.make_async_copy(piece_of(a, dev), slots[a].at[dev], local_sems.at[n_arr * dev + a])

        for dev in range(N_DEV):
            @pl.when(me == dev)
            def _():
                received = [early[0] if dev in _EARLY_IN_DEVS else late_in] + list(early[1:])
                for a in range(n_arr):
                    own(a, dev).start()
                for peer in range(N_DEV):
                    if peer != dev:
                        for a in late_arrays(peer):
                            late(a, peer, dev).start()
                        for a in range(n_big):
                            load(a, peer, received[a]).start()
                for a in range(n_arr):
                    own(a, dev).wait()
                for peer in range(N_DEV):
                    if peer != dev:
                        for a in late_arrays(dev):
                            late(a, dev, peer).wait_recv()
                        for a in range(n_big):
                            load(a, peer, received[a]).wait()

        for a in range(n_arr):
            rows = slots[a].shape[1]
            step = 64 if rows % 64 == 0 else 8

            def add_rows(t, carry, a=a, step=step):
                r0 = pl.multiple_of(t * step, step)
                total = slots[a][0, pl.ds(r0, step), :].astype(F32)
                for dev in range(1, N_DEV):
                    total = total + slots[a][dev, pl.ds(r0, step), :].astype(F32)
                sums[a][pl.ds(r0, step), :] = total
                return carry

            lax.fori_loop(0, rows // step, add_rows, 0)

        shares = []
        keeps = []
        for a in range(n_big):
            r, w = _GRAD_PIECE_SHAPES[a]
            dst = out[a].at[pl.ds(pl.multiple_of(c * r, 8), r), :]
            cp = pltpu.make_async_remote_copy(src_ref=sums[a], dst_ref=dst, send_sem=send2.at[a], recv_sem=recv2.at[a],
                                              device_id=(x, y, 1 - c), device_id_type=MESH)
            cp.start()
            shares.append(cp)
            kp = pltpu.make_async_copy(sums[a], dst, local_sems.at[N_DEV * n_arr + a])
            kp.start()
            keeps.append(kp)
        kp = pltpu.make_async_copy(sums[n_big], out[n_big].at[me], local_sems.at[N_DEV * n_arr + n_big])
        kp.start()
        keeps.append(kp)

        def small_share(dst_dev, src_dev):
            return pltpu.make_async_remote_copy(src_ref=sums[n_big], dst_ref=out[n_big].at[src_dev],
                                                send_sem=send2.at[n_big + dst_dev], recv_sem=recv2.at[n_big + src_dev],
                                                device_id=_dev_id(dst_dev), device_id_type=MESH)

        for dev in range(N_DEV):
            @pl.when(me != dev)
            def _():
                small_share(dev, me).start()
        for a in range(n_big):
            r, w = _GRAD_PIECE_SHAPES[a]
            other = out[a].at[pl.ds(pl.multiple_of((1 - c) * r, 8), r), :]
            pltpu.make_async_remote_copy(src_ref=sums[a], dst_ref=other, send_sem=send2.at[a], recv_sem=recv2.at[a],
                                         device_id=(x, y, 1 - c), device_id_type=MESH).wait_recv()
        for dev in range(N_DEV):
            @pl.when(me != dev)
            def _():
                small_share(dev, dev).wait_recv()
                small_share(dev, me).wait_send()
                for a in late_arrays(dev):
                    late(a, dev, me).wait_send()
        for cp in shares:
            cp.wait_send()
        for kp in keeps:
            kp.wait()

    any_spec = pl.BlockSpec(memory_space=pl.ANY)
    return pl.pallas_call(
        body, name="reduce_grads_tail",
        in_specs=[any_spec] * (n_arr + n_big + 1), out_specs=[any_spec] * n_arr,
        out_shape=[jax.ShapeDtypeStruct(sh, F32) for sh in shard_shapes]
        + [jax.ShapeDtypeStruct((N_DEV,) + small_piece, F32)],
        scratch_shapes=[pltpu.VMEM((N_DEV,) + sh, BF16) for sh in _GRAD_PIECE_SHAPES]
        + [pltpu.VMEM((N_DEV,) + small_piece, F32)]
        + [pltpu.VMEM(sh, F32) for sh in _GRAD_PIECE_SHAPES] + [pltpu.VMEM(small_piece, F32)]
        + [pltpu.SemaphoreType.DMA((N_DEV * n_arr,)), pltpu.SemaphoreType.DMA((N_DEV * n_arr,)),
           pltpu.SemaphoreType.DMA((n_big + N_DEV,)), pltpu.SemaphoreType.DMA((n_big + N_DEV,)),
           pltpu.SemaphoreType.DMA((N_DEV * n_arr + n_arr,))],
        compiler_params=pltpu.CompilerParams(vmem_limit_bytes=VMEM_LIMIT, has_side_effects=True),
    )(*grads, g_small, *early_slots, late_in_slots)


def _reduce_grads(g_in, g_up_a, g_up_b, g_out, g_small):
    big = (g_in, g_up_a, g_up_b, g_out)
    n_big = len(big)
    col_sharded = (True, True, True, False)
    piece_shapes = []
    for a, arr in enumerate(big):
        r, w = arr.shape
        piece_shapes.append((r // 2, w // N_CHIPS) if col_sharded[a] else (r // (2 * N_CHIPS), w))
    shard_shapes = [(2 * r, w) for r, w in piece_shapes]
    n_arr = n_big + 1

    def body(*refs):
        src = refs[:n_arr]
        out = refs[n_arr:2 * n_arr]
        slots = refs[2 * n_arr:3 * n_arr]
        sums = refs[3 * n_arr:4 * n_arr]
        send1, recv1, send2, recv2, local_sems = refs[4 * n_arr:]
        x, y, c = _place()
        me = 4 * x + 2 * y + c

        def piece_of(a, dev):
            chip_idx, half = dev // 2, dev % 2
            if a == n_big:
                return src[a].at[dev]
            r, w = piece_shapes[a]
            if col_sharded[a]:
                return src[a].at[pl.ds(pl.multiple_of(half * r, 16), r), pl.ds(pl.multiple_of(chip_idx * w, LANES), w)]
            return src[a].at[pl.ds(pl.multiple_of(dev * r, 16), r), :]

        def dev_id(dev):
            return (dev // 4, (dev // 2) % 2, dev % 2)

        own = [pltpu.make_async_copy(piece_of(a, me), slots[a].at[me], local_sems.at[a]) for a in range(n_arr)]
        for cp in own:
            cp.start()
        sends = []
        for d in range(1, N_DEV):
            peer = (me + d) % N_DEV
            for a in range(n_arr):
                cp = pltpu.make_async_remote_copy(
                    src_ref=piece_of(a, peer), dst_ref=slots[a].at[me],
                    send_sem=send1.at[n_arr * peer + a], recv_sem=recv1.at[n_arr * me + a],
                    device_id=dev_id(peer), device_id_type=MESH)
                cp.start()
                sends.append(cp)
        for cp in own:
            cp.wait()
        for d in range(1, N_DEV):
            peer = (me + d) % N_DEV
            for a in range(n_arr):
                pltpu.make_async_remote_copy(
                    src_ref=piece_of(a, peer), dst_ref=slots[a].at[peer],
                    send_sem=send1.at[n_arr * peer + a], recv_sem=recv1.at[n_arr * peer + a],
                    device_id=dev_id(peer), device_id_type=MESH).wait_recv()
        for a in range(n_arr):
            rows = slots[a].shape[1]
            step = 64 if rows % 64 == 0 else 8

            def add_rows(t, carry, a=a, step=step):
                r0 = pl.multiple_of(t * step, step)
                total = slots[a][0, pl.ds(r0, step), :].astype(F32)
                for dev in range(1, N_DEV):
                    total = total + slots[a][dev, pl.ds(r0, step), :].astype(F32)
                sums[a][pl.ds(r0, step), :] = total
                return carry

            lax.fori_loop(0, rows // step, add_rows, 0)
        shares = []
        keeps = []
        for a in range(n_big):
            r, w = piece_shapes[a]
            dst = out[a].at[pl.ds(pl.multiple_of(c * r, 8), r), :]
            cp = pltpu.make_async_remote_copy(src_ref=sums[a], dst_ref=dst, send_sem=send2.at[a], recv_sem=recv2.at[a],
                                              device_id=(x, y, 1 - c), device_id_type=MESH)
            cp.start()
            shares.append(cp)
            kp = pltpu.make_async_copy(sums[a], dst, local_sems.at[n_arr + a])
            kp.start()
            keeps.append(kp)
        kp = pltpu.make_async_copy(sums[n_big], out[n_big].at[me], local_sems.at[n_arr + n_big])
        kp.start()
        keeps.append(kp)
        for d in range(1, N_DEV):
            peer = (me + d) % N_DEV
            cp = pltpu.make_async_remote_copy(src_ref=sums[n_big], dst_ref=out[n_big].at[me],
                                              send_sem=send2.at[n_big + peer], recv_sem=recv2.at[n_big + me],
                                              device_id=dev_id(peer), device_id_type=MESH)
            cp.start()
            shares.append(cp)
        for a in range(n_big):
            r, w = piece_shapes[a]
            other = out[a].at[pl.ds(pl.multiple_of((1 - c) * r, 8), r), :]
            pltpu.make_async_remote_copy(src_ref=sums[a], dst_ref=other, send_sem=send2.at[a], recv_sem=recv2.at[a],
                                         device_id=(x, y, 1 - c), device_id_type=MESH).wait_recv()
        for d in range(1, N_DEV):
            peer = (me + d) % N_DEV
            pltpu.make_async_remote_copy(src_ref=sums[n_big], dst_ref=out[n_big].at[peer],
                                         send_sem=send2.at[n_big + peer], recv_sem=recv2.at[n_big + peer],
                                         device_id=dev_id(peer), device_id_type=MESH).wait_recv()
        for cp in sends + shares:
            cp.wait_send()
        for kp in keeps:
            kp.wait()

    any_spec = pl.BlockSpec(memory_space=pl.ANY)
    small_piece = (SMALL_PIECE, LANES)
    return pl.pallas_call(
        body, name="reduce_grads",
        in_specs=[any_spec] * n_arr, out_specs=[any_spec] * n_arr,
        out_shape=[jax.ShapeDtypeStruct(sh, F32) for sh in shard_shapes]
        + [jax.ShapeDtypeStruct((N_DEV,) + small_piece, F32)],
        scratch_shapes=[pltpu.VMEM((N_DEV,) + sh, BF16) for sh in piece_shapes]
        + [pltpu.VMEM((N_DEV,) + small_piece, F32)]
        + [pltpu.VMEM(sh, F32) for sh in piece_shapes] + [pltpu.VMEM(small_piece, F32)]
        + [pltpu.SemaphoreType.DMA((N_DEV * n_arr,)), pltpu.SemaphoreType.DMA((N_DEV * n_arr,)),
           pltpu.SemaphoreType.DMA((n_big + N_DEV,)), pltpu.SemaphoreType.DMA((n_big + N_DEV,)),
           pltpu.SemaphoreType.DMA((2 * n_arr,))],
        compiler_params=pltpu.CompilerParams(vmem_limit_bytes=VMEM_LIMIT, has_side_effects=True),
    )(*big, g_small)


_SMALL_PARTS = (("norm_g", 8), ("sgu_ln_g", 8), ("sgu_ln_b", 8), ("w_spatial", 1024), ("b_spatial", 8),
                ("final_norm_g", 8))
_LOSS_ROW = sum(n for _, n in _SMALL_PARTS)


def _pack_small(parts, loss_tile=None):
    rows = []
    for name, n_rows in _SMALL_PARTS:
        a = parts[name].reshape(-1, LANES).astype(F32)
        a = jnp.pad(a, ((0, n_rows - a.shape[0]), (0, 0)))
        rows.append(a)
    rows.append(jnp.zeros((8, LANES), F32) if loss_tile is None else loss_tile)
    rows.append(jnp.zeros((SMALL_ROWS - _LOSS_ROW - 8, LANES), F32))
    return jnp.concatenate(rows, axis=0)


def _unpack_small(packed, shapes):
    out = {}
    r0 = 0
    for name, n_rows in _SMALL_PARTS:
        n = math.prod(shapes[name])
        out[name] = packed[r0:r0 + n // LANES].reshape(shapes[name])
        r0 += n_rows
    return out


def _local_step(proj, ht, x, target, norm_g, w_in, sgu_ln_g, sgu_ln_b, w_spatial, b_spatial, w_up_a, w_up_b, w_out,
                final_norm_g, blk):
    pos = jnp.arange(SGU_CHUNK)
    keep = (pos[None, :] // SGU_SUBCHUNK) <= (pos[:, None] // SGU_SUBCHUNK)
    w_mask = jnp.where(keep[None], w_spatial, 0.0).astype(BF16)
    w_mask_t = jnp.swapaxes(w_mask, 1, 2)
    bias_full = jnp.repeat(b_spatial.T, GROUP_DIM, axis=1)
    ln_g = sgu_ln_g.reshape(1, D_BRANCH)
    ln_b = sgu_ln_b.reshape(1, D_BRANCH)
    final_g = final_norm_g.reshape(1, D_MODEL)

    o, ya, rsave = _attn_fwd(proj, blk, ATTN_PAIRS)
    yb = _sgu_fwd(proj, ln_g, ln_b, w_mask, bias_full)
    dzg, do, dyb, dx2, g_out, g_up_a, g_up_b, loss_acc, d_final = _mid(
        proj, ya, yb, o, x, target, final_g, w_up_a, w_up_b, w_out)
    dsgu, d_wsp, d_bsp, d_lng, d_lnb = _sgu_bwd(proj, dyb, ln_g, ln_b, w_mask, w_mask_t, bias_full)
    g_in = _dwin_piece(ht, dzg, lambda j: jnp.where(j == 0, COL_ZA, COL_GA - 1 + j), None)
    g_in = _dwin_piece(ht, dsgu, lambda j: COL_UB + j, g_in)
    dq, dk, dv, *early_slots = _attn_bwd(proj, do, rsave, blk, ATTN_PAIRS, (g_in, g_up_a, g_up_b, g_out))
    g_in = _dwin_piece(ht, dq, lambda j: COL_Q + j, g_in)
    g_in = _dwin_piece(ht, dk, lambda j: COL_K + j, g_in)
    g_in = _dwin_piece(ht, dv, lambda j: COL_V + j, g_in)
    pieces = [(dq, COL_Q * D_BRANCH, 0, D_BRANCH), (dk, COL_K * D_BRANCH, 0, D_BRANCH),
              (dv, COL_V * D_BRANCH, 0, D_BRANCH), (dzg, COL_ZA * D_BRANCH, 0, D_BRANCH),
              (dsgu, COL_UB * D_BRANCH, 0, 3 * D_BRANCH), (dzg, COL_GA * D_BRANCH, D_BRANCH, 2 * D_MODEL)]
    dx, d_norm, late_in_slots = _dh_dx(pieces, w_in, x, norm_g, dx2, g_in)
    small = {"norm_g": d_norm, "sgu_ln_g": d_lng, "sgu_ln_b": d_lnb, "w_spatial": d_wsp,
             "b_spatial": d_bsp[:, :N_GROUPS].T, "final_norm_g": d_final}
    return loss_acc, dx, (g_in, g_up_a, g_up_b, g_out), small, early_slots, late_in_slots


def kernel(x, norm_g, w_in, sgu_ln_g, sgu_ln_b, w_spatial, b_spatial, w_up_a, w_up_b, w_out, final_norm_g, loss_target, m_norm_g, m_w_in, m_sgu_ln_g, m_sgu_ln_b, m_w_spatial, m_b_spatial, m_w_up_a, m_w_up_b, m_w_out, m_final_norm_g, v_norm_g, v_w_in, v_sgu_ln_g, v_sgu_ln_b, v_w_spatial, v_b_spatial, v_w_up_a, v_w_up_b, v_w_out, v_final_norm_g):
    big_names = ("w_in", "w_up_a", "w_up_b", "w_out")
    small_names = tuple(n for n, _ in _SMALL_PARTS)
    names = ("norm_g", "w_in", "sgu_ln_g", "sgu_ln_b", "w_spatial", "b_spatial", "w_up_a", "w_up_b", "w_out",
             "final_norm_g")
    w = dict(norm_g=norm_g, w_in=w_in, sgu_ln_g=sgu_ln_g, sgu_ln_b=sgu_ln_b, w_spatial=w_spatial,
             b_spatial=b_spatial, w_up_a=w_up_a, w_up_b=w_up_b, w_out=w_out, final_norm_g=final_norm_g)
    m = dict(norm_g=m_norm_g, w_in=m_w_in, sgu_ln_g=m_sgu_ln_g, sgu_ln_b=m_sgu_ln_b, w_spatial=m_w_spatial,
             b_spatial=m_b_spatial, w_up_a=m_w_up_a, w_up_b=m_w_up_b, w_out=m_w_out, final_norm_g=m_final_norm_g)
    v = dict(norm_g=v_norm_g, w_in=v_w_in, sgu_ln_g=v_sgu_ln_g, sgu_ln_b=v_sgu_ln_b, w_spatial=v_w_spatial,
             b_spatial=v_b_spatial, w_up_a=v_w_up_a, w_up_b=v_w_up_b, w_out=v_w_out, final_norm_g=v_final_norm_g)
    shapes = {n: w[n].shape for n in names}
    flat2d = lambda a: a.reshape(a.shape[-2:])

    proj, ht, *full = _in_proj_gather(x[0], norm_g, *[flat2d(w[n]) for n in big_names])
    loss, dx, big_grads, small, early_slots, late_in_slots = _local_step(
        proj, ht, x[0], loss_target[0], norm_g, full[0], sgu_ln_g[0], sgu_ln_b[0], w_spatial[0], b_spatial[0],
        full[1], full[2], full[3], final_norm_g, ATTN_BLOCK)
    packed = _pack_small(small, loss).reshape(N_DEV, SMALL_PIECE, LANES)
    red = _reduce_grads_tail(big_grads, packed, early_slots, late_in_slots)

    grads, deltas, new_m, new_v = {}, {}, {}, {}
    for n, g in zip(big_names, red[:4]):
        g, d, nm, nv = _adamw(flat2d(w[n]), g, flat2d(m[n]), flat2d(v[n]))
        grads[n], deltas[n], new_m[n], new_v[n] = (a.reshape(shapes[n]) for a in (g, d, nm, nv))
    g_small = red[4].reshape(SMALL_ROWS, LANES)
    g_small, d, nm, nv = _adamw(_pack_small({n: w[n] for n in small_names}), g_small,
                                _pack_small({n: m[n] for n in small_names}),
                                _pack_small({n: v[n] for n in small_names}))
    for src, dst in ((g_small, grads), (d, deltas), (nm, new_m), (nv, new_v)):
        dst.update(_unpack_small(src, shapes))

    return (g_small[_LOSS_ROW, 0], dx[None], *[grads[n] for n in names], *[deltas[n] for n in names],
            *[new_m[n] for n in names], *[new_v[n] for n in names])
```

```python
import functools
import math

import jax
import jax.numpy as jnp
from jax import lax
from jax.experimental import pallas as pl
from jax.experimental.pallas import tpu as pltpu

F32 = jnp.float32
BF16 = jnp.bfloat16

D_MODEL = 1024
N_HEADS = 8
HEAD_DIM = 64
D_BRANCH = 512
D_IN = 4 * D_BRANCH + 3 * D_BRANCH + 2 * D_MODEL
N_GROUPS = 8
GROUP_DIM = 64
SGU_CHUNK = 128
SGU_SUBCHUNK = 64
GROUP_SHIFT = 6
EPS = 1e-6
LANES = 128
ATTN_Q_BLOCK = 256
ATTN_K_BLOCK = 256
DEAD = -110.0
SKIPPED = -1e30
SCAN_PASSES = 1
ATTN_PAIRS = 2
N_CHIPS = 4
N_DEV = 8
MESH = pl.DeviceIdType.MESH

ADAM_LR = 0.001
ADAM_B1 = 0.9
ADAM_B2 = 0.999
ADAM_EPS = 1e-08
ADAM_WD = 0.01
ADAM_STEP = 10

COL_Q, COL_K, COL_V, COL_ZA, COL_UB, COL_VB, COL_ZB, COL_GA, COL_GB = 0, 1, 2, 3, 4, 5, 6, 7, 9

VMEM_LIMIT = 56 * 1024 * 1024

SMALL_ROWS = 1088
SMALL_PIECE = SMALL_ROWS // N_DEV


def _cparams(sem=None):
    return pltpu.CompilerParams(dimension_semantics=sem, vmem_limit_bytes=VMEM_LIMIT)


def _aligned(v, m):
    return v if isinstance(v, int) else pl.multiple_of(v, m)


def _sigmoid(x):
    return 1.0 / (1.0 + jnp.exp(-x))


def _gelu_and_grad(x):
    k = math.sqrt(2.0 / math.pi)
    x2 = x * x
    inner = k * (x + 0.044715 * x * x2)
    th = jnp.tanh(inner)
    g = 0.5 * x * (1.0 + th)
    dg = 0.5 * (1.0 + th) + 0.5 * x * (1.0 - th * th) * (k * (1.0 + 3.0 * 0.044715 * x2))
    return g, dg


def _split_dot(a, b_bf16, passes):
    out = None
    rem = a
    for _ in range(passes):
        part = rem.astype(BF16)
        d = jnp.dot(part, b_bf16, preferred_element_type=F32)
        out = d if out is None else out + d
        rem = rem - part.astype(F32)
    return out


def _dot_nt(a, b):
    return lax.dot_general(a, b, (((1,), (1,)), ((), ())), preferred_element_type=F32)


def _dot_tn(a, b):
    return lax.dot_general(a, b, (((0,), (0,)), ((), ())), preferred_element_type=F32)


def _place():
    x, y, c = lax.axis_index("x"), lax.axis_index("y"), lax.axis_index("c")
    return x, y, c


def _in_proj_gather(x, norm_g, w_in, w_up_a, w_up_b, w_out):
    s = x.shape[0]
    tm = min(512, s)
    nt = s // tm
    shards = (w_in, w_up_a, w_up_b, w_out)
    n_arr = len(shards)
    col_sharded = (True, True, True, False)
    full_shapes = ((D_MODEL, D_IN), (D_BRANCH, D_MODEL), (D_BRANCH, D_MODEL), (D_MODEL, D_MODEL))
    w_shard = w_in.shape[1]
    half_rows = D_MODEL // 2
    stage_rows = 256

    def body(order_ref, x_ref, g_ref, *refs):
        src = refs[:n_arr]
        proj_ref, ht_ref = refs[n_arr:n_arr + 2]
        out = refs[n_arr + 2:2 * n_arr + 2]
        wsc, h_scr, stage = refs[2 * n_arr + 2:2 * n_arr + 5]
        small_stage = refs[2 * n_arr + 5:2 * n_arr + 8]
        small_cast = refs[2 * n_arr + 8:2 * n_arr + 11]
        send_sems, recv_sems, local_sems = refs[2 * n_arr + 11:]
        k = pl.program_id(0)
        i = pl.program_id(1)
        x_, y_, c = _place()
        chip = 2 * x_ + y_
        sibling = (x_, y_, 1 - c)
        others = [(x_, 1 - y_), (1 - x_, y_), (1 - x_, 1 - y_)]

        def region(a, chip_idx, half):
            if a == 0:
                return wsc.at[chip_idx, pl.ds(_aligned(half * half_rows, 16), half_rows), :]
            r, w = shards[a].shape
            hr = r // 2
            if col_sharded[a]:
                return out[a].at[pl.ds(_aligned(half * hr, 16), hr), pl.ds(_aligned(chip_idx * w, LANES), w)]
            return out[a].at[pl.ds(_aligned(chip_idx * r + half * hr, 16), hr), :]

        def remote(kk, a, chip_idx, half, to, own):
            s_ref = region(a, chip_idx, half)
            if own and a > 0:
                hr = shards[a].shape[0] // 2
                s_ref = small_cast[a - 1].at[pl.ds(_aligned(half * hr, 16), hr), :]
            return pltpu.make_async_remote_copy(src_ref=s_ref, dst_ref=region(a, chip_idx, half),
                                                send_sem=send_sems.at[kk], recv_sem=recv_sems.at[kk],
                                                device_id=to, device_id_type=MESH)

        def keep_whole(kk, chip_idx):
            return pltpu.make_async_copy(wsc.at[chip_idx],
                                         out[0].at[:, pl.ds(_aligned(chip_idx * w_shard, LANES), w_shard)],
                                         local_sems.at[kk])

        def small_stores():
            cps = []
            for a in range(1, n_arr):
                hr = shards[a].shape[0] // 2
                for half in range(2):
                    cps.append(pltpu.make_async_copy(small_cast[a - 1].at[pl.ds(half * hr, hr), :],
                                                     region(a, chip, half), local_sems.at[4 + 2 * (a - 1) + half]))
            return cps

        def arrive_and_pass(j):
            ochip = chip ^ j
            for a in range(n_arr):
                kk = n_arr * (j - 1) + a
                remote(kk, a, ochip, c, sibling, False).wait_recv()
                remote(3 * n_arr + kk, a, ochip, c, sibling, False).start()

        def from_sibling(j, a):
            remote(3 * n_arr + n_arr * (j - 1) + a, a, chip ^ j, 1 - c, sibling, False).wait_recv()

        @pl.when((k == 0) & (i == 0))
        def _():
            def cast_rows(half):
                for t in range(half_rows // stage_rows):
                    r0 = pl.multiple_of(half * half_rows + t * stage_rows, stage_rows)
                    pltpu.sync_copy(src[0].at[pl.ds(r0, stage_rows), :], stage)
                    wsc[chip, pl.ds(r0, stage_rows), :] = stage[...].astype(BF16)

            cast_rows(c)
            for j in (1, 2):
                remote(n_arr * (j - 1), 0, chip, c, (*others[j - 1], c), True).start()
            cast_rows(1 - c)
            for a in range(1, n_arr):
                pltpu.sync_copy(src[a], small_stage[a - 1])
                small_cast[a - 1][...] = small_stage[a - 1][...].astype(BF16)
            for j in (1, 2):
                for a in range(1, n_arr):
                    remote(n_arr * (j - 1) + a, a, chip, c, (*others[j - 1], c), True).start()
            keep_whole(0, chip).start()
            for cp in small_stores():
                cp.start()

        @pl.when((k == 1) & (i == 0))
        def _():
            for j in (1, 2):
                remote(n_arr * (j - 1), 0, chip, c, (*others[j - 1], c), True).wait_send()
            for a in range(n_arr):
                remote(n_arr * 2 + a, a, chip, c, (*others[2], c), True).start()
            arrive_and_pass(1)
            arrive_and_pass(2)
            from_sibling(1, 0)
            keep_whole(1, chip ^ 1).start()

        @pl.when((k == 2) & (i == 0))
        def _():
            from_sibling(2, 0)
            keep_whole(2, chip ^ 2).start()
            arrive_and_pass(3)

        @pl.when((k == 3) & (i == 0))
        def _():
            from_sibling(3, 0)
            keep_whole(3, chip ^ 3).start()

        @pl.when(k == 0)
        def _():
            xf = x_ref[...]
            r = lax.rsqrt(jnp.mean(xf * xf, axis=-1, keepdims=True) + EPS)
            h = xf * r * g_ref[...]
            h_scr[i] = h.astype(BF16)
            ht_ref[...] = h.T.astype(BF16)

        proj_ref[...] = jnp.dot(h_scr[i], wsc[order_ref[k]], preferred_element_type=F32).astype(BF16)

        @pl.when((k == 3) & (i == nt - 1))
        def _():
            for j in (1, 2, 3):
                for a in range(1, n_arr):
                    from_sibling(j, a)
            for j in (1, 2, 3):
                for a in range(n_arr):
                    kk = n_arr * (j - 1) + a
                    if a > 0 or j == 3:
                        remote(kk, a, chip, c, (*others[j - 1], c), True).wait_send()
                    remote(3 * n_arr + kk, a, chip ^ j, c, sibling, False).wait_send()
            for kk in range(4):
                keep_whole(kk, chip ^ kk).wait()
            for cp in small_stores():
                cp.wait()

    any_spec = pl.BlockSpec(memory_space=pl.ANY)
    tile = lambda kk, ii: jnp.where(kk == 0, ii, nt - 1)
    grid_spec = pltpu.PrefetchScalarGridSpec(
        num_scalar_prefetch=1, grid=(N_CHIPS, nt),
        in_specs=[pl.BlockSpec((tm, D_MODEL), lambda kk, ii, order: (tile(kk, ii), 0)),
                  pl.BlockSpec((1, D_MODEL), lambda kk, ii, order: (0, 0))] + [any_spec] * n_arr,
        out_specs=[pl.BlockSpec((tm, w_shard), lambda kk, ii, order: (ii, order[kk])),
                   pl.BlockSpec((D_MODEL, tm), lambda kk, ii, order: (0, tile(kk, ii)))] + [any_spec] * n_arr,
        scratch_shapes=[pltpu.VMEM((N_CHIPS, D_MODEL, w_shard), BF16), pltpu.VMEM((nt, tm, D_MODEL), BF16),
                        pltpu.VMEM((stage_rows, w_shard), F32)]
        + [pltpu.VMEM(a.shape, F32) for a in shards[1:]] + [pltpu.VMEM(a.shape, BF16) for a in shards[1:]]
        + [pltpu.SemaphoreType.DMA((6 * n_arr,)), pltpu.SemaphoreType.DMA((6 * n_arr,)),
           pltpu.SemaphoreType.DMA((4 + 2 * (n_arr - 1),))])
    x_, y_, _ = _place()
    order = (2 * x_ + y_) ^ jnp.arange(N_CHIPS, dtype=jnp.int32)
    return pl.pallas_call(
        body, name="in_proj_gather", grid_spec=grid_spec,
        out_shape=[jax.ShapeDtypeStruct((s, D_IN), BF16), jax.ShapeDtypeStruct((D_MODEL, s), BF16)]
        + [jax.ShapeDtypeStruct(sh, BF16) for sh in full_shapes],
        compiler_params=pltpu.CompilerParams(dimension_semantics=("arbitrary", "arbitrary"),
                                             vmem_limit_bytes=VMEM_LIMIT, has_side_effects=True),
    )(order, x, norm_g, *shards)


def _neg_softplus_parts(z):
    zb = z.astype(BF16)
    p = jnp.exp(-jnp.abs(zb))
    return p, jnp.maximum(zb, jnp.zeros_like(zb)) + jnp.log(1.0 + p)


def _split_cat(a, passes):
    parts = []
    rem = a
    for k in range(passes):
        part = rem.astype(BF16)
        parts.append(part)
        if k + 1 < passes:
            rem = rem - part.astype(F32)
    return parts[0] if passes == 1 else jnp.concatenate(parts, axis=1)


def _tri(blk, upper, sign):
    row = lax.broadcasted_iota(jnp.int32, (blk, blk), 0)
    col = lax.broadcasted_iota(jnp.int32, (blk, blk), 1)
    keep = (row <= col) if upper else (row >= col)
    t = jnp.where(keep, sign, 0.0).astype(BF16)
    return t if SCAN_PASSES == 1 else jnp.concatenate([t] * SCAN_PASSES, axis=0)


def _attn_fwd(proj, bq, bk, npairs):
    s = proj.shape[0]
    nq = s // bq
    ratio = bq // bk
    scale = HEAD_DIM ** -0.5
    heads = tuple(range(2 * npairs))
    width = LANES * npairs

    def body(q_ref, k_ref, v_ref, za_ref, o_ref, ya_ref, rs_ref, acc_ref, r_ref, z_ref):
        i = pl.program_id(1)
        lane = lax.broadcasted_iota(jnp.int32, (bq, LANES), 1)
        lo_half = lane < HEAD_DIM
        qm = []
        for pr in range(npairs):
            q = q_ref[:, LANES * pr:LANES * (pr + 1)] * jnp.asarray(scale, BF16)
            zero = jnp.zeros_like(q)
            qm += [jnp.where(lo_half, q, zero), jnp.where(lo_half, zero, q)]
        row = lax.broadcasted_iota(jnp.int32, (bq, bk), 0)
        col = lax.broadcasted_iota(jnp.int32, (bq, bk), 1)
        tneg = _tri(bk, False, -1.0)
        acc_ref[...] = jnp.zeros_like(acc_ref)
        r_ref[...] = jnp.zeros_like(r_ref)
        rs_ref[...] = jnp.full_like(rs_ref, SKIPPED)

        def scores(j):
            ks = pl.multiple_of(j * bk, bk)
            return [_dot_nt(qm[h], k_ref[pl.ds(ks, bk), LANES * (h // 2):LANES * (h // 2 + 1)]) for h in heads]

        for h, zh in enumerate(scores(i * ratio + ratio - 1)):
            z_ref[h] = zh

        def block(j, diag):
            ks = pl.multiple_of(j * bk, bk)
            vj = [v_ref[pl.ds(ks, bk), LANES * pr:LANES * (pr + 1)] for pr in range(npairs)]
            if diag:
                before = (j * bk + col) < (i * bq + row)
            z = [z_ref[h] for h in heads]
            sp = [_neg_softplus_parts(z[h])[1] for h in heads]
            if diag:
                sp = [jnp.where(before, sp[h], 0.0) for h in heads]
            cin = [jnp.dot(_split_cat(sp[h], SCAN_PASSES), tneg, preferred_element_type=F32) for h in heads]
            for h, zh in enumerate(scores(jnp.maximum(j - 1, 0))):
                z_ref[h] = zh
            w = [jnp.exp(z[h] + cin[h]) for h in heads]
            if diag:
                w = [jnp.where(before, w[h], 0.0) for h in heads]
            pv = [jnp.dot(w[h].astype(BF16), vj[h // 2], preferred_element_type=F32) for h in heads]
            r = [r_ref[h] for h in heads]
            for h in heads:
                acc_ref[h] += pv[h] * jnp.exp(r[h])
                r_ref[h] = r[h] + cin[h][:, 0:1]
            for pr in range(npairs):
                rs_ref[pr] = jnp.where(lane == j, r[2 * pr], jnp.where(lane == j + HEAD_DIM, r[2 * pr + 1], rs_ref[pr]))

        for t in range(ratio):
            block(i * ratio + ratio - 1 - t, True)

        def alive(carry):
            jj, r_max = carry
            return (jj < i * ratio) & (r_max > DEAD)

        def loop_body(carry):
            jj, _ = carry
            block(i * ratio - 1 - jj, False)
            return jj + 1, jnp.max(r_ref[...])

        lax.while_loop(alive, loop_body, (0, jnp.max(r_ref[...])))
        for pr in range(npairs):
            cols = slice(LANES * pr, LANES * (pr + 1))
            o = jnp.where(lo_half, acc_ref[2 * pr], acc_ref[2 * pr + 1])
            o_ref[:, cols] = o.astype(BF16)
            za = za_ref[:, cols].astype(F32)
            ya_ref[:, cols] = (o * (za * _sigmoid(za))).astype(BF16)

    n_steps = N_HEADS // (2 * npairs)
    return pl.pallas_call(
        body, name="attn_fwd", grid=(n_steps, nq),
        in_specs=[pl.BlockSpec((bq, width), lambda p, i: (i, n_steps * COL_Q + p)),
                  pl.BlockSpec((s, width), lambda p, i: (0, n_steps * COL_K + p)),
                  pl.BlockSpec((s, width), lambda p, i: (0, n_steps * COL_V + p)),
                  pl.BlockSpec((bq, width), lambda p, i: (i, n_steps * COL_ZA + p))],
        out_specs=[pl.BlockSpec((bq, width), lambda p, i: (i, p)),
                   pl.BlockSpec((bq, width), lambda p, i: (i, p)),
                   pl.BlockSpec((npairs, bq, LANES), lambda p, i: (p, i, 0))],
        out_shape=[jax.ShapeDtypeStruct((s, D_BRANCH), BF16), jax.ShapeDtypeStruct((s, D_BRANCH), BF16),
                   jax.ShapeDtypeStruct((N_HEADS // 2, s, LANES), F32)],
        scratch_shapes=[pltpu.VMEM((2 * npairs, bq, LANES), F32), pltpu.VMEM((2 * npairs, bq, 1), F32),
                        pltpu.VMEM((2 * npairs, bq, bk), F32)],
        compiler_params=_cparams(("parallel", "parallel")),
    )(proj, proj, proj, proj)


_GRAD_COL_SHARDED = (True, True, True, False)
_GRAD_FULL_SHAPES = ((D_MODEL, D_IN), (D_BRANCH, D_MODEL), (D_BRANCH, D_MODEL), (D_MODEL, D_MODEL))
_GRAD_PIECE_SHAPES = tuple((r // 2, w // N_CHIPS) if cs else (r // (2 * N_CHIPS), w)
                           for (r, w), cs in zip(_GRAD_FULL_SHAPES, _GRAD_COL_SHARDED))
_EARLY_IN_DEVS = (4, 5, 6, 7)
_LATE_IN_DEVS = (0, 1, 2, 3)


def _grad_piece(ref, a, dev):
    r, w = _GRAD_PIECE_SHAPES[a]
    if _GRAD_COL_SHARDED[a]:
        return ref.at[pl.ds((dev % 2) * r, r), pl.ds((dev // 2) * w, w)]
    return ref.at[pl.ds(dev * r, r), :]


def _dev_id(dev):
    return (dev // 4, (dev // 2) % 2, dev % 2)


def _early_arrays(dev):
    return (0, 1, 2, 3) if dev in _EARLY_IN_DEVS else (1, 2, 3)


def _attn_bwd(proj, do, rsave, bq, bk, npairs, grads):
    s = proj.shape[0]
    nq = s // bq
    ratio = bq // bk
    scale = HEAD_DIM ** -0.5
    heads = tuple(range(2 * npairs))
    width = LANES * npairs

    n_steps = N_HEADS // (2 * npairs)
    n_g = len(grads)

    def body(q_ref, k_ref, v_ref, do_ref, rs_ref, *refs):
        g_src = refs[:n_g]
        dq_ref, dk_ref, dv_ref = refs[n_g:n_g + 3]
        g_slots = refs[n_g + 3:2 * n_g + 3]
        dk_acc, dv_acc, dq_acc, e_ref, qmt_ref, domt_ref, rst_ref, send_sems, recv_sems = refs[2 * n_g + 3:]
        i = pl.program_id(1)
        x_, y_, c_ = lax.axis_index("x"), lax.axis_index("y"), lax.axis_index("c")
        me = 4 * x_ + 2 * y_ + c_

        def early(a, dst_dev, src_dev):
            return pltpu.make_async_remote_copy(
                src_ref=_grad_piece(g_src[a], a, dst_dev), dst_ref=g_slots[a].at[src_dev],
                send_sem=send_sems.at[n_g * dst_dev + a], recv_sem=recv_sems.at[n_g * src_dev + a],
                device_id=_dev_id(dst_dev), device_id_type=MESH)

        @pl.when((pl.program_id(0) == 0) & (i == 0))
        def _():
            for dev in range(N_DEV):
                @pl.when(me != dev)
                def _():
                    for a in _early_arrays(dev):
                        early(a, dev, me).start()

        lane = lax.broadcasted_iota(jnp.int32, (bq, LANES), 1)
        lo_half = lane < HEAD_DIM
        qm, dom = [], []
        for pr in range(npairs):
            cols = slice(LANES * pr, LANES * (pr + 1))
            q = q_ref[:, cols] * jnp.asarray(scale, BF16)
            zero = jnp.zeros_like(q)
            qm += [jnp.where(lo_half, q, zero), jnp.where(lo_half, zero, q)]
            dout = do_ref[:, cols].astype(F32)
            dom += [jnp.where(lo_half, dout, 0.0), jnp.where(lo_half, 0.0, dout)]
        row = lax.broadcasted_iota(jnp.int32, (bq, bk), 0)
        col = lax.broadcasted_iota(jnp.int32, (bq, bk), 1)
        tneg = _tri(bk, False, -1.0)
        tfwd = _tri(bk, True, 1.0)

        @pl.when(i == 0)
        def _():
            dk_acc[...] = jnp.zeros_like(dk_acc)
            dv_acc[...] = jnp.zeros_like(dv_acc)

        dq_acc[...] = jnp.zeros_like(dq_acc)
        e_ref[...] = jnp.zeros_like(e_ref)
        for h in heads:
            qmt_ref[h] = qm[h].astype(F32).T.astype(BF16)
            domt_ref[h] = dom[h].T
        for pr in range(npairs):
            rst_ref[pr] = rs_ref[pr].T

        def block(j, diag):
            ks = pl.multiple_of(j * bk, bk)
            kj = [k_ref[pl.ds(ks, bk), LANES * pr:LANES * (pr + 1)] for pr in range(npairs)]
            vj = [v_ref[pl.ds(ks, bk), LANES * pr:LANES * (pr + 1)] for pr in range(npairs)]
            if diag:
                before = (j * bk + col) < (i * bq + row)
            z = [_dot_nt(qm[h], kj[h // 2]) for h in heads]
            dost = [(domt_ref[h] * jnp.exp(rst_ref[h // 2, pl.ds(j + HEAD_DIM * (h % 2), 1), :])).astype(BF16)
                    for h in heads]
            er = [jnp.exp(jnp.sum(jnp.where(lane == j + HEAD_DIM * (h % 2), rs_ref[h // 2], 0.0), axis=-1,
                                  keepdims=True)) for h in heads]
            dos = [(dom[h] * er[h]).astype(BF16) for h in heads]
            dw = [_dot_nt(dos[h], vj[h // 2]) for h in heads]
            psp = [_neg_softplus_parts(z[h]) for h in heads]
            sp = [psp[h][1] for h in heads]
            if diag:
                sp = [jnp.where(before, sp[h], 0.0) for h in heads]
            cin = [jnp.dot(_split_cat(sp[h], SCAN_PASSES), tneg, preferred_element_type=F32) for h in heads]
            w = [jnp.exp(z[h] + cin[h]) for h in heads]
            if diag:
                w = [jnp.where(before, w[h], 0.0) for h in heads]
            e =[dw[h] * w[h] for h in heads]
            eincl = [jnp.dot(_split_cat(e[h], SCAN_PASSES), tfwd, preferred_element_type=F32) + e_ref[h]
                     for h in heads]
            dz = []
            for h in heads:
                p = psp[h][0]
                beta = jnp.where(z[h] >= 0.0, 1.0, p) / (1.0 + p)
                d = e[h] - beta * eincl[h]
                dz.append((jnp.where(before, d, 0.0) if diag else d).astype(BF16))
            wb = [w[h].astype(BF16) for h in heads]
            for h in heads:
                e_ref[h] = eincl[h][:, bk - 1:bk]
                dq_acc[h] += jnp.dot(dz[h], kj[h // 2], preferred_element_type=F32)
            for pr in range(npairs):
                rows = slice(LANES * pr, LANES * (pr + 1))
                h0, h1 = 2 * pr, 2 * pr + 1
                dk_acc[rows, pl.ds(ks, bk)] += (jnp.dot(qmt_ref[h0], dz[h0], preferred_element_type=F32)
                                                 + jnp.dot(qmt_ref[h1], dz[h1], preferred_element_type=F32))
                dv_acc[rows, pl.ds(ks, bk)] += (jnp.dot(dost[h0], wb[h0], preferred_element_type=F32)
                                                 + jnp.dot(dost[h1], wb[h1], preferred_element_type=F32))

        def loop_body(j, carry):
            block(j, False)
            return carry

        block_of_lane = lane & (HEAD_DIM - 1)
        live = jnp.max(rs_ref[...], axis=0) > DEAD
        first_live = jnp.min(jnp.where(live, block_of_lane, nq * ratio))
        lax.fori_loop(jnp.minimum(first_live, i * ratio), i * ratio, loop_body, 0)
        for t in range(ratio):
            block(i * ratio + t, True)
        for pr in range(npairs):
            dq = jnp.where(lo_half, dq_acc[2 * pr], dq_acc[2 * pr + 1]) * scale
            dq_ref[:, LANES * pr:LANES * (pr + 1)] = dq.astype(BF16)

        @pl.when(i == nq - 1)
        def _():
            dk_ref[...] = dk_acc[...].T.astype(BF16)
            dv_ref[...] = dv_acc[...].T.astype(BF16)

        @pl.when((pl.program_id(0) == n_steps - 1) & (i == nq - 1))
        def _():
            for dev in range(N_DEV):
                @pl.when(me != dev)
                def _():
                    for a in (1, 2, 3):
                        early(a, dev, dev).wait_recv()
                    for a in _early_arrays(dev):
                        early(a, dev, me).wait_send()

                @pl.when((me != dev) & (me >= _EARLY_IN_DEVS[0]))
                def _():
                    early(0, dev, dev).wait_recv()

    any_spec = pl.BlockSpec(memory_space=pl.ANY)
    return pl.pallas_call(
        body, name="attn_bwd", grid=(n_steps, nq),
        in_specs=[pl.BlockSpec((bq, width), lambda p, i: (i, n_steps * COL_Q + p)),
                  pl.BlockSpec((s, width), lambda p, i: (0, n_steps * COL_K + p)),
                  pl.BlockSpec((s, width), lambda p, i: (0, n_steps * COL_V + p)),
                  pl.BlockSpec((bq, width), lambda p, i: (i, p)),
                  pl.BlockSpec((npairs, bq, LANES), lambda p, i: (p, i, 0))] + [any_spec] * n_g,
        out_specs=[pl.BlockSpec((bq, width), lambda p, i: (i, p)),
                   pl.BlockSpec((s, width), lambda p, i: (0, p)),
                   pl.BlockSpec((s, width), lambda p, i: (0, p))] + [any_spec] * n_g,
        out_shape=[jax.ShapeDtypeStruct((s, D_BRANCH), BF16)] * 3
        + [jax.ShapeDtypeStruct((N_DEV,) + sh, BF16) for sh in _GRAD_PIECE_SHAPES],
        scratch_shapes=[pltpu.VMEM((width, s), F32), pltpu.VMEM((width, s), F32),
                        pltpu.VMEM((2 * npairs, bq, LANES), F32), pltpu.VMEM((2 * npairs, bq, 1), F32),
                        pltpu.VMEM((2 * npairs, LANES, bq), BF16), pltpu.VMEM((2 * npairs, LANES, bq), F32),
                        pltpu.VMEM((npairs, LANES, bq), F32),
                        pltpu.SemaphoreType.DMA((N_DEV * n_g,)), pltpu.SemaphoreType.DMA((N_DEV * n_g,))],
        compiler_params=pltpu.CompilerParams(dimension_semantics=("arbitrary", "arbitrary"),
                                             vmem_limit_bytes=VMEM_LIMIT, has_side_effects=True),
    )(proj, proj, proj, do, rsave, *grads)


def _group_avg_matrix():
    a = lax.broadcasted_iota(jnp.int32, (LANES, LANES), 0) >> GROUP_SHIFT
    b = lax.broadcasted_iota(jnp.int32, (LANES, LANES), 1) >> GROUP_SHIFT
    return jnp.where(a == b, 1.0 / GROUP_DIM, 0.0).astype(BF16)


def _group_mean(a, avg):
    parts = [_split_dot(a[:, LANES * k:LANES * (k + 1)], avg, 3) for k in range(D_BRANCH // LANES)]
    return jnp.concatenate(parts, axis=1)


def _sgu_forward_parts(ub, vb, ln_g, ln_b, avg):
    ug, dug = _gelu_and_grad(ub)
    vg, dvg = _gelu_and_grad(vb)
    mu = _group_mean(vg, avg)
    d = vg - mu
    var = _group_mean(d * d, avg)
    rstd = lax.rsqrt(var + EPS)
    vhat = d * rstd
    vn = vhat * ln_g + ln_b
    return ug, dug, dvg, rstd, vhat, vn


def _sgu_mix(w_ref, src_bf16, n_chunks):
    lane = lax.broadcasted_iota(jnp.int32, (SGU_CHUNK, LANES), 1)
    lo_half = lane < GROUP_DIM
    rows = []
    for n in range(n_chunks):
        slabs = []
        for a in range(D_BRANCH // LANES):
            blk = src_bf16[SGU_CHUNK * n:SGU_CHUNK * (n + 1), LANES * a:LANES * (a + 1)]
            zero = jnp.zeros_like(blk)
            m0 = jnp.dot(w_ref[2 * a], jnp.where(lo_half, blk, zero), preferred_element_type=F32)
            m1 = jnp.dot(w_ref[2 * a + 1], jnp.where(lo_half, zero, blk), preferred_element_type=F32)
            slabs.append(m0 + m1)
        rows.append(jnp.concatenate(slabs, axis=1))
    return jnp.concatenate(rows, axis=0)


def _sgu_fwd(proj, ln_g, ln_b, w_mask, bias_full):
    s = proj.shape[0]
    tm = min(512, s)
    n_chunks = tm // SGU_CHUNK

    def body(ub_ref, vb_ref, zb_ref, g_ref, b_ref, w_ref, bias_ref, yb_ref):
        avg = _group_avg_matrix()
        ug, _, _, _, _, vn = _sgu_forward_parts(ub_ref[...].astype(F32), vb_ref[...].astype(F32),
                                                g_ref[...], b_ref[...], avg)
        mixed = _sgu_mix(w_ref, vn.astype(BF16), n_chunks) + jnp.concatenate([bias_ref[...]] * n_chunks, axis=0)
        zb = zb_ref[...].astype(F32)
        yb_ref[...] = (ug * mixed * (zb * _sigmoid(zb))).astype(BF16)

    col = lambda c: pl.BlockSpec((tm, D_BRANCH), lambda i: (i, c))
    full = lambda shape: pl.BlockSpec(shape, lambda i: (0,) * len(shape))
    return pl.pallas_call(
        body, name="sgu_fwd", grid=(s // tm,),
        in_specs=[col(COL_UB), col(COL_VB), col(COL_ZB), full((1, D_BRANCH)), full((1, D_BRANCH)),
                  full((N_GROUPS, SGU_CHUNK, SGU_CHUNK)), full((SGU_CHUNK, D_BRANCH))],
        out_specs=pl.BlockSpec((tm, D_BRANCH), lambda i: (i, 0)),
        out_shape=jax.ShapeDtypeStruct((s, D_BRANCH), BF16),
        compiler_params=_cparams(("parallel",)),
    )(proj, proj, proj, ln_g, ln_b, w_mask, bias_full)


def _sgu_bwd(proj, dyb, ln_g, ln_b, w_mask, w_mask_t, bias_full):
    s = proj.shape[0]
    tm = min(512, s)
    n_chunks = tm // SGU_CHUNK
    n_steps = s // tm

    def body(ub_ref, vb_ref, zb_ref, dyb_ref, g_ref, b_ref, w_ref, wt_ref, bias_ref,
             dsgu_ref, dw_ref, db_ref, dg_ref, dbeta_ref, dmix_acc):
        i = pl.program_id(0)

        @pl.when(i == 0)
        def _():
            dw_ref[...] = jnp.zeros_like(dw_ref)
            dg_ref[...] = jnp.zeros_like(dg_ref)
            dbeta_ref[...] = jnp.zeros_like(dbeta_ref)
            dmix_acc[...] = jnp.zeros_like(dmix_acc)

        avg = _group_avg_matrix()
        ln_gv = g_ref[...]
        ug, dug, dvg, rstd, vhat, vn = _sgu_forward_parts(ub_ref[...].astype(F32), vb_ref[...].astype(F32),
                                                          ln_gv, b_ref[...], avg)
        vnb = vn.astype(BF16)
        mixed = _sgu_mix(w_ref, vnb, n_chunks) + jnp.concatenate([bias_ref[...]] * n_chunks, axis=0)
        zb = zb_ref[...].astype(F32)
        sg = _sigmoid(zb)
        sz = zb * sg
        dsz = sg * (1.0 + zb * (1.0 - sg))
        dy = dyb_ref[...].astype(F32)
        dmixed = dy * ug * sz
        du = dy * mixed * sz * dug
        dzb = dy * ug * mixed * dsz
        dmb = dmixed.astype(BF16)
        dvn = _sgu_mix(wt_ref, dmb, n_chunks)

        lane = lax.broadcasted_iota(jnp.int32, (SGU_CHUNK, LANES), 1)
        lo_half = lane < GROUP_DIM
        dm_sum = None
        for n in range(n_chunks):
            rows = slice(SGU_CHUNK * n, SGU_CHUNK * (n + 1))
            dm_sum = dmixed[rows] if dm_sum is None else dm_sum + dmixed[rows]
            for a in range(D_BRANCH // LANES):
                cols = slice(LANES * a, LANES * (a + 1))
                dblk = dmb[rows, cols]
                vblk = vnb[rows, cols]
                zero = jnp.zeros_like(dblk)
                dw_ref[2 * a] += _dot_nt(jnp.where(lo_half, dblk, zero), vblk)
                dw_ref[2 * a + 1] += _dot_nt(jnp.where(lo_half, zero, dblk), vblk)
        dmix_acc[...] += dm_sum

        dg_ref[...] += jnp.sum(dvn * vhat, axis=0, keepdims=True)
        dbeta_ref[...] += jnp.sum(dvn, axis=0, keepdims=True)
        dvh = dvn * ln_gv
        m1 = _group_mean(dvh, avg)
        m2 = _group_mean(dvh * vhat, avg)
        dv = rstd * (dvh - m1 - vhat * m2) * dvg
        dsgu_ref[:, 0:D_BRANCH] = du.astype(BF16)
        dsgu_ref[:, D_BRANCH:2 * D_BRANCH] = dv.astype(BF16)
        dsgu_ref[:, 2 * D_BRANCH:3 * D_BRANCH] = dzb.astype(BF16)

        @pl.when(i == n_steps - 1)
        def _():
            pos = lax.broadcasted_iota(jnp.int32, (SGU_CHUNK, SGU_CHUNK), 0) >> GROUP_SHIFT
            src = lax.broadcasted_iota(jnp.int32, (SGU_CHUNK, SGU_CHUNK), 1) >> GROUP_SHIFT
            keep = src <= pos
            for g in range(N_GROUPS):
                dw_ref[g] = jnp.where(keep, dw_ref[g], 0.0)
            grp = lax.broadcasted_iota(jnp.int32, (D_BRANCH, LANES), 0) >> GROUP_SHIFT
            sel = (grp == lax.broadcasted_iota(jnp.int32, (D_BRANCH, LANES), 1)).astype(BF16)
            db_ref[...] = _split_dot(dmix_acc[...], sel, 3)

    col = lambda c: pl.BlockSpec((tm, D_BRANCH), lambda i: (i, c))
    full = lambda shape: pl.BlockSpec(shape, lambda i: (0,) * len(shape))
    return pl.pallas_call(
        body, name="sgu_bwd", grid=(n_steps,),
        in_specs=[col(COL_UB), col(COL_VB), col(COL_ZB), pl.BlockSpec((tm, D_BRANCH), lambda i: (i, 0)),
                  full((1, D_BRANCH)), full((1, D_BRANCH)),
                  full((N_GROUPS, SGU_CHUNK, SGU_CHUNK)), full((N_GROUPS, SGU_CHUNK, SGU_CHUNK)),
                  full((SGU_CHUNK, D_BRANCH))],
        out_specs=[pl.BlockSpec((tm, 3 * D_BRANCH), lambda i: (i, 0)),
                   full((N_GROUPS, SGU_CHUNK, SGU_CHUNK)), full((SGU_CHUNK, LANES)),
                   full((1, D_BRANCH)), full((1, D_BRANCH))],
        out_shape=[jax.ShapeDtypeStruct((s, 3 * D_BRANCH), BF16),
                   jax.ShapeDtypeStruct((N_GROUPS, SGU_CHUNK, SGU_CHUNK), F32),
                   jax.ShapeDtypeStruct((SGU_CHUNK, LANES), F32),
                   jax.ShapeDtypeStruct((1, D_BRANCH), F32), jax.ShapeDtypeStruct((1, D_BRANCH), F32)],
        scratch_shapes=[pltpu.VMEM((SGU_CHUNK, D_BRANCH), F32)],
        compiler_params=_cparams(("arbitrary",)),
    )(proj, proj, proj, dyb, ln_g, ln_b, w_mask, w_mask_t, bias_full)


def _mid(proj, ya, yb, o, x, target, final_g, w_up_a, w_up_b, w_out):
    s = x.shape[0]
    tm = min(256, s)
    n_steps = s // tm
    half = D_MODEL // 2

    def body(ya_ref, yb_ref, o_ref, za_ref, ga0_ref, ga1_ref, gb0_ref, gb1_ref, x_ref, t_ref, gf_ref,
             wa_ref, wb_ref, wo_ref,
             dzg_ref, do_ref, dyb_ref, dx2_ref, gwo_ref, gwa_ref, gwb_ref, loss_ref, dgf_ref,
             acc_o, acc_a, acc_b):
        i = pl.program_id(0)

        @pl.when(i == 0)
        def _():
            acc_o[...] = jnp.zeros_like(acc_o)
            acc_a[...] = jnp.zeros_like(acc_a)
            acc_b[...] = jnp.zeros_like(acc_b)
            loss_ref[...] = jnp.zeros_like(loss_ref)
            dgf_ref[...] = jnp.zeros_like(dgf_ref)

        ya_v = ya_ref[...]
        yb_v = yb_ref[...]
        pa = jnp.dot(ya_v, wa_ref[...], preferred_element_type=F32)
        pb = jnp.dot(yb_v, wb_ref[...], preferred_element_type=F32)
        sa = _sigmoid(jnp.concatenate([ga0_ref[...], ga1_ref[...]], axis=1).astype(F32))
        sb = _sigmoid(jnp.concatenate([gb0_ref[...], gb1_ref[...]], axis=1).astype(F32))
        merged = (sa * pa + sb * pb).astype(BF16)
        x2 = x_ref[...] + jnp.dot(merged, wo_ref[...], preferred_element_type=F32)
        r2 = lax.rsqrt(jnp.mean(x2 * x2, axis=-1, keepdims=True) + EPS)
        xh = x2 * r2
        gf = gf_ref[...]
        diff = xh * gf - t_ref[...]
        loss_ref[...] += 0.5 * jnp.sum(jnp.mean(diff * diff, axis=-1, keepdims=True))
        dy = diff * (1.0 / D_MODEL)
        dgf_ref[...] += jnp.sum(dy * xh, axis=0, keepdims=True)
        dyg = dy * gf
        dx2 = r2 * (dyg - xh * jnp.mean(dyg * xh, axis=-1, keepdims=True))
        dx2_ref[...] = dx2
        dx2b = dx2.astype(BF16)
        dmerged = _dot_nt(dx2b, wo_ref[...])
        acc_o[...] += _dot_tn(merged, dx2b)
        dpa = dmerged * sa
        dpb = dmerged * sb
        dzg_ref[:, D_BRANCH:D_BRANCH + D_MODEL] = (dpa * pa * (1.0 - sa)).astype(BF16)
        dzg_ref[:, D_BRANCH + D_MODEL:D_BRANCH + 2 * D_MODEL] = (dpb * pb * (1.0 - sb)).astype(BF16)
        dpab = dpa.astype(BF16)
        dpbb = dpb.astype(BF16)
        acc_a[...] += _dot_tn(ya_v, dpab)
        acc_b[...] += _dot_tn(yb_v, dpbb)
        dya = _dot_nt(dpab, wa_ref[...])
        dyb_ref[...] = _dot_nt(dpbb, wb_ref[...]).astype(BF16)
        za = za_ref[...].astype(F32)
        sg = _sigmoid(za)
        do_ref[...] = (dya * (za * sg)).astype(BF16)
        dzg_ref[:, 0:D_BRANCH] = (dya * o_ref[...].astype(F32) * (sg * (1.0 + za * (1.0 - sg)))).astype(BF16)

        @pl.when(i == n_steps - 1)
        def _():
            gwo_ref[...] = acc_o[...].astype(BF16)
            gwa_ref[...] = acc_a[...].astype(BF16)
            gwb_ref[...] = acc_b[...].astype(BF16)

    tok = lambda w: pl.BlockSpec((tm, w), lambda i: (i, 0))
    col = lambda c: pl.BlockSpec((tm, half), lambda i: (i, c))
    full = lambda shape: pl.BlockSpec(shape, lambda i: (0,) * len(shape))
    return pl.pallas_call(
        body, name="mid", grid=(n_steps,),
        in_specs=[tok(D_BRANCH), tok(D_BRANCH), tok(D_BRANCH), col(COL_ZA), col(COL_GA), col(COL_GA + 1),
                  col(COL_GB), col(COL_GB + 1), tok(D_MODEL), tok(D_MODEL), full((1, D_MODEL)),
                  full((D_BRANCH, D_MODEL)), full((D_BRANCH, D_MODEL)), full((D_MODEL, D_MODEL))],
        out_specs=[tok(D_BRANCH + 2 * D_MODEL), tok(D_BRANCH), tok(D_BRANCH), tok(D_MODEL),
                   full((D_MODEL, D_MODEL)), full((D_BRANCH, D_MODEL)), full((D_BRANCH, D_MODEL)),
                   full((8, LANES)), full((1, D_MODEL))],
        out_shape=[jax.ShapeDtypeStruct((s, D_BRANCH + 2 * D_MODEL), BF16),
                   jax.ShapeDtypeStruct((s, D_BRANCH), BF16), jax.ShapeDtypeStruct((s, D_BRANCH), BF16),
                   jax.ShapeDtypeStruct((s, D_MODEL), F32),
                   jax.ShapeDtypeStruct((D_MODEL, D_MODEL), BF16),
                   jax.ShapeDtypeStruct((D_BRANCH, D_MODEL), BF16), jax.ShapeDtypeStruct((D_BRANCH, D_MODEL), BF16),
                   jax.ShapeDtypeStruct((8, LANES), F32), jax.ShapeDtypeStruct((1, D_MODEL), F32)],
        scratch_shapes=[pltpu.VMEM((D_MODEL, D_MODEL), F32), pltpu.VMEM((D_BRANCH, D_MODEL), F32),
                        pltpu.VMEM((D_BRANCH, D_MODEL), F32)],
        compiler_params=_cparams(("arbitrary",)),
    )(ya, yb, o, proj, proj, proj, proj, proj, x, target, final_g, w_up_a, w_up_b, w_out)


def _dwin_piece(ht, piece, tile_of, prev):
    s = ht.shape[1]
    n_tiles = piece.shape[1] // D_BRANCH

    def body(ht_ref, p_ref, *rest):
        out_ref = rest[-1]
        out_ref[...] = jnp.dot(ht_ref[...], p_ref[...], preferred_element_type=F32).astype(BF16)

    in_specs = [pl.BlockSpec((D_MODEL, s), lambda j: (0, 0)), pl.BlockSpec((s, D_BRANCH), lambda j: (0, j))]
    args = [ht, piece]
    aliases = {}
    if prev is not None:
        in_specs.append(pl.BlockSpec(memory_space=pl.ANY))
        args.append(prev)
        aliases = {2: 0}
    return pl.pallas_call(
        body, name="dwin_piece", grid=(n_tiles,),
        in_specs=in_specs,
        out_specs=pl.BlockSpec((D_MODEL, D_BRANCH), lambda j: (0, tile_of(j))),
        out_shape=jax.ShapeDtypeStruct((D_MODEL, D_IN), BF16),
        input_output_aliases=aliases,
        compiler_params=_cparams(("parallel",)),
    )(*args)


def _dh_dx(pieces, w_in, x, norm_g, dx2, g_in):
    s = x.shape[0]
    tm = min(256, s)
    n_steps = s // tm
    arrays = []
    for arr, _, _, _ in pieces:
        if not any(arr is a for a in arrays):
            arrays.append(arr)
    n_arr = len(arrays)
    plan = [([k for k, a in enumerate(arrays) if a is arr][0], wcol, off, width) for arr, wcol, off, width in pieces]

    def body(*refs):
        p_refs = refs[:n_arr]
        w_ref, x_ref, g_ref, dx2_ref, gin_ref, dx_ref, dg_ref, late_ref, send_sems, recv_sems = refs[n_arr:]
        x_, y_, c_ = lax.axis_index("x"), lax.axis_index("y"), lax.axis_index("c")
        me = 4 * x_ + 2 * y_ + c_

        def late(dst_dev, src_dev):
            return pltpu.make_async_remote_copy(
                src_ref=_grad_piece(gin_ref, 0, dst_dev), dst_ref=late_ref.at[src_dev],
                send_sem=send_sems.at[dst_dev], recv_sem=recv_sems.at[src_dev],
                device_id=_dev_id(dst_dev), device_id_type=MESH)

        @pl.when(pl.program_id(0) == 0)
        def _():
            dg_ref[...] = jnp.zeros_like(dg_ref)
            for dev in _LATE_IN_DEVS:
                @pl.when(me != dev)
                def _():
                    late(dev, me).start()

        dh = None
        for k, wcol, off, width in plan:
            d = _dot_nt(p_refs[k][:, off:off + width], w_ref[:, wcol:wcol + width])
            dh = d if dh is None else dh + d
        xf = x_ref[...]
        r = lax.rsqrt(jnp.mean(xf * xf, axis=-1, keepdims=True) + EPS)
        xh = xf * r
        dg_ref[...] += jnp.sum(dh * xh, axis=0, keepdims=True)
        dhg = dh * g_ref[...]
        dx_ref[...] = r * (dhg - xh * jnp.mean(dhg * xh, axis=-1, keepdims=True)) + dx2_ref[...]

        @pl.when(pl.program_id(0) == n_steps - 1)
        def _():
            for dev in range(N_DEV):
                @pl.when((me != dev) & (me <= _LATE_IN_DEVS[-1]))
                def _():
                    late(dev, dev).wait_recv()
            for dev in _LATE_IN_DEVS:
                @pl.when(me != dev)
                def _():
                    late(dev, me).wait_send()

    tok = lambda w: pl.BlockSpec((tm, w), lambda i: (i, 0))
    full = lambda shape: pl.BlockSpec(shape, lambda i: (0,) * len(shape))
    any_spec = pl.BlockSpec(memory_space=pl.ANY)
    return pl.pallas_call(
        body, name="dh_dx", grid=(n_steps,),
        in_specs=[tok(a.shape[1]) for a in arrays] + [full((D_MODEL, D_IN)), tok(D_MODEL), full((1, D_MODEL)),
                                                      tok(D_MODEL), any_spec],
        out_specs=[tok(D_MODEL), full((1, D_MODEL)), any_spec],
        out_shape=[jax.ShapeDtypeStruct((s, D_MODEL), F32), jax.ShapeDtypeStruct((1, D_MODEL), F32),
                   jax.ShapeDtypeStruct((N_DEV,) + _GRAD_PIECE_SHAPES[0], BF16)],
        scratch_shapes=[pltpu.SemaphoreType.DMA((N_DEV,)), pltpu.SemaphoreType.DMA((N_DEV,))],
        compiler_params=pltpu.CompilerParams(dimension_semantics=("arbitrary",), vmem_limit_bytes=VMEM_LIMIT,
                                             has_side_effects=True),
    )(*arrays, w_in, x, norm_g, dx2, g_in)


def _adamw(w, g, m, v):
    rows, cols = w.shape
    tr = max(t for t in range(8, 257, 8) if rows % t == 0)
    c1 =1.0 - ADAM_B1 ** ADAM_STEP
    c2 = 1.0 - ADAM_B2 ** ADAM_STEP

    def body(w_ref, g_ref, m_ref, v_ref, g_out_ref, d_ref, nm_ref, nv_ref):
        gv = g_ref[...]
        g_out_ref[...] = gv
        nm = ADAM_B1 * m_ref[...] + (1.0 - ADAM_B1) * gv
        nv = ADAM_B2 * v_ref[...] + (1.0 - ADAM_B2) * (gv * gv)
        d_ref[...] = -ADAM_LR * ((nm / c1) / (jnp.sqrt(nv / c2) + ADAM_EPS) + ADAM_WD * w_ref[...])
        nm_ref[...] = nm
        nv_ref[...] = nv

    spec = pl.BlockSpec((tr, cols), lambda i: (i, 0))
    return pl.pallas_call(
        body, name="adamw", grid=(rows // tr,),
        in_specs=[spec] * 4, out_specs=[spec] * 4,
        out_shape=[jax.ShapeDtypeStruct((rows, cols), F32)] * 4,
        compiler_params=_cparams(("parallel",)),
    )(w, g, m, v)


def _place():
    x, y, c = lax.axis_index("x"), lax.axis_index("y"), lax.axis_index("c")
    return x, y, c


def _gather_weights(w_in, w_up_a, w_up_b, w_out):
    shards = (w_in, w_up_a, w_up_b, w_out)
    n_arr = len(shards)
    col_sharded = (True, True, True, False)
    full_shapes = ((D_MODEL, D_IN), (D_BRANCH, D_MODEL), (D_BRANCH, D_MODEL), (D_MODEL, D_MODEL))

    def body(*refs):
        src = refs[:n_arr]
        out = refs[n_arr:2 * n_arr]
        stage = refs[2 * n_arr:3 * n_arr]
        cast = refs[3 * n_arr:4 * n_arr]
        send_sems, recv_sems, local_sems = refs[4 * n_arr:]
        x, y, c = _place()
        chip = 2 * x + y
        sibling = (x, y, 1 - c)
        others = [(1 - x, y), (x, 1 - y), (1 - x, 1 - y)]

        def region(a, chip_idx, half):
            r, w = shards[a].shape
            hr = r // 2
            if col_sharded[a]:
                return out[a].at[pl.ds(_aligned(half * hr, 16), hr), pl.ds(_aligned(chip_idx * w, LANES), w)]
            return out[a].at[pl.ds(_aligned(chip_idx * r + half * hr, 16), hr), :]

        loads = [pltpu.make_async_copy(src[a], stage[a], local_sems.at[a]) for a in range(n_arr)]
        for cp in loads:
            cp.start()
        for a in range(n_arr):
            loads[a].wait()
            cast[a][...] = stage[a][...].astype(BF16)
        stores = []
        for a in range(n_arr):
            hr = shards[a].shape[0] // 2
            for half in range(2):
                cp = pltpu.make_async_copy(cast[a].at[pl.ds(half * hr, hr), :], region(a, chip, half),
                                           local_sems.at[n_arr + 2 * a + half])
                cp.start()
                stores.append(cp)

        def remote(k, a, chip_idx, half, to, from_vmem):
            hr = shards[a].shape[0] // 2
            s_ref = cast[a].at[pl.ds(_aligned(half * hr, 16), hr), :] if from_vmem else region(a, chip_idx, half)
            return pltpu.make_async_remote_copy(src_ref=s_ref, dst_ref=region(a, chip_idx, half),
                                                send_sem=send_sems.at[k], recv_sem=recv_sems.at[k],
                                                device_id=to, device_id_type=MESH)

        first = []
        for j, (ox, oy) in enumerate(others):
            for a in range(n_arr):
                cp = remote(n_arr * j + a, a, chip, c, (ox, oy, c), True)
                cp.start()
                first.append(cp)
        passed = []
        for j, (ox, oy) in enumerate(others):
            ochip = 2 * ox + oy
            for a in range(n_arr):
                k = n_arr * j + a
                remote(k, a, ochip, c, sibling, False).wait_recv()
                cp = remote(3 * n_arr + k, a, ochip, c, sibling, False)
                cp.start()
                passed.append(cp)
        for j, (ox, oy) in enumerate(others):
            ochip = 2 * ox + oy
            for a in range(n_arr):
                remote(3 * n_arr + n_arr * j + a, a, ochip, 1 - c, sibling, False).wait_recv()
        for cp in first + passed:
            cp.wait_send()
        for cp in stores:
            cp.wait()

    any_spec = pl.BlockSpec(memory_space=pl.ANY)
    return pl.pallas_call(
        body, name="gather_weights",
        in_specs=[any_spec] * n_arr, out_specs=[any_spec] * n_arr,
        out_shape=[jax.ShapeDtypeStruct(sh, BF16) for sh in full_shapes],
        scratch_shapes=[pltpu.VMEM(a.shape, F32) for a in shards] + [pltpu.VMEM(a.shape, BF16) for a in shards]
        + [pltpu.SemaphoreType.DMA((6 * n_arr,)), pltpu.SemaphoreType.DMA((6 * n_arr,)),
           pltpu.SemaphoreType.DMA((3 * n_arr,))],
        compiler_params=pltpu.CompilerParams(vmem_limit_bytes=VMEM_LIMIT, has_side_effects=True),
    )(*shards)


def _reduce_grads_tail(grads, g_small, early_slots, late_in_slots):
    n_big = len(grads)
    n_arr = n_big + 1
    shard_shapes = [(2 * r, w) for r, w in _GRAD_PIECE_SHAPES]
    small_piece = (SMALL_PIECE, LANES)

    def body(*refs):
        src = refs[:n_arr]
        early = refs[n_arr:n_arr + n_big]
        late_in = refs[n_arr + n_big]
        n_in = n_arr + n_big + 1
        out = refs[n_in:n_in + n_arr]
        slots = refs[n_in + n_arr:n_in + 2 * n_arr]
        sums = refs[n_in + 2 * n_arr:n_in + 3 * n_arr]
        send1, recv1, send2, recv2, local_sems = refs[n_in + 3 * n_arr:]
        x, y, c = _place()
        me = 4 * x + 2 * y + c

        def piece_of(a, dev):
            return src[a].at[dev] if a == n_big else _grad_piece(src[a], a, dev)

        def late(a, dst_dev, src_dev):
            return pltpu.make_async_remote_copy(
                src_ref=piece_of(a, dst_dev), dst_ref=slots[a].at[src_dev],
                send_sem=send1.at[n_arr * dst_dev + a], recv_sem=recv1.at[n_arr * src_dev + a],
                device_id=_dev_id(dst_dev), device_id_type=MESH)

        def late_arrays(dev):
            return (n_big,)

        def load(a, dev, received):
            return pltpu.make_async_copy(received.at[dev], slots[a].at[dev], local_sems.at[n_arr * dev + a])

        def own(a, dev):
            return pltpu.make_async_copy(piece_of(a, dev), slots[a].at[dev], local_sems.at[n_arr * dev + a])

        for dev in range(N_DEV):
            @pl.when(me == dev)
            def _():
                received = [early[0] if dev in _EARLY_IN_DEVS else late_in] + list(early[1:])
                for a in range(n_arr):
                    own(a, dev).start()
                for peer in range(N_DEV):
                    if peer != dev:
                        for a in late_arrays(peer):
                            late(a, peer, dev).start()
                        for a in range(n_big):
                            load(a, peer, received[a]).start()
                for a in range(n_arr):
                    own(a, dev).wait()
                for peer in range(N_DEV):
                    if peer != dev:
                        for a in late_arrays(dev):
                            late(a, dev, peer).wait_recv()
                        for a in range(n_big):
                            load(a, peer, received[a]).wait()

        for a in range(n_arr):
            rows = slots[a].shape[1]
            step = 64 if rows % 64 == 0 else 8

            def add_rows(t, carry, a=a, step=step):
                r0 = pl.multiple_of(t * step, step)
                total = slots[a][0, pl.ds(r0, step), :].astype(F32)
                for dev in range(1, N_DEV):
                    total = total + slots[a][dev, pl.ds(r0, step), :].astype(F32)
                sums[a][pl.ds(r0, step), :] = total
                return carry

            lax.fori_loop(0, rows // step, add_rows, 0)

        shares = []
        keeps = []
        for a in range(n_big):
            r, w = _GRAD_PIECE_SHAPES[a]
            dst = out[a].at[pl.ds(pl.multiple_of(c * r, 8), r), :]
            cp = pltpu.make_async_remote_copy(src_ref=sums[a], dst_ref=dst, send_sem=send2.at[a], recv_sem=recv2.at[a],
                                              device_id=(x, y, 1 - c), device_id_type=MESH)
            cp.start()
            shares.append(cp)
            kp = pltpu.make_async_copy(sums[a], dst, local_sems.at[N_DEV * n_arr + a])
            kp.start()
            keeps.append(kp)
        kp = pltpu.make_async_copy(sums[n_big], out[n_big].at[me], local_sems.at[N_DEV * n_arr + n_big])
        kp.start()
        keeps.append(kp)

        def small_share(dst_dev, src_dev):
            return pltpu.make_async_remote_copy(src_ref=sums[n_big], dst_ref=out[n_big].at[src_dev],
                                                send_sem=send2.at[n_big + dst_dev], recv_sem=recv2.at[n_big + src_dev],
                                                device_id=_dev_id(dst_dev), device_id_type=MESH)

        for dev in range(N_DEV):
            @pl.when(me != dev)
            def _():
                small_share(dev, me).start()
        for a in range(n_big):
            r, w = _GRAD_PIECE_SHAPES[a]
            other = out[a].at[pl.ds(pl.multiple_of((1 - c) * r, 8), r), :]
            pltpu.make_async_remote_copy(src_ref=sums[a], dst_ref=other, send_sem=send2.at[a], recv_sem=recv2.at[a],
                                         device_id=(x, y, 1 - c), device_id_type=MESH).wait_recv()
        for dev in range(N_DEV):
            @pl.when(me != dev)
            def _():
                small_share(dev, dev).wait_recv()
                small_share(dev, me).wait_send()
                for a in late_arrays(dev):
                    late(a, dev, me).wait_send()
        for cp in shares:
            cp.wait_send()
        for kp in keeps:
            kp.wait()

    any_spec = pl.BlockSpec(memory_space=pl.ANY)
    return pl.pallas_call(
        body, name="reduce_grads_tail",
        in_specs=[any_spec] * (n_arr + n_big + 1), out_specs=[any_spec] * n_arr,
        out_shape=[jax.ShapeDtypeStruct(sh, F32) for sh in shard_shapes]
        + [jax.ShapeDtypeStruct((N_DEV,) + small_piece, F32)],
        scratch_shapes=[pltpu.VMEM((N_DEV,) + sh, BF16) for sh in _GRAD_PIECE_SHAPES]
        + [pltpu.VMEM((N_DEV,) + small_piece, F32)]
        + [pltpu.VMEM(sh, F32) for sh in _GRAD_PIECE_SHAPES] + [pltpu.VMEM(small_piece, F32)]
        + [pltpu.SemaphoreType.DMA((N_DEV * n_arr,)), pltpu.SemaphoreType.DMA((N_DEV * n_arr,)),
           pltpu.SemaphoreType.DMA((n_big + N_DEV,)), pltpu.SemaphoreType.DMA((n_big + N_DEV,)),
           pltpu.SemaphoreType.DMA((N_DEV * n_arr + n_arr,))],
        compiler_params=pltpu.CompilerParams(vmem_limit_bytes=VMEM_LIMIT, has_side_effects=True),
    )(*grads, g_small, *early_slots, late_in_slots)


def _reduce_grads(g_in, g_up_a, g_up_b, g_out, g_small):
    big = (g_in, g_up_a, g_up_b, g_out)
    n_big = len(big)
    col_sharded = (True, True, True, False)
    piece_shapes = []
    for a, arr in enumerate(big):
        r, w = arr.shape
        piece_shapes.append((r // 2, w // N_CHIPS) if col_sharded[a] else (r // (2 * N_CHIPS), w))
    shard_shapes = [(2 * r, w) for r, w in piece_shapes]
    n_arr = n_big + 1

    def body(*refs):
        src = refs[:n_arr]
        out = refs[n_arr:2 * n_arr]
        slots = refs[2 * n_arr:3 * n_arr]
        sums = refs[3 * n_arr:4 * n_arr]
        send1, recv1, send2, recv2, local_sems = refs[4 * n_arr:]
        x, y, c = _place()
        me = 4 * x + 2 * y + c

        def piece_of(a, dev):
            chip_idx, half = dev // 2, dev % 2
            if a == n_big:
                return src[a].at[dev]
            r, w = piece_shapes[a]
            if col_sharded[a]:
                return src[a].at[pl.ds(pl.multiple_of(half * r, 16), r), pl.ds(pl.multiple_of(chip_idx * w, LANES), w)]
            return src[a].at[pl.ds(pl.multiple_of(dev * r, 16), r), :]

        def dev_id(dev):
            return (dev // 4, (dev // 2) % 2, dev % 2)

        own = [pltpu.make_async_copy(piece_of(a, me), slots[a].at[me], local_sems.at[a]) for a in range(n_arr)]
        for cp in own:
            cp.start()
        sends = []
        for d in range(1, N_DEV):
            peer = (me + d) % N_DEV
            for a in range(n_arr):
                cp = pltpu.make_async_remote_copy(
                    src_ref=piece_of(a, peer), dst_ref=slots[a].at[me],
                    send_sem=send1.at[n_arr * peer + a], recv_sem=recv1.at[n_arr * me + a],
                    device_id=dev_id(peer), device_id_type=MESH)
                cp.start()
                sends.append(cp)
        for cp in own:
            cp.wait()
        for d in range(1, N_DEV):
            peer = (me + d) % N_DEV
            for a in range(n_arr):
                pltpu.make_async_remote_copy(
                    src_ref=piece_of(a, peer), dst_ref=slots[a].at[peer],
                    send_sem=send1.at[n_arr * peer + a], recv_sem=recv1.at[n_arr * peer + a],
                    device_id=dev_id(peer), device_id_type=MESH).wait_recv()
        for a in range(n_arr):
            rows = slots[a].shape[1]
            step = 64 if rows % 64 == 0 else 8

            def add_rows(t, carry, a=a, step=step):
                r0 = pl.multiple_of(t * step, step)
                total = slots[a][0, pl.ds(r0, step), :].astype(F32)
                for dev in range(1, N_DEV):
                    total = total + slots[a][dev, pl.ds(r0, step), :].astype(F32)
                sums[a][pl.ds(r0, step), :] = total
                return carry

            lax.fori_loop(0, rows // step, add_rows, 0)
        shares = []
        keeps = []
        for a in range(n_big):
            r, w = piece_shapes[a]
            dst = out[a].at[pl.ds(pl.multiple_of(c * r, 8), r), :]
            cp = pltpu.make_async_remote_copy(src_ref=sums[a], dst_ref=dst, send_sem=send2.at[a], recv_sem=recv2.at[a],
                                              device_id=(x, y, 1 - c), device_id_type=MESH)
            cp.start()
            shares.append(cp)
            kp = pltpu.make_async_copy(sums[a], dst, local_sems.at[n_arr + a])
            kp.start()
            keeps.append(kp)
        kp = pltpu.make_async_copy(sums[n_big], out[n_big].at[me], local_sems.at[n_arr + n_big])
        kp.start()
        keeps.append(kp)
        for d in range(1, N_DEV):
            peer = (me + d) % N_DEV
            cp = pltpu.make_async_remote_copy(src_ref=sums[n_big], dst_ref=out[n_big].at[me],
                                              send_sem=send2.at[n_big + peer], recv_sem=recv2.at[n_big + me],
                                              device_id=dev_id(peer), device_id_type=MESH)
            cp.start()
            shares.append(cp)
        for a in range(n_big):
            r, w = piece_shapes[a]
            other = out[a].at[pl.ds(pl.multiple_of((1 - c) * r, 8), r), :]
            pltpu.make_async_remote_copy(src_ref=sums[a], dst_ref=other, send_sem=send2.at[a], recv_sem=recv2.at[a],
                                         device_id=(x, y, 1 - c), device_id_type=MESH).wait_recv()
        for d in range(1, N_DEV):
            peer = (me + d) % N_DEV
            pltpu.make_async_remote_copy(src_ref=sums[n_big], dst_ref=out[n_big].at[peer],
                                         send_sem=send2.at[n_big + peer], recv_sem=recv2.at[n_big + peer],
                                         device_id=dev_id(peer), device_id_type=MESH).wait_recv()
        for cp in sends + shares:
            cp.wait_send()
        for kp in keeps:
            kp.wait()

    any_spec = pl.BlockSpec(memory_space=pl.ANY)
    small_piece = (SMALL_PIECE, LANES)
    return pl.pallas_call(
        body, name="reduce_grads",
        in_specs=[any_spec] * n_arr, out_specs=[any_spec] * n_arr,
        out_shape=[jax.ShapeDtypeStruct(sh, F32) for sh in shard_shapes]
        + [jax.ShapeDtypeStruct((N_DEV,) + small_piece, F32)],
        scratch_shapes=[pltpu.VMEM((N_DEV,) + sh, BF16) for sh in piece_shapes]
        + [pltpu.VMEM((N_DEV,) + small_piece, F32)]
        + [pltpu.VMEM(sh, F32) for sh in piece_shapes] + [pltpu.VMEM(small_piece, F32)]
        + [pltpu.SemaphoreType.DMA((N_DEV * n_arr,)), pltpu.SemaphoreType.DMA((N_DEV * n_arr,)),
           pltpu.SemaphoreType.DMA((n_big + N_DEV,)), pltpu.SemaphoreType.DMA((n_big + N_DEV,)),
           pltpu.SemaphoreType.DMA((2 * n_arr,))],
        compiler_params=pltpu.CompilerParams(vmem_limit_bytes=VMEM_LIMIT, has_side_effects=True),
    )(*big, g_small)


_SMALL_PARTS = (("norm_g", 8), ("sgu_ln_g", 8), ("sgu_ln_b", 8), ("w_spatial", 1024), ("b_spatial", 8),
                ("final_norm_g", 8))
_LOSS_ROW = sum(n for _, n in _SMALL_PARTS)


def _pack_small(parts, loss_tile=None):
    rows = []
    for name, n_rows in _SMALL_PARTS:
        a = parts[name].reshape(-1, LANES).astype(F32)
        a = jnp.pad(a, ((0, n_rows - a.shape[0]), (0, 0)))
        rows.append(a)
    rows.append(jnp.zeros((8, LANES), F32) if loss_tile is None else loss_tile)
    rows.append(jnp.zeros((SMALL_ROWS - _LOSS_ROW - 8, LANES), F32))
    return jnp.concatenate(rows, axis=0)


def _unpack_small(packed, shapes):
    out = {}
    r0 = 0
    for name, n_rows in _SMALL_PARTS:
        n = math.prod(shapes[name])
        out[name] = packed[r0:r0 + n // LANES].reshape(shapes[name])
        r0 += n_rows
    return out


def _local_step(proj, ht, x, target, norm_g, w_in, sgu_ln_g, sgu_ln_b, w_spatial, b_spatial, w_up_a, w_up_b, w_out,
                final_norm_g, bq, bk):
    pos = jnp.arange(SGU_CHUNK)
    keep = (pos[None, :] // SGU_SUBCHUNK) <= (pos[:, None] // SGU_SUBCHUNK)
    w_mask = jnp.where(keep[None], w_spatial, 0.0).astype(BF16)
    w_mask_t = jnp.swapaxes(w_mask, 1, 2)
    bias_full = jnp.repeat(b_spatial.T, GROUP_DIM, axis=1)
    ln_g = sgu_ln_g.reshape(1, D_BRANCH)
    ln_b = sgu_ln_b.reshape(1, D_BRANCH)
    final_g = final_norm_g.reshape(1, D_MODEL)

    o, ya, rsave = _attn_fwd(proj, bq, bk, ATTN_PAIRS)
    yb = _sgu_fwd(proj, ln_g, ln_b, w_mask, bias_full)
    dzg, do, dyb, dx2, g_out, g_up_a, g_up_b, loss_acc, d_final = _mid(
        proj, ya, yb, o, x, target, final_g, w_up_a, w_up_b, w_out)
    dsgu, d_wsp, d_bsp, d_lng, d_lnb = _sgu_bwd(proj, dyb, ln_g, ln_b, w_mask, w_mask_t, bias_full)
    g_in = _dwin_piece(ht, dzg, lambda j: jnp.where(j == 0, COL_ZA, COL_GA - 1 + j), None)
    g_in = _dwin_piece(ht, dsgu, lambda j: COL_UB + j, g_in)
    dq, dk, dv, *early_slots = _attn_bwd(proj, do, rsave, bq, bk, ATTN_PAIRS, (g_in, g_up_a, g_up_b, g_out))
    g_in = _dwin_piece(ht, dq, lambda j: COL_Q + j, g_in)
    g_in = _dwin_piece(ht, dk, lambda j: COL_K + j, g_in)
    g_in = _dwin_piece(ht, dv, lambda j: COL_V + j, g_in)
    pieces = [(dq, COL_Q * D_BRANCH, 0, D_BRANCH), (dk, COL_K * D_BRANCH, 0, D_BRANCH),
              (dv, COL_V * D_BRANCH, 0, D_BRANCH), (dzg, COL_ZA * D_BRANCH, 0, D_BRANCH),
              (dsgu, COL_UB * D_BRANCH, 0, 3 * D_BRANCH), (dzg, COL_GA * D_BRANCH, D_BRANCH, 2 * D_MODEL)]
    dx, d_norm, late_in_slots = _dh_dx(pieces, w_in, x, norm_g, dx2, g_in)
    small = {"norm_g": d_norm, "sgu_ln_g": d_lng, "sgu_ln_b": d_lnb, "w_spatial": d_wsp,
             "b_spatial": d_bsp[:, :N_GROUPS].T, "final_norm_g": d_final}
    return loss_acc, dx, (g_in, g_up_a, g_up_b, g_out), small, early_slots, late_in_slots


def kernel(x, norm_g, w_in, sgu_ln_g, sgu_ln_b, w_spatial, b_spatial, w_up_a, w_up_b, w_out, final_norm_g, loss_target, m_norm_g, m_w_in, m_sgu_ln_g, m_sgu_ln_b, m_w_spatial, m_b_spatial, m_w_up_a, m_w_up_b, m_w_out, m_final_norm_g, v_norm_g, v_w_in, v_sgu_ln_g, v_sgu_ln_b, v_w_spatial, v_b_spatial, v_w_up_a, v_w_up_b, v_w_out, v_final_norm_g):
    big_names = ("w_in", "w_up_a", "w_up_b", "w_out")
    small_names = tuple(n for n, _ in _SMALL_PARTS)
    names = ("norm_g", "w_in", "sgu_ln_g", "sgu_ln_b", "w_spatial", "b_spatial", "w_up_a", "w_up_b", "w_out",
             "final_norm_g")
    w = dict(norm_g=norm_g, w_in=w_in, sgu_ln_g=sgu_ln_g, sgu_ln_b=sgu_ln_b, w_spatial=w_spatial,
             b_spatial=b_spatial, w_up_a=w_up_a, w_up_b=w_up_b, w_out=w_out, final_norm_g=final_norm_g)
    m = dict(norm_g=m_norm_g, w_in=m_w_in, sgu_ln_g=m_sgu_ln_g, sgu_ln_b=m_sgu_ln_b, w_spatial=m_w_spatial,
             b_spatial=m_b_spatial, w_up_a=m_w_up_a, w_up_b=m_w_up_b, w_out=m_w_out, final_norm_g=m_final_norm_g)
    v = dict(norm_g=v_norm_g, w_in=v_w_in, sgu_ln_g=v_sgu_ln_g, sgu_ln_b=v_sgu_ln_b, w_spatial=v_w_spatial,
             b_spatial=v_b_spatial, w_up_a=v_w_up_a, w_up_b=v_w_up_b, w_out=v_w_out, final_norm_g=v_final_norm_g)
    shapes = {n: w[n].shape for n in names}
    flat2d = lambda a: a.reshape(a.shape[-2:])

    proj, ht, *full = _in_proj_gather(x[0], norm_g, *[flat2d(w[n]) for n in big_names])
    loss, dx, big_grads, small, early_slots, late_in_slots = _local_step(
        proj, ht, x[0], loss_target[0], norm_g, full[0], sgu_ln_g[0], sgu_ln_b[0], w_spatial[0], b_spatial[0],
        full[1], full[2], full[3], final_norm_g, ATTN_Q_BLOCK, ATTN_K_BLOCK)
    packed = _pack_small(small, loss).reshape(N_DEV, SMALL_PIECE, LANES)
    red = _reduce_grads_tail(big_grads, packed, early_slots, late_in_slots)

    grads, deltas, new_m, new_v = {}, {}, {}, {}
    for n, g in zip(big_names, red[:4]):
        g, d, nm, nv = _adamw(flat2d(w[n]), g, flat2d(m[n]), flat2d(v[n]))
        grads[n], deltas[n], new_m[n], new_v[n] = (a.reshape(shapes[n]) for a in (g, d, nm, nv))
    g_small = red[4].reshape(SMALL_ROWS, LANES)
    g_small, d, nm, nv = _adamw(_pack_small({n: w[n] for n in small_names}), g_small,
                                _pack_small({n: m[n] for n in small_names}),
                                _pack_small({n: v[n] for n in small_names}))
    for src, dst in ((g_small, grads), (d, deltas), (nm, new_m), (nv, new_v)):
        dst.update(_unpack_small(src, shapes))

    return (g_small[_LOSS_ROW, 0], dx[None], *[grads[n] for n in names], *[deltas[n] for n in names],
            *[new_m[n] for n in names], *[new_v[n] for n in names])
```

```python
import functools
import math

import jax
import jax.numpy as jnp
from jax import lax
from jax.experimental import pallas as pl
from jax.experimental.pallas import tpu as pltpu

F32 = jnp.float32
BF16 = jnp.bfloat16

D_MODEL = 1024
N_HEADS = 8
HEAD_DIM = 64
D_BRANCH = 512
D_IN = 4 * D_BRANCH + 3 * D_BRANCH + 2 * D_MODEL
N_GROUPS = 8
GROUP_DIM = 64
SGU_CHUNK = 128
SGU_SUBCHUNK = 64
GROUP_SHIFT = 6
EPS = 1e-6
LANES = 128
ATTN_Q_BLOCK = 256
ATTN_K_BLOCK = 256
DEAD = -110.0
SKIPPED = -1e30
SCAN_PASSES = 1
ATTN_PAIRS = 2
N_CHIPS = 4
N_DEV = 8
MESH = pl.DeviceIdType.MESH

ADAM_LR = 0.001
ADAM_B1 = 0.9
ADAM_B2 = 0.999
ADAM_EPS = 1e-08
ADAM_WD = 0.01
ADAM_STEP = 10

COL_Q, COL_K, COL_V, COL_ZA, COL_UB, COL_VB, COL_ZB, COL_GA, COL_GB = 0, 1, 2, 3, 4, 5, 6, 7, 9

VMEM_LIMIT = 56 * 1024 * 1024

SMALL_ROWS = 1088
SMALL_PIECE = SMALL_ROWS // N_DEV


def _cparams(sem=None):
    return pltpu.CompilerParams(dimension_semantics=sem, vmem_limit_bytes=VMEM_LIMIT)


def _aligned(v, m):
    return v if isinstance(v, int) else pl.multiple_of(v, m)


def _sigmoid(x):
    return 1.0 / (1.0 + jnp.exp(-x))


def _gelu_and_grad(x):
    k = math.sqrt(2.0 / math.pi)
    x2 = x * x
    inner = k * (x + 0.044715 * x * x2)
    th = jnp.tanh(inner)
    g = 0.5 * x * (1.0 + th)
    dg = 0.5 * (1.0 + th) + 0.5 * x * (1.0 - th * th) * (k * (1.0 + 3.0 * 0.044715 * x2))
    return g, dg


def _split_dot(a, b_bf16, passes):
    out = None
    rem = a
    for _ in range(passes):
        part = rem.astype(BF16)
        d = jnp.dot(part, b_bf16, preferred_element_type=F32)
        out = d if out is None else out + d
        rem = rem - part.astype(F32)
    return out


def _dot_nt(a, b):
    return lax.dot_general(a, b, (((1,), (1,)), ((), ())), preferred_element_type=F32)


def _dot_tn(a, b):
    return lax.dot_general(a, b, (((0,), (0,)), ((), ())), preferred_element_type=F32)


def _place():
    x, y, c = lax.axis_index("x"), lax.axis_index("y"), lax.axis_index("c")
    return x, y, c


def _in_proj_gather(x, norm_g, w_in, w_up_a, w_up_b, w_out):
    s = x.shape[0]
    tm = min(512, s)
    nt = s // tm
    shards = (w_in, w_up_a, w_up_b, w_out)
    n_arr = len(shards)
    col_sharded = (True, True, True, False)
    full_shapes = ((D_MODEL, D_IN), (D_BRANCH, D_MODEL), (D_BRANCH, D_MODEL), (D_MODEL, D_MODEL))
    w_shard = w_in.shape[1]
    half_rows = D_MODEL // 2
    stage_rows = 256

    def body(order_ref, x_ref, g_ref, *refs):
        src = refs[:n_arr]
        proj_ref, ht_ref = refs[n_arr:n_arr + 2]
        out = refs[n_arr + 2:2 * n_arr + 2]
        wsc, h_scr, stage = refs[2 * n_arr + 2:2 * n_arr + 5]
        small_stage = refs[2 * n_arr + 5:2 * n_arr + 8]
        small_cast = refs[2 * n_arr + 8:2 * n_arr + 11]
        send_sems, recv_sems, local_sems = refs[2 * n_arr + 11:]
        k = pl.program_id(0)
        i = pl.program_id(1)
        x_, y_, c = _place()
        chip = 2 * x_ + y_
        sibling = (x_, y_, 1 - c)
        others = [(x_, 1 - y_), (1 - x_, y_), (1 - x_, 1 - y_)]

        def region(a, chip_idx, half):
            if a == 0:
                return wsc.at[chip_idx, pl.ds(_aligned(half * half_rows, 16), half_rows), :]
            r, w = shards[a].shape
            hr = r // 2
            if col_sharded[a]:
                return out[a].at[pl.ds(_aligned(half * hr, 16), hr), pl.ds(_aligned(chip_idx * w, LANES), w)]
            return out[a].at[pl.ds(_aligned(chip_idx * r + half * hr, 16), hr), :]

        def remote(kk, a, chip_idx, half, to, own):
            s_ref = region(a, chip_idx, half)
            if own and a > 0:
                hr = shards[a].shape[0] // 2
                s_ref = small_cast[a - 1].at[pl.ds(_aligned(half * hr, 16), hr), :]
            return pltpu.make_async_remote_copy(src_ref=s_ref, dst_ref=region(a, chip_idx, half),
                                                send_sem=send_sems.at[kk], recv_sem=recv_sems.at[kk],
                                                device_id=to, device_id_type=MESH)

        def keep_whole(kk, chip_idx):
            return pltpu.make_async_copy(wsc.at[chip_idx],
                                         out[0].at[:, pl.ds(_aligned(chip_idx * w_shard, LANES), w_shard)],
                                         local_sems.at[kk])

        def small_stores():
            cps = []
            for a in range(1, n_arr):
                hr = shards[a].shape[0] // 2
                for half in range(2):
                    cps.append(pltpu.make_async_copy(small_cast[a - 1].at[pl.ds(half * hr, hr), :],
                                                     region(a, chip, half), local_sems.at[4 + 2 * (a - 1) + half]))
            return cps

        def arrive_and_pass(j):
            ochip = chip ^ j
            for a in range(n_arr):
                kk = n_arr * (j - 1) + a
                remote(kk, a, ochip, c, sibling, False).wait_recv()
                remote(3 * n_arr + kk, a, ochip, c, sibling, False).start()

        def from_sibling(j, a):
            remote(3 * n_arr + n_arr * (j - 1) + a, a, chip ^ j, 1 - c, sibling, False).wait_recv()

        @pl.when((k == 0) & (i == 0))
        def _():
            def cast_rows(half):
                for t in range(half_rows // stage_rows):
                    r0 = pl.multiple_of(half * half_rows + t * stage_rows, stage_rows)
                    pltpu.sync_copy(src[0].at[pl.ds(r0, stage_rows), :], stage)
                    wsc[chip, pl.ds(r0, stage_rows), :] = stage[...].astype(BF16)

            cast_rows(c)
            for j in (1, 2):
                remote(n_arr * (j - 1), 0, chip, c, (*others[j - 1], c), True).start()
            cast_rows(1 - c)
            for a in range(1, n_arr):
                pltpu.sync_copy(src[a], small_stage[a - 1])
                small_cast[a - 1][...] = small_stage[a - 1][...].astype(BF16)
            for j in (1, 2):
                for a in range(1, n_arr):
                    remote(n_arr * (j - 1) + a, a, chip, c, (*others[j - 1], c), True).start()
            keep_whole(0, chip).start()
            for cp in small_stores():
                cp.start()

        @pl.when((k == 1) & (i == 0))
        def _():
            for j in (1, 2):
                remote(n_arr * (j - 1), 0, chip, c, (*others[j - 1], c), True).wait_send()
            for a in range(n_arr):
                remote(n_arr * 2 + a, a, chip, c, (*others[2], c), True).start()
            arrive_and_pass(1)
            arrive_and_pass(2)
            from_sibling(1, 0)
            keep_whole(1, chip ^ 1).start()

        @pl.when((k == 2) & (i == 0))
        def _():
            from_sibling(2, 0)
            keep_whole(2, chip ^ 2).start()
            arrive_and_pass(3)

        @pl.when((k == 3) & (i == 0))
        def _():
            from_sibling(3, 0)
            keep_whole(3, chip ^ 3).start()

        @pl.when(k == 0)
        def _():
            xf = x_ref[...]
            r = lax.rsqrt(jnp.mean(xf * xf, axis=-1, keepdims=True) + EPS)
            h = xf * r * g_ref[...]
            h_scr[i] = h.astype(BF16)
            ht_ref[...] = h.T.astype(BF16)

        proj_ref[...] = jnp.dot(h_scr[i], wsc[order_ref[k]], preferred_element_type=F32).astype(BF16)

        @pl.when((k == 3) & (i == nt - 1))
        def _():
            for j in (1, 2, 3):
                for a in range(1, n_arr):
                    from_sibling(j, a)
            for j in (1, 2, 3):
                for a in range(n_arr):
                    kk = n_arr * (j - 1) + a
                    if a > 0 or j == 3:
                        remote(kk, a, chip, c, (*others[j - 1], c), True).wait_send()
                    remote(3 * n_arr + kk, a, chip ^ j, c, sibling, False).wait_send()
            for kk in range(4):
                keep_whole(kk, chip ^ kk).wait()
            for cp in small_stores():
                cp.wait()

    any_spec = pl.BlockSpec(memory_space=pl.ANY)
    tile = lambda kk, ii: jnp.where(kk == 0, ii, nt - 1)
    grid_spec = pltpu.PrefetchScalarGridSpec(
        num_scalar_prefetch=1, grid=(N_CHIPS, nt),
        in_specs=[pl.BlockSpec((tm, D_MODEL), lambda kk, ii, order: (tile(kk, ii), 0)),
                  pl.BlockSpec((1, D_MODEL), lambda kk, ii, order: (0, 0))] + [any_spec] * n_arr,
        out_specs=[pl.BlockSpec((tm, w_shard), lambda kk, ii, order: (ii, order[kk])),
                   pl.BlockSpec((D_MODEL, tm), lambda kk, ii, order: (0, tile(kk, ii)))] + [any_spec] * n_arr,
        scratch_shapes=[pltpu.VMEM((N_CHIPS, D_MODEL, w_shard), BF16), pltpu.VMEM((nt, tm, D_MODEL), BF16),
                        pltpu.VMEM((stage_rows, w_shard), F32)]
        + [pltpu.VMEM(a.shape, F32) for a in shards[1:]] + [pltpu.VMEM(a.shape, BF16) for a in shards[1:]]
        + [pltpu.SemaphoreType.DMA((6 * n_arr,)), pltpu.SemaphoreType.DMA((6 * n_arr,)),
           pltpu.SemaphoreType.DMA((4 + 2 * (n_arr - 1),))])
    x_, y_, _ = _place()
    order = (2 * x_ + y_) ^ jnp.arange(N_CHIPS, dtype=jnp.int32)
    return pl.pallas_call(
        body, name="in_proj_gather", grid_spec=grid_spec,
        out_shape=[jax.ShapeDtypeStruct((s, D_IN), BF16), jax.ShapeDtypeStruct((D_MODEL, s), BF16)]
        + [jax.ShapeDtypeStruct(sh, BF16) for sh in full_shapes],
        compiler_params=pltpu.CompilerParams(dimension_semantics=("arbitrary", "arbitrary"),
                                             vmem_limit_bytes=VMEM_LIMIT, has_side_effects=True),
    )(order, x, norm_g, *shards)


def _neg_softplus_parts(z):
    zb = z.astype(BF16)
    p = jnp.exp(-jnp.abs(zb))
    return p, jnp.maximum(zb, jnp.zeros_like(zb)) + jnp.log(1.0 + p)


def _split_cat(a, passes):
    parts = []
    rem = a
    for k in range(passes):
        part = rem.astype(BF16)
        parts.append(part)
        if k + 1 < passes:
            rem = rem - part.astype(F32)
    return parts[0] if passes == 1 else jnp.concatenate(parts, axis=1)


def _tri(blk, upper, sign):
    row = lax.broadcasted_iota(jnp.int32, (blk, blk), 0)
    col = lax.broadcasted_iota(jnp.int32, (blk, blk), 1)
    keep = (row <= col) if upper else (row >= col)
    t = jnp.where(keep, sign, 0.0).astype(BF16)
    return t if SCAN_PASSES == 1 else jnp.concatenate([t] * SCAN_PASSES, axis=0)


def _attn_fwd(proj, bq, bk, npairs):
    s = proj.shape[0]
    nq = s // bq
    ratio = bq // bk
    scale = HEAD_DIM ** -0.5
    heads = tuple(range(2 * npairs))
    width = LANES * npairs

    def body(q_ref, k_ref, v_ref, za_ref, o_ref, ya_ref, rs_ref, acc_ref, r_ref, z_ref):
        i = pl.program_id(1)
        lane = lax.broadcasted_iota(jnp.int32, (bq, LANES), 1)
        lo_half = lane < HEAD_DIM
        qm = []
        for pr in range(npairs):
            q = q_ref[:, LANES * pr:LANES * (pr + 1)] * jnp.asarray(scale, BF16)
            zero = jnp.zeros_like(q)
            qm += [jnp.where(lo_half, q, zero), jnp.where(lo_half, zero, q)]
        row = lax.broadcasted_iota(jnp.int32, (bq, bk), 0)
        col = lax.broadcasted_iota(jnp.int32, (bq, bk), 1)
        tneg = _tri(bk, False, -1.0)
        acc_ref[...] = jnp.zeros_like(acc_ref)
        r_ref[...] = jnp.zeros_like(r_ref)
        rs_ref[...] = jnp.full_like(rs_ref, SKIPPED)

        def scores(j):
            ks = pl.multiple_of(j * bk, bk)
            return [_dot_nt(qm[h], k_ref[pl.ds(ks, bk), LANES * (h // 2):LANES * (h // 2 + 1)]) for h in heads]

        for h, zh in enumerate(scores(i * ratio + ratio - 1)):
            z_ref[h] = zh

        def block(j, diag):
            ks = pl.multiple_of(j * bk, bk)
            vj = [v_ref[pl.ds(ks, bk), LANES * pr:LANES * (pr + 1)] for pr in range(npairs)]
            if diag:
                before = (j * bk + col) < (i * bq + row)
            z = [z_ref[h] for h in heads]
            sp = [_neg_softplus_parts(z[h])[1] for h in heads]
            if diag:
                sp = [jnp.where(before, sp[h], 0.0) for h in heads]
            cin = [jnp.dot(_split_cat(sp[h], SCAN_PASSES), tneg, preferred_element_type=F32) for h in heads]
            for h, zh in enumerate(scores(jnp.maximum(j - 1, 0))):
                z_ref[h] = zh
            w = [jnp.exp(z[h] + cin[h]) for h in heads]
            if diag:
                w = [jnp.where(before, w[h], 0.0) for h in heads]
            pv = [jnp.dot(w[h].astype(BF16), vj[h // 2], preferred_element_type=F32) for h in heads]
            r = [r_ref[h] for h in heads]
            for h in heads:
                acc_ref[h] += pv[h] * jnp.exp(r[h])
                r_ref[h] = r[h] + cin[h][:, 0:1]
            for pr in range(npairs):
                rs_ref[pr] = jnp.where(lane == j, r[2 * pr], jnp.where(lane == j + HEAD_DIM, r[2 * pr + 1], rs_ref[pr]))

        for t in range(ratio):
            block(i * ratio + ratio - 1 - t, True)

        def alive(carry):
            jj, r_max = carry
            return (jj < i * ratio) & (r_max > DEAD)

        def loop_body(carry):
            jj, _ = carry
            block(i * ratio - 1 - jj, False)
            return jj + 1, jnp.max(r_ref[...])

        lax.while_loop(alive, loop_body, (0, jnp.max(r_ref[...])))
        for pr in range(npairs):
            cols = slice(LANES * pr, LANES * (pr + 1))
            o = jnp.where(lo_half, acc_ref[2 * pr], acc_ref[2 * pr + 1])
            o_ref[:, cols] = o.astype(BF16)
            za = za_ref[:, cols].astype(F32)
            ya_ref[:, cols] = (o * (za * _sigmoid(za))).astype(BF16)

    n_steps = N_HEADS // (2 * npairs)
    return pl.pallas_call(
        body, name="attn_fwd", grid=(n_steps, nq),
        in_specs=[pl.BlockSpec((bq, width), lambda p, i: (i, n_steps * COL_Q + p)),
                  pl.BlockSpec((s, width), lambda p, i: (0, n_steps * COL_K + p)),
                  pl.BlockSpec((s, width), lambda p, i: (0, n_steps * COL_V + p)),
                  pl.BlockSpec((bq, width), lambda p, i: (i, n_steps * COL_ZA + p))],
        out_specs=[pl.BlockSpec((bq, width), lambda p, i: (i, p)),
                   pl.BlockSpec((bq, width), lambda p, i: (i, p)),
                   pl.BlockSpec((npairs, bq, LANES), lambda p, i: (p, i, 0))],
        out_shape=[jax.ShapeDtypeStruct((s, D_BRANCH), BF16), jax.ShapeDtypeStruct((s, D_BRANCH), BF16),
                   jax.ShapeDtypeStruct((N_HEADS // 2, s, LANES), F32)],
        scratch_shapes=[pltpu.VMEM((2 * npairs, bq, LANES), F32), pltpu.VMEM((2 * npairs, bq, 1), F32),
                        pltpu.VMEM((2 * npairs, bq, bk), F32)],
        compiler_params=_cparams(("parallel", "parallel")),
    )(proj, proj, proj, proj)


_GRAD_COL_SHARDED = (True, True, True, False)
_GRAD_FULL_SHAPES = ((D_MODEL, D_IN), (D_BRANCH, D_MODEL), (D_BRANCH, D_MODEL), (D_MODEL, D_MODEL))
_GRAD_PIECE_SHAPES = tuple((r // 2, w // N_CHIPS) if cs else (r // (2 * N_CHIPS), w)
                           for (r, w), cs in zip(_GRAD_FULL_SHAPES, _GRAD_COL_SHARDED))
_EARLY_IN_DEVS = (4, 5, 6, 7)
_LATE_IN_DEVS = (0, 1, 2, 3)


def _grad_piece(ref, a, dev):
    r, w = _GRAD_PIECE_SHAPES[a]
    if _GRAD_COL_SHARDED[a]:
        return ref.at[pl.ds((dev % 2) * r, r), pl.ds((dev // 2) * w, w)]
    return ref.at[pl.ds(dev * r, r), :]


def _dev_id(dev):
    return (dev // 4, (dev // 2) % 2, dev % 2)


def _scatter_copy(plan, src, slots, send_sems, recv_sems, pos, dst_dev, src_dev):
    a = plan[pos][0]
    n = len(plan)
    return pltpu.make_async_remote_copy(
        src_ref=_grad_piece(src[pos], a, dst_dev), dst_ref=slots[pos].at[src_dev],
        send_sem=send_sems.at[n * dst_dev + pos], recv_sem=recv_sems.at[n * src_dev + pos],
        device_id=_dev_id(dst_dev), device_id_type=MESH)


def _scatter_start(plan, src, slots, send_sems, recv_sems, me):
    for pos, (_, dests) in enumerate(plan):
        for dev in dests:
            @pl.when(me != dev)
            def _():
                _scatter_copy(plan, src, slots, send_sems, recv_sems, pos, dev, me).start()


def _scatter_wait(plan, src, slots, send_sems, recv_sems, me):
    for pos, (_, dests) in enumerate(plan):
        for dev in range(N_DEV):
            @pl.when((me != dev) & (me >= dests[0]) & (me <= dests[-1]))
            def _():
                _scatter_copy(plan, src, slots, send_sems, recv_sems, pos, dev, dev).wait_recv()
        for dev in dests:
            @pl.when(me != dev)
            def _():
                _scatter_copy(plan, src, slots, send_sems, recv_sems, pos, dev, me).wait_send()


def _me():
    return 4 * lax.axis_index("x") + 2 * lax.axis_index("y") + lax.axis_index("c")


_ALL_DEVS = tuple(range(N_DEV))


def _attn_bwd(proj, do, rsave, bq, bk, npairs, g_in):
    plan = ((0, _EARLY_IN_DEVS),)
    grads = (g_in,)
    s = proj.shape[0]
    nq = s // bq
    ratio = bq // bk
    scale = HEAD_DIM ** -0.5
    heads = tuple(range(2 * npairs))
    width = LANES * npairs

    n_steps = N_HEADS // (2 * npairs)
    n_g = len(grads)

    def body(q_ref, k_ref, v_ref, do_ref, rs_ref, *refs):
        g_src = refs[:n_g]
        dq_ref, dk_ref, dv_ref = refs[n_g:n_g + 3]
        g_slots = refs[n_g + 3:2 * n_g + 3]
        dk_acc, dv_acc, dq_acc, e_ref, qmt_ref, domt_ref, rst_ref, send_sems, recv_sems = refs[2 * n_g + 3:]
        i = pl.program_id(1)
        me = _me()

        @pl.when((pl.program_id(0) == 0) & (i == 0))
        def _():
            _scatter_start(plan, g_src, g_slots, send_sems, recv_sems, me)

        lane = lax.broadcasted_iota(jnp.int32, (bq, LANES), 1)
        lo_half = lane < HEAD_DIM
        qm, dom = [], []
        for pr in range(npairs):
            cols = slice(LANES * pr, LANES * (pr + 1))
            q = q_ref[:, cols] * jnp.asarray(scale, BF16)
            zero = jnp.zeros_like(q)
            qm += [jnp.where(lo_half, q, zero), jnp.where(lo_half, zero, q)]
            dout = do_ref[:, cols].astype(F32)
            dom += [jnp.where(lo_half, dout, 0.0), jnp.where(lo_half, 0.0, dout)]
        row = lax.broadcasted_iota(jnp.int32, (bq, bk), 0)
        col = lax.broadcasted_iota(jnp.int32, (bq, bk), 1)
        tneg = _tri(bk, False, -1.0)
        tfwd = _tri(bk, True, 1.0)

        @pl.when(i == 0)
        def _():
            dk_acc[...] = jnp.zeros_like(dk_acc)
            dv_acc[...] = jnp.zeros_like(dv_acc)

        dq_acc[...] = jnp.zeros_like(dq_acc)
        e_ref[...] = jnp.zeros_like(e_ref)
        for h in heads:
            qmt_ref[h] = qm[h].astype(F32).T.astype(BF16)
            domt_ref[h] = dom[h].T
        for pr in range(npairs):
            rst_ref[pr] = rs_ref[pr].T

        def block(j, diag):
            ks = pl.multiple_of(j * bk, bk)
            kj = [k_ref[pl.ds(ks, bk), LANES * pr:LANES * (pr + 1)] for pr in range(npairs)]
            vj = [v_ref[pl.ds(ks, bk), LANES * pr:LANES * (pr + 1)] for pr in range(npairs)]
            if diag:
                before = (j * bk + col) < (i * bq + row)
            z = [_dot_nt(qm[h], kj[h // 2]) for h in heads]
            dost = [(domt_ref[h] * jnp.exp(rst_ref[h // 2, pl.ds(j + HEAD_DIM * (h % 2), 1), :])).astype(BF16)
                    for h in heads]
            er = [jnp.exp(jnp.sum(jnp.where(lane == j + HEAD_DIM * (h % 2), rs_ref[h // 2], 0.0), axis=-1,
                                  keepdims=True)) for h in heads]
            dos = [(dom[h] * er[h]).astype(BF16) for h in heads]
            dw = [_dot_nt(dos[h], vj[h // 2]) for h in heads]
            psp = [_neg_softplus_parts(z[h]) for h in heads]
            sp = [psp[h][1] for h in heads]
            if diag:
                sp = [jnp.where(before, sp[h], 0.0) for h in heads]
            cin = [jnp.dot(_split_cat(sp[h], SCAN_PASSES), tneg, preferred_element_type=F32) for h in heads]
            w = [jnp.exp(z[h] + cin[h]) for h in heads]
            if diag:
                w = [jnp.where(before, w[h], 0.0) for h in heads]
            e =[dw[h] * w[h] for h in heads]
            eincl = [jnp.dot(_split_cat(e[h], SCAN_PASSES), tfwd, preferred_element_type=F32) + e_ref[h]
                     for h in heads]
            dz = []
            for h in heads:
                p = psp[h][0]
                beta = jnp.where(z[h] >= 0.0, 1.0, p) / (1.0 + p)
                d = e[h] - beta * eincl[h]
                dz.append((jnp.where(before, d, 0.0) if diag else d).astype(BF16))
            wb = [w[h].astype(BF16) for h in heads]
            for h in heads:
                e_ref[h] = eincl[h][:, bk - 1:bk]
                dq_acc[h] += jnp.dot(dz[h], kj[h // 2], preferred_element_type=F32)
            for pr in range(npairs):
                rows = slice(LANES * pr, LANES * (pr + 1))
                h0, h1 = 2 * pr, 2 * pr + 1
                dk_acc[rows, pl.ds(ks, bk)] += (jnp.dot(qmt_ref[h0], dz[h0], preferred_element_type=F32)
                                                 + jnp.dot(qmt_ref[h1], dz[h1], preferred_element_type=F32))
                dv_acc[rows, pl.ds(ks, bk)] += (jnp.dot(dost[h0], wb[h0], preferred_element_type=F32)
                                                 + jnp.dot(dost[h1], wb[h1], preferred_element_type=F32))

        def loop_body(j, carry):
            block(j, False)
            return carry

        block_of_lane = lane & (HEAD_DIM - 1)
        live = jnp.max(rs_ref[...], axis=0) > DEAD
        first_live = jnp.min(jnp.where(live, block_of_lane, nq * ratio))
        lax.fori_loop(jnp.minimum(first_live, i * ratio), i * ratio, loop_body, 0)
        for t in range(ratio):
            block(i * ratio + t, True)
        for pr in range(npairs):
            dq = jnp.where(lo_half, dq_acc[2 * pr], dq_acc[2 * pr + 1]) * scale
            dq_ref[:, LANES * pr:LANES * (pr + 1)] = dq.astype(BF16)

        @pl.when(i == nq - 1)
        def _():
            dk_ref[...] = dk_acc[...].T.astype(BF16)
            dv_ref[...] = dv_acc[...].T.astype(BF16)

        @pl.when((pl.program_id(0) == n_steps - 1) & (i == nq - 1))
        def _():
            _scatter_wait(plan, g_src, g_slots, send_sems, recv_sems, me)

    any_spec = pl.BlockSpec(memory_space=pl.ANY)
    return pl.pallas_call(
        body, name="attn_bwd", grid=(n_steps, nq),
        in_specs=[pl.BlockSpec((bq, width), lambda p, i: (i, n_steps * COL_Q + p)),
                  pl.BlockSpec((s, width), lambda p, i: (0, n_steps * COL_K + p)),
                  pl.BlockSpec((s, width), lambda p, i: (0, n_steps * COL_V + p)),
                  pl.BlockSpec((bq, width), lambda p, i: (i, p)),
                  pl.BlockSpec((npairs, bq, LANES), lambda p, i: (p, i, 0))] + [any_spec] * n_g,
        out_specs=[pl.BlockSpec((bq, width), lambda p, i: (i, p)),
                   pl.BlockSpec((s, width), lambda p, i: (0, p)),
                   pl.BlockSpec((s, width), lambda p, i: (0, p))] + [any_spec] * n_g,
        out_shape=[jax.ShapeDtypeStruct((s, D_BRANCH), BF16)] * 3
        + [jax.ShapeDtypeStruct((N_DEV,) + _GRAD_PIECE_SHAPES[a], BF16) for a, _ in plan],
        scratch_shapes=[pltpu.VMEM((width, s), F32), pltpu.VMEM((width, s), F32),
                        pltpu.VMEM((2 * npairs, bq, LANES), F32), pltpu.VMEM((2 * npairs, bq, 1), F32),
                        pltpu.VMEM((2 * npairs, LANES, bq), BF16), pltpu.VMEM((2 * npairs, LANES, bq), F32),
                        pltpu.VMEM((npairs, LANES, bq), F32),
                        pltpu.SemaphoreType.DMA((N_DEV * n_g,)), pltpu.SemaphoreType.DMA((N_DEV * n_g,))],
        compiler_params=pltpu.CompilerParams(dimension_semantics=("arbitrary", "arbitrary"),
                                             vmem_limit_bytes=VMEM_LIMIT, has_side_effects=True),
    )(proj, proj, proj, do, rsave, *grads)


def _group_avg_matrix():
    a = lax.broadcasted_iota(jnp.int32, (LANES, LANES), 0) >> GROUP_SHIFT
    b = lax.broadcasted_iota(jnp.int32, (LANES, LANES), 1) >> GROUP_SHIFT
    return jnp.where(a == b, 1.0 / GROUP_DIM, 0.0).astype(BF16)


def _group_mean(a, avg):
    parts = [_split_dot(a[:, LANES * k:LANES * (k + 1)], avg, 3) for k in range(D_BRANCH // LANES)]
    return jnp.concatenate(parts, axis=1)


def _sgu_forward_parts(ub, vb, ln_g, ln_b, avg):
    ug, dug = _gelu_and_grad(ub)
    vg, dvg = _gelu_and_grad(vb)
    mu = _group_mean(vg, avg)
    d = vg - mu
    var = _group_mean(d * d, avg)
    rstd = lax.rsqrt(var + EPS)
    vhat = d * rstd
    vn = vhat * ln_g + ln_b
    return ug, dug, dvg, rstd, vhat, vn


def _sgu_mix(w_ref, src_bf16, n_chunks):
    lane = lax.broadcasted_iota(jnp.int32, (SGU_CHUNK, LANES), 1)
    lo_half = lane < GROUP_DIM
    rows = []
    for n in range(n_chunks):
        slabs = []
        for a in range(D_BRANCH // LANES):
            blk = src_bf16[SGU_CHUNK * n:SGU_CHUNK * (n + 1), LANES * a:LANES * (a + 1)]
            zero = jnp.zeros_like(blk)
            m0 = jnp.dot(w_ref[2 * a], jnp.where(lo_half, blk, zero), preferred_element_type=F32)
            m1 = jnp.dot(w_ref[2 * a + 1], jnp.where(lo_half, zero, blk), preferred_element_type=F32)
            slabs.append(m0 + m1)
        rows.append(jnp.concatenate(slabs, axis=1))
    return jnp.concatenate(rows, axis=0)


def _sgu_fwd(proj, ln_g, ln_b, w_mask, bias_full):
    s = proj.shape[0]
    tm = min(512, s)
    n_chunks = tm // SGU_CHUNK

    def body(ub_ref, vb_ref, zb_ref, g_ref, b_ref, w_ref, bias_ref, yb_ref):
        avg = _group_avg_matrix()
        ug, _, _, _, _, vn = _sgu_forward_parts(ub_ref[...].astype(F32), vb_ref[...].astype(F32),
                                                g_ref[...], b_ref[...], avg)
        mixed = _sgu_mix(w_ref, vn.astype(BF16), n_chunks) + jnp.concatenate([bias_ref[...]] * n_chunks, axis=0)
        zb = zb_ref[...].astype(F32)
        yb_ref[...] = (ug * mixed * (zb * _sigmoid(zb))).astype(BF16)

    col = lambda c: pl.BlockSpec((tm, D_BRANCH), lambda i: (i, c))
    full = lambda shape: pl.BlockSpec(shape, lambda i: (0,) * len(shape))
    return pl.pallas_call(
        body, name="sgu_fwd", grid=(s // tm,),
        in_specs=[col(COL_UB), col(COL_VB), col(COL_ZB), full((1, D_BRANCH)), full((1, D_BRANCH)),
                  full((N_GROUPS, SGU_CHUNK, SGU_CHUNK)), full((SGU_CHUNK, D_BRANCH))],
        out_specs=pl.BlockSpec((tm, D_BRANCH), lambda i: (i, 0)),
        out_shape=jax.ShapeDtypeStruct((s, D_BRANCH), BF16),
        compiler_params=_cparams(("parallel",)),
    )(proj, proj, proj, ln_g, ln_b, w_mask, bias_full)


def _sgu_bwd(proj, dyb, ln_g, ln_b, w_mask, w_mask_t, bias_full, grads):
    s = proj.shape[0]
    tm = min(512, s)
    n_chunks = tm // SGU_CHUNK
    n_steps = s // tm
    plan = ((1, _ALL_DEVS), (2, _ALL_DEVS), (3, _ALL_DEVS))
    n_g = len(plan)

    def body(ub_ref, vb_ref, zb_ref, dyb_ref, g_ref, b_ref, w_ref, wt_ref, bias_ref, *refs):
        g_src = refs[:n_g]
        dsgu_ref, dw_ref, db_ref, dg_ref, dbeta_ref = refs[n_g:n_g + 5]
        g_slots = refs[n_g + 5:2 * n_g + 5]
        dmix_acc, send_sems, recv_sems = refs[2 * n_g + 5:]
        i = pl.program_id(0)
        me = _me()

        @pl.when(i == 0)
        def _():
            _scatter_start(plan, g_src, g_slots, send_sems, recv_sems, me)
            dw_ref[...] = jnp.zeros_like(dw_ref)
            dg_ref[...] = jnp.zeros_like(dg_ref)
            dbeta_ref[...] = jnp.zeros_like(dbeta_ref)
            dmix_acc[...] = jnp.zeros_like(dmix_acc)

        avg = _group_avg_matrix()
        ln_gv = g_ref[...]
        ug, dug, dvg, rstd, vhat, vn = _sgu_forward_parts(ub_ref[...].astype(F32), vb_ref[...].astype(F32),
                                                          ln_gv, b_ref[...], avg)
        vnb = vn.astype(BF16)
        mixed = _sgu_mix(w_ref, vnb, n_chunks) + jnp.concatenate([bias_ref[...]] * n_chunks, axis=0)
        zb = zb_ref[...].astype(F32)
        sg = _sigmoid(zb)
        sz = zb * sg
        dsz = sg * (1.0 + zb * (1.0 - sg))
        dy = dyb_ref[...].astype(F32)
        dmixed = dy * ug * sz
        du = dy * mixed * sz * dug
        dzb = dy * ug * mixed * dsz
        dmb = dmixed.astype(BF16)
        dvn = _sgu_mix(wt_ref, dmb, n_chunks)

        lane = lax.broadcasted_iota(jnp.int32, (SGU_CHUNK, LANES), 1)
        lo_half = lane < GROUP_DIM
        dm_sum = None
        for n in range(n_chunks):
            rows = slice(SGU_CHUNK * n, SGU_CHUNK * (n + 1))
            dm_sum = dmixed[rows] if dm_sum is None else dm_sum + dmixed[rows]
            for a in range(D_BRANCH // LANES):
                cols = slice(LANES * a, LANES * (a + 1))
                dblk = dmb[rows, cols]
                vblk = vnb[rows, cols]
                zero = jnp.zeros_like(dblk)
                dw_ref[2 * a] += _dot_nt(jnp.where(lo_half, dblk, zero), vblk)
                dw_ref[2 * a + 1] += _dot_nt(jnp.where(lo_half, zero, dblk), vblk)
        dmix_acc[...] += dm_sum

        dg_ref[...] += jnp.sum(dvn * vhat, axis=0, keepdims=True)
        dbeta_ref[...] += jnp.sum(dvn, axis=0, keepdims=True)
        dvh = dvn * ln_gv
        m1 = _group_mean(dvh, avg)
        m2 = _group_mean(dvh * vhat, avg)
        dv = rstd * (dvh - m1 - vhat * m2) * dvg
        dsgu_ref[:, 0:D_BRANCH] = du.astype(BF16)
        dsgu_ref[:, D_BRANCH:2 * D_BRANCH] = dv.astype(BF16)
        dsgu_ref[:, 2 * D_BRANCH:3 * D_BRANCH] = dzb.astype(BF16)

        @pl.when(i == n_steps - 1)
        def _():
            pos = lax.broadcasted_iota(jnp.int32, (SGU_CHUNK, SGU_CHUNK), 0) >> GROUP_SHIFT
            src = lax.broadcasted_iota(jnp.int32, (SGU_CHUNK, SGU_CHUNK), 1) >> GROUP_SHIFT
            keep = src <= pos
            for g in range(N_GROUPS):
                dw_ref[g] = jnp.where(keep, dw_ref[g], 0.0)
            grp = lax.broadcasted_iota(jnp.int32, (D_BRANCH, LANES), 0) >> GROUP_SHIFT
            sel = (grp == lax.broadcasted_iota(jnp.int32, (D_BRANCH, LANES), 1)).astype(BF16)
            db_ref[...] = _split_dot(dmix_acc[...], sel, 3)
            _scatter_wait(plan, g_src, g_slots, send_sems, recv_sems, me)

    col = lambda c: pl.BlockSpec((tm, D_BRANCH), lambda i: (i, c))
    full = lambda shape: pl.BlockSpec(shape, lambda i: (0,) * len(shape))
    any_spec = pl.BlockSpec(memory_space=pl.ANY)
    return pl.pallas_call(
        body, name="sgu_bwd", grid=(n_steps,),
        in_specs=[col(COL_UB), col(COL_VB), col(COL_ZB), pl.BlockSpec((tm, D_BRANCH), lambda i: (i, 0)),
                  full((1, D_BRANCH)), full((1, D_BRANCH)),
                  full((N_GROUPS, SGU_CHUNK, SGU_CHUNK)), full((N_GROUPS, SGU_CHUNK, SGU_CHUNK)),
                  full((SGU_CHUNK, D_BRANCH))] + [any_spec] * n_g,
        out_specs=[pl.BlockSpec((tm, 3 * D_BRANCH), lambda i: (i, 0)),
                   full((N_GROUPS, SGU_CHUNK, SGU_CHUNK)), full((SGU_CHUNK, LANES)),
                   full((1, D_BRANCH)), full((1, D_BRANCH))] + [any_spec] * n_g,
        out_shape=[jax.ShapeDtypeStruct((s, 3 * D_BRANCH), BF16),
                   jax.ShapeDtypeStruct((N_GROUPS, SGU_CHUNK, SGU_CHUNK), F32),
                   jax.ShapeDtypeStruct((SGU_CHUNK, LANES), F32),
                   jax.ShapeDtypeStruct((1, D_BRANCH), F32), jax.ShapeDtypeStruct((1, D_BRANCH), F32)]
        + [jax.ShapeDtypeStruct((N_DEV,) + _GRAD_PIECE_SHAPES[a], BF16) for a, _ in plan],
        scratch_shapes=[pltpu.VMEM((SGU_CHUNK, D_BRANCH), F32),
                        pltpu.SemaphoreType.DMA((N_DEV * n_g,)), pltpu.SemaphoreType.DMA((N_DEV * n_g,))],
        compiler_params=pltpu.CompilerParams(dimension_semantics=("arbitrary",), vmem_limit_bytes=VMEM_LIMIT,
                                             has_side_effects=True),
    )(proj, proj, proj, dyb, ln_g, ln_b, w_mask, w_mask_t, bias_full, *grads)


def _mid(proj, ya, yb, o, x, target, final_g, w_up_a, w_up_b, w_out):
    s = x.shape[0]
    tm = min(256, s)
    n_steps = s // tm
    half = D_MODEL // 2

    def body(ya_ref, yb_ref, o_ref, za_ref, ga0_ref, ga1_ref, gb0_ref, gb1_ref, x_ref, t_ref, gf_ref,
             wa_ref, wb_ref, wo_ref,
             dzg_ref, do_ref, dyb_ref, dx2_ref, gwo_ref, gwa_ref, gwb_ref, loss_ref, dgf_ref,
             acc_o, acc_a, acc_b):
        i = pl.program_id(0)

        @pl.when(i == 0)
        def _():
            acc_o[...] = jnp.zeros_like(acc_o)
            acc_a[...] = jnp.zeros_like(acc_a)
            acc_b[...] = jnp.zeros_like(acc_b)
            loss_ref[...] = jnp.zeros_like(loss_ref)
            dgf_ref[...] = jnp.zeros_like(dgf_ref)

        ya_v = ya_ref[...]
        yb_v = yb_ref[...]
        pa = jnp.dot(ya_v, wa_ref[...], preferred_element_type=F32)
        pb = jnp.dot(yb_v, wb_ref[...], preferred_element_type=F32)
        sa = _sigmoid(jnp.concatenate([ga0_ref[...], ga1_ref[...]], axis=1).astype(F32))
        sb = _sigmoid(jnp.concatenate([gb0_ref[...], gb1_ref[...]], axis=1).astype(F32))
        merged = (sa * pa + sb * pb).astype(BF16)
        x2 = x_ref[...] + jnp.dot(merged, wo_ref[...], preferred_element_type=F32)
        r2 = lax.rsqrt(jnp.mean(x2 * x2, axis=-1, keepdims=True) + EPS)
        xh = x2 * r2
        gf = gf_ref[...]
        diff = xh * gf - t_ref[...]
        loss_ref[...] += 0.5 * jnp.sum(jnp.mean(diff * diff, axis=-1, keepdims=True))
        dy = diff * (1.0 / D_MODEL)
        dgf_ref[...] += jnp.sum(dy * xh, axis=0, keepdims=True)
        dyg = dy * gf
        dx2 = r2 * (dyg - xh * jnp.mean(dyg * xh, axis=-1, keepdims=True))
        dx2_ref[...] = dx2
        dx2b = dx2.astype(BF16)
        dmerged = _dot_nt(dx2b, wo_ref[...])
        acc_o[...] += _dot_tn(merged, dx2b)
        dpa = dmerged * sa
        dpb = dmerged * sb
        dzg_ref[:, D_BRANCH:D_BRANCH + D_MODEL] = (dpa * pa * (1.0 - sa)).astype(BF16)
        dzg_ref[:, D_BRANCH + D_MODEL:D_BRANCH + 2 * D_MODEL] = (dpb * pb * (1.0 - sb)).astype(BF16)
        dpab = dpa.astype(BF16)
        dpbb = dpb.astype(BF16)
        acc_a[...] += _dot_tn(ya_v, dpab)
        acc_b[...] += _dot_tn(yb_v, dpbb)
        dya = _dot_nt(dpab, wa_ref[...])
        dyb_ref[...] = _dot_nt(dpbb, wb_ref[...]).astype(BF16)
        za = za_ref[...].astype(F32)
        sg = _sigmoid(za)
        do_ref[...] = (dya * (za * sg)).astype(BF16)
        dzg_ref[:, 0:D_BRANCH] = (dya * o_ref[...].astype(F32) * (sg * (1.0 + za * (1.0 - sg)))).astype(BF16)

        @pl.when(i == n_steps - 1)
        def _():
            gwo_ref[...] = acc_o[...].astype(BF16)
            gwa_ref[...] = acc_a[...].astype(BF16)
            gwb_ref[...] = acc_b[...].astype(BF16)

    tok = lambda w: pl.BlockSpec((tm, w), lambda i: (i, 0))
    col = lambda c: pl.BlockSpec((tm, half), lambda i: (i, c))
    full = lambda shape: pl.BlockSpec(shape, lambda i: (0,) * len(shape))
    return pl.pallas_call(
        body, name="mid", grid=(n_steps,),
        in_specs=[tok(D_BRANCH), tok(D_BRANCH), tok(D_BRANCH), col(COL_ZA), col(COL_GA), col(COL_GA + 1),
                  col(COL_GB), col(COL_GB + 1), tok(D_MODEL), tok(D_MODEL), full((1, D_MODEL)),
                  full((D_BRANCH, D_MODEL)), full((D_BRANCH, D_MODEL)), full((D_MODEL, D_MODEL))],
        out_specs=[tok(D_BRANCH + 2 * D_MODEL), tok(D_BRANCH), tok(D_BRANCH), tok(D_MODEL),
                   full((D_MODEL, D_MODEL)), full((D_BRANCH, D_MODEL)), full((D_BRANCH, D_MODEL)),
                   full((8, LANES)), full((1, D_MODEL))],
        out_shape=[jax.ShapeDtypeStruct((s, D_BRANCH + 2 * D_MODEL), BF16),
                   jax.ShapeDtypeStruct((s, D_BRANCH), BF16), jax.ShapeDtypeStruct((s, D_BRANCH), BF16),
                   jax.ShapeDtypeStruct((s, D_MODEL), F32),
                   jax.ShapeDtypeStruct((D_MODEL, D_MODEL), BF16),
                   jax.ShapeDtypeStruct((D_BRANCH, D_MODEL), BF16), jax.ShapeDtypeStruct((D_BRANCH, D_MODEL), BF16),
                   jax.ShapeDtypeStruct((8, LANES), F32), jax.ShapeDtypeStruct((1, D_MODEL), F32)],
        scratch_shapes=[pltpu.VMEM((D_MODEL, D_MODEL), F32), pltpu.VMEM((D_BRANCH, D_MODEL), F32),
                        pltpu.VMEM((D_BRANCH, D_MODEL), F32)],
        compiler_params=_cparams(("arbitrary",)),
    )(ya, yb, o, proj, proj, proj, proj, proj, x, target, final_g, w_up_a, w_up_b, w_out)


def _dwin_piece(ht, piece, tile_of, prev):
    s = ht.shape[1]
    n_tiles = piece.shape[1] // D_BRANCH

    def body(ht_ref, p_ref, *rest):
        out_ref = rest[-1]
        out_ref[...] = jnp.dot(ht_ref[...], p_ref[...], preferred_element_type=F32).astype(BF16)

    in_specs = [pl.BlockSpec((D_MODEL, s), lambda j: (0, 0)), pl.BlockSpec((s, D_BRANCH), lambda j: (0, j))]
    args = [ht, piece]
    aliases = {}
    if prev is not None:
        in_specs.append(pl.BlockSpec(memory_space=pl.ANY))
        args.append(prev)
        aliases = {2: 0}
    return pl.pallas_call(
        body, name="dwin_piece", grid=(n_tiles,),
        in_specs=in_specs,
        out_specs=pl.BlockSpec((D_MODEL, D_BRANCH), lambda j: (0, tile_of(j))),
        out_shape=jax.ShapeDtypeStruct((D_MODEL, D_IN), BF16),
        input_output_aliases=aliases,
        compiler_params=_cparams(("parallel",)),
    )(*args)


def _dh_dx(pieces, w_in, x, norm_g, dx2, g_in):
    s = x.shape[0]
    tm = min(256, s)
    n_steps = s // tm
    arrays = []
    for arr, _, _, _ in pieces:
        if not any(arr is a for a in arrays):
            arrays.append(arr)
    n_arr = len(arrays)
    plan = [([k for k, a in enumerate(arrays) if a is arr][0], wcol, off, width) for arr, wcol, off, width in pieces]

    def body(*refs):
        p_refs = refs[:n_arr]
        w_ref, x_ref, g_ref, dx2_ref, gin_ref, dx_ref, dg_ref, late_ref, send_sems, recv_sems = refs[n_arr:]
        x_, y_, c_ = lax.axis_index("x"), lax.axis_index("y"), lax.axis_index("c")
        me = 4 * x_ + 2 * y_ + c_

        def late(dst_dev, src_dev):
            return pltpu.make_async_remote_copy(
                src_ref=_grad_piece(gin_ref, 0, dst_dev), dst_ref=late_ref.at[src_dev],
                send_sem=send_sems.at[dst_dev], recv_sem=recv_sems.at[src_dev],
                device_id=_dev_id(dst_dev), device_id_type=MESH)

        @pl.when(pl.program_id(0) == 0)
        def _():
            dg_ref[...] = jnp.zeros_like(dg_ref)
            for dev in _LATE_IN_DEVS:
                @pl.when(me != dev)
                def _():
                    late(dev, me).start()

        dh = None
        for k, wcol, off, width in plan:
            d = _dot_nt(p_refs[k][:, off:off + width], w_ref[:, wcol:wcol + width])
            dh = d if dh is None else dh + d
        xf = x_ref[...]
        r = lax.rsqrt(jnp.mean(xf * xf, axis=-1, keepdims=True) + EPS)
        xh = xf * r
        dg_ref[...] += jnp.sum(dh * xh, axis=0, keepdims=True)
        dhg = dh * g_ref[...]
        dx_ref[...] = r * (dhg - xh * jnp.mean(dhg * xh, axis=-1, keepdims=True)) + dx2_ref[...]

        @pl.when(pl.program_id(0) == n_steps - 1)
        def _():
            for dev in range(N_DEV):
                @pl.when((me != dev) & (me <= _LATE_IN_DEVS[-1]))
                def _():
                    late(dev, dev).wait_recv()
            for dev in _LATE_IN_DEVS:
                @pl.when(me != dev)
                def _():
                    late(dev, me).wait_send()

    tok = lambda w: pl.BlockSpec((tm, w), lambda i: (i, 0))
    full = lambda shape: pl.BlockSpec(shape, lambda i: (0,) * len(shape))
    any_spec = pl.BlockSpec(memory_space=pl.ANY)
    return pl.pallas_call(
        body, name="dh_dx", grid=(n_steps,),
        in_specs=[tok(a.shape[1]) for a in arrays] + [full((D_MODEL, D_IN)), tok(D_MODEL), full((1, D_MODEL)),
                                                      tok(D_MODEL), any_spec],
        out_specs=[tok(D_MODEL), full((1, D_MODEL)), any_spec],
        out_shape=[jax.ShapeDtypeStruct((s, D_MODEL), F32), jax.ShapeDtypeStruct((1, D_MODEL), F32),
                   jax.ShapeDtypeStruct((N_DEV,) + _GRAD_PIECE_SHAPES[0], BF16)],
        scratch_shapes=[pltpu.SemaphoreType.DMA((N_DEV,)), pltpu.SemaphoreType.DMA((N_DEV,))],
        compiler_params=pltpu.CompilerParams(dimension_semantics=("arbitrary",), vmem_limit_bytes=VMEM_LIMIT,
                                             has_side_effects=True),
    )(*arrays, w_in, x, norm_g, dx2, g_in)


def _adamw(w, g, m, v):
    rows, cols = w.shape
    tr = max(t for t in range(8, 257, 8) if rows % t == 0)
    c1 =1.0 - ADAM_B1 ** ADAM_STEP
    c2 = 1.0 - ADAM_B2 ** ADAM_STEP

    def body(w_ref, g_ref, m_ref, v_ref, g_out_ref, d_ref, nm_ref, nv_ref):
        gv = g_ref[...]
        g_out_ref[...] = gv
        nm = ADAM_B1 * m_ref[...] + (1.0 - ADAM_B1) * gv
        nv = ADAM_B2 * v_ref[...] + (1.0 - ADAM_B2) * (gv * gv)
        d_ref[...] = -ADAM_LR * ((nm / c1) / (jnp.sqrt(nv / c2) + ADAM_EPS) + ADAM_WD * w_ref[...])
        nm_ref[...] = nm
        nv_ref[...] = nv

    spec = pl.BlockSpec((tr, cols), lambda i: (i, 0))
    return pl.pallas_call(
        body, name="adamw", grid=(rows // tr,),
        in_specs=[spec] * 4, out_specs=[spec] * 4,
        out_shape=[jax.ShapeDtypeStruct((rows, cols), F32)] * 4,
        compiler_params=_cparams(("parallel",)),
    )(w, g, m, v)


def _place():
    x, y, c = lax.axis_index("x"), lax.axis_index("y"), lax.axis_index("c")
    return x, y, c


def _gather_weights(w_in, w_up_a, w_up_b, w_out):
    shards = (w_in, w_up_a, w_up_b, w_out)
    n_arr = len(shards)
    col_sharded = (True, True, True, False)
    full_shapes = ((D_MODEL, D_IN), (D_BRANCH, D_MODEL), (D_BRANCH, D_MODEL), (D_MODEL, D_MODEL))

    def body(*refs):
        src = refs[:n_arr]
        out = refs[n_arr:2 * n_arr]
        stage = refs[2 * n_arr:3 * n_arr]
        cast = refs[3 * n_arr:4 * n_arr]
        send_sems, recv_sems, local_sems = refs[4 * n_arr:]
        x, y, c = _place()
        chip = 2 * x + y
        sibling = (x, y, 1 - c)
        others = [(1 - x, y), (x, 1 - y), (1 - x, 1 - y)]

        def region(a, chip_idx, half):
            r, w = shards[a].shape
            hr = r // 2
            if col_sharded[a]:
                return out[a].at[pl.ds(_aligned(half * hr, 16), hr), pl.ds(_aligned(chip_idx * w, LANES), w)]
            return out[a].at[pl.ds(_aligned(chip_idx * r + half * hr, 16), hr), :]

        loads = [pltpu.make_async_copy(src[a], stage[a], local_sems.at[a]) for a in range(n_arr)]
        for cp in loads:
            cp.start()
        for a in range(n_arr):
            loads[a].wait()
            cast[a][...] = stage[a][...].astype(BF16)
        stores = []
        for a in range(n_arr):
            hr = shards[a].shape[0] // 2
            for half in range(2):
                cp = pltpu.make_async_copy(cast[a].at[pl.ds(half * hr, hr), :], region(a, chip, half),
                                           local_sems.at[n_arr + 2 * a + half])
                cp.start()
                stores.append(cp)

        def remote(k, a, chip_idx, half, to, from_vmem):
            hr = shards[a].shape[0] // 2
            s_ref = cast[a].at[pl.ds(_aligned(half * hr, 16), hr), :] if from_vmem else region(a, chip_idx, half)
            return pltpu.make_async_remote_copy(src_ref=s_ref, dst_ref=region(a, chip_idx, half),
                                                send_sem=send_sems.at[k], recv_sem=recv_sems.at[k],
                                                device_id=to, device_id_type=MESH)

        first = []
        for j, (ox, oy) in enumerate(others):
            for a in range(n_arr):
                cp = remote(n_arr * j + a, a, chip, c, (ox, oy, c), True)
                cp.start()
                first.append(cp)
        passed = []
        for j, (ox, oy) in enumerate(others):
            ochip = 2 * ox + oy
            for a in range(n_arr):
                k = n_arr * j + a
                remote(k, a, ochip, c, sibling, False).wait_recv()
                cp = remote(3 * n_arr + k, a, ochip, c, sibling, False)
                cp.start()
                passed.append(cp)
        for j, (ox, oy) in enumerate(others):
            ochip = 2 * ox + oy
            for a in range(n_arr):
                remote(3 * n_arr + n_arr * j + a, a, ochip, 1 - c, sibling, False).wait_recv()
        for cp in first + passed:
            cp.wait_send()
        for cp in stores:
            cp.wait()

    any_spec = pl.BlockSpec(memory_space=pl.ANY)
    return pl.pallas_call(
        body, name="gather_weights",
        in_specs=[any_spec] * n_arr, out_specs=[any_spec] * n_arr,
        out_shape=[jax.ShapeDtypeStruct(sh, BF16) for sh in full_shapes],
        scratch_shapes=[pltpu.VMEM(a.shape, F32) for a in shards] + [pltpu.VMEM(a.shape, BF16) for a in shards]
        + [pltpu.SemaphoreType.DMA((6 * n_arr,)), pltpu.SemaphoreType.DMA((6 * n_arr,)),
           pltpu.SemaphoreType.DMA((3 * n_arr,))],
        compiler_params=pltpu.CompilerParams(vmem_limit_bytes=VMEM_LIMIT, has_side_effects=True),
    )(*shards)


def _reduce_grads_tail(grads, g_small, early_slots, late_in_slots):
    n_big = len(grads)
    n_arr = n_big + 1
    shard_shapes = [(2 * r, w) for r, w in _GRAD_PIECE_SHAPES]
    small_piece = (SMALL_PIECE, LANES)

    def body(*refs):
        src = refs[:n_arr]
        early = refs[n_arr:n_arr + n_big]
        late_in = refs[n_arr + n_big]
        n_in = n_arr + n_big + 1
        out = refs[n_in:n_in + n_arr]
        slots = refs[n_in + n_arr:n_in + 2 * n_arr]
        sums = refs[n_in + 2 * n_arr:n_in + 3 * n_arr]
        send1, recv1, send2, recv2, local_sems = refs[n_in + 3 * n_arr:]
        x, y, c = _place()
        me = 4 * x + 2 * y + c

        def piece_of(a, dev):
            return src[a].at[dev] if a == n_big else _grad_piece(src[a], a, dev)

        def late(a, dst_dev, src_dev):
            return pltpu.make_async_remote_copy(
                src_ref=piece_of(a, dst_dev), dst_ref=slots[a].at[src_dev],
                send_sem=send1.at[n_arr * dst_dev + a], recv_sem=recv1.at[n_arr * src_dev + a],
                device_id=_dev_id(dst_dev), device_id_type=MESH)

        def late_arrays(dev):
            return (n_big,)

        def load(a, dev, received):
            return pltpu.make_async_copy(received.at[dev], slots[a].at[dev], local_sems.at[n_arr * dev + a])

        def own(a, dev):
            return pltpu.make_async_copy(piece_of(a, dev), slots[a].at[dev], local_sems.at[n_arr * dev + a])

        for dev in range(N_DEV):
            @pl.when(me == dev)
            def _():
                received = [early[0] if dev in _EARLY_IN_DEVS else late_in] + list(early[1:])
                for a in range(n_arr):
                    own(a, dev).start()
                for peer in range(N_DEV):
                    if peer != dev:
                        for a in late_arrays(peer):
                            late(a, peer, dev).start()
                        for a in range(n_big):
                            load(a, peer, received[a]).start()
                for a in range(n_arr):
                    own(a, dev).wait()
                for peer in range(N_DEV):
                    if peer != dev:
                        for a in late_arrays(dev):
                            late(a, dev, peer).wait_recv()
                        for a in range(n_big):
                            load(a, peer, received[a]).wait()

        for a in range(n_arr):
            rows = slots[a].shape[1]
            step = 64 if rows % 64 == 0 else 8

            def add_rows(t, carry, a=a, step=step):
                r0 = pl.multiple_of(t * step, step)
                total = slots[a][0, pl.ds(r0, step), :].astype(F32)
                for dev in range(1, N_DEV):
                    total = total + slots[a][dev, pl.ds(r0, step), :].astype(F32)
                sums[a][pl.ds(r0, step), :] = total
                return carry

            lax.fori_loop(0, rows // step, add_rows, 0)

        shares = []
        keeps = []
        for a in range(n_big):
            r, w = _GRAD_PIECE_SHAPES[a]
            dst = out[a].at[pl.ds(pl.multiple_of(c * r, 8), r), :]
            cp = pltpu.make_async_remote_copy(src_ref=sums[a], dst_ref=dst, send_sem=send2.at[a], recv_sem=recv2.at[a],
                                              device_id=(x, y, 1 - c), device_id_type=MESH)
            cp.start()
            shares.append(cp)
            kp = pltpu.make_async_copy(sums[a], dst, local_sems.at[N_DEV * n_arr + a])
            kp.start()
            keeps.append(kp)
        kp = pltpu.make_async_copy(sums[n_big], out[n_big].at[me], local_sems.at[N_DEV * n_arr + n_big])
        kp.start()
        keeps.append(kp)

        def small_share(dst_dev, src_dev):
            return pltpu.make_async_remote_copy(src_ref=sums[n_big], dst_ref=out[n_big].at[src_dev],
                                                send_sem=send2.at[n_big + dst_dev], recv_sem=recv2.at[n_big + src_dev],
                                                device_id=_dev_id(dst_dev), device_id_type=MESH)

        for dev in range(N_DEV):
            @pl.when(me != dev)
            def _():
                small_share(dev, me).start()
        for a in range(n_big):
            r, w = _GRAD_PIECE_SHAPES[a]
            other = out[a].at[pl.ds(pl.multiple_of((1 - c) * r, 8), r), :]
            pltpu.make_async_remote_copy(src_ref=sums[a], dst_ref=other, send_sem=send2.at[a], recv_sem=recv2.at[a],
                                         device_id=(x, y, 1 - c), device_id_type=MESH).wait_recv()
        for dev in range(N_DEV):
            @pl.when(me != dev)
            def _():
                small_share(dev, dev).wait_recv()
                small_share(dev, me).wait_send()
                for a in late_arrays(dev):
                    late(a, dev, me).wait_send()
        for cp in shares:
            cp.wait_send()
        for kp in keeps:
            kp.wait()

    any_spec = pl.BlockSpec(memory_space=pl.ANY)
    return pl.pallas_call(
        body, name="reduce_grads_tail",
        in_specs=[any_spec] * (n_arr + n_big + 1), out_specs=[any_spec] * n_arr,
        out_shape=[jax.ShapeDtypeStruct(sh, F32) for sh in shard_shapes]
        + [jax.ShapeDtypeStruct((N_DEV,) + small_piece, F32)],
        scratch_shapes=[pltpu.VMEM((N_DEV,) + sh, BF16) for sh in _GRAD_PIECE_SHAPES]
        + [pltpu.VMEM((N_DEV,) + small_piece, F32)]
        + [pltpu.VMEM(sh, F32) for sh in _GRAD_PIECE_SHAPES] + [pltpu.VMEM(small_piece, F32)]
        + [pltpu.SemaphoreType.DMA((N_DEV * n_arr,)), pltpu.SemaphoreType.DMA((N_DEV * n_arr,)),
           pltpu.SemaphoreType.DMA((n_big + N_DEV,)), pltpu.SemaphoreType.DMA((n_big + N_DEV,)),
           pltpu.SemaphoreType.DMA((N_DEV * n_arr + n_arr,))],
        compiler_params=pltpu.CompilerParams(vmem_limit_bytes=VMEM_LIMIT, has_side_effects=True),
    )(*grads, g_small, *early_slots, late_in_slots)


def _reduce_grads(g_in, g_up_a, g_up_b, g_out, g_small):
    big = (g_in, g_up_a, g_up_b, g_out)
    n_big = len(big)
    col_sharded = (True, True, True, False)
    piece_shapes = []
    for a, arr in enumerate(big):
        r, w = arr.shape
        piece_shapes.append((r // 2, w // N_CHIPS) if col_sharded[a] else (r // (2 * N_CHIPS), w))
    shard_shapes = [(2 * r, w) for r, w in piece_shapes]
    n_arr = n_big + 1

    def body(*refs):
        src = refs[:n_arr]
        out = refs[n_arr:2 * n_arr]
        slots = refs[2 * n_arr:3 * n_arr]
        sums = refs[3 * n_arr:4 * n_arr]
        send1, recv1, send2, recv2, local_sems = refs[4 * n_arr:]
        x, y, c = _place()
        me = 4 * x + 2 * y + c

        def piece_of(a, dev):
            chip_idx, half = dev // 2, dev % 2
            if a == n_big:
                return src[a].at[dev]
            r, w = piece_shapes[a]
            if col_sharded[a]:
                return src[a].at[pl.ds(pl.multiple_of(half * r, 16), r), pl.ds(pl.multiple_of(chip_idx * w, LANES), w)]
            return src[a].at[pl.ds(pl.multiple_of(dev * r, 16), r), :]

        def dev_id(dev):
            return (dev // 4, (dev // 2) % 2, dev % 2)

        own = [pltpu.make_async_copy(piece_of(a, me), slots[a].at[me], local_sems.at[a]) for a in range(n_arr)]
        for cp in own:
            cp.start()
        sends = []
        for d in range(1, N_DEV):
            peer = (me + d) % N_DEV
            for a in range(n_arr):
                cp = pltpu.make_async_remote_copy(
                    src_ref=piece_of(a, peer), dst_ref=slots[a].at[me],
                    send_sem=send1.at[n_arr * peer + a], recv_sem=recv1.at[n_arr * me + a],
                    device_id=dev_id(peer), device_id_type=MESH)
                cp.start()
                sends.append(cp)
        for cp in own:
            cp.wait()
        for d in range(1, N_DEV):
            peer = (me + d) % N_DEV
            for a in range(n_arr):
                pltpu.make_async_remote_copy(
                    src_ref=piece_of(a, peer), dst_ref=slots[a].at[peer],
                    send_sem=send1.at[n_arr * peer + a], recv_sem=recv1.at[n_arr * peer + a],
                    device_id=dev_id(peer), device_id_type=MESH).wait_recv()
        for a in range(n_arr):
            rows = slots[a].shape[1]
            step = 64 if rows % 64 == 0 else 8

            def add_rows(t, carry, a=a, step=step):
                r0 = pl.multiple_of(t * step, step)
                total = slots[a][0, pl.ds(r0, step), :].astype(F32)
                for dev in range(1, N_DEV):
                    total = total + slots[a][dev, pl.ds(r0, step), :].astype(F32)
                sums[a][pl.ds(r0, step), :] = total
                return carry

            lax.fori_loop(0, rows // step, add_rows, 0)
        shares = []
        keeps = []
        for a in range(n_big):
            r, w = piece_shapes[a]
            dst = out[a].at[pl.ds(pl.multiple_of(c * r, 8), r), :]
            cp = pltpu.make_async_remote_copy(src_ref=sums[a], dst_ref=dst, send_sem=send2.at[a], recv_sem=recv2.at[a],
                                              device_id=(x, y, 1 - c), device_id_type=MESH)
            cp.start()
            shares.append(cp)
            kp = pltpu.make_async_copy(sums[a], dst, local_sems.at[n_arr + a])
            kp.start()
            keeps.append(kp)
        kp = pltpu.make_async_copy(sums[n_big], out[n_big].at[me], local_sems.at[n_arr + n_big])
        kp.start()
        keeps.append(kp)
        for d in range(1, N_DEV):
            peer = (me + d) % N_DEV
            cp = pltpu.make_async_remote_copy(src_ref=sums[n_big], dst_ref=out[n_big].at[me],
                                              send_sem=send2.at[n_big + peer], recv_sem=recv2.at[n_big + me],
                                              device_id=dev_id(peer), device_id_type=MESH)
            cp.start()
            shares.append(cp)
        for a in range(n_big):
            r, w = piece_shapes[a]
            other = out[a].at[pl.ds(pl.multiple_of((1 - c) * r, 8), r), :]
            pltpu.make_async_remote_copy(src_ref=sums[a], dst_ref=other, send_sem=send2.at[a], recv_sem=recv2.at[a],
                                         device_id=(x, y, 1 - c), device_id_type=MESH).wait_recv()
        for d in range(1, N_DEV):
            peer = (me + d) % N_DEV
            pltpu.make_async_remote_copy(src_ref=sums[n_big], dst_ref=out[n_big].at[peer],
                                         send_sem=send2.at[n_big + peer], recv_sem=recv2.at[n_big + peer],
                                         device_id=dev_id(peer), device_id_type=MESH).wait_recv()
        for cp in sends + shares:
            cp.wait_send()
        for kp in keeps:
            kp.wait()

    any_spec = pl.BlockSpec(memory_space=pl.ANY)
    small_piece = (SMALL_PIECE, LANES)
    return pl.pallas_call(
        body, name="reduce_grads",
        in_specs=[any_spec] * n_arr, out_specs=[any_spec] * n_arr,
        out_shape=[jax.ShapeDtypeStruct(sh, F32) for sh in shard_shapes]
        + [jax.ShapeDtypeStruct((N_DEV,) + small_piece, F32)],
        scratch_shapes=[pltpu.VMEM((N_DEV,) + sh, BF16) for sh in piece_shapes]
        + [pltpu.VMEM((N_DEV,) + small_piece, F32)]
        + [pltpu.VMEM(sh, F32) for sh in piece_shapes] + [pltpu.VMEM(small_piece, F32)]
        + [pltpu.SemaphoreType.DMA((N_DEV * n_arr,)), pltpu.SemaphoreType.DMA((N_DEV * n_arr,)),
           pltpu.SemaphoreType.DMA((n_big + N_DEV,)), pltpu.SemaphoreType.DMA((n_big + N_DEV,)),
           pltpu.SemaphoreType.DMA((2 * n_arr,))],
        compiler_params=pltpu.CompilerParams(vmem_limit_bytes=VMEM_LIMIT, has_side_effects=True),
    )(*big, g_small)


_SMALL_PARTS = (("norm_g", 8), ("sgu_ln_g", 8), ("sgu_ln_b", 8), ("w_spatial", 1024), ("b_spatial", 8),
                ("final_norm_g", 8))
_LOSS_ROW = sum(n for _, n in _SMALL_PARTS)


def _pack_small(parts, loss_tile=None):
    rows = []
    for name, n_rows in _SMALL_PARTS:
        a = parts[name].reshape(-1, LANES).astype(F32)
        a = jnp.pad(a, ((0, n_rows - a.shape[0]), (0, 0)))
        rows.append(a)
    rows.append(jnp.zeros((8, LANES), F32) if loss_tile is None else loss_tile)
    rows.append(jnp.zeros((SMALL_ROWS - _LOSS_ROW - 8, LANES), F32))
    return jnp.concatenate(rows, axis=0)


def _unpack_small(packed, shapes):
    out = {}
    r0 = 0
    for name, n_rows in _SMALL_PARTS:
        n = math.prod(shapes[name])
        out[name] = packed[r0:r0 + n // LANES].reshape(shapes[name])
        r0 += n_rows
    return out


def _local_step(proj, ht, x, target, norm_g, w_in, sgu_ln_g, sgu_ln_b, w_spatial, b_spatial, w_up_a, w_up_b, w_out,
                final_norm_g, bq, bk):
    pos = jnp.arange(SGU_CHUNK)
    keep = (pos[None, :] // SGU_SUBCHUNK) <= (pos[:, None] // SGU_SUBCHUNK)
    w_mask = jnp.where(keep[None], w_spatial, 0.0).astype(BF16)
    w_mask_t = jnp.swapaxes(w_mask, 1, 2)
    bias_full = jnp.repeat(b_spatial.T, GROUP_DIM, axis=1)
    ln_g = sgu_ln_g.reshape(1, D_BRANCH)
    ln_b = sgu_ln_b.reshape(1, D_BRANCH)
    final_g = final_norm_g.reshape(1, D_MODEL)

    o, ya, rsave = _attn_fwd(proj, bq, bk, ATTN_PAIRS)
    yb = _sgu_fwd(proj, ln_g, ln_b, w_mask, bias_full)
    dzg, do, dyb, dx2, g_out, g_up_a, g_up_b, loss_acc, d_final = _mid(
        proj, ya, yb, o, x, target, final_g, w_up_a, w_up_b, w_out)
    dsgu, d_wsp, d_bsp, d_lng, d_lnb, *up_out_slots = _sgu_bwd(proj, dyb, ln_g, ln_b, w_mask, w_mask_t, bias_full,
                                                               (g_up_a, g_up_b, g_out))
    g_in = _dwin_piece(ht, dzg, lambda j: jnp.where(j == 0, COL_ZA, COL_GA - 1 + j), None)
    g_in = _dwin_piece(ht, dsgu, lambda j: COL_UB + j, g_in)
    dq, dk, dv, early_in_slots = _attn_bwd(proj, do, rsave, bq, bk, ATTN_PAIRS, g_in)
    early_slots = [early_in_slots] + up_out_slots
    g_in = _dwin_piece(ht, dq, lambda j: COL_Q + j, g_in)
    g_in = _dwin_piece(ht, dk, lambda j: COL_K + j, g_in)
    g_in = _dwin_piece(ht, dv, lambda j: COL_V + j, g_in)
    pieces = [(dq, COL_Q * D_BRANCH, 0, D_BRANCH), (dk, COL_K * D_BRANCH, 0, D_BRANCH),
              (dv, COL_V * D_BRANCH, 0, D_BRANCH), (dzg, COL_ZA * D_BRANCH, 0, D_BRANCH),
              (dsgu, COL_UB * D_BRANCH, 0, 3 * D_BRANCH), (dzg, COL_GA * D_BRANCH, D_BRANCH, 2 * D_MODEL)]
    dx, d_norm, late_in_slots = _dh_dx(pieces, w_in, x, norm_g, dx2, g_in)
    small = {"norm_g": d_norm, "sgu_ln_g": d_lng, "sgu_ln_b": d_lnb, "w_spatial": d_wsp,
             "b_spatial": d_bsp[:, :N_GROUPS].T, "final_norm_g": d_final}
    return loss_acc, dx, (g_in, g_up_a, g_up_b, g_out), small, early_slots, late_in_slots


def kernel(x, norm_g, w_in, sgu_ln_g, sgu_ln_b, w_spatial, b_spatial, w_up_a, w_up_b, w_out, final_norm_g, loss_target, m_norm_g, m_w_in, m_sgu_ln_g, m_sgu_ln_b, m_w_spatial, m_b_spatial, m_w_up_a, m_w_up_b, m_w_out, m_final_norm_g, v_norm_g, v_w_in, v_sgu_ln_g, v_sgu_ln_b, v_w_spatial, v_b_spatial, v_w_up_a, v_w_up_b, v_w_out, v_final_norm_g):
    big_names = ("w_in", "w_up_a", "w_up_b", "w_out")
    small_names = tuple(n for n, _ in _SMALL_PARTS)
    names = ("norm_g", "w_in", "sgu_ln_g", "sgu_ln_b", "w_spatial", "b_spatial", "w_up_a", "w_up_b", "w_out",
             "final_norm_g")
    w = dict(norm_g=norm_g, w_in=w_in, sgu_ln_g=sgu_ln_g, sgu_ln_b=sgu_ln_b, w_spatial=w_spatial,
             b_spatial=b_spatial, w_up_a=w_up_a, w_up_b=w_up_b, w_out=w_out, final_norm_g=final_norm_g)
    m = dict(norm_g=m_norm_g, w_in=m_w_in, sgu_ln_g=m_sgu_ln_g, sgu_ln_b=m_sgu_ln_b, w_spatial=m_w_spatial,
             b_spatial=m_b_spatial, w_up_a=m_w_up_a, w_up_b=m_w_up_b, w_out=m_w_out, final_norm_g=m_final_norm_g)
    v = dict(norm_g=v_norm_g, w_in=v_w_in, sgu_ln_g=v_sgu_ln_g, sgu_ln_b=v_sgu_ln_b, w_spatial=v_w_spatial,
             b_spatial=v_b_spatial, w_up_a=v_w_up_a, w_up_b=v_w_up_b, w_out=v_w_out, final_norm_g=v_final_norm_g)
    shapes = {n: w[n].shape for n in names}
    flat2d = lambda a: a.reshape(a.shape[-2:])

    proj, ht, *full = _in_proj_gather(x[0], norm_g, *[flat2d(w[n]) for n in big_names])
    loss, dx, big_grads, small, early_slots, late_in_slots = _local_step(
        proj, ht, x[0], loss_target[0], norm_g, full[0], sgu_ln_g[0], sgu_ln_b[0], w_spatial[0], b_spatial[0],
        full[1], full[2], full[3], final_norm_g, ATTN_Q_BLOCK, ATTN_K_BLOCK)
    packed = _pack_small(small, loss).reshape(N_DEV, SMALL_PIECE, LANES)
    red = _reduce_grads_tail(big_grads, packed, early_slots, late_in_slots)

    grads, deltas, new_m, new_v = {}, {}, {}, {}
    for n, g in zip(big_names, red[:4]):
        g, d, nm, nv = _adamw(flat2d(w[n]), g, flat2d(m[n]), flat2d(v[n]))
        grads[n], deltas[n], new_m[n], new_v[n] = (a.reshape(shapes[n]) for a in (g, d, nm, nv))
    g_small = red[4].reshape(SMALL_ROWS, LANES)
    g_small, d, nm, nv = _adamw(_pack_small({n: w[n] for n in small_names}), g_small,
                                _pack_small({n: m[n] for n in small_names}),
                                _pack_small({n: v[n] for n in small_names}))
    for src, dst in ((g_small, grads), (d, deltas), (nm, new_m), (nv, new_v)):
        dst.update(_unpack_small(src, shapes))

    return (g_small[_LOSS_ROW, 0], dx[None], *[grads[n] for n in names], *[deltas[n] for n in names],
            *[new_m[n] for n in names], *[new_v[n] for n in names])
```

```python
import functools
import math

import jax
import jax.numpy as jnp
from jax import lax
from jax.experimental import pallas as pl
from jax.experimental.pallas import tpu as pltpu

F32 = jnp.float32
BF16 = jnp.bfloat16

D_MODEL = 1024
N_HEADS = 8
HEAD_DIM = 64
D_BRANCH = 512
D_IN = 4 * D_BRANCH + 3 * D_BRANCH + 2 * D_MODEL
N_GROUPS = 8
GROUP_DIM = 64
SGU_CHUNK = 128
SGU_SUBCHUNK = 64
GROUP_SHIFT = 6
EPS = 1e-6
LANES = 128
ATTN_Q_BLOCK = 256
ATTN_K_BLOCK = 256
DEAD = -110.0
SKIPPED = -1e30
SCAN_PASSES = 1
ATTN_PAIRS = 2
N_CHIPS = 4
N_DEV = 8
MESH = pl.DeviceIdType.MESH

ADAM_LR = 0.001
ADAM_B1 = 0.9
ADAM_B2 = 0.999
ADAM_EPS = 1e-08
ADAM_WD = 0.01
ADAM_STEP = 10

COL_Q, COL_K, COL_V, COL_ZA, COL_UB, COL_VB, COL_ZB, COL_GA, COL_GB = 0, 1, 2, 3, 4, 5, 6, 7, 9

VMEM_LIMIT = 56 * 1024 * 1024

SMALL_ROWS = 1088
SMALL_PIECE = SMALL_ROWS // N_DEV


def _cparams(sem=None):
    return pltpu.CompilerParams(dimension_semantics=sem, vmem_limit_bytes=VMEM_LIMIT)


def _aligned(v, m):
    return v if isinstance(v, int) else pl.multiple_of(v, m)


def _sigmoid(x):
    return 1.0 / (1.0 + jnp.exp(-x))


def _gelu_and_grad(x):
    k = math.sqrt(2.0 / math.pi)
    x2 = x * x
    inner = k * (x + 0.044715 * x * x2)
    th = jnp.tanh(inner)
    g = 0.5 * x * (1.0 + th)
    dg = 0.5 * (1.0 + th) + 0.5 * x * (1.0 - th * th) * (k * (1.0 + 3.0 * 0.044715 * x2))
    return g, dg


def _split_dot(a, b_bf16, passes):
    out = None
    rem = a
    for _ in range(passes):
        part = rem.astype(BF16)
        d = jnp.dot(part, b_bf16, preferred_element_type=F32)
        out = d if out is None else out + d
        rem = rem - part.astype(F32)
    return out


def _dot_nt(a, b):
    return lax.dot_general(a, b, (((1,), (1,)), ((), ())), preferred_element_type=F32)


def _dot_tn(a, b):
    return lax.dot_general(a, b, (((0,), (0,)), ((), ())), preferred_element_type=F32)


def _place():
    x, y, c = lax.axis_index("x"), lax.axis_index("y"), lax.axis_index("c")
    return x, y, c


def _in_proj_gather(x, norm_g, w_in, w_up_a, w_up_b, w_out):
    s = x.shape[0]
    tm = min(512, s)
    nt = s // tm
    shards = (w_in, w_up_a, w_up_b, w_out)
    n_arr = len(shards)
    col_sharded = (True, True, True, False)
    full_shapes = ((D_MODEL, D_IN), (D_BRANCH, D_MODEL), (D_BRANCH, D_MODEL), (D_MODEL, D_MODEL))
    w_shard = w_in.shape[1]
    half_rows = D_MODEL // 2
    stage_rows = 256

    def body(order_ref, x_ref, g_ref, *refs):
        src = refs[:n_arr]
        proj_ref, ht_ref = refs[n_arr:n_arr + 2]
        out = refs[n_arr + 2:2 * n_arr + 2]
        wsc, h_scr, stage = refs[2 * n_arr + 2:2 * n_arr + 5]
        small_stage = refs[2 * n_arr + 5:2 * n_arr + 8]
        small_cast = refs[2 * n_arr + 8:2 * n_arr + 11]
        send_sems, recv_sems, local_sems = refs[2 * n_arr + 11:]
        k = pl.program_id(0)
        i = pl.program_id(1)
        x_, y_, c = _place()
        chip = 2 * x_ + y_
        sibling = (x_, y_, 1 - c)
        others = [(x_, 1 - y_), (1 - x_, y_), (1 - x_, 1 - y_)]

        def region(a, chip_idx, half):
            if a == 0:
                return wsc.at[chip_idx, pl.ds(_aligned(half * half_rows, 16), half_rows), :]
            r, w = shards[a].shape
            hr = r // 2
            if col_sharded[a]:
                return out[a].at[pl.ds(_aligned(half * hr, 16), hr), pl.ds(_aligned(chip_idx * w, LANES), w)]
            return out[a].at[pl.ds(_aligned(chip_idx * r + half * hr, 16), hr), :]

        def remote(kk, a, chip_idx, half, to, own):
            s_ref = region(a, chip_idx, half)
            if own and a > 0:
                hr = shards[a].shape[0] // 2
                s_ref = small_cast[a - 1].at[pl.ds(_aligned(half * hr, 16), hr), :]
            return pltpu.make_async_remote_copy(src_ref=s_ref, dst_ref=region(a, chip_idx, half),
                                                send_sem=send_sems.at[kk], recv_sem=recv_sems.at[kk],
                                                device_id=to, device_id_type=MESH)

        def keep_whole(kk, chip_idx):
            return pltpu.make_async_copy(wsc.at[chip_idx],
                                         out[0].at[:, pl.ds(_aligned(chip_idx * w_shard, LANES), w_shard)],
                                         local_sems.at[kk])

        def small_stores():
            cps = []
            for a in range(1, n_arr):
                hr = shards[a].shape[0] // 2
                for half in range(2):
                    cps.append(pltpu.make_async_copy(small_cast[a - 1].at[pl.ds(half * hr, hr), :],
                                                     region(a, chip, half), local_sems.at[4 + 2 * (a - 1) + half]))
            return cps

        def arrive_and_pass(j):
            ochip = chip ^ j
            for a in range(n_arr):
                kk = n_arr * (j - 1) + a
                remote(kk, a, ochip, c, sibling, False).wait_recv()
                remote(3 * n_arr + kk, a, ochip, c, sibling, False).start()

        def from_sibling(j, a):
            remote(3 * n_arr + n_arr * (j - 1) + a, a, chip ^ j, 1 - c, sibling, False).wait_recv()

        @pl.when((k == 0) & (i == 0))
        def _():
            def cast_rows(half):
                for t in range(half_rows // stage_rows):
                    r0 = pl.multiple_of(half * half_rows + t * stage_rows, stage_rows)
                    pltpu.sync_copy(src[0].at[pl.ds(r0, stage_rows), :], stage)
                    wsc[chip, pl.ds(r0, stage_rows), :] = stage[...].astype(BF16)

            cast_rows(c)
            for j in (1, 2):
                remote(n_arr * (j - 1), 0, chip, c, (*others[j - 1], c), True).start()
            cast_rows(1 - c)
            for a in range(1, n_arr):
                pltpu.sync_copy(src[a], small_stage[a - 1])
                small_cast[a - 1][...] = small_stage[a - 1][...].astype(BF16)
            for j in (1, 2):
                for a in range(1, n_arr):
                    remote(n_arr * (j - 1) + a, a, chip, c, (*others[j - 1], c), True).start()
            keep_whole(0, chip).start()
            for cp in small_stores():
                cp.start()

        @pl.when((k == 1) & (i == 0))
        def _():
            for j in (1, 2):
                remote(n_arr * (j - 1), 0, chip, c, (*others[j - 1], c), True).wait_send()
            for a in range(n_arr):
                remote(n_arr * 2 + a, a, chip, c, (*others[2], c), True).start()
            arrive_and_pass(1)
            arrive_and_pass(2)
            from_sibling(1, 0)
            keep_whole(1, chip ^ 1).start()

        @pl.when((k == 2) & (i == 0))
        def _():
            from_sibling(2, 0)
            keep_whole(2, chip ^ 2).start()
            arrive_and_pass(3)

        @pl.when((k == 3) & (i == 0))
        def _():
            from_sibling(3, 0)
            keep_whole(3, chip ^ 3).start()

        @pl.when(k == 0)
        def _():
            xf = x_ref[...]
            r = lax.rsqrt(jnp.mean(xf * xf, axis=-1, keepdims=True) + EPS)
            h = xf * r * g_ref[...]
            h_scr[i] = h.astype(BF16)
            ht_ref[...] = h.T.astype(BF16)

        proj_ref[...] = jnp.dot(h_scr[i], wsc[order_ref[k]], preferred_element_type=F32).astype(BF16)

        @pl.when((k == 3) & (i == nt - 1))
        def _():
            for j in (1, 2, 3):
                for a in range(1, n_arr):
                    from_sibling(j, a)
            for j in (1, 2, 3):
                for a in range(n_arr):
                    kk = n_arr * (j - 1) + a
                    if a > 0 or j == 3:
                        remote(kk, a, chip, c, (*others[j - 1], c), True).wait_send()
                    remote(3 * n_arr + kk, a, chip ^ j, c, sibling, False).wait_send()
            for kk in range(4):
                keep_whole(kk, chip ^ kk).wait()
            for cp in small_stores():
                cp.wait()

    any_spec = pl.BlockSpec(memory_space=pl.ANY)
    tile = lambda kk, ii: jnp.where(kk == 0, ii, nt - 1)
    grid_spec = pltpu.PrefetchScalarGridSpec(
        num_scalar_prefetch=1, grid=(N_CHIPS, nt),
        in_specs=[pl.BlockSpec((tm, D_MODEL), lambda kk, ii, order: (tile(kk, ii), 0)),
                  pl.BlockSpec((1, D_MODEL), lambda kk, ii, order: (0, 0))] + [any_spec] * n_arr,
        out_specs=[pl.BlockSpec((tm, w_shard), lambda kk, ii, order: (ii, order[kk])),
                   pl.BlockSpec((D_MODEL, tm), lambda kk, ii, order: (0, tile(kk, ii)))] + [any_spec] * n_arr,
        scratch_shapes=[pltpu.VMEM((N_CHIPS, D_MODEL, w_shard), BF16), pltpu.VMEM((nt, tm, D_MODEL), BF16),
                        pltpu.VMEM((stage_rows, w_shard), F32)]
        + [pltpu.VMEM(a.shape, F32) for a in shards[1:]] + [pltpu.VMEM(a.shape, BF16) for a in shards[1:]]
        + [pltpu.SemaphoreType.DMA((6 * n_arr,)), pltpu.SemaphoreType.DMA((6 * n_arr,)),
           pltpu.SemaphoreType.DMA((4 + 2 * (n_arr - 1),))])
    x_, y_, _ = _place()
    order = (2 * x_ + y_) ^ jnp.arange(N_CHIPS, dtype=jnp.int32)
    return pl.pallas_call(
        body, name="in_proj_gather", grid_spec=grid_spec,
        out_shape=[jax.ShapeDtypeStruct((s, D_IN), BF16), jax.ShapeDtypeStruct((D_MODEL, s), BF16)]
        + [jax.ShapeDtypeStruct(sh, BF16) for sh in full_shapes],
        compiler_params=pltpu.CompilerParams(dimension_semantics=("arbitrary", "arbitrary"),
                                             vmem_limit_bytes=VMEM_LIMIT, has_side_effects=True),
    )(order, x, norm_g, *shards)


def _neg_softplus_parts(z):
    zb = z.astype(BF16)
    p = jnp.exp(-jnp.abs(zb))
    return p, jnp.maximum(zb, jnp.zeros_like(zb)) + jnp.log(1.0 + p)


def _split_cat(a, passes):
    parts = []
    rem = a
    for k in range(passes):
        part = rem.astype(BF16)
        parts.append(part)
        if k + 1 < passes:
            rem = rem - part.astype(F32)
    return parts[0] if passes == 1 else jnp.concatenate(parts, axis=1)


def _tri(blk, upper, sign):
    row = lax.broadcasted_iota(jnp.int32, (blk, blk), 0)
    col = lax.broadcasted_iota(jnp.int32, (blk, blk), 1)
    keep = (row <= col) if upper else (row >= col)
    t = jnp.where(keep, sign, 0.0).astype(BF16)
    return t if SCAN_PASSES == 1 else jnp.concatenate([t] * SCAN_PASSES, axis=0)


def _attn_fwd(proj, bq, bk, npairs):
    s = proj.shape[0]
    nq = s // bq
    ratio = bq // bk
    scale = HEAD_DIM ** -0.5
    heads = tuple(range(2 * npairs))
    width = LANES * npairs

    def body(q_ref, k_ref, v_ref, za_ref, o_ref, ya_ref, rs_ref, acc_ref, r_ref, z_ref):
        i = pl.program_id(1)
        lane = lax.broadcasted_iota(jnp.int32, (bq, LANES), 1)
        lo_half = lane < HEAD_DIM
        qm = []
        for pr in range(npairs):
            q = q_ref[:, LANES * pr:LANES * (pr + 1)] * jnp.asarray(scale, BF16)
            zero = jnp.zeros_like(q)
            qm += [jnp.where(lo_half, q, zero), jnp.where(lo_half, zero, q)]
        row = lax.broadcasted_iota(jnp.int32, (bq, bk), 0)
        col = lax.broadcasted_iota(jnp.int32, (bq, bk), 1)
        tneg = _tri(bk, False, -1.0)
        acc_ref[...] = jnp.zeros_like(acc_ref)
        r_ref[...] = jnp.zeros_like(r_ref)
        rs_ref[...] = jnp.full_like(rs_ref, SKIPPED)

        def scores(j):
            ks = pl.multiple_of(j * bk, bk)
            return [_dot_nt(qm[h], k_ref[pl.ds(ks, bk), LANES * (h // 2):LANES * (h // 2 + 1)]) for h in heads]

        for h, zh in enumerate(scores(i * ratio + ratio - 1)):
            z_ref[h] = zh

        def block(j, diag):
            ks = pl.multiple_of(j * bk, bk)
            vj = [v_ref[pl.ds(ks, bk), LANES * pr:LANES * (pr + 1)] for pr in range(npairs)]
            if diag:
                before = (j * bk + col) < (i * bq + row)
            z = [z_ref[h] for h in heads]
            sp = [_neg_softplus_parts(z[h])[1] for h in heads]
            if diag:
                sp = [jnp.where(before, sp[h], 0.0) for h in heads]
            cin = [jnp.dot(_split_cat(sp[h], SCAN_PASSES), tneg, preferred_element_type=F32) for h in heads]
            for h, zh in enumerate(scores(jnp.maximum(j - 1, 0))):
                z_ref[h] = zh
            w = [jnp.exp(z[h] + cin[h]) for h in heads]
            if diag:
                w = [jnp.where(before, w[h], 0.0) for h in heads]
            pv = [jnp.dot(w[h].astype(BF16), vj[h // 2], preferred_element_type=F32) for h in heads]
            r = [r_ref[h] for h in heads]
            for h in heads:
                acc_ref[h] += pv[h] * jnp.exp(r[h])
                r_ref[h] = r[h] + cin[h][:, 0:1]
            for pr in range(npairs):
                rs_ref[pr] = jnp.where(lane == j, r[2 * pr], jnp.where(lane == j + HEAD_DIM, r[2 * pr + 1], rs_ref[pr]))

        for t in range(ratio):
            block(i * ratio + ratio - 1 - t, True)

        def alive(carry):
            jj, r_max = carry
            return (jj < i * ratio) & (r_max > DEAD)

        def loop_body(carry):
            jj, _ = carry
            block(i * ratio - 1 - jj, False)
            return jj + 1, jnp.max(r_ref[...])

        lax.while_loop(alive, loop_body, (0, jnp.max(r_ref[...])))
        for pr in range(npairs):
            cols = slice(LANES * pr, LANES * (pr + 1))
            o = jnp.where(lo_half, acc_ref[2 * pr], acc_ref[2 * pr + 1])
            o_ref[:, cols] = o.astype(BF16)
            za = za_ref[:, cols].astype(F32)
            ya_ref[:, cols] = (o * (za * _sigmoid(za))).astype(BF16)

    n_steps = N_HEADS // (2 * npairs)
    return pl.pallas_call(
        body, name="attn_fwd", grid=(n_steps, nq),
        in_specs=[pl.BlockSpec((bq, width), lambda p, i: (i, n_steps * COL_Q + p)),
                  pl.BlockSpec((s, width), lambda p, i: (0, n_steps * COL_K + p)),
                  pl.BlockSpec((s, width), lambda p, i: (0, n_steps * COL_V + p)),
                  pl.BlockSpec((bq, width), lambda p, i: (i, n_steps * COL_ZA + p))],
        out_specs=[pl.BlockSpec((bq, width), lambda p, i: (i, p)),
                   pl.BlockSpec((bq, width), lambda p, i: (i, p)),
                   pl.BlockSpec((npairs, bq, LANES), lambda p, i: (p, i, 0))],
        out_shape=[jax.ShapeDtypeStruct((s, D_BRANCH), BF16), jax.ShapeDtypeStruct((s, D_BRANCH), BF16),
                   jax.ShapeDtypeStruct((N_HEADS // 2, s, LANES), F32)],
        scratch_shapes=[pltpu.VMEM((2 * npairs, bq, LANES), F32), pltpu.VMEM((2 * npairs, bq, 1), F32),
                        pltpu.VMEM((2 * npairs, bq, bk), F32)],
        compiler_params=_cparams(("parallel", "parallel")),
    )(proj, proj, proj, proj)


_GRAD_COL_SHARDED = (True, True, True, False)
_GRAD_FULL_SHAPES = ((D_MODEL, D_IN), (D_BRANCH, D_MODEL), (D_BRANCH, D_MODEL), (D_MODEL, D_MODEL))
_GRAD_PIECE_SHAPES = tuple((r // 2, w // N_CHIPS) if cs else (r // (2 * N_CHIPS), w)
                           for (r, w), cs in zip(_GRAD_FULL_SHAPES, _GRAD_COL_SHARDED))
_EARLY_IN_DEVS = (4, 5, 6, 7)
_LATE_IN_DEVS = (0, 1, 2, 3)
_LATE_CHIPS = (0, 1)


def _grad_piece(ref, a, dev):
    r, w = _GRAD_PIECE_SHAPES[a]
    if _GRAD_COL_SHARDED[a]:
        return ref.at[pl.ds((dev % 2) * r, r), pl.ds((dev // 2) * w, w)]
    return ref.at[pl.ds(dev * r, r), :]


def _dev_id(dev):
    return (dev // 4, (dev // 2) % 2, dev % 2)


def _scatter_copy(plan, src, slots, send_sems, recv_sems, pos, dst_dev, src_dev):
    a = plan[pos][0]
    n = len(plan)
    return pltpu.make_async_remote_copy(
        src_ref=_grad_piece(src[pos], a, dst_dev), dst_ref=slots[pos].at[src_dev],
        send_sem=send_sems.at[n * dst_dev + pos], recv_sem=recv_sems.at[n * src_dev + pos],
        device_id=_dev_id(dst_dev), device_id_type=MESH)


def _scatter_start(plan, src, slots, send_sems, recv_sems, me):
    for pos, (_, dests) in enumerate(plan):
        for dev in dests:
            @pl.when(me != dev)
            def _():
                _scatter_copy(plan, src, slots, send_sems, recv_sems, pos, dev, me).start()


def _scatter_wait(plan, src, slots, send_sems, recv_sems, me):
    for pos, (_, dests) in enumerate(plan):
        for dev in range(N_DEV):
            @pl.when((me != dev) & (me >= dests[0]) & (me <= dests[-1]))
            def _():
                _scatter_copy(plan, src, slots, send_sems, recv_sems, pos, dev, dev).wait_recv()
        for dev in dests:
            @pl.when(me != dev)
            def _():
                _scatter_copy(plan, src, slots, send_sems, recv_sems, pos, dev, me).wait_send()


def _me():
    return 4 * lax.axis_index("x") + 2 * lax.axis_index("y") + lax.axis_index("c")


def _presum_copy(src, dst, send_sem, recv_sem, to_dev):
    return pltpu.make_async_remote_copy(src_ref=src, dst_ref=dst, send_sem=send_sem, recv_sem=recv_sem,
                                        device_id=_dev_id(to_dev), device_id_type=MESH)


def _presum_hand_off(dev, a, dest_chips, g_ref, slots, pair, send_sems, recv_sems):
    chip, core = dev // 2, dev % 2
    cps = []
    for k, q in enumerate(dest_chips):
        piece = _grad_piece(g_ref, a, 2 * q + 1 - core)
        if q == chip:
            cps.append(_presum_copy(piece, slots.at[dev], send_sems.at[N_DEV + k], recv_sems.at[dev], dev ^ 1))
        else:
            cps.append(_presum_copy(piece, pair.at[k], send_sems.at[N_DEV + k], recv_sems.at[N_DEV + k], dev ^ 1))
    return cps


def _presum_sends(dev, a, dest_chips, slots, sums, send_sems, recv_sems):
    chip, core = dev // 2, dev % 2
    return [_presum_copy(sums.at[k], slots.at[dev], send_sems.at[2 * q + core], recv_sems.at[dev], 2 * q + core)
            for k, q in enumerate(dest_chips) if q != chip]


def _presum_send(dev, a, dest_chips, g_ref, slots, pair, stage, sums, send_sems, recv_sems):
    chip, core = dev // 2, dev % 2
    hand = _presum_hand_off(dev, a, dest_chips, g_ref, slots, pair, send_sems, recv_sems)
    for k, q in enumerate(dest_chips):
        if q != chip:
            hand[k].wait_recv()
            pltpu.sync_copy(_grad_piece(g_ref, a, 2 * q + core), stage)
            sums[k] = (stage[...].astype(F32) + pair[k].astype(F32)).astype(BF16)
    for cp in _presum_sends(dev, a, dest_chips, slots, sums, send_sems, recv_sems):
        cp.start()


def _presum_wait(dev, a, dest_chips, g_ref, slots, pair, sums, send_sems, recv_sems):
    chip, core = dev // 2, dev % 2
    for cp in _presum_hand_off(dev, a, dest_chips, g_ref, slots, pair, send_sems, recv_sems):
        cp.wait_send()
    for cp in _presum_sends(dev, a, dest_chips, slots, sums, send_sems, recv_sems):
        cp.wait_send()
    if chip in dest_chips:
        for src_dev in _presum_sources(dev):
            _presum_copy(sums.at[0], slots.at[src_dev], send_sems.at[src_dev], recv_sems.at[src_dev], src_dev).wait_recv()


def _presum_sources(dev):
    return [dev ^ 1] + [2 * r + dev % 2 for r in range(N_CHIPS) if r != dev // 2]


_ALL_DEVS = tuple(range(N_DEV))


def _attn_bwd(proj, do, rsave, bq, bk, npairs, g_in):
    plan = ((0, _EARLY_IN_DEVS),)
    grads = (g_in,)
    s = proj.shape[0]
    nq = s // bq
    ratio = bq // bk
    scale = HEAD_DIM ** -0.5
    heads = tuple(range(2 * npairs))
    width = LANES * npairs

    n_steps = N_HEADS // (2 * npairs)
    n_g = len(grads)

    def body(q_ref, k_ref, v_ref, do_ref, rs_ref, *refs):
        g_src = refs[:n_g]
        dq_ref, dk_ref, dv_ref = refs[n_g:n_g + 3]
        g_slots = refs[n_g + 3:2 * n_g + 3]
        dk_acc, dv_acc, dq_acc, e_ref, qmt_ref, domt_ref, rst_ref, send_sems, recv_sems = refs[2 * n_g + 3:]
        i = pl.program_id(1)
        me = _me()

        @pl.when((pl.program_id(0) == 0) & (i == 0))
        def _():
            _scatter_start(plan, g_src, g_slots, send_sems, recv_sems, me)

        lane = lax.broadcasted_iota(jnp.int32, (bq, LANES), 1)
        lo_half = lane < HEAD_DIM
        qm, dom = [], []
        for pr in range(npairs):
            cols = slice(LANES * pr, LANES * (pr + 1))
            q = q_ref[:, cols] * jnp.asarray(scale, BF16)
            zero = jnp.zeros_like(q)
            qm += [jnp.where(lo_half, q, zero), jnp.where(lo_half, zero, q)]
            dout = do_ref[:, cols].astype(F32)
            dom += [jnp.where(lo_half, dout, 0.0), jnp.where(lo_half, 0.0, dout)]
        row = lax.broadcasted_iota(jnp.int32, (bq, bk), 0)
        col = lax.broadcasted_iota(jnp.int32, (bq, bk), 1)
        tneg = _tri(bk, False, -1.0)
        tfwd = _tri(bk, True, 1.0)

        @pl.when(i == 0)
        def _():
            dk_acc[...] = jnp.zeros_like(dk_acc)
            dv_acc[...] = jnp.zeros_like(dv_acc)

        dq_acc[...] = jnp.zeros_like(dq_acc)
        e_ref[...] = jnp.zeros_like(e_ref)
        for h in heads:
            qmt_ref[h] = qm[h].astype(F32).T.astype(BF16)
            domt_ref[h] = dom[h].T
        for pr in range(npairs):
            rst_ref[pr] = rs_ref[pr].T

        def block(j, diag):
            ks = pl.multiple_of(j * bk, bk)
            kj = [k_ref[pl.ds(ks, bk), LANES * pr:LANES * (pr + 1)] for pr in range(npairs)]
            vj = [v_ref[pl.ds(ks, bk), LANES * pr:LANES * (pr + 1)] for pr in range(npairs)]
            if diag:
                before = (j * bk + col) < (i * bq + row)
            z = [_dot_nt(qm[h], kj[h // 2]) for h in heads]
            dost = [(domt_ref[h] * jnp.exp(rst_ref[h // 2, pl.ds(j + HEAD_DIM * (h % 2), 1), :])).astype(BF16)
                    for h in heads]
            er = [jnp.exp(jnp.sum(jnp.where(lane == j + HEAD_DIM * (h % 2), rs_ref[h // 2], 0.0), axis=-1,
                                  keepdims=True)) for h in heads]
            dos = [(dom[h] * er[h]).astype(BF16) for h in heads]
            dw = [_dot_nt(dos[h], vj[h // 2]) for h in heads]
            psp = [_neg_softplus_parts(z[h]) for h in heads]
            sp = [psp[h][1] for h in heads]
            if diag:
                sp = [jnp.where(before, sp[h], 0.0) for h in heads]
            cin = [jnp.dot(_split_cat(sp[h], SCAN_PASSES), tneg, preferred_element_type=F32) for h in heads]
            w = [jnp.exp(z[h] + cin[h]) for h in heads]
            if diag:
                w = [jnp.where(before, w[h], 0.0) for h in heads]
            e =[dw[h] * w[h] for h in heads]
            eincl = [jnp.dot(_split_cat(e[h], SCAN_PASSES), tfwd, preferred_element_type=F32) + e_ref[h]
                     for h in heads]
            dz = []
            for h in heads:
                p = psp[h][0]
                beta = jnp.where(z[h] >= 0.0, 1.0, p) / (1.0 + p)
                d = e[h] - beta * eincl[h]
                dz.append((jnp.where(before, d, 0.0) if diag else d).astype(BF16))
            wb = [w[h].astype(BF16) for h in heads]
            for h in heads:
                e_ref[h] = eincl[h][:, bk - 1:bk]
                dq_acc[h] += jnp.dot(dz[h], kj[h // 2], preferred_element_type=F32)
            for pr in range(npairs):
                rows = slice(LANES * pr, LANES * (pr + 1))
                h0, h1 = 2 * pr, 2 * pr + 1
                dk_acc[rows, pl.ds(ks, bk)] += (jnp.dot(qmt_ref[h0], dz[h0], preferred_element_type=F32)
                                                 + jnp.dot(qmt_ref[h1], dz[h1], preferred_element_type=F32))
                dv_acc[rows, pl.ds(ks, bk)] += (jnp.dot(dost[h0], wb[h0], preferred_element_type=F32)
                                                 + jnp.dot(dost[h1], wb[h1], preferred_element_type=F32))

        def loop_body(j, carry):
            block(j, False)
            return carry

        block_of_lane = lane & (HEAD_DIM - 1)
        live = jnp.max(rs_ref[...], axis=0) > DEAD
        first_live = jnp.min(jnp.where(live, block_of_lane, nq * ratio))
        lax.fori_loop(jnp.minimum(first_live, i * ratio), i * ratio, loop_body, 0)
        for t in range(ratio):
            block(i * ratio + t, True)
        for pr in range(npairs):
            dq = jnp.where(lo_half, dq_acc[2 * pr], dq_acc[2 * pr + 1]) * scale
            dq_ref[:, LANES * pr:LANES * (pr + 1)] = dq.astype(BF16)

        @pl.when(i == nq - 1)
        def _():
            dk_ref[...] = dk_acc[...].T.astype(BF16)
            dv_ref[...] = dv_acc[...].T.astype(BF16)

        @pl.when((pl.program_id(0) == n_steps - 1) & (i == nq - 1))
        def _():
            _scatter_wait(plan, g_src, g_slots, send_sems, recv_sems, me)

    any_spec = pl.BlockSpec(memory_space=pl.ANY)
    return pl.pallas_call(
        body, name="attn_bwd", grid=(n_steps, nq),
        in_specs=[pl.BlockSpec((bq, width), lambda p, i: (i, n_steps * COL_Q + p)),
                  pl.BlockSpec((s, width), lambda p, i: (0, n_steps * COL_K + p)),
                  pl.BlockSpec((s, width), lambda p, i: (0, n_steps * COL_V + p)),
                  pl.BlockSpec((bq, width), lambda p, i: (i, p)),
                  pl.BlockSpec((npairs, bq, LANES), lambda p, i: (p, i, 0))] + [any_spec] * n_g,
        out_specs=[pl.BlockSpec((bq, width), lambda p, i: (i, p)),
                   pl.BlockSpec((s, width), lambda p, i: (0, p)),
                   pl.BlockSpec((s, width), lambda p, i: (0, p))] + [any_spec] * n_g,
        out_shape=[jax.ShapeDtypeStruct((s, D_BRANCH), BF16)] * 3
        + [jax.ShapeDtypeStruct((N_DEV,) + _GRAD_PIECE_SHAPES[a], BF16) for a, _ in plan],
        scratch_shapes=[pltpu.VMEM((width, s), F32), pltpu.VMEM((width, s), F32),
                        pltpu.VMEM((2 * npairs, bq, LANES), F32), pltpu.VMEM((2 * npairs, bq, 1), F32),
                        pltpu.VMEM((2 * npairs, LANES, bq), BF16), pltpu.VMEM((2 * npairs, LANES, bq), F32),
                        pltpu.VMEM((npairs, LANES, bq), F32),
                        pltpu.SemaphoreType.DMA((N_DEV * n_g,)), pltpu.SemaphoreType.DMA((N_DEV * n_g,))],
        compiler_params=pltpu.CompilerParams(dimension_semantics=("arbitrary", "arbitrary"),
                                             vmem_limit_bytes=VMEM_LIMIT, has_side_effects=True),
    )(proj, proj, proj, do, rsave, *grads)


def _group_avg_matrix():
    a = lax.broadcasted_iota(jnp.int32, (LANES, LANES), 0) >> GROUP_SHIFT
    b = lax.broadcasted_iota(jnp.int32, (LANES, LANES), 1) >> GROUP_SHIFT
    return jnp.where(a == b, 1.0 / GROUP_DIM, 0.0).astype(BF16)


def _group_mean(a, avg):
    parts = [_split_dot(a[:, LANES * k:LANES * (k + 1)], avg, 3) for k in range(D_BRANCH // LANES)]
    return jnp.concatenate(parts, axis=1)


def _sgu_forward_parts(ub, vb, ln_g, ln_b, avg):
    ug, dug = _gelu_and_grad(ub)
    vg, dvg = _gelu_and_grad(vb)
    mu = _group_mean(vg, avg)
    d = vg - mu
    var = _group_mean(d * d, avg)
    rstd = lax.rsqrt(var + EPS)
    vhat = d * rstd
    vn = vhat * ln_g + ln_b
    return ug, dug, dvg, rstd, vhat, vn


def _sgu_mix(w_ref, src_bf16, n_chunks):
    lane = lax.broadcasted_iota(jnp.int32, (SGU_CHUNK, LANES), 1)
    lo_half = lane < GROUP_DIM
    rows = []
    for n in range(n_chunks):
        slabs = []
        for a in range(D_BRANCH // LANES):
            blk = src_bf16[SGU_CHUNK * n:SGU_CHUNK * (n + 1), LANES * a:LANES * (a + 1)]
            zero = jnp.zeros_like(blk)
            m0 = jnp.dot(w_ref[2 * a], jnp.where(lo_half, blk, zero), preferred_element_type=F32)
            m1 = jnp.dot(w_ref[2 * a + 1], jnp.where(lo_half, zero, blk), preferred_element_type=F32)
            slabs.append(m0 + m1)
        rows.append(jnp.concatenate(slabs, axis=1))
    return jnp.concatenate(rows, axis=0)


def _sgu_fwd(proj, ln_g, ln_b, w_mask, bias_full):
    s = proj.shape[0]
    tm = min(512, s)
    n_chunks = tm // SGU_CHUNK

    def body(ub_ref, vb_ref, zb_ref, g_ref, b_ref, w_ref, bias_ref, yb_ref):
        avg = _group_avg_matrix()
        ug, _, _, _, _, vn = _sgu_forward_parts(ub_ref[...].astype(F32), vb_ref[...].astype(F32),
                                                g_ref[...], b_ref[...], avg)
        mixed = _sgu_mix(w_ref, vn.astype(BF16), n_chunks) + jnp.concatenate([bias_ref[...]] * n_chunks, axis=0)
        zb = zb_ref[...].astype(F32)
        yb_ref[...] = (ug * mixed * (zb * _sigmoid(zb))).astype(BF16)

    col = lambda c: pl.BlockSpec((tm, D_BRANCH), lambda i: (i, c))
    full = lambda shape: pl.BlockSpec(shape, lambda i: (0,) * len(shape))
    return pl.pallas_call(
        body, name="sgu_fwd", grid=(s // tm,),
        in_specs=[col(COL_UB), col(COL_VB), col(COL_ZB), full((1, D_BRANCH)), full((1, D_BRANCH)),
                  full((N_GROUPS, SGU_CHUNK, SGU_CHUNK)), full((SGU_CHUNK, D_BRANCH))],
        out_specs=pl.BlockSpec((tm, D_BRANCH), lambda i: (i, 0)),
        out_shape=jax.ShapeDtypeStruct((s, D_BRANCH), BF16),
        compiler_params=_cparams(("parallel",)),
    )(proj, proj, proj, ln_g, ln_b, w_mask, bias_full)


def _sgu_bwd(proj, dyb, ln_g, ln_b, w_mask, w_mask_t, bias_full, grads):
    s = proj.shape[0]
    tm = min(512, s)
    n_chunks = tm // SGU_CHUNK
    n_steps = s // tm
    plan = ((1, _ALL_DEVS), (2, _ALL_DEVS), (3, _ALL_DEVS))
    n_g = len(plan)

    def body(ub_ref, vb_ref, zb_ref, dyb_ref, g_ref, b_ref, w_ref, wt_ref, bias_ref, *refs):
        g_src = refs[:n_g]
        dsgu_ref, dw_ref, db_ref, dg_ref, dbeta_ref = refs[n_g:n_g + 5]
        g_slots = refs[n_g + 5:2 * n_g + 5]
        dmix_acc, send_sems, recv_sems = refs[2 * n_g + 5:]
        i = pl.program_id(0)
        me = _me()

        @pl.when(i == 0)
        def _():
            _scatter_start(plan, g_src, g_slots, send_sems, recv_sems, me)
            dw_ref[...] = jnp.zeros_like(dw_ref)
            dg_ref[...] = jnp.zeros_like(dg_ref)
            dbeta_ref[...] = jnp.zeros_like(dbeta_ref)
            dmix_acc[...] = jnp.zeros_like(dmix_acc)

        avg = _group_avg_matrix()
        ln_gv = g_ref[...]
        ug, dug, dvg, rstd, vhat, vn = _sgu_forward_parts(ub_ref[...].astype(F32), vb_ref[...].astype(F32),
                                                          ln_gv, b_ref[...], avg)
        vnb = vn.astype(BF16)
        mixed = _sgu_mix(w_ref, vnb, n_chunks) + jnp.concatenate([bias_ref[...]] * n_chunks, axis=0)
        zb = zb_ref[...].astype(F32)
        sg = _sigmoid(zb)
        sz = zb * sg
        dsz = sg * (1.0 + zb * (1.0 - sg))
        dy = dyb_ref[...].astype(F32)
        dmixed = dy * ug * sz
        du = dy * mixed * sz * dug
        dzb = dy * ug * mixed * dsz
        dmb = dmixed.astype(BF16)
        dvn = _sgu_mix(wt_ref, dmb, n_chunks)

        lane = lax.broadcasted_iota(jnp.int32, (SGU_CHUNK, LANES), 1)
        lo_half = lane < GROUP_DIM
        dm_sum = None
        for n in range(n_chunks):
            rows = slice(SGU_CHUNK * n, SGU_CHUNK * (n + 1))
            dm_sum = dmixed[rows] if dm_sum is None else dm_sum + dmixed[rows]
            for a in range(D_BRANCH // LANES):
                cols = slice(LANES * a, LANES * (a + 1))
                dblk = dmb[rows, cols]
                vblk = vnb[rows, cols]
                zero = jnp.zeros_like(dblk)
                dw_ref[2 * a] += _dot_nt(jnp.where(lo_half, dblk, zero), vblk)
                dw_ref[2 * a + 1] += _dot_nt(jnp.where(lo_half, zero, dblk), vblk)
        dmix_acc[...] += dm_sum

        dg_ref[...] += jnp.sum(dvn * vhat, axis=0, keepdims=True)
        dbeta_ref[...] += jnp.sum(dvn, axis=0, keepdims=True)
        dvh = dvn * ln_gv
        m1 = _group_mean(dvh, avg)
        m2 = _group_mean(dvh * vhat, avg)
        dv = rstd * (dvh - m1 - vhat * m2) * dvg
        dsgu_ref[:, 0:D_BRANCH] = du.astype(BF16)
        dsgu_ref[:, D_BRANCH:2 * D_BRANCH] = dv.astype(BF16)
        dsgu_ref[:, 2 * D_BRANCH:3 * D_BRANCH] = dzb.astype(BF16)

        @pl.when(i == n_steps - 1)
        def _():
            pos = lax.broadcasted_iota(jnp.int32, (SGU_CHUNK, SGU_CHUNK), 0) >> GROUP_SHIFT
            src = lax.broadcasted_iota(jnp.int32, (SGU_CHUNK, SGU_CHUNK), 1) >> GROUP_SHIFT
            keep = src <= pos
            for g in range(N_GROUPS):
                dw_ref[g] = jnp.where(keep, dw_ref[g], 0.0)
            grp = lax.broadcasted_iota(jnp.int32, (D_BRANCH, LANES), 0) >> GROUP_SHIFT
            sel = (grp == lax.broadcasted_iota(jnp.int32, (D_BRANCH, LANES), 1)).astype(BF16)
            db_ref[...] = _split_dot(dmix_acc[...], sel, 3)
            _scatter_wait(plan, g_src, g_slots, send_sems, recv_sems, me)

    col = lambda c: pl.BlockSpec((tm, D_BRANCH), lambda i: (i, c))
    full = lambda shape: pl.BlockSpec(shape, lambda i: (0,) * len(shape))
    any_spec = pl.BlockSpec(memory_space=pl.ANY)
    return pl.pallas_call(
        body, name="sgu_bwd", grid=(n_steps,),
        in_specs=[col(COL_UB), col(COL_VB), col(COL_ZB), pl.BlockSpec((tm, D_BRANCH), lambda i: (i, 0)),
                  full((1, D_BRANCH)), full((1, D_BRANCH)),
                  full((N_GROUPS, SGU_CHUNK, SGU_CHUNK)), full((N_GROUPS, SGU_CHUNK, SGU_CHUNK)),
                  full((SGU_CHUNK, D_BRANCH))] + [any_spec] * n_g,
        out_specs=[pl.BlockSpec((tm, 3 * D_BRANCH), lambda i: (i, 0)),
                   full((N_GROUPS, SGU_CHUNK, SGU_CHUNK)), full((SGU_CHUNK, LANES)),
                   full((1, D_BRANCH)), full((1, D_BRANCH))] + [any_spec] * n_g,
        out_shape=[jax.ShapeDtypeStruct((s, 3 * D_BRANCH), BF16),
                   jax.ShapeDtypeStruct((N_GROUPS, SGU_CHUNK, SGU_CHUNK), F32),
                   jax.ShapeDtypeStruct((SGU_CHUNK, LANES), F32),
                   jax.ShapeDtypeStruct((1, D_BRANCH), F32), jax.ShapeDtypeStruct((1, D_BRANCH), F32)]
        + [jax.ShapeDtypeStruct((N_DEV,) + _GRAD_PIECE_SHAPES[a], BF16) for a, _ in plan],
        scratch_shapes=[pltpu.VMEM((SGU_CHUNK, D_BRANCH), F32),
                        pltpu.SemaphoreType.DMA((N_DEV * n_g,)), pltpu.SemaphoreType.DMA((N_DEV * n_g,))],
        compiler_params=pltpu.CompilerParams(dimension_semantics=("arbitrary",), vmem_limit_bytes=VMEM_LIMIT,
                                             has_side_effects=True),
    )(proj, proj, proj, dyb, ln_g, ln_b, w_mask, w_mask_t, bias_full, *grads)


def _mid(proj, ya, yb, o, x, target, final_g, w_up_a, w_up_b, w_out):
    s = x.shape[0]
    tm = min(256, s)
    n_steps = s // tm
    half = D_MODEL // 2

    def body(ya_ref, yb_ref, o_ref, za_ref, ga0_ref, ga1_ref, gb0_ref, gb1_ref, x_ref, t_ref, gf_ref,
             wa_ref, wb_ref, wo_ref,
             dzg_ref, do_ref, dyb_ref, dx2_ref, gwo_ref, gwa_ref, gwb_ref, loss_ref, dgf_ref,
             acc_o, acc_a, acc_b):
        i = pl.program_id(0)

        @pl.when(i == 0)
        def _():
            acc_o[...] = jnp.zeros_like(acc_o)
            acc_a[...] = jnp.zeros_like(acc_a)
            acc_b[...] = jnp.zeros_like(acc_b)
            loss_ref[...] = jnp.zeros_like(loss_ref)
            dgf_ref[...] = jnp.zeros_like(dgf_ref)

        ya_v = ya_ref[...]
        yb_v = yb_ref[...]
        pa = jnp.dot(ya_v, wa_ref[...], preferred_element_type=F32)
        pb = jnp.dot(yb_v, wb_ref[...], preferred_element_type=F32)
        sa = _sigmoid(jnp.concatenate([ga0_ref[...], ga1_ref[...]], axis=1).astype(F32))
        sb = _sigmoid(jnp.concatenate([gb0_ref[...], gb1_ref[...]], axis=1).astype(F32))
        merged = (sa * pa + sb * pb).astype(BF16)
        x2 = x_ref[...] + jnp.dot(merged, wo_ref[...], preferred_element_type=F32)
        r2 = lax.rsqrt(jnp.mean(x2 * x2, axis=-1, keepdims=True) + EPS)
        xh = x2 * r2
        gf = gf_ref[...]
        diff = xh * gf - t_ref[...]
        loss_ref[...] += 0.5 * jnp.sum(jnp.mean(diff * diff, axis=-1, keepdims=True))
        dy = diff * (1.0 / D_MODEL)
        dgf_ref[...] += jnp.sum(dy * xh, axis=0, keepdims=True)
        dyg = dy * gf
        dx2 = r2 * (dyg - xh * jnp.mean(dyg * xh, axis=-1, keepdims=True))
        dx2_ref[...] = dx2
        dx2b = dx2.astype(BF16)
        dmerged = _dot_nt(dx2b, wo_ref[...])
        acc_o[...] += _dot_tn(merged, dx2b)
        dpa = dmerged * sa
        dpb = dmerged * sb
        dzg_ref[:, D_BRANCH:D_BRANCH + D_MODEL] = (dpa * pa * (1.0 - sa)).astype(BF16)
        dzg_ref[:, D_BRANCH + D_MODEL:D_BRANCH + 2 * D_MODEL] = (dpb * pb * (1.0 - sb)).astype(BF16)
        dpab = dpa.astype(BF16)
        dpbb = dpb.astype(BF16)
        acc_a[...] += _dot_tn(ya_v, dpab)
        acc_b[...] += _dot_tn(yb_v, dpbb)
        dya = _dot_nt(dpab, wa_ref[...])
        dyb_ref[...] = _dot_nt(dpbb, wb_ref[...]).astype(BF16)
        za = za_ref[...].astype(F32)
        sg = _sigmoid(za)
        do_ref[...] = (dya * (za * sg)).astype(BF16)
        dzg_ref[:, 0:D_BRANCH] = (dya * o_ref[...].astype(F32) * (sg * (1.0 + za * (1.0 - sg)))).astype(BF16)

        @pl.when(i == n_steps - 1)
        def _():
            gwo_ref[...] = acc_o[...].astype(BF16)
            gwa_ref[...] = acc_a[...].astype(BF16)
            gwb_ref[...] = acc_b[...].astype(BF16)

    tok = lambda w: pl.BlockSpec((tm, w), lambda i: (i, 0))
    col = lambda c: pl.BlockSpec((tm, half), lambda i: (i, c))
    full = lambda shape: pl.BlockSpec(shape, lambda i: (0,) * len(shape))
    return pl.pallas_call(
        body, name="mid", grid=(n_steps,),
        in_specs=[tok(D_BRANCH), tok(D_BRANCH), tok(D_BRANCH), col(COL_ZA), col(COL_GA), col(COL_GA + 1),
                  col(COL_GB), col(COL_GB + 1), tok(D_MODEL), tok(D_MODEL), full((1, D_MODEL)),
                  full((D_BRANCH, D_MODEL)), full((D_BRANCH, D_MODEL)), full((D_MODEL, D_MODEL))],
        out_specs=[tok(D_BRANCH + 2 * D_MODEL), tok(D_BRANCH), tok(D_BRANCH), tok(D_MODEL),
                   full((D_MODEL, D_MODEL)), full((D_BRANCH, D_MODEL)), full((D_BRANCH, D_MODEL)),
                   full((8, LANES)), full((1, D_MODEL))],
        out_shape=[jax.ShapeDtypeStruct((s, D_BRANCH + 2 * D_MODEL), BF16),
                   jax.ShapeDtypeStruct((s, D_BRANCH), BF16), jax.ShapeDtypeStruct((s, D_BRANCH), BF16),
                   jax.ShapeDtypeStruct((s, D_MODEL), F32),
                   jax.ShapeDtypeStruct((D_MODEL, D_MODEL), BF16),
                   jax.ShapeDtypeStruct((D_BRANCH, D_MODEL), BF16), jax.ShapeDtypeStruct((D_BRANCH, D_MODEL), BF16),
                   jax.ShapeDtypeStruct((8, LANES), F32), jax.ShapeDtypeStruct((1, D_MODEL), F32)],
        scratch_shapes=[pltpu.VMEM((D_MODEL, D_MODEL), F32), pltpu.VMEM((D_BRANCH, D_MODEL), F32),
                        pltpu.VMEM((D_BRANCH, D_MODEL), F32)],
        compiler_params=_cparams(("arbitrary",)),
    )(ya, yb, o, proj, proj, proj, proj, proj, x, target, final_g, w_up_a, w_up_b, w_out)


def _dwin_piece(ht, piece, tile_of, prev):
    s = ht.shape[1]
    n_tiles = piece.shape[1] // D_BRANCH

    def body(ht_ref, p_ref, *rest):
        out_ref = rest[-1]
        out_ref[...] = jnp.dot(ht_ref[...], p_ref[...], preferred_element_type=F32).astype(BF16)

    in_specs = [pl.BlockSpec((D_MODEL, s), lambda j: (0, 0)), pl.BlockSpec((s, D_BRANCH), lambda j: (0, j))]
    args = [ht, piece]
    aliases = {}
    if prev is not None:
        in_specs.append(pl.BlockSpec(memory_space=pl.ANY))
        args.append(prev)
        aliases = {2: 0}
    return pl.pallas_call(
        body, name="dwin_piece", grid=(n_tiles,),
        in_specs=in_specs,
        out_specs=pl.BlockSpec((D_MODEL, D_BRANCH), lambda j: (0, tile_of(j))),
        out_shape=jax.ShapeDtypeStruct((D_MODEL, D_IN), BF16),
        input_output_aliases=aliases,
        compiler_params=_cparams(("parallel",)),
    )(*args)


def _dh_dx(pieces, w_in, x, norm_g, dx2, g_in):
    s = x.shape[0]
    tm = min(256, s)
    n_steps = s // tm
    arrays = []
    for arr, _, _, _ in pieces:
        if not any(arr is a for a in arrays):
            arrays.append(arr)
    n_arr = len(arrays)
    plan = [([k for k, a in enumerate(arrays) if a is arr][0], wcol, off, width) for arr, wcol, off, width in pieces]

    def body(*refs):
        p_refs = refs[:n_arr]
        (w_ref, x_ref, g_ref, dx2_ref, gin_ref, dx_ref, dg_ref, late_ref, pair_ref, stage_ref, sums_ref,
         send_sems, recv_sems) = refs[n_arr:]
        me = _me()

        @pl.when(pl.program_id(0) == 0)
        def _():
            dg_ref[...] = jnp.zeros_like(dg_ref)
            for dev in range(N_DEV):
                @pl.when(me == dev)
                def _():
                    for cp in _presum_hand_off(dev, 0, _LATE_CHIPS, gin_ref, late_ref, pair_ref, send_sems, recv_sems):
                        cp.start()

        @pl.when(pl.program_id(0) == min(1, n_steps - 1))
        def _():
            for dev in range(N_DEV):
                @pl.when(me == dev)
                def _():
                    _presum_send(dev, 0, _LATE_CHIPS, gin_ref, late_ref, pair_ref, stage_ref, sums_ref,
                                 send_sems, recv_sems)

        dh = None
        for k, wcol, off, width in plan:
            d = _dot_nt(p_refs[k][:, off:off + width], w_ref[:, wcol:wcol + width])
            dh = d if dh is None else dh + d
        xf = x_ref[...]
        r = lax.rsqrt(jnp.mean(xf * xf, axis=-1, keepdims=True) + EPS)
        xh = xf * r
        dg_ref[...] += jnp.sum(dh * xh, axis=0, keepdims=True)
        dhg = dh * g_ref[...]
        dx_ref[...] = r * (dhg - xh * jnp.mean(dhg * xh, axis=-1, keepdims=True)) + dx2_ref[...]

        @pl.when(pl.program_id(0) == n_steps - 1)
        def _():
            for dev in range(N_DEV):
                @pl.when(me == dev)
                def _():
                    _presum_wait(dev, 0, _LATE_CHIPS, gin_ref, late_ref, pair_ref, sums_ref, send_sems, recv_sems)

    tok = lambda w: pl.BlockSpec((tm, w), lambda i: (i, 0))
    full = lambda shape: pl.BlockSpec(shape, lambda i: (0,) * len(shape))
    any_spec = pl.BlockSpec(memory_space=pl.ANY)
    return pl.pallas_call(
        body, name="dh_dx", grid=(n_steps,),
        in_specs=[tok(a.shape[1]) for a in arrays] + [full((D_MODEL, D_IN)), tok(D_MODEL), full((1, D_MODEL)),
                                                      tok(D_MODEL), any_spec],
        out_specs=[tok(D_MODEL), full((1, D_MODEL)), any_spec],
        out_shape=[jax.ShapeDtypeStruct((s, D_MODEL), F32), jax.ShapeDtypeStruct((1, D_MODEL), F32),
                   jax.ShapeDtypeStruct((N_DEV,) + _GRAD_PIECE_SHAPES[0], BF16)],
        scratch_shapes=[pltpu.VMEM((len(_LATE_CHIPS),) + _GRAD_PIECE_SHAPES[0], BF16),
                        pltpu.VMEM(_GRAD_PIECE_SHAPES[0], BF16),
                        pltpu.VMEM((len(_LATE_CHIPS),) + _GRAD_PIECE_SHAPES[0], BF16),
                        pltpu.SemaphoreType.DMA((N_DEV + len(_LATE_CHIPS),)),
                        pltpu.SemaphoreType.DMA((N_DEV + len(_LATE_CHIPS),))],
        compiler_params=pltpu.CompilerParams(dimension_semantics=("arbitrary",), vmem_limit_bytes=VMEM_LIMIT,
                                             has_side_effects=True),
    )(*arrays, w_in, x, norm_g, dx2, g_in)


def _adamw(w, g, m, v):
    rows, cols = w.shape
    tr = max(t for t in range(8, 257, 8) if rows % t == 0)
    c1 =1.0 - ADAM_B1 ** ADAM_STEP
    c2 = 1.0 - ADAM_B2 ** ADAM_STEP

    def body(w_ref, g_ref, m_ref, v_ref, g_out_ref, d_ref, nm_ref, nv_ref):
        gv = g_ref[...]
        g_out_ref[...] = gv
        nm = ADAM_B1 * m_ref[...] + (1.0 - ADAM_B1) * gv
        nv = ADAM_B2 * v_ref[...] + (1.0 - ADAM_B2) * (gv * gv)
        d_ref[...] = -ADAM_LR * ((nm / c1) / (jnp.sqrt(nv / c2) + ADAM_EPS) + ADAM_WD * w_ref[...])
        nm_ref[...] = nm
        nv_ref[...] = nv

    spec = pl.BlockSpec((tr, cols), lambda i: (i, 0))
    return pl.pallas_call(
        body, name="adamw", grid=(rows // tr,),
        in_specs=[spec] * 4, out_specs=[spec] * 4,
        out_shape=[jax.ShapeDtypeStruct((rows, cols), F32)] * 4,
        compiler_params=_cparams(("parallel",)),
    )(w, g, m, v)


def _place():
    x, y, c = lax.axis_index("x"), lax.axis_index("y"), lax.axis_index("c")
    return x, y, c


def _gather_weights(w_in, w_up_a, w_up_b, w_out):
    shards = (w_in, w_up_a, w_up_b, w_out)
    n_arr = len(shards)
    col_sharded = (True, True, True, False)
    full_shapes = ((D_MODEL, D_IN), (D_BRANCH, D_MODEL), (D_BRANCH, D_MODEL), (D_MODEL, D_MODEL))

    def body(*refs):
        src = refs[:n_arr]
        out = refs[n_arr:2 * n_arr]
        stage = refs[2 * n_arr:3 * n_arr]
        cast = refs[3 * n_arr:4 * n_arr]
        send_sems, recv_sems, local_sems = refs[4 * n_arr:]
        x, y, c = _place()
        chip = 2 * x + y
        sibling = (x, y, 1 - c)
        others = [(1 - x, y), (x, 1 - y), (1 - x, 1 - y)]

        def region(a, chip_idx, half):
            r, w = shards[a].shape
            hr = r // 2
            if col_sharded[a]:
                return out[a].at[pl.ds(_aligned(half * hr, 16), hr), pl.ds(_aligned(chip_idx * w, LANES), w)]
            return out[a].at[pl.ds(_aligned(chip_idx * r + half * hr, 16), hr), :]

        loads = [pltpu.make_async_copy(src[a], stage[a], local_sems.at[a]) for a in range(n_arr)]
        for cp in loads:
            cp.start()
        for a in range(n_arr):
            loads[a].wait()
            cast[a][...] = stage[a][...].astype(BF16)
        stores = []
        for a in range(n_arr):
            hr = shards[a].shape[0] // 2
            for half in range(2):
                cp = pltpu.make_async_copy(cast[a].at[pl.ds(half * hr, hr), :], region(a, chip, half),
                                           local_sems.at[n_arr + 2 * a + half])
                cp.start()
                stores.append(cp)

        def remote(k, a, chip_idx, half, to, from_vmem):
            hr = shards[a].shape[0] // 2
            s_ref = cast[a].at[pl.ds(_aligned(half * hr, 16), hr), :] if from_vmem else region(a, chip_idx, half)
            return pltpu.make_async_remote_copy(src_ref=s_ref, dst_ref=region(a, chip_idx, half),
                                                send_sem=send_sems.at[k], recv_sem=recv_sems.at[k],
                                                device_id=to, device_id_type=MESH)

        first = []
        for j, (ox, oy) in enumerate(others):
            for a in range(n_arr):
                cp = remote(n_arr * j + a, a, chip, c, (ox, oy, c), True)
                cp.start()
                first.append(cp)
        passed = []
        for j, (ox, oy) in enumerate(others):
            ochip = 2 * ox + oy
            for a in range(n_arr):
                k = n_arr * j + a
                remote(k, a, ochip, c, sibling, False).wait_recv()
                cp = remote(3 * n_arr + k, a, ochip, c, sibling, False)
                cp.start()
                passed.append(cp)
        for j, (ox, oy) in enumerate(others):
            ochip = 2 * ox + oy
            for a in range(n_arr):
                remote(3 * n_arr + n_arr * j + a, a, ochip, 1 - c, sibling, False).wait_recv()
        for cp in first + passed:
            cp.wait_send()
        for cp in stores:
            cp.wait()

    any_spec = pl.BlockSpec(memory_space=pl.ANY)
    return pl.pallas_call(
        body, name="gather_weights",
        in_specs=[any_spec] * n_arr, out_specs=[any_spec] * n_arr,
        out_shape=[jax.ShapeDtypeStruct(sh, BF16) for sh in full_shapes],
        scratch_shapes=[pltpu.VMEM(a.shape, F32) for a in shards] + [pltpu.VMEM(a.shape, BF16) for a in shards]
        + [pltpu.SemaphoreType.DMA((6 * n_arr,)), pltpu.SemaphoreType.DMA((6 * n_arr,)),
           pltpu.SemaphoreType.DMA((3 * n_arr,))],
        compiler_params=pltpu.CompilerParams(vmem_limit_bytes=VMEM_LIMIT, has_side_effects=True),
    )(*shards)


def _reduce_grads_tail(grads, g_small, early_slots, late_in_slots):
    n_big = len(grads)
    n_arr = n_big + 1
    shard_shapes = [(2 * r, w) for r, w in _GRAD_PIECE_SHAPES]
    small_piece = (SMALL_PIECE, LANES)

    def body(*refs):
        src = refs[:n_arr]
        early = refs[n_arr:n_arr + n_big]
        late_in = refs[n_arr + n_big]
        n_in = n_arr + n_big + 1
        out = refs[n_in:n_in + n_arr]
        slots = refs[n_in + n_arr:n_in + 2 * n_arr]
        sums = refs[n_in + 2 * n_arr:n_in + 3 * n_arr]
        send1, recv1, send2, recv2, local_sems = refs[n_in + 3 * n_arr:]
        x, y, c = _place()
        me = 4 * x + 2 * y + c

        def piece_of(a, dev):
            return src[a].at[dev] if a == n_big else _grad_piece(src[a], a, dev)

        def late(a, dst_dev, src_dev):
            return pltpu.make_async_remote_copy(
                src_ref=piece_of(a, dst_dev), dst_ref=slots[a].at[src_dev],
                send_sem=send1.at[n_arr * dst_dev + a], recv_sem=recv1.at[n_arr * src_dev + a],
                device_id=_dev_id(dst_dev), device_id_type=MESH)

        def late_arrays(dev):
            return (n_big,)

        def load(a, dev, received):
            return pltpu.make_async_copy(received.at[dev], slots[a].at[dev], local_sems.at[n_arr * dev + a])

        def own(a, dev):
            return pltpu.make_async_copy(piece_of(a, dev), slots[a].at[dev], local_sems.at[n_arr * dev + a])

        for dev in range(N_DEV):
            @pl.when(me == dev)
            def _():
                received = [early[0] if dev in _EARLY_IN_DEVS else late_in] + list(early[1:])
                for a in range(n_arr):
                    own(a, dev).start()
                presummed = dev in _LATE_IN_DEVS
                filled = lambda a, peer: not (a == 0 and presummed) or peer in _presum_sources(dev)
                for peer in range(N_DEV):
                    if peer != dev:
                        for a in late_arrays(peer):
                            late(a, peer, dev).start()
                        for a in range(n_big):
                            if filled(a, peer):
                                load(a, peer, received[a]).start()
                            else:
                                slots[a][peer] = jnp.zeros(slots[a].shape[1:], slots[a].dtype)
                for a in range(n_arr):
                    own(a, dev).wait()
                for peer in range(N_DEV):
                    if peer != dev:
                        for a in late_arrays(dev):
                            late(a, dev, peer).wait_recv()
                        for a in range(n_big):
                            if filled(a, peer):
                                load(a, peer, received[a]).wait()

        for a in range(n_arr):
            rows = slots[a].shape[1]
            step = 64 if rows % 64 == 0 else 8

            def add_rows(t, carry, a=a, step=step):
                r0 = pl.multiple_of(t * step, step)
                total = slots[a][0, pl.ds(r0, step), :].astype(F32)
                for dev in range(1, N_DEV):
                    total = total + slots[a][dev, pl.ds(r0, step), :].astype(F32)
                sums[a][pl.ds(r0, step), :] = total
                return carry

            lax.fori_loop(0, rows // step, add_rows, 0)

        shares = []
        keeps = []
        for a in range(n_big):
            r, w = _GRAD_PIECE_SHAPES[a]
            dst = out[a].at[pl.ds(pl.multiple_of(c * r, 8), r), :]
            cp = pltpu.make_async_remote_copy(src_ref=sums[a], dst_ref=dst, send_sem=send2.at[a], recv_sem=recv2.at[a],
                                              device_id=(x, y, 1 - c), device_id_type=MESH)
            cp.start()
            shares.append(cp)
            kp = pltpu.make_async_copy(sums[a], dst, local_sems.at[N_DEV * n_arr + a])
            kp.start()
            keeps.append(kp)
        kp = pltpu.make_async_copy(sums[n_big], out[n_big].at[me], local_sems.at[N_DEV * n_arr + n_big])
        kp.start()
        keeps.append(kp)

        def small_share(dst_dev, src_dev):
            return pltpu.make_async_remote_copy(src_ref=sums[n_big], dst_ref=out[n_big].at[src_dev],
                                                send_sem=send2.at[n_big + dst_dev], recv_sem=recv2.at[n_big + src_dev],
                                                device_id=_dev_id(dst_dev), device_id_type=MESH)

        for dev in range(N_DEV):
            @pl.when(me != dev)
            def _():
                small_share(dev, me).start()
        for a in range(n_big):
            r, w = _GRAD_PIECE_SHAPES[a]
            other = out[a].at[pl.ds(pl.multiple_of((1 - c) * r, 8), r), :]
            pltpu.make_async_remote_copy(src_ref=sums[a], dst_ref=other, send_sem=send2.at[a], recv_sem=recv2.at[a],
                                         device_id=(x, y, 1 - c), device_id_type=MESH).wait_recv()
        for dev in range(N_DEV):
            @pl.when(me != dev)
            def _():
                small_share(dev, dev).wait_recv()
                small_share(dev, me).wait_send()
                for a in late_arrays(dev):
                    late(a, dev, me).wait_send()
        for cp in shares:
            cp.wait_send()
        for kp in keeps:
            kp.wait()

    any_spec = pl.BlockSpec(memory_space=pl.ANY)
    return pl.pallas_call(
        body, name="reduce_grads_tail",
        in_specs=[any_spec] * (n_arr + n_big + 1), out_specs=[any_spec] * n_arr,
        out_shape=[jax.ShapeDtypeStruct(sh, F32) for sh in shard_shapes]
        + [jax.ShapeDtypeStruct((N_DEV,) + small_piece, F32)],
        scratch_shapes=[pltpu.VMEM((N_DEV,) + sh, BF16) for sh in _GRAD_PIECE_SHAPES]
        + [pltpu.VMEM((N_DEV,) + small_piece, F32)]
        + [pltpu.VMEM(sh, F32) for sh in _GRAD_PIECE_SHAPES] + [pltpu.VMEM(small_piece, F32)]
        + [pltpu.SemaphoreType.DMA((N_DEV * n_arr,)), pltpu.SemaphoreType.DMA((N_DEV * n_arr,)),
           pltpu.SemaphoreType.DMA((n_big + N_DEV,)), pltpu.SemaphoreType.DMA((n_big + N_DEV,)),
           pltpu.SemaphoreType.DMA((N_DEV * n_arr + n_arr,))],
        compiler_params=pltpu.CompilerParams(vmem_limit_bytes=VMEM_LIMIT, has_side_effects=True),
    )(*grads, g_small, *early_slots, late_in_slots)


def _reduce_grads(g_in, g_up_a, g_up_b, g_out, g_small):
    big = (g_in, g_up_a, g_up_b, g_out)
    n_big = len(big)
    col_sharded = (True, True, True, False)
    piece_shapes = []
    for a, arr in enumerate(big):
        r, w = arr.shape
        piece_shapes.append((r // 2, w // N_CHIPS) if col_sharded[a] else (r // (2 * N_CHIPS), w))
    shard_shapes = [(2 * r, w) for r, w in piece_shapes]
    n_arr = n_big + 1

    def body(*refs):
        src = refs[:n_arr]
        out = refs[n_arr:2 * n_arr]
        slots = refs[2 * n_arr:3 * n_arr]
        sums = refs[3 * n_arr:4 * n_arr]
        send1, recv1, send2, recv2, local_sems = refs[4 * n_arr:]
        x, y, c = _place()
        me = 4 * x + 2 * y + c

        def piece_of(a, dev):
            chip_idx, half = dev // 2, dev % 2
            if a == n_big:
                return src[a].at[dev]
            r, w = piece_shapes[a]
            if col_sharded[a]:
                return src[a].at[pl.ds(pl.multiple_of(half * r, 16), r), pl.ds(pl.multiple_of(chip_idx * w, LANES), w)]
            return src[a].at[pl.ds(pl.multiple_of(dev * r, 16), r), :]

        def dev_id(dev):
            return (dev // 4, (dev // 2) % 2, dev % 2)

        own = [pltpu.make_async_copy(piece_of(a, me), slots[a].at[me], local_sems.at[a]) for a in range(n_arr)]
        for cp in own:
            cp.start()
        sends = []
        for d in range(1, N_DEV):
            peer = (me + d) % N_DEV
            for a in range(n_arr):
                cp = pltpu.make_async_remote_copy(
                    src_ref=piece_of(a, peer), dst_ref=slots[a].at[me],
                    send_sem=send1.at[n_arr * peer + a], recv_sem=recv1.at[n_arr * me + a],
                    device_id=dev_id(peer), device_id_type=MESH)
                cp.start()
                sends.append(cp)
        for cp in own:
            cp.wait()
        for d in range(1, N_DEV):
            peer = (me + d) % N_DEV
            for a in range(n_arr):
                pltpu.make_async_remote_copy(
                    src_ref=piece_of(a, peer), dst_ref=slots[a].at[peer],
                    send_sem=send1.at[n_arr * peer + a], recv_sem=recv1.at[n_arr * peer + a],
                    device_id=dev_id(peer), device_id_type=MESH).wait_recv()
        for a in range(n_arr):
            rows = slots[a].shape[1]
            step = 64 if rows % 64 == 0 else 8

            def add_rows(t, carry, a=a, step=step):
                r0 = pl.multiple_of(t * step, step)
                total = slots[a][0, pl.ds(r0, step), :].astype(F32)
                for dev in range(1, N_DEV):
                    total = total + slots[a][dev, pl.ds(r0, step), :].astype(F32)
                sums[a][pl.ds(r0, step), :] = total
                return carry

            lax.fori_loop(0, rows // step, add_rows, 0)
        shares = []
        keeps = []
        for a in range(n_big):
            r, w = piece_shapes[a]
            dst = out[a].at[pl.ds(pl.multiple_of(c * r, 8), r), :]
            cp = pltpu.make_async_remote_copy(src_ref=sums[a], dst_ref=dst, send_sem=send2.at[a], recv_sem=recv2.at[a],
                                              device_id=(x, y, 1 - c), device_id_type=MESH)
            cp.start()
            shares.append(cp)
            kp = pltpu.make_async_copy(sums[a], dst, local_sems.at[n_arr + a])
            kp.start()
            keeps.append(kp)
        kp = pltpu.make_async_copy(sums[n_big], out[n_big].at[me], local_sems.at[n_arr + n_big])
        kp.start()
        keeps.append(kp)
        for d in range(1, N_DEV):
            peer = (me + d) % N_DEV
            cp = pltpu.make_async_remote_copy(src_ref=sums[n_big], dst_ref=out[n_big].at[me],
                                              send_sem=send2.at[n_big + peer], recv_sem=recv2.at[n_big + me],
                                              device_id=dev_id(peer), device_id_type=MESH)
            cp.start()
            shares.append(cp)
        for a in range(n_big):
            r, w = piece_shapes[a]
            other = out[a].at[pl.ds(pl.multiple_of((1 - c) * r, 8), r), :]
            pltpu.make_async_remote_copy(src_ref=sums[a], dst_ref=other, send_sem=send2.at[a], recv_sem=recv2.at[a],
                                         device_id=(x, y, 1 - c), device_id_type=MESH).wait_recv()
        for d in range(1, N_DEV):
            peer = (me + d) % N_DEV
            pltpu.make_async_remote_copy(src_ref=sums[n_big], dst_ref=out[n_big].at[peer],
                                         send_sem=send2.at[n_big + peer], recv_sem=recv2.at[n_big + peer],
                                         device_id=dev_id(peer), device_id_type=MESH).wait_recv()
        for cp in sends + shares:
            cp.wait_send()
        for kp in keeps:
            kp.wait()

    any_spec = pl.BlockSpec(memory_space=pl.ANY)
    small_piece = (SMALL_PIECE, LANES)
    return pl.pallas_call(
        body, name="reduce_grads",
        in_specs=[any_spec] * n_arr, out_specs=[any_spec] * n_arr,
        out_shape=[jax.ShapeDtypeStruct(sh, F32) for sh in shard_shapes]
        + [jax.ShapeDtypeStruct((N_DEV,) + small_piece, F32)],
        scratch_shapes=[pltpu.VMEM((N_DEV,) + sh, BF16) for sh in piece_shapes]
        + [pltpu.VMEM((N_DEV,) + small_piece, F32)]
        + [pltpu.VMEM(sh, F32) for sh in piece_shapes] + [pltpu.VMEM(small_piece, F32)]
        + [pltpu.SemaphoreType.DMA((N_DEV * n_arr,)), pltpu.SemaphoreType.DMA((N_DEV * n_arr,)),
           pltpu.SemaphoreType.DMA((n_big + N_DEV,)), pltpu.SemaphoreType.DMA((n_big + N_DEV,)),
           pltpu.SemaphoreType.DMA((2 * n_arr,))],
        compiler_params=pltpu.CompilerParams(vmem_limit_bytes=VMEM_LIMIT, has_side_effects=True),
    )(*big, g_small)


_SMALL_PARTS = (("norm_g", 8), ("sgu_ln_g", 8), ("sgu_ln_b", 8), ("w_spatial", 1024), ("b_spatial", 8),
                ("final_norm_g", 8))
_LOSS_ROW = sum(n for _, n in _SMALL_PARTS)


def _pack_small(parts, loss_tile=None):
    rows = []
    for name, n_rows in _SMALL_PARTS:
        a = parts[name].reshape(-1, LANES).astype(F32)
        a = jnp.pad(a, ((0, n_rows - a.shape[0]), (0, 0)))
        rows.append(a)
    rows.append(jnp.zeros((8, LANES), F32) if loss_tile is None else loss_tile)
    rows.append(jnp.zeros((SMALL_ROWS - _LOSS_ROW - 8, LANES), F32))
    return jnp.concatenate(rows, axis=0)


def _unpack_small(packed, shapes):
    out = {}
    r0 = 0
    for name, n_rows in _SMALL_PARTS:
        n = math.prod(shapes[name])
        out[name] = packed[r0:r0 + n // LANES].reshape(shapes[name])
        r0 += n_rows
    return out


def _local_step(proj, ht, x, target, norm_g, w_in, sgu_ln_g, sgu_ln_b, w_spatial, b_spatial, w_up_a, w_up_b, w_out,
                final_norm_g, bq, bk):
    pos = jnp.arange(SGU_CHUNK)
    keep = (pos[None, :] // SGU_SUBCHUNK) <= (pos[:, None] // SGU_SUBCHUNK)
    w_mask = jnp.where(keep[None], w_spatial, 0.0).astype(BF16)
    w_mask_t = jnp.swapaxes(w_mask, 1, 2)
    bias_full = jnp.repeat(b_spatial.T, GROUP_DIM, axis=1)
    ln_g = sgu_ln_g.reshape(1, D_BRANCH)
    ln_b = sgu_ln_b.reshape(1, D_BRANCH)
    final_g = final_norm_g.reshape(1, D_MODEL)

    o, ya, rsave = _attn_fwd(proj, bq, bk, ATTN_PAIRS)
    yb = _sgu_fwd(proj, ln_g, ln_b, w_mask, bias_full)
    dzg, do, dyb, dx2, g_out, g_up_a, g_up_b, loss_acc, d_final = _mid(
        proj, ya, yb, o, x, target, final_g, w_up_a, w_up_b, w_out)
    dsgu, d_wsp, d_bsp, d_lng, d_lnb, *up_out_slots = _sgu_bwd(proj, dyb, ln_g, ln_b, w_mask, w_mask_t, bias_full,
                                                               (g_up_a, g_up_b, g_out))
    g_in = _dwin_piece(ht, dzg, lambda j: jnp.where(j == 0, COL_ZA, COL_GA - 1 + j), None)
    g_in = _dwin_piece(ht, dsgu, lambda j: COL_UB + j, g_in)
    dq, dk, dv, early_in_slots = _attn_bwd(proj, do, rsave, bq, bk, ATTN_PAIRS, g_in)
    early_slots = [early_in_slots] + up_out_slots
    g_in = _dwin_piece(ht, dq, lambda j: COL_Q + j, g_in)
    g_in = _dwin_piece(ht, dk, lambda j: COL_K + j, g_in)
    g_in = _dwin_piece(ht, dv, lambda j: COL_V + j, g_in)
    pieces = [(dq, COL_Q * D_BRANCH, 0, D_BRANCH), (dk, COL_K * D_BRANCH, 0, D_BRANCH),
              (dv, COL_V * D_BRANCH, 0, D_BRANCH), (dzg, COL_ZA * D_BRANCH, 0, D_BRANCH),
              (dsgu, COL_UB * D_BRANCH, 0, 3 * D_BRANCH), (dzg, COL_GA * D_BRANCH, D_BRANCH, 2 * D_MODEL)]
    dx, d_norm, late_in_slots = _dh_dx(pieces, w_in, x, norm_g, dx2, g_in)
    small = {"norm_g": d_norm, "sgu_ln_g": d_lng, "sgu_ln_b": d_lnb, "w_spatial": d_wsp,
             "b_spatial": d_bsp[:, :N_GROUPS].T, "final_norm_g": d_final}
    return loss_acc, dx, (g_in, g_up_a, g_up_b, g_out), small, early_slots, late_in_slots


def kernel(x, norm_g, w_in, sgu_ln_g, sgu_ln_b, w_spatial, b_spatial, w_up_a, w_up_b, w_out, final_norm_g, loss_target, m_norm_g, m_w_in, m_sgu_ln_g, m_sgu_ln_b, m_w_spatial, m_b_spatial, m_w_up_a, m_w_up_b, m_w_out, m_final_norm_g, v_norm_g, v_w_in, v_sgu_ln_g, v_sgu_ln_b, v_w_spatial, v_b_spatial, v_w_up_a, v_w_up_b, v_w_out, v_final_norm_g):
    big_names = ("w_in", "w_up_a", "w_up_b", "w_out")
    small_names = tuple(n for n, _ in _SMALL_PARTS)
    names = ("norm_g", "w_in", "sgu_ln_g", "sgu_ln_b", "w_spatial", "b_spatial", "w_up_a", "w_up_b", "w_out",
             "final_norm_g")
    w = dict(norm_g=norm_g, w_in=w_in, sgu_ln_g=sgu_ln_g, sgu_ln_b=sgu_ln_b, w_spatial=w_spatial,
             b_spatial=b_spatial, w_up_a=w_up_a, w_up_b=w_up_b, w_out=w_out, final_norm_g=final_norm_g)
    m = dict(norm_g=m_norm_g, w_in=m_w_in, sgu_ln_g=m_sgu_ln_g, sgu_ln_b=m_sgu_ln_b, w_spatial=m_w_spatial,
             b_spatial=m_b_spatial, w_up_a=m_w_up_a, w_up_b=m_w_up_b, w_out=m_w_out, final_norm_g=m_final_norm_g)
    v = dict(norm_g=v_norm_g, w_in=v_w_in, sgu_ln_g=v_sgu_ln_g, sgu_ln_b=v_sgu_ln_b, w_spatial=v_w_spatial,
             b_spatial=v_b_spatial, w_up_a=v_w_up_a, w_up_b=v_w_up_b, w_out=v_w_out, final_norm_g=v_final_norm_g)
    shapes = {n: w[n].shape for n in names}
    flat2d = lambda a: a.reshape(a.shape[-2:])

    proj, ht, *full = _in_proj_gather(x[0], norm_g, *[flat2d(w[n]) for n in big_names])
    loss, dx, big_grads, small, early_slots, late_in_slots = _local_step(
        proj, ht, x[0], loss_target[0], norm_g, full[0], sgu_ln_g[0], sgu_ln_b[0], w_spatial[0], b_spatial[0],
        full[1], full[2], full[3], final_norm_g, ATTN_Q_BLOCK, ATTN_K_BLOCK)
    packed = _pack_small(small, loss).reshape(N_DEV, SMALL_PIECE, LANES)
    red = _reduce_grads_tail(big_grads, packed, early_slots, late_in_slots)

    grads, deltas, new_m, new_v = {}, {}, {}, {}
    for n, g in zip(big_names, red[:4]):
        g, d, nm, nv = _adamw(flat2d(w[n]), g, flat2d(m[n]), flat2d(v[n]))
        grads[n], deltas[n], new_m[n], new_v[n] = (a.reshape(shapes[n]) for a in (g, d, nm, nv))
    g_small = red[4].reshape(SMALL_ROWS, LANES)
    g_small, d, nm, nv = _adamw(_pack_small({n: w[n] for n in small_names}), g_small,
                                _pack_small({n: m[n] for n in small_names}),
                                _pack_small({n: v[n] for n in small_names}))
    for src, dst in ((g_small, grads), (d, deltas), (nm, new_m), (nv, new_v)):
        dst.update(_unpack_small(src, shapes))

    return (g_small[_LOSS_ROW, 0], dx[None], *[grads[n] for n in names], *[deltas[n] for n in names],
            *[new_m[n] for n in names], *[new_v[n] for n in names])
```

```python
import functools
import math

import jax
import jax.numpy as jnp
from jax import lax
from jax.experimental import pallas as pl
from jax.experimental.pallas import tpu as pltpu

F32 = jnp.float32
BF16 = jnp.bfloat16

D_MODEL = 1024
N_HEADS = 8
HEAD_DIM = 64
D_BRANCH = 512
D_IN = 4 * D_BRANCH + 3 * D_BRANCH + 2 * D_MODEL
N_GROUPS = 8
GROUP_DIM = 64
SGU_CHUNK = 128
SGU_SUBCHUNK = 64
GROUP_SHIFT = 6
EPS = 1e-6
LANES = 128
ATTN_Q_BLOCK = 256
ATTN_K_BLOCK = 256
DEAD = -110.0
SKIPPED = -1e30
SCAN_PASSES = 1
ATTN_PAIRS = 2
N_CHIPS = 4
N_DEV = 8
MESH = pl.DeviceIdType.MESH

ADAM_LR = 0.001
ADAM_B1 = 0.9
ADAM_B2 = 0.999
ADAM_EPS = 1e-08
ADAM_WD = 0.01
ADAM_STEP = 10

COL_Q, COL_K, COL_V, COL_ZA, COL_UB, COL_VB, COL_ZB, COL_GA, COL_GB = 0, 1, 2, 3, 4, 5, 6, 7, 9

VMEM_LIMIT = 56 * 1024 * 1024

SMALL_ROWS = 1088
SMALL_PIECE = SMALL_ROWS // N_DEV


def _cparams(sem=None):
    return pltpu.CompilerParams(dimension_semantics=sem, vmem_limit_bytes=VMEM_LIMIT)


def _aligned(v, m):
    return v if isinstance(v, int) else pl.multiple_of(v, m)


def _sigmoid(x):
    return 1.0 / (1.0 + jnp.exp(-x))


def _gelu_and_grad(x):
    k = math.sqrt(2.0 / math.pi)
    x2 = x * x
    inner = k * (x + 0.044715 * x * x2)
    th = jnp.tanh(inner)
    g = 0.5 * x * (1.0 + th)
    dg = 0.5 * (1.0 + th) + 0.5 * x * (1.0 - th * th) * (k * (1.0 + 3.0 * 0.044715 * x2))
    return g, dg


def _split_dot(a, b_bf16, passes):
    out = None
    rem = a
    for _ in range(passes):
        part = rem.astype(BF16)
        d = jnp.dot(part, b_bf16, preferred_element_type=F32)
        out = d if out is None else out + d
        rem = rem - part.astype(F32)
    return out


def _dot_nt(a, b):
    return lax.dot_general(a, b, (((1,), (1,)), ((), ())), preferred_element_type=F32)


def _dot_tn(a, b):
    return lax.dot_general(a, b, (((0,), (0,)), ((), ())), preferred_element_type=F32)


def _place():
    x, y, c = lax.axis_index("x"), lax.axis_index("y"), lax.axis_index("c")
    return x, y, c


def _in_proj_gather(x, norm_g, w_in, w_up_a, w_up_b, w_out):
    s = x.shape[0]
    tm = min(512, s)
    nt = s // tm
    shards = (w_in, w_up_a, w_up_b, w_out)
    n_arr = len(shards)
    col_sharded = (True, True, True, False)
    full_shapes = ((D_MODEL, D_IN), (D_BRANCH, D_MODEL), (D_BRANCH, D_MODEL), (D_MODEL, D_MODEL))
    w_shard = w_in.shape[1]
    half_rows = D_MODEL // 2
    stage_rows = 256

    def body(order_ref, x_ref, g_ref, *refs):
        src = refs[:n_arr]
        proj_ref, ht_ref = refs[n_arr:n_arr + 2]
        out = refs[n_arr + 2:2 * n_arr + 2]
        wsc, h_scr, stage = refs[2 * n_arr + 2:2 * n_arr + 5]
        small_stage = refs[2 * n_arr + 5:2 * n_arr + 8]
        small_cast = refs[2 * n_arr + 8:2 * n_arr + 11]
        send_sems, recv_sems, local_sems = refs[2 * n_arr + 11:]
        k = pl.program_id(0)
        i = pl.program_id(1)
        x_, y_, c = _place()
        chip = 2 * x_ + y_
        sibling = (x_, y_, 1 - c)
        others = [(x_, 1 - y_), (1 - x_, y_), (1 - x_, 1 - y_)]

        def region(a, chip_idx, half):
            if a == 0:
                return wsc.at[chip_idx, pl.ds(_aligned(half * half_rows, 16), half_rows), :]
            r, w = shards[a].shape
            hr = r // 2
            if col_sharded[a]:
                return out[a].at[pl.ds(_aligned(half * hr, 16), hr), pl.ds(_aligned(chip_idx * w, LANES), w)]
            return out[a].at[pl.ds(_aligned(chip_idx * r + half * hr, 16), hr), :]

        def remote(kk, a, chip_idx, half, to, own):
            s_ref = region(a, chip_idx, half)
            if own and a > 0:
                hr = shards[a].shape[0] // 2
                s_ref = small_cast[a - 1].at[pl.ds(_aligned(half * hr, 16), hr), :]
            return pltpu.make_async_remote_copy(src_ref=s_ref, dst_ref=region(a, chip_idx, half),
                                                send_sem=send_sems.at[kk], recv_sem=recv_sems.at[kk],
                                                device_id=to, device_id_type=MESH)

        def keep_whole(kk, chip_idx):
            return pltpu.make_async_copy(wsc.at[chip_idx],
                                         out[0].at[:, pl.ds(_aligned(chip_idx * w_shard, LANES), w_shard)],
                                         local_sems.at[kk])

        def small_stores():
            cps = []
            for a in range(1, n_arr):
                hr = shards[a].shape[0] // 2
                for half in range(2):
                    cps.append(pltpu.make_async_copy(small_cast[a - 1].at[pl.ds(half * hr, hr), :],
                                                     region(a, chip, half), local_sems.at[4 + 2 * (a - 1) + half]))
            return cps

        def arrive_and_pass(j):
            ochip = chip ^ j
            for a in range(n_arr):
                kk = n_arr * (j - 1) + a
                remote(kk, a, ochip, c, sibling, False).wait_recv()
                remote(3 * n_arr + kk, a, ochip, c, sibling, False).start()

        def from_sibling(j, a):
            remote(3 * n_arr + n_arr * (j - 1) + a, a, chip ^ j, 1 - c, sibling, False).wait_recv()

        @pl.when((k == 0) & (i == 0))
        def _():
            def cast_rows(half):
                for t in range(half_rows // stage_rows):
                    r0 = pl.multiple_of(half * half_rows + t * stage_rows, stage_rows)
                    pltpu.sync_copy(src[0].at[pl.ds(r0, stage_rows), :], stage)
                    wsc[chip, pl.ds(r0, stage_rows), :] = stage[...].astype(BF16)

            cast_rows(c)
            for j in (1, 2):
                remote(n_arr * (j - 1), 0, chip, c, (*others[j - 1], c), True).start()
            cast_rows(1 - c)
            for a in range(1, n_arr):
                pltpu.sync_copy(src[a], small_stage[a - 1])
                small_cast[a - 1][...] = small_stage[a - 1][...].astype(BF16)
            for j in (1, 2):
                for a in range(1, n_arr):
                    remote(n_arr * (j - 1) + a, a, chip, c, (*others[j - 1], c), True).start()
            keep_whole(0, chip).start()
            for cp in small_stores():
                cp.start()

        @pl.when((k == 1) & (i == 0))
        def _():
            for j in (1, 2):
                remote(n_arr * (j - 1), 0, chip, c, (*others[j - 1], c), True).wait_send()
            for a in range(n_arr):
                remote(n_arr * 2 + a, a, chip, c, (*others[2], c), True).start()
            arrive_and_pass(1)
            arrive_and_pass(2)
            from_sibling(1, 0)
            keep_whole(1, chip ^ 1).start()

        @pl.when((k == 2) & (i == 0))
        def _():
            from_sibling(2, 0)
            keep_whole(2, chip ^ 2).start()
            arrive_and_pass(3)

        @pl.when((k == 3) & (i == 0))
        def _():
            from_sibling(3, 0)
            keep_whole(3, chip ^ 3).start()

        @pl.when(k == 0)
        def _():
            xf = x_ref[...]
            r = lax.rsqrt(jnp.mean(xf * xf, axis=-1, keepdims=True) + EPS)
            h = xf * r * g_ref[...]
            h_scr[i] = h.astype(BF16)
            ht_ref[...] = h.T.astype(BF16)

        proj_ref[...] = jnp.dot(h_scr[i], wsc[order_ref[k]], preferred_element_type=F32).astype(BF16)

        @pl.when((k == 3) & (i == nt - 1))
        def _():
            for j in (1, 2, 3):
                for a in range(1, n_arr):
                    from_sibling(j, a)
            for j in (1, 2, 3):
                for a in range(n_arr):
                    kk = n_arr * (j - 1) + a
                    if a > 0 or j == 3:
                        remote(kk, a, chip, c, (*others[j - 1], c), True).wait_send()
                    remote(3 * n_arr + kk, a, chip ^ j, c, sibling, False).wait_send()
            for kk in range(4):
                keep_whole(kk, chip ^ kk).wait()
            for cp in small_stores():
                cp.wait()

    any_spec = pl.BlockSpec(memory_space=pl.ANY)
    tile = lambda kk, ii: jnp.where(kk == 0, ii, nt - 1)
    grid_spec = pltpu.PrefetchScalarGridSpec(
        num_scalar_prefetch=1, grid=(N_CHIPS, nt),
        in_specs=[pl.BlockSpec((tm, D_MODEL), lambda kk, ii, order: (tile(kk, ii), 0)),
                  pl.BlockSpec((1, D_MODEL), lambda kk, ii, order: (0, 0))] + [any_spec] * n_arr,
        out_specs=[pl.BlockSpec((tm, w_shard), lambda kk, ii, order: (ii, order[kk])),
                   pl.BlockSpec((D_MODEL, tm), lambda kk, ii, order: (0, tile(kk, ii)))] + [any_spec] * n_arr,
        scratch_shapes=[pltpu.VMEM((N_CHIPS, D_MODEL, w_shard), BF16), pltpu.VMEM((nt, tm, D_MODEL), BF16),
                        pltpu.VMEM((stage_rows, w_shard), F32)]
        + [pltpu.VMEM(a.shape, F32) for a in shards[1:]] + [pltpu.VMEM(a.shape, BF16) for a in shards[1:]]
        + [pltpu.SemaphoreType.DMA((6 * n_arr,)), pltpu.SemaphoreType.DMA((6 * n_arr,)),
           pltpu.SemaphoreType.DMA((4 + 2 * (n_arr - 1),))])
    x_, y_, _ = _place()
    order = (2 * x_ + y_) ^ jnp.arange(N_CHIPS, dtype=jnp.int32)
    return pl.pallas_call(
        body, name="in_proj_gather", grid_spec=grid_spec,
        out_shape=[jax.ShapeDtypeStruct((s, D_IN), BF16), jax.ShapeDtypeStruct((D_MODEL, s), BF16)]
        + [jax.ShapeDtypeStruct(sh, BF16) for sh in full_shapes],
        compiler_params=pltpu.CompilerParams(dimension_semantics=("arbitrary", "arbitrary"),
                                             vmem_limit_bytes=VMEM_LIMIT, has_side_effects=True),
    )(order, x, norm_g, *shards)


def _neg_softplus_parts(z):
    zb = z.astype(BF16)
    p = jnp.exp(-jnp.abs(zb))
    return p, jnp.maximum(zb, jnp.zeros_like(zb)) + jnp.log(1.0 + p)


def _split_cat(a, passes):
    parts = []
    rem = a
    for k in range(passes):
        part = rem.astype(BF16)
        parts.append(part)
        if k + 1 < passes:
            rem = rem - part.astype(F32)
    return parts[0] if passes == 1 else jnp.concatenate(parts, axis=1)


def _tri(blk, upper, sign):
    row = lax.broadcasted_iota(jnp.int32, (blk, blk), 0)
    col = lax.broadcasted_iota(jnp.int32, (blk, blk), 1)
    keep = (row <= col) if upper else (row >= col)
    t = jnp.where(keep, sign, 0.0).astype(BF16)
    return t if SCAN_PASSES == 1 else jnp.concatenate([t] * SCAN_PASSES, axis=0)


def _attn_fwd(proj, bq, bk, npairs):
    s = proj.shape[0]
    nq = s // bq
    ratio = bq // bk
    scale = HEAD_DIM ** -0.5
    heads = tuple(range(2 * npairs))
    width = LANES * npairs

    def body(q_ref, k_ref, v_ref, za_ref, o_ref, ya_ref, rs_ref, acc_ref, r_ref, z_ref):
        i = pl.program_id(1)
        lane = lax.broadcasted_iota(jnp.int32, (bq, LANES), 1)
        lo_half = lane < HEAD_DIM
        qm = []
        for pr in range(npairs):
            q = q_ref[:, LANES * pr:LANES * (pr + 1)] * jnp.asarray(scale, BF16)
            zero = jnp.zeros_like(q)
            qm += [jnp.where(lo_half, q, zero), jnp.where(lo_half, zero, q)]
        row = lax.broadcasted_iota(jnp.int32, (bq, bk), 0)
        col = lax.broadcasted_iota(jnp.int32, (bq, bk), 1)
        tneg = _tri(bk, False, -1.0)
        acc_ref[...] = jnp.zeros_like(acc_ref)
        r_ref[...] = jnp.zeros_like(r_ref)
        rs_ref[...] = jnp.full_like(rs_ref, SKIPPED)

        def scores(j):
            ks = pl.multiple_of(j * bk, bk)
            return [_dot_nt(qm[h], k_ref[pl.ds(ks, bk), LANES * (h // 2):LANES * (h // 2 + 1)]) for h in heads]

        for h, zh in enumerate(scores(i * ratio + ratio - 1)):
            z_ref[h] = zh

        def block(j, diag):
            ks = pl.multiple_of(j * bk, bk)
            vj = [v_ref[pl.ds(ks, bk), LANES * pr:LANES * (pr + 1)] for pr in range(npairs)]
            if diag:
                before = (j * bk + col) < (i * bq + row)
            z = [z_ref[h] for h in heads]
            sp = [_neg_softplus_parts(z[h])[1] for h in heads]
            if diag:
                sp = [jnp.where(before, sp[h], 0.0) for h in heads]
            cin = [jnp.dot(_split_cat(sp[h], SCAN_PASSES), tneg, preferred_element_type=F32) for h in heads]
            for h, zh in enumerate(scores(jnp.maximum(j - 1, 0))):
                z_ref[h] = zh
            w = [jnp.exp(z[h] + cin[h]) for h in heads]
            if diag:
                w = [jnp.where(before, w[h], 0.0) for h in heads]
            pv = [jnp.dot(w[h].astype(BF16), vj[h // 2], preferred_element_type=F32) for h in heads]
            r = [r_ref[h] for h in heads]
            for h in heads:
                acc_ref[h] += pv[h] * jnp.exp(r[h])
                r_ref[h] = r[h] + cin[h][:, 0:1]
            for pr in range(npairs):
                rs_ref[pr] = jnp.where(lane == j, r[2 * pr], jnp.where(lane == j + HEAD_DIM, r[2 * pr + 1], rs_ref[pr]))

        for t in range(ratio):
            block(i * ratio + ratio - 1 - t, True)

        def alive(carry):
            jj, r_max = carry
            return (jj < i * ratio) & (r_max > DEAD)

        def loop_body(carry):
            jj, _ = carry
            block(i * ratio - 1 - jj, False)
            return jj + 1, jnp.max(r_ref[...])

        lax.while_loop(alive, loop_body, (0, jnp.max(r_ref[...])))
        for pr in range(npairs):
            cols = slice(LANES * pr, LANES * (pr + 1))
            o = jnp.where(lo_half, acc_ref[2 * pr], acc_ref[2 * pr + 1])
            o_ref[:, cols] = o.astype(BF16)
            za = za_ref[:, cols].astype(F32)
            ya_ref[:, cols] = (o * (za * _sigmoid(za))).astype(BF16)

    n_steps = N_HEADS // (2 * npairs)
    return pl.pallas_call(
        body, name="attn_fwd", grid=(n_steps, nq),
        in_specs=[pl.BlockSpec((bq, width), lambda p, i: (i, n_steps * COL_Q + p)),
                  pl.BlockSpec((s, width), lambda p, i: (0, n_steps * COL_K + p)),
                  pl.BlockSpec((s, width), lambda p, i: (0, n_steps * COL_V + p)),
                  pl.BlockSpec((bq, width), lambda p, i: (i, n_steps * COL_ZA + p))],
        out_specs=[pl.BlockSpec((bq, width), lambda p, i: (i, p)),
                   pl.BlockSpec((bq, width), lambda p, i: (i, p)),
                   pl.BlockSpec((npairs, bq, LANES), lambda p, i: (p, i, 0))],
        out_shape=[jax.ShapeDtypeStruct((s, D_BRANCH), BF16), jax.ShapeDtypeStruct((s, D_BRANCH), BF16),
                   jax.ShapeDtypeStruct((N_HEADS // 2, s, LANES), F32)],
        scratch_shapes=[pltpu.VMEM((2 * npairs, bq, LANES), F32), pltpu.VMEM((2 * npairs, bq, 1), F32),
                        pltpu.VMEM((2 * npairs, bq, bk), F32)],
        compiler_params=_cparams(("parallel", "parallel")),
    )(proj, proj, proj, proj)


_GRAD_COL_SHARDED = (True, True, True, False)
_GRAD_FULL_SHAPES = ((D_MODEL, D_IN), (D_BRANCH, D_MODEL), (D_BRANCH, D_MODEL), (D_MODEL, D_MODEL))
_GRAD_PIECE_SHAPES = tuple((r // 2, w // N_CHIPS) if cs else (r // (2 * N_CHIPS), w)
                           for (r, w), cs in zip(_GRAD_FULL_SHAPES, _GRAD_COL_SHARDED))
_EARLY_IN_DEVS = (4, 5, 6, 7)
_LATE_IN_DEVS = (0, 1, 2, 3)
_LATE_CHIPS = (0, 1)
_EARLY_CHIPS = (2, 3)
_ALL_CHIPS = (0, 1, 2, 3)


def _grad_piece(ref, a, dev):
    r, w = _GRAD_PIECE_SHAPES[a]
    if _GRAD_COL_SHARDED[a]:
        return ref.at[pl.ds((dev % 2) * r, r), pl.ds((dev // 2) * w, w)]
    return ref.at[pl.ds(dev * r, r), :]


def _dev_id(dev):
    return (dev // 4, (dev // 2) % 2, dev % 2)


def _scatter_copy(plan, src, slots, send_sems, recv_sems, pos, dst_dev, src_dev):
    a = plan[pos][0]
    n = len(plan)
    return pltpu.make_async_remote_copy(
        src_ref=_grad_piece(src[pos], a, dst_dev), dst_ref=slots[pos].at[src_dev],
        send_sem=send_sems.at[n * dst_dev + pos], recv_sem=recv_sems.at[n * src_dev + pos],
        device_id=_dev_id(dst_dev), device_id_type=MESH)


def _scatter_start(plan, src, slots, send_sems, recv_sems, me):
    for pos, (_, dests) in enumerate(plan):
        for dev in dests:
            @pl.when(me != dev)
            def _():
                _scatter_copy(plan, src, slots, send_sems, recv_sems, pos, dev, me).start()


def _scatter_wait(plan, src, slots, send_sems, recv_sems, me):
    for pos, (_, dests) in enumerate(plan):
        for dev in range(N_DEV):
            @pl.when((me != dev) & (me >= dests[0]) & (me <= dests[-1]))
            def _():
                _scatter_copy(plan, src, slots, send_sems, recv_sems, pos, dev, dev).wait_recv()
        for dev in dests:
            @pl.when(me != dev)
            def _():
                _scatter_copy(plan, src, slots, send_sems, recv_sems, pos, dev, me).wait_send()


def _me():
    return 4 * lax.axis_index("x") + 2 * lax.axis_index("y") + lax.axis_index("c")


def _presum_copy(src, dst, send_sem, recv_sem, to_dev):
    return pltpu.make_async_remote_copy(src_ref=src, dst_ref=dst, send_sem=send_sem, recv_sem=recv_sem,
                                        device_id=_dev_id(to_dev), device_id_type=MESH)


def _presum_hand_off(dev, a, dest_chips, g_ref, slots, pair, send_sems, recv_sems):
    chip, core = dev // 2, dev % 2
    cps = []
    for k, q in enumerate(dest_chips):
        piece = _grad_piece(g_ref, a, 2 * q + 1 - core)
        if q == chip:
            cps.append(_presum_copy(piece, slots.at[dev], send_sems.at[N_DEV + k], recv_sems.at[dev], dev ^ 1))
        else:
            cps.append(_presum_copy(piece, pair.at[k], send_sems.at[N_DEV + k], recv_sems.at[N_DEV + k], dev ^ 1))
    return cps


def _presum_sends(dev, a, dest_chips, slots, sums, send_sems, recv_sems):
    chip, core = dev // 2, dev % 2
    return [_presum_copy(sums.at[k], slots.at[dev], send_sems.at[2 * q + core], recv_sems.at[dev], 2 * q + core)
            for k, q in enumerate(dest_chips) if q != chip]


def _presum_send(dev, a, dest_chips, g_ref, slots, pair, stage, sums, send_sems, recv_sems):
    chip, core = dev // 2, dev % 2
    hand = _presum_hand_off(dev, a, dest_chips, g_ref, slots, pair, send_sems, recv_sems)
    for k, q in enumerate(dest_chips):
        if q != chip:
            hand[k].wait_recv()
            pltpu.sync_copy(_grad_piece(g_ref, a, 2 * q + core), stage)
            sums[k] = (stage[...].astype(F32) + pair[k].astype(F32)).astype(BF16)
    for cp in _presum_sends(dev, a, dest_chips, slots, sums, send_sems, recv_sems):
        cp.start()


def _presum_wait(dev, a, dest_chips, g_ref, slots, pair, sums, send_sems, recv_sems):
    chip, core = dev // 2, dev % 2
    for cp in _presum_hand_off(dev, a, dest_chips, g_ref, slots, pair, send_sems, recv_sems):
        cp.wait_send()
    for cp in _presum_sends(dev, a, dest_chips, slots, sums, send_sems, recv_sems):
        cp.wait_send()
    if chip in dest_chips:
        for src_dev in _presum_sources(dev):
            _presum_copy(sums.at[0], slots.at[src_dev], send_sems.at[src_dev], recv_sems.at[src_dev], src_dev).wait_recv()


def _presum_sources(dev):
    return [dev ^ 1] + [2 * r + dev % 2 for r in range(N_CHIPS) if r != dev // 2]


def _presum_scratch(a, dest_chips):
    n = len(dest_chips)
    piece = _GRAD_PIECE_SHAPES[a]
    return [pltpu.VMEM((n,) + piece, BF16), pltpu.VMEM(piece, BF16), pltpu.VMEM((n,) + piece, BF16),
            pltpu.SemaphoreType.DMA((N_DEV + n,)), pltpu.SemaphoreType.DMA((N_DEV + n,))]


def _presum_program(first, second, last, a, dest_chips, g_ref, slots, scratch):
    pair, stage, sums, send_sems, recv_sems = scratch
    me = _me()

    @pl.when(first)
    def _():
        for dev in range(N_DEV):
            @pl.when(me == dev)
            def _():
                for cp in _presum_hand_off(dev, a, dest_chips, g_ref, slots, pair, send_sems, recv_sems):
                    cp.start()

    @pl.when(second)
    def _():
        for dev in range(N_DEV):
            @pl.when(me == dev)
            def _():
                _presum_send(dev, a, dest_chips, g_ref, slots, pair, stage, sums, send_sems, recv_sems)

    def finish():
        for dev in range(N_DEV):
            @pl.when(me == dev)
            def _():
                _presum_wait(dev, a, dest_chips, g_ref, slots, pair, sums, send_sems, recv_sems)

    return finish


_ALL_DEVS = tuple(range(N_DEV))


def _attn_bwd(proj, do, rsave, bq, bk, npairs, g_in):
    plan = ((0, _EARLY_IN_DEVS),)
    grads = (g_in,)
    n_scratch = 5
    s = proj.shape[0]
    nq = s // bq
    ratio = bq // bk
    scale = HEAD_DIM ** -0.5
    heads = tuple(range(2 * npairs))
    width = LANES * npairs

    n_steps = N_HEADS // (2 * npairs)
    n_g = len(grads)

    def body(q_ref, k_ref, v_ref, do_ref, rs_ref, *refs):
        g_src = refs[:n_g]
        dq_ref, dk_ref, dv_ref = refs[n_g:n_g + 3]
        g_slots = refs[n_g + 3:2 * n_g + 3]
        dk_acc, dv_acc, dq_acc, e_ref, qmt_ref, domt_ref, rst_ref = refs[2 * n_g + 3:2 * n_g + 10]
        i = pl.program_id(1)
        step = pl.program_id(0) * nq + i
        finish = _presum_program(step == 0, step == 1, step == n_steps * nq - 1, 0, _EARLY_CHIPS,
                                 g_src[0], g_slots[0], refs[2 * n_g + 10:])

        lane = lax.broadcasted_iota(jnp.int32, (bq, LANES), 1)
        lo_half = lane < HEAD_DIM
        qm, dom = [], []
        for pr in range(npairs):
            cols = slice(LANES * pr, LANES * (pr + 1))
            q = q_ref[:, cols] * jnp.asarray(scale, BF16)
            zero = jnp.zeros_like(q)
            qm += [jnp.where(lo_half, q, zero), jnp.where(lo_half, zero, q)]
            dout = do_ref[:, cols].astype(F32)
            dom += [jnp.where(lo_half, dout, 0.0), jnp.where(lo_half, 0.0, dout)]
        row = lax.broadcasted_iota(jnp.int32, (bq, bk), 0)
        col = lax.broadcasted_iota(jnp.int32, (bq, bk), 1)
        tneg = _tri(bk, False, -1.0)
        tfwd = _tri(bk, True, 1.0)

        @pl.when(i == 0)
        def _():
            dk_acc[...] = jnp.zeros_like(dk_acc)
            dv_acc[...] = jnp.zeros_like(dv_acc)

        dq_acc[...] = jnp.zeros_like(dq_acc)
        e_ref[...] = jnp.zeros_like(e_ref)
        for h in heads:
            qmt_ref[h] = qm[h].astype(F32).T.astype(BF16)
            domt_ref[h] = dom[h].T
        for pr in range(npairs):
            rst_ref[pr] = rs_ref[pr].T

        def block(j, diag):
            ks = pl.multiple_of(j * bk, bk)
            kj = [k_ref[pl.ds(ks, bk), LANES * pr:LANES * (pr + 1)] for pr in range(npairs)]
            vj = [v_ref[pl.ds(ks, bk), LANES * pr:LANES * (pr + 1)] for pr in range(npairs)]
            if diag:
                before = (j * bk + col) < (i * bq + row)
            z = [_dot_nt(qm[h], kj[h // 2]) for h in heads]
            dost = [(domt_ref[h] * jnp.exp(rst_ref[h // 2, pl.ds(j + HEAD_DIM * (h % 2), 1), :])).astype(BF16)
                    for h in heads]
            er = [jnp.exp(jnp.sum(jnp.where(lane == j + HEAD_DIM * (h % 2), rs_ref[h // 2], 0.0), axis=-1,
                                  keepdims=True)) for h in heads]
            dos = [(dom[h] * er[h]).astype(BF16) for h in heads]
            dw = [_dot_nt(dos[h], vj[h // 2]) for h in heads]
            psp = [_neg_softplus_parts(z[h]) for h in heads]
            sp = [psp[h][1] for h in heads]
            if diag:
                sp = [jnp.where(before, sp[h], 0.0) for h in heads]
            cin = [jnp.dot(_split_cat(sp[h], SCAN_PASSES), tneg, preferred_element_type=F32) for h in heads]
            w = [jnp.exp(z[h] + cin[h]) for h in heads]
            if diag:
                w = [jnp.where(before, w[h], 0.0) for h in heads]
            e =[dw[h] * w[h] for h in heads]
            eincl = [jnp.dot(_split_cat(e[h], SCAN_PASSES), tfwd, preferred_element_type=F32) + e_ref[h]
                     for h in heads]
            dz = []
            for h in heads:
                p = psp[h][0]
                beta = jnp.where(z[h] >= 0.0, 1.0, p) / (1.0 + p)
                d = e[h] - beta * eincl[h]
                dz.append((jnp.where(before, d, 0.0) if diag else d).astype(BF16))
            wb = [w[h].astype(BF16) for h in heads]
            for h in heads:
                e_ref[h] = eincl[h][:, bk - 1:bk]
                dq_acc[h] += jnp.dot(dz[h], kj[h // 2], preferred_element_type=F32)
            for pr in range(npairs):
                rows = slice(LANES * pr, LANES * (pr + 1))
                h0, h1 = 2 * pr, 2 * pr + 1
                dk_acc[rows, pl.ds(ks, bk)] += (jnp.dot(qmt_ref[h0], dz[h0], preferred_element_type=F32)
                                                 + jnp.dot(qmt_ref[h1], dz[h1], preferred_element_type=F32))
                dv_acc[rows, pl.ds(ks, bk)] += (jnp.dot(dost[h0], wb[h0], preferred_element_type=F32)
                                                 + jnp.dot(dost[h1], wb[h1], preferred_element_type=F32))

        def loop_body(j, carry):
            block(j, False)
            return carry

        block_of_lane = lane & (HEAD_DIM - 1)
        live = jnp.max(rs_ref[...], axis=0) > DEAD
        first_live = jnp.min(jnp.where(live, block_of_lane, nq * ratio))
        lax.fori_loop(jnp.minimum(first_live, i * ratio), i * ratio, loop_body, 0)
        for t in range(ratio):
            block(i * ratio + t, True)
        for pr in range(npairs):
            dq = jnp.where(lo_half, dq_acc[2 * pr], dq_acc[2 * pr + 1]) * scale
            dq_ref[:, LANES * pr:LANES * (pr + 1)] = dq.astype(BF16)

        @pl.when(i == nq - 1)
        def _():
            dk_ref[...] = dk_acc[...].T.astype(BF16)
            dv_ref[...] = dv_acc[...].T.astype(BF16)

        @pl.when(step == n_steps * nq - 1)
        def _():
            finish()

    any_spec = pl.BlockSpec(memory_space=pl.ANY)
    return pl.pallas_call(
        body, name="attn_bwd", grid=(n_steps, nq),
        in_specs=[pl.BlockSpec((bq, width), lambda p, i: (i, n_steps * COL_Q + p)),
                  pl.BlockSpec((s, width), lambda p, i: (0, n_steps * COL_K + p)),
                  pl.BlockSpec((s, width), lambda p, i: (0, n_steps * COL_V + p)),
                  pl.BlockSpec((bq, width), lambda p, i: (i, p)),
                  pl.BlockSpec((npairs, bq, LANES), lambda p, i: (p, i, 0))] + [any_spec] * n_g,
        out_specs=[pl.BlockSpec((bq, width), lambda p, i: (i, p)),
                   pl.BlockSpec((s, width), lambda p, i: (0, p)),
                   pl.BlockSpec((s, width), lambda p, i: (0, p))] + [any_spec] * n_g,
        out_shape=[jax.ShapeDtypeStruct((s, D_BRANCH), BF16)] * 3
        + [jax.ShapeDtypeStruct((N_DEV,) + _GRAD_PIECE_SHAPES[a], BF16) for a, _ in plan],
        scratch_shapes=[pltpu.VMEM((width, s), F32), pltpu.VMEM((width, s), F32),
                        pltpu.VMEM((2 * npairs, bq, LANES), F32), pltpu.VMEM((2 * npairs, bq, 1), F32),
                        pltpu.VMEM((2 * npairs, LANES, bq), BF16), pltpu.VMEM((2 * npairs, LANES, bq), F32),
                        pltpu.VMEM((npairs, LANES, bq), F32)] + _presum_scratch(0, _EARLY_CHIPS),
        compiler_params=pltpu.CompilerParams(dimension_semantics=("arbitrary", "arbitrary"),
                                             vmem_limit_bytes=VMEM_LIMIT, has_side_effects=True),
    )(proj, proj, proj, do, rsave, *grads)


def _group_avg_matrix():
    a = lax.broadcasted_iota(jnp.int32, (LANES, LANES), 0) >> GROUP_SHIFT
    b = lax.broadcasted_iota(jnp.int32, (LANES, LANES), 1) >> GROUP_SHIFT
    return jnp.where(a == b, 1.0 / GROUP_DIM, 0.0).astype(BF16)


def _group_mean(a, avg):
    parts = [_split_dot(a[:, LANES * k:LANES * (k + 1)], avg, 3) for k in range(D_BRANCH // LANES)]
    return jnp.concatenate(parts, axis=1)


def _sgu_forward_parts(ub, vb, ln_g, ln_b, avg):
    ug, dug = _gelu_and_grad(ub)
    vg, dvg = _gelu_and_grad(vb)
    mu = _group_mean(vg, avg)
    d = vg - mu
    var = _group_mean(d * d, avg)
    rstd = lax.rsqrt(var + EPS)
    vhat = d * rstd
    vn = vhat * ln_g + ln_b
    return ug, dug, dvg, rstd, vhat, vn


def _sgu_mix(w_ref, src_bf16, n_chunks):
    lane = lax.broadcasted_iota(jnp.int32, (SGU_CHUNK, LANES), 1)
    lo_half = lane < GROUP_DIM
    rows = []
    for n in range(n_chunks):
        slabs = []
        for a in range(D_BRANCH // LANES):
            blk = src_bf16[SGU_CHUNK * n:SGU_CHUNK * (n + 1), LANES * a:LANES * (a + 1)]
            zero = jnp.zeros_like(blk)
            m0 = jnp.dot(w_ref[2 * a], jnp.where(lo_half, blk, zero), preferred_element_type=F32)
            m1 = jnp.dot(w_ref[2 * a + 1], jnp.where(lo_half, zero, blk), preferred_element_type=F32)
            slabs.append(m0 + m1)
        rows.append(jnp.concatenate(slabs, axis=1))
    return jnp.concatenate(rows, axis=0)


def _sgu_fwd(proj, ln_g, ln_b, w_mask, bias_full):
    s = proj.shape[0]
    tm = min(512, s)
    n_chunks = tm // SGU_CHUNK

    def body(ub_ref, vb_ref, zb_ref, g_ref, b_ref, w_ref, bias_ref, yb_ref):
        avg = _group_avg_matrix()
        ug, _, _, _, _, vn = _sgu_forward_parts(ub_ref[...].astype(F32), vb_ref[...].astype(F32),
                                                g_ref[...], b_ref[...], avg)
        mixed = _sgu_mix(w_ref, vn.astype(BF16), n_chunks) + jnp.concatenate([bias_ref[...]] * n_chunks, axis=0)
        zb = zb_ref[...].astype(F32)
        yb_ref[...] = (ug * mixed * (zb * _sigmoid(zb))).astype(BF16)

    col = lambda c: pl.BlockSpec((tm, D_BRANCH), lambda i: (i, c))
    full = lambda shape: pl.BlockSpec(shape, lambda i: (0,) * len(shape))
    return pl.pallas_call(
        body, name="sgu_fwd", grid=(s // tm,),
        in_specs=[col(COL_UB), col(COL_VB), col(COL_ZB), full((1, D_BRANCH)), full((1, D_BRANCH)),
                  full((N_GROUPS, SGU_CHUNK, SGU_CHUNK)), full((SGU_CHUNK, D_BRANCH))],
        out_specs=pl.BlockSpec((tm, D_BRANCH), lambda i: (i, 0)),
        out_shape=jax.ShapeDtypeStruct((s, D_BRANCH), BF16),
        compiler_params=_cparams(("parallel",)),
    )(proj, proj, proj, ln_g, ln_b, w_mask, bias_full)


def _sgu_bwd(proj, dyb, ln_g, ln_b, w_mask, w_mask_t, bias_full, grads):
    s = proj.shape[0]
    tm = min(512, s)
    n_chunks = tm // SGU_CHUNK
    n_steps = s // tm
    plan = ((1, _ALL_DEVS), (2, _ALL_DEVS), (3, _ALL_DEVS))
    n_g = len(plan)

    def body(ub_ref, vb_ref, zb_ref, dyb_ref, g_ref, b_ref, w_ref, wt_ref, bias_ref, *refs):
        g_src = refs[:n_g]
        dsgu_ref, dw_ref, db_ref, dg_ref, dbeta_ref = refs[n_g:n_g + 5]
        g_slots = refs[n_g + 5:2 * n_g + 5]
        dmix_acc = refs[2 * n_g + 5]
        i = pl.program_id(0)
        finish = [_presum_program(i == 0, i == min(1, n_steps - 1), i == n_steps - 1, a, _ALL_CHIPS, g_src[pos],
                                  g_slots[pos], refs[2 * n_g + 6 + 5 * pos:2 * n_g + 11 + 5 * pos])
                  for pos, (a, _) in enumerate(plan)]

        @pl.when(i == 0)
        def _():
            dw_ref[...] = jnp.zeros_like(dw_ref)
            dg_ref[...] = jnp.zeros_like(dg_ref)
            dbeta_ref[...] = jnp.zeros_like(dbeta_ref)
            dmix_acc[...] = jnp.zeros_like(dmix_acc)

        avg = _group_avg_matrix()
        ln_gv = g_ref[...]
        ug, dug, dvg, rstd, vhat, vn = _sgu_forward_parts(ub_ref[...].astype(F32), vb_ref[...].astype(F32),
                                                          ln_gv, b_ref[...], avg)
        vnb = vn.astype(BF16)
        mixed = _sgu_mix(w_ref, vnb, n_chunks) + jnp.concatenate([bias_ref[...]] * n_chunks, axis=0)
        zb = zb_ref[...].astype(F32)
        sg = _sigmoid(zb)
        sz = zb * sg
        dsz = sg * (1.0 + zb * (1.0 - sg))
        dy = dyb_ref[...].astype(F32)
        dmixed = dy * ug * sz
        du = dy * mixed * sz * dug
        dzb = dy * ug * mixed * dsz
        dmb = dmixed.astype(BF16)
        dvn = _sgu_mix(wt_ref, dmb, n_chunks)

        lane = lax.broadcasted_iota(jnp.int32, (SGU_CHUNK, LANES), 1)
        lo_half = lane < GROUP_DIM
        dm_sum = None
        for n in range(n_chunks):
            rows = slice(SGU_CHUNK * n, SGU_CHUNK * (n + 1))
            dm_sum = dmixed[rows] if dm_sum is None else dm_sum + dmixed[rows]
            for a in range(D_BRANCH // LANES):
                cols = slice(LANES * a, LANES * (a + 1))
                dblk = dmb[rows, cols]
                vblk = vnb[rows, cols]
                zero = jnp.zeros_like(dblk)
                dw_ref[2 * a] += _dot_nt(jnp.where(lo_half, dblk, zero), vblk)
                dw_ref[2 * a + 1] += _dot_nt(jnp.where(lo_half, zero, dblk), vblk)
        dmix_acc[...] += dm_sum

        dg_ref[...] += jnp.sum(dvn * vhat, axis=0, keepdims=True)
        dbeta_ref[...] += jnp.sum(dvn, axis=0, keepdims=True)
        dvh = dvn * ln_gv
        m1 = _group_mean(dvh, avg)
        m2 = _group_mean(dvh * vhat, avg)
        dv = rstd * (dvh - m1 - vhat * m2) * dvg
        dsgu_ref[:, 0:D_BRANCH] = du.astype(BF16)
        dsgu_ref[:, D_BRANCH:2 * D_BRANCH] = dv.astype(BF16)
        dsgu_ref[:, 2 * D_BRANCH:3 * D_BRANCH] = dzb.astype(BF16)

        @pl.when(i == n_steps - 1)
        def _():
            pos = lax.broadcasted_iota(jnp.int32, (SGU_CHUNK, SGU_CHUNK), 0) >> GROUP_SHIFT
            src = lax.broadcasted_iota(jnp.int32, (SGU_CHUNK, SGU_CHUNK), 1) >> GROUP_SHIFT
            keep = src <= pos
            for g in range(N_GROUPS):
                dw_ref[g] = jnp.where(keep, dw_ref[g], 0.0)
            grp = lax.broadcasted_iota(jnp.int32, (D_BRANCH, LANES), 0) >> GROUP_SHIFT
            sel = (grp == lax.broadcasted_iota(jnp.int32, (D_BRANCH, LANES), 1)).astype(BF16)
            db_ref[...] = _split_dot(dmix_acc[...], sel, 3)
            for fin in finish:
                fin()

    col = lambda c: pl.BlockSpec((tm, D_BRANCH), lambda i: (i, c))
    full = lambda shape: pl.BlockSpec(shape, lambda i: (0,) * len(shape))
    any_spec = pl.BlockSpec(memory_space=pl.ANY)
    return pl.pallas_call(
        body, name="sgu_bwd", grid=(n_steps,),
        in_specs=[col(COL_UB), col(COL_VB), col(COL_ZB), pl.BlockSpec((tm, D_BRANCH), lambda i: (i, 0)),
                  full((1, D_BRANCH)), full((1, D_BRANCH)),
                  full((N_GROUPS, SGU_CHUNK, SGU_CHUNK)), full((N_GROUPS, SGU_CHUNK, SGU_CHUNK)),
                  full((SGU_CHUNK, D_BRANCH))] + [any_spec] * n_g,
        out_specs=[pl.BlockSpec((tm, 3 * D_BRANCH), lambda i: (i, 0)),
                   full((N_GROUPS, SGU_CHUNK, SGU_CHUNK)), full((SGU_CHUNK, LANES)),
                   full((1, D_BRANCH)), full((1, D_BRANCH))] + [any_spec] * n_g,
        out_shape=[jax.ShapeDtypeStruct((s, 3 * D_BRANCH), BF16),
                   jax.ShapeDtypeStruct((N_GROUPS, SGU_CHUNK, SGU_CHUNK), F32),
                   jax.ShapeDtypeStruct((SGU_CHUNK, LANES), F32),
                   jax.ShapeDtypeStruct((1, D_BRANCH), F32), jax.ShapeDtypeStruct((1, D_BRANCH), F32)]
        + [jax.ShapeDtypeStruct((N_DEV,) + _GRAD_PIECE_SHAPES[a], BF16) for a, _ in plan],
        scratch_shapes=[pltpu.VMEM((SGU_CHUNK, D_BRANCH), F32)]
        + [sh for a, _ in plan for sh in _presum_scratch(a, _ALL_CHIPS)],
        compiler_params=pltpu.CompilerParams(dimension_semantics=("arbitrary",), vmem_limit_bytes=VMEM_LIMIT,
                                             has_side_effects=True),
    )(proj, proj, proj, dyb, ln_g, ln_b, w_mask, w_mask_t, bias_full, *grads)


def _mid(proj, ya, yb, o, x, target, final_g, w_up_a, w_up_b, w_out):
    s = x.shape[0]
    tm = min(256, s)
    n_steps = s // tm
    half = D_MODEL // 2

    def body(ya_ref, yb_ref, o_ref, za_ref, ga0_ref, ga1_ref, gb0_ref, gb1_ref, x_ref, t_ref, gf_ref,
             wa_ref, wb_ref, wo_ref,
             dzg_ref, do_ref, dyb_ref, dx2_ref, gwo_ref, gwa_ref, gwb_ref, loss_ref, dgf_ref,
             acc_o, acc_a, acc_b):
        i = pl.program_id(0)

        @pl.when(i == 0)
        def _():
            acc_o[...] = jnp.zeros_like(acc_o)
            acc_a[...] = jnp.zeros_like(acc_a)
            acc_b[...] = jnp.zeros_like(acc_b)
            loss_ref[...] = jnp.zeros_like(loss_ref)
            dgf_ref[...] = jnp.zeros_like(dgf_ref)

        ya_v = ya_ref[...]
        yb_v = yb_ref[...]
        pa = jnp.dot(ya_v, wa_ref[...], preferred_element_type=F32)
        pb = jnp.dot(yb_v, wb_ref[...], preferred_element_type=F32)
        sa = _sigmoid(jnp.concatenate([ga0_ref[...], ga1_ref[...]], axis=1).astype(F32))
        sb = _sigmoid(jnp.concatenate([gb0_ref[...], gb1_ref[...]], axis=1).astype(F32))
        merged = (sa * pa + sb * pb).astype(BF16)
        x2 = x_ref[...] + jnp.dot(merged, wo_ref[...], preferred_element_type=F32)
        r2 = lax.rsqrt(jnp.mean(x2 * x2, axis=-1, keepdims=True) + EPS)
        xh = x2 * r2
        gf = gf_ref[...]
        diff = xh * gf - t_ref[...]
        loss_ref[...] += 0.5 * jnp.sum(jnp.mean(diff * diff, axis=-1, keepdims=True))
        dy = diff * (1.0 / D_MODEL)
        dgf_ref[...] += jnp.sum(dy * xh, axis=0, keepdims=True)
        dyg = dy * gf
        dx2 = r2 * (dyg - xh * jnp.mean(dyg * xh, axis=-1, keepdims=True))
        dx2_ref[...] = dx2
        dx2b = dx2.astype(BF16)
        dmerged = _dot_nt(dx2b, wo_ref[...])
        acc_o[...] += _dot_tn(merged, dx2b)
        dpa = dmerged * sa
        dpb = dmerged * sb
        dzg_ref[:, D_BRANCH:D_BRANCH + D_MODEL] = (dpa * pa * (1.0 - sa)).astype(BF16)
        dzg_ref[:, D_BRANCH + D_MODEL:D_BRANCH + 2 * D_MODEL] = (dpb * pb * (1.0 - sb)).astype(BF16)
        dpab = dpa.astype(BF16)
        dpbb = dpb.astype(BF16)
        acc_a[...] += _dot_tn(ya_v, dpab)
        acc_b[...] += _dot_tn(yb_v, dpbb)
        dya = _dot_nt(dpab, wa_ref[...])
        dyb_ref[...] = _dot_nt(dpbb, wb_ref[...]).astype(BF16)
        za = za_ref[...].astype(F32)
        sg = _sigmoid(za)
        do_ref[...] = (dya * (za * sg)).astype(BF16)
        dzg_ref[:, 0:D_BRANCH] = (dya * o_ref[...].astype(F32) * (sg * (1.0 + za * (1.0 - sg)))).astype(BF16)

        @pl.when(i == n_steps - 1)
        def _():
            gwo_ref[...] = acc_o[...].astype(BF16)
            gwa_ref[...] = acc_a[...].astype(BF16)
            gwb_ref[...] = acc_b[...].astype(BF16)

    tok = lambda w: pl.BlockSpec((tm, w), lambda i: (i, 0))
    col = lambda c: pl.BlockSpec((tm, half), lambda i: (i, c))
    full = lambda shape: pl.BlockSpec(shape, lambda i: (0,) * len(shape))
    return pl.pallas_call(
        body, name="mid", grid=(n_steps,),
        in_specs=[tok(D_BRANCH), tok(D_BRANCH), tok(D_BRANCH), col(COL_ZA), col(COL_GA), col(COL_GA + 1),
                  col(COL_GB), col(COL_GB + 1), tok(D_MODEL), tok(D_MODEL), full((1, D_MODEL)),
                  full((D_BRANCH, D_MODEL)), full((D_BRANCH, D_MODEL)), full((D_MODEL, D_MODEL))],
        out_specs=[tok(D_BRANCH + 2 * D_MODEL), tok(D_BRANCH), tok(D_BRANCH), tok(D_MODEL),
                   full((D_MODEL, D_MODEL)), full((D_BRANCH, D_MODEL)), full((D_BRANCH, D_MODEL)),
                   full((8, LANES)), full((1, D_MODEL))],
        out_shape=[jax.ShapeDtypeStruct((s, D_BRANCH + 2 * D_MODEL), BF16),
                   jax.ShapeDtypeStruct((s, D_BRANCH), BF16), jax.ShapeDtypeStruct((s, D_BRANCH), BF16),
                   jax.ShapeDtypeStruct((s, D_MODEL), F32),
                   jax.ShapeDtypeStruct((D_MODEL, D_MODEL), BF16),
                   jax.ShapeDtypeStruct((D_BRANCH, D_MODEL), BF16), jax.ShapeDtypeStruct((D_BRANCH, D_MODEL), BF16),
                   jax.ShapeDtypeStruct((8, LANES), F32), jax.ShapeDtypeStruct((1, D_MODEL), F32)],
        scratch_shapes=[pltpu.VMEM((D_MODEL, D_MODEL), F32), pltpu.VMEM((D_BRANCH, D_MODEL), F32),
                        pltpu.VMEM((D_BRANCH, D_MODEL), F32)],
        compiler_params=_cparams(("arbitrary",)),
    )(ya, yb, o, proj, proj, proj, proj, proj, x, target, final_g, w_up_a, w_up_b, w_out)


def _dwin_piece(ht, piece, tile_of, prev):
    s = ht.shape[1]
    n_tiles = piece.shape[1] // D_BRANCH

    def body(ht_ref, p_ref, *rest):
        out_ref = rest[-1]
        out_ref[...] = jnp.dot(ht_ref[...], p_ref[...], preferred_element_type=F32).astype(BF16)

    in_specs = [pl.BlockSpec((D_MODEL, s), lambda j: (0, 0)), pl.BlockSpec((s, D_BRANCH), lambda j: (0, j))]
    args = [ht, piece]
    aliases = {}
    if prev is not None:
        in_specs.append(pl.BlockSpec(memory_space=pl.ANY))
        args.append(prev)
        aliases = {2: 0}
    return pl.pallas_call(
        body, name="dwin_piece", grid=(n_tiles,),
        in_specs=in_specs,
        out_specs=pl.BlockSpec((D_MODEL, D_BRANCH), lambda j: (0, tile_of(j))),
        out_shape=jax.ShapeDtypeStruct((D_MODEL, D_IN), BF16),
        input_output_aliases=aliases,
        compiler_params=_cparams(("parallel",)),
    )(*args)


def _dh_dx(pieces, w_in, x, norm_g, dx2, g_in):
    s = x.shape[0]
    tm = min(256, s)
    n_steps = s // tm
    arrays = []
    for arr, _, _, _ in pieces:
        if not any(arr is a for a in arrays):
            arrays.append(arr)
    n_arr = len(arrays)
    plan = [([k for k, a in enumerate(arrays) if a is arr][0], wcol, off, width) for arr, wcol, off, width in pieces]

    def body(*refs):
        p_refs = refs[:n_arr]
        w_ref, x_ref, g_ref, dx2_ref, gin_ref, dx_ref, dg_ref, late_ref = refs[n_arr:n_arr + 8]
        step = pl.program_id(0)
        finish = _presum_program(step == 0, step == min(1, n_steps - 1), step == n_steps - 1, 0, _LATE_CHIPS,
                                 gin_ref, late_ref, refs[n_arr + 8:])

        @pl.when(step == 0)
        def _():
            dg_ref[...] = jnp.zeros_like(dg_ref)

        dh = None
        for k, wcol, off, width in plan:
            d = _dot_nt(p_refs[k][:, off:off + width], w_ref[:, wcol:wcol + width])
            dh = d if dh is None else dh + d
        xf = x_ref[...]
        r = lax.rsqrt(jnp.mean(xf * xf, axis=-1, keepdims=True) + EPS)
        xh = xf * r
        dg_ref[...] += jnp.sum(dh * xh, axis=0, keepdims=True)
        dhg = dh * g_ref[...]
        dx_ref[...] = r * (dhg - xh * jnp.mean(dhg * xh, axis=-1, keepdims=True)) + dx2_ref[...]

        @pl.when(step == n_steps - 1)
        def _():
            finish()

    tok = lambda w: pl.BlockSpec((tm, w), lambda i: (i, 0))
    full = lambda shape: pl.BlockSpec(shape, lambda i: (0,) * len(shape))
    any_spec = pl.BlockSpec(memory_space=pl.ANY)
    return pl.pallas_call(
        body, name="dh_dx", grid=(n_steps,),
        in_specs=[tok(a.shape[1]) for a in arrays] + [full((D_MODEL, D_IN)), tok(D_MODEL), full((1, D_MODEL)),
                                                      tok(D_MODEL), any_spec],
        out_specs=[tok(D_MODEL), full((1, D_MODEL)), any_spec],
        out_shape=[jax.ShapeDtypeStruct((s, D_MODEL), F32), jax.ShapeDtypeStruct((1, D_MODEL), F32),
                   jax.ShapeDtypeStruct((N_DEV,) + _GRAD_PIECE_SHAPES[0], BF16)],
        scratch_shapes=_presum_scratch(0, _LATE_CHIPS),
        compiler_params=pltpu.CompilerParams(dimension_semantics=("arbitrary",), vmem_limit_bytes=VMEM_LIMIT,
                                             has_side_effects=True),
    )(*arrays, w_in, x, norm_g, dx2, g_in)


def _adamw(w, g, m, v):
    rows, cols = w.shape
    tr = max(t for t in range(8, 257, 8) if rows % t == 0)
    c1 =1.0 - ADAM_B1 ** ADAM_STEP
    c2 = 1.0 - ADAM_B2 ** ADAM_STEP

    def body(w_ref, g_ref, m_ref, v_ref, g_out_ref, d_ref, nm_ref, nv_ref):
        gv = g_ref[...]
        g_out_ref[...] = gv
        nm = ADAM_B1 * m_ref[...] + (1.0 - ADAM_B1) * gv
        nv = ADAM_B2 * v_ref[...] + (1.0 - ADAM_B2) * (gv * gv)
        d_ref[...] = -ADAM_LR * ((nm / c1) / (jnp.sqrt(nv / c2) + ADAM_EPS) + ADAM_WD * w_ref[...])
        nm_ref[...] = nm
        nv_ref[...] = nv

    spec = pl.BlockSpec((tr, cols), lambda i: (i, 0))
    return pl.pallas_call(
        body, name="adamw", grid=(rows // tr,),
        in_specs=[spec] * 4, out_specs=[spec] * 4,
        out_shape=[jax.ShapeDtypeStruct((rows, cols), F32)] * 4,
        compiler_params=_cparams(("parallel",)),
    )(w, g, m, v)


def _place():
    x, y, c = lax.axis_index("x"), lax.axis_index("y"), lax.axis_index("c")
    return x, y, c


def _gather_weights(w_in, w_up_a, w_up_b, w_out):
    shards = (w_in, w_up_a, w_up_b, w_out)
    n_arr = len(shards)
    col_sharded = (True, True, True, False)
    full_shapes = ((D_MODEL, D_IN), (D_BRANCH, D_MODEL), (D_BRANCH, D_MODEL), (D_MODEL, D_MODEL))

    def body(*refs):
        src = refs[:n_arr]
        out = refs[n_arr:2 * n_arr]
        stage = refs[2 * n_arr:3 * n_arr]
        cast = refs[3 * n_arr:4 * n_arr]
        send_sems, recv_sems, local_sems = refs[4 * n_arr:]
        x, y, c = _place()
        chip = 2 * x + y
        sibling = (x, y, 1 - c)
        others = [(1 - x, y), (x, 1 - y), (1 - x, 1 - y)]

        def region(a, chip_idx, half):
            r, w = shards[a].shape
            hr = r // 2
            if col_sharded[a]:
                return out[a].at[pl.ds(_aligned(half * hr, 16), hr), pl.ds(_aligned(chip_idx * w, LANES), w)]
            return out[a].at[pl.ds(_aligned(chip_idx * r + half * hr, 16), hr), :]

        loads = [pltpu.make_async_copy(src[a], stage[a], local_sems.at[a]) for a in range(n_arr)]
        for cp in loads:
            cp.start()
        for a in range(n_arr):
            loads[a].wait()
            cast[a][...] = stage[a][...].astype(BF16)
        stores = []
        for a in range(n_arr):
            hr = shards[a].shape[0] // 2
            for half in range(2):
                cp = pltpu.make_async_copy(cast[a].at[pl.ds(half * hr, hr), :], region(a, chip, half),
                                           local_sems.at[n_arr + 2 * a + half])
                cp.start()
                stores.append(cp)

        def remote(k, a, chip_idx, half, to, from_vmem):
            hr = shards[a].shape[0] // 2
            s_ref = cast[a].at[pl.ds(_aligned(half * hr, 16), hr), :] if from_vmem else region(a, chip_idx, half)
            return pltpu.make_async_remote_copy(src_ref=s_ref, dst_ref=region(a, chip_idx, half),
                                                send_sem=send_sems.at[k], recv_sem=recv_sems.at[k],
                                                device_id=to, device_id_type=MESH)

        first = []
        for j, (ox, oy) in enumerate(others):
            for a in range(n_arr):
                cp = remote(n_arr * j + a, a, chip, c, (ox, oy, c), True)
                cp.start()
                first.append(cp)
        passed = []
        for j, (ox, oy) in enumerate(others):
            ochip = 2 * ox + oy
            for a in range(n_arr):
                k = n_arr * j + a
                remote(k, a, ochip, c, sibling, False).wait_recv()
                cp = remote(3 * n_arr + k, a, ochip, c, sibling, False)
                cp.start()
                passed.append(cp)
        for j, (ox, oy) in enumerate(others):
            ochip = 2 * ox + oy
            for a in range(n_arr):
                remote(3 * n_arr + n_arr * j + a, a, ochip, 1 - c, sibling, False).wait_recv()
        for cp in first + passed:
            cp.wait_send()
        for cp in stores:
            cp.wait()

    any_spec = pl.BlockSpec(memory_space=pl.ANY)
    return pl.pallas_call(
        body, name="gather_weights",
        in_specs=[any_spec] * n_arr, out_specs=[any_spec] * n_arr,
        out_shape=[jax.ShapeDtypeStruct(sh, BF16) for sh in full_shapes],
        scratch_shapes=[pltpu.VMEM(a.shape, F32) for a in shards] + [pltpu.VMEM(a.shape, BF16) for a in shards]
        + [pltpu.SemaphoreType.DMA((6 * n_arr,)), pltpu.SemaphoreType.DMA((6 * n_arr,)),
           pltpu.SemaphoreType.DMA((3 * n_arr,))],
        compiler_params=pltpu.CompilerParams(vmem_limit_bytes=VMEM_LIMIT, has_side_effects=True),
    )(*shards)


def _reduce_grads_tail(grads, g_small, early_slots, late_in_slots):
    n_big = len(grads)
    n_arr = n_big + 1
    shard_shapes = [(2 * r, w) for r, w in _GRAD_PIECE_SHAPES]
    small_piece = (SMALL_PIECE, LANES)

    def body(*refs):
        src = refs[:n_arr]
        early = refs[n_arr:n_arr + n_big]
        late_in = refs[n_arr + n_big]
        n_in = n_arr + n_big + 1
        out = refs[n_in:n_in + n_arr]
        slots = refs[n_in + n_arr:n_in + 2 * n_arr]
        sums = refs[n_in + 2 * n_arr:n_in + 3 * n_arr]
        send1, recv1, send2, recv2, local_sems = refs[n_in + 3 * n_arr:]
        x, y, c = _place()
        me = 4 * x + 2 * y + c

        def piece_of(a, dev):
            return src[a].at[dev] if a == n_big else _grad_piece(src[a], a, dev)

        def late(a, dst_dev, src_dev):
            return pltpu.make_async_remote_copy(
                src_ref=piece_of(a, dst_dev), dst_ref=slots[a].at[src_dev],
                send_sem=send1.at[n_arr * dst_dev + a], recv_sem=recv1.at[n_arr * src_dev + a],
                device_id=_dev_id(dst_dev), device_id_type=MESH)

        def late_arrays(dev):
            return (n_big,)

        def load(a, dev, received):
            return pltpu.make_async_copy(received.at[dev], slots[a].at[dev], local_sems.at[n_arr * dev + a])

        def own(a, dev):
            return pltpu.make_async_copy(piece_of(a, dev), slots[a].at[dev], local_sems.at[n_arr * dev + a])

        for dev in range(N_DEV):
            @pl.when(me == dev)
            def _():
                received = [early[0] if dev in _EARLY_IN_DEVS else late_in] + list(early[1:])
                for a in range(n_arr):
                    own(a, dev).start()
                filled = lambda a, peer: peer in _presum_sources(dev)
                for peer in range(N_DEV):
                    if peer != dev:
                        for a in late_arrays(peer):
                            late(a, peer, dev).start()
                        for a in range(n_big):
                            if filled(a, peer):
                                load(a, peer, received[a]).start()
                            else:
                                slots[a][peer] = jnp.zeros(slots[a].shape[1:], slots[a].dtype)
                for a in range(n_arr):
                    own(a, dev).wait()
                for peer in range(N_DEV):
                    if peer != dev:
                        for a in late_arrays(dev):
                            late(a, dev, peer).wait_recv()
                        for a in range(n_big):
                            if filled(a, peer):
                                load(a, peer, received[a]).wait()

        for a in range(n_arr):
            rows = slots[a].shape[1]
            step = 64 if rows % 64 == 0 else 8

            def add_rows(t, carry, a=a, step=step):
                r0 = pl.multiple_of(t * step, step)
                total = slots[a][0, pl.ds(r0, step), :].astype(F32)
                for dev in range(1, N_DEV):
                    total = total + slots[a][dev, pl.ds(r0, step), :].astype(F32)
                sums[a][pl.ds(r0, step), :] = total
                return carry

            lax.fori_loop(0, rows // step, add_rows, 0)

        shares = []
        keeps = []
        for a in range(n_big):
            r, w = _GRAD_PIECE_SHAPES[a]
            dst = out[a].at[pl.ds(pl.multiple_of(c * r, 8), r), :]
            cp = pltpu.make_async_remote_copy(src_ref=sums[a], dst_ref=dst, send_sem=send2.at[a], recv_sem=recv2.at[a],
                                              device_id=(x, y, 1 - c), device_id_type=MESH)
            cp.start()
            shares.append(cp)
            kp = pltpu.make_async_copy(sums[a], dst, local_sems.at[N_DEV * n_arr + a])
            kp.start()
            keeps.append(kp)
        kp = pltpu.make_async_copy(sums[n_big], out[n_big].at[me], local_sems.at[N_DEV * n_arr + n_big])
        kp.start()
        keeps.append(kp)

        def small_share(dst_dev, src_dev):
            return pltpu.make_async_remote_copy(src_ref=sums[n_big], dst_ref=out[n_big].at[src_dev],
                                                send_sem=send2.at[n_big + dst_dev], recv_sem=recv2.at[n_big + src_dev],
                                                device_id=_dev_id(dst_dev), device_id_type=MESH)

        for dev in range(N_DEV):
            @pl.when(me != dev)
            def _():
                small_share(dev, me).start()
        for a in range(n_big):
            r, w = _GRAD_PIECE_SHAPES[a]
            other = out[a].at[pl.ds(pl.multiple_of((1 - c) * r, 8), r), :]
            pltpu.make_async_remote_copy(src_ref=sums[a], dst_ref=other, send_sem=send2.at[a], recv_sem=recv2.at[a],
                                         device_id=(x, y, 1 - c), device_id_type=MESH).wait_recv()
        for dev in range(N_DEV):
            @pl.when(me != dev)
            def _():
                small_share(dev, dev).wait_recv()
                small_share(dev, me).wait_send()
                for a in late_arrays(dev):
                    late(a, dev, me).wait_send()
        for cp in shares:
            cp.wait_send()
        for kp in keeps:
            kp.wait()

    any_spec = pl.BlockSpec(memory_space=pl.ANY)
    return pl.pallas_call(
        body, name="reduce_grads_tail",
        in_specs=[any_spec] * (n_arr + n_big + 1), out_specs=[any_spec] * n_arr,
        out_shape=[jax.ShapeDtypeStruct(sh, F32) for sh in shard_shapes]
        + [jax.ShapeDtypeStruct((N_DEV,) + small_piece, F32)],
        scratch_shapes=[pltpu.VMEM((N_DEV,) + sh, BF16) for sh in _GRAD_PIECE_SHAPES]
        + [pltpu.VMEM((N_DEV,) + small_piece, F32)]
        + [pltpu.VMEM(sh, F32) for sh in _GRAD_PIECE_SHAPES] + [pltpu.VMEM(small_piece, F32)]
        + [pltpu.SemaphoreType.DMA((N_DEV * n_arr,)), pltpu.SemaphoreType.DMA((N_DEV * n_arr,)),
           pltpu.SemaphoreType.DMA((n_big + N_DEV,)), pltpu.SemaphoreType.DMA((n_big + N_DEV,)),
           pltpu.SemaphoreType.DMA((N_DEV * n_arr + n_arr,))],
        compiler_params=pltpu.CompilerParams(vmem_limit_bytes=VMEM_LIMIT, has_side_effects=True),
    )(*grads, g_small, *early_slots, late_in_slots)


def _reduce_grads(g_in, g_up_a, g_up_b, g_out, g_small):
    big = (g_in, g_up_a, g_up_b, g_out)
    n_big = len(big)
    col_sharded = (True, True, True, False)
    piece_shapes = []
    for a, arr in enumerate(big):
        r, w = arr.shape
        piece_shapes.append((r // 2, w // N_CHIPS) if col_sharded[a] else (r // (2 * N_CHIPS), w))
    shard_shapes = [(2 * r, w) for r, w in piece_shapes]
    n_arr = n_big + 1

    def body(*refs):
        src = refs[:n_arr]
        out = refs[n_arr:2 * n_arr]
        slots = refs[2 * n_arr:3 * n_arr]
        sums = refs[3 * n_arr:4 * n_arr]
        send1, recv1, send2, recv2, local_sems = refs[4 * n_arr:]
        x, y, c = _place()
        me = 4 * x + 2 * y + c

        def piece_of(a, dev):
            chip_idx, half = dev // 2, dev % 2
            if a == n_big:
                return src[a].at[dev]
            r, w = piece_shapes[a]
            if col_sharded[a]:
                return src[a].at[pl.ds(pl.multiple_of(half * r, 16), r), pl.ds(pl.multiple_of(chip_idx * w, LANES), w)]
            return src[a].at[pl.ds(pl.multiple_of(dev * r, 16), r), :]

        def dev_id(dev):
            return (dev // 4, (dev // 2) % 2, dev % 2)

        own = [pltpu.make_async_copy(piece_of(a, me), slots[a].at[me], local_sems.at[a]) for a in range(n_arr)]
        for cp in own:
            cp.start()
        sends = []
        for d in range(1, N_DEV):
            peer = (me + d) % N_DEV
            for a in range(n_arr):
                cp = pltpu.make_async_remote_copy(
                    src_ref=piece_of(a, peer), dst_ref=slots[a].at[me],
                    send_sem=send1.at[n_arr * peer + a], recv_sem=recv1.at[n_arr * me + a],
                    device_id=dev_id(peer), device_id_type=MESH)
                cp.start()
                sends.append(cp)
        for cp in own:
            cp.wait()
        for d in range(1, N_DEV):
            peer = (me + d) % N_DEV
            for a in range(n_arr):
                pltpu.make_async_remote_copy(
                    src_ref=piece_of(a, peer), dst_ref=slots[a].at[peer],
                    send_sem=send1.at[n_arr * peer + a], recv_sem=recv1.at[n_arr * peer + a],
                    device_id=dev_id(peer), device_id_type=MESH).wait_recv()
        for a in range(n_arr):
            rows = slots[a].shape[1]
            step = 64 if rows % 64 == 0 else 8

            def add_rows(t, carry, a=a, step=step):
                r0 = pl.multiple_of(t * step, step)
                total = slots[a][0, pl.ds(r0, step), :].astype(F32)
                for dev in range(1, N_DEV):
                    total = total + slots[a][dev, pl.ds(r0, step), :].astype(F32)
                sums[a][pl.ds(r0, step), :] = total
                return carry

            lax.fori_loop(0, rows // step, add_rows, 0)
        shares = []
        keeps = []
        for a in range(n_big):
            r, w = piece_shapes[a]
            dst = out[a].at[pl.ds(pl.multiple_of(c * r, 8), r), :]
            cp = pltpu.make_async_remote_copy(src_ref=sums[a], dst_ref=dst, send_sem=send2.at[a], recv_sem=recv2.at[a],
                                              device_id=(x, y, 1 - c), device_id_type=MESH)
            cp.start()
            shares.append(cp)
            kp = pltpu.make_async_copy(sums[a], dst, local_sems.at[n_arr + a])
            kp.start()
            keeps.append(kp)
        kp = pltpu.make_async_copy(sums[n_big], out[n_big].at[me], local_sems.at[n_arr + n_big])
        kp.start()
        keeps.append(kp)
        for d in range(1, N_DEV):
            peer = (me + d) % N_DEV
            cp = pltpu.make_async_remote_copy(src_ref=sums[n_big], dst_ref=out[n_big].at[me],
                                              send_sem=send2.at[n_big + peer], recv_sem=recv2.at[n_big + me],
                                              device_id=dev_id(peer), device_id_type=MESH)
            cp.start()
            shares.append(cp)
        for a in range(n_big):
            r, w = piece_shapes[a]
            other = out[a].at[pl.ds(pl.multiple_of((1 - c) * r, 8), r), :]
            pltpu.make_async_remote_copy(src_ref=sums[a], dst_ref=other, send_sem=send2.at[a], recv_sem=recv2.at[a],
                                         device_id=(x, y, 1 - c), device_id_type=MESH).wait_recv()
        for d in range(1, N_DEV):
            peer = (me + d) % N_DEV
            pltpu.make_async_remote_copy(src_ref=sums[n_big], dst_ref=out[n_big].at[peer],
                                         send_sem=send2.at[n_big + peer], recv_sem=recv2.at[n_big + peer],
                                         device_id=dev_id(peer), device_id_type=MESH).wait_recv()
        for cp in sends + shares:
            cp.wait_send()
        for kp in keeps:
            kp.wait()

    any_spec = pl.BlockSpec(memory_space=pl.ANY)
    small_piece = (SMALL_PIECE, LANES)
    return pl.pallas_call(
        body, name="reduce_grads",
        in_specs=[any_spec] * n_arr, out_specs=[any_spec] * n_arr,
        out_shape=[jax.ShapeDtypeStruct(sh, F32) for sh in shard_shapes]
        + [jax.ShapeDtypeStruct((N_DEV,) + small_piece, F32)],
        scratch_shapes=[pltpu.VMEM((N_DEV,) + sh, BF16) for sh in piece_shapes]
        + [pltpu.VMEM((N_DEV,) + small_piece, F32)]
        + [pltpu.VMEM(sh, F32) for sh in piece_shapes] + [pltpu.VMEM(small_piece, F32)]
        + [pltpu.SemaphoreType.DMA((N_DEV * n_arr,)), pltpu.SemaphoreType.DMA((N_DEV * n_arr,)),
           pltpu.SemaphoreType.DMA((n_big + N_DEV,)), pltpu.SemaphoreType.DMA((n_big + N_DEV,)),
           pltpu.SemaphoreType.DMA((2 * n_arr,))],
        compiler_params=pltpu.CompilerParams(vmem_limit_bytes=VMEM_LIMIT, has_side_effects=True),
    )(*big, g_small)


_SMALL_PARTS = (("norm_g", 8), ("sgu_ln_g", 8), ("sgu_ln_b", 8), ("w_spatial", 1024), ("b_spatial", 8),
                ("final_norm_g", 8))
_LOSS_ROW = sum(n for _, n in _SMALL_PARTS)


def _pack_small(parts, loss_tile=None):
    rows = []
    for name, n_rows in _SMALL_PARTS:
        a = parts[name].reshape(-1, LANES).astype(F32)
        a = jnp.pad(a, ((0, n_rows - a.shape[0]), (0, 0)))
        rows.append(a)
    rows.append(jnp.zeros((8, LANES), F32) if loss_tile is None else loss_tile)
    rows.append(jnp.zeros((SMALL_ROWS - _LOSS_ROW - 8, LANES), F32))
    return jnp.concatenate(rows, axis=0)


def _unpack_small(packed, shapes):
    out = {}
    r0 = 0
    for name, n_rows in _SMALL_PARTS:
        n = math.prod(shapes[name])
        out[name] = packed[r0:r0 + n // LANES].reshape(shapes[name])
        r0 += n_rows
    return out


def _local_step(proj, ht, x, target, norm_g, w_in, sgu_ln_g, sgu_ln_b, w_spatial, b_spatial, w_up_a, w_up_b, w_out,
                final_norm_g, bq, bk):
    pos = jnp.arange(SGU_CHUNK)
    keep = (pos[None, :] // SGU_SUBCHUNK) <= (pos[:, None] // SGU_SUBCHUNK)
    w_mask = jnp.where(keep[None], w_spatial, 0.0).astype(BF16)
    w_mask_t = jnp.swapaxes(w_mask, 1, 2)
    bias_full = jnp.repeat(b_spatial.T, GROUP_DIM, axis=1)
    ln_g = sgu_ln_g.reshape(1, D_BRANCH)
    ln_b = sgu_ln_b.reshape(1, D_BRANCH)
    final_g = final_norm_g.reshape(1, D_MODEL)

    o, ya, rsave = _attn_fwd(proj, bq, bk, ATTN_PAIRS)
    yb = _sgu_fwd(proj, ln_g, ln_b, w_mask, bias_full)
    dzg, do, dyb, dx2, g_out, g_up_a, g_up_b, loss_acc, d_final = _mid(
        proj, ya, yb, o, x, target, final_g, w_up_a, w_up_b, w_out)
    dsgu, d_wsp, d_bsp, d_lng, d_lnb, *up_out_slots = _sgu_bwd(proj, dyb, ln_g, ln_b, w_mask, w_mask_t, bias_full,
                                                               (g_up_a, g_up_b, g_out))
    g_in = _dwin_piece(ht, dzg, lambda j: jnp.where(j == 0, COL_ZA, COL_GA - 1 + j), None)
    g_in = _dwin_piece(ht, dsgu, lambda j: COL_UB + j, g_in)
    dq, dk, dv, early_in_slots = _attn_bwd(proj, do, rsave, bq, bk, ATTN_PAIRS, g_in)
    early_slots = [early_in_slots] + up_out_slots
    g_in = _dwin_piece(ht, dq, lambda j: COL_Q + j, g_in)
    g_in = _dwin_piece(ht, dk, lambda j: COL_K + j, g_in)
    g_in = _dwin_piece(ht, dv, lambda j: COL_V + j, g_in)
    pieces = [(dq, COL_Q * D_BRANCH, 0, D_BRANCH), (dk, COL_K * D_BRANCH, 0, D_BRANCH),
              (dv, COL_V * D_BRANCH, 0, D_BRANCH), (dzg, COL_ZA * D_BRANCH, 0, D_BRANCH),
              (dsgu, COL_UB * D_BRANCH, 0, 3 * D_BRANCH), (dzg, COL_GA * D_BRANCH, D_BRANCH, 2 * D_MODEL)]
    dx, d_norm, late_in_slots = _dh_dx(pieces, w_in, x, norm_g, dx2, g_in)
    small = {"norm_g": d_norm, "sgu_ln_g": d_lng, "sgu_ln_b": d_lnb, "w_spatial": d_wsp,
             "b_spatial": d_bsp[:, :N_GROUPS].T, "final_norm_g": d_final}
    return loss_acc, dx, (g_in, g_up_a, g_up_b, g_out), small, early_slots, late_in_slots


def kernel(x, norm_g, w_in, sgu_ln_g, sgu_ln_b, w_spatial, b_spatial, w_up_a, w_up_b, w_out, final_norm_g, loss_target, m_norm_g, m_w_in, m_sgu_ln_g, m_sgu_ln_b, m_w_spatial, m_b_spatial, m_w_up_a, m_w_up_b, m_w_out, m_final_norm_g, v_norm_g, v_w_in, v_sgu_ln_g, v_sgu_ln_b, v_w_spatial, v_b_spatial, v_w_up_a, v_w_up_b, v_w_out, v_final_norm_g):
    big_names = ("w_in", "w_up_a", "w_up_b", "w_out")
    small_names = tuple(n for n, _ in _SMALL_PARTS)
    names = ("norm_g", "w_in", "sgu_ln_g", "sgu_ln_b", "w_spatial", "b_spatial", "w_up_a", "w_up_b", "w_out",
             "final_norm_g")
    w = dict(norm_g=norm_g, w_in=w_in, sgu_ln_g=sgu_ln_g, sgu_ln_b=sgu_ln_b, w_spatial=w_spatial,
             b_spatial=b_spatial, w_up_a=w_up_a, w_up_b=w_up_b, w_out=w_out, final_norm_g=final_norm_g)
    m = dict(norm_g=m_norm_g, w_in=m_w_in, sgu_ln_g=m_sgu_ln_g, sgu_ln_b=m_sgu_ln_b, w_spatial=m_w_spatial,
             b_spatial=m_b_spatial, w_up_a=m_w_up_a, w_up_b=m_w_up_b, w_out=m_w_out, final_norm_g=m_final_norm_g)
    v = dict(norm_g=v_norm_g, w_in=v_w_in, sgu_ln_g=v_sgu_ln_g, sgu_ln_b=v_sgu_ln_b, w_spatial=v_w_spatial,
             b_spatial=v_b_spatial, w_up_a=v_w_up_a, w_up_b=v_w_up_b, w_out=v_w_out, final_norm_g=v_final_norm_g)
    shapes = {n: w[n].shape for n in names}
    flat2d = lambda a: a.reshape(a.shape[-2:])

    proj, ht, *full = _in_proj_gather(x[0], norm_g, *[flat2d(w[n]) for n in big_names])
    loss, dx, big_grads, small, early_slots, late_in_slots = _local_step(
        proj, ht, x[0], loss_target[0], norm_g, full[0], sgu_ln_g[0], sgu_ln_b[0], w_spatial[0], b_spatial[0],
        full[1], full[2], full[3], final_norm_g, ATTN_Q_BLOCK, ATTN_K_BLOCK)
    packed = _pack_small(small, loss).reshape(N_DEV, SMALL_PIECE, LANES)
    red = _reduce_grads_tail(big_grads, packed, early_slots, late_in_slots)

    grads, deltas, new_m, new_v = {}, {}, {}, {}
    for n, g in zip(big_names, red[:4]):
        g, d, nm, nv = _adamw(flat2d(w[n]), g, flat2d(m[n]), flat2d(v[n]))
        grads[n], deltas[n], new_m[n], new_v[n] = (a.reshape(shapes[n]) for a in (g, d, nm, nv))
    g_small = red[4].reshape(SMALL_ROWS, LANES)
    g_small, d, nm, nv = _adamw(_pack_small({n: w[n] for n in small_names}), g_small,
                                _pack_small({n: m[n] for n in small_names}),
                                _pack_small({n: v[n] for n in small_names}))
    for src, dst in ((g_small, grads), (d, deltas), (nm, new_m), (nv, new_v)):
        dst.update(_unpack_small(src, shapes))

    return (g_small[_LOSS_ROW, 0], dx[None], *[grads[n] for n in names], *[deltas[n] for n in names],
            *[new_m[n] for n in names], *[new_v[n] for n in names])
```

```python
import functools
import math

import jax
import jax.numpy as jnp
from jax import lax
from jax.experimental import pallas as pl
from jax.experimental.pallas import tpu as pltpu

F32 = jnp.float32
BF16 = jnp.bfloat16

D_MODEL = 1024
N_HEADS = 8
HEAD_DIM = 64
D_BRANCH = 512
D_IN = 4 * D_BRANCH + 3 * D_BRANCH + 2 * D_MODEL
N_GROUPS = 8
GROUP_DIM = 64
SGU_CHUNK = 128
SGU_SUBCHUNK = 64
GROUP_SHIFT = 6
EPS = 1e-6
LANES = 128
ATTN_Q_BLOCK = 256
ATTN_K_BLOCK = 256
DEAD = -110.0
SKIPPED = -1e30
SCAN_PASSES = 1
ATTN_PAIRS = 2
N_CHIPS = 4
N_DEV = 8
MESH = pl.DeviceIdType.MESH

ADAM_LR = 0.001
ADAM_B1 = 0.9
ADAM_B2 = 0.999
ADAM_EPS = 1e-08
ADAM_WD = 0.01
ADAM_STEP = 10

COL_Q, COL_K, COL_V, COL_ZA, COL_UB, COL_VB, COL_ZB, COL_GA, COL_GB = 0, 1, 2, 3, 4, 5, 6, 7, 9

VMEM_LIMIT = 56 * 1024 * 1024

SMALL_ROWS = 1088
SMALL_PIECE = SMALL_ROWS // N_DEV


def _cparams(sem=None):
    return pltpu.CompilerParams(dimension_semantics=sem, vmem_limit_bytes=VMEM_LIMIT)


def _aligned(v, m):
    return v if isinstance(v, int) else pl.multiple_of(v, m)


def _sigmoid(x):
    return 1.0 / (1.0 + jnp.exp(-x))


def _gelu_and_grad(x):
    k = math.sqrt(2.0 / math.pi)
    x2 = x * x
    inner = k * (x + 0.044715 * x * x2)
    th = jnp.tanh(inner)
    g = 0.5 * x * (1.0 + th)
    dg = 0.5 * (1.0 + th) + 0.5 * x * (1.0 - th * th) * (k * (1.0 + 3.0 * 0.044715 * x2))
    return g, dg


def _split_dot(a, b_bf16, passes):
    out = None
    rem = a
    for _ in range(passes):
        part = rem.astype(BF16)
        d = jnp.dot(part, b_bf16, preferred_element_type=F32)
        out = d if out is None else out + d
        rem = rem - part.astype(F32)
    return out


def _dot_nt(a, b):
    return lax.dot_general(a, b, (((1,), (1,)), ((), ())), preferred_element_type=F32)


def _dot_tn(a, b):
    return lax.dot_general(a, b, (((0,), (0,)), ((), ())), preferred_element_type=F32)


def _place():
    x, y, c = lax.axis_index("x"), lax.axis_index("y"), lax.axis_index("c")
    return x, y, c


def _in_proj_gather(x, norm_g, w_in, w_up_a, w_up_b, w_out):
    s = x.shape[0]
    tm = min(1024, s)
    nt = s // tm
    shards = (w_in, w_up_a, w_up_b, w_out)
    n_arr = len(shards)
    col_sharded = (True, True, True, False)
    full_shapes = ((D_MODEL, D_IN), (D_BRANCH, D_MODEL), (D_BRANCH, D_MODEL), (D_MODEL, D_MODEL))
    w_shard = w_in.shape[1]
    half_rows = D_MODEL // 2
    stage_rows = 256

    def body(order_ref, x_ref, g_ref, *refs):
        src = refs[:n_arr]
        proj_ref, ht_ref = refs[n_arr:n_arr + 2]
        out = refs[n_arr + 2:2 * n_arr + 2]
        wsc, h_scr, stage = refs[2 * n_arr + 2:2 * n_arr + 5]
        small_stage = refs[2 * n_arr + 5:2 * n_arr + 8]
        small_cast = refs[2 * n_arr + 8:2 * n_arr + 11]
        send_sems, recv_sems, local_sems = refs[2 * n_arr + 11:]
        k = pl.program_id(0)
        i = pl.program_id(1)
        x_, y_, c = _place()
        chip = 2 * x_ + y_
        sibling = (x_, y_, 1 - c)
        others = [(x_, 1 - y_), (1 - x_, y_), (1 - x_, 1 - y_)]

        def region(a, chip_idx, half):
            if a == 0:
                return wsc.at[chip_idx, pl.ds(_aligned(half * half_rows, 16), half_rows), :]
            r, w = shards[a].shape
            hr = r // 2
            if col_sharded[a]:
                return out[a].at[pl.ds(_aligned(half * hr, 16), hr), pl.ds(_aligned(chip_idx * w, LANES), w)]
            return out[a].at[pl.ds(_aligned(chip_idx * r + half * hr, 16), hr), :]

        def remote(kk, a, chip_idx, half, to, own):
            s_ref = region(a, chip_idx, half)
            if own and a > 0:
                hr = shards[a].shape[0] // 2
                s_ref = small_cast[a - 1].at[pl.ds(_aligned(half * hr, 16), hr), :]
            return pltpu.make_async_remote_copy(src_ref=s_ref, dst_ref=region(a, chip_idx, half),
                                                send_sem=send_sems.at[kk], recv_sem=recv_sems.at[kk],
                                                device_id=to, device_id_type=MESH)

        def keep_whole(kk, chip_idx):
            return pltpu.make_async_copy(wsc.at[chip_idx],
                                         out[0].at[:, pl.ds(_aligned(chip_idx * w_shard, LANES), w_shard)],
                                         local_sems.at[kk])

        def small_stores():
            cps = []
            for a in range(1, n_arr):
                hr = shards[a].shape[0] // 2
                for half in range(2):
                    cps.append(pltpu.make_async_copy(small_cast[a - 1].at[pl.ds(half * hr, hr), :],
                                                     region(a, chip, half), local_sems.at[4 + 2 * (a - 1) + half]))
            return cps

        def arrive_and_pass(j):
            ochip = chip ^ j
            for a in range(n_arr):
                kk = n_arr * (j - 1) + a
                remote(kk, a, ochip, c, sibling, False).wait_recv()
                remote(3 * n_arr + kk, a, ochip, c, sibling, False).start()

        def from_sibling(j, a):
            remote(3 * n_arr + n_arr * (j - 1) + a, a, chip ^ j, 1 - c, sibling, False).wait_recv()

        @pl.when((k == 0) & (i == 0))
        def _():
            def cast_rows(half):
                for t in range(half_rows // stage_rows):
                    r0 = pl.multiple_of(half * half_rows + t * stage_rows, stage_rows)
                    pltpu.sync_copy(src[0].at[pl.ds(r0, stage_rows), :], stage)
                    wsc[chip, pl.ds(r0, stage_rows), :] = stage[...].astype(BF16)

            cast_rows(c)
            for j in (1, 2):
                remote(n_arr * (j - 1), 0, chip, c, (*others[j - 1], c), True).start()
            cast_rows(1 - c)
            for a in range(1, n_arr):
                pltpu.sync_copy(src[a], small_stage[a - 1])
                small_cast[a - 1][...] = small_stage[a - 1][...].astype(BF16)
            for j in (1, 2):
                for a in range(1, n_arr):
                    remote(n_arr * (j - 1) + a, a, chip, c, (*others[j - 1], c), True).start()
            keep_whole(0, chip).start()
            for cp in small_stores():
                cp.start()

        @pl.when((k == 1) & (i == 0))
        def _():
            for j in (1, 2):
                remote(n_arr * (j - 1), 0, chip, c, (*others[j - 1], c), True).wait_send()
            for a in range(n_arr):
                remote(n_arr * 2 + a, a, chip, c, (*others[2], c), True).start()
            arrive_and_pass(1)
            arrive_and_pass(2)
            from_sibling(1, 0)
            keep_whole(1, chip ^ 1).start()

        @pl.when((k == 2) & (i == 0))
        def _():
            from_sibling(2, 0)
            keep_whole(2, chip ^ 2).start()
            arrive_and_pass(3)

        @pl.when((k == 3) & (i == 0))
        def _():
            from_sibling(3, 0)
            keep_whole(3, chip ^ 3).start()

        @pl.when(k == 0)
        def _():
            xf = x_ref[...]
            r = lax.rsqrt(jnp.mean(xf * xf, axis=-1, keepdims=True) + EPS)
            h = xf * r * g_ref[...]
            h_scr[i] = h.astype(BF16)
            ht_ref[...] = h.T.astype(BF16)

        proj_ref[...] = jnp.dot(h_scr[i], wsc[order_ref[k]], preferred_element_type=F32).astype(BF16)

        @pl.when((k == 3) & (i == nt - 1))
        def _():
            for j in (1, 2, 3):
                for a in range(1, n_arr):
                    from_sibling(j, a)
            for j in (1, 2, 3):
                for a in range(n_arr):
                    kk = n_arr * (j - 1) + a
                    if a > 0 or j == 3:
                        remote(kk, a, chip, c, (*others[j - 1], c), True).wait_send()
                    remote(3 * n_arr + kk, a, chip ^ j, c, sibling, False).wait_send()
            for kk in range(4):
                keep_whole(kk, chip ^ kk).wait()
            for cp in small_stores():
                cp.wait()

    any_spec = pl.BlockSpec(memory_space=pl.ANY)
    tile = lambda kk, ii: jnp.where(kk == 0, ii, nt - 1)
    grid_spec = pltpu.PrefetchScalarGridSpec(
        num_scalar_prefetch=1, grid=(N_CHIPS, nt),
        in_specs=[pl.BlockSpec((tm, D_MODEL), lambda kk, ii, order: (tile(kk, ii), 0)),
                  pl.BlockSpec((1, D_MODEL), lambda kk, ii, order: (0, 0))] + [any_spec] * n_arr,
        out_specs=[pl.BlockSpec((tm, w_shard), lambda kk, ii, order: (ii, order[kk])),
                   pl.BlockSpec((D_MODEL, tm), lambda kk, ii, order: (0, tile(kk, ii)))] + [any_spec] * n_arr,
        scratch_shapes=[pltpu.VMEM((N_CHIPS, D_MODEL, w_shard), BF16), pltpu.VMEM((nt, tm, D_MODEL), BF16),
                        pltpu.VMEM((stage_rows, w_shard), F32)]
        + [pltpu.VMEM(a.shape, F32) for a in shards[1:]] + [pltpu.VMEM(a.shape, BF16) for a in shards[1:]]
        + [pltpu.SemaphoreType.DMA((6 * n_arr,)), pltpu.SemaphoreType.DMA((6 * n_arr,)),
           pltpu.SemaphoreType.DMA((4 + 2 * (n_arr - 1),))])
    x_, y_, _ = _place()
    order = (2 * x_ + y_) ^ jnp.arange(N_CHIPS, dtype=jnp.int32)
    return pl.pallas_call(
        body, name="in_proj_gather", grid_spec=grid_spec,
        out_shape=[jax.ShapeDtypeStruct((s, D_IN), BF16), jax.ShapeDtypeStruct((D_MODEL, s), BF16)]
        + [jax.ShapeDtypeStruct(sh, BF16) for sh in full_shapes],
        compiler_params=pltpu.CompilerParams(dimension_semantics=("arbitrary", "arbitrary"),
                                             vmem_limit_bytes=VMEM_LIMIT, has_side_effects=True),
    )(order, x, norm_g, *shards)


def _neg_softplus_parts(z):
    zb = z.astype(BF16)
    p = jnp.exp(-jnp.abs(zb))
    return p, jnp.maximum(zb, jnp.zeros_like(zb)) + jnp.log(1.0 + p)


def _split_cat(a, passes):
    parts = []
    rem = a
    for k in range(passes):
        part = rem.astype(BF16)
        parts.append(part)
        if k + 1 < passes:
            rem = rem - part.astype(F32)
    return parts[0] if passes == 1 else jnp.concatenate(parts, axis=1)


def _tri(blk, upper, sign):
    row = lax.broadcasted_iota(jnp.int32, (blk, blk), 0)
    col = lax.broadcasted_iota(jnp.int32, (blk, blk), 1)
    keep = (row <= col) if upper else (row >= col)
    t = jnp.where(keep, sign, 0.0).astype(BF16)
    return t if SCAN_PASSES == 1 else jnp.concatenate([t] * SCAN_PASSES, axis=0)


def _attn_fwd(proj, bq, bk, npairs):
    s = proj.shape[0]
    nq = s // bq
    ratio = bq // bk
    scale = HEAD_DIM ** -0.5
    heads = tuple(range(2 * npairs))
    width = LANES * npairs

    def body(q_ref, k_ref, v_ref, za_ref, o_ref, ya_ref, rs_ref, acc_ref, r_ref, z_ref):
        i = pl.program_id(1)
        lane = lax.broadcasted_iota(jnp.int32, (bq, LANES), 1)
        lo_half = lane < HEAD_DIM
        qm = []
        for pr in range(npairs):
            q = q_ref[:, LANES * pr:LANES * (pr + 1)] * jnp.asarray(scale, BF16)
            zero = jnp.zeros_like(q)
            qm += [jnp.where(lo_half, q, zero), jnp.where(lo_half, zero, q)]
        row = lax.broadcasted_iota(jnp.int32, (bq, bk), 0)
        col = lax.broadcasted_iota(jnp.int32, (bq, bk), 1)
        tneg = _tri(bk, False, -1.0)
        acc_ref[...] = jnp.zeros_like(acc_ref)
        r_ref[...] = jnp.zeros_like(r_ref)
        rs_ref[...] = jnp.full_like(rs_ref, SKIPPED)

        def scores(j):
            ks = pl.multiple_of(j * bk, bk)
            return [_dot_nt(qm[h], k_ref[pl.ds(ks, bk), LANES * (h // 2):LANES * (h // 2 + 1)]) for h in heads]

        for h, zh in enumerate(scores(i * ratio + ratio - 1)):
            z_ref[h] = zh

        def block(j, diag):
            ks = pl.multiple_of(j * bk, bk)
            vj = [v_ref[pl.ds(ks, bk), LANES * pr:LANES * (pr + 1)] for pr in range(npairs)]
            if diag:
                before = (j * bk + col) < (i * bq + row)
            z = [z_ref[h] for h in heads]
            sp = [_neg_softplus_parts(z[h])[1] for h in heads]
            if diag:
                sp = [jnp.where(before, sp[h], 0.0) for h in heads]
            cin = [jnp.dot(_split_cat(sp[h], SCAN_PASSES), tneg, preferred_element_type=F32) for h in heads]
            for h, zh in enumerate(scores(jnp.maximum(j - 1, 0))):
                z_ref[h] = zh
            w = [jnp.exp(z[h] + cin[h]) for h in heads]
            if diag:
                w = [jnp.where(before, w[h], 0.0) for h in heads]
            pv = [jnp.dot(w[h].astype(BF16), vj[h // 2], preferred_element_type=F32) for h in heads]
            r = [r_ref[h] for h in heads]
            for h in heads:
                acc_ref[h] += pv[h] * jnp.exp(r[h])
                r_ref[h] = r[h] + cin[h][:, 0:1]
            for pr in range(npairs):
                rs_ref[pr] = jnp.where(lane == j, r[2 * pr], jnp.where(lane == j + HEAD_DIM, r[2 * pr + 1], rs_ref[pr]))

        for t in range(ratio):
            block(i * ratio + ratio - 1 - t, True)

        def alive(carry):
            jj, r_max = carry
            return (jj < i * ratio) & (r_max > DEAD)

        def loop_body(carry):
            jj, _ = carry
            block(i * ratio - 1 - jj, False)
            return jj + 1, jnp.max(r_ref[...])

        lax.while_loop(alive, loop_body, (0, jnp.max(r_ref[...])))
        for pr in range(npairs):
            cols = slice(LANES * pr, LANES * (pr + 1))
            o = jnp.where(lo_half, acc_ref[2 * pr], acc_ref[2 * pr + 1])
            o_ref[:, cols] = o.astype(BF16)
            za = za_ref[:, cols].astype(F32)
            ya_ref[:, cols] = (o * (za * _sigmoid(za))).astype(BF16)

    n_steps = N_HEADS // (2 * npairs)
    return pl.pallas_call(
        body, name="attn_fwd", grid=(n_steps, nq),
        in_specs=[pl.BlockSpec((bq, width), lambda p, i: (i, n_steps * COL_Q + p)),
                  pl.BlockSpec((s, width), lambda p, i: (0, n_steps * COL_K + p)),
                  pl.BlockSpec((s, width), lambda p, i: (0, n_steps * COL_V + p)),
                  pl.BlockSpec((bq, width), lambda p, i: (i, n_steps * COL_ZA + p))],
        out_specs=[pl.BlockSpec((bq, width), lambda p, i: (i, p)),
                   pl.BlockSpec((bq, width), lambda p, i: (i, p)),
                   pl.BlockSpec((npairs, bq, LANES), lambda p, i: (p, i, 0))],
        out_shape=[jax.ShapeDtypeStruct((s, D_BRANCH), BF16), jax.ShapeDtypeStruct((s, D_BRANCH), BF16),
                   jax.ShapeDtypeStruct((N_HEADS // 2, s, LANES), F32)],
        scratch_shapes=[pltpu.VMEM((2 * npairs, bq, LANES), F32), pltpu.VMEM((2 * npairs, bq, 1), F32),
                        pltpu.VMEM((2 * npairs, bq, bk), F32)],
        compiler_params=_cparams(("parallel", "parallel")),
    )(proj, proj, proj, proj)


_GRAD_COL_SHARDED = (True, True, True, False)
_GRAD_FULL_SHAPES = ((D_MODEL, D_IN), (D_BRANCH, D_MODEL), (D_BRANCH, D_MODEL), (D_MODEL, D_MODEL))
_GRAD_PIECE_SHAPES = tuple((r // 2, w // N_CHIPS) if cs else (r // (2 * N_CHIPS), w)
                           for (r, w), cs in zip(_GRAD_FULL_SHAPES, _GRAD_COL_SHARDED))
_EARLY_IN_DEVS = (4, 5, 6, 7)
_LATE_IN_DEVS = (0, 1, 2, 3)
_LATE_CHIPS = (0, 1)
_EARLY_CHIPS = (2, 3)
_ALL_CHIPS = (0, 1, 2, 3)


def _grad_piece(ref, a, dev):
    r, w = _GRAD_PIECE_SHAPES[a]
    if _GRAD_COL_SHARDED[a]:
        return ref.at[pl.ds((dev % 2) * r, r), pl.ds((dev // 2) * w, w)]
    return ref.at[pl.ds(dev * r, r), :]


def _dev_id(dev):
    return (dev // 4, (dev // 2) % 2, dev % 2)


def _me():
    return 4 * lax.axis_index("x") + 2 * lax.axis_index("y") + lax.axis_index("c")


def _presum_copy(src, dst, send_sem, recv_sem, to_dev):
    return pltpu.make_async_remote_copy(src_ref=src, dst_ref=dst, send_sem=send_sem, recv_sem=recv_sem,
                                        device_id=_dev_id(to_dev), device_id_type=MESH)


def _presum_hand_off(dev, a, dest_chips, g_ref, slots, pair, send_sems, recv_sems):
    chip, core = dev // 2, dev % 2
    cps = []
    for k, q in enumerate(dest_chips):
        piece = _grad_piece(g_ref, a, 2 * q + 1 - core)
        if q == chip:
            cps.append(_presum_copy(piece, slots.at[dev], send_sems.at[N_DEV + k], recv_sems.at[dev], dev ^ 1))
        else:
            cps.append(_presum_copy(piece, pair.at[k], send_sems.at[N_DEV + k], recv_sems.at[N_DEV + k], dev ^ 1))
    return cps


def _presum_sends(dev, a, dest_chips, slots, sums, send_sems, recv_sems):
    chip, core = dev // 2, dev % 2
    return [_presum_copy(sums.at[k], slots.at[dev], send_sems.at[2 * q + core], recv_sems.at[dev], 2 * q + core)
            for k, q in enumerate(dest_chips) if q != chip]


def _presum_send(dev, a, dest_chips, g_ref, slots, pair, stage, sums, send_sems, recv_sems):
    chip, core = dev // 2, dev % 2
    hand = _presum_hand_off(dev, a, dest_chips, g_ref, slots, pair, send_sems, recv_sems)
    for k, q in enumerate(dest_chips):
        if q != chip:
            hand[k].wait_recv()
            pltpu.sync_copy(_grad_piece(g_ref, a, 2 * q + core), stage)
            sums[k] = (stage[...].astype(F32) + pair[k].astype(F32)).astype(BF16)
    for cp in _presum_sends(dev, a, dest_chips, slots, sums, send_sems, recv_sems):
        cp.start()


def _presum_wait(dev, a, dest_chips, g_ref, slots, pair, sums, send_sems, recv_sems):
    chip, core = dev // 2, dev % 2
    for cp in _presum_hand_off(dev, a, dest_chips, g_ref, slots, pair, send_sems, recv_sems):
        cp.wait_send()
    for cp in _presum_sends(dev, a, dest_chips, slots, sums, send_sems, recv_sems):
        cp.wait_send()
    if chip in dest_chips:
        for src_dev in _presum_sources(dev):
            _presum_copy(sums.at[0], slots.at[src_dev], send_sems.at[src_dev], recv_sems.at[src_dev], src_dev).wait_recv()


def _presum_sources(dev):
    return [dev ^ 1] + [2 * r + dev % 2 for r in range(N_CHIPS) if r != dev // 2]


def _presum_scratch(a, dest_chips):
    n = len(dest_chips)
    piece = _GRAD_PIECE_SHAPES[a]
    return [pltpu.VMEM((n,) + piece, BF16), pltpu.VMEM(piece, BF16), pltpu.VMEM((n,) + piece, BF16),
            pltpu.SemaphoreType.DMA((N_DEV + n,)), pltpu.SemaphoreType.DMA((N_DEV + n,))]


def _presum_program(first, second, last, a, dest_chips, g_ref, slots, scratch):
    pair, stage, sums, send_sems, recv_sems = scratch
    me = _me()

    @pl.when(first)
    def _():
        for dev in range(N_DEV):
            @pl.when(me == dev)
            def _():
                for cp in _presum_hand_off(dev, a, dest_chips, g_ref, slots, pair, send_sems, recv_sems):
                    cp.start()

    @pl.when(second)
    def _():
        for dev in range(N_DEV):
            @pl.when(me == dev)
            def _():
                _presum_send(dev, a, dest_chips, g_ref, slots, pair, stage, sums, send_sems, recv_sems)

    def finish():
        for dev in range(N_DEV):
            @pl.when(me == dev)
            def _():
                _presum_wait(dev, a, dest_chips, g_ref, slots, pair, sums, send_sems, recv_sems)

    return finish


def _attn_bwd(proj, do, rsave, bq, bk, npairs, grads):
    plan = ((0, _EARLY_CHIPS), (1, _ALL_CHIPS), (2, _ALL_CHIPS), (3, _ALL_CHIPS))
    s = proj.shape[0]
    nq = s // bq
    ratio = bq // bk
    scale = HEAD_DIM ** -0.5
    heads = tuple(range(2 * npairs))
    width = LANES * npairs

    n_steps = N_HEADS // (2 * npairs)
    n_g = len(grads)

    def body(q_ref, k_ref, v_ref, do_ref, rs_ref, *refs):
        g_src = refs[:n_g]
        dq_ref, dk_ref, dv_ref = refs[n_g:n_g + 3]
        g_slots = refs[n_g + 3:2 * n_g + 3]
        dk_acc, dv_acc, dq_acc, e_ref, qmt_ref, domt_ref, rst_ref = refs[2 * n_g + 3:2 * n_g + 10]
        i = pl.program_id(1)
        step = pl.program_id(0) * nq + i
        finish = [_presum_program(step == 0, step == 1, step == n_steps * nq - 1, a, chips, g_src[pos], g_slots[pos],
                                  refs[2 * n_g + 10 + 5 * pos:2 * n_g + 15 + 5 * pos])
                  for pos, (a, chips) in enumerate(plan)]

        lane = lax.broadcasted_iota(jnp.int32, (bq, LANES), 1)
        lo_half = lane < HEAD_DIM
        qm, dom = [], []
        for pr in range(npairs):
            cols = slice(LANES * pr, LANES * (pr + 1))
            q = q_ref[:, cols] * jnp.asarray(scale, BF16)
            zero = jnp.zeros_like(q)
            qm += [jnp.where(lo_half, q, zero), jnp.where(lo_half, zero, q)]
            dout = do_ref[:, cols].astype(F32)
            dom += [jnp.where(lo_half, dout, 0.0), jnp.where(lo_half, 0.0, dout)]
        row = lax.broadcasted_iota(jnp.int32, (bq, bk), 0)
        col = lax.broadcasted_iota(jnp.int32, (bq, bk), 1)
        tneg = _tri(bk, False, -1.0)
        tfwd = _tri(bk, True, 1.0)

        @pl.when(i == 0)
        def _():
            dk_acc[...] = jnp.zeros_like(dk_acc)
            dv_acc[...] = jnp.zeros_like(dv_acc)

        dq_acc[...] = jnp.zeros_like(dq_acc)
        e_ref[...] = jnp.zeros_like(e_ref)
        for h in heads:
            qmt_ref[h] = qm[h].astype(F32).T.astype(BF16)
            domt_ref[h] = dom[h].T
        for pr in range(npairs):
            rst_ref[pr] = rs_ref[pr].T

        def block(j, diag):
            ks = pl.multiple_of(j * bk, bk)
            kj = [k_ref[pl.ds(ks, bk), LANES * pr:LANES * (pr + 1)] for pr in range(npairs)]
            vj = [v_ref[pl.ds(ks, bk), LANES * pr:LANES * (pr + 1)] for pr in range(npairs)]
            if diag:
                before = (j * bk + col) < (i * bq + row)
            z = [_dot_nt(qm[h], kj[h // 2]) for h in heads]
            dost = [(domt_ref[h] * jnp.exp(rst_ref[h // 2, pl.ds(j + HEAD_DIM * (h % 2), 1), :])).astype(BF16)
                    for h in heads]
            er = [jnp.exp(jnp.sum(jnp.where(lane == j + HEAD_DIM * (h % 2), rs_ref[h // 2], 0.0), axis=-1,
                                  keepdims=True)) for h in heads]
            dos = [(dom[h] * er[h]).astype(BF16) for h in heads]
            dw = [_dot_nt(dos[h], vj[h // 2]) for h in heads]
            psp = [_neg_softplus_parts(z[h]) for h in heads]
            sp = [psp[h][1] for h in heads]
            if diag:
                sp = [jnp.where(before, sp[h], 0.0) for h in heads]
            cin = [jnp.dot(_split_cat(sp[h], SCAN_PASSES), tneg, preferred_element_type=F32) for h in heads]
            w = [jnp.exp(z[h] + cin[h]) for h in heads]
            if diag:
                w = [jnp.where(before, w[h], 0.0) for h in heads]
            e =[dw[h] * w[h] for h in heads]
            eincl = [jnp.dot(_split_cat(e[h], SCAN_PASSES), tfwd, preferred_element_type=F32) + e_ref[h]
                     for h in heads]
            dz = []
            for h in heads:
                p = psp[h][0]
                beta = jnp.where(z[h] >= 0.0, 1.0, p) / (1.0 + p)
                d = e[h] - beta * eincl[h]
                dz.append((jnp.where(before, d, 0.0) if diag else d).astype(BF16))
            wb = [w[h].astype(BF16) for h in heads]
            for h in heads:
                e_ref[h] = eincl[h][:, bk - 1:bk]
                dq_acc[h] += jnp.dot(dz[h], kj[h // 2], preferred_element_type=F32)
            for pr in range(npairs):
                rows = slice(LANES * pr, LANES * (pr + 1))
                h0, h1 = 2 * pr, 2 * pr + 1
                dk_acc[rows, pl.ds(ks, bk)] += (jnp.dot(qmt_ref[h0], dz[h0], preferred_element_type=F32)
                                                 + jnp.dot(qmt_ref[h1], dz[h1], preferred_element_type=F32))
                dv_acc[rows, pl.ds(ks, bk)] += (jnp.dot(dost[h0], wb[h0], preferred_element_type=F32)
                                                 + jnp.dot(dost[h1], wb[h1], preferred_element_type=F32))

        def loop_body(j, carry):
            block(j, False)
            return carry

        block_of_lane = lane & (HEAD_DIM - 1)
        live = jnp.max(rs_ref[...], axis=0) > DEAD
        first_live = jnp.min(jnp.where(live, block_of_lane, nq * ratio))
        lax.fori_loop(jnp.minimum(first_live, i * ratio), i * ratio, loop_body, 0)
        for t in range(ratio):
            block(i * ratio + t, True)
        for pr in range(npairs):
            dq = jnp.where(lo_half, dq_acc[2 * pr], dq_acc[2 * pr + 1]) * scale
            dq_ref[:, LANES * pr:LANES * (pr + 1)] = dq.astype(BF16)

        @pl.when(i == nq - 1)
        def _():
            dk_ref[...] = dk_acc[...].T.astype(BF16)
            dv_ref[...] = dv_acc[...].T.astype(BF16)

        @pl.when(step == n_steps * nq - 1)
        def _():
            for fin in finish:
                fin()

    any_spec = pl.BlockSpec(memory_space=pl.ANY)
    return pl.pallas_call(
        body, name="attn_bwd", grid=(n_steps, nq),
        in_specs=[pl.BlockSpec((bq, width), lambda p, i: (i, n_steps * COL_Q + p)),
                  pl.BlockSpec((s, width), lambda p, i: (0, n_steps * COL_K + p)),
                  pl.BlockSpec((s, width), lambda p, i: (0, n_steps * COL_V + p)),
                  pl.BlockSpec((bq, width), lambda p, i: (i, p)),
                  pl.BlockSpec((npairs, bq, LANES), lambda p, i: (p, i, 0))] + [any_spec] * n_g,
        out_specs=[pl.BlockSpec((bq, width), lambda p, i: (i, p)),
                   pl.BlockSpec((s, width), lambda p, i: (0, p)),
                   pl.BlockSpec((s, width), lambda p, i: (0, p))] + [any_spec] * n_g,
        out_shape=[jax.ShapeDtypeStruct((s, D_BRANCH), BF16)] * 3
        + [jax.ShapeDtypeStruct((N_DEV,) + _GRAD_PIECE_SHAPES[a], BF16) for a, _ in plan],
        scratch_shapes=[pltpu.VMEM((width, s), F32), pltpu.VMEM((width, s), F32),
                        pltpu.VMEM((2 * npairs, bq, LANES), F32), pltpu.VMEM((2 * npairs, bq, 1), F32),
                        pltpu.VMEM((2 * npairs, LANES, bq), BF16), pltpu.VMEM((2 * npairs, LANES, bq), F32),
                        pltpu.VMEM((npairs, LANES, bq), F32)]
        + [sh for a, chips in plan for sh in _presum_scratch(a, chips)],
        compiler_params=pltpu.CompilerParams(dimension_semantics=("arbitrary", "arbitrary"),
                                             vmem_limit_bytes=VMEM_LIMIT, has_side_effects=True),
    )(proj, proj, proj, do, rsave, *grads)


def _group_avg_matrix():
    a = lax.broadcasted_iota(jnp.int32, (LANES, LANES), 0) >> GROUP_SHIFT
    b = lax.broadcasted_iota(jnp.int32, (LANES, LANES), 1) >> GROUP_SHIFT
    return jnp.where(a == b, 1.0 / GROUP_DIM, 0.0).astype(BF16)


def _group_mean(a, avg):
    parts = [_split_dot(a[:, LANES * k:LANES * (k + 1)], avg, 3) for k in range(D_BRANCH // LANES)]
    return jnp.concatenate(parts, axis=1)


def _sgu_forward_parts(ub, vb, ln_g, ln_b, avg):
    ug, dug = _gelu_and_grad(ub)
    vg, dvg = _gelu_and_grad(vb)
    mu = _group_mean(vg, avg)
    d = vg - mu
    var = _group_mean(d * d, avg)
    rstd = lax.rsqrt(var + EPS)
    vhat = d * rstd
    vn = vhat * ln_g + ln_b
    return ug, dug, dvg, rstd, vhat, vn


def _sgu_mix(w_ref, src_bf16, n_chunks):
    lane = lax.broadcasted_iota(jnp.int32, (SGU_CHUNK, LANES), 1)
    lo_half = lane < GROUP_DIM
    rows = []
    for n in range(n_chunks):
        slabs = []
        for a in range(D_BRANCH // LANES):
            blk = src_bf16[SGU_CHUNK * n:SGU_CHUNK * (n + 1), LANES * a:LANES * (a + 1)]
            zero = jnp.zeros_like(blk)
            m0 = jnp.dot(w_ref[2 * a], jnp.where(lo_half, blk, zero), preferred_element_type=F32)
            m1 = jnp.dot(w_ref[2 * a + 1], jnp.where(lo_half, zero, blk), preferred_element_type=F32)
            slabs.append(m0 + m1)
        rows.append(jnp.concatenate(slabs, axis=1))
    return jnp.concatenate(rows, axis=0)


def _sgu_fwd(proj, ln_g, ln_b, w_mask, bias_full):
    s = proj.shape[0]
    tm = min(512, s)
    n_chunks = tm // SGU_CHUNK

    def body(ub_ref, vb_ref, zb_ref, g_ref, b_ref, w_ref, bias_ref, yb_ref):
        avg = _group_avg_matrix()
        ug, _, _, _, _, vn = _sgu_forward_parts(ub_ref[...].astype(F32), vb_ref[...].astype(F32),
                                                g_ref[...], b_ref[...], avg)
        mixed = _sgu_mix(w_ref, vn.astype(BF16), n_chunks) + jnp.concatenate([bias_ref[...]] * n_chunks, axis=0)
        zb = zb_ref[...].astype(F32)
        yb_ref[...] = (ug * mixed * (zb * _sigmoid(zb))).astype(BF16)

    col = lambda c: pl.BlockSpec((tm, D_BRANCH), lambda i: (i, c))
    full = lambda shape: pl.BlockSpec(shape, lambda i: (0,) * len(shape))
    return pl.pallas_call(
        body, name="sgu_fwd", grid=(s // tm,),
        in_specs=[col(COL_UB), col(COL_VB), col(COL_ZB), full((1, D_BRANCH)), full((1, D_BRANCH)),
                  full((N_GROUPS, SGU_CHUNK, SGU_CHUNK)), full((SGU_CHUNK, D_BRANCH))],
        out_specs=pl.BlockSpec((tm, D_BRANCH), lambda i: (i, 0)),
        out_shape=jax.ShapeDtypeStruct((s, D_BRANCH), BF16),
        compiler_params=_cparams(("parallel",)),
    )(proj, proj, proj, ln_g, ln_b, w_mask, bias_full)


def _sgu_bwd(proj, dyb, ln_g, ln_b, w_mask, w_mask_t, bias_full):
    s = proj.shape[0]
    tm = min(512, s)
    n_chunks = tm // SGU_CHUNK
    n_steps = s // tm

    def body(ub_ref, vb_ref, zb_ref, dyb_ref, g_ref, b_ref, w_ref, wt_ref, bias_ref,
             dsgu_ref, dw_ref, db_ref, dg_ref, dbeta_ref, dmix_acc):
        i = pl.program_id(0)

        @pl.when(i == 0)
        def _():
            dw_ref[...] = jnp.zeros_like(dw_ref)
            dg_ref[...] = jnp.zeros_like(dg_ref)
            dbeta_ref[...] = jnp.zeros_like(dbeta_ref)
            dmix_acc[...] = jnp.zeros_like(dmix_acc)

        avg = _group_avg_matrix()
        ln_gv = g_ref[...]
        ug, dug, dvg, rstd, vhat, vn = _sgu_forward_parts(ub_ref[...].astype(F32), vb_ref[...].astype(F32),
                                                          ln_gv, b_ref[...], avg)
        vnb = vn.astype(BF16)
        mixed = _sgu_mix(w_ref, vnb, n_chunks) + jnp.concatenate([bias_ref[...]] * n_chunks, axis=0)
        zb = zb_ref[...].astype(F32)
        sg = _sigmoid(zb)
        sz = zb * sg
        dsz = sg * (1.0 + zb * (1.0 - sg))
        dy = dyb_ref[...].astype(F32)
        dmixed = dy * ug * sz
        du = dy * mixed * sz * dug
        dzb = dy * ug * mixed * dsz
        dmb = dmixed.astype(BF16)
        dvn = _sgu_mix(wt_ref, dmb, n_chunks)

        lane = lax.broadcasted_iota(jnp.int32, (SGU_CHUNK, LANES), 1)
        lo_half = lane < GROUP_DIM
        dm_sum = None
        for n in range(n_chunks):
            rows = slice(SGU_CHUNK * n, SGU_CHUNK * (n + 1))
            dm_sum = dmixed[rows] if dm_sum is None else dm_sum + dmixed[rows]
            for a in range(D_BRANCH // LANES):
                cols = slice(LANES * a, LANES * (a + 1))
                dblk = dmb[rows, cols]
                vblk = vnb[rows, cols]
                zero = jnp.zeros_like(dblk)
                dw_ref[2 * a] += _dot_nt(jnp.where(lo_half, dblk, zero), vblk)
                dw_ref[2 * a + 1] += _dot_nt(jnp.where(lo_half, zero, dblk), vblk)
        dmix_acc[...] += dm_sum

        dg_ref[...] += jnp.sum(dvn * vhat, axis=0, keepdims=True)
        dbeta_ref[...] += jnp.sum(dvn, axis=0, keepdims=True)
        dvh = dvn * ln_gv
        m1 = _group_mean(dvh, avg)
        m2 = _group_mean(dvh * vhat, avg)
        dv = rstd * (dvh - m1 - vhat * m2) * dvg
        dsgu_ref[:, 0:D_BRANCH] = du.astype(BF16)
        dsgu_ref[:, D_BRANCH:2 * D_BRANCH] = dv.astype(BF16)
        dsgu_ref[:, 2 * D_BRANCH:3 * D_BRANCH] = dzb.astype(BF16)

        @pl.when(i == n_steps - 1)
        def _():
            pos = lax.broadcasted_iota(jnp.int32, (SGU_CHUNK, SGU_CHUNK), 0) >> GROUP_SHIFT
            src = lax.broadcasted_iota(jnp.int32, (SGU_CHUNK, SGU_CHUNK), 1) >> GROUP_SHIFT
            keep = src <= pos
            for g in range(N_GROUPS):
                dw_ref[g] = jnp.where(keep, dw_ref[g], 0.0)
            grp = lax.broadcasted_iota(jnp.int32, (D_BRANCH, LANES), 0) >> GROUP_SHIFT
            sel = (grp == lax.broadcasted_iota(jnp.int32, (D_BRANCH, LANES), 1)).astype(BF16)
            db_ref[...] = _split_dot(dmix_acc[...], sel, 3)

    col = lambda c: pl.BlockSpec((tm, D_BRANCH), lambda i: (i, c))
    full = lambda shape: pl.BlockSpec(shape, lambda i: (0,) * len(shape))
    return pl.pallas_call(
        body, name="sgu_bwd", grid=(n_steps,),
        in_specs=[col(COL_UB), col(COL_VB), col(COL_ZB), pl.BlockSpec((tm, D_BRANCH), lambda i: (i, 0)),
                  full((1, D_BRANCH)), full((1, D_BRANCH)),
                  full((N_GROUPS, SGU_CHUNK, SGU_CHUNK)), full((N_GROUPS, SGU_CHUNK, SGU_CHUNK)),
                  full((SGU_CHUNK, D_BRANCH))],
        out_specs=[pl.BlockSpec((tm, 3 * D_BRANCH), lambda i: (i, 0)),
                   full((N_GROUPS, SGU_CHUNK, SGU_CHUNK)), full((SGU_CHUNK, LANES)),
                   full((1, D_BRANCH)), full((1, D_BRANCH))],
        out_shape=[jax.ShapeDtypeStruct((s, 3 * D_BRANCH), BF16),
                   jax.ShapeDtypeStruct((N_GROUPS, SGU_CHUNK, SGU_CHUNK), F32),
                   jax.ShapeDtypeStruct((SGU_CHUNK, LANES), F32),
                   jax.ShapeDtypeStruct((1, D_BRANCH), F32), jax.ShapeDtypeStruct((1, D_BRANCH), F32)],
        scratch_shapes=[pltpu.VMEM((SGU_CHUNK, D_BRANCH), F32)],
        compiler_params=_cparams(("arbitrary",)),
    )(proj, proj, proj, dyb, ln_g, ln_b, w_mask, w_mask_t, bias_full)


def _mid(proj, ya, yb, o, x, target, final_g, w_up_a, w_up_b, w_out):
    s = x.shape[0]
    tm = min(256, s)
    n_steps = s // tm
    half = D_MODEL // 2

    def body(ya_ref, yb_ref, o_ref, za_ref, ga0_ref, ga1_ref, gb0_ref, gb1_ref, x_ref, t_ref, gf_ref,
             wa_ref, wb_ref, wo_ref,
             dzg_ref, do_ref, dyb_ref, dx2_ref, gwo_ref, gwa_ref, gwb_ref, loss_ref, dgf_ref,
             acc_o, acc_a, acc_b):
        i = pl.program_id(0)

        @pl.when(i == 0)
        def _():
            acc_o[...] = jnp.zeros_like(acc_o)
            acc_a[...] = jnp.zeros_like(acc_a)
            acc_b[...] = jnp.zeros_like(acc_b)
            loss_ref[...] = jnp.zeros_like(loss_ref)
            dgf_ref[...] = jnp.zeros_like(dgf_ref)

        ya_v = ya_ref[...]
        yb_v = yb_ref[...]
        pa = jnp.dot(ya_v, wa_ref[...], preferred_element_type=F32)
        pb = jnp.dot(yb_v, wb_ref[...], preferred_element_type=F32)
        sa = _sigmoid(jnp.concatenate([ga0_ref[...], ga1_ref[...]], axis=1).astype(F32))
        sb = _sigmoid(jnp.concatenate([gb0_ref[...], gb1_ref[...]], axis=1).astype(F32))
        merged = (sa * pa + sb * pb).astype(BF16)
        x2 = x_ref[...] + jnp.dot(merged, wo_ref[...], preferred_element_type=F32)
        r2 = lax.rsqrt(jnp.mean(x2 * x2, axis=-1, keepdims=True) + EPS)
        xh = x2 * r2
        gf = gf_ref[...]
        diff = xh * gf - t_ref[...]
        loss_ref[...] += 0.5 * jnp.sum(jnp.mean(diff * diff, axis=-1, keepdims=True))
        dy = diff * (1.0 / D_MODEL)
        dgf_ref[...] += jnp.sum(dy * xh, axis=0, keepdims=True)
        dyg = dy * gf
        dx2 = r2 * (dyg - xh * jnp.mean(dyg * xh, axis=-1, keepdims=True))
        dx2_ref[...] = dx2
        dx2b = dx2.astype(BF16)
        dmerged = _dot_nt(dx2b, wo_ref[...])
        acc_o[...] += _dot_tn(merged, dx2b)
        dpa = dmerged * sa
        dpb = dmerged * sb
        dzg_ref[:, D_BRANCH:D_BRANCH + D_MODEL] = (dpa * pa * (1.0 - sa)).astype(BF16)
        dzg_ref[:, D_BRANCH + D_MODEL:D_BRANCH + 2 * D_MODEL] = (dpb * pb * (1.0 - sb)).astype(BF16)
        dpab = dpa.astype(BF16)
        dpbb = dpb.astype(BF16)
        acc_a[...] += _dot_tn(ya_v, dpab)
        acc_b[...] += _dot_tn(yb_v, dpbb)
        dya = _dot_nt(dpab, wa_ref[...])
        dyb_ref[...] = _dot_nt(dpbb, wb_ref[...]).astype(BF16)
        za = za_ref[...].astype(F32)
        sg = _sigmoid(za)
        do_ref[...] = (dya * (za * sg)).astype(BF16)
        dzg_ref[:, 0:D_BRANCH] = (dya * o_ref[...].astype(F32) * (sg * (1.0 + za * (1.0 - sg)))).astype(BF16)

        @pl.when(i == n_steps - 1)
        def _():
            gwo_ref[...] = acc_o[...].astype(BF16)
            gwa_ref[...] = acc_a[...].astype(BF16)
            gwb_ref[...] = acc_b[...].astype(BF16)

    tok = lambda w: pl.BlockSpec((tm, w), lambda i: (i, 0))
    col = lambda c: pl.BlockSpec((tm, half), lambda i: (i, c))
    full = lambda shape: pl.BlockSpec(shape, lambda i: (0,) * len(shape))
    return pl.pallas_call(
        body, name="mid", grid=(n_steps,),
        in_specs=[tok(D_BRANCH), tok(D_BRANCH), tok(D_BRANCH), col(COL_ZA), col(COL_GA), col(COL_GA + 1),
                  col(COL_GB), col(COL_GB + 1), tok(D_MODEL), tok(D_MODEL), full((1, D_MODEL)),
                  full((D_BRANCH, D_MODEL)), full((D_BRANCH, D_MODEL)), full((D_MODEL, D_MODEL))],
        out_specs=[tok(D_BRANCH + 2 * D_MODEL), tok(D_BRANCH), tok(D_BRANCH), tok(D_MODEL),
                   full((D_MODEL, D_MODEL)), full((D_BRANCH, D_MODEL)), full((D_BRANCH, D_MODEL)),
                   full((8, LANES)), full((1, D_MODEL))],
        out_shape=[jax.ShapeDtypeStruct((s, D_BRANCH + 2 * D_MODEL), BF16),
                   jax.ShapeDtypeStruct((s, D_BRANCH), BF16), jax.ShapeDtypeStruct((s, D_BRANCH), BF16),
                   jax.ShapeDtypeStruct((s, D_MODEL), F32),
                   jax.ShapeDtypeStruct((D_MODEL, D_MODEL), BF16),
                   jax.ShapeDtypeStruct((D_BRANCH, D_MODEL), BF16), jax.ShapeDtypeStruct((D_BRANCH, D_MODEL), BF16),
                   jax.ShapeDtypeStruct((8, LANES), F32), jax.ShapeDtypeStruct((1, D_MODEL), F32)],
        scratch_shapes=[pltpu.VMEM((D_MODEL, D_MODEL), F32), pltpu.VMEM((D_BRANCH, D_MODEL), F32),
                        pltpu.VMEM((D_BRANCH, D_MODEL), F32)],
        compiler_params=_cparams(("arbitrary",)),
    )(ya, yb, o, proj, proj, proj, proj, proj, x, target, final_g, w_up_a, w_up_b, w_out)


def _dwin_piece(ht, piece, tile_of, prev):
    s = ht.shape[1]
    n_tiles = piece.shape[1] // D_BRANCH

    def body(ht_ref, p_ref, *rest):
        out_ref = rest[-1]
        out_ref[...] = jnp.dot(ht_ref[...], p_ref[...], preferred_element_type=F32).astype(BF16)

    in_specs = [pl.BlockSpec((D_MODEL, s), lambda j: (0, 0)), pl.BlockSpec((s, D_BRANCH), lambda j: (0, j))]
    args = [ht, piece]
    aliases = {}
    if prev is not None:
        in_specs.append(pl.BlockSpec(memory_space=pl.ANY))
        args.append(prev)
        aliases = {2: 0}
    return pl.pallas_call(
        body, name="dwin_piece", grid=(n_tiles,),
        in_specs=in_specs,
        out_specs=pl.BlockSpec((D_MODEL, D_BRANCH), lambda j: (0, tile_of(j))),
        out_shape=jax.ShapeDtypeStruct((D_MODEL, D_IN), BF16),
        input_output_aliases=aliases,
        compiler_params=_cparams(("parallel",)),
    )(*args)


def _dh_dx(pieces, w_in, x, norm_g, dx2, g_in):
    s = x.shape[0]
    tm = min(256, s)
    n_steps = s // tm
    arrays = []
    for arr, _, _, _ in pieces:
        if not any(arr is a for a in arrays):
            arrays.append(arr)
    n_arr = len(arrays)
    plan = [([k for k, a in enumerate(arrays) if a is arr][0], wcol, off, width) for arr, wcol, off, width in pieces]

    def body(*refs):
        p_refs = refs[:n_arr]
        w_ref, x_ref, g_ref, dx2_ref, gin_ref, dx_ref, dg_ref, late_ref = refs[n_arr:n_arr + 8]
        step = pl.program_id(0)
        finish = _presum_program(step == 0, step == min(1, n_steps - 1), step == n_steps - 1, 0, _LATE_CHIPS,
                                 gin_ref, late_ref, refs[n_arr + 8:])

        @pl.when(step == 0)
        def _():
            dg_ref[...] = jnp.zeros_like(dg_ref)

        dh = None
        for k, wcol, off, width in plan:
            d = _dot_nt(p_refs[k][:, off:off + width], w_ref[:, wcol:wcol + width])
            dh = d if dh is None else dh + d
        xf = x_ref[...]
        r = lax.rsqrt(jnp.mean(xf * xf, axis=-1, keepdims=True) + EPS)
        xh = xf * r
        dg_ref[...] += jnp.sum(dh * xh, axis=0, keepdims=True)
        dhg = dh * g_ref[...]
        dx_ref[...] = r * (dhg - xh * jnp.mean(dhg * xh, axis=-1, keepdims=True)) + dx2_ref[...]

        @pl.when(step == n_steps - 1)
        def _():
            finish()

    tok = lambda w: pl.BlockSpec((tm, w), lambda i: (i, 0))
    full = lambda shape: pl.BlockSpec(shape, lambda i: (0,) * len(shape))
    any_spec = pl.BlockSpec(memory_space=pl.ANY)
    return pl.pallas_call(
        body, name="dh_dx", grid=(n_steps,),
        in_specs=[tok(a.shape[1]) for a in arrays] + [full((D_MODEL, D_IN)), tok(D_MODEL), full((1, D_MODEL)),
                                                      tok(D_MODEL), any_spec],
        out_specs=[tok(D_MODEL), full((1, D_MODEL)), any_spec],
        out_shape=[jax.ShapeDtypeStruct((s, D_MODEL), F32), jax.ShapeDtypeStruct((1, D_MODEL), F32),
                   jax.ShapeDtypeStruct((N_DEV,) + _GRAD_PIECE_SHAPES[0], BF16)],
        scratch_shapes=_presum_scratch(0, _LATE_CHIPS),
        compiler_params=pltpu.CompilerParams(dimension_semantics=("arbitrary",), vmem_limit_bytes=VMEM_LIMIT,
                                             has_side_effects=True),
    )(*arrays, w_in, x, norm_g, dx2, g_in)


def _adamw(w, g, m, v):
    rows, cols = w.shape
    tr = max(t for t in range(8, 257, 8) if rows % t == 0)
    c1 =1.0 - ADAM_B1 ** ADAM_STEP
    c2 = 1.0 - ADAM_B2 ** ADAM_STEP

    def body(w_ref, g_ref, m_ref, v_ref, g_out_ref, d_ref, nm_ref, nv_ref):
        gv = g_ref[...]
        g_out_ref[...] = gv
        nm = ADAM_B1 * m_ref[...] + (1.0 - ADAM_B1) * gv
        nv = ADAM_B2 * v_ref[...] + (1.0 - ADAM_B2) * (gv * gv)
        d_ref[...] = -ADAM_LR * ((nm / c1) / (jnp.sqrt(nv / c2) + ADAM_EPS) + ADAM_WD * w_ref[...])
        nm_ref[...] = nm
        nv_ref[...] = nv

    spec = pl.BlockSpec((tr, cols), lambda i: (i, 0))
    return pl.pallas_call(
        body, name="adamw", grid=(rows // tr,),
        in_specs=[spec] * 4, out_specs=[spec] * 4,
        out_shape=[jax.ShapeDtypeStruct((rows, cols), F32)] * 4,
        compiler_params=_cparams(("parallel",)),
    )(w, g, m, v)


def _place():
    x, y, c = lax.axis_index("x"), lax.axis_index("y"), lax.axis_index("c")
    return x, y, c


def _gather_weights(w_in, w_up_a, w_up_b, w_out):
    shards = (w_in, w_up_a, w_up_b, w_out)
    n_arr = len(shards)
    col_sharded = (True, True, True, False)
    full_shapes = ((D_MODEL, D_IN), (D_BRANCH, D_MODEL), (D_BRANCH, D_MODEL), (D_MODEL, D_MODEL))

    def body(*refs):
        src = refs[:n_arr]
        out = refs[n_arr:2 * n_arr]
        stage = refs[2 * n_arr:3 * n_arr]
        cast = refs[3 * n_arr:4 * n_arr]
        send_sems, recv_sems, local_sems = refs[4 * n_arr:]
        x, y, c = _place()
        chip = 2 * x + y
        sibling = (x, y, 1 - c)
        others = [(1 - x, y), (x, 1 - y), (1 - x, 1 - y)]

        def region(a, chip_idx, half):
            r, w = shards[a].shape
            hr = r // 2
            if col_sharded[a]:
                return out[a].at[pl.ds(_aligned(half * hr, 16), hr), pl.ds(_aligned(chip_idx * w, LANES), w)]
            return out[a].at[pl.ds(_aligned(chip_idx * r + half * hr, 16), hr), :]

        loads = [pltpu.make_async_copy(src[a], stage[a], local_sems.at[a]) for a in range(n_arr)]
        for cp in loads:
            cp.start()
        for a in range(n_arr):
            loads[a].wait()
            cast[a][...] = stage[a][...].astype(BF16)
        stores = []
        for a in range(n_arr):
            hr = shards[a].shape[0] // 2
            for half in range(2):
                cp = pltpu.make_async_copy(cast[a].at[pl.ds(half * hr, hr), :], region(a, chip, half),
                                           local_sems.at[n_arr + 2 * a + half])
                cp.start()
                stores.append(cp)

        def remote(k, a, chip_idx, half, to, from_vmem):
            hr = shards[a].shape[0] // 2
            s_ref = cast[a].at[pl.ds(_aligned(half * hr, 16), hr), :] if from_vmem else region(a, chip_idx, half)
            return pltpu.make_async_remote_copy(src_ref=s_ref, dst_ref=region(a, chip_idx, half),
                                                send_sem=send_sems.at[k], recv_sem=recv_sems.at[k],
                                                device_id=to, device_id_type=MESH)

        first = []
        for j, (ox, oy) in enumerate(others):
            for a in range(n_arr):
                cp = remote(n_arr * j + a, a, chip, c, (ox, oy, c), True)
                cp.start()
                first.append(cp)
        passed = []
        for j, (ox, oy) in enumerate(others):
            ochip = 2 * ox + oy
            for a in range(n_arr):
                k = n_arr * j + a
                remote(k, a, ochip, c, sibling, False).wait_recv()
                cp = remote(3 * n_arr + k, a, ochip, c, sibling, False)
                cp.start()
                passed.append(cp)
        for j, (ox, oy) in enumerate(others):
            ochip = 2 * ox + oy
            for a in range(n_arr):
                remote(3 * n_arr + n_arr * j + a, a, ochip, 1 - c, sibling, False).wait_recv()
        for cp in first + passed:
            cp.wait_send()
        for cp in stores:
            cp.wait()

    any_spec = pl.BlockSpec(memory_space=pl.ANY)
    return pl.pallas_call(
        body, name="gather_weights",
        in_specs=[any_spec] * n_arr, out_specs=[any_spec] * n_arr,
        out_shape=[jax.ShapeDtypeStruct(sh, BF16) for sh in full_shapes],
        scratch_shapes=[pltpu.VMEM(a.shape, F32) for a in shards] + [pltpu.VMEM(a.shape, BF16) for a in shards]
        + [pltpu.SemaphoreType.DMA((6 * n_arr,)), pltpu.SemaphoreType.DMA((6 * n_arr,)),
           pltpu.SemaphoreType.DMA((3 * n_arr,))],
        compiler_params=pltpu.CompilerParams(vmem_limit_bytes=VMEM_LIMIT, has_side_effects=True),
    )(*shards)


def _reduce_grads_tail(grads, g_small, early_slots, late_in_slots):
    n_big = len(grads)
    n_arr = n_big + 1
    shard_shapes = [(2 * r, w) for r, w in _GRAD_PIECE_SHAPES]
    small_piece = (SMALL_PIECE, LANES)

    def body(*refs):
        src = refs[:n_arr]
        early = refs[n_arr:n_arr + n_big]
        late_in = refs[n_arr + n_big]
        n_in = n_arr + n_big + 1
        out = refs[n_in:n_in + n_arr]
        slots = refs[n_in + n_arr:n_in + 2 * n_arr]
        sums = refs[n_in + 2 * n_arr:n_in + 3 * n_arr]
        send1, recv1, send2, recv2, local_sems = refs[n_in + 3 * n_arr:]
        x, y, c = _place()
        me = 4 * x + 2 * y + c

        def piece_of(a, dev):
            return src[a].at[dev] if a == n_big else _grad_piece(src[a], a, dev)

        def late(a, dst_dev, src_dev):
            return pltpu.make_async_remote_copy(
                src_ref=piece_of(a, dst_dev), dst_ref=slots[a].at[src_dev],
                send_sem=send1.at[n_arr * dst_dev + a], recv_sem=recv1.at[n_arr * src_dev + a],
                device_id=_dev_id(dst_dev), device_id_type=MESH)

        def late_arrays(dev):
            return (n_big,)

        def load(a, dev, received):
            return pltpu.make_async_copy(received.at[dev], slots[a].at[dev], local_sems.at[n_arr * dev + a])

        def own(a, dev):
            return pltpu.make_async_copy(piece_of(a, dev), slots[a].at[dev], local_sems.at[n_arr * dev + a])

        for dev in range(N_DEV):
            @pl.when(me == dev)
            def _():
                received = [early[0] if dev in _EARLY_IN_DEVS else late_in] + list(early[1:])
                for a in range(n_arr):
                    own(a, dev).start()
                filled = lambda a, peer: peer in _presum_sources(dev)
                for peer in range(N_DEV):
                    if peer != dev:
                        for a in late_arrays(peer):
                            late(a, peer, dev).start()
                        for a in range(n_big):
                            if filled(a, peer):
                                load(a, peer, received[a]).start()
                            else:
                                slots[a][peer] = jnp.zeros(slots[a].shape[1:], slots[a].dtype)
                for a in range(n_arr):
                    own(a, dev).wait()
                for peer in range(N_DEV):
                    if peer != dev:
                        for a in late_arrays(dev):
                            late(a, dev, peer).wait_recv()
                        for a in range(n_big):
                            if filled(a, peer):
                                load(a, peer, received[a]).wait()

        for a in range(n_arr):
            rows = slots[a].shape[1]
            step = 64 if rows % 64 == 0 else 8

            def add_rows(t, carry, a=a, step=step):
                r0 = pl.multiple_of(t * step, step)
                total = slots[a][0, pl.ds(r0, step), :].astype(F32)
                for dev in range(1, N_DEV):
                    total = total + slots[a][dev, pl.ds(r0, step), :].astype(F32)
                sums[a][pl.ds(r0, step), :] = total
                return carry

            lax.fori_loop(0, rows // step, add_rows, 0)

        shares = []
        keeps = []
        for a in range(n_big):
            r, w = _GRAD_PIECE_SHAPES[a]
            dst = out[a].at[pl.ds(pl.multiple_of(c * r, 8), r), :]
            cp = pltpu.make_async_remote_copy(src_ref=sums[a], dst_ref=dst, send_sem=send2.at[a], recv_sem=recv2.at[a],
                                              device_id=(x, y, 1 - c), device_id_type=MESH)
            cp.start()
            shares.append(cp)
            kp = pltpu.make_async_copy(sums[a], dst, local_sems.at[N_DEV * n_arr + a])
            kp.start()
            keeps.append(kp)
        kp = pltpu.make_async_copy(sums[n_big], out[n_big].at[me], local_sems.at[N_DEV * n_arr + n_big])
        kp.start()
        keeps.append(kp)

        def small_share(dst_dev, src_dev):
            return pltpu.make_async_remote_copy(src_ref=sums[n_big], dst_ref=out[n_big].at[src_dev],
                                                send_sem=send2.at[n_big + dst_dev], recv_sem=recv2.at[n_big + src_dev],
                                                device_id=_dev_id(dst_dev), device_id_type=MESH)

        for dev in range(N_DEV):
            @pl.when(me != dev)
            def _():
                small_share(dev, me).start()
        for a in range(n_big):
            r, w = _GRAD_PIECE_SHAPES[a]
            other = out[a].at[pl.ds(pl.multiple_of((1 - c) * r, 8), r), :]
            pltpu.make_async_remote_copy(src_ref=sums[a], dst_ref=other, send_sem=send2.at[a], recv_sem=recv2.at[a],
                                         device_id=(x, y, 1 - c), device_id_type=MESH).wait_recv()
        for dev in range(N_DEV):
            @pl.when(me != dev)
            def _():
                small_share(dev, dev).wait_recv()
                small_share(dev, me).wait_send()
                for a in late_arrays(dev):
                    late(a, dev, me).wait_send()
        for cp in shares:
            cp.wait_send()
        for kp in keeps:
            kp.wait()

    any_spec = pl.BlockSpec(memory_space=pl.ANY)
    return pl.pallas_call(
        body, name="reduce_grads_tail",
        in_specs=[any_spec] * (n_arr + n_big + 1), out_specs=[any_spec] * n_arr,
        out_shape=[jax.ShapeDtypeStruct(sh, F32) for sh in shard_shapes]
        + [jax.ShapeDtypeStruct((N_DEV,) + small_piece, F32)],
        scratch_shapes=[pltpu.VMEM((N_DEV,) + sh, BF16) for sh in _GRAD_PIECE_SHAPES]
        + [pltpu.VMEM((N_DEV,) + small_piece, F32)]
        + [pltpu.VMEM(sh, F32) for sh in _GRAD_PIECE_SHAPES] + [pltpu.VMEM(small_piece, F32)]
        + [pltpu.SemaphoreType.DMA((N_DEV * n_arr,)), pltpu.SemaphoreType.DMA((N_DEV * n_arr,)),
           pltpu.SemaphoreType.DMA((n_big + N_DEV,)), pltpu.SemaphoreType.DMA((n_big + N_DEV,)),
           pltpu.SemaphoreType.DMA((N_DEV * n_arr + n_arr,))],
        compiler_params=pltpu.CompilerParams(vmem_limit_bytes=VMEM_LIMIT, has_side_effects=True),
    )(*grads, g_small, *early_slots, late_in_slots)


def _reduce_grads(g_in, g_up_a, g_up_b, g_out, g_small):
    big = (g_in, g_up_a, g_up_b, g_out)
    n_big = len(big)
    col_sharded = (True, True, True, False)
    piece_shapes = []
    for a, arr in enumerate(big):
        r, w = arr.shape
        piece_shapes.append((r // 2, w // N_CHIPS) if col_sharded[a] else (r // (2 * N_CHIPS), w))
    shard_shapes = [(2 * r, w) for r, w in piece_shapes]
    n_arr = n_big + 1

    def body(*refs):
        src = refs[:n_arr]
        out = refs[n_arr:2 * n_arr]
        slots = refs[2 * n_arr:3 * n_arr]
        sums = refs[3 * n_arr:4 * n_arr]
        send1, recv1, send2, recv2, local_sems = refs[4 * n_arr:]
        x, y, c = _place()
        me = 4 * x + 2 * y + c

        def piece_of(a, dev):
            chip_idx, half = dev // 2, dev % 2
            if a == n_big:
                return src[a].at[dev]
            r, w = piece_shapes[a]
            if col_sharded[a]:
                return src[a].at[pl.ds(pl.multiple_of(half * r, 16), r), pl.ds(pl.multiple_of(chip_idx * w, LANES), w)]
            return src[a].at[pl.ds(pl.multiple_of(dev * r, 16), r), :]

        def dev_id(dev):
            return (dev // 4, (dev // 2) % 2, dev % 2)

        own = [pltpu.make_async_copy(piece_of(a, me), slots[a].at[me], local_sems.at[a]) for a in range(n_arr)]
        for cp in own:
            cp.start()
        sends = []
        for d in range(1, N_DEV):
            peer = (me + d) % N_DEV
            for a in range(n_arr):
                cp = pltpu.make_async_remote_copy(
                    src_ref=piece_of(a, peer), dst_ref=slots[a].at[me],
                    send_sem=send1.at[n_arr * peer + a], recv_sem=recv1.at[n_arr * me + a],
                    device_id=dev_id(peer), device_id_type=MESH)
                cp.start()
                sends.append(cp)
        for cp in own:
            cp.wait()
        for d in range(1, N_DEV):
            peer = (me + d) % N_DEV
            for a in range(n_arr):
                pltpu.make_async_remote_copy(
                    src_ref=piece_of(a, peer), dst_ref=slots[a].at[peer],
                    send_sem=send1.at[n_arr * peer + a], recv_sem=recv1.at[n_arr * peer + a],
                    device_id=dev_id(peer), device_id_type=MESH).wait_recv()
        for a in range(n_arr):
            rows = slots[a].shape[1]
            step = 64 if rows % 64 == 0 else 8

            def add_rows(t, carry, a=a, step=step):
                r0 = pl.multiple_of(t * step, step)
                total = slots[a][0, pl.ds(r0, step), :].astype(F32)
                for dev in range(1, N_DEV):
                    total = total + slots[a][dev, pl.ds(r0, step), :].astype(F32)
                sums[a][pl.ds(r0, step), :] = total
                return carry

            lax.fori_loop(0, rows // step, add_rows, 0)
        shares = []
        keeps = []
        for a in range(n_big):
            r, w = piece_shapes[a]
            dst = out[a].at[pl.ds(pl.multiple_of(c * r, 8), r), :]
            cp = pltpu.make_async_remote_copy(src_ref=sums[a], dst_ref=dst, send_sem=send2.at[a], recv_sem=recv2.at[a],
                                              device_id=(x, y, 1 - c), device_id_type=MESH)
            cp.start()
            shares.append(cp)
            kp = pltpu.make_async_copy(sums[a], dst, local_sems.at[n_arr + a])
            kp.start()
            keeps.append(kp)
        kp = pltpu.make_async_copy(sums[n_big], out[n_big].at[me], local_sems.at[n_arr + n_big])
        kp.start()
        keeps.append(kp)
        for d in range(1, N_DEV):
            peer = (me + d) % N_DEV
            cp = pltpu.make_async_remote_copy(src_ref=sums[n_big], dst_ref=out[n_big].at[me],
                                              send_sem=send2.at[n_big + peer], recv_sem=recv2.at[n_big + me],
                                              device_id=dev_id(peer), device_id_type=MESH)
            cp.start()
            shares.append(cp)
        for a in range(n_big):
            r, w = piece_shapes[a]
            other = out[a].at[pl.ds(pl.multiple_of((1 - c) * r, 8), r), :]
            pltpu.make_async_remote_copy(src_ref=sums[a], dst_ref=other, send_sem=send2.at[a], recv_sem=recv2.at[a],
                                         device_id=(x, y, 1 - c), device_id_type=MESH).wait_recv()
        for d in range(1, N_DEV):
            peer = (me + d) % N_DEV
            pltpu.make_async_remote_copy(src_ref=sums[n_big], dst_ref=out[n_big].at[peer],
                                         send_sem=send2.at[n_big + peer], recv_sem=recv2.at[n_big + peer],
                                         device_id=dev_id(peer), device_id_type=MESH).wait_recv()
        for cp in sends + shares:
            cp.wait_send()
        for kp in keeps:
            kp.wait()

    any_spec = pl.BlockSpec(memory_space=pl.ANY)
    small_piece = (SMALL_PIECE, LANES)
    return pl.pallas_call(
        body, name="reduce_grads",
        in_specs=[any_spec] * n_arr, out_specs=[any_spec] * n_arr,
        out_shape=[jax.ShapeDtypeStruct(sh, F32) for sh in shard_shapes]
        + [jax.ShapeDtypeStruct((N_DEV,) + small_piece, F32)],
        scratch_shapes=[pltpu.VMEM((N_DEV,) + sh, BF16) for sh in piece_shapes]
        + [pltpu.VMEM((N_DEV,) + small_piece, F32)]
        + [pltpu.VMEM(sh, F32) for sh in piece_shapes] + [pltpu.VMEM(small_piece, F32)]
        + [pltpu.SemaphoreType.DMA((N_DEV * n_arr,)), pltpu.SemaphoreType.DMA((N_DEV * n_arr,)),
           pltpu.SemaphoreType.DMA((n_big + N_DEV,)), pltpu.SemaphoreType.DMA((n_big + N_DEV,)),
           pltpu.SemaphoreType.DMA((2 * n_arr,))],
        compiler_params=pltpu.CompilerParams(vmem_limit_bytes=VMEM_LIMIT, has_side_effects=True),
    )(*big, g_small)


_SMALL_PARTS = (("norm_g", 8), ("sgu_ln_g", 8), ("sgu_ln_b", 8), ("w_spatial", 1024), ("b_spatial", 8),
                ("final_norm_g", 8))
_LOSS_ROW = sum(n for _, n in _SMALL_PARTS)


def _pack_small(parts, loss_tile=None):
    rows = []
    for name, n_rows in _SMALL_PARTS:
        a = parts[name].reshape(-1, LANES).astype(F32)
        a = jnp.pad(a, ((0, n_rows - a.shape[0]), (0, 0)))
        rows.append(a)
    rows.append(jnp.zeros((8, LANES), F32) if loss_tile is None else loss_tile)
    rows.append(jnp.zeros((SMALL_ROWS - _LOSS_ROW - 8, LANES), F32))
    return jnp.concatenate(rows, axis=0)


def _unpack_small(packed, shapes):
    out = {}
    r0 = 0
    for name, n_rows in _SMALL_PARTS:
        n = math.prod(shapes[name])
        out[name] = packed[r0:r0 + n // LANES].reshape(shapes[name])
        r0 += n_rows
    return out


def _local_step(proj, ht, x, target, norm_g, w_in, sgu_ln_g, sgu_ln_b, w_spatial, b_spatial, w_up_a, w_up_b, w_out,
                final_norm_g, bq, bk):
    pos = jnp.arange(SGU_CHUNK)
    keep = (pos[None, :] // SGU_SUBCHUNK) <= (pos[:, None] // SGU_SUBCHUNK)
    w_mask = jnp.where(keep[None], w_spatial, 0.0).astype(BF16)
    w_mask_t = jnp.swapaxes(w_mask, 1, 2)
    bias_full = jnp.repeat(b_spatial.T, GROUP_DIM, axis=1)
    ln_g = sgu_ln_g.reshape(1, D_BRANCH)
    ln_b = sgu_ln_b.reshape(1, D_BRANCH)
    final_g = final_norm_g.reshape(1, D_MODEL)

    o, ya, rsave = _attn_fwd(proj, bq, bk, ATTN_PAIRS)
    yb = _sgu_fwd(proj, ln_g, ln_b, w_mask, bias_full)
    dzg, do, dyb, dx2, g_out, g_up_a, g_up_b, loss_acc, d_final = _mid(
        proj, ya, yb, o, x, target, final_g, w_up_a, w_up_b, w_out)
    dsgu, d_wsp, d_bsp, d_lng, d_lnb = _sgu_bwd(proj, dyb, ln_g, ln_b, w_mask, w_mask_t, bias_full)
    g_in = _dwin_piece(ht, dzg, lambda j: jnp.where(j == 0, COL_ZA, COL_GA - 1 + j), None)
    g_in = _dwin_piece(ht, dsgu, lambda j: COL_UB + j, g_in)
    dq, dk, dv, *early_slots = _attn_bwd(proj, do, rsave, bq, bk, ATTN_PAIRS, (g_in, g_up_a, g_up_b, g_out))
    g_in = _dwin_piece(ht, dq, lambda j: COL_Q + j, g_in)
    g_in = _dwin_piece(ht, dk, lambda j: COL_K + j, g_in)
    g_in = _dwin_piece(ht, dv, lambda j: COL_V + j, g_in)
    pieces = [(dq, COL_Q * D_BRANCH, 0, D_BRANCH), (dk, COL_K * D_BRANCH, 0, D_BRANCH),
              (dv, COL_V * D_BRANCH, 0, D_BRANCH), (dzg, COL_ZA * D_BRANCH, 0, D_BRANCH),
              (dsgu, COL_UB * D_BRANCH, 0, 3 * D_BRANCH), (dzg, COL_GA * D_BRANCH, D_BRANCH, 2 * D_MODEL)]
    dx, d_norm, late_in_slots = _dh_dx(pieces, w_in, x, norm_g, dx2, g_in)
    small = {"norm_g": d_norm, "sgu_ln_g": d_lng, "sgu_ln_b": d_lnb, "w_spatial": d_wsp,
             "b_spatial": d_bsp[:, :N_GROUPS].T, "final_norm_g": d_final}
    return loss_acc, dx, (g_in, g_up_a, g_up_b, g_out), small, early_slots, late_in_slots


def kernel(x, norm_g, w_in, sgu_ln_g, sgu_ln_b, w_spatial, b_spatial, w_up_a, w_up_b, w_out, final_norm_g, loss_target, m_norm_g, m_w_in, m_sgu_ln_g, m_sgu_ln_b, m_w_spatial, m_b_spatial, m_w_up_a, m_w_up_b, m_w_out, m_final_norm_g, v_norm_g, v_w_in, v_sgu_ln_g, v_sgu_ln_b, v_w_spatial, v_b_spatial, v_w_up_a, v_w_up_b, v_w_out, v_final_norm_g):
    big_names = ("w_in", "w_up_a", "w_up_b", "w_out")
    small_names = tuple(n for n, _ in _SMALL_PARTS)
    names = ("norm_g", "w_in", "sgu_ln_g", "sgu_ln_b", "w_spatial", "b_spatial", "w_up_a", "w_up_b", "w_out",
             "final_norm_g")
    w = dict(norm_g=norm_g, w_in=w_in, sgu_ln_g=sgu_ln_g, sgu_ln_b=sgu_ln_b, w_spatial=w_spatial,
             b_spatial=b_spatial, w_up_a=w_up_a, w_up_b=w_up_b, w_out=w_out, final_norm_g=final_norm_g)
    m = dict(norm_g=m_norm_g, w_in=m_w_in, sgu_ln_g=m_sgu_ln_g, sgu_ln_b=m_sgu_ln_b, w_spatial=m_w_spatial,
             b_spatial=m_b_spatial, w_up_a=m_w_up_a, w_up_b=m_w_up_b, w_out=m_w_out, final_norm_g=m_final_norm_g)
    v = dict(norm_g=v_norm_g, w_in=v_w_in, sgu_ln_g=v_sgu_ln_g, sgu_ln_b=v_sgu_ln_b, w_spatial=v_w_spatial,
             b_spatial=v_b_spatial, w_up_a=v_w_up_a, w_up_b=v_w_up_b, w_out=v_w_out, final_norm_g=v_final_norm_g)
    shapes = {n: w[n].shape for n in names}
    flat2d = lambda a: a.reshape(a.shape[-2:])

    proj, ht, *full = _in_proj_gather(x[0], norm_g, *[flat2d(w[n]) for n in big_names])
    loss, dx, big_grads, small, early_slots, late_in_slots = _local_step(
        proj, ht, x[0], loss_target[0], norm_g, full[0], sgu_ln_g[0], sgu_ln_b[0], w_spatial[0], b_spatial[0],
        full[1], full[2], full[3], final_norm_g, ATTN_Q_BLOCK, ATTN_K_BLOCK)
    packed = _pack_small(small, loss).reshape(N_DEV, SMALL_PIECE, LANES)
    red = _reduce_grads_tail(big_grads, packed, early_slots, late_in_slots)

    grads, deltas, new_m, new_v = {}, {}, {}, {}
    for n, g in zip(big_names, red[:4]):
        g, d, nm, nv = _adamw(flat2d(w[n]), g, flat2d(m[n]), flat2d(v[n]))
        grads[n], deltas[n], new_m[n], new_v[n] = (a.reshape(shapes[n]) for a in (g, d, nm, nv))
    g_small = red[4].reshape(SMALL_ROWS, LANES)
    g_small, d, nm, nv = _adamw(_pack_small({n: w[n] for n in small_names}), g_small,
                                _pack_small({n: m[n] for n in small_names}),
                                _pack_small({n: v[n] for n in small_names}))
    for src, dst in ((g_small, grads), (d, deltas), (nm, new_m), (nv, new_v)):
        dst.update(_unpack_small(src, shapes))

    return (g_small[_LOSS_ROW, 0], dx[None], *[grads[n] for n in names], *[deltas[n] for n in names],
            *[new_m[n] for n in names], *[new_v[n] for n in names])
```

```python
import functools
import math

import jax
import jax.numpy as jnp
from jax import lax
from jax.experimental import pallas as pl
from jax.experimental.pallas import tpu as pltpu

F32 = jnp.float32
BF16 = jnp.bfloat16

D_MODEL = 1024
N_HEADS = 8
HEAD_DIM = 64
D_BRANCH = 512
D_IN = 4 * D_BRANCH + 3 * D_BRANCH + 2 * D_MODEL
N_GROUPS = 8
GROUP_DIM = 64
SGU_CHUNK = 128
SGU_SUBCHUNK = 64
GROUP_SHIFT = 6
EPS = 1e-6
LANES = 128
ATTN_Q_BLOCK = 256
ATTN_K_BLOCK = 256
DEAD = -110.0
SKIPPED = -1e30
SCAN_PASSES = 1
ATTN_PAIRS = 2
N_CHIPS = 4
N_DEV = 8
MESH = pl.DeviceIdType.MESH

ADAM_LR = 0.001
ADAM_B1 = 0.9
ADAM_B2 = 0.999
ADAM_EPS = 1e-08
ADAM_WD = 0.01
ADAM_STEP = 10

COL_Q, COL_K, COL_V, COL_ZA, COL_UB, COL_VB, COL_ZB, COL_GA, COL_GB = 0, 1, 2, 3, 4, 5, 6, 7, 9

VMEM_LIMIT = 56 * 1024 * 1024

SMALL_ROWS = 1088
SMALL_PIECE = SMALL_ROWS // N_DEV


def _cparams(sem=None):
    return pltpu.CompilerParams(dimension_semantics=sem, vmem_limit_bytes=VMEM_LIMIT)


def _aligned(v, m):
    return v if isinstance(v, int) else pl.multiple_of(v, m)


def _sigmoid(x):
    return 1.0 / (1.0 + jnp.exp(-x))


def _gelu_and_grad(x):
    k = math.sqrt(2.0 / math.pi)
    x2 = x * x
    inner = k * (x + 0.044715 * x * x2)
    th = jnp.tanh(inner)
    g = 0.5 * x * (1.0 + th)
    dg = 0.5 * (1.0 + th) + 0.5 * x * (1.0 - th * th) * (k * (1.0 + 3.0 * 0.044715 * x2))
    return g, dg


def _split_dot(a, b_bf16, passes):
    out = None
    rem = a
    for _ in range(passes):
        part = rem.astype(BF16)
        d = jnp.dot(part, b_bf16, preferred_element_type=F32)
        out = d if out is None else out + d
        rem = rem - part.astype(F32)
    return out


def _dot_nt(a, b):
    return lax.dot_general(a, b, (((1,), (1,)), ((), ())), preferred_element_type=F32)


def _dot_tn(a, b):
    return lax.dot_general(a, b, (((0,), (0,)), ((), ())), preferred_element_type=F32)


def _place():
    x, y, c = lax.axis_index("x"), lax.axis_index("y"), lax.axis_index("c")
    return x, y, c


def _in_proj_gather(x, norm_g, w_in, w_up_a, w_up_b, w_out):
    s = x.shape[0]
    tm = min(1024, s)
    nt = s // tm
    shards = (w_in, w_up_a, w_up_b, w_out)
    n_arr = len(shards)
    col_sharded = (True, True, True, False)
    full_shapes = ((D_MODEL, D_IN), (D_BRANCH, D_MODEL), (D_BRANCH, D_MODEL), (D_MODEL, D_MODEL))
    w_shard = w_in.shape[1]
    half_rows = D_MODEL // 2
    stage_rows = 256

    def body(order_ref, x_ref, g_ref, *refs):
        src = refs[:n_arr]
        proj_ref, ht_ref = refs[n_arr:n_arr + 2]
        out = refs[n_arr + 2:2 * n_arr + 2]
        wsc, h_scr, stage = refs[2 * n_arr + 2:2 * n_arr + 5]
        small_stage = refs[2 * n_arr + 5:2 * n_arr + 8]
        small_cast = refs[2 * n_arr + 8:2 * n_arr + 11]
        send_sems, recv_sems, local_sems = refs[2 * n_arr + 11:]
        k = pl.program_id(0)
        i = pl.program_id(1)
        x_, y_, c = _place()
        chip = 2 * x_ + y_
        sibling = (x_, y_, 1 - c)
        others = [(x_, 1 - y_), (1 - x_, y_), (1 - x_, 1 - y_)]

        def region(a, chip_idx, half):
            if a == 0:
                return wsc.at[chip_idx, pl.ds(_aligned(half * half_rows, 16), half_rows), :]
            r, w = shards[a].shape
            hr = r // 2
            if col_sharded[a]:
                return out[a].at[pl.ds(_aligned(half * hr, 16), hr), pl.ds(_aligned(chip_idx * w, LANES), w)]
            return out[a].at[pl.ds(_aligned(chip_idx * r + half * hr, 16), hr), :]

        def remote(kk, a, chip_idx, half, to, own):
            s_ref = region(a, chip_idx, half)
            if own and a > 0:
                hr = shards[a].shape[0] // 2
                s_ref = small_cast[a - 1].at[pl.ds(_aligned(half * hr, 16), hr), :]
            return pltpu.make_async_remote_copy(src_ref=s_ref, dst_ref=region(a, chip_idx, half),
                                                send_sem=send_sems.at[kk], recv_sem=recv_sems.at[kk],
                                                device_id=to, device_id_type=MESH)

        def keep_whole(kk, chip_idx):
            return pltpu.make_async_copy(wsc.at[chip_idx],
                                         out[0].at[:, pl.ds(_aligned(chip_idx * w_shard, LANES), w_shard)],
                                         local_sems.at[kk])

        def small_stores():
            cps = []
            for a in range(1, n_arr):
                hr = shards[a].shape[0] // 2
                for half in range(2):
                    cps.append(pltpu.make_async_copy(small_cast[a - 1].at[pl.ds(half * hr, hr), :],
                                                     region(a, chip, half), local_sems.at[4 + 2 * (a - 1) + half]))
            return cps

        def arrive_and_pass(j):
            ochip = chip ^ j
            for a in range(n_arr):
                kk = n_arr * (j - 1) + a
                remote(kk, a, ochip, c, sibling, False).wait_recv()
                remote(3 * n_arr + kk, a, ochip, c, sibling, False).start()

        def from_sibling(j, a):
            remote(3 * n_arr + n_arr * (j - 1) + a, a, chip ^ j, 1 - c, sibling, False).wait_recv()

        @pl.when((k == 0) & (i == 0))
        def _():
            def cast_rows(half):
                for t in range(half_rows // stage_rows):
                    r0 = pl.multiple_of(half * half_rows + t * stage_rows, stage_rows)
                    pltpu.sync_copy(src[0].at[pl.ds(r0, stage_rows), :], stage)
                    wsc[chip, pl.ds(r0, stage_rows), :] = stage[...].astype(BF16)

            cast_rows(c)
            for j in (1, 2):
                remote(n_arr * (j - 1), 0, chip, c, (*others[j - 1], c), True).start()
            cast_rows(1 - c)
            for a in range(1, n_arr):
                pltpu.sync_copy(src[a], small_stage[a - 1])
                small_cast[a - 1][...] = small_stage[a - 1][...].astype(BF16)
            for j in (1, 2):
                for a in range(1, n_arr):
                    remote(n_arr * (j - 1) + a, a, chip, c, (*others[j - 1], c), True).start()
            keep_whole(0, chip).start()
            for cp in small_stores():
                cp.start()

        @pl.when((k == 1) & (i == 0))
        def _():
            for j in (1, 2):
                remote(n_arr * (j - 1), 0, chip, c, (*others[j - 1], c), True).wait_send()
            for a in range(n_arr):
                remote(n_arr * 2 + a, a, chip, c, (*others[2], c), True).start()
            arrive_and_pass(1)
            arrive_and_pass(2)
            from_sibling(1, 0)
            keep_whole(1, chip ^ 1).start()

        @pl.when((k == 2) & (i == 0))
        def _():
            from_sibling(2, 0)
            keep_whole(2, chip ^ 2).start()
            arrive_and_pass(3)

        @pl.when((k == 3) & (i == 0))
        def _():
            from_sibling(3, 0)
            keep_whole(3, chip ^ 3).start()

        @pl.when(k == 0)
        def _():
            xf = x_ref[...]
            r = lax.rsqrt(jnp.mean(xf * xf, axis=-1, keepdims=True) + EPS)
            h = xf * r * g_ref[...]
            h_scr[i] = h.astype(BF16)
            ht_ref[...] = h.T.astype(BF16)

        proj_ref[...] = jnp.dot(h_scr[i], wsc[order_ref[k]], preferred_element_type=F32).astype(BF16)

        @pl.when((k == 3) & (i == nt - 1))
        def _():
            for j in (1, 2, 3):
                for a in range(1, n_arr):
                    from_sibling(j, a)
            for j in (1, 2, 3):
                for a in range(n_arr):
                    kk = n_arr * (j - 1) + a
                    if a > 0 or j == 3:
                        remote(kk, a, chip, c, (*others[j - 1], c), True).wait_send()
                    remote(3 * n_arr + kk, a, chip ^ j, c, sibling, False).wait_send()
            for kk in range(4):
                keep_whole(kk, chip ^ kk).wait()
            for cp in small_stores():
                cp.wait()

    any_spec = pl.BlockSpec(memory_space=pl.ANY)
    tile = lambda kk, ii: jnp.where(kk == 0, ii, nt - 1)
    grid_spec = pltpu.PrefetchScalarGridSpec(
        num_scalar_prefetch=1, grid=(N_CHIPS, nt),
        in_specs=[pl.BlockSpec((tm, D_MODEL), lambda kk, ii, order: (tile(kk, ii), 0)),
                  pl.BlockSpec((1, D_MODEL), lambda kk, ii, order: (0, 0))] + [any_spec] * n_arr,
        out_specs=[pl.BlockSpec((tm, w_shard), lambda kk, ii, order: (ii, order[kk])),
                   pl.BlockSpec((D_MODEL, tm), lambda kk, ii, order: (0, tile(kk, ii)))] + [any_spec] * n_arr,
        scratch_shapes=[pltpu.VMEM((N_CHIPS, D_MODEL, w_shard), BF16), pltpu.VMEM((nt, tm, D_MODEL), BF16),
                        pltpu.VMEM((stage_rows, w_shard), F32)]
        + [pltpu.VMEM(a.shape, F32) for a in shards[1:]] + [pltpu.VMEM(a.shape, BF16) for a in shards[1:]]
        + [pltpu.SemaphoreType.DMA((6 * n_arr,)), pltpu.SemaphoreType.DMA((6 * n_arr,)),
           pltpu.SemaphoreType.DMA((4 + 2 * (n_arr - 1),))])
    x_, y_, _ = _place()
    order = (2 * x_ + y_) ^ jnp.arange(N_CHIPS, dtype=jnp.int32)
    return pl.pallas_call(
        body, name="in_proj_gather", grid_spec=grid_spec,
        out_shape=[jax.ShapeDtypeStruct((s, D_IN), BF16), jax.ShapeDtypeStruct((D_MODEL, s), BF16)]
        + [jax.ShapeDtypeStruct(sh, BF16) for sh in full_shapes],
        compiler_params=pltpu.CompilerParams(dimension_semantics=("arbitrary", "arbitrary"),
                                             vmem_limit_bytes=VMEM_LIMIT, has_side_effects=True),
    )(order, x, norm_g, *shards)


def _neg_softplus_parts(z):
    zb = z.astype(BF16)
    p = jnp.exp(-jnp.abs(zb))
    return p, jnp.maximum(zb, jnp.zeros_like(zb)) + jnp.log(1.0 + p)


def _split_cat(a, passes):
    parts = []
    rem = a
    for k in range(passes):
        part = rem.astype(BF16)
        parts.append(part)
        if k + 1 < passes:
            rem = rem - part.astype(F32)
    return parts[0] if passes == 1 else jnp.concatenate(parts, axis=1)


def _tri(blk, upper, sign):
    row = lax.broadcasted_iota(jnp.int32, (blk, blk), 0)
    col = lax.broadcasted_iota(jnp.int32, (blk, blk), 1)
    keep = (row <= col) if upper else (row >= col)
    t = jnp.where(keep, sign, 0.0).astype(BF16)
    return t if SCAN_PASSES == 1 else jnp.concatenate([t] * SCAN_PASSES, axis=0)


def _attn_fwd(proj, bq, bk, npairs):
    s = proj.shape[0]
    nq = s // bq
    ratio = bq // bk
    scale = HEAD_DIM ** -0.5
    heads = tuple(range(2 * npairs))
    width = LANES * npairs

    def body(q_ref, k_ref, v_ref, za_ref, o_ref, ya_ref, rs_ref, acc_ref, r_ref, z_ref):
        i = pl.program_id(1)
        lane = lax.broadcasted_iota(jnp.int32, (bq, LANES), 1)
        lo_half = lane < HEAD_DIM
        qm = []
        for pr in range(npairs):
            q = q_ref[:, LANES * pr:LANES * (pr + 1)] * jnp.asarray(scale, BF16)
            zero = jnp.zeros_like(q)
            qm += [jnp.where(lo_half, q, zero), jnp.where(lo_half, zero, q)]
        row = lax.broadcasted_iota(jnp.int32, (bq, bk), 0)
        col = lax.broadcasted_iota(jnp.int32, (bq, bk), 1)
        tneg = _tri(bk, False, -1.0)
        acc_ref[...] = jnp.zeros_like(acc_ref)
        r_ref[...] = jnp.zeros_like(r_ref)
        rs_ref[...] = jnp.full_like(rs_ref, SKIPPED)

        def scores(j):
            ks = pl.multiple_of(j * bk, bk)
            return [_dot_nt(qm[h], k_ref[pl.ds(ks, bk), LANES * (h // 2):LANES * (h // 2 + 1)]) for h in heads]

        for h, zh in enumerate(scores(i * ratio + ratio - 1)):
            z_ref[h] = zh

        def block(j, diag):
            ks = pl.multiple_of(j * bk, bk)
            vj = [v_ref[pl.ds(ks, bk), LANES * pr:LANES * (pr + 1)] for pr in range(npairs)]
            if diag:
                before = (j * bk + col) < (i * bq + row)
            z = [z_ref[h] for h in heads]
            sp = [_neg_softplus_parts(z[h])[1] for h in heads]
            if diag:
                sp = [jnp.where(before, sp[h], 0.0) for h in heads]
            cin = [jnp.dot(_split_cat(sp[h], SCAN_PASSES), tneg, preferred_element_type=F32) for h in heads]
            for h, zh in enumerate(scores(jnp.maximum(j - 1, 0))):
                z_ref[h] = zh
            w = [jnp.exp(z[h] + cin[h]) for h in heads]
            if diag:
                w = [jnp.where(before, w[h], 0.0) for h in heads]
            pv = [jnp.dot(w[h].astype(BF16), vj[h // 2], preferred_element_type=F32) for h in heads]
            r = [r_ref[h] for h in heads]
            for h in heads:
                acc_ref[h] += pv[h] * jnp.exp(r[h])
                r_ref[h] = r[h] + cin[h][:, 0:1]
            for pr in range(npairs):
                rs_ref[pr] = jnp.where(lane == j, r[2 * pr], jnp.where(lane == j + HEAD_DIM, r[2 * pr + 1], rs_ref[pr]))

        for t in range(ratio):
            block(i * ratio + ratio - 1 - t, True)

        def alive(carry):
            jj, r_max = carry
            return (jj < i * ratio) & (r_max > DEAD)

        def loop_body(carry):
            jj, _ = carry
            block(i * ratio - 1 - jj, False)
            return jj + 1, jnp.max(r_ref[...])

        lax.while_loop(alive, loop_body, (0, jnp.max(r_ref[...])))
        for pr in range(npairs):
            cols = slice(LANES * pr, LANES * (pr + 1))
            o = jnp.where(lo_half, acc_ref[2 * pr], acc_ref[2 * pr + 1])
            o_ref[:, cols] = o.astype(BF16)
            za = za_ref[:, cols].astype(F32)
            ya_ref[:, cols] = (o * (za * _sigmoid(za))).astype(BF16)

    n_steps = N_HEADS // (2 * npairs)
    return pl.pallas_call(
        body, name="attn_fwd", grid=(n_steps, nq),
        in_specs=[pl.BlockSpec((bq, width), lambda p, i: (i, n_steps * COL_Q + p)),
                  pl.BlockSpec((s, width), lambda p, i: (0, n_steps * COL_K + p)),
                  pl.BlockSpec((s, width), lambda p, i: (0, n_steps * COL_V + p)),
                  pl.BlockSpec((bq, width), lambda p, i: (i, n_steps * COL_ZA + p))],
        out_specs=[pl.BlockSpec((bq, width), lambda p, i: (i, p)),
                   pl.BlockSpec((bq, width), lambda p, i: (i, p)),
                   pl.BlockSpec((npairs, bq, LANES), lambda p, i: (p, i, 0))],
        out_shape=[jax.ShapeDtypeStruct((s, D_BRANCH), BF16), jax.ShapeDtypeStruct((s, D_BRANCH), BF16),
                   jax.ShapeDtypeStruct((N_HEADS // 2, s, LANES), F32)],
        scratch_shapes=[pltpu.VMEM((2 * npairs, bq, LANES), F32), pltpu.VMEM((2 * npairs, bq, 1), F32),
                        pltpu.VMEM((2 * npairs, bq, bk), F32)],
        compiler_params=_cparams(("parallel", "parallel")),
    )(proj, proj, proj, proj)


_GRAD_COL_SHARDED = (True, True, True, False)
_GRAD_FULL_SHAPES = ((D_MODEL, D_IN), (D_BRANCH, D_MODEL), (D_BRANCH, D_MODEL), (D_MODEL, D_MODEL))
_GRAD_PIECE_SHAPES = tuple((r // 2, w // N_CHIPS) if cs else (r // (2 * N_CHIPS), w)
                           for (r, w), cs in zip(_GRAD_FULL_SHAPES, _GRAD_COL_SHARDED))
_EARLY_IN_DEVS = (4, 5, 6, 7)
_LATE_IN_DEVS = (0, 1, 2, 3)
_LATE_CHIPS = (0, 1)
_EARLY_CHIPS = (2, 3)
_ALL_CHIPS = (0, 1, 2, 3)


def _grad_piece(ref, a, dev):
    r, w = _GRAD_PIECE_SHAPES[a]
    if _GRAD_COL_SHARDED[a]:
        return ref.at[pl.ds((dev % 2) * r, r), pl.ds((dev // 2) * w, w)]
    return ref.at[pl.ds(dev * r, r), :]


def _dev_id(dev):
    return (dev // 4, (dev // 2) % 2, dev % 2)


def _me():
    return 4 * lax.axis_index("x") + 2 * lax.axis_index("y") + lax.axis_index("c")


def _presum_copy(src, dst, send_sem, recv_sem, to_dev):
    return pltpu.make_async_remote_copy(src_ref=src, dst_ref=dst, send_sem=send_sem, recv_sem=recv_sem,
                                        device_id=_dev_id(to_dev), device_id_type=MESH)


def _presum_hand_off(dev, a, dest_chips, g_ref, slots, pair, send_sems, recv_sems):
    chip, core = dev // 2, dev % 2
    cps = []
    for k, q in enumerate(dest_chips):
        piece = _grad_piece(g_ref, a, 2 * q + 1 - core)
        if q == chip:
            cps.append(_presum_copy(piece, slots.at[dev], send_sems.at[N_DEV + k], recv_sems.at[dev], dev ^ 1))
        else:
            cps.append(_presum_copy(piece, pair.at[k], send_sems.at[N_DEV + k], recv_sems.at[N_DEV + k], dev ^ 1))
    return cps


def _presum_sends(dev, a, dest_chips, slots, sums, send_sems, recv_sems):
    chip, core = dev // 2, dev % 2
    return [_presum_copy(sums.at[k], slots.at[dev], send_sems.at[2 * q + core], recv_sems.at[dev], 2 * q + core)
            for k, q in enumerate(dest_chips) if q != chip]


def _presum_loads(dev, a, dest_chips, g_ref, stage, load_sems):
    chip, core = dev // 2, dev % 2
    return [pltpu.make_async_copy(_grad_piece(g_ref, a, 2 * q + core), stage.at[k], load_sems.at[k])
            for k, q in enumerate(dest_chips) if q != chip]


def _presum_send(dev, a, dest_chips, g_ref, slots, pair, stage, sums, send_sems, recv_sems, load_sems):
    chip, core = dev // 2, dev % 2
    hand = _presum_hand_off(dev, a, dest_chips, g_ref, slots, pair, send_sems, recv_sems)
    for cp in _presum_loads(dev, a, dest_chips, g_ref, stage, load_sems):
        cp.wait()
    for k, q in enumerate(dest_chips):
        if q != chip:
            hand[k].wait_recv()
            sums[k] = (stage[k].astype(F32) + pair[k].astype(F32)).astype(BF16)
    for cp in _presum_sends(dev, a, dest_chips, slots, sums, send_sems, recv_sems):
        cp.start()


def _presum_wait(dev, a, dest_chips, g_ref, slots, pair, sums, send_sems, recv_sems):
    chip, core = dev // 2, dev % 2
    for cp in _presum_hand_off(dev, a, dest_chips, g_ref, slots, pair, send_sems, recv_sems):
        cp.wait_send()
    for cp in _presum_sends(dev, a, dest_chips, slots, sums, send_sems, recv_sems):
        cp.wait_send()
    if chip in dest_chips:
        for src_dev in _presum_sources(dev):
            _presum_copy(sums.at[0], slots.at[src_dev], send_sems.at[src_dev], recv_sems.at[src_dev], src_dev).wait_recv()


def _presum_sources(dev):
    return [dev ^ 1] + [2 * r + dev % 2 for r in range(N_CHIPS) if r != dev // 2]


def _presum_scratch(a, dest_chips):
    n = len(dest_chips)
    piece = _GRAD_PIECE_SHAPES[a]
    return [pltpu.VMEM((n,) + piece, BF16), pltpu.VMEM((n,) + piece, BF16), pltpu.VMEM((n,) + piece, BF16),
            pltpu.SemaphoreType.DMA((N_DEV + n,)), pltpu.SemaphoreType.DMA((N_DEV + n,)),
            pltpu.SemaphoreType.DMA((n,))]


PRESUM_SCRATCH = 6


def _presum_program(first, second, last, a, dest_chips, g_ref, slots, scratch):
    pair, stage, sums, send_sems, recv_sems, load_sems = scratch
    me = _me()

    @pl.when(first)
    def _():
        for dev in range(N_DEV):
            @pl.when(me == dev)
            def _():
                for cp in _presum_hand_off(dev, a, dest_chips, g_ref, slots, pair, send_sems, recv_sems):
                    cp.start()
                for cp in _presum_loads(dev, a, dest_chips, g_ref, stage, load_sems):
                    cp.start()

    @pl.when(second)
    def _():
        for dev in range(N_DEV):
            @pl.when(me == dev)
            def _():
                _presum_send(dev, a, dest_chips, g_ref, slots, pair, stage, sums, send_sems, recv_sems, load_sems)

    def finish():
        for dev in range(N_DEV):
            @pl.when(me == dev)
            def _():
                _presum_wait(dev, a, dest_chips, g_ref, slots, pair, sums, send_sems, recv_sems)

    return finish


def _attn_bwd(proj, do, rsave, bq, bk, npairs, grads):
    plan = ((0, _EARLY_CHIPS), (1, _ALL_CHIPS), (2, _ALL_CHIPS), (3, _ALL_CHIPS))
    s = proj.shape[0]
    nq = s // bq
    ratio = bq // bk
    scale = HEAD_DIM ** -0.5
    heads = tuple(range(2 * npairs))
    width = LANES * npairs

    n_steps = N_HEADS // (2 * npairs)
    n_g = len(grads)

    def body(q_ref, k_ref, v_ref, do_ref, rs_ref, *refs):
        g_src = refs[:n_g]
        dq_ref, dk_ref, dv_ref = refs[n_g:n_g + 3]
        g_slots = refs[n_g + 3:2 * n_g + 3]
        dk_acc, dv_acc, dq_acc, e_ref = refs[2 * n_g + 3:2 * n_g + 7]
        i = pl.program_id(1)
        step = pl.program_id(0) * nq + i
        finish = [_presum_program(step == 0, step == 1, step == n_steps * nq - 1, a, chips, g_src[pos], g_slots[pos],
                                  refs[2 * n_g + 7 + PRESUM_SCRATCH * pos:2 * n_g + 7 + PRESUM_SCRATCH * (pos + 1)])
                  for pos, (a, chips) in enumerate(plan)]

        lane = lax.broadcasted_iota(jnp.int32, (bq, LANES), 1)
        lo_half = lane < HEAD_DIM
        qm, dom = [], []
        for pr in range(npairs):
            cols = slice(LANES * pr, LANES * (pr + 1))
            q = q_ref[:, cols] * jnp.asarray(scale, BF16)
            zero = jnp.zeros_like(q)
            qm += [jnp.where(lo_half, q, zero), jnp.where(lo_half, zero, q)]
            dout = do_ref[:, cols].astype(F32)
            dom += [jnp.where(lo_half, dout, 0.0), jnp.where(lo_half, 0.0, dout)]
        row = lax.broadcasted_iota(jnp.int32, (bq, bk), 0)
        col = lax.broadcasted_iota(jnp.int32, (bq, bk), 1)
        tneg = _tri(bk, False, -1.0)
        tfwd = _tri(bk, True, 1.0)

        @pl.when(i == 0)
        def _():
            dk_acc[...] = jnp.zeros_like(dk_acc)
            dv_acc[...] = jnp.zeros_like(dv_acc)

        dq_acc[...] = jnp.zeros_like(dq_acc)
        e_ref[...] = jnp.zeros_like(e_ref)

        def block(j, diag):
            ks = pl.multiple_of(j * bk, bk)
            kj = [k_ref[pl.ds(ks, bk), LANES * pr:LANES * (pr + 1)] for pr in range(npairs)]
            vj = [v_ref[pl.ds(ks, bk), LANES * pr:LANES * (pr + 1)] for pr in range(npairs)]
            if diag:
                before = (j * bk + col) < (i * bq + row)
            z = [_dot_nt(qm[h], kj[h // 2]) for h in heads]
            er = [jnp.exp(jnp.sum(jnp.where(lane == j + HEAD_DIM * (h % 2), rs_ref[h // 2], 0.0), axis=-1,
                                  keepdims=True)) for h in heads]
            dos = [(dom[h] * er[h]).astype(BF16) for h in heads]
            dw = [_dot_nt(dos[h], vj[h // 2]) for h in heads]
            psp = [_neg_softplus_parts(z[h]) for h in heads]
            sp = [psp[h][1] for h in heads]
            if diag:
                sp = [jnp.where(before, sp[h], 0.0) for h in heads]
            cin = [jnp.dot(_split_cat(sp[h], SCAN_PASSES), tneg, preferred_element_type=F32) for h in heads]
            w = [jnp.exp(z[h] + cin[h]) for h in heads]
            if diag:
                w = [jnp.where(before, w[h], 0.0) for h in heads]
            e =[dw[h] * w[h] for h in heads]
            eincl = [jnp.dot(_split_cat(e[h], SCAN_PASSES), tfwd, preferred_element_type=F32) + e_ref[h]
                     for h in heads]
            dz = []
            for h in heads:
                p = psp[h][0]
                beta = jnp.where(z[h] >= 0.0, 1.0, p) / (1.0 + p)
                d = e[h] - beta * eincl[h]
                dz.append((jnp.where(before, d, 0.0) if diag else d).astype(BF16))
            wb = [w[h].astype(BF16) for h in heads]
            for h in heads:
                e_ref[h] = eincl[h][:, bk - 1:bk]
                dq_acc[h] += jnp.dot(dz[h], kj[h // 2], preferred_element_type=F32)
            for pr in range(npairs):
                cols = slice(LANES * pr, LANES * (pr + 1))
                h0, h1 = 2 * pr, 2 * pr + 1
                dk_acc[pl.ds(ks, bk), cols] += _dot_tn(dz[h0], qm[h0]) + _dot_tn(dz[h1], qm[h1])
                dv_acc[pl.ds(ks, bk), cols] += _dot_tn(wb[h0], dos[h0]) + _dot_tn(wb[h1], dos[h1])

        def loop_body(j, carry):
            block(j, False)
            return carry

        block_of_lane = lane & (HEAD_DIM - 1)
        live = jnp.max(rs_ref[...], axis=0) > DEAD
        first_live = jnp.min(jnp.where(live, block_of_lane, nq * ratio))
        lax.fori_loop(jnp.minimum(first_live, i * ratio), i * ratio, loop_body, 0)
        for t in range(ratio):
            block(i * ratio + t, True)
        for pr in range(npairs):
            dq = jnp.where(lo_half, dq_acc[2 * pr], dq_acc[2 * pr + 1]) * scale
            dq_ref[:, LANES * pr:LANES * (pr + 1)] = dq.astype(BF16)

        @pl.when(i == nq - 1)
        def _():
            dk_ref[...] = dk_acc[...].astype(BF16)
            dv_ref[...] = dv_acc[...].astype(BF16)

        @pl.when(step == n_steps * nq - 1)
        def _():
            for fin in finish:
                fin()

    any_spec = pl.BlockSpec(memory_space=pl.ANY)
    return pl.pallas_call(
        body, name="attn_bwd", grid=(n_steps, nq),
        in_specs=[pl.BlockSpec((bq, width), lambda p, i: (i, n_steps * COL_Q + p)),
                  pl.BlockSpec((s, width), lambda p, i: (0, n_steps * COL_K + p)),
                  pl.BlockSpec((s, width), lambda p, i: (0, n_steps * COL_V + p)),
                  pl.BlockSpec((bq, width), lambda p, i: (i, p)),
                  pl.BlockSpec((npairs, bq, LANES), lambda p, i: (p, i, 0))] + [any_spec] * n_g,
        out_specs=[pl.BlockSpec((bq, width), lambda p, i: (i, p)),
                   pl.BlockSpec((s, width), lambda p, i: (0, p)),
                   pl.BlockSpec((s, width), lambda p, i: (0, p))] + [any_spec] * n_g,
        out_shape=[jax.ShapeDtypeStruct((s, D_BRANCH), BF16)] * 3
        + [jax.ShapeDtypeStruct((N_DEV,) + _GRAD_PIECE_SHAPES[a], BF16) for a, _ in plan],
        scratch_shapes=[pltpu.VMEM((s, width), F32), pltpu.VMEM((s, width), F32),
                        pltpu.VMEM((2 * npairs, bq, LANES), F32), pltpu.VMEM((2 * npairs, bq, 1), F32)]
        + [sh for a, chips in plan for sh in _presum_scratch(a, chips)],
        compiler_params=pltpu.CompilerParams(dimension_semantics=("arbitrary", "arbitrary"),
                                             vmem_limit_bytes=VMEM_LIMIT, has_side_effects=True),
    )(proj, proj, proj, do, rsave, *grads)


def _group_avg_matrix():
    a = lax.broadcasted_iota(jnp.int32, (LANES, LANES), 0) >> GROUP_SHIFT
    b = lax.broadcasted_iota(jnp.int32, (LANES, LANES), 1) >> GROUP_SHIFT
    return jnp.where(a == b, 1.0 / GROUP_DIM, 0.0).astype(BF16)


def _group_mean(a, avg):
    parts = [_split_dot(a[:, LANES * k:LANES * (k + 1)], avg, 3) for k in range(D_BRANCH // LANES)]
    return jnp.concatenate(parts, axis=1)


def _sgu_forward_parts(ub, vb, ln_g, ln_b, avg):
    ug, dug = _gelu_and_grad(ub)
    vg, dvg = _gelu_and_grad(vb)
    mu = _group_mean(vg, avg)
    d = vg - mu
    var = _group_mean(d * d, avg)
    rstd = lax.rsqrt(var + EPS)
    vhat = d * rstd
    vn = vhat * ln_g + ln_b
    return ug, dug, dvg, rstd, vhat, vn


def _sgu_mix(w_ref, src_bf16, n_chunks):
    lane = lax.broadcasted_iota(jnp.int32, (SGU_CHUNK, LANES), 1)
    lo_half = lane < GROUP_DIM
    rows = []
    for n in range(n_chunks):
        slabs = []
        for a in range(D_BRANCH // LANES):
            blk = src_bf16[SGU_CHUNK * n:SGU_CHUNK * (n + 1), LANES * a:LANES * (a + 1)]
            zero = jnp.zeros_like(blk)
            m0 = jnp.dot(w_ref[2 * a], jnp.where(lo_half, blk, zero), preferred_element_type=F32)
            m1 = jnp.dot(w_ref[2 * a + 1], jnp.where(lo_half, zero, blk), preferred_element_type=F32)
            slabs.append(m0 + m1)
        rows.append(jnp.concatenate(slabs, axis=1))
    return jnp.concatenate(rows, axis=0)


def _sgu_fwd(proj, ln_g, ln_b, w_mask, bias_full):
    s = proj.shape[0]
    tm = min(512, s)
    n_chunks = tm // SGU_CHUNK

    def body(ub_ref, vb_ref, zb_ref, g_ref, b_ref, w_ref, bias_ref, yb_ref):
        avg = _group_avg_matrix()
        ug, _, _, _, _, vn = _sgu_forward_parts(ub_ref[...].astype(F32), vb_ref[...].astype(F32),
                                                g_ref[...], b_ref[...], avg)
        mixed = _sgu_mix(w_ref, vn.astype(BF16), n_chunks) + jnp.concatenate([bias_ref[...]] * n_chunks, axis=0)
        zb = zb_ref[...].astype(F32)
        yb_ref[...] = (ug * mixed * (zb * _sigmoid(zb))).astype(BF16)

    col = lambda c: pl.BlockSpec((tm, D_BRANCH), lambda i: (i, c))
    full = lambda shape: pl.BlockSpec(shape, lambda i: (0,) * len(shape))
    return pl.pallas_call(
        body, name="sgu_fwd", grid=(s // tm,),
        in_specs=[col(COL_UB), col(COL_VB), col(COL_ZB), full((1, D_BRANCH)), full((1, D_BRANCH)),
                  full((N_GROUPS, SGU_CHUNK, SGU_CHUNK)), full((SGU_CHUNK, D_BRANCH))],
        out_specs=pl.BlockSpec((tm, D_BRANCH), lambda i: (i, 0)),
        out_shape=jax.ShapeDtypeStruct((s, D_BRANCH), BF16),
        compiler_params=_cparams(("parallel",)),
    )(proj, proj, proj, ln_g, ln_b, w_mask, bias_full)


def _sgu_bwd(proj, dyb, ln_g, ln_b, w_mask, w_mask_t, bias_full):
    s = proj.shape[0]
    tm = min(512, s)
    n_chunks = tm // SGU_CHUNK
    n_steps = s // tm

    def body(ub_ref, vb_ref, zb_ref, dyb_ref, g_ref, b_ref, w_ref, wt_ref, bias_ref,
             dsgu_ref, dw_ref, db_ref, dg_ref, dbeta_ref, dmix_acc):
        i = pl.program_id(0)

        @pl.when(i == 0)
        def _():
            dw_ref[...] = jnp.zeros_like(dw_ref)
            dg_ref[...] = jnp.zeros_like(dg_ref)
            dbeta_ref[...] = jnp.zeros_like(dbeta_ref)
            dmix_acc[...] = jnp.zeros_like(dmix_acc)

        avg = _group_avg_matrix()
        ln_gv = g_ref[...]
        ug, dug, dvg, rstd, vhat, vn = _sgu_forward_parts(ub_ref[...].astype(F32), vb_ref[...].astype(F32),
                                                          ln_gv, b_ref[...], avg)
        vnb = vn.astype(BF16)
        mixed = _sgu_mix(w_ref, vnb, n_chunks) + jnp.concatenate([bias_ref[...]] * n_chunks, axis=0)
        zb = zb_ref[...].astype(F32)
        sg = _sigmoid(zb)
        sz = zb * sg
        dsz = sg * (1.0 + zb * (1.0 - sg))
        dy = dyb_ref[...].astype(F32)
        dmixed = dy * ug * sz
        du = dy * mixed * sz * dug
        dzb = dy * ug * mixed * dsz
        dmb = dmixed.astype(BF16)
        dvn = _sgu_mix(wt_ref, dmb, n_chunks)

        lane = lax.broadcasted_iota(jnp.int32, (SGU_CHUNK, LANES), 1)
        lo_half = lane < GROUP_DIM
        dm_sum = None
        for n in range(n_chunks):
            rows = slice(SGU_CHUNK * n, SGU_CHUNK * (n + 1))
            dm_sum = dmixed[rows] if dm_sum is None else dm_sum + dmixed[rows]
            for a in range(D_BRANCH // LANES):
                cols = slice(LANES * a, LANES * (a + 1))
                dblk = dmb[rows, cols]
                vblk = vnb[rows, cols]
                zero = jnp.zeros_like(dblk)
                dw_ref[2 * a] += _dot_nt(jnp.where(lo_half, dblk, zero), vblk)
                dw_ref[2 * a + 1] += _dot_nt(jnp.where(lo_half, zero, dblk), vblk)
        dmix_acc[...] += dm_sum

        dg_ref[...] += jnp.sum(dvn * vhat, axis=0, keepdims=True)
        dbeta_ref[...] += jnp.sum(dvn, axis=0, keepdims=True)
        dvh = dvn * ln_gv
        m1 = _group_mean(dvh, avg)
        m2 = _group_mean(dvh * vhat, avg)
        dv = rstd * (dvh - m1 - vhat * m2) * dvg
        dsgu_ref[:, 0:D_BRANCH] = du.astype(BF16)
        dsgu_ref[:, D_BRANCH:2 * D_BRANCH] = dv.astype(BF16)
        dsgu_ref[:, 2 * D_BRANCH:3 * D_BRANCH] = dzb.astype(BF16)

        @pl.when(i == n_steps - 1)
        def _():
            pos = lax.broadcasted_iota(jnp.int32, (SGU_CHUNK, SGU_CHUNK), 0) >> GROUP_SHIFT
            src = lax.broadcasted_iota(jnp.int32, (SGU_CHUNK, SGU_CHUNK), 1) >> GROUP_SHIFT
            keep = src <= pos
            for g in range(N_GROUPS):
                dw_ref[g] = jnp.where(keep, dw_ref[g], 0.0)
            grp = lax.broadcasted_iota(jnp.int32, (D_BRANCH, LANES), 0) >> GROUP_SHIFT
            sel = (grp == lax.broadcasted_iota(jnp.int32, (D_BRANCH, LANES), 1)).astype(BF16)
            db_ref[...] = _split_dot(dmix_acc[...], sel, 3)

    col = lambda c: pl.BlockSpec((tm, D_BRANCH), lambda i: (i, c))
    full = lambda shape: pl.BlockSpec(shape, lambda i: (0,) * len(shape))
    return pl.pallas_call(
        body, name="sgu_bwd", grid=(n_steps,),
        in_specs=[col(COL_UB), col(COL_VB), col(COL_ZB), pl.BlockSpec((tm, D_BRANCH), lambda i: (i, 0)),
                  full((1, D_BRANCH)), full((1, D_BRANCH)),
                  full((N_GROUPS, SGU_CHUNK, SGU_CHUNK)), full((N_GROUPS, SGU_CHUNK, SGU_CHUNK)),
                  full((SGU_CHUNK, D_BRANCH))],
        out_specs=[pl.BlockSpec((tm, 3 * D_BRANCH), lambda i: (i, 0)),
                   full((N_GROUPS, SGU_CHUNK, SGU_CHUNK)), full((SGU_CHUNK, LANES)),
                   full((1, D_BRANCH)), full((1, D_BRANCH))],
        out_shape=[jax.ShapeDtypeStruct((s, 3 * D_BRANCH), BF16),
                   jax.ShapeDtypeStruct((N_GROUPS, SGU_CHUNK, SGU_CHUNK), F32),
                   jax.ShapeDtypeStruct((SGU_CHUNK, LANES), F32),
                   jax.ShapeDtypeStruct((1, D_BRANCH), F32), jax.ShapeDtypeStruct((1, D_BRANCH), F32)],
        scratch_shapes=[pltpu.VMEM((SGU_CHUNK, D_BRANCH), F32)],
        compiler_params=_cparams(("arbitrary",)),
    )(proj, proj, proj, dyb, ln_g, ln_b, w_mask, w_mask_t, bias_full)


def _mid(proj, ya, yb, o, x, target, final_g, w_up_a, w_up_b, w_out):
    s = x.shape[0]
    tm = min(256, s)
    n_steps = s // tm
    half = D_MODEL // 2

    def body(ya_ref, yb_ref, o_ref, za_ref, ga0_ref, ga1_ref, gb0_ref, gb1_ref, x_ref, t_ref, gf_ref,
             wa_ref, wb_ref, wo_ref,
             dzg_ref, do_ref, dyb_ref, dx2_ref, gwo_ref, gwa_ref, gwb_ref, loss_ref, dgf_ref,
             acc_o, acc_a, acc_b):
        i = pl.program_id(0)

        @pl.when(i == 0)
        def _():
            acc_o[...] = jnp.zeros_like(acc_o)
            acc_a[...] = jnp.zeros_like(acc_a)
            acc_b[...] = jnp.zeros_like(acc_b)
            loss_ref[...] = jnp.zeros_like(loss_ref)
            dgf_ref[...] = jnp.zeros_like(dgf_ref)

        ya_v = ya_ref[...]
        yb_v = yb_ref[...]
        pa = jnp.dot(ya_v, wa_ref[...], preferred_element_type=F32)
        pb = jnp.dot(yb_v, wb_ref[...], preferred_element_type=F32)
        sa = _sigmoid(jnp.concatenate([ga0_ref[...], ga1_ref[...]], axis=1).astype(F32))
        sb = _sigmoid(jnp.concatenate([gb0_ref[...], gb1_ref[...]], axis=1).astype(F32))
        merged = (sa * pa + sb * pb).astype(BF16)
        x2 = x_ref[...] + jnp.dot(merged, wo_ref[...], preferred_element_type=F32)
        r2 = lax.rsqrt(jnp.mean(x2 * x2, axis=-1, keepdims=True) + EPS)
        xh = x2 * r2
        gf = gf_ref[...]
        diff = xh * gf - t_ref[...]
        loss_ref[...] += 0.5 * jnp.sum(jnp.mean(diff * diff, axis=-1, keepdims=True))
        dy = diff * (1.0 / D_MODEL)
        dgf_ref[...] += jnp.sum(dy * xh, axis=0, keepdims=True)
        dyg = dy * gf
        dx2 = r2 * (dyg - xh * jnp.mean(dyg * xh, axis=-1, keepdims=True))
        dx2_ref[...] = dx2
        dx2b = dx2.astype(BF16)
        dmerged = _dot_nt(dx2b, wo_ref[...])
        acc_o[...] += _dot_tn(merged, dx2b)
        dpa = dmerged * sa
        dpb = dmerged * sb
        dzg_ref[:, D_BRANCH:D_BRANCH + D_MODEL] = (dpa * pa * (1.0 - sa)).astype(BF16)
        dzg_ref[:, D_BRANCH + D_MODEL:D_BRANCH + 2 * D_MODEL] = (dpb * pb * (1.0 - sb)).astype(BF16)
        dpab = dpa.astype(BF16)
        dpbb = dpb.astype(BF16)
        acc_a[...] += _dot_tn(ya_v, dpab)
        acc_b[...] += _dot_tn(yb_v, dpbb)
        dya = _dot_nt(dpab, wa_ref[...])
        dyb_ref[...] = _dot_nt(dpbb, wb_ref[...]).astype(BF16)
        za = za_ref[...].astype(F32)
        sg = _sigmoid(za)
        do_ref[...] = (dya * (za * sg)).astype(BF16)
        dzg_ref[:, 0:D_BRANCH] = (dya * o_ref[...].astype(F32) * (sg * (1.0 + za * (1.0 - sg)))).astype(BF16)

        @pl.when(i == n_steps - 1)
        def _():
            gwo_ref[...] = acc_o[...].astype(BF16)
            gwa_ref[...] = acc_a[...].astype(BF16)
            gwb_ref[...] = acc_b[...].astype(BF16)

    tok = lambda w: pl.BlockSpec((tm, w), lambda i: (i, 0))
    col = lambda c: pl.BlockSpec((tm, half), lambda i: (i, c))
    full = lambda shape: pl.BlockSpec(shape, lambda i: (0,) * len(shape))
    return pl.pallas_call(
        body, name="mid", grid=(n_steps,),
        in_specs=[tok(D_BRANCH), tok(D_BRANCH), tok(D_BRANCH), col(COL_ZA), col(COL_GA), col(COL_GA + 1),
                  col(COL_GB), col(COL_GB + 1), tok(D_MODEL), tok(D_MODEL), full((1, D_MODEL)),
                  full((D_BRANCH, D_MODEL)), full((D_BRANCH, D_MODEL)), full((D_MODEL, D_MODEL))],
        out_specs=[tok(D_BRANCH + 2 * D_MODEL), tok(D_BRANCH), tok(D_BRANCH), tok(D_MODEL),
                   full((D_MODEL, D_MODEL)), full((D_BRANCH, D_MODEL)), full((D_BRANCH, D_MODEL)),
                   full((8, LANES)), full((1, D_MODEL))],
        out_shape=[jax.ShapeDtypeStruct((s, D_BRANCH + 2 * D_MODEL), BF16),
                   jax.ShapeDtypeStruct((s, D_BRANCH), BF16), jax.ShapeDtypeStruct((s, D_BRANCH), BF16),
                   jax.ShapeDtypeStruct((s, D_MODEL), F32),
                   jax.ShapeDtypeStruct((D_MODEL, D_MODEL), BF16),
                   jax.ShapeDtypeStruct((D_BRANCH, D_MODEL), BF16), jax.ShapeDtypeStruct((D_BRANCH, D_MODEL), BF16),
                   jax.ShapeDtypeStruct((8, LANES), F32), jax.ShapeDtypeStruct((1, D_MODEL), F32)],
        scratch_shapes=[pltpu.VMEM((D_MODEL, D_MODEL), F32), pltpu.VMEM((D_BRANCH, D_MODEL), F32),
                        pltpu.VMEM((D_BRANCH, D_MODEL), F32)],
        compiler_params=_cparams(("arbitrary",)),
    )(ya, yb, o, proj, proj, proj, proj, proj, x, target, final_g, w_up_a, w_up_b, w_out)


def _dwin_piece(ht, piece, tile_of, prev):
    s = ht.shape[1]
    n_tiles = piece.shape[1] // D_BRANCH

    def body(ht_ref, p_ref, *rest):
        out_ref = rest[-1]
        out_ref[...] = jnp.dot(ht_ref[...], p_ref[...], preferred_element_type=F32).astype(BF16)

    in_specs = [pl.BlockSpec((D_MODEL, s), lambda j: (0, 0)), pl.BlockSpec((s, D_BRANCH), lambda j: (0, j))]
    args = [ht, piece]
    aliases = {}
    if prev is not None:
        in_specs.append(pl.BlockSpec(memory_space=pl.ANY))
        args.append(prev)
        aliases = {2: 0}
    return pl.pallas_call(
        body, name="dwin_piece", grid=(n_tiles,),
        in_specs=in_specs,
        out_specs=pl.BlockSpec((D_MODEL, D_BRANCH), lambda j: (0, tile_of(j))),
        out_shape=jax.ShapeDtypeStruct((D_MODEL, D_IN), BF16),
        input_output_aliases=aliases,
        compiler_params=_cparams(("parallel",)),
    )(*args)


def _dh_dx(pieces, w_in, x, norm_g, dx2, g_in):
    s = x.shape[0]
    tm = min(256, s)
    n_steps = s // tm
    arrays = []
    for arr, _, _, _ in pieces:
        if not any(arr is a for a in arrays):
            arrays.append(arr)
    n_arr = len(arrays)
    plan = [([k for k, a in enumerate(arrays) if a is arr][0], wcol, off, width) for arr, wcol, off, width in pieces]

    def body(*refs):
        p_refs = refs[:n_arr]
        w_ref, x_ref, g_ref, dx2_ref, gin_ref, dx_ref, dg_ref, late_ref = refs[n_arr:n_arr + 8]
        step = pl.program_id(0)
        finish = _presum_program(step == 0, step == min(1, n_steps - 1), step == n_steps - 1, 0, _LATE_CHIPS,
                                 gin_ref, late_ref, refs[n_arr + 8:])

        @pl.when(step == 0)
        def _():
            dg_ref[...] = jnp.zeros_like(dg_ref)

        dh = None
        for k, wcol, off, width in plan:
            d = _dot_nt(p_refs[k][:, off:off + width], w_ref[:, wcol:wcol + width])
            dh = d if dh is None else dh + d
        xf = x_ref[...]
        r = lax.rsqrt(jnp.mean(xf * xf, axis=-1, keepdims=True) + EPS)
        xh = xf * r
        dg_ref[...] += jnp.sum(dh * xh, axis=0, keepdims=True)
        dhg = dh * g_ref[...]
        dx_ref[...] = r * (dhg - xh * jnp.mean(dhg * xh, axis=-1, keepdims=True)) + dx2_ref[...]

        @pl.when(step == n_steps - 1)
        def _():
            finish()

    tok = lambda w: pl.BlockSpec((tm, w), lambda i: (i, 0))
    full = lambda shape: pl.BlockSpec(shape, lambda i: (0,) * len(shape))
    any_spec = pl.BlockSpec(memory_space=pl.ANY)
    return pl.pallas_call(
        body, name="dh_dx", grid=(n_steps,),
        in_specs=[tok(a.shape[1]) for a in arrays] + [full((D_MODEL, D_IN)), tok(D_MODEL), full((1, D_MODEL)),
                                                      tok(D_MODEL), any_spec],
        out_specs=[tok(D_MODEL), full((1, D_MODEL)), any_spec],
        out_shape=[jax.ShapeDtypeStruct((s, D_MODEL), F32), jax.ShapeDtypeStruct((1, D_MODEL), F32),
                   jax.ShapeDtypeStruct((N_DEV,) + _GRAD_PIECE_SHAPES[0], BF16)],
        scratch_shapes=_presum_scratch(0, _LATE_CHIPS),
        compiler_params=pltpu.CompilerParams(dimension_semantics=("arbitrary",), vmem_limit_bytes=VMEM_LIMIT,
                                             has_side_effects=True),
    )(*arrays, w_in, x, norm_g, dx2, g_in)


def _adamw(w, g, m, v):
    rows, cols = w.shape
    tr = max(t for t in range(8, 257, 8) if rows % t == 0)
    c1 =1.0 - ADAM_B1 ** ADAM_STEP
    c2 = 1.0 - ADAM_B2 ** ADAM_STEP

    def body(w_ref, g_ref, m_ref, v_ref, g_out_ref, d_ref, nm_ref, nv_ref):
        gv = g_ref[...]
        g_out_ref[...] = gv
        nm = ADAM_B1 * m_ref[...] + (1.0 - ADAM_B1) * gv
        nv = ADAM_B2 * v_ref[...] + (1.0 - ADAM_B2) * (gv * gv)
        d_ref[...] = -ADAM_LR * ((nm / c1) / (jnp.sqrt(nv / c2) + ADAM_EPS) + ADAM_WD * w_ref[...])
        nm_ref[...] = nm
        nv_ref[...] = nv

    spec = pl.BlockSpec((tr, cols), lambda i: (i, 0))
    return pl.pallas_call(
        body, name="adamw", grid=(rows // tr,),
        in_specs=[spec] * 4, out_specs=[spec] * 4,
        out_shape=[jax.ShapeDtypeStruct((rows, cols), F32)] * 4,
        compiler_params=_cparams(("parallel",)),
    )(w, g, m, v)


def _place():
    x, y, c = lax.axis_index("x"), lax.axis_index("y"), lax.axis_index("c")
    return x, y, c


def _gather_weights(w_in, w_up_a, w_up_b, w_out):
    shards = (w_in, w_up_a, w_up_b, w_out)
    n_arr = len(shards)
    col_sharded = (True, True, True, False)
    full_shapes = ((D_MODEL, D_IN), (D_BRANCH, D_MODEL), (D_BRANCH, D_MODEL), (D_MODEL, D_MODEL))

    def body(*refs):
        src = refs[:n_arr]
        out = refs[n_arr:2 * n_arr]
        stage = refs[2 * n_arr:3 * n_arr]
        cast = refs[3 * n_arr:4 * n_arr]
        send_sems, recv_sems, local_sems = refs[4 * n_arr:]
        x, y, c = _place()
        chip = 2 * x + y
        sibling = (x, y, 1 - c)
        others = [(1 - x, y), (x, 1 - y), (1 - x, 1 - y)]

        def region(a, chip_idx, half):
            r, w = shards[a].shape
            hr = r // 2
            if col_sharded[a]:
                return out[a].at[pl.ds(_aligned(half * hr, 16), hr), pl.ds(_aligned(chip_idx * w, LANES), w)]
            return out[a].at[pl.ds(_aligned(chip_idx * r + half * hr, 16), hr), :]

        loads = [pltpu.make_async_copy(src[a], stage[a], local_sems.at[a]) for a in range(n_arr)]
        for cp in loads:
            cp.start()
        for a in range(n_arr):
            loads[a].wait()
            cast[a][...] = stage[a][...].astype(BF16)
        stores = []
        for a in range(n_arr):
            hr = shards[a].shape[0] // 2
            for half in range(2):
                cp = pltpu.make_async_copy(cast[a].at[pl.ds(half * hr, hr), :], region(a, chip, half),
                                           local_sems.at[n_arr + 2 * a + half])
                cp.start()
                stores.append(cp)

        def remote(k, a, chip_idx, half, to, from_vmem):
            hr = shards[a].shape[0] // 2
            s_ref = cast[a].at[pl.ds(_aligned(half * hr, 16), hr), :] if from_vmem else region(a, chip_idx, half)
            return pltpu.make_async_remote_copy(src_ref=s_ref, dst_ref=region(a, chip_idx, half),
                                                send_sem=send_sems.at[k], recv_sem=recv_sems.at[k],
                                                device_id=to, device_id_type=MESH)

        first = []
        for j, (ox, oy) in enumerate(others):
            for a in range(n_arr):
                cp = remote(n_arr * j + a, a, chip, c, (ox, oy, c), True)
                cp.start()
                first.append(cp)
        passed = []
        for j, (ox, oy) in enumerate(others):
            ochip = 2 * ox + oy
            for a in range(n_arr):
                k = n_arr * j + a
                remote(k, a, ochip, c, sibling, False).wait_recv()
                cp = remote(3 * n_arr + k, a, ochip, c, sibling, False)
                cp.start()
                passed.append(cp)
        for j, (ox, oy) in enumerate(others):
            ochip = 2 * ox + oy
            for a in range(n_arr):
                remote(3 * n_arr + n_arr * j + a, a, ochip, 1 - c, sibling, False).wait_recv()
        for cp in first + passed:
            cp.wait_send()
        for cp in stores:
            cp.wait()

    any_spec = pl.BlockSpec(memory_space=pl.ANY)
    return pl.pallas_call(
        body, name="gather_weights",
        in_specs=[any_spec] * n_arr, out_specs=[any_spec] * n_arr,
        out_shape=[jax.ShapeDtypeStruct(sh, BF16) for sh in full_shapes],
        scratch_shapes=[pltpu.VMEM(a.shape, F32) for a in shards] + [pltpu.VMEM(a.shape, BF16) for a in shards]
        + [pltpu.SemaphoreType.DMA((6 * n_arr,)), pltpu.SemaphoreType.DMA((6 * n_arr,)),
           pltpu.SemaphoreType.DMA((3 * n_arr,))],
        compiler_params=pltpu.CompilerParams(vmem_limit_bytes=VMEM_LIMIT, has_side_effects=True),
    )(*shards)


def _reduce_grads_tail(grads, g_small, early_slots, late_in_slots):
    n_big = len(grads)
    n_arr = n_big + 1
    shard_shapes = [(2 * r, w) for r, w in _GRAD_PIECE_SHAPES]
    small_piece = (SMALL_PIECE, LANES)

    def body(*refs):
        src = refs[:n_arr]
        early = refs[n_arr:n_arr + n_big]
        late_in = refs[n_arr + n_big]
        n_in = n_arr + n_big + 1
        out = refs[n_in:n_in + n_arr]
        slots = refs[n_in + n_arr:n_in + 2 * n_arr]
        sums = refs[n_in + 2 * n_arr:n_in + 3 * n_arr]
        send1, recv1, send2, recv2, local_sems = refs[n_in + 3 * n_arr:]
        x, y, c = _place()
        me = 4 * x + 2 * y + c

        def piece_of(a, dev):
            return src[a].at[dev] if a == n_big else _grad_piece(src[a], a, dev)

        def late(a, dst_dev, src_dev):
            return pltpu.make_async_remote_copy(
                src_ref=piece_of(a, dst_dev), dst_ref=slots[a].at[src_dev],
                send_sem=send1.at[n_arr * dst_dev + a], recv_sem=recv1.at[n_arr * src_dev + a],
                device_id=_dev_id(dst_dev), device_id_type=MESH)

        def late_arrays(dev):
            return (n_big,)

        def load(a, dev, received):
            return pltpu.make_async_copy(received.at[dev], slots[a].at[dev], local_sems.at[n_arr * dev + a])

        def own(a, dev):
            return pltpu.make_async_copy(piece_of(a, dev), slots[a].at[dev], local_sems.at[n_arr * dev + a])

        for dev in range(N_DEV):
            @pl.when(me == dev)
            def _():
                received = [early[0] if dev in _EARLY_IN_DEVS else late_in] + list(early[1:])
                for a in range(n_arr):
                    own(a, dev).start()
                filled = lambda a, peer: peer in _presum_sources(dev)
                for peer in range(N_DEV):
                    if peer != dev:
                        for a in late_arrays(peer):
                            late(a, peer, dev).start()
                        for a in range(n_big):
                            if filled(a, peer):
                                load(a, peer, received[a]).start()
                            else:
                                slots[a][peer] = jnp.zeros(slots[a].shape[1:], slots[a].dtype)
                for a in range(n_arr):
                    own(a, dev).wait()
                for peer in range(N_DEV):
                    if peer != dev:
                        for a in late_arrays(dev):
                            late(a, dev, peer).wait_recv()
                        for a in range(n_big):
                            if filled(a, peer):
                                load(a, peer, received[a]).wait()

        for a in range(n_arr):
            rows = slots[a].shape[1]
            step = 64 if rows % 64 == 0 else 8

            def add_rows(t, carry, a=a, step=step):
                r0 = pl.multiple_of(t * step, step)
                total = slots[a][0, pl.ds(r0, step), :].astype(F32)
                for dev in range(1, N_DEV):
                    total = total + slots[a][dev, pl.ds(r0, step), :].astype(F32)
                sums[a][pl.ds(r0, step), :] = total
                return carry

            lax.fori_loop(0, rows // step, add_rows, 0)

        shares = []
        keeps = []
        for a in range(n_big):
            r, w = _GRAD_PIECE_SHAPES[a]
            dst = out[a].at[pl.ds(pl.multiple_of(c * r, 8), r), :]
            cp = pltpu.make_async_remote_copy(src_ref=sums[a], dst_ref=dst, send_sem=send2.at[a], recv_sem=recv2.at[a],
                                              device_id=(x, y, 1 - c), device_id_type=MESH)
            cp.start()
            shares.append(cp)
            kp = pltpu.make_async_copy(sums[a], dst, local_sems.at[N_DEV * n_arr + a])
            kp.start()
            keeps.append(kp)
        kp = pltpu.make_async_copy(sums[n_big], out[n_big].at[me], local_sems.at[N_DEV * n_arr + n_big])
        kp.start()
        keeps.append(kp)

        def small_share(dst_dev, src_dev):
            return pltpu.make_async_remote_copy(src_ref=sums[n_big], dst_ref=out[n_big].at[src_dev],
                                                send_sem=send2.at[n_big + dst_dev], recv_sem=recv2.at[n_big + src_dev],
                                                device_id=_dev_id(dst_dev), device_id_type=MESH)

        for dev in range(N_DEV):
            @pl.when(me != dev)
            def _():
                small_share(dev, me).start()
        for a in range(n_big):
            r, w = _GRAD_PIECE_SHAPES[a]
            other = out[a].at[pl.ds(pl.multiple_of((1 - c) * r, 8), r), :]
            pltpu.make_async_remote_copy(src_ref=sums[a], dst_ref=other, send_sem=send2.at[a], recv_sem=recv2.at[a],
                                         device_id=(x, y, 1 - c), device_id_type=MESH).wait_recv()
        for dev in range(N_DEV):
            @pl.when(me != dev)
            def _():
                small_share(dev, dev).wait_recv()
                small_share(dev, me).wait_send()
                for a in late_arrays(dev):
                    late(a, dev, me).wait_send()
        for cp in shares:
            cp.wait_send()
        for kp in keeps:
            kp.wait()

    any_spec = pl.BlockSpec(memory_space=pl.ANY)
    return pl.pallas_call(
        body, name="reduce_grads_tail",
        in_specs=[any_spec] * (n_arr + n_big + 1), out_specs=[any_spec] * n_arr,
        out_shape=[jax.ShapeDtypeStruct(sh, F32) for sh in shard_shapes]
        + [jax.ShapeDtypeStruct((N_DEV,) + small_piece, F32)],
        scratch_shapes=[pltpu.VMEM((N_DEV,) + sh, BF16) for sh in _GRAD_PIECE_SHAPES]
        + [pltpu.VMEM((N_DEV,) + small_piece, F32)]
        + [pltpu.VMEM(sh, F32) for sh in _GRAD_PIECE_SHAPES] + [pltpu.VMEM(small_piece, F32)]
        + [pltpu.SemaphoreType.DMA((N_DEV * n_arr,)), pltpu.SemaphoreType.DMA((N_DEV * n_arr,)),
           pltpu.SemaphoreType.DMA((n_big + N_DEV,)), pltpu.SemaphoreType.DMA((n_big + N_DEV,)),
           pltpu.SemaphoreType.DMA((N_DEV * n_arr + n_arr,))],
        compiler_params=pltpu.CompilerParams(vmem_limit_bytes=VMEM_LIMIT, has_side_effects=True),
    )(*grads, g_small, *early_slots, late_in_slots)


def _reduce_grads(g_in, g_up_a, g_up_b, g_out, g_small):
    big = (g_in, g_up_a, g_up_b, g_out)
    n_big = len(big)
    col_sharded = (True, True, True, False)
    piece_shapes = []
    for a, arr in enumerate(big):
        r, w = arr.shape
        piece_shapes.append((r // 2, w // N_CHIPS) if col_sharded[a] else (r // (2 * N_CHIPS), w))
    shard_shapes = [(2 * r, w) for r, w in piece_shapes]
    n_arr = n_big + 1

    def body(*refs):
        src = refs[:n_arr]
        out = refs[n_arr:2 * n_arr]
        slots = refs[2 * n_arr:3 * n_arr]
        sums = refs[3 * n_arr:4 * n_arr]
        send1, recv1, send2, recv2, local_sems = refs[4 * n_arr:]
        x, y, c = _place()
        me = 4 * x + 2 * y + c

        def piece_of(a, dev):
            chip_idx, half = dev // 2, dev % 2
            if a == n_big:
                return src[a].at[dev]
            r, w = piece_shapes[a]
            if col_sharded[a]:
                return src[a].at[pl.ds(pl.multiple_of(half * r, 16), r), pl.ds(pl.multiple_of(chip_idx * w, LANES), w)]
            return src[a].at[pl.ds(pl.multiple_of(dev * r, 16), r), :]

        def dev_id(dev):
            return (dev // 4, (dev // 2) % 2, dev % 2)

        own = [pltpu.make_async_copy(piece_of(a, me), slots[a].at[me], local_sems.at[a]) for a in range(n_arr)]
        for cp in own:
            cp.start()
        sends = []
        for d in range(1, N_DEV):
            peer = (me + d) % N_DEV
            for a in range(n_arr):
                cp = pltpu.make_async_remote_copy(
                    src_ref=piece_of(a, peer), dst_ref=slots[a].at[me],
                    send_sem=send1.at[n_arr * peer + a], recv_sem=recv1.at[n_arr * me + a],
                    device_id=dev_id(peer), device_id_type=MESH)
                cp.start()
                sends.append(cp)
        for cp in own:
            cp.wait()
        for d in range(1, N_DEV):
            peer = (me + d) % N_DEV
            for a in range(n_arr):
                pltpu.make_async_remote_copy(
                    src_ref=piece_of(a, peer), dst_ref=slots[a].at[peer],
                    send_sem=send1.at[n_arr * peer + a], recv_sem=recv1.at[n_arr * peer + a],
                    device_id=dev_id(peer), device_id_type=MESH).wait_recv()
        for a in range(n_arr):
            rows = slots[a].shape[1]
            step = 64 if rows % 64 == 0 else 8

            def add_rows(t, carry, a=a, step=step):
                r0 = pl.multiple_of(t * step, step)
                total = slots[a][0, pl.ds(r0, step), :].astype(F32)
                for dev in range(1, N_DEV):
                    total = total + slots[a][dev, pl.ds(r0, step), :].astype(F32)
                sums[a][pl.ds(r0, step), :] = total
                return carry

            lax.fori_loop(0, rows // step, add_rows, 0)
        shares = []
        keeps = []
        for a in range(n_big):
            r, w = piece_shapes[a]
            dst = out[a].at[pl.ds(pl.multiple_of(c * r, 8), r), :]
            cp = pltpu.make_async_remote_copy(src_ref=sums[a], dst_ref=dst, send_sem=send2.at[a], recv_sem=recv2.at[a],
                                              device_id=(x, y, 1 - c), device_id_type=MESH)
            cp.start()
            shares.append(cp)
            kp = pltpu.make_async_copy(sums[a], dst, local_sems.at[n_arr + a])
            kp.start()
            keeps.append(kp)
        kp = pltpu.make_async_copy(sums[n_big], out[n_big].at[me], local_sems.at[n_arr + n_big])
        kp.start()
        keeps.append(kp)
        for d in range(1, N_DEV):
            peer = (me + d) % N_DEV
            cp = pltpu.make_async_remote_copy(src_ref=sums[n_big], dst_ref=out[n_big].at[me],
                                              send_sem=send2.at[n_big + peer], recv_sem=recv2.at[n_big + me],
                                              device_id=dev_id(peer), device_id_type=MESH)
            cp.start()
            shares.append(cp)
        for a in range(n_big):
            r, w = piece_shapes[a]
            other = out[a].at[pl.ds(pl.multiple_of((1 - c) * r, 8), r), :]
            pltpu.make_async_remote_copy(src_ref=sums[a], dst_ref=other, send_sem=send2.at[a], recv_sem=recv2.at[a],
                                         device_id=(x, y, 1 - c), device_id_type=MESH).wait_recv()
        for d in range(1, N_DEV):
            peer = (me + d) % N_DEV
            pltpu.make_async_remote_copy(src_ref=sums[n_big], dst_ref=out[n_big].at[peer],
                                         send_sem=send2.at[n_big + peer], recv_sem=recv2.at[n_big + peer],
                                         device_id=dev_id(peer), device_id_type=MESH).wait_recv()
        for cp in sends + shares:
            cp.wait_send()
        for kp in keeps:
            kp.wait()

    any_spec = pl.BlockSpec(memory_space=pl.ANY)
    small_piece = (SMALL_PIECE, LANES)
    return pl.pallas_call(
        body, name="reduce_grads",
        in_specs=[any_spec] * n_arr, out_specs=[any_spec] * n_arr,
        out_shape=[jax.ShapeDtypeStruct(sh, F32) for sh in shard_shapes]
        + [jax.ShapeDtypeStruct((N_DEV,) + small_piece, F32)],
        scratch_shapes=[pltpu.VMEM((N_DEV,) + sh, BF16) for sh in piece_shapes]
        + [pltpu.VMEM((N_DEV,) + small_piece, F32)]
        + [pltpu.VMEM(sh, F32) for sh in piece_shapes] + [pltpu.VMEM(small_piece, F32)]
        + [pltpu.SemaphoreType.DMA((N_DEV * n_arr,)), pltpu.SemaphoreType.DMA((N_DEV * n_arr,)),
           pltpu.SemaphoreType.DMA((n_big + N_DEV,)), pltpu.SemaphoreType.DMA((n_big + N_DEV,)),
           pltpu.SemaphoreType.DMA((2 * n_arr,))],
        compiler_params=pltpu.CompilerParams(vmem_limit_bytes=VMEM_LIMIT, has_side_effects=True),
    )(*big, g_small)


_SMALL_PARTS = (("norm_g", 8), ("sgu_ln_g", 8), ("sgu_ln_b", 8), ("w_spatial", 1024), ("b_spatial", 8),
                ("final_norm_g", 8))
_LOSS_ROW = sum(n for _, n in _SMALL_PARTS)


def _pack_small(parts, loss_tile=None):
    rows = []
    for name, n_rows in _SMALL_PARTS:
        a = parts[name].reshape(-1, LANES).astype(F32)
        a = jnp.pad(a, ((0, n_rows - a.shape[0]), (0, 0)))
        rows.append(a)
    rows.append(jnp.zeros((8, LANES), F32) if loss_tile is None else loss_tile)
    rows.append(jnp.zeros((SMALL_ROWS - _LOSS_ROW - 8, LANES), F32))
    return jnp.concatenate(rows, axis=0)


def _unpack_small(packed, shapes):
    out = {}
    r0 = 0
    for name, n_rows in _SMALL_PARTS:
        n = math.prod(shapes[name])
        out[name] = packed[r0:r0 + n // LANES].reshape(shapes[name])
        r0 += n_rows
    return out


def _local_step(proj, ht, x, target, norm_g, w_in, sgu_ln_g, sgu_ln_b, w_spatial, b_spatial, w_up_a, w_up_b, w_out,
                final_norm_g, bq, bk):
    pos = jnp.arange(SGU_CHUNK)
    keep = (pos[None, :] // SGU_SUBCHUNK) <= (pos[:, None] // SGU_SUBCHUNK)
    w_mask = jnp.where(keep[None], w_spatial, 0.0).astype(BF16)
    w_mask_t = jnp.swapaxes(w_mask, 1, 2)
    bias_full = jnp.repeat(b_spatial.T, GROUP_DIM, axis=1)
    ln_g = sgu_ln_g.reshape(1, D_BRANCH)
    ln_b = sgu_ln_b.reshape(1, D_BRANCH)
    final_g = final_norm_g.reshape(1, D_MODEL)

    o, ya, rsave = _attn_fwd(proj, bq, bk, ATTN_PAIRS)
    yb = _sgu_fwd(proj, ln_g, ln_b, w_mask, bias_full)
    dzg, do, dyb, dx2, g_out, g_up_a, g_up_b, loss_acc, d_final = _mid(
        proj, ya, yb, o, x, target, final_g, w_up_a, w_up_b, w_out)
    dsgu, d_wsp, d_bsp, d_lng, d_lnb = _sgu_bwd(proj, dyb, ln_g, ln_b, w_mask, w_mask_t, bias_full)
    g_in = _dwin_piece(ht, dzg, lambda j: jnp.where(j == 0, COL_ZA, COL_GA - 1 + j), None)
    g_in = _dwin_piece(ht, dsgu, lambda j: COL_UB + j, g_in)
    dq, dk, dv, *early_slots = _attn_bwd(proj, do, rsave, bq, bk, ATTN_PAIRS, (g_in, g_up_a, g_up_b, g_out))
    g_in = _dwin_piece(ht, dq, lambda j: COL_Q + j, g_in)
    g_in = _dwin_piece(ht, dk, lambda j: COL_K + j, g_in)
    g_in = _dwin_piece(ht, dv, lambda j: COL_V + j, g_in)
    pieces = [(dq, COL_Q * D_BRANCH, 0, D_BRANCH), (dk, COL_K * D_BRANCH, 0, D_BRANCH),
              (dv, COL_V * D_BRANCH, 0, D_BRANCH), (dzg, COL_ZA * D_BRANCH, 0, D_BRANCH),
              (dsgu, COL_UB * D_BRANCH, 0, 3 * D_BRANCH), (dzg, COL_GA * D_BRANCH, D_BRANCH, 2 * D_MODEL)]
    dx, d_norm, late_in_slots = _dh_dx(pieces, w_in, x, norm_g, dx2, g_in)
    small = {"norm_g": d_norm, "sgu_ln_g": d_lng, "sgu_ln_b": d_lnb, "w_spatial": d_wsp,
             "b_spatial": d_bsp[:, :N_GROUPS].T, "final_norm_g": d_final}
    return loss_acc, dx, (g_in, g_up_a, g_up_b, g_out), small, early_slots, late_in_slots


def kernel(x, norm_g, w_in, sgu_ln_g, sgu_ln_b, w_spatial, b_spatial, w_up_a, w_up_b, w_out, final_norm_g, loss_target, m_norm_g, m_w_in, m_sgu_ln_g, m_sgu_ln_b, m_w_spatial, m_b_spatial, m_w_up_a, m_w_up_b, m_w_out, m_final_norm_g, v_norm_g, v_w_in, v_sgu_ln_g, v_sgu_ln_b, v_w_spatial, v_b_spatial, v_w_up_a, v_w_up_b, v_w_out, v_final_norm_g):
    big_names = ("w_in", "w_up_a", "w_up_b", "w_out")
    small_names = tuple(n for n, _ in _SMALL_PARTS)
    names = ("norm_g", "w_in", "sgu_ln_g", "sgu_ln_b", "w_spatial", "b_spatial", "w_up_a", "w_up_b", "w_out",
             "final_norm_g")
    w = dict(norm_g=norm_g, w_in=w_in, sgu_ln_g=sgu_ln_g, sgu_ln_b=sgu_ln_b, w_spatial=w_spatial,
             b_spatial=b_spatial, w_up_a=w_up_a, w_up_b=w_up_b, w_out=w_out, final_norm_g=final_norm_g)
    m = dict(norm_g=m_norm_g, w_in=m_w_in, sgu_ln_g=m_sgu_ln_g, sgu_ln_b=m_sgu_ln_b, w_spatial=m_w_spatial,
             b_spatial=m_b_spatial, w_up_a=m_w_up_a, w_up_b=m_w_up_b, w_out=m_w_out, final_norm_g=m_final_norm_g)
    v = dict(norm_g=v_norm_g, w_in=v_w_in, sgu_ln_g=v_sgu_ln_g, sgu_ln_b=v_sgu_ln_b, w_spatial=v_w_spatial,
             b_spatial=v_b_spatial, w_up_a=v_w_up_a, w_up_b=v_w_up_b, w_out=v_w_out, final_norm_g=v_final_norm_g)
    shapes = {n: w[n].shape for n in names}
    flat2d = lambda a: a.reshape(a.shape[-2:])

    proj, ht, *full = _in_proj_gather(x[0], norm_g, *[flat2d(w[n]) for n in big_names])
    loss, dx, big_grads, small, early_slots, late_in_slots = _local_step(
        proj, ht, x[0], loss_target[0], norm_g, full[0], sgu_ln_g[0], sgu_ln_b[0], w_spatial[0], b_spatial[0],
        full[1], full[2], full[3], final_norm_g, ATTN_Q_BLOCK, ATTN_K_BLOCK)
    packed = _pack_small(small, loss).reshape(N_DEV, SMALL_PIECE, LANES)
    red = _reduce_grads_tail(big_grads, packed, early_slots, late_in_slots)

    grads, deltas, new_m, new_v = {}, {}, {}, {}
    for n, g in zip(big_names, red[:4]):
        g, d, nm, nv = _adamw(flat2d(w[n]), g, flat2d(m[n]), flat2d(v[n]))
        grads[n], deltas[n], new_m[n], new_v[n] = (a.reshape(shapes[n]) for a in (g, d, nm, nv))
    g_small = red[4].reshape(SMALL_ROWS, LANES)
    g_small, d, nm, nv = _adamw(_pack_small({n: w[n] for n in small_names}), g_small,
                                _pack_small({n: m[n] for n in small_names}),
                                _pack_small({n: v[n] for n in small_names}))
    for src, dst in ((g_small, grads), (d, deltas), (nm, new_m), (nv, new_v)):
        dst.update(_unpack_small(src, shapes))

    return (g_small[_LOSS_ROW, 0], dx[None], *[grads[n] for n in names], *[deltas[n] for n in names],
            *[new_m[n] for n in names], *[new_v[n] for n in names])
```

```python
import math

import jax
import jax.numpy as jnp
from jax import lax
from jax.experimental import pallas as pl
from jax.experimental.pallas import tpu as pltpu

F32 = jnp.float32
BF16 = jnp.bfloat16

D_MODEL = 1024
N_HEADS = 8
HEAD_DIM = 64
D_BRANCH = 512
D_IN = 4 * D_BRANCH + 3 * D_BRANCH + 2 * D_MODEL
N_GROUPS = 8
GROUP_DIM = 64
SGU_CHUNK = 128
SGU_SUBCHUNK = 64
GROUP_SHIFT = 6
EPS = 1e-6
LANES = 128
ATTN_Q_BLOCK = 256
ATTN_K_BLOCK = 256
DEAD = -110.0
SKIPPED = -1e30
SCAN_PASSES = 1
ATTN_PAIRS = 2
N_CHIPS = 4
N_DEV = 8
MESH = pl.DeviceIdType.MESH

ADAM_LR = 0.001
ADAM_B1 = 0.9
ADAM_B2 = 0.999
ADAM_EPS = 1e-08
ADAM_WD = 0.01
ADAM_STEP = 10

COL_Q, COL_K, COL_V, COL_ZA, COL_UB, COL_VB, COL_ZB, COL_GA, COL_GB = 0, 1, 2, 3, 4, 5, 6, 7, 9

VMEM_LIMIT = 56 * 1024 * 1024

SMALL_ROWS = 1088
SMALL_PIECE = SMALL_ROWS // N_DEV


def _cparams(sem=None):
    return pltpu.CompilerParams(dimension_semantics=sem, vmem_limit_bytes=VMEM_LIMIT)


def _aligned(v, m):
    return v if isinstance(v, int) else pl.multiple_of(v, m)


def _sigmoid(x):
    return 1.0 / (1.0 + jnp.exp(-x))


def _gelu_and_grad(x):
    k = math.sqrt(2.0 / math.pi)
    x2 = x * x
    inner = k * (x + 0.044715 * x * x2)
    th = jnp.tanh(inner)
    g = 0.5 * x * (1.0 + th)
    dg = 0.5 * (1.0 + th) + 0.5 * x * (1.0 - th * th) * (k * (1.0 + 3.0 * 0.044715 * x2))
    return g, dg


def _split_dot(a, b_bf16, passes):
    out = None
    rem = a
    for _ in range(passes):
        part = rem.astype(BF16)
        d = jnp.dot(part, b_bf16, preferred_element_type=F32)
        out = d if out is None else out + d
        rem = rem - part.astype(F32)
    return out


def _dot_nt(a, b):
    return lax.dot_general(a, b, (((1,), (1,)), ((), ())), preferred_element_type=F32)


def _dot_tn(a, b):
    return lax.dot_general(a, b, (((0,), (0,)), ((), ())), preferred_element_type=F32)


def _place():
    x, y, c = lax.axis_index("x"), lax.axis_index("y"), lax.axis_index("c")
    return x, y, c


def _in_proj_gather(x, norm_g, w_in, w_up_a, w_up_b, w_out):
    s = x.shape[0]
    tm = min(1024, s)
    nt = s // tm
    shards = (w_in, w_up_a, w_up_b, w_out)
    n_arr = len(shards)
    col_sharded = (True, True, True, False)
    full_shapes = ((D_MODEL, D_IN), (D_BRANCH, D_MODEL), (D_BRANCH, D_MODEL), (D_MODEL, D_MODEL))
    w_shard = w_in.shape[1]
    half_rows = D_MODEL // 2
    stage_rows = 256

    def body(order_ref, x_ref, g_ref, *refs):
        src = refs[:n_arr]
        proj_ref, ht_ref = refs[n_arr:n_arr + 2]
        out = refs[n_arr + 2:2 * n_arr + 2]
        wsc, h_scr, stage = refs[2 * n_arr + 2:2 * n_arr + 5]
        small_stage = refs[2 * n_arr + 5:2 * n_arr + 8]
        small_cast = refs[2 * n_arr + 8:2 * n_arr + 11]
        send_sems, recv_sems, local_sems = refs[2 * n_arr + 11:]
        k = pl.program_id(0)
        i = pl.program_id(1)
        x_, y_, c = _place()
        chip = 2 * x_ + y_
        sibling = (x_, y_, 1 - c)
        others = [(x_, 1 - y_), (1 - x_, y_), (1 - x_, 1 - y_)]

        def region(a, chip_idx, half):
            if a == 0:
                return wsc.at[chip_idx, pl.ds(_aligned(half * half_rows, 16), half_rows), :]
            r, w = shards[a].shape
            hr = r // 2
            if col_sharded[a]:
                return out[a].at[pl.ds(_aligned(half * hr, 16), hr), pl.ds(_aligned(chip_idx * w, LANES), w)]
            return out[a].at[pl.ds(_aligned(chip_idx * r + half * hr, 16), hr), :]

        def remote(kk, a, chip_idx, half, to, own):
            s_ref = region(a, chip_idx, half)
            if own and a > 0:
                hr = shards[a].shape[0] // 2
                s_ref = small_cast[a - 1].at[pl.ds(_aligned(half * hr, 16), hr), :]
            return pltpu.make_async_remote_copy(src_ref=s_ref, dst_ref=region(a, chip_idx, half),
                                                send_sem=send_sems.at[kk], recv_sem=recv_sems.at[kk],
                                                device_id=to, device_id_type=MESH)

        def keep_whole(kk, chip_idx):
            return pltpu.make_async_copy(wsc.at[chip_idx],
                                         out[0].at[:, pl.ds(_aligned(chip_idx * w_shard, LANES), w_shard)],
                                         local_sems.at[kk])

        def small_stores():
            cps = []
            for a in range(1, n_arr):
                hr = shards[a].shape[0] // 2
                for half in range(2):
                    cps.append(pltpu.make_async_copy(small_cast[a - 1].at[pl.ds(half * hr, hr), :],
                                                     region(a, chip, half), local_sems.at[4 + 2 * (a - 1) + half]))
            return cps

        def arrive_and_pass(j):
            ochip = chip ^ j
            for a in range(n_arr):
                kk = n_arr * (j - 1) + a
                remote(kk, a, ochip, c, sibling, False).wait_recv()
                remote(3 * n_arr + kk, a, ochip, c, sibling, False).start()

        def from_sibling(j, a):
            remote(3 * n_arr + n_arr * (j - 1) + a, a, chip ^ j, 1 - c, sibling, False).wait_recv()

        @pl.when((k == 0) & (i == 0))
        def _():
            def cast_rows(half):
                for t in range(half_rows // stage_rows):
                    r0 = pl.multiple_of(half * half_rows + t * stage_rows, stage_rows)
                    pltpu.sync_copy(src[0].at[pl.ds(r0, stage_rows), :], stage)
                    wsc[chip, pl.ds(r0, stage_rows), :] = stage[...].astype(BF16)

            cast_rows(c)
            for j in (1, 2):
                remote(n_arr * (j - 1), 0, chip, c, (*others[j - 1], c), True).start()
            cast_rows(1 - c)
            for a in range(1, n_arr):
                pltpu.sync_copy(src[a], small_stage[a - 1])
                small_cast[a - 1][...] = small_stage[a - 1][...].astype(BF16)
            for j in (1, 2):
                for a in range(1, n_arr):
                    remote(n_arr * (j - 1) + a, a, chip, c, (*others[j - 1], c), True).start()
            keep_whole(0, chip).start()
            for cp in small_stores():
                cp.start()

        @pl.when((k == 1) & (i == 0))
        def _():
            for j in (1, 2):
                remote(n_arr * (j - 1), 0, chip, c, (*others[j - 1], c), True).wait_send()
            for a in range(n_arr):
                remote(n_arr * 2 + a, a, chip, c, (*others[2], c), True).start()
            arrive_and_pass(1)
            arrive_and_pass(2)
            from_sibling(1, 0)
            keep_whole(1, chip ^ 1).start()

        @pl.when((k == 2) & (i == 0))
        def _():
            from_sibling(2, 0)
            keep_whole(2, chip ^ 2).start()
            arrive_and_pass(3)

        @pl.when((k == 3) & (i == 0))
        def _():
            from_sibling(3, 0)
            keep_whole(3, chip ^ 3).start()

        @pl.when(k == 0)
        def _():
            xf = x_ref[...]
            r = lax.rsqrt(jnp.mean(xf * xf, axis=-1, keepdims=True) + EPS)
            h = xf * r * g_ref[...]
            h_scr[i] = h.astype(BF16)
            ht_ref[...] = h.T.astype(BF16)

        proj_ref[...] = jnp.dot(h_scr[i], wsc[order_ref[k]], preferred_element_type=F32).astype(BF16)

        @pl.when((k == 3) & (i == nt - 1))
        def _():
            for j in (1, 2, 3):
                for a in range(1, n_arr):
                    from_sibling(j, a)
            for j in (1, 2, 3):
                for a in range(n_arr):
                    kk = n_arr * (j - 1) + a
                    if a > 0 or j == 3:
                        remote(kk, a, chip, c, (*others[j - 1], c), True).wait_send()
                    remote(3 * n_arr + kk, a, chip ^ j, c, sibling, False).wait_send()
            for kk in range(4):
                keep_whole(kk, chip ^ kk).wait()
            for cp in small_stores():
                cp.wait()

    any_spec = pl.BlockSpec(memory_space=pl.ANY)
    tile = lambda kk, ii: jnp.where(kk == 0, ii, nt - 1)
    grid_spec = pltpu.PrefetchScalarGridSpec(
        num_scalar_prefetch=1, grid=(N_CHIPS, nt),
        in_specs=[pl.BlockSpec((tm, D_MODEL), lambda kk, ii, order: (tile(kk, ii), 0)),
                  pl.BlockSpec((1, D_MODEL), lambda kk, ii, order: (0, 0))] + [any_spec] * n_arr,
        out_specs=[pl.BlockSpec((tm, w_shard), lambda kk, ii, order: (ii, order[kk])),
                   pl.BlockSpec((D_MODEL, tm), lambda kk, ii, order: (0, tile(kk, ii)))] + [any_spec] * n_arr,
        scratch_shapes=[pltpu.VMEM((N_CHIPS, D_MODEL, w_shard), BF16), pltpu.VMEM((nt, tm, D_MODEL), BF16),
                        pltpu.VMEM((stage_rows, w_shard), F32)]
        + [pltpu.VMEM(a.shape, F32) for a in shards[1:]] + [pltpu.VMEM(a.shape, BF16) for a in shards[1:]]
        + [pltpu.SemaphoreType.DMA((6 * n_arr,)), pltpu.SemaphoreType.DMA((6 * n_arr,)),
           pltpu.SemaphoreType.DMA((4 + 2 * (n_arr - 1),))])
    x_, y_, _ = _place()
    order = (2 * x_ + y_) ^ jnp.arange(N_CHIPS, dtype=jnp.int32)
    return pl.pallas_call(
        body, name="in_proj_gather", grid_spec=grid_spec,
        out_shape=[jax.ShapeDtypeStruct((s, D_IN), BF16), jax.ShapeDtypeStruct((D_MODEL, s), BF16)]
        + [jax.ShapeDtypeStruct(sh, BF16) for sh in full_shapes],
        compiler_params=pltpu.CompilerParams(dimension_semantics=("arbitrary", "arbitrary"),
                                             vmem_limit_bytes=VMEM_LIMIT, has_side_effects=True),
    )(order, x, norm_g, *shards)


def _neg_softplus_parts(z):
    zb = z.astype(BF16)
    p = jnp.exp(-jnp.abs(zb))
    return p, jnp.maximum(zb, jnp.zeros_like(zb)) + jnp.log(1.0 + p)


def _split_cat(a, passes):
    parts = []
    rem = a
    for k in range(passes):
        part = rem.astype(BF16)
        parts.append(part)
        if k + 1 < passes:
            rem = rem - part.astype(F32)
    return parts[0] if passes == 1 else jnp.concatenate(parts, axis=1)


def _tri(blk, upper, sign):
    row = lax.broadcasted_iota(jnp.int32, (blk, blk), 0)
    col = lax.broadcasted_iota(jnp.int32, (blk, blk), 1)
    keep = (row <= col) if upper else (row >= col)
    t = jnp.where(keep, sign, 0.0).astype(BF16)
    return t if SCAN_PASSES == 1 else jnp.concatenate([t] * SCAN_PASSES, axis=0)


def _attn_fwd(proj, bq, bk, npairs):
    s = proj.shape[0]
    nq = s // bq
    ratio = bq // bk
    scale = HEAD_DIM ** -0.5
    heads = tuple(range(2 * npairs))
    width = LANES * npairs

    def body(q_ref, k_ref, v_ref, za_ref, o_ref, ya_ref, rs_ref, acc_ref, r_ref, z_ref):
        i = pl.program_id(1)
        lane = lax.broadcasted_iota(jnp.int32, (bq, LANES), 1)
        lo_half = lane < HEAD_DIM
        qm = []
        for pr in range(npairs):
            q = q_ref[:, LANES * pr:LANES * (pr + 1)] * jnp.asarray(scale, BF16)
            zero = jnp.zeros_like(q)
            qm += [jnp.where(lo_half, q, zero), jnp.where(lo_half, zero, q)]
        row = lax.broadcasted_iota(jnp.int32, (bq, bk), 0)
        col = lax.broadcasted_iota(jnp.int32, (bq, bk), 1)
        tneg = _tri(bk, False, -1.0)
        acc_ref[...] = jnp.zeros_like(acc_ref)
        r_ref[...] = jnp.zeros_like(r_ref)
        rs_ref[...] = jnp.full_like(rs_ref, SKIPPED)

        def scores(j):
            ks = pl.multiple_of(j * bk, bk)
            return [_dot_nt(qm[h], k_ref[pl.ds(ks, bk), LANES * (h // 2):LANES * (h // 2 + 1)]) for h in heads]

        for h, zh in enumerate(scores(i * ratio + ratio - 1)):
            z_ref[h] = zh

        def block(j, diag):
            ks = pl.multiple_of(j * bk, bk)
            vj = [v_ref[pl.ds(ks, bk), LANES * pr:LANES * (pr + 1)] for pr in range(npairs)]
            if diag:
                before = (j * bk + col) < (i * bq + row)
            z = [z_ref[h] for h in heads]
            sp = [_neg_softplus_parts(z[h])[1] for h in heads]
            if diag:
                sp = [jnp.where(before, sp[h], 0.0) for h in heads]
            cin = [jnp.dot(_split_cat(sp[h], SCAN_PASSES), tneg, preferred_element_type=F32) for h in heads]
            for h, zh in enumerate(scores(jnp.maximum(j - 1, 0))):
                z_ref[h] = zh
            w = [jnp.exp(z[h] + cin[h]) for h in heads]
            if diag:
                w = [jnp.where(before, w[h], 0.0) for h in heads]
            pv = [jnp.dot(w[h].astype(BF16), vj[h // 2], preferred_element_type=F32) for h in heads]
            r = [r_ref[h] for h in heads]
            for h in heads:
                acc_ref[h] += pv[h] * jnp.exp(r[h])
                r_ref[h] = r[h] + cin[h][:, 0:1]
            for pr in range(npairs):
                rs_ref[pr] = jnp.where(lane == j, r[2 * pr], jnp.where(lane == j + HEAD_DIM, r[2 * pr + 1], rs_ref[pr]))

        for t in range(ratio):
            block(i * ratio + ratio - 1 - t, True)

        def alive(carry):
            jj, r_max = carry
            return (jj < i * ratio) & (r_max > DEAD)

        def loop_body(carry):
            jj, _ = carry
            block(i * ratio - 1 - jj, False)
            return jj + 1, jnp.max(r_ref[...])

        lax.while_loop(alive, loop_body, (0, jnp.max(r_ref[...])))
        for pr in range(npairs):
            cols = slice(LANES * pr, LANES * (pr + 1))
            o = jnp.where(lo_half, acc_ref[2 * pr], acc_ref[2 * pr + 1])
            o_ref[:, cols] = o.astype(BF16)
            za = za_ref[:, cols].astype(F32)
            ya_ref[:, cols] = (o * (za * _sigmoid(za))).astype(BF16)

    n_steps = N_HEADS // (2 * npairs)
    return pl.pallas_call(
        body, name="attn_fwd", grid=(n_steps, nq),
        in_specs=[pl.BlockSpec((bq, width), lambda p, i: (i, n_steps * COL_Q + p)),
                  pl.BlockSpec((s, width), lambda p, i: (0, n_steps * COL_K + p)),
                  pl.BlockSpec((s, width), lambda p, i: (0, n_steps * COL_V + p)),
                  pl.BlockSpec((bq, width), lambda p, i: (i, n_steps * COL_ZA + p))],
        out_specs=[pl.BlockSpec((bq, width), lambda p, i: (i, p)),
                   pl.BlockSpec((bq, width), lambda p, i: (i, p)),
                   pl.BlockSpec((npairs, bq, LANES), lambda p, i: (p, i, 0))],
        out_shape=[jax.ShapeDtypeStruct((s, D_BRANCH), BF16), jax.ShapeDtypeStruct((s, D_BRANCH), BF16),
                   jax.ShapeDtypeStruct((N_HEADS // 2, s, LANES), F32)],
        scratch_shapes=[pltpu.VMEM((2 * npairs, bq, LANES), F32), pltpu.VMEM((2 * npairs, bq, 1), F32),
                        pltpu.VMEM((2 * npairs, bq, bk), F32)],
        compiler_params=_cparams(("parallel", "parallel")),
    )(proj, proj, proj, proj)


_GRAD_COL_SHARDED = (True, True, True, False)
_GRAD_FULL_SHAPES = ((D_MODEL, D_IN), (D_BRANCH, D_MODEL), (D_BRANCH, D_MODEL), (D_MODEL, D_MODEL))
_GRAD_PIECE_SHAPES = tuple((r // 2, w // N_CHIPS) if cs else (r // (2 * N_CHIPS), w)
                           for (r, w), cs in zip(_GRAD_FULL_SHAPES, _GRAD_COL_SHARDED))
_EARLY_IN_DEVS = (4, 5, 6, 7)
_LATE_IN_DEVS = (0, 1, 2, 3)
_LATE_CHIPS = (0, 1)
_EARLY_CHIPS = (2, 3)
_ALL_CHIPS = (0, 1, 2, 3)


def _grad_piece(ref, a, dev):
    r, w = _GRAD_PIECE_SHAPES[a]
    if _GRAD_COL_SHARDED[a]:
        return ref.at[pl.ds((dev % 2) * r, r), pl.ds((dev // 2) * w, w)]
    return ref.at[pl.ds(dev * r, r), :]


def _dev_id(dev):
    return (dev // 4, (dev // 2) % 2, dev % 2)


def _me():
    return 4 * lax.axis_index("x") + 2 * lax.axis_index("y") + lax.axis_index("c")


def _presum_copy(src, dst, send_sem, recv_sem, to_dev):
    return pltpu.make_async_remote_copy(src_ref=src, dst_ref=dst, send_sem=send_sem, recv_sem=recv_sem,
                                        device_id=_dev_id(to_dev), device_id_type=MESH)


def _presum_hand_off(dev, a, dest_chips, g_ref, slots, pair, send_sems, recv_sems):
    chip, core = dev // 2, dev % 2
    cps = []
    for k, q in enumerate(dest_chips):
        piece = _grad_piece(g_ref, a, 2 * q + 1 - core)
        if q == chip:
            cps.append(_presum_copy(piece, slots.at[dev], send_sems.at[N_DEV + k], recv_sems.at[dev], dev ^ 1))
        else:
            cps.append(_presum_copy(piece, pair.at[k], send_sems.at[N_DEV + k], recv_sems.at[N_DEV + k], dev ^ 1))
    return cps


def _presum_sends(dev, a, dest_chips, slots, sums, send_sems, recv_sems):
    chip, core = dev // 2, dev % 2
    return [_presum_copy(sums.at[k], slots.at[dev], send_sems.at[2 * q + core], recv_sems.at[dev], 2 * q + core)
            for k, q in enumerate(dest_chips) if q != chip]


def _presum_loads(dev, a, dest_chips, g_ref, stage, load_sems):
    chip, core = dev // 2, dev % 2
    return [pltpu.make_async_copy(_grad_piece(g_ref, a, 2 * q + core), stage.at[k], load_sems.at[k])
            for k, q in enumerate(dest_chips) if q != chip]


def _presum_send(dev, a, dest_chips, g_ref, slots, pair, stage, sums, send_sems, recv_sems, load_sems):
    chip, core = dev // 2, dev % 2
    hand = _presum_hand_off(dev, a, dest_chips, g_ref, slots, pair, send_sems, recv_sems)
    for cp in _presum_loads(dev, a, dest_chips, g_ref, stage, load_sems):
        cp.wait()
    for k, q in enumerate(dest_chips):
        if q != chip:
            hand[k].wait_recv()
            sums[k] = (stage[k].astype(F32) + pair[k].astype(F32)).astype(BF16)
    for cp in _presum_sends(dev, a, dest_chips, slots, sums, send_sems, recv_sems):
        cp.start()


def _presum_wait(dev, a, dest_chips, g_ref, slots, pair, sums, send_sems, recv_sems):
    chip, core = dev // 2, dev % 2
    for cp in _presum_hand_off(dev, a, dest_chips, g_ref, slots, pair, send_sems, recv_sems):
        cp.wait_send()
    for cp in _presum_sends(dev, a, dest_chips, slots, sums, send_sems, recv_sems):
        cp.wait_send()
    if chip in dest_chips:
        for src_dev in _presum_sources(dev):
            _presum_copy(sums.at[0], slots.at[src_dev], send_sems.at[src_dev], recv_sems.at[src_dev], src_dev).wait_recv()


def _presum_sources(dev):
    return [dev ^ 1] + [2 * r + dev % 2 for r in range(N_CHIPS) if r != dev // 2]


def _presum_scratch(a, dest_chips):
    n = len(dest_chips)
    piece = _GRAD_PIECE_SHAPES[a]
    return [pltpu.VMEM((n,) + piece, BF16), pltpu.VMEM((n,) + piece, BF16), pltpu.VMEM((n,) + piece, BF16),
            pltpu.SemaphoreType.DMA((N_DEV + n,)), pltpu.SemaphoreType.DMA((N_DEV + n,)),
            pltpu.SemaphoreType.DMA((n,))]


PRESUM_SCRATCH = 6


def _presum_program(first, second, last, a, dest_chips, g_ref, slots, scratch):
    pair, stage, sums, send_sems, recv_sems, load_sems = scratch
    me = _me()

    @pl.when(first)
    def _():
        for dev in range(N_DEV):
            @pl.when(me == dev)
            def _():
                for cp in _presum_hand_off(dev, a, dest_chips, g_ref, slots, pair, send_sems, recv_sems):
                    cp.start()
                for cp in _presum_loads(dev, a, dest_chips, g_ref, stage, load_sems):
                    cp.start()

    @pl.when(second)
    def _():
        for dev in range(N_DEV):
            @pl.when(me == dev)
            def _():
                _presum_send(dev, a, dest_chips, g_ref, slots, pair, stage, sums, send_sems, recv_sems, load_sems)

    def finish():
        for dev in range(N_DEV):
            @pl.when(me == dev)
            def _():
                _presum_wait(dev, a, dest_chips, g_ref, slots, pair, sums, send_sems, recv_sems)

    return finish


def _attn_bwd(proj, do, rsave, bq, bk, npairs, grads):
    plan = ((0, _EARLY_CHIPS), (1, _ALL_CHIPS), (2, _ALL_CHIPS), (3, _ALL_CHIPS))
    s = proj.shape[0]
    nq = s // bq
    ratio = bq // bk
    scale = HEAD_DIM ** -0.5
    heads = tuple(range(2 * npairs))
    width = LANES * npairs

    n_steps = N_HEADS // (2 * npairs)
    n_g = len(grads)

    def body(q_ref, k_ref, v_ref, do_ref, rs_ref, *refs):
        g_src = refs[:n_g]
        dq_ref, dk_ref, dv_ref = refs[n_g:n_g + 3]
        g_slots = refs[n_g + 3:2 * n_g + 3]
        dk_acc, dv_acc, dq_acc, e_ref = refs[2 * n_g + 3:2 * n_g + 7]
        i = pl.program_id(1)
        step = pl.program_id(0) * nq + i
        finish = [_presum_program(step == 0, step == 1, step == n_steps * nq - 1, a, chips, g_src[pos], g_slots[pos],
                                  refs[2 * n_g + 7 + PRESUM_SCRATCH * pos:2 * n_g + 7 + PRESUM_SCRATCH * (pos + 1)])
                  for pos, (a, chips) in enumerate(plan)]

        lane = lax.broadcasted_iota(jnp.int32, (bq, LANES), 1)
        lo_half = lane < HEAD_DIM
        qm, dom = [], []
        for pr in range(npairs):
            cols = slice(LANES * pr, LANES * (pr + 1))
            q = q_ref[:, cols] * jnp.asarray(scale, BF16)
            zero = jnp.zeros_like(q)
            qm += [jnp.where(lo_half, q, zero), jnp.where(lo_half, zero, q)]
            dout = do_ref[:, cols].astype(F32)
            dom += [jnp.where(lo_half, dout, 0.0), jnp.where(lo_half, 0.0, dout)]
        row = lax.broadcasted_iota(jnp.int32, (bq, bk), 0)
        col = lax.broadcasted_iota(jnp.int32, (bq, bk), 1)
        tneg = _tri(bk, False, -1.0)
        tfwd = _tri(bk, True, 1.0)

        @pl.when(i == 0)
        def _():
            dk_acc[...] = jnp.zeros_like(dk_acc)
            dv_acc[...] = jnp.zeros_like(dv_acc)

        dq_acc[...] = jnp.zeros_like(dq_acc)
        e_ref[...] = jnp.zeros_like(e_ref)

        def block(j, diag):
            ks = pl.multiple_of(j * bk, bk)
            kj = [k_ref[pl.ds(ks, bk), LANES * pr:LANES * (pr + 1)] for pr in range(npairs)]
            vj = [v_ref[pl.ds(ks, bk), LANES * pr:LANES * (pr + 1)] for pr in range(npairs)]
            if diag:
                before = (j * bk + col) < (i * bq + row)
            z = [_dot_nt(qm[h], kj[h // 2]) for h in heads]
            er = [jnp.exp(jnp.sum(jnp.where(lane == j + HEAD_DIM * (h % 2), rs_ref[h // 2], 0.0), axis=-1,
                                  keepdims=True)) for h in heads]
            dos = [(dom[h] * er[h]).astype(BF16) for h in heads]
            dw = [_dot_nt(dos[h], vj[h // 2]) for h in heads]
            psp = [_neg_softplus_parts(z[h]) for h in heads]
            sp = [psp[h][1] for h in heads]
            if diag:
                sp = [jnp.where(before, sp[h], 0.0) for h in heads]
            cin = [jnp.dot(_split_cat(sp[h], SCAN_PASSES), tneg, preferred_element_type=F32) for h in heads]
            w = [jnp.exp(z[h] + cin[h]) for h in heads]
            if diag:
                w = [jnp.where(before, w[h], 0.0) for h in heads]
            e =[dw[h] * w[h] for h in heads]
            eincl = [jnp.dot(_split_cat(e[h], SCAN_PASSES), tfwd, preferred_element_type=F32) + e_ref[h]
                     for h in heads]
            dz = []
            for h in heads:
                p = psp[h][0]
                beta = jnp.where(z[h] >= 0.0, 1.0, p) / (1.0 + p)
                d = e[h] - beta * eincl[h]
                dz.append((jnp.where(before, d, 0.0) if diag else d).astype(BF16))
            wb = [w[h].astype(BF16) for h in heads]
            for h in heads:
                e_ref[h] = eincl[h][:, bk - 1:bk]
                dq_acc[h] += jnp.dot(dz[h], kj[h // 2], preferred_element_type=F32)
            for pr in range(npairs):
                cols = slice(LANES * pr, LANES * (pr + 1))
                h0, h1 = 2 * pr, 2 * pr + 1
                dk_acc[pl.ds(ks, bk), cols] += _dot_tn(dz[h0], qm[h0]) + _dot_tn(dz[h1], qm[h1])
                dv_acc[pl.ds(ks, bk), cols] += _dot_tn(wb[h0], dos[h0]) + _dot_tn(wb[h1], dos[h1])

        def loop_body(j, carry):
            block(j, False)
            return carry

        block_of_lane = lane & (HEAD_DIM - 1)
        live = jnp.max(rs_ref[...], axis=0) > DEAD
        first_live = jnp.min(jnp.where(live, block_of_lane, nq * ratio))
        lax.fori_loop(jnp.minimum(first_live, i * ratio), i * ratio, loop_body, 0)
        for t in range(ratio):
            block(i * ratio + t, True)
        for pr in range(npairs):
            dq = jnp.where(lo_half, dq_acc[2 * pr], dq_acc[2 * pr + 1]) * scale
            dq_ref[:, LANES * pr:LANES * (pr + 1)] = dq.astype(BF16)

        @pl.when(i == nq - 1)
        def _():
            dk_ref[...] = dk_acc[...].astype(BF16)
            dv_ref[...] = dv_acc[...].astype(BF16)

        @pl.when(step == n_steps * nq - 1)
        def _():
            for fin in finish:
                fin()

    any_spec = pl.BlockSpec(memory_space=pl.ANY)
    return pl.pallas_call(
        body, name="attn_bwd", grid=(n_steps, nq),
        in_specs=[pl.BlockSpec((bq, width), lambda p, i: (i, n_steps * COL_Q + p)),
                  pl.BlockSpec((s, width), lambda p, i: (0, n_steps * COL_K + p)),
                  pl.BlockSpec((s, width), lambda p, i: (0, n_steps * COL_V + p)),
                  pl.BlockSpec((bq, width), lambda p, i: (i, p)),
                  pl.BlockSpec((npairs, bq, LANES), lambda p, i: (p, i, 0))] + [any_spec] * n_g,
        out_specs=[pl.BlockSpec((bq, width), lambda p, i: (i, p)),
                   pl.BlockSpec((s, width), lambda p, i: (0, p)),
                   pl.BlockSpec((s, width), lambda p, i: (0, p))] + [any_spec] * n_g,
        out_shape=[jax.ShapeDtypeStruct((s, D_BRANCH), BF16)] * 3
        + [jax.ShapeDtypeStruct((N_DEV,) + _GRAD_PIECE_SHAPES[a], BF16) for a, _ in plan],
        scratch_shapes=[pltpu.VMEM((s, width), F32), pltpu.VMEM((s, width), F32),
                        pltpu.VMEM((2 * npairs, bq, LANES), F32), pltpu.VMEM((2 * npairs, bq, 1), F32)]
        + [sh for a, chips in plan for sh in _presum_scratch(a, chips)],
        compiler_params=pltpu.CompilerParams(dimension_semantics=("arbitrary", "arbitrary"),
                                             vmem_limit_bytes=VMEM_LIMIT, has_side_effects=True),
    )(proj, proj, proj, do, rsave, *grads)


def _group_avg_matrix():
    a = lax.broadcasted_iota(jnp.int32, (LANES, LANES), 0) >> GROUP_SHIFT
    b = lax.broadcasted_iota(jnp.int32, (LANES, LANES), 1) >> GROUP_SHIFT
    return jnp.where(a == b, 1.0 / GROUP_DIM, 0.0).astype(BF16)


def _group_mean(a, avg):
    parts = [_split_dot(a[:, LANES * k:LANES * (k + 1)], avg, 3) for k in range(D_BRANCH // LANES)]
    return jnp.concatenate(parts, axis=1)


def _sgu_forward_parts(ub, vb, ln_g, ln_b, avg):
    ug, dug = _gelu_and_grad(ub)
    vg, dvg = _gelu_and_grad(vb)
    mu = _group_mean(vg, avg)
    d = vg - mu
    var = _group_mean(d * d, avg)
    rstd = lax.rsqrt(var + EPS)
    vhat = d * rstd
    vn = vhat * ln_g + ln_b
    return ug, dug, dvg, rstd, vhat, vn


def _sgu_mix(w_ref, src_bf16, n_chunks):
    lane = lax.broadcasted_iota(jnp.int32, (SGU_CHUNK, LANES), 1)
    lo_half = lane < GROUP_DIM
    rows = []
    for n in range(n_chunks):
        slabs = []
        for a in range(D_BRANCH // LANES):
            blk = src_bf16[SGU_CHUNK * n:SGU_CHUNK * (n + 1), LANES * a:LANES * (a + 1)]
            zero = jnp.zeros_like(blk)
            m0 = jnp.dot(w_ref[2 * a], jnp.where(lo_half, blk, zero), preferred_element_type=F32)
            m1 = jnp.dot(w_ref[2 * a + 1], jnp.where(lo_half, zero, blk), preferred_element_type=F32)
            slabs.append(m0 + m1)
        rows.append(jnp.concatenate(slabs, axis=1))
    return jnp.concatenate(rows, axis=0)


def _sgu_fwd(proj, ln_g, ln_b, w_mask, bias_full):
    s = proj.shape[0]
    tm = min(512, s)
    n_chunks = tm // SGU_CHUNK

    def body(ub_ref, vb_ref, zb_ref, g_ref, b_ref, w_ref, bias_ref, yb_ref):
        avg = _group_avg_matrix()
        ug, _, _, _, _, vn = _sgu_forward_parts(ub_ref[...].astype(F32), vb_ref[...].astype(F32),
                                                g_ref[...], b_ref[...], avg)
        mixed = _sgu_mix(w_ref, vn.astype(BF16), n_chunks) + jnp.concatenate([bias_ref[...]] * n_chunks, axis=0)
        zb = zb_ref[...].astype(F32)
        yb_ref[...] = (ug * mixed * (zb * _sigmoid(zb))).astype(BF16)

    col = lambda c: pl.BlockSpec((tm, D_BRANCH), lambda i: (i, c))
    full = lambda shape: pl.BlockSpec(shape, lambda i: (0,) * len(shape))
    return pl.pallas_call(
        body, name="sgu_fwd", grid=(s // tm,),
        in_specs=[col(COL_UB), col(COL_VB), col(COL_ZB), full((1, D_BRANCH)), full((1, D_BRANCH)),
                  full((N_GROUPS, SGU_CHUNK, SGU_CHUNK)), full((SGU_CHUNK, D_BRANCH))],
        out_specs=pl.BlockSpec((tm, D_BRANCH), lambda i: (i, 0)),
        out_shape=jax.ShapeDtypeStruct((s, D_BRANCH), BF16),
        compiler_params=_cparams(("parallel",)),
    )(proj, proj, proj, ln_g, ln_b, w_mask, bias_full)


def _sgu_bwd(proj, dyb, ln_g, ln_b, w_mask, w_mask_t, bias_full):
    s = proj.shape[0]
    tm = min(512, s)
    n_chunks = tm // SGU_CHUNK
    n_steps = s // tm

    def body(ub_ref, vb_ref, zb_ref, dyb_ref, g_ref, b_ref, w_ref, wt_ref, bias_ref,
             dsgu_ref, dw_ref, db_ref, dg_ref, dbeta_ref, dmix_acc):
        i = pl.program_id(0)

        @pl.when(i == 0)
        def _():
            dw_ref[...] = jnp.zeros_like(dw_ref)
            dg_ref[...] = jnp.zeros_like(dg_ref)
            dbeta_ref[...] = jnp.zeros_like(dbeta_ref)
            dmix_acc[...] = jnp.zeros_like(dmix_acc)

        avg = _group_avg_matrix()
        ln_gv = g_ref[...]
        ug, dug, dvg, rstd, vhat, vn = _sgu_forward_parts(ub_ref[...].astype(F32), vb_ref[...].astype(F32),
                                                          ln_gv, b_ref[...], avg)
        vnb = vn.astype(BF16)
        mixed = _sgu_mix(w_ref, vnb, n_chunks) + jnp.concatenate([bias_ref[...]] * n_chunks, axis=0)
        zb = zb_ref[...].astype(F32)
        sg = _sigmoid(zb)
        sz = zb * sg
        dsz = sg * (1.0 + zb * (1.0 - sg))
        dy = dyb_ref[...].astype(F32)
        dmixed = dy * ug * sz
        du = dy * mixed * sz * dug
        dzb = dy * ug * mixed * dsz
        dmb = dmixed.astype(BF16)
        dvn = _sgu_mix(wt_ref, dmb, n_chunks)

        lane = lax.broadcasted_iota(jnp.int32, (SGU_CHUNK, LANES), 1)
        lo_half = lane < GROUP_DIM
        dm_sum = None
        for n in range(n_chunks):
            rows = slice(SGU_CHUNK * n, SGU_CHUNK * (n + 1))
            dm_sum = dmixed[rows] if dm_sum is None else dm_sum + dmixed[rows]
            for a in range(D_BRANCH // LANES):
                cols = slice(LANES * a, LANES * (a + 1))
                dblk = dmb[rows, cols]
                vblk = vnb[rows, cols]
                zero = jnp.zeros_like(dblk)
                dw_ref[2 * a] += _dot_nt(jnp.where(lo_half, dblk, zero), vblk)
                dw_ref[2 * a + 1] += _dot_nt(jnp.where(lo_half, zero, dblk), vblk)
        dmix_acc[...] += dm_sum

        dg_ref[...] += jnp.sum(dvn * vhat, axis=0, keepdims=True)
        dbeta_ref[...] += jnp.sum(dvn, axis=0, keepdims=True)
        dvh = dvn * ln_gv
        m1 = _group_mean(dvh, avg)
        m2 = _group_mean(dvh * vhat, avg)
        dv = rstd * (dvh - m1 - vhat * m2) * dvg
        dsgu_ref[:, 0:D_BRANCH] = du.astype(BF16)
        dsgu_ref[:, D_BRANCH:2 * D_BRANCH] = dv.astype(BF16)
        dsgu_ref[:, 2 * D_BRANCH:3 * D_BRANCH] = dzb.astype(BF16)

        @pl.when(i == n_steps - 1)
        def _():
            pos = lax.broadcasted_iota(jnp.int32, (SGU_CHUNK, SGU_CHUNK), 0) >> GROUP_SHIFT
            src = lax.broadcasted_iota(jnp.int32, (SGU_CHUNK, SGU_CHUNK), 1) >> GROUP_SHIFT
            keep = src <= pos
            for g in range(N_GROUPS):
                dw_ref[g] = jnp.where(keep, dw_ref[g], 0.0)
            grp = lax.broadcasted_iota(jnp.int32, (D_BRANCH, LANES), 0) >> GROUP_SHIFT
            sel = (grp == lax.broadcasted_iota(jnp.int32, (D_BRANCH, LANES), 1)).astype(BF16)
            db_ref[...] = _split_dot(dmix_acc[...], sel, 3)

    col = lambda c: pl.BlockSpec((tm, D_BRANCH), lambda i: (i, c))
    full = lambda shape: pl.BlockSpec(shape, lambda i: (0,) * len(shape))
    return pl.pallas_call(
        body, name="sgu_bwd", grid=(n_steps,),
        in_specs=[col(COL_UB), col(COL_VB), col(COL_ZB), pl.BlockSpec((tm, D_BRANCH), lambda i: (i, 0)),
                  full((1, D_BRANCH)), full((1, D_BRANCH)),
                  full((N_GROUPS, SGU_CHUNK, SGU_CHUNK)), full((N_GROUPS, SGU_CHUNK, SGU_CHUNK)),
                  full((SGU_CHUNK, D_BRANCH))],
        out_specs=[pl.BlockSpec((tm, 3 * D_BRANCH), lambda i: (i, 0)),
                   full((N_GROUPS, SGU_CHUNK, SGU_CHUNK)), full((SGU_CHUNK, LANES)),
                   full((1, D_BRANCH)), full((1, D_BRANCH))],
        out_shape=[jax.ShapeDtypeStruct((s, 3 * D_BRANCH), BF16),
                   jax.ShapeDtypeStruct((N_GROUPS, SGU_CHUNK, SGU_CHUNK), F32),
                   jax.ShapeDtypeStruct((SGU_CHUNK, LANES), F32),
                   jax.ShapeDtypeStruct((1, D_BRANCH), F32), jax.ShapeDtypeStruct((1, D_BRANCH), F32)],
        scratch_shapes=[pltpu.VMEM((SGU_CHUNK, D_BRANCH), F32)],
        compiler_params=_cparams(("arbitrary",)),
    )(proj, proj, proj, dyb, ln_g, ln_b, w_mask, w_mask_t, bias_full)


def _mid(proj, ya, yb, o, x, target, final_g, w_up_a, w_up_b, w_out):
    s = x.shape[0]
    tm = min(256, s)
    n_steps = s // tm
    half = D_MODEL // 2

    def body(ya_ref, yb_ref, o_ref, za_ref, ga0_ref, ga1_ref, gb0_ref, gb1_ref, x_ref, t_ref, gf_ref,
             wa_ref, wb_ref, wo_ref,
             dzg_ref, do_ref, dyb_ref, dx2_ref, gwo_ref, gwa_ref, gwb_ref, loss_ref, dgf_ref,
             acc_o, acc_a, acc_b, merged_s, dx2_s, ya_s, dpa_s, yb_s, dpb_s):
        i = pl.program_id(0)
        held = (merged_s, dx2_s, ya_s, dpa_s, yb_s, dpb_s)

        @pl.when(i == 0)
        def _():
            acc_o[...] = jnp.zeros_like(acc_o)
            acc_a[...] = jnp.zeros_like(acc_a)
            acc_b[...] = jnp.zeros_like(acc_b)
            loss_ref[...] = jnp.zeros_like(loss_ref)
            dgf_ref[...] = jnp.zeros_like(dgf_ref)
            for ref in held:
                ref[...] = jnp.zeros_like(ref)

        def weight_grads():
            acc_o[...] += _dot_tn(merged_s[...], dx2_s[...])
            acc_a[...] += _dot_tn(ya_s[...], dpa_s[...])
            acc_b[...] += _dot_tn(yb_s[...], dpb_s[...])

        weight_grads()
        ya_v = ya_ref[...]
        yb_v = yb_ref[...]
        pa = jnp.dot(ya_v, wa_ref[...], preferred_element_type=F32)
        pb = jnp.dot(yb_v, wb_ref[...], preferred_element_type=F32)
        sa = _sigmoid(jnp.concatenate([ga0_ref[...], ga1_ref[...]], axis=1).astype(F32))
        sb = _sigmoid(jnp.concatenate([gb0_ref[...], gb1_ref[...]], axis=1).astype(F32))
        merged = (sa * pa + sb * pb).astype(BF16)
        x2 = x_ref[...] + jnp.dot(merged, wo_ref[...], preferred_element_type=F32)
        r2 = lax.rsqrt(jnp.mean(x2 * x2, axis=-1, keepdims=True) + EPS)
        xh = x2 * r2
        gf = gf_ref[...]
        diff = xh * gf - t_ref[...]
        loss_ref[...] += 0.5 * jnp.sum(jnp.mean(diff * diff, axis=-1, keepdims=True))
        dy = diff * (1.0 / D_MODEL)
        dgf_ref[...] += jnp.sum(dy * xh, axis=0, keepdims=True)
        dyg = dy * gf
        dx2 = r2 * (dyg - xh * jnp.mean(dyg * xh, axis=-1, keepdims=True))
        dx2_ref[...] = dx2
        dx2b = dx2.astype(BF16)
        dmerged = _dot_nt(dx2b, wo_ref[...])
        dpa = dmerged * sa
        dpb = dmerged * sb
        dzg_ref[:, D_BRANCH:D_BRANCH + D_MODEL] = (dpa * pa * (1.0 - sa)).astype(BF16)
        dzg_ref[:, D_BRANCH + D_MODEL:D_BRANCH + 2 * D_MODEL] = (dpb * pb * (1.0 - sb)).astype(BF16)
        dpab = dpa.astype(BF16)
        dpbb = dpb.astype(BF16)
        dya = _dot_nt(dpab, wa_ref[...])
        dyb_ref[...] = _dot_nt(dpbb, wb_ref[...]).astype(BF16)
        za = za_ref[...].astype(F32)
        sg = _sigmoid(za)
        do_ref[...] = (dya * (za * sg)).astype(BF16)
        dzg_ref[:, 0:D_BRANCH] = (dya * o_ref[...].astype(F32) * (sg * (1.0 + za * (1.0 - sg)))).astype(BF16)
        for ref, val in zip(held, (merged, dx2b, ya_v, dpab, yb_v, dpbb)):
            ref[...] = val

        @pl.when(i == n_steps - 1)
        def _():
            weight_grads()
            gwo_ref[...] = acc_o[...].astype(BF16)
            gwa_ref[...] = acc_a[...].astype(BF16)
            gwb_ref[...] = acc_b[...].astype(BF16)

    tok = lambda w: pl.BlockSpec((tm, w), lambda i: (i, 0))
    col = lambda c: pl.BlockSpec((tm, half), lambda i: (i, c))
    full = lambda shape: pl.BlockSpec(shape, lambda i: (0,) * len(shape))
    return pl.pallas_call(
        body, name="mid", grid=(n_steps,),
        in_specs=[tok(D_BRANCH), tok(D_BRANCH), tok(D_BRANCH), col(COL_ZA), col(COL_GA), col(COL_GA + 1),
                  col(COL_GB), col(COL_GB + 1), tok(D_MODEL), tok(D_MODEL), full((1, D_MODEL)),
                  full((D_BRANCH, D_MODEL)), full((D_BRANCH, D_MODEL)), full((D_MODEL, D_MODEL))],
        out_specs=[tok(D_BRANCH + 2 * D_MODEL), tok(D_BRANCH), tok(D_BRANCH), tok(D_MODEL),
                   full((D_MODEL, D_MODEL)), full((D_BRANCH, D_MODEL)), full((D_BRANCH, D_MODEL)),
                   full((8, LANES)), full((1, D_MODEL))],
        out_shape=[jax.ShapeDtypeStruct((s, D_BRANCH + 2 * D_MODEL), BF16),
                   jax.ShapeDtypeStruct((s, D_BRANCH), BF16), jax.ShapeDtypeStruct((s, D_BRANCH), BF16),
                   jax.ShapeDtypeStruct((s, D_MODEL), F32),
                   jax.ShapeDtypeStruct((D_MODEL, D_MODEL), BF16),
                   jax.ShapeDtypeStruct((D_BRANCH, D_MODEL), BF16), jax.ShapeDtypeStruct((D_BRANCH, D_MODEL), BF16),
                   jax.ShapeDtypeStruct((8, LANES), F32), jax.ShapeDtypeStruct((1, D_MODEL), F32)],
        scratch_shapes=[pltpu.VMEM((D_MODEL, D_MODEL), F32), pltpu.VMEM((D_BRANCH, D_MODEL), F32),
                        pltpu.VMEM((D_BRANCH, D_MODEL), F32),
                        pltpu.VMEM((tm, D_MODEL), BF16), pltpu.VMEM((tm, D_MODEL), BF16),
                        pltpu.VMEM((tm, D_BRANCH), BF16), pltpu.VMEM((tm, D_MODEL), BF16),
                        pltpu.VMEM((tm, D_BRANCH), BF16), pltpu.VMEM((tm, D_MODEL), BF16)],
        compiler_params=_cparams(("arbitrary",)),
    )(ya, yb, o, proj, proj, proj, proj, proj, x, target, final_g, w_up_a, w_up_b, w_out)


def _dwin_piece(ht, piece, tile_of, prev):
    s = ht.shape[1]
    n_tiles = piece.shape[1] // D_BRANCH

    def body(ht_ref, p_ref, *rest):
        out_ref = rest[-1]
        out_ref[...] = jnp.dot(ht_ref[...], p_ref[...], preferred_element_type=F32).astype(BF16)

    in_specs = [pl.BlockSpec((D_MODEL, s), lambda j: (0, 0)), pl.BlockSpec((s, D_BRANCH), lambda j: (0, j))]
    args = [ht, piece]
    aliases = {}
    if prev is not None:
        in_specs.append(pl.BlockSpec(memory_space=pl.ANY))
        args.append(prev)
        aliases = {2: 0}
    return pl.pallas_call(
        body, name="dwin_piece", grid=(n_tiles,),
        in_specs=in_specs,
        out_specs=pl.BlockSpec((D_MODEL, D_BRANCH), lambda j: (0, tile_of(j))),
        out_shape=jax.ShapeDtypeStruct((D_MODEL, D_IN), BF16),
        input_output_aliases=aliases,
        compiler_params=_cparams(("parallel",)),
    )(*args)


def _dh_dx(pieces, w_in, x, norm_g, dx2, g_in):
    s = x.shape[0]
    tm = min(256, s)
    n_steps = s // tm
    arrays = []
    for arr, _, _, _ in pieces:
        if not any(arr is a for a in arrays):
            arrays.append(arr)
    n_arr = len(arrays)
    plan = [([k for k, a in enumerate(arrays) if a is arr][0], wcol, off, width) for arr, wcol, off, width in pieces]

    def body(*refs):
        p_refs = refs[:n_arr]
        w_ref, x_ref, g_ref, dx2_ref, gin_ref, dx_ref, dg_ref, late_ref = refs[n_arr:n_arr + 8]
        step = pl.program_id(0)
        finish = _presum_program(step == 0, step == min(1, n_steps - 1), step == n_steps - 1, 0, _LATE_CHIPS,
                                 gin_ref, late_ref, refs[n_arr + 8:])

        @pl.when(step == 0)
        def _():
            dg_ref[...] = jnp.zeros_like(dg_ref)

        dh = None
        for k, wcol, off, width in plan:
            d = _dot_nt(p_refs[k][:, off:off + width], w_ref[:, wcol:wcol + width])
            dh = d if dh is None else dh + d
        xf = x_ref[...]
        r = lax.rsqrt(jnp.mean(xf * xf, axis=-1, keepdims=True) + EPS)
        xh = xf * r
        dg_ref[...] += jnp.sum(dh * xh, axis=0, keepdims=True)
        dhg = dh * g_ref[...]
        dx_ref[...] = r * (dhg - xh * jnp.mean(dhg * xh, axis=-1, keepdims=True)) + dx2_ref[...]

        @pl.when(step == n_steps - 1)
        def _():
            finish()

    tok = lambda w: pl.BlockSpec((tm, w), lambda i: (i, 0))
    full = lambda shape: pl.BlockSpec(shape, lambda i: (0,) * len(shape))
    any_spec = pl.BlockSpec(memory_space=pl.ANY)
    return pl.pallas_call(
        body, name="dh_dx", grid=(n_steps,),
        in_specs=[tok(a.shape[1]) for a in arrays] + [full((D_MODEL, D_IN)), tok(D_MODEL), full((1, D_MODEL)),
                                                      tok(D_MODEL), any_spec],
        out_specs=[tok(D_MODEL), full((1, D_MODEL)), any_spec],
        out_shape=[jax.ShapeDtypeStruct((s, D_MODEL), F32), jax.ShapeDtypeStruct((1, D_MODEL), F32),
                   jax.ShapeDtypeStruct((N_DEV,) + _GRAD_PIECE_SHAPES[0], BF16)],
        scratch_shapes=_presum_scratch(0, _LATE_CHIPS),
        compiler_params=pltpu.CompilerParams(dimension_semantics=("arbitrary",), vmem_limit_bytes=VMEM_LIMIT,
                                             has_side_effects=True),
    )(*arrays, w_in, x, norm_g, dx2, g_in)


def _adamw(w, g, m, v):
    rows, cols = w.shape
    tr = max(t for t in range(8, 257, 8) if rows % t == 0)
    c1 =1.0 - ADAM_B1 ** ADAM_STEP
    c2 = 1.0 - ADAM_B2 ** ADAM_STEP

    def body(w_ref, g_ref, m_ref, v_ref, g_out_ref, d_ref, nm_ref, nv_ref):
        gv = g_ref[...]
        g_out_ref[...] = gv
        nm = ADAM_B1 * m_ref[...] + (1.0 - ADAM_B1) * gv
        nv = ADAM_B2 * v_ref[...] + (1.0 - ADAM_B2) * (gv * gv)
        d_ref[...] = -ADAM_LR * ((nm / c1) / (jnp.sqrt(nv / c2) + ADAM_EPS) + ADAM_WD * w_ref[...])
        nm_ref[...] = nm
        nv_ref[...] = nv

    spec = pl.BlockSpec((tr, cols), lambda i: (i, 0))
    return pl.pallas_call(
        body, name="adamw", grid=(rows // tr,),
        in_specs=[spec] * 4, out_specs=[spec] * 4,
        out_shape=[jax.ShapeDtypeStruct((rows, cols), F32)] * 4,
        compiler_params=_cparams(("parallel",)),
    )(w, g, m, v)


def _reduce_grads_tail(grads, g_small, early_slots, late_in_slots):
    n_big = len(grads)
    n_arr = n_big + 1
    shard_shapes = [(2 * r, w) for r, w in _GRAD_PIECE_SHAPES]
    small_piece = (SMALL_PIECE, LANES)

    def body(*refs):
        src = refs[:n_arr]
        early = refs[n_arr:n_arr + n_big]
        late_in = refs[n_arr + n_big]
        n_in = n_arr + n_big + 1
        out = refs[n_in:n_in + n_arr]
        slots = refs[n_in + n_arr:n_in + 2 * n_arr]
        sums = refs[n_in + 2 * n_arr:n_in + 3 * n_arr]
        send1, recv1, send2, recv2, local_sems = refs[n_in + 3 * n_arr:]
        x, y, c = _place()
        me = 4 * x + 2 * y + c

        def piece_of(a, dev):
            return src[a].at[dev] if a == n_big else _grad_piece(src[a], a, dev)

        def late(a, dst_dev, src_dev):
            return pltpu.make_async_remote_copy(
                src_ref=piece_of(a, dst_dev), dst_ref=slots[a].at[src_dev],
                send_sem=send1.at[n_arr * dst_dev + a], recv_sem=recv1.at[n_arr * src_dev + a],
                device_id=_dev_id(dst_dev), device_id_type=MESH)

        def late_arrays(dev):
            return (n_big,)

        def load(a, dev, received):
            return pltpu.make_async_copy(received.at[dev], slots[a].at[dev], local_sems.at[n_arr * dev + a])

        def own(a, dev):
            return pltpu.make_async_copy(piece_of(a, dev), slots[a].at[dev], local_sems.at[n_arr * dev + a])

        for dev in range(N_DEV):
            @pl.when(me == dev)
            def _():
                received = [early[0] if dev in _EARLY_IN_DEVS else late_in] + list(early[1:])
                for a in range(n_arr):
                    own(a, dev).start()
                filled = lambda a, peer: peer in _presum_sources(dev)
                for peer in range(N_DEV):
                    if peer != dev:
                        for a in late_arrays(peer):
                            late(a, peer, dev).start()
                        for a in range(n_big):
                            if filled(a, peer):
                                load(a, peer, received[a]).start()
                            else:
                                slots[a][peer] = jnp.zeros(slots[a].shape[1:], slots[a].dtype)
                for a in range(n_arr):
                    own(a, dev).wait()
                for peer in range(N_DEV):
                    if peer != dev:
                        for a in late_arrays(dev):
                            late(a, dev, peer).wait_recv()
                        for a in range(n_big):
                            if filled(a, peer):
                                load(a, peer, received[a]).wait()

        for a in range(n_arr):
            rows = slots[a].shape[1]
            step = 64 if rows % 64 == 0 else 8

            def add_rows(t, carry, a=a, step=step):
                r0 = pl.multiple_of(t * step, step)
                total = slots[a][0, pl.ds(r0, step), :].astype(F32)
                for dev in range(1, N_DEV):
                    total = total + slots[a][dev, pl.ds(r0, step), :].astype(F32)
                sums[a][pl.ds(r0, step), :] = total
                return carry

            lax.fori_loop(0, rows // step, add_rows, 0)

        shares = []
        keeps = []
        for a in range(n_big):
            r, w = _GRAD_PIECE_SHAPES[a]
            dst = out[a].at[pl.ds(pl.multiple_of(c * r, 8), r), :]
            cp = pltpu.make_async_remote_copy(src_ref=sums[a], dst_ref=dst, send_sem=send2.at[a], recv_sem=recv2.at[a],
                                              device_id=(x, y, 1 - c), device_id_type=MESH)
            cp.start()
            shares.append(cp)
            kp = pltpu.make_async_copy(sums[a], dst, local_sems.at[N_DEV * n_arr + a])
            kp.start()
            keeps.append(kp)
        kp = pltpu.make_async_copy(sums[n_big], out[n_big].at[me], local_sems.at[N_DEV * n_arr + n_big])
        kp.start()
        keeps.append(kp)

        def small_share(dst_dev, src_dev):
            return pltpu.make_async_remote_copy(src_ref=sums[n_big], dst_ref=out[n_big].at[src_dev],
                                                send_sem=send2.at[n_big + dst_dev], recv_sem=recv2.at[n_big + src_dev],
                                                device_id=_dev_id(dst_dev), device_id_type=MESH)

        for dev in range(N_DEV):
            @pl.when(me != dev)
            def _():
                small_share(dev, me).start()
        for a in range(n_big):
            r, w = _GRAD_PIECE_SHAPES[a]
            other = out[a].at[pl.ds(pl.multiple_of((1 - c) * r, 8), r), :]
            pltpu.make_async_remote_copy(src_ref=sums[a], dst_ref=other, send_sem=send2.at[a], recv_sem=recv2.at[a],
                                         device_id=(x, y, 1 - c), device_id_type=MESH).wait_recv()
        for dev in range(N_DEV):
            @pl.when(me != dev)
            def _():
                small_share(dev, dev).wait_recv()
                small_share(dev, me).wait_send()
                for a in late_arrays(dev):
                    late(a, dev, me).wait_send()
        for cp in shares:
            cp.wait_send()
        for kp in keeps:
            kp.wait()

    any_spec = pl.BlockSpec(memory_space=pl.ANY)
    return pl.pallas_call(
        body, name="reduce_grads_tail",
        in_specs=[any_spec] * (n_arr + n_big + 1), out_specs=[any_spec] * n_arr,
        out_shape=[jax.ShapeDtypeStruct(sh, F32) for sh in shard_shapes]
        + [jax.ShapeDtypeStruct((N_DEV,) + small_piece, F32)],
        scratch_shapes=[pltpu.VMEM((N_DEV,) + sh, BF16) for sh in _GRAD_PIECE_SHAPES]
        + [pltpu.VMEM((N_DEV,) + small_piece, F32)]
        + [pltpu.VMEM(sh, F32) for sh in _GRAD_PIECE_SHAPES] + [pltpu.VMEM(small_piece, F32)]
        + [pltpu.SemaphoreType.DMA((N_DEV * n_arr,)), pltpu.SemaphoreType.DMA((N_DEV * n_arr,)),
           pltpu.SemaphoreType.DMA((n_big + N_DEV,)), pltpu.SemaphoreType.DMA((n_big + N_DEV,)),
           pltpu.SemaphoreType.DMA((N_DEV * n_arr + n_arr,))],
        compiler_params=pltpu.CompilerParams(vmem_limit_bytes=VMEM_LIMIT, has_side_effects=True),
    )(*grads, g_small, *early_slots, late_in_slots)


_SMALL_PARTS = (("norm_g", 8), ("sgu_ln_g", 8), ("sgu_ln_b", 8), ("w_spatial", 1024), ("b_spatial", 8),
                ("final_norm_g", 8))
_LOSS_ROW = sum(n for _, n in _SMALL_PARTS)


def _pack_small(parts, loss_tile=None):
    rows = []
    for name, n_rows in _SMALL_PARTS:
        a = parts[name].reshape(-1, LANES).astype(F32)
        a = jnp.pad(a, ((0, n_rows - a.shape[0]), (0, 0)))
        rows.append(a)
    rows.append(jnp.zeros((8, LANES), F32) if loss_tile is None else loss_tile)
    rows.append(jnp.zeros((SMALL_ROWS - _LOSS_ROW - 8, LANES), F32))
    return jnp.concatenate(rows, axis=0)


def _unpack_small(packed, shapes):
    out = {}
    r0 = 0
    for name, n_rows in _SMALL_PARTS:
        n = math.prod(shapes[name])
        out[name] = packed[r0:r0 + n // LANES].reshape(shapes[name])
        r0 += n_rows
    return out


def _local_step(proj, ht, x, target, norm_g, w_in, sgu_ln_g, sgu_ln_b, w_spatial, b_spatial, w_up_a, w_up_b, w_out,
                final_norm_g, bq, bk):
    pos = jnp.arange(SGU_CHUNK)
    keep = (pos[None, :] // SGU_SUBCHUNK) <= (pos[:, None] // SGU_SUBCHUNK)
    w_mask = jnp.where(keep[None], w_spatial, 0.0).astype(BF16)
    w_mask_t = jnp.swapaxes(w_mask, 1, 2)
    bias_full = jnp.repeat(b_spatial.T, GROUP_DIM, axis=1)
    ln_g = sgu_ln_g.reshape(1, D_BRANCH)
    ln_b = sgu_ln_b.reshape(1, D_BRANCH)
    final_g = final_norm_g.reshape(1, D_MODEL)

    o, ya, rsave = _attn_fwd(proj, bq, bk, ATTN_PAIRS)
    yb = _sgu_fwd(proj, ln_g, ln_b, w_mask, bias_full)
    dzg, do, dyb, dx2, g_out, g_up_a, g_up_b, loss_acc, d_final = _mid(
        proj, ya, yb, o, x, target, final_g, w_up_a, w_up_b, w_out)
    dsgu, d_wsp, d_bsp, d_lng, d_lnb = _sgu_bwd(proj, dyb, ln_g, ln_b, w_mask, w_mask_t, bias_full)
    g_in = _dwin_piece(ht, dzg, lambda j: jnp.where(j == 0, COL_ZA, COL_GA - 1 + j), None)
    g_in = _dwin_piece(ht, dsgu, lambda j: COL_UB + j, g_in)
    dq, dk, dv, *early_slots = _attn_bwd(proj, do, rsave, bq, bk, ATTN_PAIRS, (g_in, g_up_a, g_up_b, g_out))
    g_in = _dwin_piece(ht, dq, lambda j: COL_Q + j, g_in)
    g_in = _dwin_piece(ht, dk, lambda j: COL_K + j, g_in)
    g_in = _dwin_piece(ht, dv, lambda j: COL_V + j, g_in)
    pieces = [(dq, COL_Q * D_BRANCH, 0, D_BRANCH), (dk, COL_K * D_BRANCH, 0, D_BRANCH),
              (dv, COL_V * D_BRANCH, 0, D_BRANCH), (dzg, COL_ZA * D_BRANCH, 0, D_BRANCH),
              (dsgu, COL_UB * D_BRANCH, 0, 3 * D_BRANCH), (dzg, COL_GA * D_BRANCH, D_BRANCH, 2 * D_MODEL)]
    dx, d_norm, late_in_slots = _dh_dx(pieces, w_in, x, norm_g, dx2, g_in)
    small = {"norm_g": d_norm, "sgu_ln_g": d_lng, "sgu_ln_b": d_lnb, "w_spatial": d_wsp,
             "b_spatial": d_bsp[:, :N_GROUPS].T, "final_norm_g": d_final}
    return loss_acc, dx, (g_in, g_up_a, g_up_b, g_out), small, early_slots, late_in_slots


def kernel(x, norm_g, w_in, sgu_ln_g, sgu_ln_b, w_spatial, b_spatial, w_up_a, w_up_b, w_out, final_norm_g, loss_target, m_norm_g, m_w_in, m_sgu_ln_g, m_sgu_ln_b, m_w_spatial, m_b_spatial, m_w_up_a, m_w_up_b, m_w_out, m_final_norm_g, v_norm_g, v_w_in, v_sgu_ln_g, v_sgu_ln_b, v_w_spatial, v_b_spatial, v_w_up_a, v_w_up_b, v_w_out, v_final_norm_g):
    big_names = ("w_in", "w_up_a", "w_up_b", "w_out")
    small_names = tuple(n for n, _ in _SMALL_PARTS)
    names = ("norm_g", "w_in", "sgu_ln_g", "sgu_ln_b", "w_spatial", "b_spatial", "w_up_a", "w_up_b", "w_out",
             "final_norm_g")
    w = dict(norm_g=norm_g, w_in=w_in, sgu_ln_g=sgu_ln_g, sgu_ln_b=sgu_ln_b, w_spatial=w_spatial,
             b_spatial=b_spatial, w_up_a=w_up_a, w_up_b=w_up_b, w_out=w_out, final_norm_g=final_norm_g)
    m = dict(norm_g=m_norm_g, w_in=m_w_in, sgu_ln_g=m_sgu_ln_g, sgu_ln_b=m_sgu_ln_b, w_spatial=m_w_spatial,
             b_spatial=m_b_spatial, w_up_a=m_w_up_a, w_up_b=m_w_up_b, w_out=m_w_out, final_norm_g=m_final_norm_g)
    v = dict(norm_g=v_norm_g, w_in=v_w_in, sgu_ln_g=v_sgu_ln_g, sgu_ln_b=v_sgu_ln_b, w_spatial=v_w_spatial,
             b_spatial=v_b_spatial, w_up_a=v_w_up_a, w_up_b=v_w_up_b, w_out=v_w_out, final_norm_g=v_final_norm_g)
    shapes = {n: w[n].shape for n in names}
    flat2d = lambda a: a.reshape(a.shape[-2:])

    proj, ht, *full = _in_proj_gather(x[0], norm_g, *[flat2d(w[n]) for n in big_names])
    loss, dx, big_grads, small, early_slots, late_in_slots = _local_step(
        proj, ht, x[0], loss_target[0], norm_g, full[0], sgu_ln_g[0], sgu_ln_b[0], w_spatial[0], b_spatial[0],
        full[1], full[2], full[3], final_norm_g, ATTN_Q_BLOCK, ATTN_K_BLOCK)
    packed = _pack_small(small, loss).reshape(N_DEV, SMALL_PIECE, LANES)
    red = _reduce_grads_tail(big_grads, packed, early_slots, late_in_slots)

    grads, deltas, new_m, new_v = {}, {}, {}, {}
    for n, g in zip(big_names, red[:4]):
        g, d, nm, nv = _adamw(flat2d(w[n]), g, flat2d(m[n]), flat2d(v[n]))
        grads[n], deltas[n], new_m[n], new_v[n] = (a.reshape(shapes[n]) for a in (g, d, nm, nv))
    g_small = red[4].reshape(SMALL_ROWS, LANES)
    g_small, d, nm, nv = _adamw(_pack_small({n: w[n] for n in small_names}), g_small,
                                _pack_small({n: m[n] for n in small_names}),
                                _pack_small({n: v[n] for n in small_names}))
    for src, dst in ((g_small, grads), (d, deltas), (nm, new_m), (nv, new_v)):
        dst.update(_unpack_small(src, shapes))

    return (g_small[_LOSS_ROW, 0], dx[None], *[grads[n] for n in names], *[deltas[n] for n in names],
            *[new_m[n] for n in names], *[new_v[n] for n in names])
```

```python
import math

import jax
import jax.numpy as jnp
from jax import lax
from jax.experimental import pallas as pl
from jax.experimental.pallas import tpu as pltpu

F32 = jnp.float32
BF16 = jnp.bfloat16

D_MODEL = 1024
N_HEADS = 8
HEAD_DIM = 64
D_BRANCH = 512
D_IN = 4 * D_BRANCH + 3 * D_BRANCH + 2 * D_MODEL
N_GROUPS = 8
GROUP_DIM = 64
SGU_CHUNK = 128
SGU_SUBCHUNK = 64
GROUP_SHIFT = 6
EPS = 1e-6
LANES = 128
ATTN_Q_BLOCK = 256
ATTN_K_BLOCK = 256
DEAD = -110.0
SKIPPED = -1e30
SCAN_PASSES = 1
ATTN_PAIRS = 2
N_CHIPS = 4
N_DEV = 8
MESH = pl.DeviceIdType.MESH

ADAM_LR = 0.001
ADAM_B1 = 0.9
ADAM_B2 = 0.999
ADAM_EPS = 1e-08
ADAM_WD = 0.01
ADAM_STEP = 10

COL_Q, COL_K, COL_V, COL_ZA, COL_UB, COL_VB, COL_ZB, COL_GA, COL_GB = 0, 1, 2, 3, 4, 5, 6, 7, 9

VMEM_LIMIT = 56 * 1024 * 1024

SMALL_ROWS = 1088
SMALL_PIECE = SMALL_ROWS // N_DEV


def _cparams(sem=None):
    return pltpu.CompilerParams(dimension_semantics=sem, vmem_limit_bytes=VMEM_LIMIT)


def _aligned(v, m):
    return v if isinstance(v, int) else pl.multiple_of(v, m)


def _sigmoid(x):
    return 1.0 / (1.0 + jnp.exp(-x))


def _gelu_and_grad(x):
    k = math.sqrt(2.0 / math.pi)
    x2 = x * x
    inner = k * (x + 0.044715 * x * x2)
    th = jnp.tanh(inner)
    g = 0.5 * x * (1.0 + th)
    dg = 0.5 * (1.0 + th) + 0.5 * x * (1.0 - th * th) * (k * (1.0 + 3.0 * 0.044715 * x2))
    return g, dg


def _split_dot(a, b_bf16, passes):
    out = None
    rem = a
    for _ in range(passes):
        part = rem.astype(BF16)
        d = jnp.dot(part, b_bf16, preferred_element_type=F32)
        out = d if out is None else out + d
        rem = rem - part.astype(F32)
    return out


def _dot_nt(a, b):
    return lax.dot_general(a, b, (((1,), (1,)), ((), ())), preferred_element_type=F32)


def _dot_tn(a, b):
    return lax.dot_general(a, b, (((0,), (0,)), ((), ())), preferred_element_type=F32)


def _place():
    x, y, c = lax.axis_index("x"), lax.axis_index("y"), lax.axis_index("c")
    return x, y, c


def _in_proj_gather(x, norm_g, w_in, w_up_a, w_up_b, w_out):
    s = x.shape[0]
    tm = min(1024, s)
    nt = s // tm
    shards = (w_in, w_up_a, w_up_b, w_out)
    n_arr = len(shards)
    col_sharded = (True, True, True, False)
    full_shapes = ((D_MODEL, D_IN), (D_BRANCH, D_MODEL), (D_BRANCH, D_MODEL), (D_MODEL, D_MODEL))
    w_shard = w_in.shape[1]
    half_rows = D_MODEL // 2
    stage_rows = 256

    def body(order_ref, x_ref, g_ref, *refs):
        src = refs[:n_arr]
        proj_ref, ht_ref = refs[n_arr:n_arr + 2]
        out = refs[n_arr + 2:2 * n_arr + 2]
        wsc, h_scr, stage = refs[2 * n_arr + 2:2 * n_arr + 5]
        small_stage = refs[2 * n_arr + 5:2 * n_arr + 8]
        small_cast = refs[2 * n_arr + 8:2 * n_arr + 11]
        send_sems, recv_sems, local_sems = refs[2 * n_arr + 11:]
        k = pl.program_id(0)
        i = pl.program_id(1)
        x_, y_, c = _place()
        chip = 2 * x_ + y_
        sibling = (x_, y_, 1 - c)
        others = [(x_, 1 - y_), (1 - x_, y_), (1 - x_, 1 - y_)]

        def region(a, chip_idx, half):
            if a == 0:
                return wsc.at[chip_idx, pl.ds(_aligned(half * half_rows, 16), half_rows), :]
            r, w = shards[a].shape
            hr = r // 2
            if col_sharded[a]:
                return out[a].at[pl.ds(_aligned(half * hr, 16), hr), pl.ds(_aligned(chip_idx * w, LANES), w)]
            return out[a].at[pl.ds(_aligned(chip_idx * r + half * hr, 16), hr), :]

        def remote(kk, a, chip_idx, half, to, own):
            s_ref = region(a, chip_idx, half)
            if own and a > 0:
                hr = shards[a].shape[0] // 2
                s_ref = small_cast[a - 1].at[pl.ds(_aligned(half * hr, 16), hr), :]
            return pltpu.make_async_remote_copy(src_ref=s_ref, dst_ref=region(a, chip_idx, half),
                                                send_sem=send_sems.at[kk], recv_sem=recv_sems.at[kk],
                                                device_id=to, device_id_type=MESH)

        def keep_whole(kk, chip_idx):
            return pltpu.make_async_copy(wsc.at[chip_idx],
                                         out[0].at[:, pl.ds(_aligned(chip_idx * w_shard, LANES), w_shard)],
                                         local_sems.at[kk])

        def small_stores():
            cps = []
            for a in range(1, n_arr):
                hr = shards[a].shape[0] // 2
                for half in range(2):
                    cps.append(pltpu.make_async_copy(small_cast[a - 1].at[pl.ds(half * hr, hr), :],
                                                     region(a, chip, half), local_sems.at[4 + 2 * (a - 1) + half]))
            return cps

        def arrive_and_pass(j):
            ochip = chip ^ j
            for a in range(n_arr):
                kk = n_arr * (j - 1) + a
                remote(kk, a, ochip, c, sibling, False).wait_recv()
                remote(3 * n_arr + kk, a, ochip, c, sibling, False).start()

        def from_sibling(j, a):
            remote(3 * n_arr + n_arr * (j - 1) + a, a, chip ^ j, 1 - c, sibling, False).wait_recv()

        @pl.when((k == 0) & (i == 0))
        def _():
            def cast_rows(half):
                for t in range(half_rows // stage_rows):
                    r0 = pl.multiple_of(half * half_rows + t * stage_rows, stage_rows)
                    pltpu.sync_copy(src[0].at[pl.ds(r0, stage_rows), :], stage)
                    wsc[chip, pl.ds(r0, stage_rows), :] = stage[...].astype(BF16)

            cast_rows(c)
            for j in (1, 2):
                remote(n_arr * (j - 1), 0, chip, c, (*others[j - 1], c), True).start()
            cast_rows(1 - c)
            for a in range(1, n_arr):
                pltpu.sync_copy(src[a], small_stage[a - 1])
                small_cast[a - 1][...] = small_stage[a - 1][...].astype(BF16)
            for j in (1, 2):
                for a in range(1, n_arr):
                    remote(n_arr * (j - 1) + a, a, chip, c, (*others[j - 1], c), True).start()
            keep_whole(0, chip).start()
            for cp in small_stores():
                cp.start()

        @pl.when((k == 1) & (i == 0))
        def _():
            for j in (1, 2):
                remote(n_arr * (j - 1), 0, chip, c, (*others[j - 1], c), True).wait_send()
            for a in range(n_arr):
                remote(n_arr * 2 + a, a, chip, c, (*others[2], c), True).start()
            arrive_and_pass(1)
            arrive_and_pass(2)
            from_sibling(1, 0)
            keep_whole(1, chip ^ 1).start()

        @pl.when((k == 2) & (i == 0))
        def _():
            from_sibling(2, 0)
            keep_whole(2, chip ^ 2).start()
            arrive_and_pass(3)

        @pl.when((k == 3) & (i == 0))
        def _():
            from_sibling(3, 0)
            keep_whole(3, chip ^ 3).start()

        @pl.when(k == 0)
        def _():
            xf = x_ref[...]
            r = lax.rsqrt(jnp.mean(xf * xf, axis=-1, keepdims=True) + EPS)
            h = xf * r * g_ref[...]
            h_scr[i] = h.astype(BF16)
            ht_ref[...] = h.T.astype(BF16)

        proj_ref[...] = jnp.dot(h_scr[i], wsc[order_ref[k]], preferred_element_type=F32).astype(BF16)

        @pl.when((k == 3) & (i == nt - 1))
        def _():
            for j in (1, 2, 3):
                for a in range(1, n_arr):
                    from_sibling(j, a)
            for j in (1, 2, 3):
                for a in range(n_arr):
                    kk = n_arr * (j - 1) + a
                    if a > 0 or j == 3:
                        remote(kk, a, chip, c, (*others[j - 1], c), True).wait_send()
                    remote(3 * n_arr + kk, a, chip ^ j, c, sibling, False).wait_send()
            for kk in range(4):
                keep_whole(kk, chip ^ kk).wait()
            for cp in small_stores():
                cp.wait()

    any_spec = pl.BlockSpec(memory_space=pl.ANY)
    tile = lambda kk, ii: jnp.where(kk == 0, ii, nt - 1)
    grid_spec = pltpu.PrefetchScalarGridSpec(
        num_scalar_prefetch=1, grid=(N_CHIPS, nt),
        in_specs=[pl.BlockSpec((tm, D_MODEL), lambda kk, ii, order: (tile(kk, ii), 0)),
                  pl.BlockSpec((1, D_MODEL), lambda kk, ii, order: (0, 0))] + [any_spec] * n_arr,
        out_specs=[pl.BlockSpec((tm, w_shard), lambda kk, ii, order: (ii, order[kk])),
                   pl.BlockSpec((D_MODEL, tm), lambda kk, ii, order: (0, tile(kk, ii)))] + [any_spec] * n_arr,
        scratch_shapes=[pltpu.VMEM((N_CHIPS, D_MODEL, w_shard), BF16), pltpu.VMEM((nt, tm, D_MODEL), BF16),
                        pltpu.VMEM((stage_rows, w_shard), F32)]
        + [pltpu.VMEM(a.shape, F32) for a in shards[1:]] + [pltpu.VMEM(a.shape, BF16) for a in shards[1:]]
        + [pltpu.SemaphoreType.DMA((6 * n_arr,)), pltpu.SemaphoreType.DMA((6 * n_arr,)),
           pltpu.SemaphoreType.DMA((4 + 2 * (n_arr - 1),))])
    x_, y_, _ = _place()
    order = (2 * x_ + y_) ^ jnp.arange(N_CHIPS, dtype=jnp.int32)
    return pl.pallas_call(
        body, name="in_proj_gather", grid_spec=grid_spec,
        out_shape=[jax.ShapeDtypeStruct((s, D_IN), BF16), jax.ShapeDtypeStruct((D_MODEL, s), BF16)]
        + [jax.ShapeDtypeStruct(sh, BF16) for sh in full_shapes],
        compiler_params=pltpu.CompilerParams(dimension_semantics=("arbitrary", "arbitrary"),
                                             vmem_limit_bytes=VMEM_LIMIT, has_side_effects=True),
    )(order, x, norm_g, *shards)


def _neg_softplus_parts(z):
    zb = z.astype(BF16)
    p = jnp.exp(-jnp.abs(zb))
    return p, jnp.maximum(zb, jnp.zeros_like(zb)) + jnp.log(1.0 + p)


def _split_cat(a, passes):
    parts = []
    rem = a
    for k in range(passes):
        part = rem.astype(BF16)
        parts.append(part)
        if k + 1 < passes:
            rem = rem - part.astype(F32)
    return parts[0] if passes == 1 else jnp.concatenate(parts, axis=1)


def _tri(blk, upper, sign):
    row = lax.broadcasted_iota(jnp.int32, (blk, blk), 0)
    col = lax.broadcasted_iota(jnp.int32, (blk, blk), 1)
    keep = (row <= col) if upper else (row >= col)
    t = jnp.where(keep, sign, 0.0).astype(BF16)
    return t if SCAN_PASSES == 1 else jnp.concatenate([t] * SCAN_PASSES, axis=0)


def _attn_fwd(proj, bq, bk, npairs):
    s = proj.shape[0]
    nq = s // bq
    ratio = bq // bk
    scale = HEAD_DIM ** -0.5
    heads = tuple(range(2 * npairs))
    width = LANES * npairs

    def body(q_ref, k_ref, v_ref, za_ref, o_ref, ya_ref, rs_ref, acc_ref, r_ref):
        i = pl.program_id(1)
        lane = lax.broadcasted_iota(jnp.int32, (bq, LANES), 1)
        lo_half = lane < HEAD_DIM
        qm = []
        for pr in range(npairs):
            q = q_ref[:, LANES * pr:LANES * (pr + 1)] * jnp.asarray(scale, BF16)
            zero = jnp.zeros_like(q)
            qm += [jnp.where(lo_half, q, zero), jnp.where(lo_half, zero, q)]
        row = lax.broadcasted_iota(jnp.int32, (bq, bk), 0)
        col = lax.broadcasted_iota(jnp.int32, (bq, bk), 1)
        tneg = _tri(bk, False, -1.0)
        acc_ref[...] = jnp.zeros_like(acc_ref)
        r_ref[...] = jnp.zeros_like(r_ref)
        rs_ref[...] = jnp.full_like(rs_ref, SKIPPED)

        def scores(j):
            ks = pl.multiple_of(j * bk, bk)
            return [_dot_nt(qm[h], k_ref[pl.ds(ks, bk), LANES * (h // 2):LANES * (h // 2 + 1)]) for h in heads]

        def block(j, diag, valid=None):
            ks = pl.multiple_of(j * bk, bk)
            vj = [v_ref[pl.ds(ks, bk), LANES * pr:LANES * (pr + 1)] for pr in range(npairs)]
            if diag:
                before = (j * bk + col) < (i * bq + row)
            z = scores(j)
            sp = [_neg_softplus_parts(z[h])[1] for h in heads]
            if diag:
                sp = [jnp.where(before, sp[h], 0.0) for h in heads]
            cin = [jnp.dot(_split_cat(sp[h], SCAN_PASSES), tneg, preferred_element_type=F32) for h in heads]
            w = [jnp.exp(z[h] + cin[h]) for h in heads]
            if diag:
                w = [jnp.where(before, w[h], 0.0) for h in heads]
            pv = [jnp.dot(w[h].astype(BF16), vj[h // 2], preferred_element_type=F32) for h in heads]
            r = [r_ref[h] for h in heads]
            keep = 1.0 if valid is None else valid.astype(F32)
            for h in heads:
                acc_ref[h] += pv[h] * (jnp.exp(r[h]) * keep)
                r_ref[h] = r[h] + cin[h][:, 0:1] * keep
            for pr in range(npairs):
                hit = [lane == j, lane == j + HEAD_DIM]
                if valid is not None:
                    hit = [m & valid for m in hit]
                rs_ref[pr] = jnp.where(hit[0], r[2 * pr], jnp.where(hit[1], r[2 * pr + 1], rs_ref[pr]))

        for t in range(ratio):
            block(i * ratio + ratio - 1 - t, True)
        block(jnp.maximum(i * ratio - 1, 0), False, valid=i > 0)

        def alive(carry):
            jj, r_max = carry
            return (jj < i * ratio - 1) & (r_max > DEAD)

        def loop_body(carry):
            jj, _ = carry
            block(i * ratio - 2 - jj, False)
            return jj + 1, jnp.max(r_ref[...])

        lax.while_loop(alive, loop_body, (0, jnp.max(r_ref[...])))
        for pr in range(npairs):
            cols = slice(LANES * pr, LANES * (pr + 1))
            o = jnp.where(lo_half, acc_ref[2 * pr], acc_ref[2 * pr + 1])
            o_ref[:, cols] = o.astype(BF16)
            za = za_ref[:, cols].astype(F32)
            ya_ref[:, cols] = (o * (za * _sigmoid(za))).astype(BF16)

    n_steps = N_HEADS // (2 * npairs)
    return pl.pallas_call(
        body, name="attn_fwd", grid=(n_steps, nq),
        in_specs=[pl.BlockSpec((bq, width), lambda p, i: (i, n_steps * COL_Q + p)),
                  pl.BlockSpec((s, width), lambda p, i: (0, n_steps * COL_K + p)),
                  pl.BlockSpec((s, width), lambda p, i: (0, n_steps * COL_V + p)),
                  pl.BlockSpec((bq, width), lambda p, i: (i, n_steps * COL_ZA + p))],
        out_specs=[pl.BlockSpec((bq, width), lambda p, i: (i, p)),
                   pl.BlockSpec((bq, width), lambda p, i: (i, p)),
                   pl.BlockSpec((npairs, bq, LANES), lambda p, i: (p, i, 0))],
        out_shape=[jax.ShapeDtypeStruct((s, D_BRANCH), BF16), jax.ShapeDtypeStruct((s, D_BRANCH), BF16),
                   jax.ShapeDtypeStruct((N_HEADS // 2, s, LANES), F32)],
        scratch_shapes=[pltpu.VMEM((2 * npairs, bq, LANES), F32), pltpu.VMEM((2 * npairs, bq, 1), F32)],
        compiler_params=_cparams(("parallel", "parallel")),
    )(proj, proj, proj, proj)


_GRAD_COL_SHARDED = (True, True, True, False)
_GRAD_FULL_SHAPES = ((D_MODEL, D_IN), (D_BRANCH, D_MODEL), (D_BRANCH, D_MODEL), (D_MODEL, D_MODEL))
_GRAD_PIECE_SHAPES = tuple((r // 2, w // N_CHIPS) if cs else (r // (2 * N_CHIPS), w)
                           for (r, w), cs in zip(_GRAD_FULL_SHAPES, _GRAD_COL_SHARDED))
_EARLY_IN_DEVS = (4, 5, 6, 7)
_LATE_IN_DEVS = (0, 1, 2, 3)
_LATE_CHIPS = (0, 1)
_EARLY_CHIPS = (2, 3)
_ALL_CHIPS = (0, 1, 2, 3)


def _grad_piece(ref, a, dev):
    r, w = _GRAD_PIECE_SHAPES[a]
    if _GRAD_COL_SHARDED[a]:
        return ref.at[pl.ds((dev % 2) * r, r), pl.ds((dev // 2) * w, w)]
    return ref.at[pl.ds(dev * r, r), :]


def _dev_id(dev):
    return (dev // 4, (dev // 2) % 2, dev % 2)


def _me():
    return 4 * lax.axis_index("x") + 2 * lax.axis_index("y") + lax.axis_index("c")


def _presum_copy(src, dst, send_sem, recv_sem, to_dev):
    return pltpu.make_async_remote_copy(src_ref=src, dst_ref=dst, send_sem=send_sem, recv_sem=recv_sem,
                                        device_id=_dev_id(to_dev), device_id_type=MESH)


def _presum_hand_off(dev, a, dest_chips, g_ref, slots, pair, send_sems, recv_sems):
    chip, core = dev // 2, dev % 2
    cps = []
    for k, q in enumerate(dest_chips):
        piece = _grad_piece(g_ref, a, 2 * q + 1 - core)
        if q == chip:
            cps.append(_presum_copy(piece, slots.at[dev], send_sems.at[N_DEV + k], recv_sems.at[dev], dev ^ 1))
        else:
            cps.append(_presum_copy(piece, pair.at[k], send_sems.at[N_DEV + k], recv_sems.at[N_DEV + k], dev ^ 1))
    return cps


def _presum_sends(dev, a, dest_chips, slots, sums, send_sems, recv_sems):
    chip, core = dev // 2, dev % 2
    return [_presum_copy(sums.at[k], slots.at[dev], send_sems.at[2 * q + core], recv_sems.at[dev], 2 * q + core)
            for k, q in enumerate(dest_chips) if q != chip]


def _presum_loads(dev, a, dest_chips, g_ref, stage, load_sems):
    chip, core = dev // 2, dev % 2
    return [pltpu.make_async_copy(_grad_piece(g_ref, a, 2 * q + core), stage.at[k], load_sems.at[k])
            for k, q in enumerate(dest_chips) if q != chip]


def _presum_send(dev, a, dest_chips, g_ref, slots, pair, stage, sums, send_sems, recv_sems, load_sems):
    chip, core = dev // 2, dev % 2
    hand = _presum_hand_off(dev, a, dest_chips, g_ref, slots, pair, send_sems, recv_sems)
    for cp in _presum_loads(dev, a, dest_chips, g_ref, stage, load_sems):
        cp.wait()
    for k, q in enumerate(dest_chips):
        if q != chip:
            hand[k].wait_recv()
            sums[k] = (stage[k].astype(F32) + pair[k].astype(F32)).astype(BF16)
    for cp in _presum_sends(dev, a, dest_chips, slots, sums, send_sems, recv_sems):
        cp.start()


def _presum_wait(dev, a, dest_chips, g_ref, slots, pair, sums, send_sems, recv_sems):
    chip, core = dev // 2, dev % 2
    for cp in _presum_hand_off(dev, a, dest_chips, g_ref, slots, pair, send_sems, recv_sems):
        cp.wait_send()
    for cp in _presum_sends(dev, a, dest_chips, slots, sums, send_sems, recv_sems):
        cp.wait_send()
    if chip in dest_chips:
        for src_dev in _presum_sources(dev):
            _presum_copy(sums.at[0], slots.at[src_dev], send_sems.at[src_dev], recv_sems.at[src_dev], src_dev).wait_recv()


def _presum_sources(dev):
    return [dev ^ 1] + [2 * r + dev % 2 for r in range(N_CHIPS) if r != dev // 2]


def _presum_scratch(a, dest_chips):
    n = len(dest_chips)
    piece = _GRAD_PIECE_SHAPES[a]
    return [pltpu.VMEM((n,) + piece, BF16), pltpu.VMEM((n,) + piece, BF16), pltpu.VMEM((n,) + piece, BF16),
            pltpu.SemaphoreType.DMA((N_DEV + n,)), pltpu.SemaphoreType.DMA((N_DEV + n,)),
            pltpu.SemaphoreType.DMA((n,))]


PRESUM_SCRATCH = 6


def _presum_program(first, second, last, a, dest_chips, g_ref, slots, scratch):
    pair, stage, sums, send_sems, recv_sems, load_sems = scratch
    me = _me()

    @pl.when(first)
    def _():
        for dev in range(N_DEV):
            @pl.when(me == dev)
            def _():
                for cp in _presum_hand_off(dev, a, dest_chips, g_ref, slots, pair, send_sems, recv_sems):
                    cp.start()
                for cp in _presum_loads(dev, a, dest_chips, g_ref, stage, load_sems):
                    cp.start()

    @pl.when(second)
    def _():
        for dev in range(N_DEV):
            @pl.when(me == dev)
            def _():
                _presum_send(dev, a, dest_chips, g_ref, slots, pair, stage, sums, send_sems, recv_sems, load_sems)

    def finish():
        for dev in range(N_DEV):
            @pl.when(me == dev)
            def _():
                _presum_wait(dev, a, dest_chips, g_ref, slots, pair, sums, send_sems, recv_sems)

    return finish


def _attn_bwd(proj, do, rsave, bq, bk, npairs, grads):
    plan = ((0, _EARLY_CHIPS), (1, _ALL_CHIPS), (2, _ALL_CHIPS), (3, _ALL_CHIPS))
    s = proj.shape[0]
    nq = s // bq
    ratio = bq // bk
    scale = HEAD_DIM ** -0.5
    heads = tuple(range(2 * npairs))
    width = LANES * npairs

    n_steps = N_HEADS // (2 * npairs)
    n_g = len(grads)

    def body(q_ref, k_ref, v_ref, do_ref, rs_ref, *refs):
        g_src = refs[:n_g]
        dq_ref, dk_ref, dv_ref = refs[n_g:n_g + 3]
        g_slots = refs[n_g + 3:2 * n_g + 3]
        dk_acc, dv_acc, dq_acc, e_ref = refs[2 * n_g + 3:2 * n_g + 7]
        i = pl.program_id(1)
        step = pl.program_id(0) * nq + i
        finish = [_presum_program(step == 0, step == 1, step == n_steps * nq - 1, a, chips, g_src[pos], g_slots[pos],
                                  refs[2 * n_g + 7 + PRESUM_SCRATCH * pos:2 * n_g + 7 + PRESUM_SCRATCH * (pos + 1)])
                  for pos, (a, chips) in enumerate(plan)]

        lane = lax.broadcasted_iota(jnp.int32, (bq, LANES), 1)
        lo_half = lane < HEAD_DIM
        qm, dom = [], []
        for pr in range(npairs):
            cols = slice(LANES * pr, LANES * (pr + 1))
            q = q_ref[:, cols] * jnp.asarray(scale, BF16)
            zero = jnp.zeros_like(q)
            qm += [jnp.where(lo_half, q, zero), jnp.where(lo_half, zero, q)]
            dout = do_ref[:, cols].astype(F32)
            dom += [jnp.where(lo_half, dout, 0.0), jnp.where(lo_half, 0.0, dout)]
        row = lax.broadcasted_iota(jnp.int32, (bq, bk), 0)
        col = lax.broadcasted_iota(jnp.int32, (bq, bk), 1)
        tneg = _tri(bk, False, -1.0)
        tfwd = _tri(bk, True, 1.0)

        @pl.when(i == 0)
        def _():
            dk_acc[...] = jnp.zeros_like(dk_acc)
            dv_acc[...] = jnp.zeros_like(dv_acc)

        dq_acc[...] = jnp.zeros_like(dq_acc)
        e_ref[...] = jnp.zeros_like(e_ref)

        def block(j, diag, valid=None):
            ks = pl.multiple_of(j * bk, bk)
            kj = [k_ref[pl.ds(ks, bk), LANES * pr:LANES * (pr + 1)] for pr in range(npairs)]
            vj = [v_ref[pl.ds(ks, bk), LANES * pr:LANES * (pr + 1)] for pr in range(npairs)]
            if diag:
                before = (j * bk + col) < (i * bq + row)
            z = [_dot_nt(qm[h], kj[h // 2]) for h in heads]
            er = [jnp.exp(jnp.sum(jnp.where(lane == j + HEAD_DIM * (h % 2), rs_ref[h // 2], 0.0), axis=-1,
                                  keepdims=True)) for h in heads]
            if valid is not None:
                er = [er[h] * valid.astype(F32) for h in heads]
            dos = [(dom[h] * er[h]).astype(BF16) for h in heads]
            dw = [_dot_nt(dos[h], vj[h // 2]) for h in heads]
            psp = [_neg_softplus_parts(z[h]) for h in heads]
            sp = [psp[h][1] for h in heads]
            if diag:
                sp = [jnp.where(before, sp[h], 0.0) for h in heads]
            cin = [jnp.dot(_split_cat(sp[h], SCAN_PASSES), tneg, preferred_element_type=F32) for h in heads]
            w = [jnp.exp(z[h] + cin[h]) for h in heads]
            if diag:
                w = [jnp.where(before, w[h], 0.0) for h in heads]
            e =[dw[h] * w[h] for h in heads]
            eincl = [jnp.dot(_split_cat(e[h], SCAN_PASSES), tfwd, preferred_element_type=F32) + e_ref[h]
                     for h in heads]
            dz = []
            for h in heads:
                p = psp[h][0]
                beta = jnp.where(z[h] >= 0.0, 1.0, p) / (1.0 + p)
                d = e[h] - beta * eincl[h]
                dz.append((jnp.where(before, d, 0.0) if diag else d).astype(BF16))
            wb = [w[h].astype(BF16) for h in heads]
            for h in heads:
                e_ref[h] = eincl[h][:, bk - 1:bk]
                dq_acc[h] += jnp.dot(dz[h], kj[h // 2], preferred_element_type=F32)
            for pr in range(npairs):
                cols = slice(LANES * pr, LANES * (pr + 1))
                h0, h1 = 2 * pr, 2 * pr + 1
                dk_acc[pl.ds(ks, bk), cols] += _dot_tn(dz[h0], qm[h0]) + _dot_tn(dz[h1], qm[h1])
                dv_acc[pl.ds(ks, bk), cols] += _dot_tn(wb[h0], dos[h0]) + _dot_tn(wb[h1], dos[h1])

        def loop_body(j, carry):
            block(j, False)
            return carry

        block_of_lane = lane & (HEAD_DIM - 1)
        live = jnp.max(rs_ref[...], axis=0) > DEAD
        first_live = jnp.min(jnp.where(live, block_of_lane, nq * ratio))
        last = jnp.maximum(i * ratio - 1, 0)
        lax.fori_loop(jnp.minimum(first_live, last), last, loop_body, 0)
        block(last, False, valid=i > 0)
        for t in range(ratio):
            block(i * ratio + t, True)
        for pr in range(npairs):
            dq = jnp.where(lo_half, dq_acc[2 * pr], dq_acc[2 * pr + 1]) * scale
            dq_ref[:, LANES * pr:LANES * (pr + 1)] = dq.astype(BF16)

        @pl.when(i == nq - 1)
        def _():
            dk_ref[...] = dk_acc[...].astype(BF16)
            dv_ref[...] = dv_acc[...].astype(BF16)

        @pl.when(step == n_steps * nq - 1)
        def _():
            for fin in finish:
                fin()

    any_spec = pl.BlockSpec(memory_space=pl.ANY)
    return pl.pallas_call(
        body, name="attn_bwd", grid=(n_steps, nq),
        in_specs=[pl.BlockSpec((bq, width), lambda p, i: (i, n_steps * COL_Q + p)),
                  pl.BlockSpec((s, width), lambda p, i: (0, n_steps * COL_K + p)),
                  pl.BlockSpec((s, width), lambda p, i: (0, n_steps * COL_V + p)),
                  pl.BlockSpec((bq, width), lambda p, i: (i, p)),
                  pl.BlockSpec((npairs, bq, LANES), lambda p, i: (p, i, 0))] + [any_spec] * n_g,
        out_specs=[pl.BlockSpec((bq, width), lambda p, i: (i, p)),
                   pl.BlockSpec((s, width), lambda p, i: (0, p)),
                   pl.BlockSpec((s, width), lambda p, i: (0, p))] + [any_spec] * n_g,
        out_shape=[jax.ShapeDtypeStruct((s, D_BRANCH), BF16)] * 3
        + [jax.ShapeDtypeStruct((N_DEV,) + _GRAD_PIECE_SHAPES[a], BF16) for a, _ in plan],
        scratch_shapes=[pltpu.VMEM((s, width), F32), pltpu.VMEM((s, width), F32),
                        pltpu.VMEM((2 * npairs, bq, LANES), F32), pltpu.VMEM((2 * npairs, bq, 1), F32)]
        + [sh for a, chips in plan for sh in _presum_scratch(a, chips)],
        compiler_params=pltpu.CompilerParams(dimension_semantics=("arbitrary", "arbitrary"),
                                             vmem_limit_bytes=VMEM_LIMIT, has_side_effects=True),
    )(proj, proj, proj, do, rsave, *grads)


def _group_avg_matrix():
    a = lax.broadcasted_iota(jnp.int32, (LANES, LANES), 0) >> GROUP_SHIFT
    b = lax.broadcasted_iota(jnp.int32, (LANES, LANES), 1) >> GROUP_SHIFT
    return jnp.where(a == b, 1.0 / GROUP_DIM, 0.0).astype(BF16)


def _group_mean(a, avg):
    parts = [_split_dot(a[:, LANES * k:LANES * (k + 1)], avg, 3) for k in range(D_BRANCH // LANES)]
    return jnp.concatenate(parts, axis=1)


def _sgu_forward_parts(ub, vb, ln_g, ln_b, avg):
    ug, dug = _gelu_and_grad(ub)
    vg, dvg = _gelu_and_grad(vb)
    mu = _group_mean(vg, avg)
    d = vg - mu
    var = _group_mean(d * d, avg)
    rstd = lax.rsqrt(var + EPS)
    vhat = d * rstd
    vn = vhat * ln_g + ln_b
    return ug, dug, dvg, rstd, vhat, vn


def _sgu_mix(w_ref, src_bf16, n_chunks):
    lane = lax.broadcasted_iota(jnp.int32, (SGU_CHUNK, LANES), 1)
    lo_half = lane < GROUP_DIM
    rows = []
    for n in range(n_chunks):
        slabs = []
        for a in range(D_BRANCH // LANES):
            blk = src_bf16[SGU_CHUNK * n:SGU_CHUNK * (n + 1), LANES * a:LANES * (a + 1)]
            zero = jnp.zeros_like(blk)
            m0 = jnp.dot(w_ref[2 * a], jnp.where(lo_half, blk, zero), preferred_element_type=F32)
            m1 = jnp.dot(w_ref[2 * a + 1], jnp.where(lo_half, zero, blk), preferred_element_type=F32)
            slabs.append(m0 + m1)
        rows.append(jnp.concatenate(slabs, axis=1))
    return jnp.concatenate(rows, axis=0)


def _sgu_fwd(proj, ln_g, ln_b, w_mask, bias_full):
    s = proj.shape[0]
    tm = min(512, s)
    n_chunks = tm // SGU_CHUNK

    def body(ub_ref, vb_ref, zb_ref, g_ref, b_ref, w_ref, bias_ref, yb_ref):
        avg = _group_avg_matrix()
        ug, _, _, _, _, vn = _sgu_forward_parts(ub_ref[...].astype(F32), vb_ref[...].astype(F32),
                                                g_ref[...], b_ref[...], avg)
        mixed = _sgu_mix(w_ref, vn.astype(BF16), n_chunks) + jnp.concatenate([bias_ref[...]] * n_chunks, axis=0)
        zb = zb_ref[...].astype(F32)
        yb_ref[...] = (ug * mixed * (zb * _sigmoid(zb))).astype(BF16)

    col = lambda c: pl.BlockSpec((tm, D_BRANCH), lambda i: (i, c))
    full = lambda shape: pl.BlockSpec(shape, lambda i: (0,) * len(shape))
    return pl.pallas_call(
        body, name="sgu_fwd", grid=(s // tm,),
        in_specs=[col(COL_UB), col(COL_VB), col(COL_ZB), full((1, D_BRANCH)), full((1, D_BRANCH)),
                  full((N_GROUPS, SGU_CHUNK, SGU_CHUNK)), full((SGU_CHUNK, D_BRANCH))],
        out_specs=pl.BlockSpec((tm, D_BRANCH), lambda i: (i, 0)),
        out_shape=jax.ShapeDtypeStruct((s, D_BRANCH), BF16),
        compiler_params=_cparams(("parallel",)),
    )(proj, proj, proj, ln_g, ln_b, w_mask, bias_full)


def _sgu_bwd(proj, dyb, ln_g, ln_b, w_mask, w_mask_t, bias_full):
    s = proj.shape[0]
    tm = min(512, s)
    n_chunks = tm // SGU_CHUNK
    n_steps = s // tm

    def body(ub_ref, vb_ref, zb_ref, dyb_ref, g_ref, b_ref, w_ref, wt_ref, bias_ref,
             dsgu_ref, dw_ref, db_ref, dg_ref, dbeta_ref, dmix_acc):
        i = pl.program_id(0)

        @pl.when(i == 0)
        def _():
            dw_ref[...] = jnp.zeros_like(dw_ref)
            dg_ref[...] = jnp.zeros_like(dg_ref)
            dbeta_ref[...] = jnp.zeros_like(dbeta_ref)
            dmix_acc[...] = jnp.zeros_like(dmix_acc)

        avg = _group_avg_matrix()
        ln_gv = g_ref[...]
        ug, dug, dvg, rstd, vhat, vn = _sgu_forward_parts(ub_ref[...].astype(F32), vb_ref[...].astype(F32),
                                                          ln_gv, b_ref[...], avg)
        vnb = vn.astype(BF16)
        mixed = _sgu_mix(w_ref, vnb, n_chunks) + jnp.concatenate([bias_ref[...]] * n_chunks, axis=0)
        zb = zb_ref[...].astype(F32)
        sg = _sigmoid(zb)
        sz = zb * sg
        dsz = sg * (1.0 + zb * (1.0 - sg))
        dy = dyb_ref[...].astype(F32)
        dmixed = dy * ug * sz
        du = dy * mixed * sz * dug
        dzb = dy * ug * mixed * dsz
        dmb = dmixed.astype(BF16)
        dvn = _sgu_mix(wt_ref, dmb, n_chunks)

        lane = lax.broadcasted_iota(jnp.int32, (SGU_CHUNK, LANES), 1)
        lo_half = lane < GROUP_DIM
        dm_sum = None
        for n in range(n_chunks):
            rows = slice(SGU_CHUNK * n, SGU_CHUNK * (n + 1))
            dm_sum = dmixed[rows] if dm_sum is None else dm_sum + dmixed[rows]
            for a in range(D_BRANCH // LANES):
                cols = slice(LANES * a, LANES * (a + 1))
                dblk = dmb[rows, cols]
                vblk = vnb[rows, cols]
                zero = jnp.zeros_like(dblk)
                dw_ref[2 * a] += _dot_nt(jnp.where(lo_half, dblk, zero), vblk)
                dw_ref[2 * a + 1] += _dot_nt(jnp.where(lo_half, zero, dblk), vblk)
        dmix_acc[...] += dm_sum

        dg_ref[...] += jnp.sum(dvn * vhat, axis=0, keepdims=True)
        dbeta_ref[...] += jnp.sum(dvn, axis=0, keepdims=True)
        dvh = dvn * ln_gv
        m1 = _group_mean(dvh, avg)
        m2 = _group_mean(dvh * vhat, avg)
        dv = rstd * (dvh - m1 - vhat * m2) * dvg
        dsgu_ref[:, 0:D_BRANCH] = du.astype(BF16)
        dsgu_ref[:, D_BRANCH:2 * D_BRANCH] = dv.astype(BF16)
        dsgu_ref[:, 2 * D_BRANCH:3 * D_BRANCH] = dzb.astype(BF16)

        @pl.when(i == n_steps - 1)
        def _():
            pos = lax.broadcasted_iota(jnp.int32, (SGU_CHUNK, SGU_CHUNK), 0) >> GROUP_SHIFT
            src = lax.broadcasted_iota(jnp.int32, (SGU_CHUNK, SGU_CHUNK), 1) >> GROUP_SHIFT
            keep = src <= pos
            for g in range(N_GROUPS):
                dw_ref[g] = jnp.where(keep, dw_ref[g], 0.0)
            grp = lax.broadcasted_iota(jnp.int32, (D_BRANCH, LANES), 0) >> GROUP_SHIFT
            sel = (grp == lax.broadcasted_iota(jnp.int32, (D_BRANCH, LANES), 1)).astype(BF16)
            db_ref[...] = _split_dot(dmix_acc[...], sel, 3)

    col = lambda c: pl.BlockSpec((tm, D_BRANCH), lambda i: (i, c))
    full = lambda shape: pl.BlockSpec(shape, lambda i: (0,) * len(shape))
    return pl.pallas_call(
        body, name="sgu_bwd", grid=(n_steps,),
        in_specs=[col(COL_UB), col(COL_VB), col(COL_ZB), pl.BlockSpec((tm, D_BRANCH), lambda i: (i, 0)),
                  full((1, D_BRANCH)), full((1, D_BRANCH)),
                  full((N_GROUPS, SGU_CHUNK, SGU_CHUNK)), full((N_GROUPS, SGU_CHUNK, SGU_CHUNK)),
                  full((SGU_CHUNK, D_BRANCH))],
        out_specs=[pl.BlockSpec((tm, 3 * D_BRANCH), lambda i: (i, 0)),
                   full((N_GROUPS, SGU_CHUNK, SGU_CHUNK)), full((SGU_CHUNK, LANES)),
                   full((1, D_BRANCH)), full((1, D_BRANCH))],
        out_shape=[jax.ShapeDtypeStruct((s, 3 * D_BRANCH), BF16),
                   jax.ShapeDtypeStruct((N_GROUPS, SGU_CHUNK, SGU_CHUNK), F32),
                   jax.ShapeDtypeStruct((SGU_CHUNK, LANES), F32),
                   jax.ShapeDtypeStruct((1, D_BRANCH), F32), jax.ShapeDtypeStruct((1, D_BRANCH), F32)],
        scratch_shapes=[pltpu.VMEM((SGU_CHUNK, D_BRANCH), F32)],
        compiler_params=_cparams(("arbitrary",)),
    )(proj, proj, proj, dyb, ln_g, ln_b, w_mask, w_mask_t, bias_full)


def _mid(proj, ya, yb, o, x, target, final_g, w_up_a, w_up_b, w_out):
    s = x.shape[0]
    tm = min(256, s)
    n_steps = s // tm
    half = D_MODEL // 2

    def body(ya_ref, yb_ref, o_ref, za_ref, ga0_ref, ga1_ref, gb0_ref, gb1_ref, x_ref, t_ref, gf_ref,
             wa_ref, wb_ref, wo_ref,
             dzg_ref, do_ref, dyb_ref, dx2_ref, gwo_ref, gwa_ref, gwb_ref, loss_ref, dgf_ref,
             acc_o, acc_a, acc_b):
        i = pl.program_id(0)

        @pl.when(i == 0)
        def _():
            acc_o[...] = jnp.zeros_like(acc_o)
            acc_a[...] = jnp.zeros_like(acc_a)
            acc_b[...] = jnp.zeros_like(acc_b)
            loss_ref[...] = jnp.zeros_like(loss_ref)
            dgf_ref[...] = jnp.zeros_like(dgf_ref)

        ya_v = ya_ref[...]
        yb_v = yb_ref[...]
        pa = jnp.dot(ya_v, wa_ref[...], preferred_element_type=F32)
        pb = jnp.dot(yb_v, wb_ref[...], preferred_element_type=F32)
        sa = _sigmoid(jnp.concatenate([ga0_ref[...], ga1_ref[...]], axis=1).astype(F32))
        sb = _sigmoid(jnp.concatenate([gb0_ref[...], gb1_ref[...]], axis=1).astype(F32))
        merged = (sa * pa + sb * pb).astype(BF16)
        x2 = x_ref[...] + jnp.dot(merged, wo_ref[...], preferred_element_type=F32)
        r2 = lax.rsqrt(jnp.mean(x2 * x2, axis=-1, keepdims=True) + EPS)
        xh = x2 * r2
        gf = gf_ref[...]
        diff = xh * gf - t_ref[...]
        loss_ref[...] += 0.5 * jnp.sum(jnp.mean(diff * diff, axis=-1, keepdims=True))
        dy = diff * (1.0 / D_MODEL)
        dgf_ref[...] += jnp.sum(dy * xh, axis=0, keepdims=True)
        dyg = dy * gf
        dx2 = r2 * (dyg - xh * jnp.mean(dyg * xh, axis=-1, keepdims=True))
        dx2_ref[...] = dx2
        dx2b = dx2.astype(BF16)
        dmerged = _dot_nt(dx2b, wo_ref[...])
        acc_o[...] += _dot_tn(merged, dx2b)
        dpa = dmerged * sa
        dpb = dmerged * sb
        dzg_ref[:, D_BRANCH:D_BRANCH + D_MODEL] = (dpa * pa * (1.0 - sa)).astype(BF16)
        dzg_ref[:, D_BRANCH + D_MODEL:D_BRANCH + 2 * D_MODEL] = (dpb * pb * (1.0 - sb)).astype(BF16)
        dpab = dpa.astype(BF16)
        dpbb = dpb.astype(BF16)
        acc_a[...] += _dot_tn(ya_v, dpab)
        acc_b[...] += _dot_tn(yb_v, dpbb)
        dya = _dot_nt(dpab, wa_ref[...])
        dyb_ref[...] = _dot_nt(dpbb, wb_ref[...]).astype(BF16)
        za = za_ref[...].astype(F32)
        sg = _sigmoid(za)
        do_ref[...] = (dya * (za * sg)).astype(BF16)
        dzg_ref[:, 0:D_BRANCH] = (dya * o_ref[...].astype(F32) * (sg * (1.0 + za * (1.0 - sg)))).astype(BF16)

        @pl.when(i == n_steps - 1)
        def _():
            gwo_ref[...] = acc_o[...].astype(BF16)
            gwa_ref[...] = acc_a[...].astype(BF16)
            gwb_ref[...] = acc_b[...].astype(BF16)

    tok = lambda w: pl.BlockSpec((tm, w), lambda i: (i, 0))
    col = lambda c: pl.BlockSpec((tm, half), lambda i: (i, c))
    full = lambda shape: pl.BlockSpec(shape, lambda i: (0,) * len(shape))
    return pl.pallas_call(
        body, name="mid", grid=(n_steps,),
        in_specs=[tok(D_BRANCH), tok(D_BRANCH), tok(D_BRANCH), col(COL_ZA), col(COL_GA), col(COL_GA + 1),
                  col(COL_GB), col(COL_GB + 1), tok(D_MODEL), tok(D_MODEL), full((1, D_MODEL)),
                  full((D_BRANCH, D_MODEL)), full((D_BRANCH, D_MODEL)), full((D_MODEL, D_MODEL))],
        out_specs=[tok(D_BRANCH + 2 * D_MODEL), tok(D_BRANCH), tok(D_BRANCH), tok(D_MODEL),
                   full((D_MODEL, D_MODEL)), full((D_BRANCH, D_MODEL)), full((D_BRANCH, D_MODEL)),
                   full((8, LANES)), full((1, D_MODEL))],
        out_shape=[jax.ShapeDtypeStruct((s, D_BRANCH + 2 * D_MODEL), BF16),
                   jax.ShapeDtypeStruct((s, D_BRANCH), BF16), jax.ShapeDtypeStruct((s, D_BRANCH), BF16),
                   jax.ShapeDtypeStruct((s, D_MODEL), F32),
                   jax.ShapeDtypeStruct((D_MODEL, D_MODEL), BF16),
                   jax.ShapeDtypeStruct((D_BRANCH, D_MODEL), BF16), jax.ShapeDtypeStruct((D_BRANCH, D_MODEL), BF16),
                   jax.ShapeDtypeStruct((8, LANES), F32), jax.ShapeDtypeStruct((1, D_MODEL), F32)],
        scratch_shapes=[pltpu.VMEM((D_MODEL, D_MODEL), F32), pltpu.VMEM((D_BRANCH, D_MODEL), F32),
                        pltpu.VMEM((D_BRANCH, D_MODEL), F32)],
        compiler_params=_cparams(("arbitrary",)),
    )(ya, yb, o, proj, proj, proj, proj, proj, x, target, final_g, w_up_a, w_up_b, w_out)


def _dwin_piece(ht, piece, tile_of, prev):
    s = ht.shape[1]
    n_tiles = piece.shape[1] // D_BRANCH

    def body(ht_ref, p_ref, *rest):
        out_ref = rest[-1]
        out_ref[...] = jnp.dot(ht_ref[...], p_ref[...], preferred_element_type=F32).astype(BF16)

    in_specs = [pl.BlockSpec((D_MODEL, s), lambda j: (0, 0)), pl.BlockSpec((s, D_BRANCH), lambda j: (0, j))]
    args = [ht, piece]
    aliases = {}
    if prev is not None:
        in_specs.append(pl.BlockSpec(memory_space=pl.ANY))
        args.append(prev)
        aliases = {2: 0}
    return pl.pallas_call(
        body, name="dwin_piece", grid=(n_tiles,),
        in_specs=in_specs,
        out_specs=pl.BlockSpec((D_MODEL, D_BRANCH), lambda j: (0, tile_of(j))),
        out_shape=jax.ShapeDtypeStruct((D_MODEL, D_IN), BF16),
        input_output_aliases=aliases,
        compiler_params=_cparams(("parallel",)),
    )(*args)


def _dh_dx(pieces, w_in, x, norm_g, dx2, g_in):
    s = x.shape[0]
    tm = min(256, s)
    n_steps = s // tm
    arrays = []
    for arr, _, _, _ in pieces:
        if not any(arr is a for a in arrays):
            arrays.append(arr)
    n_arr = len(arrays)
    plan = [([k for k, a in enumerate(arrays) if a is arr][0], wcol, off, width) for arr, wcol, off, width in pieces]

    def body(*refs):
        p_refs = refs[:n_arr]
        w_ref, x_ref, g_ref, dx2_ref, gin_ref, dx_ref, dg_ref, late_ref = refs[n_arr:n_arr + 8]
        step = pl.program_id(0)
        finish = _presum_program(step == 0, step == min(1, n_steps - 1), step == n_steps - 1, 0, _LATE_CHIPS,
                                 gin_ref, late_ref, refs[n_arr + 8:])

        @pl.when(step == 0)
        def _():
            dg_ref[...] = jnp.zeros_like(dg_ref)

        dh = None
        for k, wcol, off, width in plan:
            d = _dot_nt(p_refs[k][:, off:off + width], w_ref[:, wcol:wcol + width])
            dh = d if dh is None else dh + d
        xf = x_ref[...]
        r = lax.rsqrt(jnp.mean(xf * xf, axis=-1, keepdims=True) + EPS)
        xh = xf * r
        dg_ref[...] += jnp.sum(dh * xh, axis=0, keepdims=True)
        dhg = dh * g_ref[...]
        dx_ref[...] = r * (dhg - xh * jnp.mean(dhg * xh, axis=-1, keepdims=True)) + dx2_ref[...]

        @pl.when(step == n_steps - 1)
        def _():
            finish()

    tok = lambda w: pl.BlockSpec((tm, w), lambda i: (i, 0))
    full = lambda shape: pl.BlockSpec(shape, lambda i: (0,) * len(shape))
    any_spec = pl.BlockSpec(memory_space=pl.ANY)
    return pl.pallas_call(
        body, name="dh_dx", grid=(n_steps,),
        in_specs=[tok(a.shape[1]) for a in arrays] + [full((D_MODEL, D_IN)), tok(D_MODEL), full((1, D_MODEL)),
                                                      tok(D_MODEL), any_spec],
        out_specs=[tok(D_MODEL), full((1, D_MODEL)), any_spec],
        out_shape=[jax.ShapeDtypeStruct((s, D_MODEL), F32), jax.ShapeDtypeStruct((1, D_MODEL), F32),
                   jax.ShapeDtypeStruct((N_DEV,) + _GRAD_PIECE_SHAPES[0], BF16)],
        scratch_shapes=_presum_scratch(0, _LATE_CHIPS),
        compiler_params=pltpu.CompilerParams(dimension_semantics=("arbitrary",), vmem_limit_bytes=VMEM_LIMIT,
                                             has_side_effects=True),
    )(*arrays, w_in, x, norm_g, dx2, g_in)


def _adamw(w, g, m, v):
    rows, cols = w.shape
    tr = max(t for t in range(8, 257, 8) if rows % t == 0)
    c1 =1.0 - ADAM_B1 ** ADAM_STEP
    c2 = 1.0 - ADAM_B2 ** ADAM_STEP

    def body(w_ref, g_ref, m_ref, v_ref, g_out_ref, d_ref, nm_ref, nv_ref):
        gv = g_ref[...]
        g_out_ref[...] = gv
        nm = ADAM_B1 * m_ref[...] + (1.0 - ADAM_B1) * gv
        nv = ADAM_B2 * v_ref[...] + (1.0 - ADAM_B2) * (gv * gv)
        d_ref[...] = -ADAM_LR * ((nm / c1) / (jnp.sqrt(nv / c2) + ADAM_EPS) + ADAM_WD * w_ref[...])
        nm_ref[...] = nm
        nv_ref[...] = nv

    spec = pl.BlockSpec((tr, cols), lambda i: (i, 0))
    return pl.pallas_call(
        body, name="adamw", grid=(rows // tr,),
        in_specs=[spec] * 4, out_specs=[spec] * 4,
        out_shape=[jax.ShapeDtypeStruct((rows, cols), F32)] * 4,
        compiler_params=_cparams(("parallel",)),
    )(w, g, m, v)


def _reduce_grads_tail(grads, g_small, early_slots, late_in_slots):
    n_big = len(grads)
    n_arr = n_big + 1
    shard_shapes = [(2 * r, w) for r, w in _GRAD_PIECE_SHAPES]
    small_piece = (SMALL_PIECE, LANES)

    def body(*refs):
        src = refs[:n_arr]
        early = refs[n_arr:n_arr + n_big]
        late_in = refs[n_arr + n_big]
        n_in = n_arr + n_big + 1
        out = refs[n_in:n_in + n_arr]
        slots = refs[n_in + n_arr:n_in + 2 * n_arr]
        sums = refs[n_in + 2 * n_arr:n_in + 3 * n_arr]
        send1, recv1, send2, recv2, local_sems = refs[n_in + 3 * n_arr:]
        x, y, c = _place()
        me = 4 * x + 2 * y + c

        def piece_of(a, dev):
            return src[a].at[dev] if a == n_big else _grad_piece(src[a], a, dev)

        def late(a, dst_dev, src_dev):
            return pltpu.make_async_remote_copy(
                src_ref=piece_of(a, dst_dev), dst_ref=slots[a].at[src_dev],
                send_sem=send1.at[n_arr * dst_dev + a], recv_sem=recv1.at[n_arr * src_dev + a],
                device_id=_dev_id(dst_dev), device_id_type=MESH)

        def late_arrays(dev):
            return (n_big,)

        def load(a, dev, received):
            return pltpu.make_async_copy(received.at[dev], slots[a].at[dev], local_sems.at[n_arr * dev + a])

        def own(a, dev):
            return pltpu.make_async_copy(piece_of(a, dev), slots[a].at[dev], local_sems.at[n_arr * dev + a])

        for dev in range(N_DEV):
            @pl.when(me == dev)
            def _():
                received = [early[0] if dev in _EARLY_IN_DEVS else late_in] + list(early[1:])
                for a in range(n_arr):
                    own(a, dev).start()
                filled = lambda a, peer: peer in _presum_sources(dev)
                for peer in range(N_DEV):
                    if peer != dev:
                        for a in late_arrays(peer):
                            late(a, peer, dev).start()
                        for a in range(n_big):
                            if filled(a, peer):
                                load(a, peer, received[a]).start()
                            else:
                                slots[a][peer] = jnp.zeros(slots[a].shape[1:], slots[a].dtype)
                for a in range(n_arr):
                    own(a, dev).wait()
                for peer in range(N_DEV):
                    if peer != dev:
                        for a in late_arrays(dev):
                            late(a, dev, peer).wait_recv()
                        for a in range(n_big):
                            if filled(a, peer):
                                load(a, peer, received[a]).wait()

        for a in range(n_arr):
            rows = slots[a].shape[1]
            step = 64 if rows % 64 == 0 else 8

            def add_rows(t, carry, a=a, step=step):
                r0 = pl.multiple_of(t * step, step)
                total = slots[a][0, pl.ds(r0, step), :].astype(F32)
                for dev in range(1, N_DEV):
                    total = total + slots[a][dev, pl.ds(r0, step), :].astype(F32)
                sums[a][pl.ds(r0, step), :] = total
                return carry

            lax.fori_loop(0, rows // step, add_rows, 0)

        shares = []
        keeps = []
        for a in range(n_big):
            r, w = _GRAD_PIECE_SHAPES[a]
            dst = out[a].at[pl.ds(pl.multiple_of(c * r, 8), r), :]
            cp = pltpu.make_async_remote_copy(src_ref=sums[a], dst_ref=dst, send_sem=send2.at[a], recv_sem=recv2.at[a],
                                              device_id=(x, y, 1 - c), device_id_type=MESH)
            cp.start()
            shares.append(cp)
            kp = pltpu.make_async_copy(sums[a], dst, local_sems.at[N_DEV * n_arr + a])
            kp.start()
            keeps.append(kp)
        kp = pltpu.make_async_copy(sums[n_big], out[n_big].at[me], local_sems.at[N_DEV * n_arr + n_big])
        kp.start()
        keeps.append(kp)

        def small_share(dst_dev, src_dev):
            return pltpu.make_async_remote_copy(src_ref=sums[n_big], dst_ref=out[n_big].at[src_dev],
                                                send_sem=send2.at[n_big + dst_dev], recv_sem=recv2.at[n_big + src_dev],
                                                device_id=_dev_id(dst_dev), device_id_type=MESH)

        for dev in range(N_DEV):
            @pl.when(me != dev)
            def _():
                small_share(dev, me).start()
        for a in range(n_big):
            r, w = _GRAD_PIECE_SHAPES[a]
            other = out[a].at[pl.ds(pl.multiple_of((1 - c) * r, 8), r), :]
            pltpu.make_async_remote_copy(src_ref=sums[a], dst_ref=other, send_sem=send2.at[a], recv_sem=recv2.at[a],
                                         device_id=(x, y, 1 - c), device_id_type=MESH).wait_recv()
        for dev in range(N_DEV):
            @pl.when(me != dev)
            def _():
                small_share(dev, dev).wait_recv()
                small_share(dev, me).wait_send()
                for a in late_arrays(dev):
                    late(a, dev, me).wait_send()
        for cp in shares:
            cp.wait_send()
        for kp in keeps:
            kp.wait()

    any_spec = pl.BlockSpec(memory_space=pl.ANY)
    return pl.pallas_call(
        body, name="reduce_grads_tail",
        in_specs=[any_spec] * (n_arr + n_big + 1), out_specs=[any_spec] * n_arr,
        out_shape=[jax.ShapeDtypeStruct(sh, F32) for sh in shard_shapes]
        + [jax.ShapeDtypeStruct((N_DEV,) + small_piece, F32)],
        scratch_shapes=[pltpu.VMEM((N_DEV,) + sh, BF16) for sh in _GRAD_PIECE_SHAPES]
        + [pltpu.VMEM((N_DEV,) + small_piece, F32)]
        + [pltpu.VMEM(sh, F32) for sh in _GRAD_PIECE_SHAPES] + [pltpu.VMEM(small_piece, F32)]
        + [pltpu.SemaphoreType.DMA((N_DEV * n_arr,)), pltpu.SemaphoreType.DMA((N_DEV * n_arr,)),
           pltpu.SemaphoreType.DMA((n_big + N_DEV,)), pltpu.SemaphoreType.DMA((n_big + N_DEV,)),
           pltpu.SemaphoreType.DMA((N_DEV * n_arr + n_arr,))],
        compiler_params=pltpu.CompilerParams(vmem_limit_bytes=VMEM_LIMIT, has_side_effects=True),
    )(*grads, g_small, *early_slots, late_in_slots)


_SMALL_PARTS = (("norm_g", 8), ("sgu_ln_g", 8), ("sgu_ln_b", 8), ("w_spatial", 1024), ("b_spatial", 8),
                ("final_norm_g", 8))
_LOSS_ROW = sum(n for _, n in _SMALL_PARTS)


def _pack_small(parts, loss_tile=None):
    rows = []
    for name, n_rows in _SMALL_PARTS:
        a = parts[name].reshape(-1, LANES).astype(F32)
        a = jnp.pad(a, ((0, n_rows - a.shape[0]), (0, 0)))
        rows.append(a)
    rows.append(jnp.zeros((8, LANES), F32) if loss_tile is None else loss_tile)
    rows.append(jnp.zeros((SMALL_ROWS - _LOSS_ROW - 8, LANES), F32))
    return jnp.concatenate(rows, axis=0)


def _unpack_small(packed, shapes):
    out = {}
    r0 = 0
    for name, n_rows in _SMALL_PARTS:
        n = math.prod(shapes[name])
        out[name] = packed[r0:r0 + n // LANES].reshape(shapes[name])
        r0 += n_rows
    return out


def _local_step(proj, ht, x, target, norm_g, w_in, sgu_ln_g, sgu_ln_b, w_spatial, b_spatial, w_up_a, w_up_b, w_out,
                final_norm_g, bq, bk):
    pos = jnp.arange(SGU_CHUNK)
    keep = (pos[None, :] // SGU_SUBCHUNK) <= (pos[:, None] // SGU_SUBCHUNK)
    w_mask = jnp.where(keep[None], w_spatial, 0.0).astype(BF16)
    w_mask_t = jnp.swapaxes(w_mask, 1, 2)
    bias_full = jnp.repeat(b_spatial.T, GROUP_DIM, axis=1)
    ln_g = sgu_ln_g.reshape(1, D_BRANCH)
    ln_b = sgu_ln_b.reshape(1, D_BRANCH)
    final_g = final_norm_g.reshape(1, D_MODEL)

    o, ya, rsave = _attn_fwd(proj, bq, bk, ATTN_PAIRS)
    yb = _sgu_fwd(proj, ln_g, ln_b, w_mask, bias_full)
    dzg, do, dyb, dx2, g_out, g_up_a, g_up_b, loss_acc, d_final = _mid(
        proj, ya, yb, o, x, target, final_g, w_up_a, w_up_b, w_out)
    dsgu, d_wsp, d_bsp, d_lng, d_lnb = _sgu_bwd(proj, dyb, ln_g, ln_b, w_mask, w_mask_t, bias_full)
    g_in = _dwin_piece(ht, dzg, lambda j: jnp.where(j == 0, COL_ZA, COL_GA - 1 + j), None)
    g_in = _dwin_piece(ht, dsgu, lambda j: COL_UB + j, g_in)
    dq, dk, dv, *early_slots = _attn_bwd(proj, do, rsave, bq, bk, ATTN_PAIRS, (g_in, g_up_a, g_up_b, g_out))
    g_in = _dwin_piece(ht, dq, lambda j: COL_Q + j, g_in)
    g_in = _dwin_piece(ht, dk, lambda j: COL_K + j, g_in)
    g_in = _dwin_piece(ht, dv, lambda j: COL_V + j, g_in)
    pieces = [(dq, COL_Q * D_BRANCH, 0, D_BRANCH), (dk, COL_K * D_BRANCH, 0, D_BRANCH),
              (dv, COL_V * D_BRANCH, 0, D_BRANCH), (dzg, COL_ZA * D_BRANCH, 0, D_BRANCH),
              (dsgu, COL_UB * D_BRANCH, 0, 3 * D_BRANCH), (dzg, COL_GA * D_BRANCH, D_BRANCH, 2 * D_MODEL)]
    dx, d_norm, late_in_slots = _dh_dx(pieces, w_in, x, norm_g, dx2, g_in)
    small = {"norm_g": d_norm, "sgu_ln_g": d_lng, "sgu_ln_b": d_lnb, "w_spatial": d_wsp,
             "b_spatial": d_bsp[:, :N_GROUPS].T, "final_norm_g": d_final}
    return loss_acc, dx, (g_in, g_up_a, g_up_b, g_out), small, early_slots, late_in_slots


def kernel(x, norm_g, w_in, sgu_ln_g, sgu_ln_b, w_spatial, b_spatial, w_up_a, w_up_b, w_out, final_norm_g, loss_target, m_norm_g, m_w_in, m_sgu_ln_g, m_sgu_ln_b, m_w_spatial, m_b_spatial, m_w_up_a, m_w_up_b, m_w_out, m_final_norm_g, v_norm_g, v_w_in, v_sgu_ln_g, v_sgu_ln_b, v_w_spatial, v_b_spatial, v_w_up_a, v_w_up_b, v_w_out, v_final_norm_g):
    big_names = ("w_in", "w_up_a", "w_up_b", "w_out")
    small_names = tuple(n for n, _ in _SMALL_PARTS)
    names = ("norm_g", "w_in", "sgu_ln_g", "sgu_ln_b", "w_spatial", "b_spatial", "w_up_a", "w_up_b", "w_out",
             "final_norm_g")
    w = dict(norm_g=norm_g, w_in=w_in, sgu_ln_g=sgu_ln_g, sgu_ln_b=sgu_ln_b, w_spatial=w_spatial,
             b_spatial=b_spatial, w_up_a=w_up_a, w_up_b=w_up_b, w_out=w_out, final_norm_g=final_norm_g)
    m = dict(norm_g=m_norm_g, w_in=m_w_in, sgu_ln_g=m_sgu_ln_g, sgu_ln_b=m_sgu_ln_b, w_spatial=m_w_spatial,
             b_spatial=m_b_spatial, w_up_a=m_w_up_a, w_up_b=m_w_up_b, w_out=m_w_out, final_norm_g=m_final_norm_g)
    v = dict(norm_g=v_norm_g, w_in=v_w_in, sgu_ln_g=v_sgu_ln_g, sgu_ln_b=v_sgu_ln_b, w_spatial=v_w_spatial,
             b_spatial=v_b_spatial, w_up_a=v_w_up_a, w_up_b=v_w_up_b, w_out=v_w_out, final_norm_g=v_final_norm_g)
    shapes = {n: w[n].shape for n in names}
    flat2d = lambda a: a.reshape(a.shape[-2:])

    proj, ht, *full = _in_proj_gather(x[0], norm_g, *[flat2d(w[n]) for n in big_names])
    loss, dx, big_grads, small, early_slots, late_in_slots = _local_step(
        proj, ht, x[0], loss_target[0], norm_g, full[0], sgu_ln_g[0], sgu_ln_b[0], w_spatial[0], b_spatial[0],
        full[1], full[2], full[3], final_norm_g, ATTN_Q_BLOCK, ATTN_K_BLOCK)
    packed = _pack_small(small, loss).reshape(N_DEV, SMALL_PIECE, LANES)
    red = _reduce_grads_tail(big_grads, packed, early_slots, late_in_slots)

    grads, deltas, new_m, new_v = {}, {}, {}, {}
    for n, g in zip(big_names, red[:4]):
        g, d, nm, nv = _adamw(flat2d(w[n]), g, flat2d(m[n]), flat2d(v[n]))
        grads[n], deltas[n], new_m[n], new_v[n] = (a.reshape(shapes[n]) for a in (g, d, nm, nv))
    g_small = red[4].reshape(SMALL_ROWS, LANES)
    g_small, d, nm, nv = _adamw(_pack_small({n: w[n] for n in small_names}), g_small,
                                _pack_small({n: m[n] for n in small_names}),
                                _pack_small({n: v[n] for n in small_names}))
    for src, dst in ((g_small, grads), (d, deltas), (nm, new_m), (nv, new_v)):
        dst.update(_unpack_small(src, shapes))

    return (g_small[_LOSS_ROW, 0], dx[None], *[grads[n] for n in names], *[deltas[n] for n in names],
            *[new_m[n] for n in names], *[new_v[n] for n in names])
```

```python
import math

import jax
import jax.numpy as jnp
from jax import lax
from jax.experimental import pallas as pl
from jax.experimental.pallas import tpu as pltpu

F32 = jnp.float32
BF16 = jnp.bfloat16

D_MODEL = 1024
N_HEADS = 8
HEAD_DIM = 64
D_BRANCH = 512
D_IN = 4 * D_BRANCH + 3 * D_BRANCH + 2 * D_MODEL
N_GROUPS = 8
GROUP_DIM = 64
SGU_CHUNK = 128
SGU_SUBCHUNK = 64
GROUP_SHIFT = 6
EPS = 1e-6
LANES = 128
ATTN_Q_BLOCK = 256
ATTN_K_BLOCK = 256
DEAD = -110.0
SKIPPED = -1e30
SCAN_PASSES = 1
ATTN_PAIRS = 2
N_CHIPS = 4
N_DEV = 8
MESH = pl.DeviceIdType.MESH

ADAM_LR = 0.001
ADAM_B1 = 0.9
ADAM_B2 = 0.999
ADAM_EPS = 1e-08
ADAM_WD = 0.01
ADAM_STEP = 10

COL_Q, COL_K, COL_V, COL_ZA, COL_UB, COL_VB, COL_ZB, COL_GA, COL_GB = 0, 1, 2, 3, 4, 5, 6, 7, 9

VMEM_LIMIT = 56 * 1024 * 1024

SMALL_ROWS = 1088
SMALL_PIECE = SMALL_ROWS // N_DEV


def _cparams(sem=None):
    return pltpu.CompilerParams(dimension_semantics=sem, vmem_limit_bytes=VMEM_LIMIT)


def _aligned(v, m):
    return v if isinstance(v, int) else pl.multiple_of(v, m)


def _sigmoid(x):
    return 1.0 / (1.0 + jnp.exp(-x))


def _gelu_and_grad(x):
    k = math.sqrt(2.0 / math.pi)
    x2 = x * x
    inner = k * (x + 0.044715 * x * x2)
    th = jnp.tanh(inner)
    g = 0.5 * x * (1.0 + th)
    dg = 0.5 * (1.0 + th) + 0.5 * x * (1.0 - th * th) * (k * (1.0 + 3.0 * 0.044715 * x2))
    return g, dg


def _split_dot(a, b_bf16, passes):
    out = None
    rem = a
    for _ in range(passes):
        part = rem.astype(BF16)
        d = jnp.dot(part, b_bf16, preferred_element_type=F32)
        out = d if out is None else out + d
        rem = rem - part.astype(F32)
    return out


def _dot_nt(a, b):
    return lax.dot_general(a, b, (((1,), (1,)), ((), ())), preferred_element_type=F32)


def _dot_tn(a, b):
    return lax.dot_general(a, b, (((0,), (0,)), ((), ())), preferred_element_type=F32)


def _place():
    x, y, c = lax.axis_index("x"), lax.axis_index("y"), lax.axis_index("c")
    return x, y, c


def _in_proj_gather(x, norm_g, w_in, w_up_a, w_up_b, w_out):
    s = x.shape[0]
    tm = min(1024, s)
    nt = s // tm
    shards = (w_in, w_up_a, w_up_b, w_out)
    n_arr = len(shards)
    col_sharded = (True, True, True, False)
    full_shapes = ((D_MODEL, D_IN), (D_BRANCH, D_MODEL), (D_BRANCH, D_MODEL), (D_MODEL, D_MODEL))
    w_shard = w_in.shape[1]
    half_rows = D_MODEL // 2
    stage_rows = 256

    def body(order_ref, x_ref, g_ref, *refs):
        src = refs[:n_arr]
        proj_ref, ht_ref = refs[n_arr:n_arr + 2]
        out = refs[n_arr + 2:2 * n_arr + 2]
        wsc, h_scr, stage = refs[2 * n_arr + 2:2 * n_arr + 5]
        small_stage = refs[2 * n_arr + 5:2 * n_arr + 8]
        small_cast = refs[2 * n_arr + 8:2 * n_arr + 11]
        send_sems, recv_sems, local_sems = refs[2 * n_arr + 11:]
        k = pl.program_id(0)
        i = pl.program_id(1)
        x_, y_, c = _place()
        chip = 2 * x_ + y_
        sibling = (x_, y_, 1 - c)
        others = [(x_, 1 - y_), (1 - x_, y_), (1 - x_, 1 - y_)]

        def region(a, chip_idx, half):
            if a == 0:
                return wsc.at[chip_idx, pl.ds(_aligned(half * half_rows, 16), half_rows), :]
            r, w = shards[a].shape
            hr = r // 2
            if col_sharded[a]:
                return out[a].at[pl.ds(_aligned(half * hr, 16), hr), pl.ds(_aligned(chip_idx * w, LANES), w)]
            return out[a].at[pl.ds(_aligned(chip_idx * r + half * hr, 16), hr), :]

        def remote(kk, a, chip_idx, half, to, own):
            s_ref = region(a, chip_idx, half)
            if own and a > 0:
                hr = shards[a].shape[0] // 2
                s_ref = small_cast[a - 1].at[pl.ds(_aligned(half * hr, 16), hr), :]
            return pltpu.make_async_remote_copy(src_ref=s_ref, dst_ref=region(a, chip_idx, half),
                                                send_sem=send_sems.at[kk], recv_sem=recv_sems.at[kk],
                                                device_id=to, device_id_type=MESH)

        def keep_whole(kk, chip_idx):
            return pltpu.make_async_copy(wsc.at[chip_idx],
                                         out[0].at[:, pl.ds(_aligned(chip_idx * w_shard, LANES), w_shard)],
                                         local_sems.at[kk])

        def small_stores():
            cps = []
            for a in range(1, n_arr):
                hr = shards[a].shape[0] // 2
                for half in range(2):
                    cps.append(pltpu.make_async_copy(small_cast[a - 1].at[pl.ds(half * hr, hr), :],
                                                     region(a, chip, half), local_sems.at[4 + 2 * (a - 1) + half]))
            return cps

        def arrive_and_pass(j):
            ochip = chip ^ j
            for a in range(n_arr):
                kk = n_arr * (j - 1) + a
                remote(kk, a, ochip, c, sibling, False).wait_recv()
                remote(3 * n_arr + kk, a, ochip, c, sibling, False).start()

        def from_sibling(j, a):
            remote(3 * n_arr + n_arr * (j - 1) + a, a, chip ^ j, 1 - c, sibling, False).wait_recv()

        @pl.when((k == 0) & (i == 0))
        def _():
            def cast_rows(half):
                for t in range(half_rows // stage_rows):
                    r0 = pl.multiple_of(half * half_rows + t * stage_rows, stage_rows)
                    pltpu.sync_copy(src[0].at[pl.ds(r0, stage_rows), :], stage)
                    wsc[chip, pl.ds(r0, stage_rows), :] = stage[...].astype(BF16)

            cast_rows(c)
            for j in (1, 2):
                remote(n_arr * (j - 1), 0, chip, c, (*others[j - 1], c), True).start()
            cast_rows(1 - c)
            for a in range(1, n_arr):
                pltpu.sync_copy(src[a], small_stage[a - 1])
                small_cast[a - 1][...] = small_stage[a - 1][...].astype(BF16)
            for j in (1, 2):
                for a in range(1, n_arr):
                    remote(n_arr * (j - 1) + a, a, chip, c, (*others[j - 1], c), True).start()
            keep_whole(0, chip).start()
            for cp in small_stores():
                cp.start()

        @pl.when((k == 1) & (i == 0))
        def _():
            for j in (1, 2):
                remote(n_arr * (j - 1), 0, chip, c, (*others[j - 1], c), True).wait_send()
            for a in range(n_arr):
                remote(n_arr * 2 + a, a, chip, c, (*others[2], c), True).start()
            arrive_and_pass(1)
            arrive_and_pass(2)
            from_sibling(1, 0)
            keep_whole(1, chip ^ 1).start()

        @pl.when((k == 2) & (i == 0))
        def _():
            from_sibling(2, 0)
            keep_whole(2, chip ^ 2).start()
            arrive_and_pass(3)

        @pl.when((k == 3) & (i == 0))
        def _():
            from_sibling(3, 0)
            keep_whole(3, chip ^ 3).start()

        @pl.when(k == 0)
        def _():
            xf = x_ref[...]
            r = lax.rsqrt(jnp.mean(xf * xf, axis=-1, keepdims=True) + EPS)
            h = xf * r * g_ref[...]
            h_scr[i] = h.astype(BF16)
            ht_ref[...] = h.T.astype(BF16)

        proj_ref[...] = jnp.dot(h_scr[i], wsc[order_ref[k]], preferred_element_type=F32).astype(BF16)

        @pl.when((k == 3) & (i == nt - 1))
        def _():
            for j in (1, 2, 3):
                for a in range(1, n_arr):
                    from_sibling(j, a)
            for j in (1, 2, 3):
                for a in range(n_arr):
                    kk = n_arr * (j - 1) + a
                    if a > 0 or j == 3:
                        remote(kk, a, chip, c, (*others[j - 1], c), True).wait_send()
                    remote(3 * n_arr + kk, a, chip ^ j, c, sibling, False).wait_send()
            for kk in range(4):
                keep_whole(kk, chip ^ kk).wait()
            for cp in small_stores():
                cp.wait()

    any_spec = pl.BlockSpec(memory_space=pl.ANY)
    tile = lambda kk, ii: jnp.where(kk == 0, ii, nt - 1)
    grid_spec = pltpu.PrefetchScalarGridSpec(
        num_scalar_prefetch=1, grid=(N_CHIPS, nt),
        in_specs=[pl.BlockSpec((tm, D_MODEL), lambda kk, ii, order: (tile(kk, ii), 0)),
                  pl.BlockSpec((1, D_MODEL), lambda kk, ii, order: (0, 0))] + [any_spec] * n_arr,
        out_specs=[pl.BlockSpec((tm, w_shard), lambda kk, ii, order: (ii, order[kk])),
                   pl.BlockSpec((D_MODEL, tm), lambda kk, ii, order: (0, tile(kk, ii)))] + [any_spec] * n_arr,
        scratch_shapes=[pltpu.VMEM((N_CHIPS, D_MODEL, w_shard), BF16), pltpu.VMEM((nt, tm, D_MODEL), BF16),
                        pltpu.VMEM((stage_rows, w_shard), F32)]
        + [pltpu.VMEM(a.shape, F32) for a in shards[1:]] + [pltpu.VMEM(a.shape, BF16) for a in shards[1:]]
        + [pltpu.SemaphoreType.DMA((6 * n_arr,)), pltpu.SemaphoreType.DMA((6 * n_arr,)),
           pltpu.SemaphoreType.DMA((4 + 2 * (n_arr - 1),))])
    x_, y_, _ = _place()
    order = (2 * x_ + y_) ^ jnp.arange(N_CHIPS, dtype=jnp.int32)
    return pl.pallas_call(
        body, name="in_proj_gather", grid_spec=grid_spec,
        out_shape=[jax.ShapeDtypeStruct((s, D_IN), BF16), jax.ShapeDtypeStruct((D_MODEL, s), BF16)]
        + [jax.ShapeDtypeStruct(sh, BF16) for sh in full_shapes],
        compiler_params=pltpu.CompilerParams(dimension_semantics=("arbitrary", "arbitrary"),
                                             vmem_limit_bytes=VMEM_LIMIT, has_side_effects=True),
    )(order, x, norm_g, *shards)


def _neg_softplus_parts(z):
    zb = z.astype(BF16)
    p = jnp.exp(-jnp.abs(zb))
    return p, jnp.maximum(zb, jnp.zeros_like(zb)) + jnp.log(1.0 + p)


def _split_cat(a, passes):
    parts = []
    rem = a
    for k in range(passes):
        part = rem.astype(BF16)
        parts.append(part)
        if k + 1 < passes:
            rem = rem - part.astype(F32)
    return parts[0] if passes == 1 else jnp.concatenate(parts, axis=1)


def _tri(blk, upper, sign):
    row = lax.broadcasted_iota(jnp.int32, (blk, blk), 0)
    col = lax.broadcasted_iota(jnp.int32, (blk, blk), 1)
    keep = (row <= col) if upper else (row >= col)
    t = jnp.where(keep, sign, 0.0).astype(BF16)
    return t if SCAN_PASSES == 1 else jnp.concatenate([t] * SCAN_PASSES, axis=0)


def _attn_fwd(proj, bq, bk, npairs):
    s = proj.shape[0]
    nq = s // bq
    ratio = bq // bk
    scale = HEAD_DIM ** -0.5
    heads = tuple(range(2 * npairs))
    width = LANES * npairs

    def body(q_ref, k_ref, v_ref, za_ref, o_ref, ya_ref, rs_ref, acc_ref, r_ref):
        i = pl.program_id(1)
        lane = lax.broadcasted_iota(jnp.int32, (bq, LANES), 1)
        lo_half = lane < HEAD_DIM
        qm = []
        for pr in range(npairs):
            q = q_ref[:, LANES * pr:LANES * (pr + 1)] * jnp.asarray(scale, BF16)
            zero = jnp.zeros_like(q)
            qm += [jnp.where(lo_half, q, zero), jnp.where(lo_half, zero, q)]
        row = lax.broadcasted_iota(jnp.int32, (bq, bk), 0)
        col = lax.broadcasted_iota(jnp.int32, (bq, bk), 1)
        tneg = _tri(bk, False, -1.0)
        acc_ref[...] = jnp.zeros_like(acc_ref)
        r_ref[...] = jnp.zeros_like(r_ref)
        rs_ref[...] = jnp.full_like(rs_ref, SKIPPED)

        def scores(j):
            ks = pl.multiple_of(j * bk, bk)
            return [_dot_nt(qm[h], k_ref[pl.ds(ks, bk), LANES * (h // 2):LANES * (h // 2 + 1)]) for h in heads]

        def block(j, diag, valid=None):
            ks = pl.multiple_of(j * bk, bk)
            vj = [v_ref[pl.ds(ks, bk), LANES * pr:LANES * (pr + 1)] for pr in range(npairs)]
            if diag:
                before = (j * bk + col) < (i * bq + row)
            z = scores(j)
            sp = [_neg_softplus_parts(z[h])[1] for h in heads]
            if diag:
                sp = [jnp.where(before, sp[h], 0.0) for h in heads]
            cin = [jnp.dot(_split_cat(sp[h], SCAN_PASSES), tneg, preferred_element_type=F32) for h in heads]
            w = [jnp.exp(z[h] + cin[h]) for h in heads]
            if diag:
                w = [jnp.where(before, w[h], 0.0) for h in heads]
            pv = [jnp.dot(w[h].astype(BF16), vj[h // 2], preferred_element_type=F32) for h in heads]
            r = [r_ref[h] for h in heads]
            keep = 1.0 if valid is None else valid.astype(F32)
            for h in heads:
                acc_ref[h] += pv[h] * (jnp.exp(r[h]) * keep)
                r_ref[h] = r[h] + cin[h][:, 0:1] * keep
            for pr in range(npairs):
                hit = [lane == j, lane == j + HEAD_DIM]
                if valid is not None:
                    hit = [m & valid for m in hit]
                rs_ref[pr] = jnp.where(hit[0], r[2 * pr], jnp.where(hit[1], r[2 * pr + 1], rs_ref[pr]))

        for t in range(ratio):
            block(i * ratio + ratio - 1 - t, True)
        block(jnp.maximum(i * ratio - 1, 0), False, valid=i > 0)

        def alive(carry):
            jj, r_max = carry
            return (jj < i * ratio - 1) & (r_max > DEAD)

        def loop_body(carry):
            jj, _ = carry
            block(i * ratio - 2 - jj, False)
            return jj + 1, jnp.max(r_ref[...])

        lax.while_loop(alive, loop_body, (0, jnp.max(r_ref[...])))
        for pr in range(npairs):
            cols = slice(LANES * pr, LANES * (pr + 1))
            o = jnp.where(lo_half, acc_ref[2 * pr], acc_ref[2 * pr + 1])
            o_ref[:, cols] = o.astype(BF16)
            za = za_ref[:, cols].astype(F32)
            ya_ref[:, cols] = (o * (za * _sigmoid(za))).astype(BF16)

    n_steps = N_HEADS // (2 * npairs)
    return pl.pallas_call(
        body, name="attn_fwd", grid=(n_steps, nq),
        in_specs=[pl.BlockSpec((bq, width), lambda p, i: (i, n_steps * COL_Q + p)),
                  pl.BlockSpec((s, width), lambda p, i: (0, n_steps * COL_K + p)),
                  pl.BlockSpec((s, width), lambda p, i: (0, n_steps * COL_V + p)),
                  pl.BlockSpec((bq, width), lambda p, i: (i, n_steps * COL_ZA + p))],
        out_specs=[pl.BlockSpec((bq, width), lambda p, i: (i, p)),
                   pl.BlockSpec((bq, width), lambda p, i: (i, p)),
                   pl.BlockSpec((npairs, bq, LANES), lambda p, i: (p, i, 0))],
        out_shape=[jax.ShapeDtypeStruct((s, D_BRANCH), BF16), jax.ShapeDtypeStruct((s, D_BRANCH), BF16),
                   jax.ShapeDtypeStruct((N_HEADS // 2, s, LANES), F32)],
        scratch_shapes=[pltpu.VMEM((2 * npairs, bq, LANES), F32), pltpu.VMEM((2 * npairs, bq, 1), F32)],
        compiler_params=_cparams(("parallel", "parallel")),
    )(proj, proj, proj, proj)


_GRAD_COL_SHARDED = (True, True, True, False)
_GRAD_FULL_SHAPES = ((D_MODEL, D_IN), (D_BRANCH, D_MODEL), (D_BRANCH, D_MODEL), (D_MODEL, D_MODEL))
_GRAD_PIECE_SHAPES = tuple((r // 2, w // N_CHIPS) if cs else (r // (2 * N_CHIPS), w)
                           for (r, w), cs in zip(_GRAD_FULL_SHAPES, _GRAD_COL_SHARDED))
_EARLY_IN_DEVS = (4, 5, 6, 7)
_LATE_IN_DEVS = (0, 1, 2, 3)
_LATE_CHIPS = (0, 1)
_EARLY_CHIPS = (2, 3)
_ALL_CHIPS = (0, 1, 2, 3)


def _grad_piece(ref, a, dev):
    r, w = _GRAD_PIECE_SHAPES[a]
    if _GRAD_COL_SHARDED[a]:
        return ref.at[pl.ds((dev % 2) * r, r), pl.ds((dev // 2) * w, w)]
    return ref.at[pl.ds(dev * r, r), :]


def _dev_id(dev):
    return (dev // 4, (dev // 2) % 2, dev % 2)


def _me():
    return 4 * lax.axis_index("x") + 2 * lax.axis_index("y") + lax.axis_index("c")


def _presum_copy(src, dst, send_sem, recv_sem, to_dev):
    return pltpu.make_async_remote_copy(src_ref=src, dst_ref=dst, send_sem=send_sem, recv_sem=recv_sem,
                                        device_id=_dev_id(to_dev), device_id_type=MESH)


def _presum_hand_off(dev, a, dest_chips, g_ref, slots, pair, send_sems, recv_sems):
    chip, core = dev // 2, dev % 2
    cps = []
    for k, q in enumerate(dest_chips):
        piece = _grad_piece(g_ref, a, 2 * q + 1 - core)
        if q == chip:
            cps.append(_presum_copy(piece, slots.at[dev], send_sems.at[N_DEV + k], recv_sems.at[dev], dev ^ 1))
        else:
            cps.append(_presum_copy(piece, pair.at[k], send_sems.at[N_DEV + k], recv_sems.at[N_DEV + k], dev ^ 1))
    return cps


def _presum_sends(dev, a, dest_chips, slots, sums, send_sems, recv_sems):
    chip, core = dev // 2, dev % 2
    return [_presum_copy(sums.at[k], slots.at[dev], send_sems.at[2 * q + core], recv_sems.at[dev], 2 * q + core)
            for k, q in enumerate(dest_chips) if q != chip]


def _presum_loads(dev, a, dest_chips, g_ref, stage, load_sems):
    chip, core = dev // 2, dev % 2
    return [pltpu.make_async_copy(_grad_piece(g_ref, a, 2 * q + core), stage.at[k], load_sems.at[k])
            for k, q in enumerate(dest_chips) if q != chip]


def _presum_send(dev, a, dest_chips, g_ref, slots, pair, stage, sums, send_sems, recv_sems, load_sems):
    chip, core = dev // 2, dev % 2
    hand = _presum_hand_off(dev, a, dest_chips, g_ref, slots, pair, send_sems, recv_sems)
    for cp in _presum_loads(dev, a, dest_chips, g_ref, stage, load_sems):
        cp.wait()
    for k, q in enumerate(dest_chips):
        if q != chip:
            hand[k].wait_recv()
            sums[k] = (stage[k].astype(F32) + pair[k].astype(F32)).astype(BF16)
    for cp in _presum_sends(dev, a, dest_chips, slots, sums, send_sems, recv_sems):
        cp.start()


def _presum_wait(dev, a, dest_chips, g_ref, slots, pair, sums, send_sems, recv_sems):
    chip, core = dev // 2, dev % 2
    for cp in _presum_hand_off(dev, a, dest_chips, g_ref, slots, pair, send_sems, recv_sems):
        cp.wait_send()
    for cp in _presum_sends(dev, a, dest_chips, slots, sums, send_sems, recv_sems):
        cp.wait_send()
    if chip in dest_chips:
        for src_dev in _presum_sources(dev):
            _presum_copy(sums.at[0], slots.at[src_dev], send_sems.at[src_dev], recv_sems.at[src_dev], src_dev).wait_recv()


def _presum_sources(dev):
    return [dev ^ 1] + [2 * r + dev % 2 for r in range(N_CHIPS) if r != dev // 2]


def _presum_scratch(a, dest_chips):
    n = len(dest_chips)
    piece = _GRAD_PIECE_SHAPES[a]
    return [pltpu.VMEM((n,) + piece, BF16), pltpu.VMEM((n,) + piece, BF16), pltpu.VMEM((n,) + piece, BF16),
            pltpu.SemaphoreType.DMA((N_DEV + n,)), pltpu.SemaphoreType.DMA((N_DEV + n,)),
            pltpu.SemaphoreType.DMA((n,))]


PRESUM_SCRATCH = 6


def _presum_program(first, second, last, a, dest_chips, g_ref, slots, scratch):
    pair, stage, sums, send_sems, recv_sems, load_sems = scratch
    me = _me()

    @pl.when(first)
    def _():
        for dev in range(N_DEV):
            @pl.when(me == dev)
            def _():
                for cp in _presum_hand_off(dev, a, dest_chips, g_ref, slots, pair, send_sems, recv_sems):
                    cp.start()
                for cp in _presum_loads(dev, a, dest_chips, g_ref, stage, load_sems):
                    cp.start()

    @pl.when(second)
    def _():
        for dev in range(N_DEV):
            @pl.when(me == dev)
            def _():
                _presum_send(dev, a, dest_chips, g_ref, slots, pair, stage, sums, send_sems, recv_sems, load_sems)

    def finish():
        for dev in range(N_DEV):
            @pl.when(me == dev)
            def _():
                _presum_wait(dev, a, dest_chips, g_ref, slots, pair, sums, send_sems, recv_sems)

    return finish


def _attn_bwd(proj, do, rsave, bq, bk, npairs, grads):
    plan = ((0, _EARLY_CHIPS), (1, _ALL_CHIPS), (2, _ALL_CHIPS), (3, _ALL_CHIPS))
    s = proj.shape[0]
    nq = s // bq
    ratio = bq // bk
    scale = HEAD_DIM ** -0.5
    heads = tuple(range(2 * npairs))
    width = LANES * npairs

    n_steps = N_HEADS // (2 * npairs)
    n_g = len(grads)

    def body(q_ref, k_ref, v_ref, do_ref, rs_ref, *refs):
        g_src = refs[:n_g]
        dq_ref, dk_ref, dv_ref = refs[n_g:n_g + 3]
        g_slots = refs[n_g + 3:2 * n_g + 3]
        dk_acc, dv_acc, dq_acc, e_ref = refs[2 * n_g + 3:2 * n_g + 7]
        i = pl.program_id(1)
        step = pl.program_id(0) * nq + i
        finish = [_presum_program(step == 0, step == 1, step == n_steps * nq - 1, a, chips, g_src[pos], g_slots[pos],
                                  refs[2 * n_g + 7 + PRESUM_SCRATCH * pos:2 * n_g + 7 + PRESUM_SCRATCH * (pos + 1)])
                  for pos, (a, chips) in enumerate(plan)]

        lane = lax.broadcasted_iota(jnp.int32, (bq, LANES), 1)
        lo_half = lane < HEAD_DIM
        qm, dom = [], []
        for pr in range(npairs):
            cols = slice(LANES * pr, LANES * (pr + 1))
            q = q_ref[:, cols] * jnp.asarray(scale, BF16)
            zero = jnp.zeros_like(q)
            qm += [jnp.where(lo_half, q, zero), jnp.where(lo_half, zero, q)]
            dout = do_ref[:, cols].astype(F32)
            dom += [jnp.where(lo_half, dout, 0.0), jnp.where(lo_half, 0.0, dout)]
        row = lax.broadcasted_iota(jnp.int32, (bq, bk), 0)
        col = lax.broadcasted_iota(jnp.int32, (bq, bk), 1)
        tneg = _tri(bk, False, -1.0)
        tfwd = _tri(bk, True, 1.0)

        @pl.when(i == 0)
        def _():
            dk_acc[...] = jnp.zeros_like(dk_acc)
            dv_acc[...] = jnp.zeros_like(dv_acc)

        dq_acc[...] = jnp.zeros_like(dq_acc)
        e_ref[...] = jnp.zeros_like(e_ref)

        def block(j, diag, valid=None):
            ks = pl.multiple_of(j * bk, bk)
            kj = [k_ref[pl.ds(ks, bk), LANES * pr:LANES * (pr + 1)] for pr in range(npairs)]
            vj = [v_ref[pl.ds(ks, bk), LANES * pr:LANES * (pr + 1)] for pr in range(npairs)]
            if diag:
                before = (j * bk + col) < (i * bq + row)
            z = [_dot_nt(qm[h], kj[h // 2]) for h in heads]
            er = [jnp.exp(jnp.sum(jnp.where(lane == j + HEAD_DIM * (h % 2), rs_ref[h // 2], 0.0), axis=-1,
                                  keepdims=True)) for h in heads]
            if valid is not None:
                er = [er[h] * valid.astype(F32) for h in heads]
            dos = [(dom[h] * er[h]).astype(BF16) for h in heads]
            dw = [_dot_nt(dos[h], vj[h // 2]) for h in heads]
            psp = [_neg_softplus_parts(z[h]) for h in heads]
            sp = [psp[h][1] for h in heads]
            if diag:
                sp = [jnp.where(before, sp[h], 0.0) for h in heads]
            cin = [jnp.dot(_split_cat(sp[h], SCAN_PASSES), tneg, preferred_element_type=F32) for h in heads]
            w = [jnp.exp(z[h] + cin[h]) for h in heads]
            if diag:
                w = [jnp.where(before, w[h], 0.0) for h in heads]
            e =[dw[h] * w[h] for h in heads]
            eincl = [jnp.dot(_split_cat(e[h], SCAN_PASSES), tfwd, preferred_element_type=F32) + e_ref[h]
                     for h in heads]
            dz = []
            for h in heads:
                p = psp[h][0]
                beta = jnp.where(z[h] >= 0.0, 1.0, p) / (1.0 + p)
                d = e[h] - beta * eincl[h]
                dz.append((jnp.where(before, d, 0.0) if diag else d).astype(BF16))
            wb = [w[h].astype(BF16) for h in heads]
            for h in heads:
                e_ref[h] = eincl[h][:, bk - 1:bk]
                dq_acc[h] += jnp.dot(dz[h], kj[h // 2], preferred_element_type=F32)
            for pr in range(npairs):
                cols = slice(LANES * pr, LANES * (pr + 1))
                h0, h1 = 2 * pr, 2 * pr + 1
                dk_acc[pl.ds(ks, bk), cols] += _dot_tn(dz[h0], qm[h0]) + _dot_tn(dz[h1], qm[h1])
                dv_acc[pl.ds(ks, bk), cols] += _dot_tn(wb[h0], dos[h0]) + _dot_tn(wb[h1], dos[h1])

        def loop_body(j, carry):
            block(j, False)
            return carry

        block_of_lane = lane & (HEAD_DIM - 1)
        live = jnp.max(rs_ref[...], axis=0) > DEAD
        first_live = jnp.min(jnp.where(live, block_of_lane, nq * ratio))
        last = jnp.maximum(i * ratio - 1, 0)
        lax.fori_loop(jnp.minimum(first_live, last), last, loop_body, 0)
        block(last, False, valid=i > 0)
        for t in range(ratio):
            block(i * ratio + t, True)
        for pr in range(npairs):
            dq = jnp.where(lo_half, dq_acc[2 * pr], dq_acc[2 * pr + 1]) * scale
            dq_ref[:, LANES * pr:LANES * (pr + 1)] = dq.astype(BF16)

        @pl.when(i == nq - 1)
        def _():
            dk_ref[...] = dk_acc[...].astype(BF16)
            dv_ref[...] = dv_acc[...].astype(BF16)

        @pl.when(step == n_steps * nq - 1)
        def _():
            for fin in finish:
                fin()

    any_spec = pl.BlockSpec(memory_space=pl.ANY)
    return pl.pallas_call(
        body, name="attn_bwd", grid=(n_steps, nq),
        in_specs=[pl.BlockSpec((bq, width), lambda p, i: (i, n_steps * COL_Q + p)),
                  pl.BlockSpec((s, width), lambda p, i: (0, n_steps * COL_K + p)),
                  pl.BlockSpec((s, width), lambda p, i: (0, n_steps * COL_V + p)),
                  pl.BlockSpec((bq, width), lambda p, i: (i, p)),
                  pl.BlockSpec((npairs, bq, LANES), lambda p, i: (p, i, 0))] + [any_spec] * n_g,
        out_specs=[pl.BlockSpec((bq, width), lambda p, i: (i, p)),
                   pl.BlockSpec((s, width), lambda p, i: (0, p)),
                   pl.BlockSpec((s, width), lambda p, i: (0, p))] + [any_spec] * n_g,
        out_shape=[jax.ShapeDtypeStruct((s, D_BRANCH), BF16)] * 3
        + [jax.ShapeDtypeStruct((N_DEV,) + _GRAD_PIECE_SHAPES[a], BF16) for a, _ in plan],
        scratch_shapes=[pltpu.VMEM((s, width), F32), pltpu.VMEM((s, width), F32),
                        pltpu.VMEM((2 * npairs, bq, LANES), F32), pltpu.VMEM((2 * npairs, bq, 1), F32)]
        + [sh for a, chips in plan for sh in _presum_scratch(a, chips)],
        compiler_params=pltpu.CompilerParams(dimension_semantics=("arbitrary", "arbitrary"),
                                             vmem_limit_bytes=VMEM_LIMIT, has_side_effects=True),
    )(proj, proj, proj, do, rsave, *grads)


def _group_avg_matrix():
    a = lax.broadcasted_iota(jnp.int32, (LANES, LANES), 0) >> GROUP_SHIFT
    b = lax.broadcasted_iota(jnp.int32, (LANES, LANES), 1) >> GROUP_SHIFT
    return jnp.where(a == b, 1.0 / GROUP_DIM, 0.0).astype(BF16)


def _group_mean(a, avg):
    parts = [_split_dot(a[:, LANES * k:LANES * (k + 1)], avg, 2) for k in range(D_BRANCH // LANES)]
    return jnp.concatenate(parts, axis=1)


def _sgu_forward_parts(ub, vb, ln_g, ln_b, avg):
    ug, dug = _gelu_and_grad(ub)
    vg, dvg = _gelu_and_grad(vb)
    mu = _group_mean(vg, avg)
    d = vg - mu
    var = _group_mean(d * d, avg)
    rstd = lax.rsqrt(var + EPS)
    vhat = d * rstd
    vn = vhat * ln_g + ln_b
    return ug, dug, dvg, rstd, vhat, vn


def _sgu_mix(w_ref, src_bf16, n_chunks):
    lane = lax.broadcasted_iota(jnp.int32, (SGU_CHUNK, LANES), 1)
    lo_half = lane < GROUP_DIM
    rows = []
    for n in range(n_chunks):
        slabs = []
        for a in range(D_BRANCH // LANES):
            blk = src_bf16[SGU_CHUNK * n:SGU_CHUNK * (n + 1), LANES * a:LANES * (a + 1)]
            zero = jnp.zeros_like(blk)
            m0 = jnp.dot(w_ref[2 * a], jnp.where(lo_half, blk, zero), preferred_element_type=F32)
            m1 = jnp.dot(w_ref[2 * a + 1], jnp.where(lo_half, zero, blk), preferred_element_type=F32)
            slabs.append(m0 + m1)
        rows.append(jnp.concatenate(slabs, axis=1))
    return jnp.concatenate(rows, axis=0)


def _sgu_fwd(proj, ln_g, ln_b, w_mask, bias_full):
    s = proj.shape[0]
    tm = min(512, s)
    n_chunks = tm // SGU_CHUNK

    def body(ub_ref, vb_ref, zb_ref, g_ref, b_ref, w_ref, bias_ref, yb_ref):
        avg = _group_avg_matrix()
        ug, _, _, _, _, vn = _sgu_forward_parts(ub_ref[...].astype(F32), vb_ref[...].astype(F32),
                                                g_ref[...], b_ref[...], avg)
        mixed = _sgu_mix(w_ref, vn.astype(BF16), n_chunks) + jnp.concatenate([bias_ref[...]] * n_chunks, axis=0)
        zb = zb_ref[...].astype(F32)
        yb_ref[...] = (ug * mixed * (zb * _sigmoid(zb))).astype(BF16)

    col = lambda c: pl.BlockSpec((tm, D_BRANCH), lambda i: (i, c))
    full = lambda shape: pl.BlockSpec(shape, lambda i: (0,) * len(shape))
    return pl.pallas_call(
        body, name="sgu_fwd", grid=(s // tm,),
        in_specs=[col(COL_UB), col(COL_VB), col(COL_ZB), full((1, D_BRANCH)), full((1, D_BRANCH)),
                  full((N_GROUPS, SGU_CHUNK, SGU_CHUNK)), full((SGU_CHUNK, D_BRANCH))],
        out_specs=pl.BlockSpec((tm, D_BRANCH), lambda i: (i, 0)),
        out_shape=jax.ShapeDtypeStruct((s, D_BRANCH), BF16),
        compiler_params=_cparams(("parallel",)),
    )(proj, proj, proj, ln_g, ln_b, w_mask, bias_full)


def _sgu_bwd(proj, dyb, ln_g, ln_b, w_mask, w_mask_t, bias_full):
    s = proj.shape[0]
    tm = min(512, s)
    n_chunks = tm // SGU_CHUNK
    n_steps = s // tm

    def body(ub_ref, vb_ref, zb_ref, dyb_ref, g_ref, b_ref, w_ref, wt_ref, bias_ref,
             dsgu_ref, dw_ref, db_ref, dg_ref, dbeta_ref, dmix_acc):
        i = pl.program_id(0)

        @pl.when(i == 0)
        def _():
            dw_ref[...] = jnp.zeros_like(dw_ref)
            dg_ref[...] = jnp.zeros_like(dg_ref)
            dbeta_ref[...] = jnp.zeros_like(dbeta_ref)
            dmix_acc[...] = jnp.zeros_like(dmix_acc)

        avg = _group_avg_matrix()
        ln_gv = g_ref[...]
        ug, dug, dvg, rstd, vhat, vn = _sgu_forward_parts(ub_ref[...].astype(F32), vb_ref[...].astype(F32),
                                                          ln_gv, b_ref[...], avg)
        vnb = vn.astype(BF16)
        mixed = _sgu_mix(w_ref, vnb, n_chunks) + jnp.concatenate([bias_ref[...]] * n_chunks, axis=0)
        zb = zb_ref[...].astype(F32)
        sg = _sigmoid(zb)
        sz = zb * sg
        dsz = sg * (1.0 + zb * (1.0 - sg))
        dy = dyb_ref[...].astype(F32)
        dmixed = dy * ug * sz
        du = dy * mixed * sz * dug
        dzb = dy * ug * mixed * dsz
        dmb = dmixed.astype(BF16)
        dvn = _sgu_mix(wt_ref, dmb, n_chunks)

        lane = lax.broadcasted_iota(jnp.int32, (SGU_CHUNK, LANES), 1)
        lo_half = lane < GROUP_DIM
        dm_sum = None
        for n in range(n_chunks):
            rows = slice(SGU_CHUNK * n, SGU_CHUNK * (n + 1))
            dm_sum = dmixed[rows] if dm_sum is None else dm_sum + dmixed[rows]
            for a in range(D_BRANCH // LANES):
                cols = slice(LANES * a, LANES * (a + 1))
                dblk = dmb[rows, cols]
                vblk = vnb[rows, cols]
                zero = jnp.zeros_like(dblk)
                dw_ref[2 * a] += _dot_nt(jnp.where(lo_half, dblk, zero), vblk)
                dw_ref[2 * a + 1] += _dot_nt(jnp.where(lo_half, zero, dblk), vblk)
        dmix_acc[...] += dm_sum

        dg_ref[...] += jnp.sum(dvn * vhat, axis=0, keepdims=True)
        dbeta_ref[...] += jnp.sum(dvn, axis=0, keepdims=True)
        dvh = dvn * ln_gv
        m1 = _group_mean(dvh, avg)
        m2 = _group_mean(dvh * vhat, avg)
        dv = rstd * (dvh - m1 - vhat * m2) * dvg
        dsgu_ref[:, 0:D_BRANCH] = du.astype(BF16)
        dsgu_ref[:, D_BRANCH:2 * D_BRANCH] = dv.astype(BF16)
        dsgu_ref[:, 2 * D_BRANCH:3 * D_BRANCH] = dzb.astype(BF16)

        @pl.when(i == n_steps - 1)
        def _():
            pos = lax.broadcasted_iota(jnp.int32, (SGU_CHUNK, SGU_CHUNK), 0) >> GROUP_SHIFT
            src = lax.broadcasted_iota(jnp.int32, (SGU_CHUNK, SGU_CHUNK), 1) >> GROUP_SHIFT
            keep = src <= pos
            for g in range(N_GROUPS):
                dw_ref[g] = jnp.where(keep, dw_ref[g], 0.0)
            grp = lax.broadcasted_iota(jnp.int32, (D_BRANCH, LANES), 0) >> GROUP_SHIFT
            sel = (grp == lax.broadcasted_iota(jnp.int32, (D_BRANCH, LANES), 1)).astype(BF16)
            db_ref[...] = _split_dot(dmix_acc[...], sel, 3)

    col = lambda c: pl.BlockSpec((tm, D_BRANCH), lambda i: (i, c))
    full = lambda shape: pl.BlockSpec(shape, lambda i: (0,) * len(shape))
    return pl.pallas_call(
        body, name="sgu_bwd", grid=(n_steps,),
        in_specs=[col(COL_UB), col(COL_VB), col(COL_ZB), pl.BlockSpec((tm, D_BRANCH), lambda i: (i, 0)),
                  full((1, D_BRANCH)), full((1, D_BRANCH)),
                  full((N_GROUPS, SGU_CHUNK, SGU_CHUNK)), full((N_GROUPS, SGU_CHUNK, SGU_CHUNK)),
                  full((SGU_CHUNK, D_BRANCH))],
        out_specs=[pl.BlockSpec((tm, 3 * D_BRANCH), lambda i: (i, 0)),
                   full((N_GROUPS, SGU_CHUNK, SGU_CHUNK)), full((SGU_CHUNK, LANES)),
                   full((1, D_BRANCH)), full((1, D_BRANCH))],
        out_shape=[jax.ShapeDtypeStruct((s, 3 * D_BRANCH), BF16),
                   jax.ShapeDtypeStruct((N_GROUPS, SGU_CHUNK, SGU_CHUNK), F32),
                   jax.ShapeDtypeStruct((SGU_CHUNK, LANES), F32),
                   jax.ShapeDtypeStruct((1, D_BRANCH), F32), jax.ShapeDtypeStruct((1, D_BRANCH), F32)],
        scratch_shapes=[pltpu.VMEM((SGU_CHUNK, D_BRANCH), F32)],
        compiler_params=_cparams(("arbitrary",)),
    )(proj, proj, proj, dyb, ln_g, ln_b, w_mask, w_mask_t, bias_full)


def _mid(proj, ya, yb, o, x, target, final_g, w_up_a, w_up_b, w_out):
    s = x.shape[0]
    tm = min(256, s)
    n_steps = s // tm
    half = D_MODEL // 2

    def body(ya_ref, yb_ref, o_ref, za_ref, ga0_ref, ga1_ref, gb0_ref, gb1_ref, x_ref, t_ref, gf_ref,
             wa_ref, wb_ref, wo_ref,
             dzg_ref, do_ref, dyb_ref, dx2_ref, gwo_ref, gwa_ref, gwb_ref, loss_ref, dgf_ref,
             acc_o, acc_a, acc_b):
        i = pl.program_id(0)

        @pl.when(i == 0)
        def _():
            acc_o[...] = jnp.zeros_like(acc_o)
            acc_a[...] = jnp.zeros_like(acc_a)
            acc_b[...] = jnp.zeros_like(acc_b)
            loss_ref[...] = jnp.zeros_like(loss_ref)
            dgf_ref[...] = jnp.zeros_like(dgf_ref)

        ya_v = ya_ref[...]
        yb_v = yb_ref[...]
        pa = jnp.dot(ya_v, wa_ref[...], preferred_element_type=F32)
        pb = jnp.dot(yb_v, wb_ref[...], preferred_element_type=F32)
        sa = _sigmoid(jnp.concatenate([ga0_ref[...], ga1_ref[...]], axis=1).astype(F32))
        sb = _sigmoid(jnp.concatenate([gb0_ref[...], gb1_ref[...]], axis=1).astype(F32))
        merged = (sa * pa + sb * pb).astype(BF16)
        x2 = x_ref[...] + jnp.dot(merged, wo_ref[...], preferred_element_type=F32)
        r2 = lax.rsqrt(jnp.mean(x2 * x2, axis=-1, keepdims=True) + EPS)
        xh = x2 * r2
        gf = gf_ref[...]
        diff = xh * gf - t_ref[...]
        loss_ref[...] += 0.5 * jnp.sum(jnp.mean(diff * diff, axis=-1, keepdims=True))
        dy = diff * (1.0 / D_MODEL)
        dgf_ref[...] += jnp.sum(dy * xh, axis=0, keepdims=True)
        dyg = dy * gf
        dx2 = r2 * (dyg - xh * jnp.mean(dyg * xh, axis=-1, keepdims=True))
        dx2_ref[...] = dx2
        dx2b = dx2.astype(BF16)
        dmerged = _dot_nt(dx2b, wo_ref[...])
        acc_o[...] += _dot_tn(merged, dx2b)
        dpa = dmerged * sa
        dpb = dmerged * sb
        dzg_ref[:, D_BRANCH:D_BRANCH + D_MODEL] = (dpa * pa * (1.0 - sa)).astype(BF16)
        dzg_ref[:, D_BRANCH + D_MODEL:D_BRANCH + 2 * D_MODEL] = (dpb * pb * (1.0 - sb)).astype(BF16)
        dpab = dpa.astype(BF16)
        dpbb = dpb.astype(BF16)
        acc_a[...] += _dot_tn(ya_v, dpab)
        acc_b[...] += _dot_tn(yb_v, dpbb)
        dya = _dot_nt(dpab, wa_ref[...])
        dyb_ref[...] = _dot_nt(dpbb, wb_ref[...]).astype(BF16)
        za = za_ref[...].astype(F32)
        sg = _sigmoid(za)
        do_ref[...] = (dya * (za * sg)).astype(BF16)
        dzg_ref[:, 0:D_BRANCH] = (dya * o_ref[...].astype(F32) * (sg * (1.0 + za * (1.0 - sg)))).astype(BF16)

        @pl.when(i == n_steps - 1)
        def _():
            gwo_ref[...] = acc_o[...].astype(BF16)
            gwa_ref[...] = acc_a[...].astype(BF16)
            gwb_ref[...] = acc_b[...].astype(BF16)

    tok = lambda w: pl.BlockSpec((tm, w), lambda i: (i, 0))
    col = lambda c: pl.BlockSpec((tm, half), lambda i: (i, c))
    full = lambda shape: pl.BlockSpec(shape, lambda i: (0,) * len(shape))
    return pl.pallas_call(
        body, name="mid", grid=(n_steps,),
        in_specs=[tok(D_BRANCH), tok(D_BRANCH), tok(D_BRANCH), col(COL_ZA), col(COL_GA), col(COL_GA + 1),
                  col(COL_GB), col(COL_GB + 1), tok(D_MODEL), tok(D_MODEL), full((1, D_MODEL)),
                  full((D_BRANCH, D_MODEL)), full((D_BRANCH, D_MODEL)), full((D_MODEL, D_MODEL))],
        out_specs=[tok(D_BRANCH + 2 * D_MODEL), tok(D_BRANCH), tok(D_BRANCH), tok(D_MODEL),
                   full((D_MODEL, D_MODEL)), full((D_BRANCH, D_MODEL)), full((D_BRANCH, D_MODEL)),
                   full((8, LANES)), full((1, D_MODEL))],
        out_shape=[jax.ShapeDtypeStruct((s, D_BRANCH + 2 * D_MODEL), BF16),
                   jax.ShapeDtypeStruct((s, D_BRANCH), BF16), jax.ShapeDtypeStruct((s, D_BRANCH), BF16),
                   jax.ShapeDtypeStruct((s, D_MODEL), F32),
                   jax.ShapeDtypeStruct((D_MODEL, D_MODEL), BF16),
                   jax.ShapeDtypeStruct((D_BRANCH, D_MODEL), BF16), jax.ShapeDtypeStruct((D_BRANCH, D_MODEL), BF16),
                   jax.ShapeDtypeStruct((8, LANES), F32), jax.ShapeDtypeStruct((1, D_MODEL), F32)],
        scratch_shapes=[pltpu.VMEM((D_MODEL, D_MODEL), F32), pltpu.VMEM((D_BRANCH, D_MODEL), F32),
                        pltpu.VMEM((D_BRANCH, D_MODEL), F32)],
        compiler_params=_cparams(("arbitrary",)),
    )(ya, yb, o, proj, proj, proj, proj, proj, x, target, final_g, w_up_a, w_up_b, w_out)


def _dwin_piece(ht, piece, tile_of, prev):
    s = ht.shape[1]
    n_tiles = piece.shape[1] // D_BRANCH

    def body(ht_ref, p_ref, *rest):
        out_ref = rest[-1]
        out_ref[...] = jnp.dot(ht_ref[...], p_ref[...], preferred_element_type=F32).astype(BF16)

    in_specs = [pl.BlockSpec((D_MODEL, s), lambda j: (0, 0)), pl.BlockSpec((s, D_BRANCH), lambda j: (0, j))]
    args = [ht, piece]
    aliases = {}
    if prev is not None:
        in_specs.append(pl.BlockSpec(memory_space=pl.ANY))
        args.append(prev)
        aliases = {2: 0}
    return pl.pallas_call(
        body, name="dwin_piece", grid=(n_tiles,),
        in_specs=in_specs,
        out_specs=pl.BlockSpec((D_MODEL, D_BRANCH), lambda j: (0, tile_of(j))),
        out_shape=jax.ShapeDtypeStruct((D_MODEL, D_IN), BF16),
        input_output_aliases=aliases,
        compiler_params=_cparams(("parallel",)),
    )(*args)


def _dh_dx(pieces, w_in, x, norm_g, dx2, g_in):
    s = x.shape[0]
    tm = min(256, s)
    n_steps = s // tm
    arrays = []
    for arr, _, _, _ in pieces:
        if not any(arr is a for a in arrays):
            arrays.append(arr)
    n_arr = len(arrays)
    plan = [([k for k, a in enumerate(arrays) if a is arr][0], wcol, off, width) for arr, wcol, off, width in pieces]

    def body(*refs):
        p_refs = refs[:n_arr]
        w_ref, x_ref, g_ref, dx2_ref, gin_ref, dx_ref, dg_ref, late_ref = refs[n_arr:n_arr + 8]
        step = pl.program_id(0)
        finish = _presum_program(step == 0, step == min(1, n_steps - 1), step == n_steps - 1, 0, _LATE_CHIPS,
                                 gin_ref, late_ref, refs[n_arr + 8:])

        @pl.when(step == 0)
        def _():
            dg_ref[...] = jnp.zeros_like(dg_ref)

        dh = None
        for k, wcol, off, width in plan:
            d = _dot_nt(p_refs[k][:, off:off + width], w_ref[:, wcol:wcol + width])
            dh = d if dh is None else dh + d
        xf = x_ref[...]
        r = lax.rsqrt(jnp.mean(xf * xf, axis=-1, keepdims=True) + EPS)
        xh = xf * r
        dg_ref[...] += jnp.sum(dh * xh, axis=0, keepdims=True)
        dhg = dh * g_ref[...]
        dx_ref[...] = r * (dhg - xh * jnp.mean(dhg * xh, axis=-1, keepdims=True)) + dx2_ref[...]

        @pl.when(step == n_steps - 1)
        def _():
            finish()

    tok = lambda w: pl.BlockSpec((tm, w), lambda i: (i, 0))
    full = lambda shape: pl.BlockSpec(shape, lambda i: (0,) * len(shape))
    any_spec = pl.BlockSpec(memory_space=pl.ANY)
    return pl.pallas_call(
        body, name="dh_dx", grid=(n_steps,),
        in_specs=[tok(a.shape[1]) for a in arrays] + [full((D_MODEL, D_IN)), tok(D_MODEL), full((1, D_MODEL)),
                                                      tok(D_MODEL), any_spec],
        out_specs=[tok(D_MODEL), full((1, D_MODEL)), any_spec],
        out_shape=[jax.ShapeDtypeStruct((s, D_MODEL), F32), jax.ShapeDtypeStruct((1, D_MODEL), F32),
                   jax.ShapeDtypeStruct((N_DEV,) + _GRAD_PIECE_SHAPES[0], BF16)],
        scratch_shapes=_presum_scratch(0, _LATE_CHIPS),
        compiler_params=pltpu.CompilerParams(dimension_semantics=("arbitrary",), vmem_limit_bytes=VMEM_LIMIT,
                                             has_side_effects=True),
    )(*arrays, w_in, x, norm_g, dx2, g_in)


def _adamw(w, g, m, v):
    rows, cols = w.shape
    tr = max(t for t in range(8, 257, 8) if rows % t == 0)
    c1 =1.0 - ADAM_B1 ** ADAM_STEP
    c2 = 1.0 - ADAM_B2 ** ADAM_STEP

    def body(w_ref, g_ref, m_ref, v_ref, g_out_ref, d_ref, nm_ref, nv_ref):
        gv = g_ref[...]
        g_out_ref[...] = gv
        nm = ADAM_B1 * m_ref[...] + (1.0 - ADAM_B1) * gv
        nv = ADAM_B2 * v_ref[...] + (1.0 - ADAM_B2) * (gv * gv)
        d_ref[...] = -ADAM_LR * ((nm / c1) / (jnp.sqrt(nv / c2) + ADAM_EPS) + ADAM_WD * w_ref[...])
        nm_ref[...] = nm
        nv_ref[...] = nv

    spec = pl.BlockSpec((tr, cols), lambda i: (i, 0))
    return pl.pallas_call(
        body, name="adamw", grid=(rows // tr,),
        in_specs=[spec] * 4, out_specs=[spec] * 4,
        out_shape=[jax.ShapeDtypeStruct((rows, cols), F32)] * 4,
        compiler_params=_cparams(("parallel",)),
    )(w, g, m, v)


def _reduce_grads_tail(grads, g_small, early_slots, late_in_slots):
    n_big = len(grads)
    n_arr = n_big + 1
    shard_shapes = [(2 * r, w) for r, w in _GRAD_PIECE_SHAPES]
    small_piece = (SMALL_PIECE, LANES)

    def body(*refs):
        src = refs[:n_arr]
        early = refs[n_arr:n_arr + n_big]
        late_in = refs[n_arr + n_big]
        n_in = n_arr + n_big + 1
        out = refs[n_in:n_in + n_arr]
        slots = refs[n_in + n_arr:n_in + 2 * n_arr]
        sums = refs[n_in + 2 * n_arr:n_in + 3 * n_arr]
        send1, recv1, send2, recv2, local_sems = refs[n_in + 3 * n_arr:]
        x, y, c = _place()
        me = 4 * x + 2 * y + c

        def piece_of(a, dev):
            return src[a].at[dev] if a == n_big else _grad_piece(src[a], a, dev)

        def late(a, dst_dev, src_dev):
            return pltpu.make_async_remote_copy(
                src_ref=piece_of(a, dst_dev), dst_ref=slots[a].at[src_dev],
                send_sem=send1.at[n_arr * dst_dev + a], recv_sem=recv1.at[n_arr * src_dev + a],
                device_id=_dev_id(dst_dev), device_id_type=MESH)

        def late_arrays(dev):
            return (n_big,)

        def load(a, dev, received):
            return pltpu.make_async_copy(received.at[dev], slots[a].at[dev], local_sems.at[n_arr * dev + a])

        def own(a, dev):
            return pltpu.make_async_copy(piece_of(a, dev), slots[a].at[dev], local_sems.at[n_arr * dev + a])

        for dev in range(N_DEV):
            @pl.when(me == dev)
            def _():
                received = [early[0] if dev in _EARLY_IN_DEVS else late_in] + list(early[1:])
                for a in range(n_arr):
                    own(a, dev).start()
                sources = sorted([dev] + _presum_sources(dev))
                for peer in range(N_DEV):
                    if peer != dev:
                        for a in late_arrays(peer):
                            late(a, peer, dev).start()
                        for a in range(n_big):
                            if peer in sources:
                                load(a, peer, received[a]).start()
                for a in range(n_arr):
                    own(a, dev).wait()
                for peer in range(N_DEV):
                    if peer != dev:
                        for a in late_arrays(dev):
                            late(a, dev, peer).wait_recv()
                        for a in range(n_big):
                            if peer in sources:
                                load(a, peer, received[a]).wait()
                for a in range(n_arr):
                    rows = slots[a].shape[1]
                    step = 64 if rows % 64 == 0 else 8
                    used = sources if a < n_big else list(range(N_DEV))

                    def add_rows(t, carry, a=a, step=step, used=used):
                        r0 = pl.multiple_of(t * step, step)
                        total = slots[a][used[0], pl.ds(r0, step), :].astype(F32)
                        for src_dev in used[1:]:
                            total = total + slots[a][src_dev, pl.ds(r0, step), :].astype(F32)
                        sums[a][pl.ds(r0, step), :] = total
                        return carry

                    lax.fori_loop(0, rows // step, add_rows, 0)

        shares = []
        keeps = []
        for a in range(n_big):
            r, w = _GRAD_PIECE_SHAPES[a]
            dst = out[a].at[pl.ds(pl.multiple_of(c * r, 8), r), :]
            cp = pltpu.make_async_remote_copy(src_ref=sums[a], dst_ref=dst, send_sem=send2.at[a], recv_sem=recv2.at[a],
                                              device_id=(x, y, 1 - c), device_id_type=MESH)
            cp.start()
            shares.append(cp)
            kp = pltpu.make_async_copy(sums[a], dst, local_sems.at[N_DEV * n_arr + a])
            kp.start()
            keeps.append(kp)
        kp = pltpu.make_async_copy(sums[n_big], out[n_big].at[me], local_sems.at[N_DEV * n_arr + n_big])
        kp.start()
        keeps.append(kp)

        def small_share(dst_dev, src_dev):
            return pltpu.make_async_remote_copy(src_ref=sums[n_big], dst_ref=out[n_big].at[src_dev],
                                                send_sem=send2.at[n_big + dst_dev], recv_sem=recv2.at[n_big + src_dev],
                                                device_id=_dev_id(dst_dev), device_id_type=MESH)

        for dev in range(N_DEV):
            @pl.when(me != dev)
            def _():
                small_share(dev, me).start()
        for a in range(n_big):
            r, w = _GRAD_PIECE_SHAPES[a]
            other = out[a].at[pl.ds(pl.multiple_of((1 - c) * r, 8), r), :]
            pltpu.make_async_remote_copy(src_ref=sums[a], dst_ref=other, send_sem=send2.at[a], recv_sem=recv2.at[a],
                                         device_id=(x, y, 1 - c), device_id_type=MESH).wait_recv()
        for dev in range(N_DEV):
            @pl.when(me != dev)
            def _():
                small_share(dev, dev).wait_recv()
                small_share(dev, me).wait_send()
                for a in late_arrays(dev):
                    late(a, dev, me).wait_send()
        for cp in shares:
            cp.wait_send()
        for kp in keeps:
            kp.wait()

    any_spec = pl.BlockSpec(memory_space=pl.ANY)
    return pl.pallas_call(
        body, name="reduce_grads_tail",
        in_specs=[any_spec] * (n_arr + n_big + 1), out_specs=[any_spec] * n_arr,
        out_shape=[jax.ShapeDtypeStruct(sh, F32) for sh in shard_shapes]
        + [jax.ShapeDtypeStruct((N_DEV,) + small_piece, F32)],
        scratch_shapes=[pltpu.VMEM((N_DEV,) + sh, BF16) for sh in _GRAD_PIECE_SHAPES]
        + [pltpu.VMEM((N_DEV,) + small_piece, F32)]
        + [pltpu.VMEM(sh, F32) for sh in _GRAD_PIECE_SHAPES] + [pltpu.VMEM(small_piece, F32)]
        + [pltpu.SemaphoreType.DMA((N_DEV * n_arr,)), pltpu.SemaphoreType.DMA((N_DEV * n_arr,)),
           pltpu.SemaphoreType.DMA((n_big + N_DEV,)), pltpu.SemaphoreType.DMA((n_big + N_DEV,)),
           pltpu.SemaphoreType.DMA((N_DEV * n_arr + n_arr,))],
        compiler_params=pltpu.CompilerParams(vmem_limit_bytes=VMEM_LIMIT, has_side_effects=True),
    )(*grads, g_small, *early_slots, late_in_slots)


_SMALL_PARTS = (("norm_g", 8), ("sgu_ln_g", 8), ("sgu_ln_b", 8), ("w_spatial", 1024), ("b_spatial", 8),
                ("final_norm_g", 8))
_LOSS_ROW = sum(n for _, n in _SMALL_PARTS)


def _pack_small(parts, loss_tile=None):
    rows = []
    for name, n_rows in _SMALL_PARTS:
        a = parts[name].reshape(-1, LANES).astype(F32)
        a = jnp.pad(a, ((0, n_rows - a.shape[0]), (0, 0)))
        rows.append(a)
    rows.append(jnp.zeros((8, LANES), F32) if loss_tile is None else loss_tile)
    rows.append(jnp.zeros((SMALL_ROWS - _LOSS_ROW - 8, LANES), F32))
    return jnp.concatenate(rows, axis=0)


def _unpack_small(packed, shapes):
    out = {}
    r0 = 0
    for name, n_rows in _SMALL_PARTS:
        n = math.prod(shapes[name])
        out[name] = packed[r0:r0 + n // LANES].reshape(shapes[name])
        r0 += n_rows
    return out


def _local_step(proj, ht, x, target, norm_g, w_in, sgu_ln_g, sgu_ln_b, w_spatial, b_spatial, w_up_a, w_up_b, w_out,
                final_norm_g, bq, bk):
    pos = jnp.arange(SGU_CHUNK)
    keep = (pos[None, :] // SGU_SUBCHUNK) <= (pos[:, None] // SGU_SUBCHUNK)
    w_mask = jnp.where(keep[None], w_spatial, 0.0).astype(BF16)
    w_mask_t = jnp.swapaxes(w_mask, 1, 2)
    bias_full = jnp.repeat(b_spatial.T, GROUP_DIM, axis=1)
    ln_g = sgu_ln_g.reshape(1, D_BRANCH)
    ln_b = sgu_ln_b.reshape(1, D_BRANCH)
    final_g = final_norm_g.reshape(1, D_MODEL)

    o, ya, rsave = _attn_fwd(proj, bq, bk, ATTN_PAIRS)
    yb = _sgu_fwd(proj, ln_g, ln_b, w_mask, bias_full)
    dzg, do, dyb, dx2, g_out, g_up_a, g_up_b, loss_acc, d_final = _mid(
        proj, ya, yb, o, x, target, final_g, w_up_a, w_up_b, w_out)
    dsgu, d_wsp, d_bsp, d_lng, d_lnb = _sgu_bwd(proj, dyb, ln_g, ln_b, w_mask, w_mask_t, bias_full)
    g_in = _dwin_piece(ht, dzg, lambda j: jnp.where(j == 0, COL_ZA, COL_GA - 1 + j), None)
    g_in = _dwin_piece(ht, dsgu, lambda j: COL_UB + j, g_in)
    dq, dk, dv, *early_slots = _attn_bwd(proj, do, rsave, bq, bk, ATTN_PAIRS, (g_in, g_up_a, g_up_b, g_out))
    g_in = _dwin_piece(ht, dq, lambda j: COL_Q + j, g_in)
    g_in = _dwin_piece(ht, dk, lambda j: COL_K + j, g_in)
    g_in = _dwin_piece(ht, dv, lambda j: COL_V + j, g_in)
    pieces = [(dq, COL_Q * D_BRANCH, 0, D_BRANCH), (dk, COL_K * D_BRANCH, 0, D_BRANCH),
              (dv, COL_V * D_BRANCH, 0, D_BRANCH), (dzg, COL_ZA * D_BRANCH, 0, D_BRANCH),
              (dsgu, COL_UB * D_BRANCH, 0, 3 * D_BRANCH), (dzg, COL_GA * D_BRANCH, D_BRANCH, 2 * D_MODEL)]
    dx, d_norm, late_in_slots = _dh_dx(pieces, w_in, x, norm_g, dx2, g_in)
    small = {"norm_g": d_norm, "sgu_ln_g": d_lng, "sgu_ln_b": d_lnb, "w_spatial": d_wsp,
             "b_spatial": d_bsp[:, :N_GROUPS].T, "final_norm_g": d_final}
    return loss_acc, dx, (g_in, g_up_a, g_up_b, g_out), small, early_slots, late_in_slots


def kernel(x, norm_g, w_in, sgu_ln_g, sgu_ln_b, w_spatial, b_spatial, w_up_a, w_up_b, w_out, final_norm_g, loss_target, m_norm_g, m_w_in, m_sgu_ln_g, m_sgu_ln_b, m_w_spatial, m_b_spatial, m_w_up_a, m_w_up_b, m_w_out, m_final_norm_g, v_norm_g, v_w_in, v_sgu_ln_g, v_sgu_ln_b, v_w_spatial, v_b_spatial, v_w_up_a, v_w_up_b, v_w_out, v_final_norm_g):
    big_names = ("w_in", "w_up_a", "w_up_b", "w_out")
    small_names = tuple(n for n, _ in _SMALL_PARTS)
    names = ("norm_g", "w_in", "sgu_ln_g", "sgu_ln_b", "w_spatial", "b_spatial", "w_up_a", "w_up_b", "w_out",
             "final_norm_g")
    w = dict(norm_g=norm_g, w_in=w_in, sgu_ln_g=sgu_ln_g, sgu_ln_b=sgu_ln_b, w_spatial=w_spatial,
             b_spatial=b_spatial, w_up_a=w_up_a, w_up_b=w_up_b, w_out=w_out, final_norm_g=final_norm_g)
    m = dict(norm_g=m_norm_g, w_in=m_w_in, sgu_ln_g=m_sgu_ln_g, sgu_ln_b=m_sgu_ln_b, w_spatial=m_w_spatial,
             b_spatial=m_b_spatial, w_up_a=m_w_up_a, w_up_b=m_w_up_b, w_out=m_w_out, final_norm_g=m_final_norm_g)
    v = dict(norm_g=v_norm_g, w_in=v_w_in, sgu_ln_g=v_sgu_ln_g, sgu_ln_b=v_sgu_ln_b, w_spatial=v_w_spatial,
             b_spatial=v_b_spatial, w_up_a=v_w_up_a, w_up_b=v_w_up_b, w_out=v_w_out, final_norm_g=v_final_norm_g)
    shapes = {n: w[n].shape for n in names}
    flat2d = lambda a: a.reshape(a.shape[-2:])

    proj, ht, *full = _in_proj_gather(x[0], norm_g, *[flat2d(w[n]) for n in big_names])
    loss, dx, big_grads, small, early_slots, late_in_slots = _local_step(
        proj, ht, x[0], loss_target[0], norm_g, full[0], sgu_ln_g[0], sgu_ln_b[0], w_spatial[0], b_spatial[0],
        full[1], full[2], full[3], final_norm_g, ATTN_Q_BLOCK, ATTN_K_BLOCK)
    packed = _pack_small(small, loss).reshape(N_DEV, SMALL_PIECE, LANES)
    red = _reduce_grads_tail(big_grads, packed, early_slots, late_in_slots)

    grads, deltas, new_m, new_v = {}, {}, {}, {}
    for n, g in zip(big_names, red[:4]):
        g, d, nm, nv = _adamw(flat2d(w[n]), g, flat2d(m[n]), flat2d(v[n]))
        grads[n], deltas[n], new_m[n], new_v[n] = (a.reshape(shapes[n]) for a in (g, d, nm, nv))
    g_small = red[4].reshape(SMALL_ROWS, LANES)
    g_small, d, nm, nv = _adamw(_pack_small({n: w[n] for n in small_names}), g_small,
                                _pack_small({n: m[n] for n in small_names}),
                                _pack_small({n: v[n] for n in small_names}))
    for src, dst in ((g_small, grads), (d, deltas), (nm, new_m), (nv, new_v)):
        dst.update(_unpack_small(src, shapes))

    return (g_small[_LOSS_ROW, 0], dx[None], *[grads[n] for n in names], *[deltas[n] for n in names],
            *[new_m[n] for n in names], *[new_v[n] for n in names])
```

```python
import math

import jax
import jax.numpy as jnp
from jax import lax
from jax.experimental import pallas as pl
from jax.experimental.pallas import tpu as pltpu

F32 = jnp.float32
BF16 = jnp.bfloat16

D_MODEL = 1024
N_HEADS = 8
HEAD_DIM = 64
D_BRANCH = 512
D_IN = 4 * D_BRANCH + 3 * D_BRANCH + 2 * D_MODEL
N_GROUPS = 8
GROUP_DIM = 64
SGU_CHUNK = 128
SGU_SUBCHUNK = 64
GROUP_SHIFT = 6
EPS = 1e-6
LANES = 128
ATTN_Q_BLOCK = 256
ATTN_K_BLOCK = 256
DEAD = -110.0
SKIPPED = -1e30
SCAN_PASSES = 1
ATTN_PAIRS = 2
N_CHIPS = 4
N_DEV = 8
MESH = pl.DeviceIdType.MESH

ADAM_LR = 0.001
ADAM_B1 = 0.9
ADAM_B2 = 0.999
ADAM_EPS = 1e-08
ADAM_WD = 0.01
ADAM_STEP = 10

COL_Q, COL_K, COL_V, COL_ZA, COL_UB, COL_VB, COL_ZB, COL_GA, COL_GB = 0, 1, 2, 3, 4, 5, 6, 7, 9

VMEM_LIMIT = 56 * 1024 * 1024

SMALL_ROWS = 1088
SMALL_PIECE = SMALL_ROWS // N_DEV


def _cparams(sem=None):
    return pltpu.CompilerParams(dimension_semantics=sem, vmem_limit_bytes=VMEM_LIMIT)


def _aligned(v, m):
    return v if isinstance(v, int) else pl.multiple_of(v, m)


def _sigmoid(x):
    return 1.0 / (1.0 + jnp.exp(-x))


def _gelu_and_grad(x):
    k = math.sqrt(2.0 / math.pi)
    x2 = x * x
    inner = k * (x + 0.044715 * x * x2)
    th = jnp.tanh(inner)
    g = 0.5 * x * (1.0 + th)
    dg = 0.5 * (1.0 + th) + 0.5 * x * (1.0 - th * th) * (k * (1.0 + 3.0 * 0.044715 * x2))
    return g, dg


def _split_dot(a, b_bf16, passes):
    out = None
    rem = a
    for _ in range(passes):
        part = rem.astype(BF16)
        d = jnp.dot(part, b_bf16, preferred_element_type=F32)
        out = d if out is None else out + d
        rem = rem - part.astype(F32)
    return out


def _dot_nt(a, b):
    return lax.dot_general(a, b, (((1,), (1,)), ((), ())), preferred_element_type=F32)


def _dot_tn(a, b):
    return lax.dot_general(a, b, (((0,), (0,)), ((), ())), preferred_element_type=F32)


def _place():
    x, y, c = lax.axis_index("x"), lax.axis_index("y"), lax.axis_index("c")
    return x, y, c


def _in_proj_gather(x, norm_g, w_in, w_up_a, w_up_b, w_out):
    s = x.shape[0]
    tm = min(1024, s)
    nt = s // tm
    shards = (w_in, w_up_a, w_up_b, w_out)
    n_arr = len(shards)
    col_sharded = (True, True, True, False)
    full_shapes = ((D_MODEL, D_IN), (D_BRANCH, D_MODEL), (D_BRANCH, D_MODEL), (D_MODEL, D_MODEL))
    w_shard = w_in.shape[1]
    half_rows = D_MODEL // 2
    stage_rows = 256

    def body(order_ref, x_ref, g_ref, *refs):
        src = refs[:n_arr]
        proj_ref, ht_ref = refs[n_arr:n_arr + 2]
        out = refs[n_arr + 2:2 * n_arr + 2]
        wsc, h_scr, stage = refs[2 * n_arr + 2:2 * n_arr + 5]
        small_stage = refs[2 * n_arr + 5:2 * n_arr + 8]
        small_cast = refs[2 * n_arr + 8:2 * n_arr + 11]
        send_sems, recv_sems, local_sems = refs[2 * n_arr + 11:]
        k = pl.program_id(0)
        i = pl.program_id(1)
        x_, y_, c = _place()
        chip = 2 * x_ + y_
        sibling = (x_, y_, 1 - c)
        others = [(x_, 1 - y_), (1 - x_, y_), (1 - x_, 1 - y_)]

        def region(a, chip_idx, half):
            if a == 0:
                return wsc.at[chip_idx, pl.ds(_aligned(half * half_rows, 16), half_rows), :]
            r, w = shards[a].shape
            hr = r // 2
            if col_sharded[a]:
                return out[a].at[pl.ds(_aligned(half * hr, 16), hr), pl.ds(_aligned(chip_idx * w, LANES), w)]
            return out[a].at[pl.ds(_aligned(chip_idx * r + half * hr, 16), hr), :]

        def remote(kk, a, chip_idx, half, to, own):
            s_ref = region(a, chip_idx, half)
            if own and a > 0:
                hr = shards[a].shape[0] // 2
                s_ref = small_cast[a - 1].at[pl.ds(_aligned(half * hr, 16), hr), :]
            return pltpu.make_async_remote_copy(src_ref=s_ref, dst_ref=region(a, chip_idx, half),
                                                send_sem=send_sems.at[kk], recv_sem=recv_sems.at[kk],
                                                device_id=to, device_id_type=MESH)

        def keep_whole(kk, chip_idx):
            return pltpu.make_async_copy(wsc.at[chip_idx],
                                         out[0].at[:, pl.ds(_aligned(chip_idx * w_shard, LANES), w_shard)],
                                         local_sems.at[kk])

        def small_stores():
            cps = []
            for a in range(1, n_arr):
                hr = shards[a].shape[0] // 2
                for half in range(2):
                    cps.append(pltpu.make_async_copy(small_cast[a - 1].at[pl.ds(half * hr, hr), :],
                                                     region(a, chip, half), local_sems.at[4 + 2 * (a - 1) + half]))
            return cps

        def arrive_and_pass(j):
            ochip = chip ^ j
            for a in range(n_arr):
                kk = n_arr * (j - 1) + a
                remote(kk, a, ochip, c, sibling, False).wait_recv()
                remote(3 * n_arr + kk, a, ochip, c, sibling, False).start()

        def from_sibling(j, a):
            remote(3 * n_arr + n_arr * (j - 1) + a, a, chip ^ j, 1 - c, sibling, False).wait_recv()

        @pl.when((k == 0) & (i == 0))
        def _():
            def cast_rows(half):
                for t in range(half_rows // stage_rows):
                    r0 = pl.multiple_of(half * half_rows + t * stage_rows, stage_rows)
                    pltpu.sync_copy(src[0].at[pl.ds(r0, stage_rows), :], stage)
                    wsc[chip, pl.ds(r0, stage_rows), :] = stage[...].astype(BF16)

            cast_rows(c)
            for j in (1, 2):
                remote(n_arr * (j - 1), 0, chip, c, (*others[j - 1], c), True).start()
            cast_rows(1 - c)
            for a in range(1, n_arr):
                pltpu.sync_copy(src[a], small_stage[a - 1])
                small_cast[a - 1][...] = small_stage[a - 1][...].astype(BF16)
            for j in (1, 2):
                for a in range(1, n_arr):
                    remote(n_arr * (j - 1) + a, a, chip, c, (*others[j - 1], c), True).start()
            keep_whole(0, chip).start()
            for cp in small_stores():
                cp.start()

        @pl.when((k == 1) & (i == 0))
        def _():
            for j in (1, 2):
                remote(n_arr * (j - 1), 0, chip, c, (*others[j - 1], c), True).wait_send()
            for a in range(n_arr):
                remote(n_arr * 2 + a, a, chip, c, (*others[2], c), True).start()
            arrive_and_pass(1)
            arrive_and_pass(2)
            from_sibling(1, 0)
            keep_whole(1, chip ^ 1).start()

        @pl.when((k == 2) & (i == 0))
        def _():
            from_sibling(2, 0)
            keep_whole(2, chip ^ 2).start()
            arrive_and_pass(3)

        @pl.when((k == 3) & (i == 0))
        def _():
            from_sibling(3, 0)
            keep_whole(3, chip ^ 3).start()

        @pl.when(k == 0)
        def _():
            xf = x_ref[...]
            r = lax.rsqrt(jnp.mean(xf * xf, axis=-1, keepdims=True) + EPS)
            h = xf * r * g_ref[...]
            h_scr[i] = h.astype(BF16)
            ht_ref[...] = h.T.astype(BF16)

        proj_ref[...] = jnp.dot(h_scr[i], wsc[order_ref[k]], preferred_element_type=F32).astype(BF16)

        @pl.when((k == 3) & (i == nt - 1))
        def _():
            for j in (1, 2, 3):
                for a in range(1, n_arr):
                    from_sibling(j, a)
            for j in (1, 2, 3):
                for a in range(n_arr):
                    kk = n_arr * (j - 1) + a
                    if a > 0 or j == 3:
                        remote(kk, a, chip, c, (*others[j - 1], c), True).wait_send()
                    remote(3 * n_arr + kk, a, chip ^ j, c, sibling, False).wait_send()
            for kk in range(4):
                keep_whole(kk, chip ^ kk).wait()
            for cp in small_stores():
                cp.wait()

    any_spec = pl.BlockSpec(memory_space=pl.ANY)
    tile = lambda kk, ii: jnp.where(kk == 0, ii, nt - 1)
    grid_spec = pltpu.PrefetchScalarGridSpec(
        num_scalar_prefetch=1, grid=(N_CHIPS, nt),
        in_specs=[pl.BlockSpec((tm, D_MODEL), lambda kk, ii, order: (tile(kk, ii), 0)),
                  pl.BlockSpec((1, D_MODEL), lambda kk, ii, order: (0, 0))] + [any_spec] * n_arr,
        out_specs=[pl.BlockSpec((tm, w_shard), lambda kk, ii, order: (ii, order[kk])),
                   pl.BlockSpec((D_MODEL, tm), lambda kk, ii, order: (0, tile(kk, ii)))] + [any_spec] * n_arr,
        scratch_shapes=[pltpu.VMEM((N_CHIPS, D_MODEL, w_shard), BF16), pltpu.VMEM((nt, tm, D_MODEL), BF16),
                        pltpu.VMEM((stage_rows, w_shard), F32)]
        + [pltpu.VMEM(a.shape, F32) for a in shards[1:]] + [pltpu.VMEM(a.shape, BF16) for a in shards[1:]]
        + [pltpu.SemaphoreType.DMA((6 * n_arr,)), pltpu.SemaphoreType.DMA((6 * n_arr,)),
           pltpu.SemaphoreType.DMA((4 + 2 * (n_arr - 1),))])
    x_, y_, _ = _place()
    order = (2 * x_ + y_) ^ jnp.arange(N_CHIPS, dtype=jnp.int32)
    return pl.pallas_call(
        body, name="in_proj_gather", grid_spec=grid_spec,
        out_shape=[jax.ShapeDtypeStruct((s, D_IN), BF16), jax.ShapeDtypeStruct((D_MODEL, s), BF16)]
        + [jax.ShapeDtypeStruct(sh, BF16) for sh in full_shapes],
        compiler_params=pltpu.CompilerParams(dimension_semantics=("arbitrary", "arbitrary"),
                                             vmem_limit_bytes=VMEM_LIMIT, has_side_effects=True),
    )(order, x, norm_g, *shards)


def _neg_softplus_parts(z):
    zb = z.astype(BF16)
    p = jnp.exp(-jnp.abs(zb))
    return p, jnp.maximum(zb, jnp.zeros_like(zb)) + jnp.log(1.0 + p)


def _split_cat(a, passes):
    parts = []
    rem = a
    for k in range(passes):
        part = rem.astype(BF16)
        parts.append(part)
        if k + 1 < passes:
            rem = rem - part.astype(F32)
    return parts[0] if passes == 1 else jnp.concatenate(parts, axis=1)


def _tri(blk, upper, sign):
    row = lax.broadcasted_iota(jnp.int32, (blk, blk), 0)
    col = lax.broadcasted_iota(jnp.int32, (blk, blk), 1)
    keep = (row <= col) if upper else (row >= col)
    t = jnp.where(keep, sign, 0.0).astype(BF16)
    return t if SCAN_PASSES == 1 else jnp.concatenate([t] * SCAN_PASSES, axis=0)


def _attn_fwd(proj, bq, bk, npairs):
    s = proj.shape[0]
    nq = s // bq
    ratio = bq // bk
    scale = HEAD_DIM ** -0.5
    heads = tuple(range(2 * npairs))
    width = LANES * npairs

    def body(q_ref, k_ref, v_ref, za_ref, o_ref, ya_ref, rs_ref, acc_ref, r_ref):
        i = pl.program_id(1)
        lane = lax.broadcasted_iota(jnp.int32, (bq, LANES), 1)
        lo_half = lane < HEAD_DIM
        qm = []
        for pr in range(npairs):
            q = q_ref[:, LANES * pr:LANES * (pr + 1)] * jnp.asarray(scale, BF16)
            zero = jnp.zeros_like(q)
            qm += [jnp.where(lo_half, q, zero), jnp.where(lo_half, zero, q)]
        row = lax.broadcasted_iota(jnp.int32, (bq, bk), 0)
        col = lax.broadcasted_iota(jnp.int32, (bq, bk), 1)
        tneg = _tri(bk, False, -1.0)
        acc_ref[...] = jnp.zeros_like(acc_ref)
        r_ref[...] = jnp.zeros_like(r_ref)
        rs_ref[...] = jnp.full_like(rs_ref, SKIPPED)

        def scores(j):
            ks = pl.multiple_of(j * bk, bk)
            return [_dot_nt(qm[h], k_ref[pl.ds(ks, bk), LANES * (h // 2):LANES * (h // 2 + 1)]) for h in heads]

        def block(j, diag, valid=None):
            ks = pl.multiple_of(j * bk, bk)
            vj = [v_ref[pl.ds(ks, bk), LANES * pr:LANES * (pr + 1)] for pr in range(npairs)]
            if diag:
                before = (j * bk + col) < (i * bq + row)
            z = scores(j)
            sp = [_neg_softplus_parts(z[h])[1] for h in heads]
            if diag:
                sp = [jnp.where(before, sp[h], 0.0) for h in heads]
            cin = [jnp.dot(_split_cat(sp[h], SCAN_PASSES), tneg, preferred_element_type=F32) for h in heads]
            w = [jnp.exp(z[h] + cin[h]) for h in heads]
            if diag:
                w = [jnp.where(before, w[h], 0.0) for h in heads]
            pv = [jnp.dot(w[h].astype(BF16), vj[h // 2], preferred_element_type=F32) for h in heads]
            r = [r_ref[h] for h in heads]
            keep = 1.0 if valid is None else valid.astype(F32)
            for h in heads:
                acc_ref[h] += pv[h] * (jnp.exp(r[h]) * keep)
                r_ref[h] = r[h] + cin[h][:, 0:1] * keep
            for pr in range(npairs):
                hit = [lane == j, lane == j + HEAD_DIM]
                if valid is not None:
                    hit = [m & valid for m in hit]
                rs_ref[pr] = jnp.where(hit[0], r[2 * pr], jnp.where(hit[1], r[2 * pr + 1], rs_ref[pr]))

        for t in range(ratio):
            block(i * ratio + ratio - 1 - t, True)
        block(jnp.maximum(i * ratio - 1, 0), False, valid=i > 0)

        def alive(carry):
            jj, r_max = carry
            return (jj < i * ratio - 1) & (r_max > DEAD)

        def loop_body(carry):
            jj, _ = carry
            block(i * ratio - 2 - jj, False)
            return jj + 1, jnp.max(r_ref[...])

        lax.while_loop(alive, loop_body, (0, jnp.max(r_ref[...])))
        for pr in range(npairs):
            cols = slice(LANES * pr, LANES * (pr + 1))
            o = jnp.where(lo_half, acc_ref[2 * pr], acc_ref[2 * pr + 1])
            o_ref[:, cols] = o.astype(BF16)
            za = za_ref[:, cols].astype(F32)
            ya_ref[:, cols] = (o * (za * _sigmoid(za))).astype(BF16)

    n_steps = N_HEADS // (2 * npairs)
    return pl.pallas_call(
        body, name="attn_fwd", grid=(n_steps, nq),
        in_specs=[pl.BlockSpec((bq, width), lambda p, i: (i, n_steps * COL_Q + p)),
                  pl.BlockSpec((s, width), lambda p, i: (0, n_steps * COL_K + p)),
                  pl.BlockSpec((s, width), lambda p, i: (0, n_steps * COL_V + p)),
                  pl.BlockSpec((bq, width), lambda p, i: (i, n_steps * COL_ZA + p))],
        out_specs=[pl.BlockSpec((bq, width), lambda p, i: (i, p)),
                   pl.BlockSpec((bq, width), lambda p, i: (i, p)),
                   pl.BlockSpec((npairs, bq, LANES), lambda p, i: (p, i, 0))],
        out_shape=[jax.ShapeDtypeStruct((s, D_BRANCH), BF16), jax.ShapeDtypeStruct((s, D_BRANCH), BF16),
                   jax.ShapeDtypeStruct((N_HEADS // 2, s, LANES), F32)],
        scratch_shapes=[pltpu.VMEM((2 * npairs, bq, LANES), F32), pltpu.VMEM((2 * npairs, bq, 1), F32)],
        compiler_params=_cparams(("parallel", "parallel")),
    )(proj, proj, proj, proj)


_GRAD_COL_SHARDED = (True, True, True, False)
_GRAD_FULL_SHAPES = ((D_MODEL, D_IN), (D_BRANCH, D_MODEL), (D_BRANCH, D_MODEL), (D_MODEL, D_MODEL))
_GRAD_PIECE_SHAPES = tuple((r // 2, w // N_CHIPS) if cs else (r // (2 * N_CHIPS), w)
                           for (r, w), cs in zip(_GRAD_FULL_SHAPES, _GRAD_COL_SHARDED))
_EARLY_IN_DEVS = (4, 5, 6, 7)
_LATE_IN_DEVS = (0, 1, 2, 3)
_LATE_CHIPS = (0, 1)
_EARLY_CHIPS = (2, 3)
_ALL_CHIPS = (0, 1, 2, 3)


def _grad_piece(ref, a, dev):
    r, w = _GRAD_PIECE_SHAPES[a]
    if _GRAD_COL_SHARDED[a]:
        return ref.at[pl.ds((dev % 2) * r, r), pl.ds((dev // 2) * w, w)]
    return ref.at[pl.ds(dev * r, r), :]


def _dev_id(dev):
    return (dev // 4, (dev // 2) % 2, dev % 2)


def _me():
    return 4 * lax.axis_index("x") + 2 * lax.axis_index("y") + lax.axis_index("c")


def _presum_copy(src, dst, send_sem, recv_sem, to_dev):
    return pltpu.make_async_remote_copy(src_ref=src, dst_ref=dst, send_sem=send_sem, recv_sem=recv_sem,
                                        device_id=_dev_id(to_dev), device_id_type=MESH)


def _presum_hand_off(dev, a, dest_chips, g_ref, slots, pair, send_sems, recv_sems):
    chip, core = dev // 2, dev % 2
    cps = []
    for k, q in enumerate(dest_chips):
        piece = _grad_piece(g_ref, a, 2 * q + 1 - core)
        if q == chip:
            cps.append(_presum_copy(piece, slots.at[dev], send_sems.at[N_DEV + k], recv_sems.at[dev], dev ^ 1))
        else:
            cps.append(_presum_copy(piece, pair.at[k], send_sems.at[N_DEV + k], recv_sems.at[N_DEV + k], dev ^ 1))
    return cps


def _presum_sends(dev, a, dest_chips, slots, sums, send_sems, recv_sems):
    chip, core = dev // 2, dev % 2
    return [_presum_copy(sums.at[k], slots.at[dev], send_sems.at[2 * q + core], recv_sems.at[dev], 2 * q + core)
            for k, q in enumerate(dest_chips) if q != chip]


def _presum_loads(dev, a, dest_chips, g_ref, stage, load_sems):
    chip, core = dev // 2, dev % 2
    return [pltpu.make_async_copy(_grad_piece(g_ref, a, 2 * q + core), stage.at[k], load_sems.at[k])
            for k, q in enumerate(dest_chips) if q != chip]


def _presum_send(dev, a, dest_chips, g_ref, slots, pair, stage, sums, send_sems, recv_sems, load_sems):
    chip, core = dev // 2, dev % 2
    hand = _presum_hand_off(dev, a, dest_chips, g_ref, slots, pair, send_sems, recv_sems)
    for cp in _presum_loads(dev, a, dest_chips, g_ref, stage, load_sems):
        cp.wait()
    for k, q in enumerate(dest_chips):
        if q != chip:
            hand[k].wait_recv()
            sums[k] = (stage[k].astype(F32) + pair[k].astype(F32)).astype(BF16)
    for cp in _presum_sends(dev, a, dest_chips, slots, sums, send_sems, recv_sems):
        cp.start()


def _presum_wait(dev, a, dest_chips, g_ref, slots, pair, sums, send_sems, recv_sems):
    chip, core = dev // 2, dev % 2
    for cp in _presum_hand_off(dev, a, dest_chips, g_ref, slots, pair, send_sems, recv_sems):
        cp.wait_send()
    for cp in _presum_sends(dev, a, dest_chips, slots, sums, send_sems, recv_sems):
        cp.wait_send()
    if chip in dest_chips:
        for src_dev in _presum_sources(dev):
            _presum_copy(sums.at[0], slots.at[src_dev], send_sems.at[src_dev], recv_sems.at[src_dev], src_dev).wait_recv()


def _presum_sources(dev):
    return [dev ^ 1] + [2 * r + dev % 2 for r in range(N_CHIPS) if r != dev // 2]


def _presum_scratch(a, dest_chips):
    n = len(dest_chips)
    piece = _GRAD_PIECE_SHAPES[a]
    return [pltpu.VMEM((n,) + piece, BF16), pltpu.VMEM((n,) + piece, BF16), pltpu.VMEM((n,) + piece, BF16),
            pltpu.SemaphoreType.DMA((N_DEV + n,)), pltpu.SemaphoreType.DMA((N_DEV + n,)),
            pltpu.SemaphoreType.DMA((n,))]


PRESUM_SCRATCH = 6


def _presum_program(first, second, last, a, dest_chips, g_ref, slots, scratch):
    pair, stage, sums, send_sems, recv_sems, load_sems = scratch
    me = _me()

    @pl.when(first)
    def _():
        for dev in range(N_DEV):
            @pl.when(me == dev)
            def _():
                for cp in _presum_hand_off(dev, a, dest_chips, g_ref, slots, pair, send_sems, recv_sems):
                    cp.start()
                for cp in _presum_loads(dev, a, dest_chips, g_ref, stage, load_sems):
                    cp.start()

    @pl.when(second)
    def _():
        for dev in range(N_DEV):
            @pl.when(me == dev)
            def _():
                _presum_send(dev, a, dest_chips, g_ref, slots, pair, stage, sums, send_sems, recv_sems, load_sems)

    def finish():
        for dev in range(N_DEV):
            @pl.when(me == dev)
            def _():
                _presum_wait(dev, a, dest_chips, g_ref, slots, pair, sums, send_sems, recv_sems)

    return finish


def _attn_bwd(proj, do, rsave, bq, bk, npairs, grads):
    plan = ((0, _EARLY_CHIPS), (1, _ALL_CHIPS), (2, _ALL_CHIPS), (3, _ALL_CHIPS))
    s = proj.shape[0]
    nq = s // bq
    ratio = bq // bk
    scale = HEAD_DIM ** -0.5
    heads = tuple(range(2 * npairs))
    width = LANES * npairs

    n_steps = N_HEADS // (2 * npairs)
    n_g = len(grads)

    def body(q_ref, k_ref, v_ref, do_ref, rs_ref, *refs):
        g_src = refs[:n_g]
        dq_ref, dk_ref, dv_ref = refs[n_g:n_g + 3]
        g_slots = refs[n_g + 3:2 * n_g + 3]
        dk_acc, dv_acc, dq_acc, e_ref = refs[2 * n_g + 3:2 * n_g + 7]
        i = pl.program_id(1)
        step = pl.program_id(0) * nq + i
        finish = [_presum_program(step == 0, step == 1, step == n_steps * nq - 1, a, chips, g_src[pos], g_slots[pos],
                                  refs[2 * n_g + 7 + PRESUM_SCRATCH * pos:2 * n_g + 7 + PRESUM_SCRATCH * (pos + 1)])
                  for pos, (a, chips) in enumerate(plan)]

        lane = lax.broadcasted_iota(jnp.int32, (bq, LANES), 1)
        lo_half = lane < HEAD_DIM
        qm, dom = [], []
        for pr in range(npairs):
            cols = slice(LANES * pr, LANES * (pr + 1))
            q = q_ref[:, cols] * jnp.asarray(scale, BF16)
            zero = jnp.zeros_like(q)
            qm += [jnp.where(lo_half, q, zero), jnp.where(lo_half, zero, q)]
            dout = do_ref[:, cols].astype(F32)
            dom += [jnp.where(lo_half, dout, 0.0), jnp.where(lo_half, 0.0, dout)]
        row = lax.broadcasted_iota(jnp.int32, (bq, bk), 0)
        col = lax.broadcasted_iota(jnp.int32, (bq, bk), 1)
        tneg = _tri(bk, False, -1.0)
        tfwd = _tri(bk, True, 1.0)

        @pl.when(i == 0)
        def _():
            dk_acc[...] = jnp.zeros_like(dk_acc)
            dv_acc[...] = jnp.zeros_like(dv_acc)

        dq_acc[...] = jnp.zeros_like(dq_acc)
        e_ref[...] = jnp.zeros_like(e_ref)

        def block(j, diag, valid=None):
            ks = pl.multiple_of(j * bk, bk)
            kj = [k_ref[pl.ds(ks, bk), LANES * pr:LANES * (pr + 1)] for pr in range(npairs)]
            vj = [v_ref[pl.ds(ks, bk), LANES * pr:LANES * (pr + 1)] for pr in range(npairs)]
            if diag:
                before = (j * bk + col) < (i * bq + row)
            z = [_dot_nt(qm[h], kj[h // 2]) for h in heads]
            er = [jnp.exp(jnp.sum(jnp.where(lane == j + HEAD_DIM * (h % 2), rs_ref[h // 2], 0.0), axis=-1,
                                  keepdims=True)) for h in heads]
            if valid is not None:
                er = [er[h] * valid.astype(F32) for h in heads]
            dos = [(dom[h] * er[h]).astype(BF16) for h in heads]
            dw = [_dot_nt(dos[h], vj[h // 2]) for h in heads]
            psp = [_neg_softplus_parts(z[h]) for h in heads]
            sp = [psp[h][1] for h in heads]
            if diag:
                sp = [jnp.where(before, sp[h], 0.0) for h in heads]
            cin = [jnp.dot(_split_cat(sp[h], SCAN_PASSES), tneg, preferred_element_type=F32) for h in heads]
            w = [jnp.exp(z[h] + cin[h]) for h in heads]
            if diag:
                w = [jnp.where(before, w[h], 0.0) for h in heads]
            e =[dw[h] * w[h] for h in heads]
            eincl = [jnp.dot(_split_cat(e[h], SCAN_PASSES), tfwd, preferred_element_type=F32) + e_ref[h]
                     for h in heads]
            dz = []
            for h in heads:
                p = psp[h][0]
                beta = jnp.where(z[h] >= 0.0, 1.0, p) / (1.0 + p)
                d = e[h] - beta * eincl[h]
                dz.append((jnp.where(before, d, 0.0) if diag else d).astype(BF16))
            wb = [w[h].astype(BF16) for h in heads]
            for h in heads:
                e_ref[h] = eincl[h][:, bk - 1:bk]
                dq_acc[h] += jnp.dot(dz[h], kj[h // 2], preferred_element_type=F32)
            for pr in range(npairs):
                cols = slice(LANES * pr, LANES * (pr + 1))
                h0, h1 = 2 * pr, 2 * pr + 1
                dk_acc[pl.ds(ks, bk), cols] += _dot_tn(dz[h0], qm[h0]) + _dot_tn(dz[h1], qm[h1])
                dv_acc[pl.ds(ks, bk), cols] += _dot_tn(wb[h0], dos[h0]) + _dot_tn(wb[h1], dos[h1])

        def loop_body(j, carry):
            block(j, False)
            return carry

        block_of_lane = lane & (HEAD_DIM - 1)
        live = jnp.max(rs_ref[...], axis=0) > DEAD
        first_live = jnp.min(jnp.where(live, block_of_lane, nq * ratio))
        last = jnp.maximum(i * ratio - 1, 0)
        lax.fori_loop(jnp.minimum(first_live, last), last, loop_body, 0)
        block(last, False, valid=i > 0)
        for t in range(ratio):
            block(i * ratio + t, True)
        for pr in range(npairs):
            dq = jnp.where(lo_half, dq_acc[2 * pr], dq_acc[2 * pr + 1]) * scale
            dq_ref[:, LANES * pr:LANES * (pr + 1)] = dq.astype(BF16)

        @pl.when(i == nq - 1)
        def _():
            dk_ref[...] = dk_acc[...].astype(BF16)
            dv_ref[...] = dv_acc[...].astype(BF16)

        @pl.when(step == n_steps * nq - 1)
        def _():
            for fin in finish:
                fin()

    any_spec = pl.BlockSpec(memory_space=pl.ANY)
    return pl.pallas_call(
        body, name="attn_bwd", grid=(n_steps, nq),
        in_specs=[pl.BlockSpec((bq, width), lambda p, i: (i, n_steps * COL_Q + p)),
                  pl.BlockSpec((s, width), lambda p, i: (0, n_steps * COL_K + p)),
                  pl.BlockSpec((s, width), lambda p, i: (0, n_steps * COL_V + p)),
                  pl.BlockSpec((bq, width), lambda p, i: (i, p)),
                  pl.BlockSpec((npairs, bq, LANES), lambda p, i: (p, i, 0))] + [any_spec] * n_g,
        out_specs=[pl.BlockSpec((bq, width), lambda p, i: (i, p)),
                   pl.BlockSpec((s, width), lambda p, i: (0, p)),
                   pl.BlockSpec((s, width), lambda p, i: (0, p))] + [any_spec] * n_g,
        out_shape=[jax.ShapeDtypeStruct((s, D_BRANCH), BF16)] * 3
        + [jax.ShapeDtypeStruct((N_DEV,) + _GRAD_PIECE_SHAPES[a], BF16) for a, _ in plan],
        scratch_shapes=[pltpu.VMEM((s, width), F32), pltpu.VMEM((s, width), F32),
                        pltpu.VMEM((2 * npairs, bq, LANES), F32), pltpu.VMEM((2 * npairs, bq, 1), F32)]
        + [sh for a, chips in plan for sh in _presum_scratch(a, chips)],
        compiler_params=pltpu.CompilerParams(dimension_semantics=("arbitrary", "arbitrary"),
                                             vmem_limit_bytes=VMEM_LIMIT, has_side_effects=True),
    )(proj, proj, proj, do, rsave, *grads)


def _group_avg_matrix():
    a = lax.broadcasted_iota(jnp.int32, (LANES, LANES), 0) >> GROUP_SHIFT
    b = lax.broadcasted_iota(jnp.int32, (LANES, LANES), 1) >> GROUP_SHIFT
    return jnp.where(a == b, 1.0 / GROUP_DIM, 0.0).astype(BF16)


def _group_mean(a, avg):
    parts = [_split_dot(a[:, LANES * k:LANES * (k + 1)], avg, 2) for k in range(D_BRANCH // LANES)]
    return jnp.concatenate(parts, axis=1)


def _sgu_forward_parts(ub, vb, ln_g, ln_b, avg):
    ug, dug = _gelu_and_grad(ub)
    vg, dvg = _gelu_and_grad(vb)
    mu = _group_mean(vg, avg)
    d = vg - mu
    var = _group_mean(d * d, avg)
    rstd = lax.rsqrt(var + EPS)
    vhat = d * rstd
    vn = vhat * ln_g + ln_b
    return ug, dug, dvg, rstd, vhat, vn


def _sgu_mix(w_ref, src_bf16, n_chunks):
    lane = lax.broadcasted_iota(jnp.int32, (SGU_CHUNK, LANES), 1)
    lo_half = lane < GROUP_DIM
    rows = []
    for n in range(n_chunks):
        slabs = []
        for a in range(D_BRANCH // LANES):
            blk = src_bf16[SGU_CHUNK * n:SGU_CHUNK * (n + 1), LANES * a:LANES * (a + 1)]
            zero = jnp.zeros_like(blk)
            m0 = jnp.dot(w_ref[2 * a], jnp.where(lo_half, blk, zero), preferred_element_type=F32)
            m1 = jnp.dot(w_ref[2 * a + 1], jnp.where(lo_half, zero, blk), preferred_element_type=F32)
            slabs.append(m0 + m1)
        rows.append(jnp.concatenate(slabs, axis=1))
    return jnp.concatenate(rows, axis=0)


def _sgu_fwd(proj, ln_g, ln_b, w_mask, bias_full):
    s = proj.shape[0]
    tm = min(512, s)
    n_chunks = tm // SGU_CHUNK

    def body(ub_ref, vb_ref, zb_ref, g_ref, b_ref, w_ref, bias_ref, yb_ref):
        avg = _group_avg_matrix()
        ug, _, _, _, _, vn = _sgu_forward_parts(ub_ref[...].astype(F32), vb_ref[...].astype(F32),
                                                g_ref[...], b_ref[...], avg)
        mixed = _sgu_mix(w_ref, vn.astype(BF16), n_chunks) + jnp.concatenate([bias_ref[...]] * n_chunks, axis=0)
        zb = zb_ref[...].astype(F32)
        yb_ref[...] = (ug * mixed * (zb * _sigmoid(zb))).astype(BF16)

    col = lambda c: pl.BlockSpec((tm, D_BRANCH), lambda i: (i, c))
    full = lambda shape: pl.BlockSpec(shape, lambda i: (0,) * len(shape))
    return pl.pallas_call(
        body, name="sgu_fwd", grid=(s // tm,),
        in_specs=[col(COL_UB), col(COL_VB), col(COL_ZB), full((1, D_BRANCH)), full((1, D_BRANCH)),
                  full((N_GROUPS, SGU_CHUNK, SGU_CHUNK)), full((SGU_CHUNK, D_BRANCH))],
        out_specs=pl.BlockSpec((tm, D_BRANCH), lambda i: (i, 0)),
        out_shape=jax.ShapeDtypeStruct((s, D_BRANCH), BF16),
        compiler_params=_cparams(("parallel",)),
    )(proj, proj, proj, ln_g, ln_b, w_mask, bias_full)


def _sgu_bwd(proj, dyb, ln_g, ln_b, w_mask, w_mask_t, bias_full):
    s = proj.shape[0]
    tm = min(512, s)
    n_chunks = tm // SGU_CHUNK
    n_steps = s // tm

    def body(ub_ref, vb_ref, zb_ref, dyb_ref, g_ref, b_ref, w_ref, wt_ref, bias_ref,
             dsgu_ref, dw_ref, db_ref, dg_ref, dbeta_ref, dmix_acc):
        i = pl.program_id(0)

        @pl.when(i == 0)
        def _():
            dw_ref[...] = jnp.zeros_like(dw_ref)
            dg_ref[...] = jnp.zeros_like(dg_ref)
            dbeta_ref[...] = jnp.zeros_like(dbeta_ref)
            dmix_acc[...] = jnp.zeros_like(dmix_acc)

        avg = _group_avg_matrix()
        ln_gv = g_ref[...]
        ug, dug, dvg, rstd, vhat, vn = _sgu_forward_parts(ub_ref[...].astype(F32), vb_ref[...].astype(F32),
                                                          ln_gv, b_ref[...], avg)
        vnb = vn.astype(BF16)
        mixed = _sgu_mix(w_ref, vnb, n_chunks) + jnp.concatenate([bias_ref[...]] * n_chunks, axis=0)
        zb = zb_ref[...].astype(F32)
        sg = _sigmoid(zb)
        sz = zb * sg
        dsz = sg * (1.0 + zb * (1.0 - sg))
        dy = dyb_ref[...].astype(F32)
        dmixed = dy * ug * sz
        du = dy * mixed * sz * dug
        dzb = dy * ug * mixed * dsz
        dmb = dmixed.astype(BF16)
        dvn = _sgu_mix(wt_ref, dmb, n_chunks)

        lane = lax.broadcasted_iota(jnp.int32, (SGU_CHUNK, LANES), 1)
        lo_half = lane < GROUP_DIM
        dm_sum = None
        for n in range(n_chunks):
            rows = slice(SGU_CHUNK * n, SGU_CHUNK * (n + 1))
            dm_sum = dmixed[rows] if dm_sum is None else dm_sum + dmixed[rows]
            for a in range(D_BRANCH // LANES):
                cols = slice(LANES * a, LANES * (a + 1))
                dblk = dmb[rows, cols]
                vblk = vnb[rows, cols]
                zero = jnp.zeros_like(dblk)
                dw_ref[2 * a] += _dot_nt(jnp.where(lo_half, dblk, zero), vblk)
                dw_ref[2 * a + 1] += _dot_nt(jnp.where(lo_half, zero, dblk), vblk)
        dmix_acc[...] += dm_sum

        dg_ref[...] += jnp.sum(dvn * vhat, axis=0, keepdims=True)
        dbeta_ref[...] += jnp.sum(dvn, axis=0, keepdims=True)
        dvh = dvn * ln_gv
        m1 = _group_mean(dvh, avg)
        m2 = _group_mean(dvh * vhat, avg)
        dv = rstd * (dvh - m1 - vhat * m2) * dvg
        dsgu_ref[:, 0:D_BRANCH] = du.astype(BF16)
        dsgu_ref[:, D_BRANCH:2 * D_BRANCH] = dv.astype(BF16)
        dsgu_ref[:, 2 * D_BRANCH:3 * D_BRANCH] = dzb.astype(BF16)

        @pl.when(i == n_steps - 1)
        def _():
            pos = lax.broadcasted_iota(jnp.int32, (SGU_CHUNK, SGU_CHUNK), 0) >> GROUP_SHIFT
            src = lax.broadcasted_iota(jnp.int32, (SGU_CHUNK, SGU_CHUNK), 1) >> GROUP_SHIFT
            keep = src <= pos
            for g in range(N_GROUPS):
                dw_ref[g] = jnp.where(keep, dw_ref[g], 0.0)
            grp = lax.broadcasted_iota(jnp.int32, (D_BRANCH, LANES), 0) >> GROUP_SHIFT
            sel = (grp == lax.broadcasted_iota(jnp.int32, (D_BRANCH, LANES), 1)).astype(BF16)
            db_ref[...] = _split_dot(dmix_acc[...], sel, 3)

    col = lambda c: pl.BlockSpec((tm, D_BRANCH), lambda i: (i, c))
    full = lambda shape: pl.BlockSpec(shape, lambda i: (0,) * len(shape))
    return pl.pallas_call(
        body, name="sgu_bwd", grid=(n_steps,),
        in_specs=[col(COL_UB), col(COL_VB), col(COL_ZB), pl.BlockSpec((tm, D_BRANCH), lambda i: (i, 0)),
                  full((1, D_BRANCH)), full((1, D_BRANCH)),
                  full((N_GROUPS, SGU_CHUNK, SGU_CHUNK)), full((N_GROUPS, SGU_CHUNK, SGU_CHUNK)),
                  full((SGU_CHUNK, D_BRANCH))],
        out_specs=[pl.BlockSpec((tm, 3 * D_BRANCH), lambda i: (i, 0)),
                   full((N_GROUPS, SGU_CHUNK, SGU_CHUNK)), full((SGU_CHUNK, LANES)),
                   full((1, D_BRANCH)), full((1, D_BRANCH))],
        out_shape=[jax.ShapeDtypeStruct((s, 3 * D_BRANCH), BF16),
                   jax.ShapeDtypeStruct((N_GROUPS, SGU_CHUNK, SGU_CHUNK), F32),
                   jax.ShapeDtypeStruct((SGU_CHUNK, LANES), F32),
                   jax.ShapeDtypeStruct((1, D_BRANCH), F32), jax.ShapeDtypeStruct((1, D_BRANCH), F32)],
        scratch_shapes=[pltpu.VMEM((SGU_CHUNK, D_BRANCH), F32)],
        compiler_params=_cparams(("arbitrary",)),
    )(proj, proj, proj, dyb, ln_g, ln_b, w_mask, w_mask_t, bias_full)


def _mid(proj, ya, yb, o, x, target, final_g, w_up_a, w_up_b, w_out):
    s = x.shape[0]
    tm = min(256, s)
    n_steps = s // tm
    half = D_MODEL // 2

    def body(ya_ref, yb_ref, o_ref, za_ref, ga0_ref, ga1_ref, gb0_ref, gb1_ref, x_ref, t_ref, gf_ref,
             wa_ref, wb_ref, wo_ref,
             dzg_ref, do_ref, dyb_ref, dx2_ref, gwo_ref, gwa_ref, gwb_ref, loss_ref, dgf_ref,
             acc_o, acc_a, acc_b):
        i = pl.program_id(0)

        @pl.when(i == 0)
        def _():
            acc_o[...] = jnp.zeros_like(acc_o)
            acc_a[...] = jnp.zeros_like(acc_a)
            acc_b[...] = jnp.zeros_like(acc_b)
            loss_ref[...] = jnp.zeros_like(loss_ref)
            dgf_ref[...] = jnp.zeros_like(dgf_ref)

        ya_v = ya_ref[...]
        yb_v = yb_ref[...]
        pa = jnp.dot(ya_v, wa_ref[...], preferred_element_type=F32)
        pb = jnp.dot(yb_v, wb_ref[...], preferred_element_type=F32)
        sa = _sigmoid(jnp.concatenate([ga0_ref[...], ga1_ref[...]], axis=1).astype(F32))
        sb = _sigmoid(jnp.concatenate([gb0_ref[...], gb1_ref[...]], axis=1).astype(F32))
        merged = (sa * pa + sb * pb).astype(BF16)
        x2 = x_ref[...] + jnp.dot(merged, wo_ref[...], preferred_element_type=F32)
        r2 = lax.rsqrt(jnp.mean(x2 * x2, axis=-1, keepdims=True) + EPS)
        xh = x2 * r2
        gf = gf_ref[...]
        diff = xh * gf - t_ref[...]
        loss_ref[...] += 0.5 * jnp.sum(jnp.mean(diff * diff, axis=-1, keepdims=True))
        dy = diff * (1.0 / D_MODEL)
        dgf_ref[...] += jnp.sum(dy * xh, axis=0, keepdims=True)
        dyg = dy * gf
        dx2 = r2 * (dyg - xh * jnp.mean(dyg * xh, axis=-1, keepdims=True))
        dx2_ref[...] = dx2
        dx2b = dx2.astype(BF16)
        dmerged = _dot_nt(dx2b, wo_ref[...])
        acc_o[...] += _dot_tn(merged, dx2b)
        dpa = dmerged * sa
        dpb = dmerged * sb
        dzg_ref[:, D_BRANCH:D_BRANCH + D_MODEL] = (dpa * pa * (1.0 - sa)).astype(BF16)
        dzg_ref[:, D_BRANCH + D_MODEL:D_BRANCH + 2 * D_MODEL] = (dpb * pb * (1.0 - sb)).astype(BF16)
        dpab = dpa.astype(BF16)
        dpbb = dpb.astype(BF16)
        acc_a[...] += _dot_tn(ya_v, dpab)
        acc_b[...] += _dot_tn(yb_v, dpbb)
        dya = _dot_nt(dpab, wa_ref[...])
        dyb_ref[...] = _dot_nt(dpbb, wb_ref[...]).astype(BF16)
        za = za_ref[...].astype(F32)
        sg = _sigmoid(za)
        do_ref[...] = (dya * (za * sg)).astype(BF16)
        dzg_ref[:, 0:D_BRANCH] = (dya * o_ref[...].astype(F32) * (sg * (1.0 + za * (1.0 - sg)))).astype(BF16)

        @pl.when(i == n_steps - 1)
        def _():
            gwo_ref[...] = acc_o[...].astype(BF16)
            gwa_ref[...] = acc_a[...].astype(BF16)
            gwb_ref[...] = acc_b[...].astype(BF16)

    tok = lambda w: pl.BlockSpec((tm, w), lambda i: (i, 0))
    col = lambda c: pl.BlockSpec((tm, half), lambda i: (i, c))
    full = lambda shape: pl.BlockSpec(shape, lambda i: (0,) * len(shape))
    return pl.pallas_call(
        body, name="mid", grid=(n_steps,),
        in_specs=[tok(D_BRANCH), tok(D_BRANCH), tok(D_BRANCH), col(COL_ZA), col(COL_GA), col(COL_GA + 1),
                  col(COL_GB), col(COL_GB + 1), tok(D_MODEL), tok(D_MODEL), full((1, D_MODEL)),
                  full((D_BRANCH, D_MODEL)), full((D_BRANCH, D_MODEL)), full((D_MODEL, D_MODEL))],
        out_specs=[tok(D_BRANCH + 2 * D_MODEL), tok(D_BRANCH), tok(D_BRANCH), tok(D_MODEL),
                   full((D_MODEL, D_MODEL)), full((D_BRANCH, D_MODEL)), full((D_BRANCH, D_MODEL)),
                   full((8, LANES)), full((1, D_MODEL))],
        out_shape=[jax.ShapeDtypeStruct((s, D_BRANCH + 2 * D_MODEL), BF16),
                   jax.ShapeDtypeStruct((s, D_BRANCH), BF16), jax.ShapeDtypeStruct((s, D_BRANCH), BF16),
                   jax.ShapeDtypeStruct((s, D_MODEL), F32),
                   jax.ShapeDtypeStruct((D_MODEL, D_MODEL), BF16),
                   jax.ShapeDtypeStruct((D_BRANCH, D_MODEL), BF16), jax.ShapeDtypeStruct((D_BRANCH, D_MODEL), BF16),
                   jax.ShapeDtypeStruct((8, LANES), F32), jax.ShapeDtypeStruct((1, D_MODEL), F32)],
        scratch_shapes=[pltpu.VMEM((D_MODEL, D_MODEL), F32), pltpu.VMEM((D_BRANCH, D_MODEL), F32),
                        pltpu.VMEM((D_BRANCH, D_MODEL), F32)],
        compiler_params=_cparams(("arbitrary",)),
    )(ya, yb, o, proj, proj, proj, proj, proj, x, target, final_g, w_up_a, w_up_b, w_out)


def _dwin_early(ht, first, tile_of_first, second, tile_of_second):
    s = ht.shape[1]
    n1 = first.shape[1] // D_BRANCH
    n2 = second.shape[1] // D_BRANCH

    def body(ht_ref, a_ref, b_ref, out_ref):
        j = pl.program_id(0)

        @pl.when(j < n1)
        def _():
            out_ref[...] = jnp.dot(ht_ref[...], a_ref[...], preferred_element_type=F32).astype(BF16)

        @pl.when(j >= n1)
        def _():
            out_ref[...] = jnp.dot(ht_ref[...], b_ref[...], preferred_element_type=F32).astype(BF16)

    return pl.pallas_call(
        body, name="dwin_early", grid=(n1 + n2,),
        in_specs=[pl.BlockSpec((D_MODEL, s), lambda j: (0, 0)),
                  pl.BlockSpec((s, D_BRANCH), lambda j: (0, jnp.minimum(j, n1 - 1))),
                  pl.BlockSpec((s, D_BRANCH), lambda j: (0, jnp.maximum(j - n1, 0)))],
        out_specs=pl.BlockSpec((D_MODEL, D_BRANCH),
                               lambda j: (0, jnp.where(j < n1, tile_of_first(j), tile_of_second(j - n1)))),
        out_shape=jax.ShapeDtypeStruct((D_MODEL, D_IN), BF16),
        compiler_params=_cparams(("arbitrary",)),
    )(ht, first, second)


def _dwin_pieces(ht, pieces, first_tile, prev):
    s = ht.shape[1]
    n = len(pieces)

    def body(ht_ref, *refs):
        srcs = refs[:n]
        out_ref, buf, sems = refs[n + 1:]
        j = pl.program_id(0)

        @pl.when(j == 0)
        def _():
            for k in range(n):
                pltpu.make_async_copy(srcs[k], buf.at[k], sems.at[k]).start()

        for k in range(n):
            @pl.when(j == k)
            def _():
                pltpu.make_async_copy(srcs[k], buf.at[k], sems.at[k]).wait()

        out_ref[...] = jnp.dot(ht_ref[...], buf[j], preferred_element_type=F32).astype(BF16)

    any_spec = pl.BlockSpec(memory_space=pl.ANY)
    return pl.pallas_call(
        body, name="dwin_pieces", grid=(n,),
        in_specs=[pl.BlockSpec((D_MODEL, s), lambda j: (0, 0))] + [any_spec] * (n + 1),
        out_specs=pl.BlockSpec((D_MODEL, D_BRANCH), lambda j: (0, first_tile + j)),
        out_shape=jax.ShapeDtypeStruct((D_MODEL, D_IN), BF16),
        scratch_shapes=[pltpu.VMEM((n, s, D_BRANCH), BF16), pltpu.SemaphoreType.DMA((n,))],
        input_output_aliases={n + 1: 0},
        compiler_params=_cparams(("arbitrary",)),
    )(ht, *pieces, prev)


def _dh_dx(pieces, w_in, x, norm_g, dx2, g_in):
    s = x.shape[0]
    tm = min(256, s)
    n_steps = s // tm
    arrays = []
    for arr, _, _, _ in pieces:
        if not any(arr is a for a in arrays):
            arrays.append(arr)
    n_arr = len(arrays)
    plan = [([k for k, a in enumerate(arrays) if a is arr][0], wcol, off, width) for arr, wcol, off, width in pieces]

    def body(*refs):
        p_refs = refs[:n_arr]
        w_ref, x_ref, g_ref, dx2_ref, gin_ref, dx_ref, dg_ref, late_ref = refs[n_arr:n_arr + 8]
        step = pl.program_id(0)
        finish = _presum_program(step == 0, step == min(1, n_steps - 1), step == n_steps - 1, 0, _LATE_CHIPS,
                                 gin_ref, late_ref, refs[n_arr + 8:])

        @pl.when(step == 0)
        def _():
            dg_ref[...] = jnp.zeros_like(dg_ref)

        dh = None
        for k, wcol, off, width in plan:
            d = _dot_nt(p_refs[k][:, off:off + width], w_ref[:, wcol:wcol + width])
            dh = d if dh is None else dh + d
        xf = x_ref[...]
        r = lax.rsqrt(jnp.mean(xf * xf, axis=-1, keepdims=True) + EPS)
        xh = xf * r
        dg_ref[...] += jnp.sum(dh * xh, axis=0, keepdims=True)
        dhg = dh * g_ref[...]
        dx_ref[...] = r * (dhg - xh * jnp.mean(dhg * xh, axis=-1, keepdims=True)) + dx2_ref[...]

        @pl.when(step == n_steps - 1)
        def _():
            finish()

    tok = lambda w: pl.BlockSpec((tm, w), lambda i: (i, 0))
    full = lambda shape: pl.BlockSpec(shape, lambda i: (0,) * len(shape))
    any_spec = pl.BlockSpec(memory_space=pl.ANY)
    return pl.pallas_call(
        body, name="dh_dx", grid=(n_steps,),
        in_specs=[tok(a.shape[1]) for a in arrays] + [full((D_MODEL, D_IN)), tok(D_MODEL), full((1, D_MODEL)),
                                                      tok(D_MODEL), any_spec],
        out_specs=[tok(D_MODEL), full((1, D_MODEL)), any_spec],
        out_shape=[jax.ShapeDtypeStruct((s, D_MODEL), F32), jax.ShapeDtypeStruct((1, D_MODEL), F32),
                   jax.ShapeDtypeStruct((N_DEV,) + _GRAD_PIECE_SHAPES[0], BF16)],
        scratch_shapes=_presum_scratch(0, _LATE_CHIPS),
        compiler_params=pltpu.CompilerParams(dimension_semantics=("arbitrary",), vmem_limit_bytes=VMEM_LIMIT,
                                             has_side_effects=True),
    )(*arrays, w_in, x, norm_g, dx2, g_in)


def _adamw(w, g, m, v):
    rows, cols = w.shape
    tr = max(t for t in range(8, 257, 8) if rows % t == 0)
    c1 =1.0 - ADAM_B1 ** ADAM_STEP
    c2 = 1.0 - ADAM_B2 ** ADAM_STEP

    def body(w_ref, g_ref, m_ref, v_ref, g_out_ref, d_ref, nm_ref, nv_ref):
        gv = g_ref[...]
        g_out_ref[...] = gv
        nm = ADAM_B1 * m_ref[...] + (1.0 - ADAM_B1) * gv
        nv = ADAM_B2 * v_ref[...] + (1.0 - ADAM_B2) * (gv * gv)
        d_ref[...] = -ADAM_LR * ((nm / c1) / (jnp.sqrt(nv / c2) + ADAM_EPS) + ADAM_WD * w_ref[...])
        nm_ref[...] = nm
        nv_ref[...] = nv

    spec = pl.BlockSpec((tr, cols), lambda i: (i, 0))
    return pl.pallas_call(
        body, name="adamw", grid=(rows // tr,),
        in_specs=[spec] * 4, out_specs=[spec] * 4,
        out_shape=[jax.ShapeDtypeStruct((rows, cols), F32)] * 4,
        compiler_params=_cparams(("parallel",)),
    )(w, g, m, v)


def _reduce_grads_tail(grads, g_small, early_slots, late_in_slots):
    n_big = len(grads)
    n_arr = n_big + 1
    shard_shapes = [(2 * r, w) for r, w in _GRAD_PIECE_SHAPES]
    small_piece = (SMALL_PIECE, LANES)

    def body(*refs):
        src = refs[:n_arr]
        early = refs[n_arr:n_arr + n_big]
        late_in = refs[n_arr + n_big]
        n_in = n_arr + n_big + 1
        out = refs[n_in:n_in + n_arr]
        slots = refs[n_in + n_arr:n_in + 2 * n_arr]
        sums = refs[n_in + 2 * n_arr:n_in + 3 * n_arr]
        send1, recv1, send2, recv2, local_sems = refs[n_in + 3 * n_arr:]
        x, y, c = _place()
        me = 4 * x + 2 * y + c

        def piece_of(a, dev):
            return src[a].at[dev] if a == n_big else _grad_piece(src[a], a, dev)

        def late(a, dst_dev, src_dev):
            return pltpu.make_async_remote_copy(
                src_ref=piece_of(a, dst_dev), dst_ref=slots[a].at[src_dev],
                send_sem=send1.at[n_arr * dst_dev + a], recv_sem=recv1.at[n_arr * src_dev + a],
                device_id=_dev_id(dst_dev), device_id_type=MESH)

        def late_arrays(dev):
            return (n_big,)

        def load(a, dev, received):
            return pltpu.make_async_copy(received.at[dev], slots[a].at[dev], local_sems.at[n_arr * dev + a])

        def own(a, dev):
            return pltpu.make_async_copy(piece_of(a, dev), slots[a].at[dev], local_sems.at[n_arr * dev + a])

        for dev in range(N_DEV):
            @pl.when(me == dev)
            def _():
                received = [early[0] if dev in _EARLY_IN_DEVS else late_in] + list(early[1:])
                for a in range(n_arr):
                    own(a, dev).start()
                sources = sorted([dev] + _presum_sources(dev))
                for peer in range(N_DEV):
                    if peer != dev:
                        for a in late_arrays(peer):
                            late(a, peer, dev).start()
                        for a in range(n_big):
                            if peer in sources:
                                load(a, peer, received[a]).start()
                for a in range(n_arr):
                    own(a, dev).wait()
                for peer in range(N_DEV):
                    if peer != dev:
                        for a in late_arrays(dev):
                            late(a, dev, peer).wait_recv()
                        for a in range(n_big):
                            if peer in sources:
                                load(a, peer, received[a]).wait()
                for a in range(n_arr):
                    rows = slots[a].shape[1]
                    step = 64 if rows % 64 == 0 else 8
                    used = sources if a < n_big else list(range(N_DEV))

                    def add_rows(t, carry, a=a, step=step, used=used):
                        r0 = pl.multiple_of(t * step, step)
                        total = slots[a][used[0], pl.ds(r0, step), :].astype(F32)
                        for src_dev in used[1:]:
                            total = total + slots[a][src_dev, pl.ds(r0, step), :].astype(F32)
                        sums[a][pl.ds(r0, step), :] = total
                        return carry

                    lax.fori_loop(0, rows // step, add_rows, 0)

        shares = []
        keeps = []
        for a in range(n_big):
            r, w = _GRAD_PIECE_SHAPES[a]
            dst = out[a].at[pl.ds(pl.multiple_of(c * r, 8), r), :]
            cp = pltpu.make_async_remote_copy(src_ref=sums[a], dst_ref=dst, send_sem=send2.at[a], recv_sem=recv2.at[a],
                                              device_id=(x, y, 1 - c), device_id_type=MESH)
            cp.start()
            shares.append(cp)
            kp = pltpu.make_async_copy(sums[a], dst, local_sems.at[N_DEV * n_arr + a])
            kp.start()
            keeps.append(kp)
        kp = pltpu.make_async_copy(sums[n_big], out[n_big].at[me], local_sems.at[N_DEV * n_arr + n_big])
        kp.start()
        keeps.append(kp)

        def small_share(dst_dev, src_dev):
            return pltpu.make_async_remote_copy(src_ref=sums[n_big], dst_ref=out[n_big].at[src_dev],
                                                send_sem=send2.at[n_big + dst_dev], recv_sem=recv2.at[n_big + src_dev],
                                                device_id=_dev_id(dst_dev), device_id_type=MESH)

        for dev in range(N_DEV):
            @pl.when(me != dev)
            def _():
                small_share(dev, me).start()
        for a in range(n_big):
            r, w = _GRAD_PIECE_SHAPES[a]
            other = out[a].at[pl.ds(pl.multiple_of((1 - c) * r, 8), r), :]
            pltpu.make_async_remote_copy(src_ref=sums[a], dst_ref=other, send_sem=send2.at[a], recv_sem=recv2.at[a],
                                         device_id=(x, y, 1 - c), device_id_type=MESH).wait_recv()
        for dev in range(N_DEV):
            @pl.when(me != dev)
            def _():
                small_share(dev, dev).wait_recv()
                small_share(dev, me).wait_send()
                for a in late_arrays(dev):
                    late(a, dev, me).wait_send()
        for cp in shares:
            cp.wait_send()
        for kp in keeps:
            kp.wait()

    any_spec = pl.BlockSpec(memory_space=pl.ANY)
    return pl.pallas_call(
        body, name="reduce_grads_tail",
        in_specs=[any_spec] * (n_arr + n_big + 1), out_specs=[any_spec] * n_arr,
        out_shape=[jax.ShapeDtypeStruct(sh, F32) for sh in shard_shapes]
        + [jax.ShapeDtypeStruct((N_DEV,) + small_piece, F32)],
        scratch_shapes=[pltpu.VMEM((N_DEV,) + sh, BF16) for sh in _GRAD_PIECE_SHAPES]
        + [pltpu.VMEM((N_DEV,) + small_piece, F32)]
        + [pltpu.VMEM(sh, F32) for sh in _GRAD_PIECE_SHAPES] + [pltpu.VMEM(small_piece, F32)]
        + [pltpu.SemaphoreType.DMA((N_DEV * n_arr,)), pltpu.SemaphoreType.DMA((N_DEV * n_arr,)),
           pltpu.SemaphoreType.DMA((n_big + N_DEV,)), pltpu.SemaphoreType.DMA((n_big + N_DEV,)),
           pltpu.SemaphoreType.DMA((N_DEV * n_arr + n_arr,))],
        compiler_params=pltpu.CompilerParams(vmem_limit_bytes=VMEM_LIMIT, has_side_effects=True),
    )(*grads, g_small, *early_slots, late_in_slots)


_SMALL_PARTS = (("norm_g", 8), ("sgu_ln_g", 8), ("sgu_ln_b", 8), ("w_spatial", 1024), ("b_spatial", 8),
                ("final_norm_g", 8))
_LOSS_ROW = sum(n for _, n in _SMALL_PARTS)


def _pack_small(parts, loss_tile=None):
    rows = []
    for name, n_rows in _SMALL_PARTS:
        a = parts[name].reshape(-1, LANES).astype(F32)
        a = jnp.pad(a, ((0, n_rows - a.shape[0]), (0, 0)))
        rows.append(a)
    rows.append(jnp.zeros((8, LANES), F32) if loss_tile is None else loss_tile)
    rows.append(jnp.zeros((SMALL_ROWS - _LOSS_ROW - 8, LANES), F32))
    return jnp.concatenate(rows, axis=0)


def _unpack_small(packed, shapes):
    out = {}
    r0 = 0
    for name, n_rows in _SMALL_PARTS:
        n = math.prod(shapes[name])
        out[name] = packed[r0:r0 + n // LANES].reshape(shapes[name])
        r0 += n_rows
    return out


def _local_step(proj, ht, x, target, norm_g, w_in, sgu_ln_g, sgu_ln_b, w_spatial, b_spatial, w_up_a, w_up_b, w_out,
                final_norm_g, bq, bk):
    pos = jnp.arange(SGU_CHUNK)
    keep = (pos[None, :] // SGU_SUBCHUNK) <= (pos[:, None] // SGU_SUBCHUNK)
    w_mask = jnp.where(keep[None], w_spatial, 0.0).astype(BF16)
    w_mask_t = jnp.swapaxes(w_mask, 1, 2)
    bias_full = jnp.repeat(b_spatial.T, GROUP_DIM, axis=1)
    ln_g = sgu_ln_g.reshape(1, D_BRANCH)
    ln_b = sgu_ln_b.reshape(1, D_BRANCH)
    final_g = final_norm_g.reshape(1, D_MODEL)

    o, ya, rsave = _attn_fwd(proj, bq, bk, ATTN_PAIRS)
    yb = _sgu_fwd(proj, ln_g, ln_b, w_mask, bias_full)
    dzg, do, dyb, dx2, g_out, g_up_a, g_up_b, loss_acc, d_final = _mid(
        proj, ya, yb, o, x, target, final_g, w_up_a, w_up_b, w_out)
    dsgu, d_wsp, d_bsp, d_lng, d_lnb = _sgu_bwd(proj, dyb, ln_g, ln_b, w_mask, w_mask_t, bias_full)
    g_in = _dwin_early(ht, dzg, lambda j: jnp.where(j == 0, COL_ZA, COL_GA - 1 + j), dsgu, lambda j: COL_UB + j)
    dq, dk, dv, *early_slots = _attn_bwd(proj, do, rsave, bq, bk, ATTN_PAIRS, (g_in, g_up_a, g_up_b, g_out))
    g_in = _dwin_pieces(ht, (dq, dk, dv), COL_Q, g_in)
    pieces = [(dq, COL_Q * D_BRANCH, 0, D_BRANCH), (dk, COL_K * D_BRANCH, 0, D_BRANCH),
              (dv, COL_V * D_BRANCH, 0, D_BRANCH), (dzg, COL_ZA * D_BRANCH, 0, D_BRANCH),
              (dsgu, COL_UB * D_BRANCH, 0, 3 * D_BRANCH), (dzg, COL_GA * D_BRANCH, D_BRANCH, 2 * D_MODEL)]
    dx, d_norm, late_in_slots = _dh_dx(pieces, w_in, x, norm_g, dx2, g_in)
    small = {"norm_g": d_norm, "sgu_ln_g": d_lng, "sgu_ln_b": d_lnb, "w_spatial": d_wsp,
             "b_spatial": d_bsp[:, :N_GROUPS].T, "final_norm_g": d_final}
    return loss_acc, dx, (g_in, g_up_a, g_up_b, g_out), small, early_slots, late_in_slots


def kernel(x, norm_g, w_in, sgu_ln_g, sgu_ln_b, w_spatial, b_spatial, w_up_a, w_up_b, w_out, final_norm_g, loss_target, m_norm_g, m_w_in, m_sgu_ln_g, m_sgu_ln_b, m_w_spatial, m_b_spatial, m_w_up_a, m_w_up_b, m_w_out, m_final_norm_g, v_norm_g, v_w_in, v_sgu_ln_g, v_sgu_ln_b, v_w_spatial, v_b_spatial, v_w_up_a, v_w_up_b, v_w_out, v_final_norm_g):
    big_names = ("w_in", "w_up_a", "w_up_b", "w_out")
    small_names = tuple(n for n, _ in _SMALL_PARTS)
    names = ("norm_g", "w_in", "sgu_ln_g", "sgu_ln_b", "w_spatial", "b_spatial", "w_up_a", "w_up_b", "w_out",
             "final_norm_g")
    w = dict(norm_g=norm_g, w_in=w_in, sgu_ln_g=sgu_ln_g, sgu_ln_b=sgu_ln_b, w_spatial=w_spatial,
             b_spatial=b_spatial, w_up_a=w_up_a, w_up_b=w_up_b, w_out=w_out, final_norm_g=final_norm_g)
    m = dict(norm_g=m_norm_g, w_in=m_w_in, sgu_ln_g=m_sgu_ln_g, sgu_ln_b=m_sgu_ln_b, w_spatial=m_w_spatial,
             b_spatial=m_b_spatial, w_up_a=m_w_up_a, w_up_b=m_w_up_b, w_out=m_w_out, final_norm_g=m_final_norm_g)
    v = dict(norm_g=v_norm_g, w_in=v_w_in, sgu_ln_g=v_sgu_ln_g, sgu_ln_b=v_sgu_ln_b, w_spatial=v_w_spatial,
             b_spatial=v_b_spatial, w_up_a=v_w_up_a, w_up_b=v_w_up_b, w_out=v_w_out, final_norm_g=v_final_norm_g)
    shapes = {n: w[n].shape for n in names}
    flat2d = lambda a: a.reshape(a.shape[-2:])

    proj, ht, *full = _in_proj_gather(x[0], norm_g, *[flat2d(w[n]) for n in big_names])
    loss, dx, big_grads, small, early_slots, late_in_slots = _local_step(
        proj, ht, x[0], loss_target[0], norm_g, full[0], sgu_ln_g[0], sgu_ln_b[0], w_spatial[0], b_spatial[0],
        full[1], full[2], full[3], final_norm_g, ATTN_Q_BLOCK, ATTN_K_BLOCK)
    packed = _pack_small(small, loss).reshape(N_DEV, SMALL_PIECE, LANES)
    red = _reduce_grads_tail(big_grads, packed, early_slots, late_in_slots)

    grads, deltas, new_m, new_v = {}, {}, {}, {}
    for n, g in zip(big_names, red[:4]):
        g, d, nm, nv = _adamw(flat2d(w[n]), g, flat2d(m[n]), flat2d(v[n]))
        grads[n], deltas[n], new_m[n], new_v[n] = (a.reshape(shapes[n]) for a in (g, d, nm, nv))
    g_small = red[4].reshape(SMALL_ROWS, LANES)
    g_small, d, nm, nv = _adamw(_pack_small({n: w[n] for n in small_names}), g_small,
                                _pack_small({n: m[n] for n in small_names}),
                                _pack_small({n: v[n] for n in small_names}))
    for src, dst in ((g_small, grads), (d, deltas), (nm, new_m), (nv, new_v)):
        dst.update(_unpack_small(src, shapes))

    return (g_small[_LOSS_ROW, 0], dx[None], *[grads[n] for n in names], *[deltas[n] for n in names],
            *[new_m[n] for n in names], *[new_v[n] for n in names])
```

```python
import math

import jax
import jax.numpy as jnp
from jax import lax
from jax.experimental import pallas as pl
from jax.experimental.pallas import tpu as pltpu

F32 = jnp.float32
BF16 = jnp.bfloat16

D_MODEL = 1024
N_HEADS = 8
HEAD_DIM = 64
D_BRANCH = 512
D_IN = 4 * D_BRANCH + 3 * D_BRANCH + 2 * D_MODEL
N_GROUPS = 8
GROUP_DIM = 64
SGU_CHUNK = 128
SGU_SUBCHUNK = 64
GROUP_SHIFT = 6
EPS = 1e-6
LANES = 128
ATTN_Q_BLOCK = 256
ATTN_K_BLOCK = 256
DEAD = -110.0
SKIPPED = -1e30
SCAN_PASSES = 1
ATTN_PAIRS = 2
N_CHIPS = 4
N_DEV = 8
MESH = pl.DeviceIdType.MESH

ADAM_LR = 0.001
ADAM_B1 = 0.9
ADAM_B2 = 0.999
ADAM_EPS = 1e-08
ADAM_WD = 0.01
ADAM_STEP = 10

COL_Q, COL_K, COL_V, COL_ZA, COL_UB, COL_VB, COL_ZB, COL_GA, COL_GB = 0, 1, 2, 3, 4, 5, 6, 7, 9

VMEM_LIMIT = 56 * 1024 * 1024

SMALL_ROWS = 1088
SMALL_PIECE = SMALL_ROWS // N_DEV


def _cparams(sem=None):
    return pltpu.CompilerParams(dimension_semantics=sem, vmem_limit_bytes=VMEM_LIMIT)


def _aligned(v, m):
    return v if isinstance(v, int) else pl.multiple_of(v, m)


def _sigmoid(x):
    return 1.0 / (1.0 + jnp.exp(-x))


def _gelu_and_grad(x):
    k = math.sqrt(2.0 / math.pi)
    x2 = x * x
    inner = k * (x + 0.044715 * x * x2)
    th = jnp.tanh(inner)
    g = 0.5 * x * (1.0 + th)
    dg = 0.5 * (1.0 + th) + 0.5 * x * (1.0 - th * th) * (k * (1.0 + 3.0 * 0.044715 * x2))
    return g, dg


def _split_dot(a, b_bf16, passes):
    out = None
    rem = a
    for _ in range(passes):
        part = rem.astype(BF16)
        d = jnp.dot(part, b_bf16, preferred_element_type=F32)
        out = d if out is None else out + d
        rem = rem - part.astype(F32)
    return out


def _dot_nt(a, b):
    return lax.dot_general(a, b, (((1,), (1,)), ((), ())), preferred_element_type=F32)


def _dot_tn(a, b):
    return lax.dot_general(a, b, (((0,), (0,)), ((), ())), preferred_element_type=F32)


def _place():
    x, y, c = lax.axis_index("x"), lax.axis_index("y"), lax.axis_index("c")
    return x, y, c


def _in_proj_gather(x, norm_g, w_in, w_up_a, w_up_b, w_out):
    s = x.shape[0]
    tm = min(1024, s)
    nt = s // tm
    shards = (w_in, w_up_a, w_up_b, w_out)
    n_arr = len(shards)
    col_sharded = (True, True, True, False)
    full_shapes = ((D_MODEL, D_IN), (D_BRANCH, D_MODEL), (D_BRANCH, D_MODEL), (D_MODEL, D_MODEL))
    w_shard = w_in.shape[1]
    half_rows = D_MODEL // 2
    stage_rows = 256

    def body(order_ref, x_ref, g_ref, *refs):
        src = refs[:n_arr]
        proj_ref, ht_ref = refs[n_arr:n_arr + 2]
        out = refs[n_arr + 2:2 * n_arr + 2]
        wsc, h_scr, stage = refs[2 * n_arr + 2:2 * n_arr + 5]
        small_stage = refs[2 * n_arr + 5:2 * n_arr + 8]
        small_cast = refs[2 * n_arr + 8:2 * n_arr + 11]
        send_sems, recv_sems, local_sems = refs[2 * n_arr + 11:]
        k = pl.program_id(0)
        i = pl.program_id(1)
        x_, y_, c = _place()
        chip = 2 * x_ + y_
        sibling = (x_, y_, 1 - c)
        others = [(x_, 1 - y_), (1 - x_, y_), (1 - x_, 1 - y_)]

        def region(a, chip_idx, half):
            if a == 0:
                return wsc.at[chip_idx, pl.ds(_aligned(half * half_rows, 16), half_rows), :]
            r, w = shards[a].shape
            hr = r // 2
            if col_sharded[a]:
                return out[a].at[pl.ds(_aligned(half * hr, 16), hr), pl.ds(_aligned(chip_idx * w, LANES), w)]
            return out[a].at[pl.ds(_aligned(chip_idx * r + half * hr, 16), hr), :]

        def remote(kk, a, chip_idx, half, to, own):
            s_ref = region(a, chip_idx, half)
            if own and a > 0:
                hr = shards[a].shape[0] // 2
                s_ref = small_cast[a - 1].at[pl.ds(_aligned(half * hr, 16), hr), :]
            return pltpu.make_async_remote_copy(src_ref=s_ref, dst_ref=region(a, chip_idx, half),
                                                send_sem=send_sems.at[kk], recv_sem=recv_sems.at[kk],
                                                device_id=to, device_id_type=MESH)

        def keep_whole(kk, chip_idx):
            return pltpu.make_async_copy(wsc.at[chip_idx],
                                         out[0].at[:, pl.ds(_aligned(chip_idx * w_shard, LANES), w_shard)],
                                         local_sems.at[kk])

        def small_stores():
            cps = []
            for a in range(1, n_arr):
                hr = shards[a].shape[0] // 2
                for half in range(2):
                    cps.append(pltpu.make_async_copy(small_cast[a - 1].at[pl.ds(half * hr, hr), :],
                                                     region(a, chip, half), local_sems.at[4 + 2 * (a - 1) + half]))
            return cps

        def arrive_and_pass(j):
            ochip = chip ^ j
            for a in range(n_arr):
                kk = n_arr * (j - 1) + a
                remote(kk, a, ochip, c, sibling, False).wait_recv()
                remote(3 * n_arr + kk, a, ochip, c, sibling, False).start()

        def from_sibling(j, a):
            remote(3 * n_arr + n_arr * (j - 1) + a, a, chip ^ j, 1 - c, sibling, False).wait_recv()

        @pl.when((k == 0) & (i == 0))
        def _():
            def cast_rows(half):
                for t in range(half_rows // stage_rows):
                    r0 = pl.multiple_of(half * half_rows + t * stage_rows, stage_rows)
                    pltpu.sync_copy(src[0].at[pl.ds(r0, stage_rows), :], stage)
                    wsc[chip, pl.ds(r0, stage_rows), :] = stage[...].astype(BF16)

            cast_rows(c)
            for j in (1, 2):
                remote(n_arr * (j - 1), 0, chip, c, (*others[j - 1], c), True).start()
            cast_rows(1 - c)
            for a in range(1, n_arr):
                pltpu.sync_copy(src[a], small_stage[a - 1])
                small_cast[a - 1][...] = small_stage[a - 1][...].astype(BF16)
            for j in (1, 2):
                for a in range(1, n_arr):
                    remote(n_arr * (j - 1) + a, a, chip, c, (*others[j - 1], c), True).start()
            keep_whole(0, chip).start()
            for cp in small_stores():
                cp.start()

        @pl.when((k == 1) & (i == 0))
        def _():
            for j in (1, 2):
                remote(n_arr * (j - 1), 0, chip, c, (*others[j - 1], c), True).wait_send()
            for a in range(n_arr):
                remote(n_arr * 2 + a, a, chip, c, (*others[2], c), True).start()
            arrive_and_pass(1)
            arrive_and_pass(2)
            from_sibling(1, 0)
            keep_whole(1, chip ^ 1).start()

        @pl.when((k == 2) & (i == 0))
        def _():
            from_sibling(2, 0)
            keep_whole(2, chip ^ 2).start()
            arrive_and_pass(3)

        @pl.when((k == 3) & (i == 0))
        def _():
            from_sibling(3, 0)
            keep_whole(3, chip ^ 3).start()

        @pl.when(k == 0)
        def _():
            xf = x_ref[...]
            r = lax.rsqrt(jnp.mean(xf * xf, axis=-1, keepdims=True) + EPS)
            h = xf * r * g_ref[...]
            h_scr[i] = h.astype(BF16)
            ht_ref[...] = h.T.astype(BF16)

        proj_ref[...] = jnp.dot(h_scr[i], wsc[order_ref[k]], preferred_element_type=F32).astype(BF16)

        @pl.when((k == 3) & (i == nt - 1))
        def _():
            for j in (1, 2, 3):
                for a in range(1, n_arr):
                    from_sibling(j, a)
            for j in (1, 2, 3):
                for a in range(n_arr):
                    kk = n_arr * (j - 1) + a
                    if a > 0 or j == 3:
                        remote(kk, a, chip, c, (*others[j - 1], c), True).wait_send()
                    remote(3 * n_arr + kk, a, chip ^ j, c, sibling, False).wait_send()
            for kk in range(4):
                keep_whole(kk, chip ^ kk).wait()
            for cp in small_stores():
                cp.wait()

    any_spec = pl.BlockSpec(memory_space=pl.ANY)
    tile = lambda kk, ii: jnp.where(kk == 0, ii, nt - 1)
    grid_spec = pltpu.PrefetchScalarGridSpec(
        num_scalar_prefetch=1, grid=(N_CHIPS, nt),
        in_specs=[pl.BlockSpec((tm, D_MODEL), lambda kk, ii, order: (tile(kk, ii), 0)),
                  pl.BlockSpec((1, D_MODEL), lambda kk, ii, order: (0, 0))] + [any_spec] * n_arr,
        out_specs=[pl.BlockSpec((tm, w_shard), lambda kk, ii, order: (ii, order[kk])),
                   pl.BlockSpec((D_MODEL, tm), lambda kk, ii, order: (0, tile(kk, ii)))] + [any_spec] * n_arr,
        scratch_shapes=[pltpu.VMEM((N_CHIPS, D_MODEL, w_shard), BF16), pltpu.VMEM((nt, tm, D_MODEL), BF16),
                        pltpu.VMEM((stage_rows, w_shard), F32)]
        + [pltpu.VMEM(a.shape, F32) for a in shards[1:]] + [pltpu.VMEM(a.shape, BF16) for a in shards[1:]]
        + [pltpu.SemaphoreType.DMA((6 * n_arr,)), pltpu.SemaphoreType.DMA((6 * n_arr,)),
           pltpu.SemaphoreType.DMA((4 + 2 * (n_arr - 1),))])
    x_, y_, _ = _place()
    order = (2 * x_ + y_) ^ jnp.arange(N_CHIPS, dtype=jnp.int32)
    return pl.pallas_call(
        body, name="in_proj_gather", grid_spec=grid_spec,
        out_shape=[jax.ShapeDtypeStruct((s, D_IN), BF16), jax.ShapeDtypeStruct((D_MODEL, s), BF16)]
        + [jax.ShapeDtypeStruct(sh, BF16) for sh in full_shapes],
        compiler_params=pltpu.CompilerParams(dimension_semantics=("arbitrary", "arbitrary"),
                                             vmem_limit_bytes=VMEM_LIMIT, has_side_effects=True),
    )(order, x, norm_g, *shards)


def _neg_softplus_parts(z):
    zb = z.astype(BF16)
    p = jnp.exp(-jnp.abs(zb))
    return p, jnp.maximum(zb, jnp.zeros_like(zb)) + jnp.log(1.0 + p)


def _split_cat(a, passes):
    parts = []
    rem = a
    for k in range(passes):
        part = rem.astype(BF16)
        parts.append(part)
        if k + 1 < passes:
            rem = rem - part.astype(F32)
    return parts[0] if passes == 1 else jnp.concatenate(parts, axis=1)


def _tri(blk, upper, sign):
    row = lax.broadcasted_iota(jnp.int32, (blk, blk), 0)
    col = lax.broadcasted_iota(jnp.int32, (blk, blk), 1)
    keep = (row <= col) if upper else (row >= col)
    t = jnp.where(keep, sign, 0.0).astype(BF16)
    return t if SCAN_PASSES == 1 else jnp.concatenate([t] * SCAN_PASSES, axis=0)


def _attn_fwd(proj, bq, bk, npairs):
    s = proj.shape[0]
    nq = s // bq
    ratio = bq // bk
    scale = HEAD_DIM ** -0.5
    heads = tuple(range(2 * npairs))
    width = LANES * npairs

    def body(q_ref, k_ref, v_ref, za_ref, o_ref, ya_ref, rs_ref, acc_ref, r_ref):
        i = pl.program_id(1)
        lane = lax.broadcasted_iota(jnp.int32, (bq, LANES), 1)
        lo_half = lane < HEAD_DIM
        qm = []
        for pr in range(npairs):
            q = q_ref[:, LANES * pr:LANES * (pr + 1)] * jnp.asarray(scale, BF16)
            zero = jnp.zeros_like(q)
            qm += [jnp.where(lo_half, q, zero), jnp.where(lo_half, zero, q)]
        row = lax.broadcasted_iota(jnp.int32, (bq, bk), 0)
        col = lax.broadcasted_iota(jnp.int32, (bq, bk), 1)
        tneg = _tri(bk, False, -1.0)
        acc_ref[...] = jnp.zeros_like(acc_ref)
        r_ref[...] = jnp.zeros_like(r_ref)
        rs_ref[...] = jnp.full_like(rs_ref, SKIPPED)

        def scores(j):
            ks = pl.multiple_of(j * bk, bk)
            return [_dot_nt(qm[h], k_ref[pl.ds(ks, bk), LANES * (h // 2):LANES * (h // 2 + 1)]) for h in heads]

        def block(j, diag, valid=None):
            ks = pl.multiple_of(j * bk, bk)
            vj = [v_ref[pl.ds(ks, bk), LANES * pr:LANES * (pr + 1)] for pr in range(npairs)]
            if diag:
                before = (j * bk + col) < (i * bq + row)
            z = scores(j)
            sp = [_neg_softplus_parts(z[h])[1] for h in heads]
            if diag:
                sp = [jnp.where(before, sp[h], 0.0) for h in heads]
            cin = [jnp.dot(_split_cat(sp[h], SCAN_PASSES), tneg, preferred_element_type=F32) for h in heads]
            w = [jnp.exp(z[h] + cin[h]) for h in heads]
            if diag:
                w = [jnp.where(before, w[h], 0.0) for h in heads]
            pv = [jnp.dot(w[h].astype(BF16), vj[h // 2], preferred_element_type=F32) for h in heads]
            r = [r_ref[h] for h in heads]
            keep = 1.0 if valid is None else valid.astype(F32)
            for h in heads:
                acc_ref[h] += pv[h] * (jnp.exp(r[h]) * keep)
                r_ref[h] = r[h] + cin[h][:, 0:1] * keep
            for pr in range(npairs):
                hit = [lane == j, lane == j + HEAD_DIM]
                if valid is not None:
                    hit = [m & valid for m in hit]
                rs_ref[pr] = jnp.where(hit[0], r[2 * pr], jnp.where(hit[1], r[2 * pr + 1], rs_ref[pr]))

        for t in range(ratio):
            block(i * ratio + ratio - 1 - t, True)
        block(jnp.maximum(i * ratio - 1, 0), False, valid=i > 0)

        def alive(carry):
            jj, r_max = carry
            return (jj < i * ratio - 1) & (r_max > DEAD)

        def loop_body(carry):
            jj, _ = carry
            block(i * ratio - 2 - jj, False)
            return jj + 1, jnp.max(r_ref[...])

        lax.while_loop(alive, loop_body, (0, jnp.max(r_ref[...])))
        for pr in range(npairs):
            cols = slice(LANES * pr, LANES * (pr + 1))
            o = jnp.where(lo_half, acc_ref[2 * pr], acc_ref[2 * pr + 1])
            o_ref[:, cols] = o.astype(BF16)
            za = za_ref[:, cols].astype(F32)
            ya_ref[:, cols] = (o * (za * _sigmoid(za))).astype(BF16)

    n_steps = N_HEADS // (2 * npairs)
    return pl.pallas_call(
        body, name="attn_fwd", grid=(n_steps, nq),
        in_specs=[pl.BlockSpec((bq, width), lambda p, i: (i, n_steps * COL_Q + p)),
                  pl.BlockSpec((s, width), lambda p, i: (0, n_steps * COL_K + p)),
                  pl.BlockSpec((s, width), lambda p, i: (0, n_steps * COL_V + p)),
                  pl.BlockSpec((bq, width), lambda p, i: (i, n_steps * COL_ZA + p))],
        out_specs=[pl.BlockSpec((bq, width), lambda p, i: (i, p)),
                   pl.BlockSpec((bq, width), lambda p, i: (i, p)),
                   pl.BlockSpec((npairs, bq, LANES), lambda p, i: (p, i, 0))],
        out_shape=[jax.ShapeDtypeStruct((s, D_BRANCH), BF16), jax.ShapeDtypeStruct((s, D_BRANCH), BF16),
                   jax.ShapeDtypeStruct((N_HEADS // 2, s, LANES), F32)],
        scratch_shapes=[pltpu.VMEM((2 * npairs, bq, LANES), F32), pltpu.VMEM((2 * npairs, bq, 1), F32)],
        compiler_params=_cparams(("parallel", "parallel")),
    )(proj, proj, proj, proj)


_GRAD_COL_SHARDED = (True, True, True, False)
_GRAD_FULL_SHAPES = ((D_MODEL, D_IN), (D_BRANCH, D_MODEL), (D_BRANCH, D_MODEL), (D_MODEL, D_MODEL))
_GRAD_PIECE_SHAPES = tuple((r // 2, w // N_CHIPS) if cs else (r // (2 * N_CHIPS), w)
                           for (r, w), cs in zip(_GRAD_FULL_SHAPES, _GRAD_COL_SHARDED))
_EARLY_IN_DEVS = (4, 5, 6, 7)
_LATE_IN_DEVS = (0, 1, 2, 3)
_LATE_CHIPS = (0, 1)
_EARLY_CHIPS = (2, 3)
_ALL_CHIPS = (0, 1, 2, 3)


def _grad_piece(ref, a, dev):
    r, w = _GRAD_PIECE_SHAPES[a]
    if _GRAD_COL_SHARDED[a]:
        return ref.at[pl.ds((dev % 2) * r, r), pl.ds((dev // 2) * w, w)]
    return ref.at[pl.ds(dev * r, r), :]


def _dev_id(dev):
    return (dev // 4, (dev // 2) % 2, dev % 2)


def _me():
    return 4 * lax.axis_index("x") + 2 * lax.axis_index("y") + lax.axis_index("c")


def _presum_copy(src, dst, send_sem, recv_sem, to_dev):
    return pltpu.make_async_remote_copy(src_ref=src, dst_ref=dst, send_sem=send_sem, recv_sem=recv_sem,
                                        device_id=_dev_id(to_dev), device_id_type=MESH)


def _presum_hand_off(dev, a, dest_chips, g_ref, slots, pair, send_sems, recv_sems):
    chip, core = dev // 2, dev % 2
    cps = []
    for k, q in enumerate(dest_chips):
        piece = _grad_piece(g_ref, a, 2 * q + 1 - core)
        if q == chip:
            cps.append(_presum_copy(piece, slots.at[dev], send_sems.at[N_DEV + k], recv_sems.at[dev], dev ^ 1))
        else:
            cps.append(_presum_copy(piece, pair.at[k], send_sems.at[N_DEV + k], recv_sems.at[N_DEV + k], dev ^ 1))
    return cps


def _presum_sends(dev, a, dest_chips, slots, sums, send_sems, recv_sems):
    chip, core = dev // 2, dev % 2
    return [_presum_copy(sums.at[k], slots.at[dev], send_sems.at[2 * q + core], recv_sems.at[dev], 2 * q + core)
            for k, q in enumerate(dest_chips) if q != chip]


def _presum_loads(dev, a, dest_chips, g_ref, stage, load_sems):
    chip, core = dev // 2, dev % 2
    return [pltpu.make_async_copy(_grad_piece(g_ref, a, 2 * q + core), stage.at[k], load_sems.at[k])
            for k, q in enumerate(dest_chips) if q != chip]


def _presum_send(dev, a, dest_chips, g_ref, slots, pair, stage, sums, send_sems, recv_sems, load_sems):
    chip, core = dev // 2, dev % 2
    hand = _presum_hand_off(dev, a, dest_chips, g_ref, slots, pair, send_sems, recv_sems)
    for cp in _presum_loads(dev, a, dest_chips, g_ref, stage, load_sems):
        cp.wait()
    for k, q in enumerate(dest_chips):
        if q != chip:
            hand[k].wait_recv()
            sums[k] = (stage[k].astype(F32) + pair[k].astype(F32)).astype(BF16)
    for cp in _presum_sends(dev, a, dest_chips, slots, sums, send_sems, recv_sems):
        cp.start()


def _presum_wait(dev, a, dest_chips, g_ref, slots, pair, sums, send_sems, recv_sems):
    chip, core = dev // 2, dev % 2
    for cp in _presum_hand_off(dev, a, dest_chips, g_ref, slots, pair, send_sems, recv_sems):
        cp.wait_send()
    for cp in _presum_sends(dev, a, dest_chips, slots, sums, send_sems, recv_sems):
        cp.wait_send()
    if chip in dest_chips:
        for src_dev in _presum_sources(dev):
            _presum_copy(sums.at[0], slots.at[src_dev], send_sems.at[src_dev], recv_sems.at[src_dev], src_dev).wait_recv()


def _presum_sources(dev):
    return [dev ^ 1] + [2 * r + dev % 2 for r in range(N_CHIPS) if r != dev // 2]


def _presum_scratch(a, dest_chips):
    n = len(dest_chips)
    piece = _GRAD_PIECE_SHAPES[a]
    return [pltpu.VMEM((n,) + piece, BF16), pltpu.VMEM((n,) + piece, BF16), pltpu.VMEM((n,) + piece, BF16),
            pltpu.SemaphoreType.DMA((N_DEV + n,)), pltpu.SemaphoreType.DMA((N_DEV + n,)),
            pltpu.SemaphoreType.DMA((n,))]


PRESUM_SCRATCH = 6


def _presum_program(first, second, last, a, dest_chips, g_ref, slots, scratch):
    pair, stage, sums, send_sems, recv_sems, load_sems = scratch
    me = _me()

    @pl.when(first)
    def _():
        for dev in range(N_DEV):
            @pl.when(me == dev)
            def _():
                for cp in _presum_hand_off(dev, a, dest_chips, g_ref, slots, pair, send_sems, recv_sems):
                    cp.start()
                for cp in _presum_loads(dev, a, dest_chips, g_ref, stage, load_sems):
                    cp.start()

    @pl.when(second)
    def _():
        for dev in range(N_DEV):
            @pl.when(me == dev)
            def _():
                _presum_send(dev, a, dest_chips, g_ref, slots, pair, stage, sums, send_sems, recv_sems, load_sems)

    def finish():
        for dev in range(N_DEV):
            @pl.when(me == dev)
            def _():
                _presum_wait(dev, a, dest_chips, g_ref, slots, pair, sums, send_sems, recv_sems)

    return finish


def _attn_bwd(proj, do, rsave, bq, bk, npairs, grads):
    plan = ((0, _EARLY_CHIPS), (1, _ALL_CHIPS), (2, _ALL_CHIPS), (3, _ALL_CHIPS))
    s = proj.shape[0]
    nq = s // bq
    ratio = bq // bk
    scale = HEAD_DIM ** -0.5
    heads = tuple(range(2 * npairs))
    width = LANES * npairs

    n_steps = N_HEADS // (2 * npairs)
    n_g = len(grads)

    def body(q_ref, k_ref, v_ref, do_ref, rs_ref, *refs):
        g_src = refs[:n_g]
        dq_ref, dk_ref, dv_ref = refs[n_g:n_g + 3]
        g_slots = refs[n_g + 3:2 * n_g + 3]
        dk_acc, dv_acc, dq_acc, e_ref = refs[2 * n_g + 3:2 * n_g + 7]
        i = pl.program_id(1)
        step = pl.program_id(0) * nq + i
        finish = [_presum_program(step == 0, step == 1, step == n_steps * nq - 1, a, chips, g_src[pos], g_slots[pos],
                                  refs[2 * n_g + 7 + PRESUM_SCRATCH * pos:2 * n_g + 7 + PRESUM_SCRATCH * (pos + 1)])
                  for pos, (a, chips) in enumerate(plan)]

        lane = lax.broadcasted_iota(jnp.int32, (bq, LANES), 1)
        lo_half = lane < HEAD_DIM
        qm, dom = [], []
        for pr in range(npairs):
            cols = slice(LANES * pr, LANES * (pr + 1))
            q = q_ref[:, cols] * jnp.asarray(scale, BF16)
            zero = jnp.zeros_like(q)
            qm += [jnp.where(lo_half, q, zero), jnp.where(lo_half, zero, q)]
            dout = do_ref[:, cols].astype(F32)
            dom += [jnp.where(lo_half, dout, 0.0), jnp.where(lo_half, 0.0, dout)]
        row = lax.broadcasted_iota(jnp.int32, (bq, bk), 0)
        col = lax.broadcasted_iota(jnp.int32, (bq, bk), 1)
        tneg = _tri(bk, False, -1.0)
        tfwd = _tri(bk, True, 1.0)

        @pl.when(i == 0)
        def _():
            dk_acc[...] = jnp.zeros_like(dk_acc)
            dv_acc[...] = jnp.zeros_like(dv_acc)

        dq_acc[...] = jnp.zeros_like(dq_acc)
        e_ref[...] = jnp.zeros_like(e_ref)

        def block(j, diag, valid=None):
            ks = pl.multiple_of(j * bk, bk)
            kj = [k_ref[pl.ds(ks, bk), LANES * pr:LANES * (pr + 1)] for pr in range(npairs)]
            vj = [v_ref[pl.ds(ks, bk), LANES * pr:LANES * (pr + 1)] for pr in range(npairs)]
            if diag:
                before = (j * bk + col) < (i * bq + row)
            z = [_dot_nt(qm[h], kj[h // 2]) for h in heads]
            er = [jnp.exp(jnp.sum(jnp.where(lane == j + HEAD_DIM * (h % 2), rs_ref[h // 2], 0.0), axis=-1,
                                  keepdims=True)) for h in heads]
            if valid is not None:
                er = [er[h] * valid.astype(F32) for h in heads]
            dos = [(dom[h] * er[h]).astype(BF16) for h in heads]
            dw = [_dot_nt(dos[h], vj[h // 2]) for h in heads]
            psp = [_neg_softplus_parts(z[h]) for h in heads]
            sp = [psp[h][1] for h in heads]
            if diag:
                sp = [jnp.where(before, sp[h], 0.0) for h in heads]
            cin = [jnp.dot(_split_cat(sp[h], SCAN_PASSES), tneg, preferred_element_type=F32) for h in heads]
            w = [jnp.exp(z[h] + cin[h]) for h in heads]
            if diag:
                w = [jnp.where(before, w[h], 0.0) for h in heads]
            e =[dw[h] * w[h] for h in heads]
            eincl = [jnp.dot(_split_cat(e[h], SCAN_PASSES), tfwd, preferred_element_type=F32) + e_ref[h]
                     for h in heads]
            dz = []
            for h in heads:
                p = psp[h][0]
                beta = jnp.where(z[h] >= 0.0, 1.0, p) / (1.0 + p)
                d = e[h] - beta * eincl[h]
                dz.append((jnp.where(before, d, 0.0) if diag else d).astype(BF16))
            wb = [w[h].astype(BF16) for h in heads]
            for h in heads:
                e_ref[h] = eincl[h][:, bk - 1:bk]
                dq_acc[h] += jnp.dot(dz[h], kj[h // 2], preferred_element_type=F32)
            for pr in range(npairs):
                cols = slice(LANES * pr, LANES * (pr + 1))
                h0, h1 = 2 * pr, 2 * pr + 1
                dk_acc[pl.ds(ks, bk), cols] += _dot_tn(dz[h0], qm[h0]) + _dot_tn(dz[h1], qm[h1])
                dv_acc[pl.ds(ks, bk), cols] += _dot_tn(wb[h0], dos[h0]) + _dot_tn(wb[h1], dos[h1])

        def loop_body(j, carry):
            block(j, False)
            return carry

        block_of_lane = lane & (HEAD_DIM - 1)
        live = jnp.max(rs_ref[...], axis=0) > DEAD
        first_live = jnp.min(jnp.where(live, block_of_lane, nq * ratio))
        last = jnp.maximum(i * ratio - 1, 0)
        lax.fori_loop(jnp.minimum(first_live, last), last, loop_body, 0)
        block(last, False, valid=i > 0)
        for t in range(ratio):
            block(i * ratio + t, True)
        for pr in range(npairs):
            dq = jnp.where(lo_half, dq_acc[2 * pr], dq_acc[2 * pr + 1]) * scale
            dq_ref[:, LANES * pr:LANES * (pr + 1)] = dq.astype(BF16)

        @pl.when(i == nq - 1)
        def _():
            dk_ref[...] = dk_acc[...].astype(BF16)
            dv_ref[...] = dv_acc[...].astype(BF16)

        @pl.when(step == n_steps * nq - 1)
        def _():
            for fin in finish:
                fin()

    any_spec = pl.BlockSpec(memory_space=pl.ANY)
    return pl.pallas_call(
        body, name="attn_bwd", grid=(n_steps, nq),
        in_specs=[pl.BlockSpec((bq, width), lambda p, i: (i, n_steps * COL_Q + p)),
                  pl.BlockSpec((s, width), lambda p, i: (0, n_steps * COL_K + p)),
                  pl.BlockSpec((s, width), lambda p, i: (0, n_steps * COL_V + p)),
                  pl.BlockSpec((bq, width), lambda p, i: (i, p)),
                  pl.BlockSpec((npairs, bq, LANES), lambda p, i: (p, i, 0))] + [any_spec] * n_g,
        out_specs=[pl.BlockSpec((bq, width), lambda p, i: (i, p)),
                   pl.BlockSpec((s, width), lambda p, i: (0, p)),
                   pl.BlockSpec((s, width), lambda p, i: (0, p))] + [any_spec] * n_g,
        out_shape=[jax.ShapeDtypeStruct((s, D_BRANCH), BF16)] * 3
        + [jax.ShapeDtypeStruct((N_DEV,) + _GRAD_PIECE_SHAPES[a], BF16) for a, _ in plan],
        scratch_shapes=[pltpu.VMEM((s, width), F32), pltpu.VMEM((s, width), F32),
                        pltpu.VMEM((2 * npairs, bq, LANES), F32), pltpu.VMEM((2 * npairs, bq, 1), F32)]
        + [sh for a, chips in plan for sh in _presum_scratch(a, chips)],
        compiler_params=pltpu.CompilerParams(dimension_semantics=("arbitrary", "arbitrary"),
                                             vmem_limit_bytes=VMEM_LIMIT, has_side_effects=True),
    )(proj, proj, proj, do, rsave, *grads)


def _group_avg_matrix():
    a = lax.broadcasted_iota(jnp.int32, (LANES, LANES), 0) >> GROUP_SHIFT
    b = lax.broadcasted_iota(jnp.int32, (LANES, LANES), 1) >> GROUP_SHIFT
    return jnp.where(a == b, 1.0 / GROUP_DIM, 0.0).astype(BF16)


def _group_mean(a, avg):
    parts = [_split_dot(a[:, LANES * k:LANES * (k + 1)], avg, 2) for k in range(D_BRANCH // LANES)]
    return jnp.concatenate(parts, axis=1)


def _sgu_forward_parts(ub, vb, ln_g, ln_b, avg):
    ug, dug = _gelu_and_grad(ub)
    vg, dvg = _gelu_and_grad(vb)
    mu = _group_mean(vg, avg)
    d = vg - mu
    var = _group_mean(d * d, avg)
    rstd = lax.rsqrt(var + EPS)
    vhat = d * rstd
    vn = vhat * ln_g + ln_b
    return ug, dug, dvg, rstd, vhat, vn


def _sgu_mix(w_ref, src_bf16, n_chunks):
    lane = lax.broadcasted_iota(jnp.int32, (SGU_CHUNK, LANES), 1)
    lo_half = lane < GROUP_DIM
    rows = []
    for n in range(n_chunks):
        slabs = []
        for a in range(D_BRANCH // LANES):
            blk = src_bf16[SGU_CHUNK * n:SGU_CHUNK * (n + 1), LANES * a:LANES * (a + 1)]
            zero = jnp.zeros_like(blk)
            m0 = jnp.dot(w_ref[2 * a], jnp.where(lo_half, blk, zero), preferred_element_type=F32)
            m1 = jnp.dot(w_ref[2 * a + 1], jnp.where(lo_half, zero, blk), preferred_element_type=F32)
            slabs.append(m0 + m1)
        rows.append(jnp.concatenate(slabs, axis=1))
    return jnp.concatenate(rows, axis=0)


def _sgu_fwd(proj, ln_g, ln_b, w_mask, bias_full):
    s = proj.shape[0]
    tm = min(512, s)
    n_chunks = tm // SGU_CHUNK

    def body(ub_ref, vb_ref, zb_ref, g_ref, b_ref, w_ref, bias_ref, yb_ref):
        avg = _group_avg_matrix()
        ug, _, _, _, _, vn = _sgu_forward_parts(ub_ref[...].astype(F32), vb_ref[...].astype(F32),
                                                g_ref[...], b_ref[...], avg)
        mixed = _sgu_mix(w_ref, vn.astype(BF16), n_chunks) + jnp.concatenate([bias_ref[...]] * n_chunks, axis=0)
        zb = zb_ref[...].astype(F32)
        yb_ref[...] = (ug * mixed * (zb * _sigmoid(zb))).astype(BF16)

    col = lambda c: pl.BlockSpec((tm, D_BRANCH), lambda i: (i, c))
    full = lambda shape: pl.BlockSpec(shape, lambda i: (0,) * len(shape))
    return pl.pallas_call(
        body, name="sgu_fwd", grid=(s // tm,),
        in_specs=[col(COL_UB), col(COL_VB), col(COL_ZB), full((1, D_BRANCH)), full((1, D_BRANCH)),
                  full((N_GROUPS, SGU_CHUNK, SGU_CHUNK)), full((SGU_CHUNK, D_BRANCH))],
        out_specs=pl.BlockSpec((tm, D_BRANCH), lambda i: (i, 0)),
        out_shape=jax.ShapeDtypeStruct((s, D_BRANCH), BF16),
        compiler_params=_cparams(("parallel",)),
    )(proj, proj, proj, ln_g, ln_b, w_mask, bias_full)


def _sgu_bwd(proj, dyb, ln_g, ln_b, w_mask, w_mask_t, bias_full):
    s = proj.shape[0]
    tm = min(512, s)
    n_chunks = tm // SGU_CHUNK
    n_steps = s // tm

    def body(ub_ref, vb_ref, zb_ref, dyb_ref, g_ref, b_ref, w_ref, wt_ref, bias_ref,
             dsgu_ref, dw_ref, db_ref, dg_ref, dbeta_ref, dmix_acc):
        i = pl.program_id(0)

        @pl.when(i == 0)
        def _():
            dw_ref[...] = jnp.zeros_like(dw_ref)
            dg_ref[...] = jnp.zeros_like(dg_ref)
            dbeta_ref[...] = jnp.zeros_like(dbeta_ref)
            dmix_acc[...] = jnp.zeros_like(dmix_acc)

        avg = _group_avg_matrix()
        ln_gv = g_ref[...]
        ug, dug, dvg, rstd, vhat, vn = _sgu_forward_parts(ub_ref[...].astype(F32), vb_ref[...].astype(F32),
                                                          ln_gv, b_ref[...], avg)
        vnb = vn.astype(BF16)
        mixed = _sgu_mix(w_ref, vnb, n_chunks) + jnp.concatenate([bias_ref[...]] * n_chunks, axis=0)
        zb = zb_ref[...].astype(F32)
        sg = _sigmoid(zb)
        sz = zb * sg
        dsz = sg * (1.0 + zb * (1.0 - sg))
        dy = dyb_ref[...].astype(F32)
        dmixed = dy * ug * sz
        du = dy * mixed * sz * dug
        dzb = dy * ug * mixed * dsz
        dmb = dmixed.astype(BF16)
        dvn = _sgu_mix(wt_ref, dmb, n_chunks)

        lane = lax.broadcasted_iota(jnp.int32, (SGU_CHUNK, LANES), 1)
        lo_half = lane < GROUP_DIM
        dm_sum = None
        for n in range(n_chunks):
            rows = slice(SGU_CHUNK * n, SGU_CHUNK * (n + 1))
            dm_sum = dmixed[rows] if dm_sum is None else dm_sum + dmixed[rows]
            for a in range(D_BRANCH // LANES):
                cols = slice(LANES * a, LANES * (a + 1))
                dblk = dmb[rows, cols]
                vblk = vnb[rows, cols]
                zero = jnp.zeros_like(dblk)
                dw_ref[2 * a] += _dot_nt(jnp.where(lo_half, dblk, zero), vblk)
                dw_ref[2 * a + 1] += _dot_nt(jnp.where(lo_half, zero, dblk), vblk)
        dmix_acc[...] += dm_sum

        dg_ref[...] += jnp.sum(dvn * vhat, axis=0, keepdims=True)
        dbeta_ref[...] += jnp.sum(dvn, axis=0, keepdims=True)
        dvh = dvn * ln_gv
        m1 = _group_mean(dvh, avg)
        m2 = _group_mean(dvh * vhat, avg)
        dv = rstd * (dvh - m1 - vhat * m2) * dvg
        dsgu_ref[:, 0:D_BRANCH] = du.astype(BF16)
        dsgu_ref[:, D_BRANCH:2 * D_BRANCH] = dv.astype(BF16)
        dsgu_ref[:, 2 * D_BRANCH:3 * D_BRANCH] = dzb.astype(BF16)

        @pl.when(i == n_steps - 1)
        def _():
            pos = lax.broadcasted_iota(jnp.int32, (SGU_CHUNK, SGU_CHUNK), 0) >> GROUP_SHIFT
            src = lax.broadcasted_iota(jnp.int32, (SGU_CHUNK, SGU_CHUNK), 1) >> GROUP_SHIFT
            keep = src <= pos
            for g in range(N_GROUPS):
                dw_ref[g] = jnp.where(keep, dw_ref[g], 0.0)
            grp = lax.broadcasted_iota(jnp.int32, (D_BRANCH, LANES), 0) >> GROUP_SHIFT
            sel = (grp == lax.broadcasted_iota(jnp.int32, (D_BRANCH, LANES), 1)).astype(BF16)
            db_ref[...] = _split_dot(dmix_acc[...], sel, 3)

    col = lambda c: pl.BlockSpec((tm, D_BRANCH), lambda i: (i, c))
    full = lambda shape: pl.BlockSpec(shape, lambda i: (0,) * len(shape))
    return pl.pallas_call(
        body, name="sgu_bwd", grid=(n_steps,),
        in_specs=[col(COL_UB), col(COL_VB), col(COL_ZB), pl.BlockSpec((tm, D_BRANCH), lambda i: (i, 0)),
                  full((1, D_BRANCH)), full((1, D_BRANCH)),
                  full((N_GROUPS, SGU_CHUNK, SGU_CHUNK)), full((N_GROUPS, SGU_CHUNK, SGU_CHUNK)),
                  full((SGU_CHUNK, D_BRANCH))],
        out_specs=[pl.BlockSpec((tm, 3 * D_BRANCH), lambda i: (i, 0)),
                   full((N_GROUPS, SGU_CHUNK, SGU_CHUNK)), full((SGU_CHUNK, LANES)),
                   full((1, D_BRANCH)), full((1, D_BRANCH))],
        out_shape=[jax.ShapeDtypeStruct((s, 3 * D_BRANCH), BF16),
                   jax.ShapeDtypeStruct((N_GROUPS, SGU_CHUNK, SGU_CHUNK), F32),
                   jax.ShapeDtypeStruct((SGU_CHUNK, LANES), F32),
                   jax.ShapeDtypeStruct((1, D_BRANCH), F32), jax.ShapeDtypeStruct((1, D_BRANCH), F32)],
        scratch_shapes=[pltpu.VMEM((SGU_CHUNK, D_BRANCH), F32)],
        compiler_params=_cparams(("arbitrary",)),
    )(proj, proj, proj, dyb, ln_g, ln_b, w_mask, w_mask_t, bias_full)


def _mid(proj, ya, yb, o, x, target, final_g, w_up_a, w_up_b, w_out):
    s = x.shape[0]
    tm = min(256, s)
    n_steps = s // tm
    half = D_MODEL // 2

    def body(ya_ref, yb_ref, o_ref, za_ref, ga0_ref, ga1_ref, gb0_ref, gb1_ref, x_ref, t_ref, gf_ref,
             wa_ref, wb_ref, wo_ref,
             dzg_ref, do_ref, dyb_ref, dx2_ref, gwo_ref, gwa_ref, gwb_ref, loss_ref, dgf_ref,
             acc_o, acc_a, acc_b):
        i = pl.program_id(0)

        @pl.when(i == 0)
        def _():
            acc_o[...] = jnp.zeros_like(acc_o)
            acc_a[...] = jnp.zeros_like(acc_a)
            acc_b[...] = jnp.zeros_like(acc_b)
            loss_ref[...] = jnp.zeros_like(loss_ref)
            dgf_ref[...] = jnp.zeros_like(dgf_ref)

        ya_v = ya_ref[...]
        yb_v = yb_ref[...]
        pa = jnp.dot(ya_v, wa_ref[...], preferred_element_type=F32)
        pb = jnp.dot(yb_v, wb_ref[...], preferred_element_type=F32)
        sa = _sigmoid(jnp.concatenate([ga0_ref[...], ga1_ref[...]], axis=1).astype(F32))
        sb = _sigmoid(jnp.concatenate([gb0_ref[...], gb1_ref[...]], axis=1).astype(F32))
        merged = (sa * pa + sb * pb).astype(BF16)
        x2 = x_ref[...] + jnp.dot(merged, wo_ref[...], preferred_element_type=F32)
        r2 = lax.rsqrt(jnp.mean(x2 * x2, axis=-1, keepdims=True) + EPS)
        xh = x2 * r2
        gf = gf_ref[...]
        diff = xh * gf - t_ref[...]
        loss_ref[...] += 0.5 * jnp.sum(jnp.mean(diff * diff, axis=-1, keepdims=True))
        dy = diff * (1.0 / D_MODEL)
        dgf_ref[...] += jnp.sum(dy * xh, axis=0, keepdims=True)
        dyg = dy * gf
        dx2 = r2 * (dyg - xh * jnp.mean(dyg * xh, axis=-1, keepdims=True))
        dx2_ref[...] = dx2
        dx2b = dx2.astype(BF16)
        dmerged = _dot_nt(dx2b, wo_ref[...])
        acc_o[...] += _dot_tn(merged, dx2b)
        dpa = dmerged * sa
        dpb = dmerged * sb
        dzg_ref[:, D_BRANCH:D_BRANCH + D_MODEL] = (dpa * pa * (1.0 - sa)).astype(BF16)
        dzg_ref[:, D_BRANCH + D_MODEL:D_BRANCH + 2 * D_MODEL] = (dpb * pb * (1.0 - sb)).astype(BF16)
        dpab = dpa.astype(BF16)
        dpbb = dpb.astype(BF16)
        acc_a[...] += _dot_tn(ya_v, dpab)
        acc_b[...] += _dot_tn(yb_v, dpbb)
        dya = _dot_nt(dpab, wa_ref[...])
        dyb_ref[...] = _dot_nt(dpbb, wb_ref[...]).astype(BF16)
        za = za_ref[...].astype(F32)
        sg = _sigmoid(za)
        do_ref[...] = (dya * (za * sg)).astype(BF16)
        dzg_ref[:, 0:D_BRANCH] = (dya * o_ref[...].astype(F32) * (sg * (1.0 + za * (1.0 - sg)))).astype(BF16)

        @pl.when(i == n_steps - 1)
        def _():
            gwo_ref[...] = acc_o[...].astype(BF16)
            gwa_ref[...] = acc_a[...].astype(BF16)
            gwb_ref[...] = acc_b[...].astype(BF16)

    tok = lambda w: pl.BlockSpec((tm, w), lambda i: (i, 0))
    col = lambda c: pl.BlockSpec((tm, half), lambda i: (i, c))
    full = lambda shape: pl.BlockSpec(shape, lambda i: (0,) * len(shape))
    return pl.pallas_call(
        body, name="mid", grid=(n_steps,),
        in_specs=[tok(D_BRANCH), tok(D_BRANCH), tok(D_BRANCH), col(COL_ZA), col(COL_GA), col(COL_GA + 1),
                  col(COL_GB), col(COL_GB + 1), tok(D_MODEL), tok(D_MODEL), full((1, D_MODEL)),
                  full((D_BRANCH, D_MODEL)), full((D_BRANCH, D_MODEL)), full((D_MODEL, D_MODEL))],
        out_specs=[tok(D_BRANCH + 2 * D_MODEL), tok(D_BRANCH), tok(D_BRANCH), tok(D_MODEL),
                   full((D_MODEL, D_MODEL)), full((D_BRANCH, D_MODEL)), full((D_BRANCH, D_MODEL)),
                   full((8, LANES)), full((1, D_MODEL))],
        out_shape=[jax.ShapeDtypeStruct((s, D_BRANCH + 2 * D_MODEL), BF16),
                   jax.ShapeDtypeStruct((s, D_BRANCH), BF16), jax.ShapeDtypeStruct((s, D_BRANCH), BF16),
                   jax.ShapeDtypeStruct((s, D_MODEL), F32),
                   jax.ShapeDtypeStruct((D_MODEL, D_MODEL), BF16),
                   jax.ShapeDtypeStruct((D_BRANCH, D_MODEL), BF16), jax.ShapeDtypeStruct((D_BRANCH, D_MODEL), BF16),
                   jax.ShapeDtypeStruct((8, LANES), F32), jax.ShapeDtypeStruct((1, D_MODEL), F32)],
        scratch_shapes=[pltpu.VMEM((D_MODEL, D_MODEL), F32), pltpu.VMEM((D_BRANCH, D_MODEL), F32),
                        pltpu.VMEM((D_BRANCH, D_MODEL), F32)],
        compiler_params=_cparams(("arbitrary",)),
    )(ya, yb, o, proj, proj, proj, proj, proj, x, target, final_g, w_up_a, w_up_b, w_out)


def _dwin_early(ht, first, tile_of_first, second, tile_of_second):
    s = ht.shape[1]
    n1 = first.shape[1] // D_BRANCH
    n2 = second.shape[1] // D_BRANCH

    def body(ht_ref, a_ref, b_ref, out_ref):
        j = pl.program_id(0)

        @pl.when(j < n1)
        def _():
            out_ref[...] = jnp.dot(ht_ref[...], a_ref[...], preferred_element_type=F32).astype(BF16)

        @pl.when(j >= n1)
        def _():
            out_ref[...] = jnp.dot(ht_ref[...], b_ref[...], preferred_element_type=F32).astype(BF16)

    return pl.pallas_call(
        body, name="dwin_early", grid=(n1 + n2,),
        in_specs=[pl.BlockSpec((D_MODEL, s), lambda j: (0, 0)),
                  pl.BlockSpec((s, D_BRANCH), lambda j: (0, jnp.minimum(j, n1 - 1))),
                  pl.BlockSpec((s, D_BRANCH), lambda j: (0, jnp.maximum(j - n1, 0)))],
        out_specs=pl.BlockSpec((D_MODEL, D_BRANCH),
                               lambda j: (0, jnp.where(j < n1, tile_of_first(j), tile_of_second(j - n1)))),
        out_shape=jax.ShapeDtypeStruct((D_MODEL, D_IN), BF16),
        compiler_params=_cparams(("arbitrary",)),
    )(ht, first, second)


def _dwin_pieces(ht, pieces, first_tile, prev):
    s = ht.shape[1]
    n = len(pieces)

    def body(ht_ref, *refs):
        srcs = refs[:n]
        out_ref, buf, sems = refs[n + 1:]
        j = pl.program_id(0)

        @pl.when(j == 0)
        def _():
            for k in range(n):
                pltpu.make_async_copy(srcs[k], buf.at[k], sems.at[k]).start()

        for k in range(n):
            @pl.when(j == k)
            def _():
                pltpu.make_async_copy(srcs[k], buf.at[k], sems.at[k]).wait()

        out_ref[...] = jnp.dot(ht_ref[...], buf[j], preferred_element_type=F32).astype(BF16)

    any_spec = pl.BlockSpec(memory_space=pl.ANY)
    return pl.pallas_call(
        body, name="dwin_pieces", grid=(n,),
        in_specs=[pl.BlockSpec((D_MODEL, s), lambda j: (0, 0))] + [any_spec] * (n + 1),
        out_specs=pl.BlockSpec((D_MODEL, D_BRANCH), lambda j: (0, first_tile + j)),
        out_shape=jax.ShapeDtypeStruct((D_MODEL, D_IN), BF16),
        scratch_shapes=[pltpu.VMEM((n, s, D_BRANCH), BF16), pltpu.SemaphoreType.DMA((n,))],
        input_output_aliases={n + 1: 0},
        compiler_params=_cparams(("arbitrary",)),
    )(ht, *pieces, prev)


def _dh_dx(pieces, w_in, x, norm_g, dx2, g_in):
    s = x.shape[0]
    tm = min(256, s)
    n_steps = s // tm
    arrays = []
    for arr, _, _, _ in pieces:
        if not any(arr is a for a in arrays):
            arrays.append(arr)
    n_arr = len(arrays)
    plan = [([k for k, a in enumerate(arrays) if a is arr][0], wcol, off, width) for arr, wcol, off, width in pieces]

    def body(*refs):
        p_refs = refs[:n_arr]
        w_ref, x_ref, g_ref, dx2_ref, gin_ref, dx_ref, dg_ref, late_ref = refs[n_arr:n_arr + 8]
        step = pl.program_id(0)
        finish = _presum_program(step == 0, step == min(1, n_steps - 1), step == n_steps - 1, 0, _LATE_CHIPS,
                                 gin_ref, late_ref, refs[n_arr + 8:])

        @pl.when(step == 0)
        def _():
            dg_ref[...] = jnp.zeros_like(dg_ref)

        dh = None
        for k, wcol, off, width in plan:
            d = _dot_nt(p_refs[k][:, off:off + width], w_ref[:, wcol:wcol + width])
            dh = d if dh is None else dh + d
        xf = x_ref[...]
        r = lax.rsqrt(jnp.mean(xf * xf, axis=-1, keepdims=True) + EPS)
        xh = xf * r
        dg_ref[...] += jnp.sum(dh * xh, axis=0, keepdims=True)
        dhg = dh * g_ref[...]
        dx_ref[...] = r * (dhg - xh * jnp.mean(dhg * xh, axis=-1, keepdims=True)) + dx2_ref[...]

        @pl.when(step == n_steps - 1)
        def _():
            finish()

    tok = lambda w: pl.BlockSpec((tm, w), lambda i: (i, 0))
    full = lambda shape: pl.BlockSpec(shape, lambda i: (0,) * len(shape))
    any_spec = pl.BlockSpec(memory_space=pl.ANY)
    return pl.pallas_call(
        body, name="dh_dx", grid=(n_steps,),
        in_specs=[tok(a.shape[1]) for a in arrays] + [full((D_MODEL, D_IN)), tok(D_MODEL), full((1, D_MODEL)),
                                                      tok(D_MODEL), any_spec],
        out_specs=[tok(D_MODEL), full((1, D_MODEL)), any_spec],
        out_shape=[jax.ShapeDtypeStruct((s, D_MODEL), F32), jax.ShapeDtypeStruct((1, D_MODEL), F32),
                   jax.ShapeDtypeStruct((N_DEV,) + _GRAD_PIECE_SHAPES[0], BF16)],
        scratch_shapes=_presum_scratch(0, _LATE_CHIPS),
        compiler_params=pltpu.CompilerParams(dimension_semantics=("arbitrary",), vmem_limit_bytes=VMEM_LIMIT,
                                             has_side_effects=True),
    )(*arrays, w_in, x, norm_g, dx2, g_in)


ADAM_GRID = 8


def _adamw(sets):
    n = len(sets)
    c1 = 1.0 - ADAM_B1 ** ADAM_STEP
    c2 = 1.0 - ADAM_B2 ** ADAM_STEP

    def body(*refs):
        for k in range(n):
            w_ref, g_ref, m_ref, v_ref = refs[4 * k:4 * k + 4]
            g_out_ref, d_ref, nm_ref, nv_ref = refs[4 * n + 4 * k:4 * n + 4 * k + 4]
            gv = g_ref[...]
            g_out_ref[...] = gv
            nm = ADAM_B1 * m_ref[...] + (1.0 - ADAM_B1) * gv
            nv = ADAM_B2 * v_ref[...] + (1.0 - ADAM_B2) * (gv * gv)
            d_ref[...] = -ADAM_LR * ((nm / c1) / (jnp.sqrt(nv / c2) + ADAM_EPS) + ADAM_WD * w_ref[...])
            nm_ref[...] = nm
            nv_ref[...] = nv

    specs, shapes, args = [], [], []
    for w, g, m, v in sets:
        rows, cols = w.shape
        assert rows % (8 * ADAM_GRID) == 0, (rows, cols)
        specs += [pl.BlockSpec((rows // ADAM_GRID, cols), lambda i: (i, 0))] * 4
        shapes += [jax.ShapeDtypeStruct((rows, cols), F32)] * 4
        args += [w, g, m, v]
    outs = pl.pallas_call(
        body, name="adamw", grid=(ADAM_GRID,),
        in_specs=specs, out_specs=specs, out_shape=shapes,
        compiler_params=_cparams(("parallel",)),
    )(*args)
    return [outs[4 * k:4 * k + 4] for k in range(n)]


def _reduce_grads_tail(grads, g_small, early_slots, late_in_slots):
    n_big = len(grads)
    n_arr = n_big + 1
    shard_shapes = [(2 * r, w) for r, w in _GRAD_PIECE_SHAPES]
    small_piece = (SMALL_PIECE, LANES)

    def body(*refs):
        src = refs[:n_arr]
        early = refs[n_arr:n_arr + n_big]
        late_in = refs[n_arr + n_big]
        n_in = n_arr + n_big + 1
        out = refs[n_in:n_in + n_arr]
        slots = refs[n_in + n_arr:n_in + 2 * n_arr]
        sums = refs[n_in + 2 * n_arr:n_in + 3 * n_arr]
        send1, recv1, send2, recv2, local_sems = refs[n_in + 3 * n_arr:]
        x, y, c = _place()
        me = 4 * x + 2 * y + c

        def piece_of(a, dev):
            return src[a].at[dev] if a == n_big else _grad_piece(src[a], a, dev)

        def late(a, dst_dev, src_dev):
            return pltpu.make_async_remote_copy(
                src_ref=piece_of(a, dst_dev), dst_ref=slots[a].at[src_dev],
                send_sem=send1.at[n_arr * dst_dev + a], recv_sem=recv1.at[n_arr * src_dev + a],
                device_id=_dev_id(dst_dev), device_id_type=MESH)

        def late_arrays(dev):
            return (n_big,)

        def load(a, dev, received):
            return pltpu.make_async_copy(received.at[dev], slots[a].at[dev], local_sems.at[n_arr * dev + a])

        def own(a, dev):
            return pltpu.make_async_copy(piece_of(a, dev), slots[a].at[dev], local_sems.at[n_arr * dev + a])

        for dev in range(N_DEV):
            @pl.when(me == dev)
            def _():
                received = [early[0] if dev in _EARLY_IN_DEVS else late_in] + list(early[1:])
                for a in range(n_arr):
                    own(a, dev).start()
                sources = sorted([dev] + _presum_sources(dev))
                for peer in range(N_DEV):
                    if peer != dev:
                        for a in late_arrays(peer):
                            late(a, peer, dev).start()
                        for a in range(n_big):
                            if peer in sources:
                                load(a, peer, received[a]).start()
                for a in range(n_arr):
                    own(a, dev).wait()
                for peer in range(N_DEV):
                    if peer != dev:
                        for a in late_arrays(dev):
                            late(a, dev, peer).wait_recv()
                        for a in range(n_big):
                            if peer in sources:
                                load(a, peer, received[a]).wait()
                for a in range(n_arr):
                    rows = slots[a].shape[1]
                    step = 64 if rows % 64 == 0 else 8
                    used = sources if a < n_big else list(range(N_DEV))

                    def add_rows(t, carry, a=a, step=step, used=used):
                        r0 = pl.multiple_of(t * step, step)
                        total = slots[a][used[0], pl.ds(r0, step), :].astype(F32)
                        for src_dev in used[1:]:
                            total = total + slots[a][src_dev, pl.ds(r0, step), :].astype(F32)
                        sums[a][pl.ds(r0, step), :] = total
                        return carry

                    lax.fori_loop(0, rows // step, add_rows, 0)

        shares = []
        keeps = []
        for a in range(n_big):
            r, w = _GRAD_PIECE_SHAPES[a]
            dst = out[a].at[pl.ds(pl.multiple_of(c * r, 8), r), :]
            cp = pltpu.make_async_remote_copy(src_ref=sums[a], dst_ref=dst, send_sem=send2.at[a], recv_sem=recv2.at[a],
                                              device_id=(x, y, 1 - c), device_id_type=MESH)
            cp.start()
            shares.append(cp)
            kp = pltpu.make_async_copy(sums[a], dst, local_sems.at[N_DEV * n_arr + a])
            kp.start()
            keeps.append(kp)
        kp = pltpu.make_async_copy(sums[n_big], out[n_big].at[me], local_sems.at[N_DEV * n_arr + n_big])
        kp.start()
        keeps.append(kp)

        def small_share(dst_dev, src_dev):
            return pltpu.make_async_remote_copy(src_ref=sums[n_big], dst_ref=out[n_big].at[src_dev],
                                                send_sem=send2.at[n_big + dst_dev], recv_sem=recv2.at[n_big + src_dev],
                                                device_id=_dev_id(dst_dev), device_id_type=MESH)

        for dev in range(N_DEV):
            @pl.when(me != dev)
            def _():
                small_share(dev, me).start()
        for a in range(n_big):
            r, w = _GRAD_PIECE_SHAPES[a]
            other = out[a].at[pl.ds(pl.multiple_of((1 - c) * r, 8), r), :]
            pltpu.make_async_remote_copy(src_ref=sums[a], dst_ref=other, send_sem=send2.at[a], recv_sem=recv2.at[a],
                                         device_id=(x, y, 1 - c), device_id_type=MESH).wait_recv()
        for dev in range(N_DEV):
            @pl.when(me != dev)
            def _():
                small_share(dev, dev).wait_recv()
                small_share(dev, me).wait_send()
                for a in late_arrays(dev):
                    late(a, dev, me).wait_send()
        for cp in shares:
            cp.wait_send()
        for kp in keeps:
            kp.wait()

    any_spec = pl.BlockSpec(memory_space=pl.ANY)
    return pl.pallas_call(
        body, name="reduce_grads_tail",
        in_specs=[any_spec] * (n_arr + n_big + 1), out_specs=[any_spec] * n_arr,
        out_shape=[jax.ShapeDtypeStruct(sh, F32) for sh in shard_shapes]
        + [jax.ShapeDtypeStruct((N_DEV,) + small_piece, F32)],
        scratch_shapes=[pltpu.VMEM((N_DEV,) + sh, BF16) for sh in _GRAD_PIECE_SHAPES]
        + [pltpu.VMEM((N_DEV,) + small_piece, F32)]
        + [pltpu.VMEM(sh, F32) for sh in _GRAD_PIECE_SHAPES] + [pltpu.VMEM(small_piece, F32)]
        + [pltpu.SemaphoreType.DMA((N_DEV * n_arr,)), pltpu.SemaphoreType.DMA((N_DEV * n_arr,)),
           pltpu.SemaphoreType.DMA((n_big + N_DEV,)), pltpu.SemaphoreType.DMA((n_big + N_DEV,)),
           pltpu.SemaphoreType.DMA((N_DEV * n_arr + n_arr,))],
        compiler_params=pltpu.CompilerParams(vmem_limit_bytes=VMEM_LIMIT, has_side_effects=True),
    )(*grads, g_small, *early_slots, late_in_slots)


_SMALL_PARTS = (("norm_g", 8), ("sgu_ln_g", 8), ("sgu_ln_b", 8), ("w_spatial", 1024), ("b_spatial", 8),
                ("final_norm_g", 8))
_LOSS_ROW = sum(n for _, n in _SMALL_PARTS)


def _pack_small(parts, loss_tile=None):
    rows = []
    for name, n_rows in _SMALL_PARTS:
        a = parts[name].reshape(-1, LANES).astype(F32)
        a = jnp.pad(a, ((0, n_rows - a.shape[0]), (0, 0)))
        rows.append(a)
    rows.append(jnp.zeros((8, LANES), F32) if loss_tile is None else loss_tile)
    rows.append(jnp.zeros((SMALL_ROWS - _LOSS_ROW - 8, LANES), F32))
    return jnp.concatenate(rows, axis=0)


def _unpack_small(packed, shapes):
    out = {}
    r0 = 0
    for name, n_rows in _SMALL_PARTS:
        n = math.prod(shapes[name])
        out[name] = packed[r0:r0 + n // LANES].reshape(shapes[name])
        r0 += n_rows
    return out


def _local_step(proj, ht, x, target, norm_g, w_in, sgu_ln_g, sgu_ln_b, w_spatial, b_spatial, w_up_a, w_up_b, w_out,
                final_norm_g, bq, bk):
    pos = jnp.arange(SGU_CHUNK)
    keep = (pos[None, :] // SGU_SUBCHUNK) <= (pos[:, None] // SGU_SUBCHUNK)
    w_mask = jnp.where(keep[None], w_spatial, 0.0).astype(BF16)
    w_mask_t = jnp.swapaxes(w_mask, 1, 2)
    bias_full = jnp.repeat(b_spatial.T, GROUP_DIM, axis=1)
    ln_g = sgu_ln_g.reshape(1, D_BRANCH)
    ln_b = sgu_ln_b.reshape(1, D_BRANCH)
    final_g = final_norm_g.reshape(1, D_MODEL)

    o, ya, rsave = _attn_fwd(proj, bq, bk, ATTN_PAIRS)
    yb = _sgu_fwd(proj, ln_g, ln_b, w_mask, bias_full)
    dzg, do, dyb, dx2, g_out, g_up_a, g_up_b, loss_acc, d_final = _mid(
        proj, ya, yb, o, x, target, final_g, w_up_a, w_up_b, w_out)
    dsgu, d_wsp, d_bsp, d_lng, d_lnb = _sgu_bwd(proj, dyb, ln_g, ln_b, w_mask, w_mask_t, bias_full)
    g_in = _dwin_early(ht, dzg, lambda j: jnp.where(j == 0, COL_ZA, COL_GA - 1 + j), dsgu, lambda j: COL_UB + j)
    dq, dk, dv, *early_slots = _attn_bwd(proj, do, rsave, bq, bk, ATTN_PAIRS, (g_in, g_up_a, g_up_b, g_out))
    g_in = _dwin_pieces(ht, (dq, dk, dv), COL_Q, g_in)
    pieces = [(dq, COL_Q * D_BRANCH, 0, D_BRANCH), (dk, COL_K * D_BRANCH, 0, D_BRANCH),
              (dv, COL_V * D_BRANCH, 0, D_BRANCH), (dzg, COL_ZA * D_BRANCH, 0, D_BRANCH),
              (dsgu, COL_UB * D_BRANCH, 0, 3 * D_BRANCH), (dzg, COL_GA * D_BRANCH, D_BRANCH, 2 * D_MODEL)]
    dx, d_norm, late_in_slots = _dh_dx(pieces, w_in, x, norm_g, dx2, g_in)
    small = {"norm_g": d_norm, "sgu_ln_g": d_lng, "sgu_ln_b": d_lnb, "w_spatial": d_wsp,
             "b_spatial": d_bsp[:, :N_GROUPS].T, "final_norm_g": d_final}
    return loss_acc, dx, (g_in, g_up_a, g_up_b, g_out), small, early_slots, late_in_slots


def kernel(x, norm_g, w_in, sgu_ln_g, sgu_ln_b, w_spatial, b_spatial, w_up_a, w_up_b, w_out, final_norm_g, loss_target, m_norm_g, m_w_in, m_sgu_ln_g, m_sgu_ln_b, m_w_spatial, m_b_spatial, m_w_up_a, m_w_up_b, m_w_out, m_final_norm_g, v_norm_g, v_w_in, v_sgu_ln_g, v_sgu_ln_b, v_w_spatial, v_b_spatial, v_w_up_a, v_w_up_b, v_w_out, v_final_norm_g):
    big_names = ("w_in", "w_up_a", "w_up_b", "w_out")
    small_names = tuple(n for n, _ in _SMALL_PARTS)
    names = ("norm_g", "w_in", "sgu_ln_g", "sgu_ln_b", "w_spatial", "b_spatial", "w_up_a", "w_up_b", "w_out",
             "final_norm_g")
    w = dict(norm_g=norm_g, w_in=w_in, sgu_ln_g=sgu_ln_g, sgu_ln_b=sgu_ln_b, w_spatial=w_spatial,
             b_spatial=b_spatial, w_up_a=w_up_a, w_up_b=w_up_b, w_out=w_out, final_norm_g=final_norm_g)
    m = dict(norm_g=m_norm_g, w_in=m_w_in, sgu_ln_g=m_sgu_ln_g, sgu_ln_b=m_sgu_ln_b, w_spatial=m_w_spatial,
             b_spatial=m_b_spatial, w_up_a=m_w_up_a, w_up_b=m_w_up_b, w_out=m_w_out, final_norm_g=m_final_norm_g)
    v = dict(norm_g=v_norm_g, w_in=v_w_in, sgu_ln_g=v_sgu_ln_g, sgu_ln_b=v_sgu_ln_b, w_spatial=v_w_spatial,
             b_spatial=v_b_spatial, w_up_a=v_w_up_a, w_up_b=v_w_up_b, w_out=v_w_out, final_norm_g=v_final_norm_g)
    shapes = {n: w[n].shape for n in names}
    flat2d = lambda a: a.reshape(a.shape[-2:])

    proj, ht, *full = _in_proj_gather(x[0], norm_g, *[flat2d(w[n]) for n in big_names])
    loss, dx, big_grads, small, early_slots, late_in_slots = _local_step(
        proj, ht, x[0], loss_target[0], norm_g, full[0], sgu_ln_g[0], sgu_ln_b[0], w_spatial[0], b_spatial[0],
        full[1], full[2], full[3], final_norm_g, ATTN_Q_BLOCK, ATTN_K_BLOCK)
    packed = _pack_small(small, loss).reshape(N_DEV, SMALL_PIECE, LANES)
    red = _reduce_grads_tail(big_grads, packed, early_slots, late_in_slots)

    grads, deltas, new_m, new_v = {}, {}, {}, {}
    sets = [(flat2d(w[n]), g, flat2d(m[n]), flat2d(v[n])) for n, g in zip(big_names, red[:4])]
    sets.append((_pack_small({n: w[n] for n in small_names}), red[4].reshape(SMALL_ROWS, LANES),
                 _pack_small({n: m[n] for n in small_names}), _pack_small({n: v[n] for n in small_names})))
    updated = _adamw(sets)
    for n, (g, d, nm, nv) in zip(big_names, updated[:4]):
        grads[n], deltas[n], new_m[n], new_v[n] = (a.reshape(shapes[n]) for a in (g, d, nm, nv))
    g_small, d, nm, nv = updated[4]
    for src, dst in ((g_small, grads), (d, deltas), (nm, new_m), (nv, new_v)):
        dst.update(_unpack_small(src, shapes))

    return (g_small[_LOSS_ROW, 0], dx[None], *[grads[n] for n in names], *[deltas[n] for n in names],
            *[new_m[n] for n in names], *[new_v[n] for n in names])
```

```python
import math

import jax
import jax.numpy as jnp
from jax import lax
from jax.experimental import pallas as pl
from jax.experimental.pallas import tpu as pltpu

F32 = jnp.float32
BF16 = jnp.bfloat16

D_MODEL = 1024
N_HEADS = 8
HEAD_DIM = 64
D_BRANCH = 512
D_IN = 4 * D_BRANCH + 3 * D_BRANCH + 2 * D_MODEL
N_GROUPS = 8
GROUP_DIM = 64
SGU_CHUNK = 128
SGU_SUBCHUNK = 64
GROUP_SHIFT = 6
EPS = 1e-6
LANES = 128
ATTN_Q_BLOCK = 256
ATTN_K_BLOCK = 256
DEAD = -110.0
SKIPPED = -1e30
SCAN_PASSES = 1
ATTN_PAIRS = 2
N_CHIPS = 4
N_DEV = 8
MESH = pl.DeviceIdType.MESH

ADAM_LR = 0.001
ADAM_B1 = 0.9
ADAM_B2 = 0.999
ADAM_EPS = 1e-08
ADAM_WD = 0.01
ADAM_STEP = 10

COL_Q, COL_K, COL_V, COL_ZA, COL_UB, COL_VB, COL_ZB, COL_GA, COL_GB = 0, 1, 2, 3, 4, 5, 6, 7, 9

VMEM_LIMIT = 56 * 1024 * 1024

SMALL_ROWS = 1088
SMALL_PIECE = SMALL_ROWS // N_DEV


def _cparams(sem=None):
    return pltpu.CompilerParams(dimension_semantics=sem, vmem_limit_bytes=VMEM_LIMIT)


def _aligned(v, m):
    return v if isinstance(v, int) else pl.multiple_of(v, m)


def _sigmoid(x):
    return 1.0 / (1.0 + jnp.exp(-x))


def _gelu_and_grad(x):
    k = math.sqrt(2.0 / math.pi)
    x2 = x * x
    inner = k * (x + 0.044715 * x * x2)
    th = jnp.tanh(inner)
    g = 0.5 * x * (1.0 + th)
    dg = 0.5 * (1.0 + th) + 0.5 * x * (1.0 - th * th) * (k * (1.0 + 3.0 * 0.044715 * x2))
    return g, dg


def _split_dot(a, b_bf16, passes):
    out = None
    rem = a
    for _ in range(passes):
        part = rem.astype(BF16)
        d = jnp.dot(part, b_bf16, preferred_element_type=F32)
        out = d if out is None else out + d
        rem = rem - part.astype(F32)
    return out


def _dot_nt(a, b):
    return lax.dot_general(a, b, (((1,), (1,)), ((), ())), preferred_element_type=F32)


def _dot_tn(a, b):
    return lax.dot_general(a, b, (((0,), (0,)), ((), ())), preferred_element_type=F32)


def _place():
    x, y, c = lax.axis_index("x"), lax.axis_index("y"), lax.axis_index("c")
    return x, y, c


def _in_proj_gather(x, norm_g, w_in, w_up_a, w_up_b, w_out):
    s = x.shape[0]
    tm = min(1024, s)
    nt = s // tm
    shards = (w_in, w_up_a, w_up_b, w_out)
    n_arr = len(shards)
    col_sharded = (True, True, True, False)
    full_shapes = ((D_MODEL, D_IN), (D_BRANCH, D_MODEL), (D_BRANCH, D_MODEL), (D_MODEL, D_MODEL))
    w_shard = w_in.shape[1]
    half_rows = D_MODEL // 2
    stage_rows = 256

    def body(order_ref, x_ref, g_ref, *refs):
        src = refs[:n_arr]
        proj_ref, ht_ref = refs[n_arr:n_arr + 2]
        out = refs[n_arr + 2:2 * n_arr + 2]
        wsc, h_scr, stage = refs[2 * n_arr + 2:2 * n_arr + 5]
        small_stage = refs[2 * n_arr + 5:2 * n_arr + 8]
        small_cast = refs[2 * n_arr + 8:2 * n_arr + 11]
        send_sems, recv_sems, local_sems = refs[2 * n_arr + 11:]
        k = pl.program_id(0)
        i = pl.program_id(1)
        x_, y_, c = _place()
        chip = 2 * x_ + y_
        sibling = (x_, y_, 1 - c)
        others = [(x_, 1 - y_), (1 - x_, y_), (1 - x_, 1 - y_)]

        def region(a, chip_idx, half):
            if a == 0:
                return wsc.at[chip_idx, pl.ds(_aligned(half * half_rows, 16), half_rows), :]
            r, w = shards[a].shape
            hr = r // 2
            if col_sharded[a]:
                return out[a].at[pl.ds(_aligned(half * hr, 16), hr), pl.ds(_aligned(chip_idx * w, LANES), w)]
            return out[a].at[pl.ds(_aligned(chip_idx * r + half * hr, 16), hr), :]

        def remote(kk, a, chip_idx, half, to, own):
            s_ref = region(a, chip_idx, half)
            if own and a > 0:
                hr = shards[a].shape[0] // 2
                s_ref = small_cast[a - 1].at[pl.ds(_aligned(half * hr, 16), hr), :]
            return pltpu.make_async_remote_copy(src_ref=s_ref, dst_ref=region(a, chip_idx, half),
                                                send_sem=send_sems.at[kk], recv_sem=recv_sems.at[kk],
                                                device_id=to, device_id_type=MESH)

        def keep_whole(kk, chip_idx):
            return pltpu.make_async_copy(wsc.at[chip_idx],
                                         out[0].at[:, pl.ds(_aligned(chip_idx * w_shard, LANES), w_shard)],
                                         local_sems.at[kk])

        def small_stores():
            cps = []
            for a in range(1, n_arr):
                hr = shards[a].shape[0] // 2
                for half in range(2):
                    cps.append(pltpu.make_async_copy(small_cast[a - 1].at[pl.ds(half * hr, hr), :],
                                                     region(a, chip, half), local_sems.at[4 + 2 * (a - 1) + half]))
            return cps

        def arrive_and_pass(j):
            ochip = chip ^ j
            for a in range(n_arr):
                kk = n_arr * (j - 1) + a
                remote(kk, a, ochip, c, sibling, False).wait_recv()
                remote(3 * n_arr + kk, a, ochip, c, sibling, False).start()

        def from_sibling(j, a):
            remote(3 * n_arr + n_arr * (j - 1) + a, a, chip ^ j, 1 - c, sibling, False).wait_recv()

        @pl.when((k == 0) & (i == 0))
        def _():
            def cast_rows(half):
                for t in range(half_rows // stage_rows):
                    r0 = pl.multiple_of(half * half_rows + t * stage_rows, stage_rows)
                    pltpu.sync_copy(src[0].at[pl.ds(r0, stage_rows), :], stage)
                    wsc[chip, pl.ds(r0, stage_rows), :] = stage[...].astype(BF16)

            cast_rows(c)
            for j in (1, 2):
                remote(n_arr * (j - 1), 0, chip, c, (*others[j - 1], c), True).start()
            cast_rows(1 - c)
            for a in range(1, n_arr):
                pltpu.sync_copy(src[a], small_stage[a - 1])
                small_cast[a - 1][...] = small_stage[a - 1][...].astype(BF16)
            for j in (1, 2):
                for a in range(1, n_arr):
                    remote(n_arr * (j - 1) + a, a, chip, c, (*others[j - 1], c), True).start()
            keep_whole(0, chip).start()
            for cp in small_stores():
                cp.start()

        @pl.when((k == 1) & (i == 0))
        def _():
            for j in (1, 2):
                remote(n_arr * (j - 1), 0, chip, c, (*others[j - 1], c), True).wait_send()
            for a in range(n_arr):
                remote(n_arr * 2 + a, a, chip, c, (*others[2], c), True).start()
            arrive_and_pass(1)
            arrive_and_pass(2)
            from_sibling(1, 0)
            keep_whole(1, chip ^ 1).start()

        @pl.when((k == 2) & (i == 0))
        def _():
            from_sibling(2, 0)
            keep_whole(2, chip ^ 2).start()
            arrive_and_pass(3)

        @pl.when((k == 3) & (i == 0))
        def _():
            from_sibling(3, 0)
            keep_whole(3, chip ^ 3).start()

        @pl.when(k == 0)
        def _():
            xf = x_ref[...]
            r = lax.rsqrt(jnp.mean(xf * xf, axis=-1, keepdims=True) + EPS)
            h = xf * r * g_ref[...]
            h_scr[i] = h.astype(BF16)
            ht_ref[...] = h.T.astype(BF16)

        proj_ref[...] = jnp.dot(h_scr[i], wsc[order_ref[k]], preferred_element_type=F32).astype(BF16)

        @pl.when((k == 3) & (i == nt - 1))
        def _():
            for j in (1, 2, 3):
                for a in range(1, n_arr):
                    from_sibling(j, a)
            for j in (1, 2, 3):
                for a in range(n_arr):
                    kk = n_arr * (j - 1) + a
                    if a > 0 or j == 3:
                        remote(kk, a, chip, c, (*others[j - 1], c), True).wait_send()
                    remote(3 * n_arr + kk, a, chip ^ j, c, sibling, False).wait_send()
            for kk in range(4):
                keep_whole(kk, chip ^ kk).wait()
            for cp in small_stores():
                cp.wait()

    any_spec = pl.BlockSpec(memory_space=pl.ANY)
    tile = lambda kk, ii: jnp.where(kk == 0, ii, nt - 1)
    grid_spec = pltpu.PrefetchScalarGridSpec(
        num_scalar_prefetch=1, grid=(N_CHIPS, nt),
        in_specs=[pl.BlockSpec((tm, D_MODEL), lambda kk, ii, order: (tile(kk, ii), 0)),
                  pl.BlockSpec((1, D_MODEL), lambda kk, ii, order: (0, 0))] + [any_spec] * n_arr,
        out_specs=[pl.BlockSpec((tm, w_shard), lambda kk, ii, order: (ii, order[kk])),
                   pl.BlockSpec((D_MODEL, tm), lambda kk, ii, order: (0, tile(kk, ii)))] + [any_spec] * n_arr,
        scratch_shapes=[pltpu.VMEM((N_CHIPS, D_MODEL, w_shard), BF16), pltpu.VMEM((nt, tm, D_MODEL), BF16),
                        pltpu.VMEM((stage_rows, w_shard), F32)]
        + [pltpu.VMEM(a.shape, F32) for a in shards[1:]] + [pltpu.VMEM(a.shape, BF16) for a in shards[1:]]
        + [pltpu.SemaphoreType.DMA((6 * n_arr,)), pltpu.SemaphoreType.DMA((6 * n_arr,)),
           pltpu.SemaphoreType.DMA((4 + 2 * (n_arr - 1),))])
    x_, y_, _ = _place()
    order = (2 * x_ + y_) ^ jnp.arange(N_CHIPS, dtype=jnp.int32)
    return pl.pallas_call(
        body, name="in_proj_gather", grid_spec=grid_spec,
        out_shape=[jax.ShapeDtypeStruct((s, D_IN), BF16), jax.ShapeDtypeStruct((D_MODEL, s), BF16)]
        + [jax.ShapeDtypeStruct(sh, BF16) for sh in full_shapes],
        compiler_params=pltpu.CompilerParams(dimension_semantics=("arbitrary", "arbitrary"),
                                             vmem_limit_bytes=VMEM_LIMIT, has_side_effects=True),
    )(order, x, norm_g, *shards)


def _neg_softplus_parts(z):
    zb = z.astype(BF16)
    p = jnp.exp(-jnp.abs(zb))
    return p, jnp.maximum(zb, jnp.zeros_like(zb)) + jnp.log(1.0 + p)


def _split_cat(a, passes):
    parts = []
    rem = a
    for k in range(passes):
        part = rem.astype(BF16)
        parts.append(part)
        if k + 1 < passes:
            rem = rem - part.astype(F32)
    return parts[0] if passes == 1 else jnp.concatenate(parts, axis=1)


def _tri(blk, upper, sign):
    row = lax.broadcasted_iota(jnp.int32, (blk, blk), 0)
    col = lax.broadcasted_iota(jnp.int32, (blk, blk), 1)
    keep = (row <= col) if upper else (row >= col)
    t = jnp.where(keep, sign, 0.0).astype(BF16)
    return t if SCAN_PASSES == 1 else jnp.concatenate([t] * SCAN_PASSES, axis=0)


def _attn_fwd(proj, bq, bk, npairs):
    s = proj.shape[0]
    nq = s // bq
    ratio = bq // bk
    scale = HEAD_DIM ** -0.5
    heads = tuple(range(2 * npairs))
    width = LANES * npairs

    def body(q_ref, k_ref, v_ref, za_ref, o_ref, ya_ref, rs_ref, acc_ref, r_ref):
        i = pl.program_id(1)
        lane = lax.broadcasted_iota(jnp.int32, (bq, LANES), 1)
        lo_half = lane < HEAD_DIM
        qm = []
        for pr in range(npairs):
            q = q_ref[:, LANES * pr:LANES * (pr + 1)] * jnp.asarray(scale, BF16)
            zero = jnp.zeros_like(q)
            qm += [jnp.where(lo_half, q, zero), jnp.where(lo_half, zero, q)]
        row = lax.broadcasted_iota(jnp.int32, (bq, bk), 0)
        col = lax.broadcasted_iota(jnp.int32, (bq, bk), 1)
        tneg = _tri(bk, False, -1.0)
        acc_ref[...] = jnp.zeros_like(acc_ref)
        r_ref[...] = jnp.zeros_like(r_ref)
        rs_ref[...] = jnp.full_like(rs_ref, SKIPPED)

        def scores(j):
            ks = pl.multiple_of(j * bk, bk)
            return [_dot_nt(qm[h], k_ref[pl.ds(ks, bk), LANES * (h // 2):LANES * (h // 2 + 1)]) for h in heads]

        def block(j, diag, valid=None):
            ks = pl.multiple_of(j * bk, bk)
            vj = [v_ref[pl.ds(ks, bk), LANES * pr:LANES * (pr + 1)] for pr in range(npairs)]
            if diag:
                before = (j * bk + col) < (i * bq + row)
            z = scores(j)
            sp = [_neg_softplus_parts(z[h])[1] for h in heads]
            if diag:
                sp = [jnp.where(before, sp[h], 0.0) for h in heads]
            cin = [jnp.dot(_split_cat(sp[h], SCAN_PASSES), tneg, preferred_element_type=F32) for h in heads]
            w = [jnp.exp(z[h] + cin[h]) for h in heads]
            if diag:
                w = [jnp.where(before, w[h], 0.0) for h in heads]
            pv = [jnp.dot(w[h].astype(BF16), vj[h // 2], preferred_element_type=F32) for h in heads]
            r = [r_ref[h] for h in heads]
            keep = 1.0 if valid is None else valid.astype(F32)
            for h in heads:
                acc_ref[h] += pv[h] * (jnp.exp(r[h]) * keep)
                r_ref[h] = r[h] + cin[h][:, 0:1] * keep
            for pr in range(npairs):
                hit = [lane == j, lane == j + HEAD_DIM]
                if valid is not None:
                    hit = [m & valid for m in hit]
                rs_ref[pr] = jnp.where(hit[0], r[2 * pr], jnp.where(hit[1], r[2 * pr + 1], rs_ref[pr]))

        for t in range(ratio):
            block(i * ratio + ratio - 1 - t, True)
        block(jnp.maximum(i * ratio - 1, 0), False, valid=i > 0)

        def alive(carry):
            jj, r_max = carry
            return (jj < i * ratio - 1) & (r_max > DEAD)

        def loop_body(carry):
            jj, _ = carry
            block(i * ratio - 2 - jj, False)
            return jj + 1, jnp.max(r_ref[...])

        lax.while_loop(alive, loop_body, (0, jnp.max(r_ref[...])))
        for pr in range(npairs):
            cols = slice(LANES * pr, LANES * (pr + 1))
            o = jnp.where(lo_half, acc_ref[2 * pr], acc_ref[2 * pr + 1])
            o_ref[:, cols] = o.astype(BF16)
            za = za_ref[:, cols].astype(F32)
            ya_ref[:, cols] = (o * (za * _sigmoid(za))).astype(BF16)

    n_steps = N_HEADS // (2 * npairs)
    return pl.pallas_call(
        body, name="attn_fwd", grid=(n_steps, nq),
        in_specs=[pl.BlockSpec((bq, width), lambda p, i: (i, n_steps * COL_Q + p)),
                  pl.BlockSpec((s, width), lambda p, i: (0, n_steps * COL_K + p)),
                  pl.BlockSpec((s, width), lambda p, i: (0, n_steps * COL_V + p)),
                  pl.BlockSpec((bq, width), lambda p, i: (i, n_steps * COL_ZA + p))],
        out_specs=[pl.BlockSpec((bq, width), lambda p, i: (i, p)),
                   pl.BlockSpec((bq, width), lambda p, i: (i, p)),
                   pl.BlockSpec((npairs, bq, LANES), lambda p, i: (p, i, 0))],
        out_shape=[jax.ShapeDtypeStruct((s, D_BRANCH), BF16), jax.ShapeDtypeStruct((s, D_BRANCH), BF16),
                   jax.ShapeDtypeStruct((N_HEADS // 2, s, LANES), F32)],
        scratch_shapes=[pltpu.VMEM((2 * npairs, bq, LANES), F32), pltpu.VMEM((2 * npairs, bq, 1), F32)],
        compiler_params=_cparams(("parallel", "parallel")),
    )(proj, proj, proj, proj)


_GRAD_COL_SHARDED = (True, True, True, False)
_GRAD_FULL_SHAPES = ((D_MODEL, D_IN), (D_BRANCH, D_MODEL), (D_BRANCH, D_MODEL), (D_MODEL, D_MODEL))
_GRAD_PIECE_SHAPES = tuple((r // 2, w // N_CHIPS) if cs else (r // (2 * N_CHIPS), w)
                           for (r, w), cs in zip(_GRAD_FULL_SHAPES, _GRAD_COL_SHARDED))
_EARLY_IN_DEVS = (4, 5, 6, 7)
_LATE_IN_DEVS = (0, 1, 2, 3)
_LATE_CHIPS = (0, 1)
_EARLY_CHIPS = (2, 3)
_ALL_CHIPS = (0, 1, 2, 3)


def _grad_piece(ref, a, dev):
    r, w = _GRAD_PIECE_SHAPES[a]
    if _GRAD_COL_SHARDED[a]:
        return ref.at[pl.ds((dev % 2) * r, r), pl.ds((dev // 2) * w, w)]
    return ref.at[pl.ds(dev * r, r), :]


def _dev_id(dev):
    return (dev // 4, (dev // 2) % 2, dev % 2)


def _me():
    return 4 * lax.axis_index("x") + 2 * lax.axis_index("y") + lax.axis_index("c")


def _presum_copy(src, dst, send_sem, recv_sem, to_dev):
    return pltpu.make_async_remote_copy(src_ref=src, dst_ref=dst, send_sem=send_sem, recv_sem=recv_sem,
                                        device_id=_dev_id(to_dev), device_id_type=MESH)


def _presum_hand_off(dev, a, dest_chips, g_ref, slots, pair, send_sems, recv_sems):
    chip, core = dev // 2, dev % 2
    cps = []
    for k, q in enumerate(dest_chips):
        piece = _grad_piece(g_ref, a, 2 * q + 1 - core)
        if q == chip:
            cps.append(_presum_copy(piece, slots.at[dev], send_sems.at[N_DEV + k], recv_sems.at[dev], dev ^ 1))
        else:
            cps.append(_presum_copy(piece, pair.at[k], send_sems.at[N_DEV + k], recv_sems.at[N_DEV + k], dev ^ 1))
    return cps


def _presum_sends(dev, a, dest_chips, slots, sums, send_sems, recv_sems):
    chip, core = dev // 2, dev % 2
    return [_presum_copy(sums.at[k], slots.at[dev], send_sems.at[2 * q + core], recv_sems.at[dev], 2 * q + core)
            for k, q in enumerate(dest_chips) if q != chip]


def _presum_loads(dev, a, dest_chips, g_ref, stage, load_sems):
    chip, core = dev // 2, dev % 2
    return [pltpu.make_async_copy(_grad_piece(g_ref, a, 2 * q + core), stage.at[k], load_sems.at[k])
            for k, q in enumerate(dest_chips) if q != chip]


def _presum_send(dev, a, dest_chips, g_ref, slots, pair, stage, sums, send_sems, recv_sems, load_sems):
    chip, core = dev // 2, dev % 2
    hand = _presum_hand_off(dev, a, dest_chips, g_ref, slots, pair, send_sems, recv_sems)
    for cp in _presum_loads(dev, a, dest_chips, g_ref, stage, load_sems):
        cp.wait()
    for k, q in enumerate(dest_chips):
        if q != chip:
            hand[k].wait_recv()
            sums[k] = (stage[k].astype(F32) + pair[k].astype(F32)).astype(BF16)
    for cp in _presum_sends(dev, a, dest_chips, slots, sums, send_sems, recv_sems):
        cp.start()


def _presum_wait(dev, a, dest_chips, g_ref, slots, pair, sums, send_sems, recv_sems):
    chip, core = dev // 2, dev % 2
    for cp in _presum_hand_off(dev, a, dest_chips, g_ref, slots, pair, send_sems, recv_sems):
        cp.wait_send()
    for cp in _presum_sends(dev, a, dest_chips, slots, sums, send_sems, recv_sems):
        cp.wait_send()
    if chip in dest_chips:
        for src_dev in _presum_sources(dev):
            _presum_copy(sums.at[0], slots.at[src_dev], send_sems.at[src_dev], recv_sems.at[src_dev], src_dev).wait_recv()


def _presum_sources(dev):
    return [dev ^ 1] + [2 * r + dev % 2 for r in range(N_CHIPS) if r != dev // 2]


def _presum_scratch(a, dest_chips):
    n = len(dest_chips)
    piece = _GRAD_PIECE_SHAPES[a]
    return [pltpu.VMEM((n,) + piece, BF16), pltpu.VMEM((n,) + piece, BF16), pltpu.VMEM((n,) + piece, BF16),
            pltpu.SemaphoreType.DMA((N_DEV + n,)), pltpu.SemaphoreType.DMA((N_DEV + n,)),
            pltpu.SemaphoreType.DMA((n,))]


PRESUM_SCRATCH = 6


def _presum_program(first, second, last, a, dest_chips, g_ref, slots, scratch):
    pair, stage, sums, send_sems, recv_sems, load_sems = scratch
    me = _me()

    @pl.when(first)
    def _():
        for dev in range(N_DEV):
            @pl.when(me == dev)
            def _():
                for cp in _presum_hand_off(dev, a, dest_chips, g_ref, slots, pair, send_sems, recv_sems):
                    cp.start()
                for cp in _presum_loads(dev, a, dest_chips, g_ref, stage, load_sems):
                    cp.start()

    @pl.when(second)
    def _():
        for dev in range(N_DEV):
            @pl.when(me == dev)
            def _():
                _presum_send(dev, a, dest_chips, g_ref, slots, pair, stage, sums, send_sems, recv_sems, load_sems)

    def finish():
        for dev in range(N_DEV):
            @pl.when(me == dev)
            def _():
                _presum_wait(dev, a, dest_chips, g_ref, slots, pair, sums, send_sems, recv_sems)

    return finish


def _attn_bwd(proj, do, rsave, bq, bk, npairs, grads):
    plan = ((0, _EARLY_CHIPS), (1, _ALL_CHIPS), (2, _ALL_CHIPS), (3, _ALL_CHIPS))
    s = proj.shape[0]
    nq = s // bq
    ratio = bq // bk
    scale = HEAD_DIM ** -0.5
    heads = tuple(range(2 * npairs))
    width = LANES * npairs

    n_steps = N_HEADS // (2 * npairs)
    n_g = len(grads)

    def body(q_ref, k_ref, v_ref, do_ref, rs_ref, *refs):
        g_src = refs[:n_g]
        dq_ref, dk_ref, dv_ref = refs[n_g:n_g + 3]
        g_slots = refs[n_g + 3:2 * n_g + 3]
        dk_acc, dv_acc, dq_acc, e_ref = refs[2 * n_g + 3:2 * n_g + 7]
        i = pl.program_id(1)
        step = pl.program_id(0) * nq + i
        finish = [_presum_program(step == 0, step == 1, step == n_steps * nq - 1, a, chips, g_src[pos], g_slots[pos],
                                  refs[2 * n_g + 7 + PRESUM_SCRATCH * pos:2 * n_g + 7 + PRESUM_SCRATCH * (pos + 1)])
                  for pos, (a, chips) in enumerate(plan)]

        lane = lax.broadcasted_iota(jnp.int32, (bq, LANES), 1)
        lo_half = lane < HEAD_DIM
        qm, dom = [], []
        for pr in range(npairs):
            cols = slice(LANES * pr, LANES * (pr + 1))
            q = q_ref[:, cols] * jnp.asarray(scale, BF16)
            zero = jnp.zeros_like(q)
            qm += [jnp.where(lo_half, q, zero), jnp.where(lo_half, zero, q)]
            dout = do_ref[:, cols].astype(F32)
            dom += [jnp.where(lo_half, dout, 0.0), jnp.where(lo_half, 0.0, dout)]
        row = lax.broadcasted_iota(jnp.int32, (bq, bk), 0)
        col = lax.broadcasted_iota(jnp.int32, (bq, bk), 1)
        tneg = _tri(bk, False, -1.0)
        tfwd = _tri(bk, True, 1.0)

        @pl.when(i == 0)
        def _():
            dk_acc[...] = jnp.zeros_like(dk_acc)
            dv_acc[...] = jnp.zeros_like(dv_acc)

        dq_acc[...] = jnp.zeros_like(dq_acc)
        e_ref[...] = jnp.zeros_like(e_ref)

        def block(j, diag, valid=None):
            ks = pl.multiple_of(j * bk, bk)
            kj = [k_ref[pl.ds(ks, bk), LANES * pr:LANES * (pr + 1)] for pr in range(npairs)]
            vj = [v_ref[pl.ds(ks, bk), LANES * pr:LANES * (pr + 1)] for pr in range(npairs)]
            if diag:
                before = (j * bk + col) < (i * bq + row)
            z = [_dot_nt(qm[h], kj[h // 2]) for h in heads]
            er = [jnp.exp(jnp.sum(jnp.where(lane == j + HEAD_DIM * (h % 2), rs_ref[h // 2], 0.0), axis=-1,
                                  keepdims=True)) for h in heads]
            if valid is not None:
                er = [er[h] * valid.astype(F32) for h in heads]
            dos = [(dom[h] * er[h]).astype(BF16) for h in heads]
            dw = [_dot_nt(dos[h], vj[h // 2]) for h in heads]
            psp = [_neg_softplus_parts(z[h]) for h in heads]
            sp = [psp[h][1] for h in heads]
            if diag:
                sp = [jnp.where(before, sp[h], 0.0) for h in heads]
            cin = [jnp.dot(_split_cat(sp[h], SCAN_PASSES), tneg, preferred_element_type=F32) for h in heads]
            w = [jnp.exp(z[h] + cin[h]) for h in heads]
            if diag:
                w = [jnp.where(before, w[h], 0.0) for h in heads]
            e =[dw[h] * w[h] for h in heads]
            eincl = [jnp.dot(_split_cat(e[h], SCAN_PASSES), tfwd, preferred_element_type=F32) + e_ref[h]
                     for h in heads]
            dz = []
            for h in heads:
                p = psp[h][0]
                beta = jnp.where(z[h] >= 0.0, 1.0, p) / (1.0 + p)
                d = e[h] - beta * eincl[h]
                dz.append((jnp.where(before, d, 0.0) if diag else d).astype(BF16))
            wb = [w[h].astype(BF16) for h in heads]
            for h in heads:
                e_ref[h] = eincl[h][:, bk - 1:bk]
                dq_acc[h] += jnp.dot(dz[h], kj[h // 2], preferred_element_type=F32)
            for pr in range(npairs):
                cols = slice(LANES * pr, LANES * (pr + 1))
                h0, h1 = 2 * pr, 2 * pr + 1
                dk_acc[pl.ds(ks, bk), cols] += _dot_tn(dz[h0], qm[h0]) + _dot_tn(dz[h1], qm[h1])
                dv_acc[pl.ds(ks, bk), cols] += _dot_tn(wb[h0], dos[h0]) + _dot_tn(wb[h1], dos[h1])

        def loop_body(j, carry):
            block(j, False)
            return carry

        block_of_lane = lane & (HEAD_DIM - 1)
        live = jnp.max(rs_ref[...], axis=0) > DEAD
        first_live = jnp.min(jnp.where(live, block_of_lane, nq * ratio))
        last = jnp.maximum(i * ratio - 1, 0)
        lax.fori_loop(jnp.minimum(first_live, last), last, loop_body, 0)
        block(last, False, valid=i > 0)
        for t in range(ratio):
            block(i * ratio + t, True)
        for pr in range(npairs):
            dq = jnp.where(lo_half, dq_acc[2 * pr], dq_acc[2 * pr + 1]) * scale
            dq_ref[:, LANES * pr:LANES * (pr + 1)] = dq.astype(BF16)

        @pl.when(i == nq - 1)
        def _():
            dk_ref[...] = dk_acc[...].astype(BF16)
            dv_ref[...] = dv_acc[...].astype(BF16)

        @pl.when(step == n_steps * nq - 1)
        def _():
            for fin in finish:
                fin()

    any_spec = pl.BlockSpec(memory_space=pl.ANY)
    return pl.pallas_call(
        body, name="attn_bwd", grid=(n_steps, nq),
        in_specs=[pl.BlockSpec((bq, width), lambda p, i: (i, n_steps * COL_Q + p)),
                  pl.BlockSpec((s, width), lambda p, i: (0, n_steps * COL_K + p)),
                  pl.BlockSpec((s, width), lambda p, i: (0, n_steps * COL_V + p)),
                  pl.BlockSpec((bq, width), lambda p, i: (i, p)),
                  pl.BlockSpec((npairs, bq, LANES), lambda p, i: (p, i, 0))] + [any_spec] * n_g,
        out_specs=[pl.BlockSpec((bq, width), lambda p, i: (i, p)),
                   pl.BlockSpec((s, width), lambda p, i: (0, p)),
                   pl.BlockSpec((s, width), lambda p, i: (0, p))] + [any_spec] * n_g,
        out_shape=[jax.ShapeDtypeStruct((s, D_BRANCH), BF16)] * 3
        + [jax.ShapeDtypeStruct((N_DEV,) + _GRAD_PIECE_SHAPES[a], BF16) for a, _ in plan],
        scratch_shapes=[pltpu.VMEM((s, width), F32), pltpu.VMEM((s, width), F32),
                        pltpu.VMEM((2 * npairs, bq, LANES), F32), pltpu.VMEM((2 * npairs, bq, 1), F32)]
        + [sh for a, chips in plan for sh in _presum_scratch(a, chips)],
        compiler_params=pltpu.CompilerParams(dimension_semantics=("arbitrary", "arbitrary"),
                                             vmem_limit_bytes=VMEM_LIMIT, has_side_effects=True),
    )(proj, proj, proj, do, rsave, *grads)


def _group_avg_matrix():
    a = lax.broadcasted_iota(jnp.int32, (LANES, LANES), 0) >> GROUP_SHIFT
    b = lax.broadcasted_iota(jnp.int32, (LANES, LANES), 1) >> GROUP_SHIFT
    return jnp.where(a == b, 1.0 / GROUP_DIM, 0.0).astype(BF16)


def _group_mean(a, avg):
    parts = [_split_dot(a[:, LANES * k:LANES * (k + 1)], avg, 2) for k in range(D_BRANCH // LANES)]
    return jnp.concatenate(parts, axis=1)


def _sgu_forward_parts(ub, vb, ln_g, ln_b, avg):
    ug, dug = _gelu_and_grad(ub)
    vg, dvg = _gelu_and_grad(vb)
    mu = _group_mean(vg, avg)
    d = vg - mu
    var = _group_mean(d * d, avg)
    rstd = lax.rsqrt(var + EPS)
    vhat = d * rstd
    vn = vhat * ln_g + ln_b
    return ug, dug, dvg, rstd, vhat, vn


def _sgu_mix(w_ref, src_bf16, n_chunks):
    lane = lax.broadcasted_iota(jnp.int32, (SGU_CHUNK, LANES), 1)
    lo_half = lane < GROUP_DIM
    rows = []
    for n in range(n_chunks):
        slabs = []
        for a in range(D_BRANCH // LANES):
            blk = src_bf16[SGU_CHUNK * n:SGU_CHUNK * (n + 1), LANES * a:LANES * (a + 1)]
            zero = jnp.zeros_like(blk)
            m0 = jnp.dot(w_ref[2 * a], jnp.where(lo_half, blk, zero), preferred_element_type=F32)
            m1 = jnp.dot(w_ref[2 * a + 1], jnp.where(lo_half, zero, blk), preferred_element_type=F32)
            slabs.append(m0 + m1)
        rows.append(jnp.concatenate(slabs, axis=1))
    return jnp.concatenate(rows, axis=0)


def _sgu_fwd(proj, ln_g, ln_b, w_mask, bias_full):
    s = proj.shape[0]
    tm = min(512, s)
    n_chunks = tm // SGU_CHUNK

    def body(ub_ref, vb_ref, zb_ref, g_ref, b_ref, w_ref, bias_ref, yb_ref):
        avg = _group_avg_matrix()
        ug, _, _, _, _, vn = _sgu_forward_parts(ub_ref[...].astype(F32), vb_ref[...].astype(F32),
                                                g_ref[...], b_ref[...], avg)
        mixed = _sgu_mix(w_ref, vn.astype(BF16), n_chunks) + jnp.concatenate([bias_ref[...]] * n_chunks, axis=0)
        zb = zb_ref[...].astype(F32)
        yb_ref[...] = (ug * mixed * (zb * _sigmoid(zb))).astype(BF16)

    col = lambda c: pl.BlockSpec((tm, D_BRANCH), lambda i: (i, c))
    full = lambda shape: pl.BlockSpec(shape, lambda i: (0,) * len(shape))
    return pl.pallas_call(
        body, name="sgu_fwd", grid=(s // tm,),
        in_specs=[col(COL_UB), col(COL_VB), col(COL_ZB), full((1, D_BRANCH)), full((1, D_BRANCH)),
                  full((N_GROUPS, SGU_CHUNK, SGU_CHUNK)), full((SGU_CHUNK, D_BRANCH))],
        out_specs=pl.BlockSpec((tm, D_BRANCH), lambda i: (i, 0)),
        out_shape=jax.ShapeDtypeStruct((s, D_BRANCH), BF16),
        compiler_params=_cparams(("parallel",)),
    )(proj, proj, proj, ln_g, ln_b, w_mask, bias_full)


def _sgu_bwd(proj, dyb, ln_g, ln_b, w_mask, w_mask_t, bias_full):
    s = proj.shape[0]
    tm = min(512, s)
    n_chunks = tm // SGU_CHUNK
    n_steps = s // tm

    def body(ub_ref, vb_ref, zb_ref, dyb_ref, g_ref, b_ref, w_ref, wt_ref, bias_ref,
             dsgu_ref, dw_ref, db_ref, dg_ref, dbeta_ref, dmix_acc):
        i = pl.program_id(0)

        @pl.when(i == 0)
        def _():
            dw_ref[...] = jnp.zeros_like(dw_ref)
            dg_ref[...] = jnp.zeros_like(dg_ref)
            dbeta_ref[...] = jnp.zeros_like(dbeta_ref)
            dmix_acc[...] = jnp.zeros_like(dmix_acc)

        avg = _group_avg_matrix()
        ln_gv = g_ref[...]
        ug, dug, dvg, rstd, vhat, vn = _sgu_forward_parts(ub_ref[...].astype(F32), vb_ref[...].astype(F32),
                                                          ln_gv, b_ref[...], avg)
        vnb = vn.astype(BF16)
        mixed = _sgu_mix(w_ref, vnb, n_chunks) + jnp.concatenate([bias_ref[...]] * n_chunks, axis=0)
        zb = zb_ref[...].astype(F32)
        sg = _sigmoid(zb)
        sz = zb * sg
        dsz = sg * (1.0 + zb * (1.0 - sg))
        dy = dyb_ref[...].astype(F32)
        dmixed = dy * ug * sz
        du = dy * mixed * sz * dug
        dzb = dy * ug * mixed * dsz
        dmb = dmixed.astype(BF16)
        dvn = _sgu_mix(wt_ref, dmb, n_chunks)

        lane = lax.broadcasted_iota(jnp.int32, (SGU_CHUNK, LANES), 1)
        lo_half = lane < GROUP_DIM
        dm_sum = None
        for n in range(n_chunks):
            rows = slice(SGU_CHUNK * n, SGU_CHUNK * (n + 1))
            dm_sum = dmixed[rows] if dm_sum is None else dm_sum + dmixed[rows]
            for a in range(D_BRANCH // LANES):
                cols = slice(LANES * a, LANES * (a + 1))
                dblk = dmb[rows, cols]
                vblk = vnb[rows, cols]
                zero = jnp.zeros_like(dblk)
                dw_ref[2 * a] += _dot_nt(jnp.where(lo_half, dblk, zero), vblk)
                dw_ref[2 * a + 1] += _dot_nt(jnp.where(lo_half, zero, dblk), vblk)
        dmix_acc[...] += dm_sum

        dg_ref[...] += jnp.sum(dvn * vhat, axis=0, keepdims=True)
        dbeta_ref[...] += jnp.sum(dvn, axis=0, keepdims=True)
        dvh = dvn * ln_gv
        m1 = _group_mean(dvh, avg)
        m2 = _group_mean(dvh * vhat, avg)
        dv = rstd * (dvh - m1 - vhat * m2) * dvg
        dsgu_ref[:, 0:D_BRANCH] = du.astype(BF16)
        dsgu_ref[:, D_BRANCH:2 * D_BRANCH] = dv.astype(BF16)
        dsgu_ref[:, 2 * D_BRANCH:3 * D_BRANCH] = dzb.astype(BF16)

        @pl.when(i == n_steps - 1)
        def _():
            pos = lax.broadcasted_iota(jnp.int32, (SGU_CHUNK, SGU_CHUNK), 0) >> GROUP_SHIFT
            src = lax.broadcasted_iota(jnp.int32, (SGU_CHUNK, SGU_CHUNK), 1) >> GROUP_SHIFT
            keep = src <= pos
            for g in range(N_GROUPS):
                dw_ref[g] = jnp.where(keep, dw_ref[g], 0.0)
            grp = lax.broadcasted_iota(jnp.int32, (D_BRANCH, LANES), 0) >> GROUP_SHIFT
            sel = (grp == lax.broadcasted_iota(jnp.int32, (D_BRANCH, LANES), 1)).astype(BF16)
            db_ref[...] = _split_dot(dmix_acc[...], sel, 3)

    col = lambda c: pl.BlockSpec((tm, D_BRANCH), lambda i: (i, c))
    full = lambda shape: pl.BlockSpec(shape, lambda i: (0,) * len(shape))
    return pl.pallas_call(
        body, name="sgu_bwd", grid=(n_steps,),
        in_specs=[col(COL_UB), col(COL_VB), col(COL_ZB), pl.BlockSpec((tm, D_BRANCH), lambda i: (i, 0)),
                  full((1, D_BRANCH)), full((1, D_BRANCH)),
                  full((N_GROUPS, SGU_CHUNK, SGU_CHUNK)), full((N_GROUPS, SGU_CHUNK, SGU_CHUNK)),
                  full((SGU_CHUNK, D_BRANCH))],
        out_specs=[pl.BlockSpec((tm, 3 * D_BRANCH), lambda i: (i, 0)),
                   full((N_GROUPS, SGU_CHUNK, SGU_CHUNK)), full((SGU_CHUNK, LANES)),
                   full((1, D_BRANCH)), full((1, D_BRANCH))],
        out_shape=[jax.ShapeDtypeStruct((s, 3 * D_BRANCH), BF16),
                   jax.ShapeDtypeStruct((N_GROUPS, SGU_CHUNK, SGU_CHUNK), F32),
                   jax.ShapeDtypeStruct((SGU_CHUNK, LANES), F32),
                   jax.ShapeDtypeStruct((1, D_BRANCH), F32), jax.ShapeDtypeStruct((1, D_BRANCH), F32)],
        scratch_shapes=[pltpu.VMEM((SGU_CHUNK, D_BRANCH), F32)],
        compiler_params=_cparams(("arbitrary",)),
    )(proj, proj, proj, dyb, ln_g, ln_b, w_mask, w_mask_t, bias_full)


def _mid(proj, ya, yb, o, x, target, final_g, w_up_a, w_up_b, w_out):
    s = x.shape[0]
    tm = min(256, s)
    n_steps = s // tm
    half = D_MODEL // 2

    def body(ya_ref, yb_ref, o_ref, za_ref, ga0_ref, ga1_ref, gb0_ref, gb1_ref, x_ref, t_ref, gf_ref,
             wa_ref, wb_ref, wo_ref,
             dzg_ref, do_ref, dyb_ref, dx2_ref, gwo_ref, gwa_ref, gwb_ref, loss_ref, dgf_ref,
             acc_o, acc_a, acc_b):
        i = pl.program_id(0)

        @pl.when(i == 0)
        def _():
            acc_o[...] = jnp.zeros_like(acc_o)
            acc_a[...] = jnp.zeros_like(acc_a)
            acc_b[...] = jnp.zeros_like(acc_b)
            loss_ref[...] = jnp.zeros_like(loss_ref)
            dgf_ref[...] = jnp.zeros_like(dgf_ref)

        ya_v = ya_ref[...]
        yb_v = yb_ref[...]
        pa = jnp.dot(ya_v, wa_ref[...], preferred_element_type=F32)
        pb = jnp.dot(yb_v, wb_ref[...], preferred_element_type=F32)
        sa = _sigmoid(jnp.concatenate([ga0_ref[...], ga1_ref[...]], axis=1).astype(F32))
        sb = _sigmoid(jnp.concatenate([gb0_ref[...], gb1_ref[...]], axis=1).astype(F32))
        merged = (sa * pa + sb * pb).astype(BF16)
        x2 = x_ref[...] + jnp.dot(merged, wo_ref[...], preferred_element_type=F32)
        r2 = lax.rsqrt(jnp.mean(x2 * x2, axis=-1, keepdims=True) + EPS)
        xh = x2 * r2
        gf = gf_ref[...]
        diff = xh * gf - t_ref[...]
        loss_ref[...] += 0.5 * jnp.sum(jnp.mean(diff * diff, axis=-1, keepdims=True))
        dy = diff * (1.0 / D_MODEL)
        dgf_ref[...] += jnp.sum(dy * xh, axis=0, keepdims=True)
        dyg = dy * gf
        dx2 = r2 * (dyg - xh * jnp.mean(dyg * xh, axis=-1, keepdims=True))
        dx2_ref[...] = dx2
        dx2b = dx2.astype(BF16)
        dmerged = _dot_nt(dx2b, wo_ref[...])
        acc_o[...] += _dot_tn(merged, dx2b)
        dpa = dmerged * sa
        dpb = dmerged * sb
        dzg_ref[:, D_BRANCH:D_BRANCH + D_MODEL] = (dpa * pa * (1.0 - sa)).astype(BF16)
        dzg_ref[:, D_BRANCH + D_MODEL:D_BRANCH + 2 * D_MODEL] = (dpb * pb * (1.0 - sb)).astype(BF16)
        dpab = dpa.astype(BF16)
        dpbb = dpb.astype(BF16)
        acc_a[...] += _dot_tn(ya_v, dpab)
        acc_b[...] += _dot_tn(yb_v, dpbb)
        dya = _dot_nt(dpab, wa_ref[...])
        dyb_ref[...] = _dot_nt(dpbb, wb_ref[...]).astype(BF16)
        za = za_ref[...].astype(F32)
        sg = _sigmoid(za)
        do_ref[...] = (dya * (za * sg)).astype(BF16)
        dzg_ref[:, 0:D_BRANCH] = (dya * o_ref[...].astype(F32) * (sg * (1.0 + za * (1.0 - sg)))).astype(BF16)

        @pl.when(i == n_steps - 1)
        def _():
            gwo_ref[...] = acc_o[...].astype(BF16)
            gwa_ref[...] = acc_a[...].astype(BF16)
            gwb_ref[...] = acc_b[...].astype(BF16)

    tok = lambda w: pl.BlockSpec((tm, w), lambda i: (i, 0))
    col = lambda c: pl.BlockSpec((tm, half), lambda i: (i, c))
    full = lambda shape: pl.BlockSpec(shape, lambda i: (0,) * len(shape))
    return pl.pallas_call(
        body, name="mid", grid=(n_steps,),
        in_specs=[tok(D_BRANCH), tok(D_BRANCH), tok(D_BRANCH), col(COL_ZA), col(COL_GA), col(COL_GA + 1),
                  col(COL_GB), col(COL_GB + 1), tok(D_MODEL), tok(D_MODEL), full((1, D_MODEL)),
                  full((D_BRANCH, D_MODEL)), full((D_BRANCH, D_MODEL)), full((D_MODEL, D_MODEL))],
        out_specs=[tok(D_BRANCH + 2 * D_MODEL), tok(D_BRANCH), tok(D_BRANCH), tok(D_MODEL),
                   full((D_MODEL, D_MODEL)), full((D_BRANCH, D_MODEL)), full((D_BRANCH, D_MODEL)),
                   full((8, LANES)), full((1, D_MODEL))],
        out_shape=[jax.ShapeDtypeStruct((s, D_BRANCH + 2 * D_MODEL), BF16),
                   jax.ShapeDtypeStruct((s, D_BRANCH), BF16), jax.ShapeDtypeStruct((s, D_BRANCH), BF16),
                   jax.ShapeDtypeStruct((s, D_MODEL), F32),
                   jax.ShapeDtypeStruct((D_MODEL, D_MODEL), BF16),
                   jax.ShapeDtypeStruct((D_BRANCH, D_MODEL), BF16), jax.ShapeDtypeStruct((D_BRANCH, D_MODEL), BF16),
                   jax.ShapeDtypeStruct((8, LANES), F32), jax.ShapeDtypeStruct((1, D_MODEL), F32)],
        scratch_shapes=[pltpu.VMEM((D_MODEL, D_MODEL), F32), pltpu.VMEM((D_BRANCH, D_MODEL), F32),
                        pltpu.VMEM((D_BRANCH, D_MODEL), F32)],
        compiler_params=_cparams(("arbitrary",)),
    )(ya, yb, o, proj, proj, proj, proj, proj, x, target, final_g, w_up_a, w_up_b, w_out)


def _dwin_early(ht, first, tile_of_first, second, tile_of_second):
    s = ht.shape[1]
    n1 = first.shape[1] // D_BRANCH
    n2 = second.shape[1] // D_BRANCH

    def body(ht_ref, a_ref, b_ref, out_ref):
        j = pl.program_id(0)

        @pl.when(j < n1)
        def _():
            out_ref[...] = jnp.dot(ht_ref[...], a_ref[...], preferred_element_type=F32).astype(BF16)

        @pl.when(j >= n1)
        def _():
            out_ref[...] = jnp.dot(ht_ref[...], b_ref[...], preferred_element_type=F32).astype(BF16)

    return pl.pallas_call(
        body, name="dwin_early", grid=(n1 + n2,),
        in_specs=[pl.BlockSpec((D_MODEL, s), lambda j: (0, 0)),
                  pl.BlockSpec((s, D_BRANCH), lambda j: (0, jnp.minimum(j, n1 - 1))),
                  pl.BlockSpec((s, D_BRANCH), lambda j: (0, jnp.maximum(j - n1, 0)))],
        out_specs=pl.BlockSpec((D_MODEL, D_BRANCH),
                               lambda j: (0, jnp.where(j < n1, tile_of_first(j), tile_of_second(j - n1)))),
        out_shape=jax.ShapeDtypeStruct((D_MODEL, D_IN), BF16),
        compiler_params=_cparams(("arbitrary",)),
    )(ht, first, second)


def _dwin_pieces(ht, pieces, first_tile, prev):
    s = ht.shape[1]
    n = len(pieces)

    def body(ht_ref, *refs):
        srcs = refs[:n]
        out_ref, buf, sems = refs[n + 1:]
        j = pl.program_id(0)

        @pl.when(j == 0)
        def _():
            for k in range(n):
                pltpu.make_async_copy(srcs[k], buf.at[k], sems.at[k]).start()

        for k in range(n):
            @pl.when(j == k)
            def _():
                pltpu.make_async_copy(srcs[k], buf.at[k], sems.at[k]).wait()

        out_ref[...] = jnp.dot(ht_ref[...], buf[j], preferred_element_type=F32).astype(BF16)

    any_spec = pl.BlockSpec(memory_space=pl.ANY)
    return pl.pallas_call(
        body, name="dwin_pieces", grid=(n,),
        in_specs=[pl.BlockSpec((D_MODEL, s), lambda j: (0, 0))] + [any_spec] * (n + 1),
        out_specs=pl.BlockSpec((D_MODEL, D_BRANCH), lambda j: (0, first_tile + j)),
        out_shape=jax.ShapeDtypeStruct((D_MODEL, D_IN), BF16),
        scratch_shapes=[pltpu.VMEM((n, s, D_BRANCH), BF16), pltpu.SemaphoreType.DMA((n,))],
        input_output_aliases={n + 1: 0},
        compiler_params=_cparams(("arbitrary",)),
    )(ht, *pieces, prev)


def _dh_dx(pieces, w_in, x, norm_g, dx2, g_in):
    s = x.shape[0]
    tm = min(256, s)
    n_steps = s // tm
    arrays = []
    for arr, _, _, _ in pieces:
        if not any(arr is a for a in arrays):
            arrays.append(arr)
    n_arr = len(arrays)
    plan = [([k for k, a in enumerate(arrays) if a is arr][0], wcol, off, width) for arr, wcol, off, width in pieces]

    def body(*refs):
        p_refs = refs[:n_arr]
        w_ref, x_ref, g_ref, dx2_ref, gin_ref, dx_ref, dg_ref, late_ref = refs[n_arr:n_arr + 8]
        step = pl.program_id(0)
        finish = _presum_program(step == 0, step == min(1, n_steps - 1), step == n_steps - 1, 0, _LATE_CHIPS,
                                 gin_ref, late_ref, refs[n_arr + 8:])

        @pl.when(step == 0)
        def _():
            dg_ref[...] = jnp.zeros_like(dg_ref)

        dh = None
        for k, wcol, off, width in plan:
            d = _dot_nt(p_refs[k][:, off:off + width], w_ref[:, wcol:wcol + width])
            dh = d if dh is None else dh + d
        xf = x_ref[...]
        r = lax.rsqrt(jnp.mean(xf * xf, axis=-1, keepdims=True) + EPS)
        xh = xf * r
        dg_ref[...] += jnp.sum(dh * xh, axis=0, keepdims=True)
        dhg = dh * g_ref[...]
        dx_ref[...] = r * (dhg - xh * jnp.mean(dhg * xh, axis=-1, keepdims=True)) + dx2_ref[...]

        @pl.when(step == n_steps - 1)
        def _():
            finish()

    tok = lambda w: pl.BlockSpec((tm, w), lambda i: (i, 0))
    full = lambda shape: pl.BlockSpec(shape, lambda i: (0,) * len(shape))
    any_spec = pl.BlockSpec(memory_space=pl.ANY)
    return pl.pallas_call(
        body, name="dh_dx", grid=(n_steps,),
        in_specs=[tok(a.shape[1]) for a in arrays] + [full((D_MODEL, D_IN)), tok(D_MODEL), full((1, D_MODEL)),
                                                      tok(D_MODEL), any_spec],
        out_specs=[tok(D_MODEL), full((1, D_MODEL)), any_spec],
        out_shape=[jax.ShapeDtypeStruct((s, D_MODEL), F32), jax.ShapeDtypeStruct((1, D_MODEL), F32),
                   jax.ShapeDtypeStruct((N_DEV,) + _GRAD_PIECE_SHAPES[0], BF16)],
        scratch_shapes=_presum_scratch(0, _LATE_CHIPS),
        compiler_params=pltpu.CompilerParams(dimension_semantics=("arbitrary",), vmem_limit_bytes=VMEM_LIMIT,
                                             has_side_effects=True),
    )(*arrays, w_in, x, norm_g, dx2, g_in)


ADAM_GRID = 8


def _adamw(sets):
    n = len(sets)
    c1 = 1.0 - ADAM_B1 ** ADAM_STEP
    c2 = 1.0 - ADAM_B2 ** ADAM_STEP

    def body(*refs):
        for k in range(n):
            w_ref, g_ref, m_ref, v_ref = refs[4 * k:4 * k + 4]
            g_out_ref, d_ref, nm_ref, nv_ref = refs[4 * n + 4 * k:4 * n + 4 * k + 4]
            gv = g_ref[...]
            g_out_ref[...] = gv
            nm = ADAM_B1 * m_ref[...] + (1.0 - ADAM_B1) * gv
            nv = ADAM_B2 * v_ref[...] + (1.0 - ADAM_B2) * (gv * gv)
            d_ref[...] = -ADAM_LR * ((nm / c1) / (jnp.sqrt(nv / c2) + ADAM_EPS) + ADAM_WD * w_ref[...])
            nm_ref[...] = nm
            nv_ref[...] = nv

    in_specs, out_specs, shapes, args = [], [], [], []
    for w, g, m, v, g_tile0 in sets:
        rows, cols = w.shape
        assert rows % (8 * ADAM_GRID) == 0, (rows, cols)
        spec = pl.BlockSpec((rows // ADAM_GRID, cols), lambda i: (i, 0))
        g_spec = pl.BlockSpec((rows // ADAM_GRID, cols), lambda i, t0=g_tile0: (t0 + i, 0))
        in_specs += [spec, g_spec, spec, spec]
        out_specs += [spec] * 4
        shapes += [jax.ShapeDtypeStruct((rows, cols), F32)] * 4
        args += [w, g, m, v]
    outs = pl.pallas_call(
        body, name="adamw", grid=(ADAM_GRID,),
        in_specs=in_specs, out_specs=out_specs, out_shape=shapes,
        compiler_params=_cparams(("parallel",)),
    )(*args)
    return [outs[4 * k:4 * k + 4] for k in range(n)]


def _reduce_grads_tail(grads, g_small, early_slots, late_in_slots):
    n_big = len(grads)
    n_arr = n_big + 1
    shard_shapes = [(2 * r, w) for r, w in _GRAD_PIECE_SHAPES]
    small_piece = (SMALL_PIECE, LANES)

    def body(*refs):
        src = refs[:n_arr]
        early = refs[n_arr:n_arr + n_big]
        late_in = refs[n_arr + n_big]
        n_in = n_arr + n_big + 1
        out = refs[n_in:n_in + n_arr]
        slots = refs[n_in + n_arr:n_in + 2 * n_arr]
        sums = refs[n_in + 2 * n_arr:n_in + 3 * n_arr]
        send1, recv1, send2, recv2, local_sems = refs[n_in + 3 * n_arr:]
        x, y, c = _place()
        me = 4 * x + 2 * y + c

        def piece_of(a, dev):
            return src[a].at[dev] if a == n_big else _grad_piece(src[a], a, dev)

        def late(a, dst_dev, src_dev):
            return pltpu.make_async_remote_copy(
                src_ref=piece_of(a, dst_dev), dst_ref=slots[a].at[src_dev],
                send_sem=send1.at[n_arr * dst_dev + a], recv_sem=recv1.at[n_arr * src_dev + a],
                device_id=_dev_id(dst_dev), device_id_type=MESH)

        def late_arrays(dev):
            return (n_big,)

        def load(a, dev, received):
            return pltpu.make_async_copy(received.at[dev], slots[a].at[dev], local_sems.at[n_arr * dev + a])

        def own(a, dev):
            return pltpu.make_async_copy(piece_of(a, dev), slots[a].at[dev], local_sems.at[n_arr * dev + a])

        for dev in range(N_DEV):
            @pl.when(me == dev)
            def _():
                received = [early[0] if dev in _EARLY_IN_DEVS else late_in] + list(early[1:])
                for a in range(n_arr):
                    own(a, dev).start()
                sources = sorted([dev] + _presum_sources(dev))
                for peer in range(N_DEV):
                    if peer != dev:
                        for a in late_arrays(peer):
                            late(a, peer, dev).start()
                        for a in range(n_big):
                            if peer in sources:
                                load(a, peer, received[a]).start()
                for a in range(n_arr):
                    own(a, dev).wait()
                for peer in range(N_DEV):
                    if peer != dev:
                        for a in late_arrays(dev):
                            late(a, dev, peer).wait_recv()
                        for a in range(n_big):
                            if peer in sources:
                                load(a, peer, received[a]).wait()
                for a in range(n_arr):
                    rows = slots[a].shape[1]
                    step = 64 if rows % 64 == 0 else 8
                    used = sources if a < n_big else list(range(N_DEV))

                    def add_rows(t, carry, a=a, step=step, used=used):
                        r0 = pl.multiple_of(t * step, step)
                        total = slots[a][used[0], pl.ds(r0, step), :].astype(F32)
                        for src_dev in used[1:]:
                            total = total + slots[a][src_dev, pl.ds(r0, step), :].astype(F32)
                        sums[a][pl.ds(r0, step), :] = total
                        return carry

                    lax.fori_loop(0, rows // step, add_rows, 0)

        shares = []
        keeps = []
        for a in range(n_big):
            r, w = _GRAD_PIECE_SHAPES[a]
            dst = out[a].at[pl.ds(pl.multiple_of(c * r, 8), r), :]
            cp = pltpu.make_async_remote_copy(src_ref=sums[a], dst_ref=dst, send_sem=send2.at[a], recv_sem=recv2.at[a],
                                              device_id=(x, y, 1 - c), device_id_type=MESH)
            cp.start()
            shares.append(cp)
            kp = pltpu.make_async_copy(sums[a], dst, local_sems.at[N_DEV * n_arr + a])
            kp.start()
            keeps.append(kp)
        kp = pltpu.make_async_copy(sums[n_big], out[n_big].at[me], local_sems.at[N_DEV * n_arr + n_big])
        kp.start()
        keeps.append(kp)

        def small_share(dst_dev, src_dev):
            return pltpu.make_async_remote_copy(src_ref=sums[n_big], dst_ref=out[n_big].at[src_dev],
                                                send_sem=send2.at[n_big + dst_dev], recv_sem=recv2.at[n_big + src_dev],
                                                device_id=_dev_id(dst_dev), device_id_type=MESH)

        for dev in range(N_DEV):
            @pl.when(me != dev)
            def _():
                small_share(dev, me).start()
        for a in range(n_big):
            r, w = _GRAD_PIECE_SHAPES[a]
            other = out[a].at[pl.ds(pl.multiple_of((1 - c) * r, 8), r), :]
            pltpu.make_async_remote_copy(src_ref=sums[a], dst_ref=other, send_sem=send2.at[a], recv_sem=recv2.at[a],
                                         device_id=(x, y, 1 - c), device_id_type=MESH).wait_recv()
        for dev in range(N_DEV):
            @pl.when(me != dev)
            def _():
                small_share(dev, dev).wait_recv()
                small_share(dev, me).wait_send()
                for a in late_arrays(dev):
                    late(a, dev, me).wait_send()
        for cp in shares:
            cp.wait_send()
        for kp in keeps:
            kp.wait()

    any_spec = pl.BlockSpec(memory_space=pl.ANY)
    return pl.pallas_call(
        body, name="reduce_grads_tail",
        in_specs=[any_spec] * (n_arr + n_big + 1), out_specs=[any_spec] * n_arr,
        out_shape=[jax.ShapeDtypeStruct(sh, F32) for sh in shard_shapes]
        + [jax.ShapeDtypeStruct((N_DEV,) + small_piece, F32)],
        scratch_shapes=[pltpu.VMEM((N_DEV,) + sh, BF16) for sh in _GRAD_PIECE_SHAPES]
        + [pltpu.VMEM((N_DEV,) + small_piece, F32)]
        + [pltpu.VMEM(sh, F32) for sh in _GRAD_PIECE_SHAPES] + [pltpu.VMEM(small_piece, F32)]
        + [pltpu.SemaphoreType.DMA((N_DEV * n_arr,)), pltpu.SemaphoreType.DMA((N_DEV * n_arr,)),
           pltpu.SemaphoreType.DMA((n_big + N_DEV,)), pltpu.SemaphoreType.DMA((n_big + N_DEV,)),
           pltpu.SemaphoreType.DMA((N_DEV * n_arr + n_arr,))],
        compiler_params=pltpu.CompilerParams(vmem_limit_bytes=VMEM_LIMIT, has_side_effects=True),
    )(*grads, g_small, *early_slots, late_in_slots)


W_SPATIAL_ROWS = N_GROUPS * SGU_CHUNK
_REST_PARTS = ("norm_g", "sgu_ln_g", "sgu_ln_b", "b_spatial", "final_norm_g")
REST_ROWS = SMALL_ROWS - W_SPATIAL_ROWS
_LOSS_ROW = W_SPATIAL_ROWS + 8 * len(_REST_PARTS)


def _pack_rest(parts, loss_tile=None):
    rows = []
    for name in _REST_PARTS:
        a = parts[name].reshape(-1, LANES).astype(F32)
        rows.append(jnp.pad(a, ((0, 8 - a.shape[0]), (0, 0))))
    rows.append(jnp.zeros((8, LANES), F32) if loss_tile is None else loss_tile)
    rows.append(jnp.zeros((REST_ROWS - 8 * len(rows), LANES), F32))
    return jnp.concatenate(rows, axis=0)


def _pack_small(parts, loss_tile):
    return jnp.concatenate([parts["w_spatial"].reshape(W_SPATIAL_ROWS, LANES), _pack_rest(parts, loss_tile)], axis=0)


def _unpack_rest(packed, shapes):
    out = {}
    for k, name in enumerate(_REST_PARTS):
        n = math.prod(shapes[name])
        out[name] = packed[8 * k:8 * k + n // LANES].reshape(shapes[name])
    return out


def _local_step(proj, ht, x, target, norm_g, w_in, sgu_ln_g, sgu_ln_b, w_spatial, b_spatial, w_up_a, w_up_b, w_out,
                final_norm_g, bq, bk):
    pos = jnp.arange(SGU_CHUNK)
    keep = (pos[None, :] // SGU_SUBCHUNK) <= (pos[:, None] // SGU_SUBCHUNK)
    w_mask = jnp.where(keep[None], w_spatial, 0.0).astype(BF16)
    w_mask_t = jnp.swapaxes(w_mask, 1, 2)
    bias_full = jnp.repeat(b_spatial.T, GROUP_DIM, axis=1)
    ln_g = sgu_ln_g.reshape(1, D_BRANCH)
    ln_b = sgu_ln_b.reshape(1, D_BRANCH)
    final_g = final_norm_g.reshape(1, D_MODEL)

    o, ya, rsave = _attn_fwd(proj, bq, bk, ATTN_PAIRS)
    yb = _sgu_fwd(proj, ln_g, ln_b, w_mask, bias_full)
    dzg, do, dyb, dx2, g_out, g_up_a, g_up_b, loss_acc, d_final = _mid(
        proj, ya, yb, o, x, target, final_g, w_up_a, w_up_b, w_out)
    dsgu, d_wsp, d_bsp, d_lng, d_lnb = _sgu_bwd(proj, dyb, ln_g, ln_b, w_mask, w_mask_t, bias_full)
    g_in = _dwin_early(ht, dzg, lambda j: jnp.where(j == 0, COL_ZA, COL_GA - 1 + j), dsgu, lambda j: COL_UB + j)
    dq, dk, dv, *early_slots = _attn_bwd(proj, do, rsave, bq, bk, ATTN_PAIRS, (g_in, g_up_a, g_up_b, g_out))
    g_in = _dwin_pieces(ht, (dq, dk, dv), COL_Q, g_in)
    pieces = [(dq, COL_Q * D_BRANCH, 0, D_BRANCH), (dk, COL_K * D_BRANCH, 0, D_BRANCH),
              (dv, COL_V * D_BRANCH, 0, D_BRANCH), (dzg, COL_ZA * D_BRANCH, 0, D_BRANCH),
              (dsgu, COL_UB * D_BRANCH, 0, 3 * D_BRANCH), (dzg, COL_GA * D_BRANCH, D_BRANCH, 2 * D_MODEL)]
    dx, d_norm, late_in_slots = _dh_dx(pieces, w_in, x, norm_g, dx2, g_in)
    small = {"norm_g": d_norm, "sgu_ln_g": d_lng, "sgu_ln_b": d_lnb, "w_spatial": d_wsp,
             "b_spatial": d_bsp[:, :N_GROUPS].T, "final_norm_g": d_final}
    return loss_acc, dx, (g_in, g_up_a, g_up_b, g_out), small, early_slots, late_in_slots


def kernel(x, norm_g, w_in, sgu_ln_g, sgu_ln_b, w_spatial, b_spatial, w_up_a, w_up_b, w_out, final_norm_g, loss_target, m_norm_g, m_w_in, m_sgu_ln_g, m_sgu_ln_b, m_w_spatial, m_b_spatial, m_w_up_a, m_w_up_b, m_w_out, m_final_norm_g, v_norm_g, v_w_in, v_sgu_ln_g, v_sgu_ln_b, v_w_spatial, v_b_spatial, v_w_up_a, v_w_up_b, v_w_out, v_final_norm_g):
    big_names = ("w_in", "w_up_a", "w_up_b", "w_out")
    names = ("norm_g", "w_in", "sgu_ln_g", "sgu_ln_b", "w_spatial", "b_spatial", "w_up_a", "w_up_b", "w_out",
             "final_norm_g")
    w = dict(norm_g=norm_g, w_in=w_in, sgu_ln_g=sgu_ln_g, sgu_ln_b=sgu_ln_b, w_spatial=w_spatial,
             b_spatial=b_spatial, w_up_a=w_up_a, w_up_b=w_up_b, w_out=w_out, final_norm_g=final_norm_g)
    m = dict(norm_g=m_norm_g, w_in=m_w_in, sgu_ln_g=m_sgu_ln_g, sgu_ln_b=m_sgu_ln_b, w_spatial=m_w_spatial,
             b_spatial=m_b_spatial, w_up_a=m_w_up_a, w_up_b=m_w_up_b, w_out=m_w_out, final_norm_g=m_final_norm_g)
    v = dict(norm_g=v_norm_g, w_in=v_w_in, sgu_ln_g=v_sgu_ln_g, sgu_ln_b=v_sgu_ln_b, w_spatial=v_w_spatial,
             b_spatial=v_b_spatial, w_up_a=v_w_up_a, w_up_b=v_w_up_b, w_out=v_w_out, final_norm_g=v_final_norm_g)
    shapes = {n: w[n].shape for n in names}
    flat2d = lambda a: a.reshape(a.shape[-2:])

    proj, ht, *full = _in_proj_gather(x[0], norm_g, *[flat2d(w[n]) for n in big_names])
    loss, dx, big_grads, small, early_slots, late_in_slots = _local_step(
        proj, ht, x[0], loss_target[0], norm_g, full[0], sgu_ln_g[0], sgu_ln_b[0], w_spatial[0], b_spatial[0],
        full[1], full[2], full[3], final_norm_g, ATTN_Q_BLOCK, ATTN_K_BLOCK)
    packed = _pack_small(small, loss).reshape(N_DEV, SMALL_PIECE, LANES)
    red = _reduce_grads_tail(big_grads, packed, early_slots, late_in_slots)

    grads, deltas, new_m, new_v = {}, {}, {}, {}
    g_small = red[4].reshape(SMALL_ROWS, LANES)
    rows2d = lambda a: a.reshape(W_SPATIAL_ROWS, LANES)
    sets = [(flat2d(w[n]), g, flat2d(m[n]), flat2d(v[n]), 0) for n, g in zip(big_names, red[:4])]
    sets.append((rows2d(w_spatial), g_small, rows2d(m_w_spatial), rows2d(v_w_spatial), 0))
    sets.append((_pack_rest(w), g_small, _pack_rest(m), _pack_rest(v), W_SPATIAL_ROWS * ADAM_GRID // REST_ROWS))
    updated = _adamw(sets)
    for n, (g, d, nm, nv) in zip(big_names + ("w_spatial",), updated[:5]):
        grads[n], deltas[n], new_m[n], new_v[n] = (a.reshape(shapes[n]) for a in (g, d, nm, nv))
    g_rest, d, nm, nv = updated[5]
    for src, dst in ((g_rest, grads), (d, deltas), (nm, new_m), (nv, new_v)):
        dst.update(_unpack_rest(src, shapes))

    return (g_rest[_LOSS_ROW - W_SPATIAL_ROWS, 0], dx[None], *[grads[n] for n in names], *[deltas[n] for n in names],
            *[new_m[n] for n in names], *[new_v[n] for n in names])
```

```python
import math

import jax
import jax.numpy as jnp
from jax import lax
from jax.experimental import pallas as pl
from jax.experimental.pallas import tpu as pltpu

F32 = jnp.float32
BF16 = jnp.bfloat16

D_MODEL = 1024
N_HEADS = 8
HEAD_DIM = 64
D_BRANCH = 512
D_IN = 4 * D_BRANCH + 3 * D_BRANCH + 2 * D_MODEL
N_GROUPS = 8
GROUP_DIM = 64
SGU_CHUNK = 128
SGU_SUBCHUNK = 64
GROUP_SHIFT = 6
EPS = 1e-6
LANES = 128
ATTN_Q_BLOCK = 256
ATTN_K_BLOCK = 256
DEAD = -110.0
SKIPPED = -1e30
SCAN_PASSES = 1
ATTN_PAIRS = 2
N_CHIPS = 4
N_DEV = 8
MESH = pl.DeviceIdType.MESH

ADAM_LR = 0.001
ADAM_B1 = 0.9
ADAM_B2 = 0.999
ADAM_EPS = 1e-08
ADAM_WD = 0.01
ADAM_STEP = 10

COL_Q, COL_K, COL_V, COL_ZA, COL_UB, COL_VB, COL_ZB, COL_GA, COL_GB = 0, 1, 2, 3, 4, 5, 6, 7, 9

VMEM_LIMIT = 56 * 1024 * 1024

SMALL_ROWS = 1088
SMALL_PIECE = SMALL_ROWS // N_DEV


def _cparams(sem=None):
    return pltpu.CompilerParams(dimension_semantics=sem, vmem_limit_bytes=VMEM_LIMIT)


def _aligned(v, m):
    return v if isinstance(v, int) else pl.multiple_of(v, m)


def _sigmoid(x):
    return 1.0 / (1.0 + jnp.exp(-x))


def _gelu_and_grad(x):
    k = math.sqrt(2.0 / math.pi)
    x2 = x * x
    inner = k * (x + 0.044715 * x * x2)
    th = jnp.tanh(inner)
    g = 0.5 * x * (1.0 + th)
    dg = 0.5 * (1.0 + th) + 0.5 * x * (1.0 - th * th) * (k * (1.0 + 3.0 * 0.044715 * x2))
    return g, dg


def _split_dot(a, b_bf16, passes):
    out = None
    rem = a
    for _ in range(passes):
        part = rem.astype(BF16)
        d = jnp.dot(part, b_bf16, preferred_element_type=F32)
        out = d if out is None else out + d
        rem = rem - part.astype(F32)
    return out


def _dot_nt(a, b):
    return lax.dot_general(a, b, (((1,), (1,)), ((), ())), preferred_element_type=F32)


def _dot_tn(a, b):
    return lax.dot_general(a, b, (((0,), (0,)), ((), ())), preferred_element_type=F32)


def _place():
    x, y, c = lax.axis_index("x"), lax.axis_index("y"), lax.axis_index("c")
    return x, y, c


def _in_proj_gather(x, norm_g, w_in, w_up_a, w_up_b, w_out):
    s = x.shape[0]
    tm = min(1024, s)
    nt = s // tm
    shards = (w_in, w_up_a, w_up_b, w_out)
    n_arr = len(shards)
    col_sharded = (True, True, True, False)
    full_shapes = ((D_MODEL, D_IN), (D_BRANCH, D_MODEL), (D_BRANCH, D_MODEL), (D_MODEL, D_MODEL))
    w_shard = w_in.shape[1]
    half_rows = D_MODEL // 2
    stage_rows = 256

    def body(order_ref, x_ref, g_ref, *refs):
        src = refs[:n_arr]
        proj_ref, ht_ref = refs[n_arr:n_arr + 2]
        out = refs[n_arr + 2:2 * n_arr + 2]
        wsc, h_scr, stage = refs[2 * n_arr + 2:2 * n_arr + 5]
        small_stage = refs[2 * n_arr + 5:2 * n_arr + 8]
        small_cast = refs[2 * n_arr + 8:2 * n_arr + 11]
        send_sems, recv_sems, local_sems = refs[2 * n_arr + 11:]
        k = pl.program_id(0)
        i = pl.program_id(1)
        x_, y_, c = _place()
        chip = 2 * x_ + y_
        sibling = (x_, y_, 1 - c)
        others = [(x_, 1 - y_), (1 - x_, y_), (1 - x_, 1 - y_)]

        def region(a, chip_idx, half):
            if a == 0:
                return wsc.at[chip_idx, pl.ds(_aligned(half * half_rows, 16), half_rows), :]
            r, w = shards[a].shape
            hr = r // 2
            if col_sharded[a]:
                return out[a].at[pl.ds(_aligned(half * hr, 16), hr), pl.ds(_aligned(chip_idx * w, LANES), w)]
            return out[a].at[pl.ds(_aligned(chip_idx * r + half * hr, 16), hr), :]

        def remote(kk, a, chip_idx, half, to, own):
            s_ref = region(a, chip_idx, half)
            if own and a > 0:
                hr = shards[a].shape[0] // 2
                s_ref = small_cast[a - 1].at[pl.ds(_aligned(half * hr, 16), hr), :]
            return pltpu.make_async_remote_copy(src_ref=s_ref, dst_ref=region(a, chip_idx, half),
                                                send_sem=send_sems.at[kk], recv_sem=recv_sems.at[kk],
                                                device_id=to, device_id_type=MESH)

        def keep_whole(kk, chip_idx):
            return pltpu.make_async_copy(wsc.at[chip_idx],
                                         out[0].at[:, pl.ds(_aligned(chip_idx * w_shard, LANES), w_shard)],
                                         local_sems.at[kk])

        def small_stores():
            cps = []
            for a in range(1, n_arr):
                hr = shards[a].shape[0] // 2
                for half in range(2):
                    cps.append(pltpu.make_async_copy(small_cast[a - 1].at[pl.ds(half * hr, hr), :],
                                                     region(a, chip, half), local_sems.at[4 + 2 * (a - 1) + half]))
            return cps

        def arrive_and_pass(j):
            ochip = chip ^ j
            for a in range(n_arr):
                kk = n_arr * (j - 1) + a
                remote(kk, a, ochip, c, sibling, False).wait_recv()
                remote(3 * n_arr + kk, a, ochip, c, sibling, False).start()

        def from_sibling(j, a):
            remote(3 * n_arr + n_arr * (j - 1) + a, a, chip ^ j, 1 - c, sibling, False).wait_recv()

        @pl.when((k == 0) & (i == 0))
        def _():
            def cast_rows(half):
                for t in range(half_rows // stage_rows):
                    r0 = pl.multiple_of(half * half_rows + t * stage_rows, stage_rows)
                    pltpu.sync_copy(src[0].at[pl.ds(r0, stage_rows), :], stage)
                    wsc[chip, pl.ds(r0, stage_rows), :] = stage[...].astype(BF16)

            cast_rows(c)
            for j in (1, 2):
                remote(n_arr * (j - 1), 0, chip, c, (*others[j - 1], c), True).start()
            cast_rows(1 - c)
            for a in range(1, n_arr):
                pltpu.sync_copy(src[a], small_stage[a - 1])
                small_cast[a - 1][...] = small_stage[a - 1][...].astype(BF16)
            for j in (1, 2):
                for a in range(1, n_arr):
                    remote(n_arr * (j - 1) + a, a, chip, c, (*others[j - 1], c), True).start()
            keep_whole(0, chip).start()
            for cp in small_stores():
                cp.start()

        @pl.when((k == 1) & (i == 0))
        def _():
            for j in (1, 2):
                remote(n_arr * (j - 1), 0, chip, c, (*others[j - 1], c), True).wait_send()
            for a in range(n_arr):
                remote(n_arr * 2 + a, a, chip, c, (*others[2], c), True).start()
            arrive_and_pass(1)
            arrive_and_pass(2)
            from_sibling(1, 0)
            keep_whole(1, chip ^ 1).start()

        @pl.when((k == 2) & (i == 0))
        def _():
            from_sibling(2, 0)
            keep_whole(2, chip ^ 2).start()
            arrive_and_pass(3)

        @pl.when((k == 3) & (i == 0))
        def _():
            from_sibling(3, 0)
            keep_whole(3, chip ^ 3).start()

        @pl.when(k == 0)
        def _():
            xf = x_ref[...]
            r = lax.rsqrt(jnp.mean(xf * xf, axis=-1, keepdims=True) + EPS)
            h = xf * r * g_ref[...]
            h_scr[i] = h.astype(BF16)
            ht_ref[...] = h.T.astype(BF16)

        proj_ref[...] = jnp.dot(h_scr[i], wsc[order_ref[k]], preferred_element_type=F32).astype(BF16)

        @pl.when((k == 3) & (i == nt - 1))
        def _():
            for j in (1, 2, 3):
                for a in range(1, n_arr):
                    from_sibling(j, a)
            for j in (1, 2, 3):
                for a in range(n_arr):
                    kk = n_arr * (j - 1) + a
                    if a > 0 or j == 3:
                        remote(kk, a, chip, c, (*others[j - 1], c), True).wait_send()
                    remote(3 * n_arr + kk, a, chip ^ j, c, sibling, False).wait_send()
            for kk in range(4):
                keep_whole(kk, chip ^ kk).wait()
            for cp in small_stores():
                cp.wait()

    any_spec = pl.BlockSpec(memory_space=pl.ANY)
    tile = lambda kk, ii: jnp.where(kk == 0, ii, nt - 1)
    grid_spec = pltpu.PrefetchScalarGridSpec(
        num_scalar_prefetch=1, grid=(N_CHIPS, nt),
        in_specs=[pl.BlockSpec((tm, D_MODEL), lambda kk, ii, order: (tile(kk, ii), 0)),
                  pl.BlockSpec((1, D_MODEL), lambda kk, ii, order: (0, 0))] + [any_spec] * n_arr,
        out_specs=[pl.BlockSpec((tm, w_shard), lambda kk, ii, order: (ii, order[kk])),
                   pl.BlockSpec((D_MODEL, tm), lambda kk, ii, order: (0, tile(kk, ii)))] + [any_spec] * n_arr,
        scratch_shapes=[pltpu.VMEM((N_CHIPS, D_MODEL, w_shard), BF16), pltpu.VMEM((nt, tm, D_MODEL), BF16),
                        pltpu.VMEM((stage_rows, w_shard), F32)]
        + [pltpu.VMEM(a.shape, F32) for a in shards[1:]] + [pltpu.VMEM(a.shape, BF16) for a in shards[1:]]
        + [pltpu.SemaphoreType.DMA((6 * n_arr,)), pltpu.SemaphoreType.DMA((6 * n_arr,)),
           pltpu.SemaphoreType.DMA((4 + 2 * (n_arr - 1),))])
    x_, y_, _ = _place()
    order = (2 * x_ + y_) ^ jnp.arange(N_CHIPS, dtype=jnp.int32)
    return pl.pallas_call(
        body, name="in_proj_gather", grid_spec=grid_spec,
        out_shape=[jax.ShapeDtypeStruct((s, D_IN), BF16), jax.ShapeDtypeStruct((D_MODEL, s), BF16)]
        + [jax.ShapeDtypeStruct(sh, BF16) for sh in full_shapes],
        compiler_params=pltpu.CompilerParams(dimension_semantics=("arbitrary", "arbitrary"),
                                             vmem_limit_bytes=VMEM_LIMIT, has_side_effects=True),
    )(order, x, norm_g, *shards)


def _neg_softplus_parts(z):
    zb = z.astype(BF16)
    p = jnp.exp(-jnp.abs(zb))
    return p, jnp.maximum(zb, jnp.zeros_like(zb)) + jnp.log(1.0 + p)


def _split_cat(a, passes):
    parts = []
    rem = a
    for k in range(passes):
        part = rem.astype(BF16)
        parts.append(part)
        if k + 1 < passes:
            rem = rem - part.astype(F32)
    return parts[0] if passes == 1 else jnp.concatenate(parts, axis=1)


def _tri(blk, upper, sign):
    row = lax.broadcasted_iota(jnp.int32, (blk, blk), 0)
    col = lax.broadcasted_iota(jnp.int32, (blk, blk), 1)
    keep = (row <= col) if upper else (row >= col)
    t = jnp.where(keep, sign, 0.0).astype(BF16)
    return t if SCAN_PASSES == 1 else jnp.concatenate([t] * SCAN_PASSES, axis=0)


def _attn_fwd(proj, bq, bk, npairs):
    s = proj.shape[0]
    nq = s // bq
    ratio = bq // bk
    scale = HEAD_DIM ** -0.5
    heads = tuple(range(2 * npairs))
    width = LANES * npairs

    def body(q_ref, k_ref, v_ref, za_ref, o_ref, ya_ref, rs_ref, acc_ref, r_ref):
        i = pl.program_id(1)
        lane = lax.broadcasted_iota(jnp.int32, (bq, LANES), 1)
        lo_half = lane < HEAD_DIM
        qm = []
        for pr in range(npairs):
            q = q_ref[:, LANES * pr:LANES * (pr + 1)] * jnp.asarray(scale, BF16)
            zero = jnp.zeros_like(q)
            qm += [jnp.where(lo_half, q, zero), jnp.where(lo_half, zero, q)]
        row = lax.broadcasted_iota(jnp.int32, (bq, bk), 0)
        col = lax.broadcasted_iota(jnp.int32, (bq, bk), 1)
        tneg = _tri(bk, False, -1.0)
        acc_ref[...] = jnp.zeros_like(acc_ref)
        r_ref[...] = jnp.zeros_like(r_ref)
        rs_ref[...] = jnp.full_like(rs_ref, SKIPPED)

        def scores(j):
            ks = pl.multiple_of(j * bk, bk)
            return [_dot_nt(qm[h], k_ref[pl.ds(ks, bk), LANES * (h // 2):LANES * (h // 2 + 1)]) for h in heads]

        def block(j, diag, valid=None):
            ks = pl.multiple_of(j * bk, bk)
            vj = [v_ref[pl.ds(ks, bk), LANES * pr:LANES * (pr + 1)] for pr in range(npairs)]
            if diag:
                before = (j * bk + col) < (i * bq + row)
            z = scores(j)
            sp = [_neg_softplus_parts(z[h])[1] for h in heads]
            if diag:
                sp = [jnp.where(before, sp[h], 0.0) for h in heads]
            cin = [jnp.dot(_split_cat(sp[h], SCAN_PASSES), tneg, preferred_element_type=F32) for h in heads]
            w = [jnp.exp(z[h] + cin[h]) for h in heads]
            if diag:
                w = [jnp.where(before, w[h], 0.0) for h in heads]
            pv = [jnp.dot(w[h].astype(BF16), vj[h // 2], preferred_element_type=F32) for h in heads]
            r = [r_ref[h] for h in heads]
            keep = 1.0 if valid is None else valid.astype(F32)
            for h in heads:
                acc_ref[h] += pv[h] * (jnp.exp(r[h]) * keep)
                r_ref[h] = r[h] + cin[h][:, 0:1] * keep
            for pr in range(npairs):
                hit = [lane == j, lane == j + HEAD_DIM]
                if valid is not None:
                    hit = [m & valid for m in hit]
                rs_ref[pr] = jnp.where(hit[0], r[2 * pr], jnp.where(hit[1], r[2 * pr + 1], rs_ref[pr]))

        for t in range(ratio):
            block(i * ratio + ratio - 1 - t, True)
        block(jnp.maximum(i * ratio - 1, 0), False, valid=i > 0)

        def alive(carry):
            jj, r_max = carry
            return (jj < i * ratio - 1) & (r_max > DEAD)

        def loop_body(carry):
            jj, _ = carry
            block(i * ratio - 2 - jj, False)
            return jj + 1, jnp.max(r_ref[...])

        lax.while_loop(alive, loop_body, (0, jnp.max(r_ref[...])))
        for pr in range(npairs):
            cols = slice(LANES * pr, LANES * (pr + 1))
            o = jnp.where(lo_half, acc_ref[2 * pr], acc_ref[2 * pr + 1])
            o_ref[:, cols] = o.astype(BF16)
            za = za_ref[:, cols].astype(F32)
            ya_ref[:, cols] = (o * (za * _sigmoid(za))).astype(BF16)

    n_steps = N_HEADS // (2 * npairs)
    return pl.pallas_call(
        body, name="attn_fwd", grid=(n_steps, nq),
        in_specs=[pl.BlockSpec((bq, width), lambda p, i: (i, n_steps * COL_Q + p)),
                  pl.BlockSpec((s, width), lambda p, i: (0, n_steps * COL_K + p)),
                  pl.BlockSpec((s, width), lambda p, i: (0, n_steps * COL_V + p)),
                  pl.BlockSpec((bq, width), lambda p, i: (i, n_steps * COL_ZA + p))],
        out_specs=[pl.BlockSpec((bq, width), lambda p, i: (i, p)),
                   pl.BlockSpec((bq, width), lambda p, i: (i, p)),
                   pl.BlockSpec((npairs, bq, LANES), lambda p, i: (p, i, 0))],
        out_shape=[jax.ShapeDtypeStruct((s, D_BRANCH), BF16), jax.ShapeDtypeStruct((s, D_BRANCH), BF16),
                   jax.ShapeDtypeStruct((N_HEADS // 2, s, LANES), F32)],
        scratch_shapes=[pltpu.VMEM((2 * npairs, bq, LANES), F32), pltpu.VMEM((2 * npairs, bq, 1), F32)],
        compiler_params=_cparams(("parallel", "parallel")),
    )(proj, proj, proj, proj)


_GRAD_COL_SHARDED = (True, True, True, False)
_GRAD_FULL_SHAPES = ((D_MODEL, D_IN), (D_BRANCH, D_MODEL), (D_BRANCH, D_MODEL), (D_MODEL, D_MODEL))
_GRAD_PIECE_SHAPES = tuple((r // 2, w // N_CHIPS) if cs else (r // (2 * N_CHIPS), w)
                           for (r, w), cs in zip(_GRAD_FULL_SHAPES, _GRAD_COL_SHARDED))
_EARLY_IN_DEVS = (4, 5, 6, 7)
_LATE_IN_DEVS = (0, 1, 2, 3)
_LATE_CHIPS = (0, 1)
_EARLY_CHIPS = (2, 3)
_ALL_CHIPS = (0, 1, 2, 3)


def _grad_piece(ref, a, dev):
    r, w = _GRAD_PIECE_SHAPES[a]
    if _GRAD_COL_SHARDED[a]:
        return ref.at[pl.ds((dev % 2) * r, r), pl.ds((dev // 2) * w, w)]
    return ref.at[pl.ds(dev * r, r), :]


def _dev_id(dev):
    return (dev // 4, (dev // 2) % 2, dev % 2)


def _me():
    return 4 * lax.axis_index("x") + 2 * lax.axis_index("y") + lax.axis_index("c")


def _presum_copy(src, dst, send_sem, recv_sem, to_dev):
    return pltpu.make_async_remote_copy(src_ref=src, dst_ref=dst, send_sem=send_sem, recv_sem=recv_sem,
                                        device_id=_dev_id(to_dev), device_id_type=MESH)


def _presum_hand_off(dev, a, dest_chips, g_ref, slots, pair, send_sems, recv_sems):
    chip, core = dev // 2, dev % 2
    cps = []
    for k, q in enumerate(dest_chips):
        piece = _grad_piece(g_ref, a, 2 * q + 1 - core)
        if q == chip:
            cps.append(_presum_copy(piece, slots.at[dev], send_sems.at[N_DEV + k], recv_sems.at[dev], dev ^ 1))
        else:
            cps.append(_presum_copy(piece, pair.at[k], send_sems.at[N_DEV + k], recv_sems.at[N_DEV + k], dev ^ 1))
    return cps


def _presum_sends(dev, a, dest_chips, slots, sums, send_sems, recv_sems):
    chip, core = dev // 2, dev % 2
    return [_presum_copy(sums.at[k], slots.at[dev], send_sems.at[2 * q + core], recv_sems.at[dev], 2 * q + core)
            for k, q in enumerate(dest_chips) if q != chip]


def _presum_loads(dev, a, dest_chips, g_ref, stage, load_sems):
    chip, core = dev // 2, dev % 2
    return [pltpu.make_async_copy(_grad_piece(g_ref, a, 2 * q + core), stage.at[k], load_sems.at[k])
            for k, q in enumerate(dest_chips) if q != chip]


def _presum_send(dev, a, dest_chips, g_ref, slots, pair, stage, sums, send_sems, recv_sems, load_sems):
    chip, core = dev // 2, dev % 2
    hand = _presum_hand_off(dev, a, dest_chips, g_ref, slots, pair, send_sems, recv_sems)
    for cp in _presum_loads(dev, a, dest_chips, g_ref, stage, load_sems):
        cp.wait()
    for k, q in enumerate(dest_chips):
        if q != chip:
            hand[k].wait_recv()
            sums[k] = (stage[k].astype(F32) + pair[k].astype(F32)).astype(BF16)
    for cp in _presum_sends(dev, a, dest_chips, slots, sums, send_sems, recv_sems):
        cp.start()


def _presum_wait(dev, a, dest_chips, g_ref, slots, pair, sums, send_sems, recv_sems):
    chip, core = dev // 2, dev % 2
    for cp in _presum_hand_off(dev, a, dest_chips, g_ref, slots, pair, send_sems, recv_sems):
        cp.wait_send()
    for cp in _presum_sends(dev, a, dest_chips, slots, sums, send_sems, recv_sems):
        cp.wait_send()
    if chip in dest_chips:
        for src_dev in _presum_sources(dev):
            _presum_copy(sums.at[0], slots.at[src_dev], send_sems.at[src_dev], recv_sems.at[src_dev], src_dev).wait_recv()


def _presum_sources(dev):
    return [dev ^ 1] + [2 * r + dev % 2 for r in range(N_CHIPS) if r != dev // 2]


def _presum_scratch(a, dest_chips):
    n = len(dest_chips)
    piece = _GRAD_PIECE_SHAPES[a]
    return [pltpu.VMEM((n,) + piece, BF16), pltpu.VMEM((n,) + piece, BF16), pltpu.VMEM((n,) + piece, BF16),
            pltpu.SemaphoreType.DMA((N_DEV + n,)), pltpu.SemaphoreType.DMA((N_DEV + n,)),
            pltpu.SemaphoreType.DMA((n,))]


PRESUM_SCRATCH = 6


def _presum_program(first, second, last, a, dest_chips, g_ref, slots, scratch):
    pair, stage, sums, send_sems, recv_sems, load_sems = scratch
    me = _me()

    @pl.when(first)
    def _():
        for dev in range(N_DEV):
            @pl.when(me == dev)
            def _():
                for cp in _presum_hand_off(dev, a, dest_chips, g_ref, slots, pair, send_sems, recv_sems):
                    cp.start()
                for cp in _presum_loads(dev, a, dest_chips, g_ref, stage, load_sems):
                    cp.start()

    @pl.when(second)
    def _():
        for dev in range(N_DEV):
            @pl.when(me == dev)
            def _():
                _presum_send(dev, a, dest_chips, g_ref, slots, pair, stage, sums, send_sems, recv_sems, load_sems)

    def finish():
        for dev in range(N_DEV):
            @pl.when(me == dev)
            def _():
                _presum_wait(dev, a, dest_chips, g_ref, slots, pair, sums, send_sems, recv_sems)

    return finish


def _attn_bwd(proj, do, rsave, bq, bk, npairs, grads):
    plan = ((0, _EARLY_CHIPS), (1, _ALL_CHIPS), (2, _ALL_CHIPS), (3, _ALL_CHIPS))
    s = proj.shape[0]
    nq = s // bq
    ratio = bq // bk
    scale = HEAD_DIM ** -0.5
    heads = tuple(range(2 * npairs))
    width = LANES * npairs

    n_steps = N_HEADS // (2 * npairs)
    n_g = len(grads)

    def body(q_ref, k_ref, v_ref, do_ref, rs_ref, *refs):
        g_src = refs[:n_g]
        dq_ref, dk_ref, dv_ref = refs[n_g:n_g + 3]
        g_slots = refs[n_g + 3:2 * n_g + 3]
        dk_acc, dv_acc, dq_acc, e_ref = refs[2 * n_g + 3:2 * n_g + 7]
        i = pl.program_id(1)
        step = pl.program_id(0) * nq + i
        finish = [_presum_program(step == 0, step == 1, step == n_steps * nq - 1, a, chips, g_src[pos], g_slots[pos],
                                  refs[2 * n_g + 7 + PRESUM_SCRATCH * pos:2 * n_g + 7 + PRESUM_SCRATCH * (pos + 1)])
                  for pos, (a, chips) in enumerate(plan)]

        lane = lax.broadcasted_iota(jnp.int32, (bq, LANES), 1)
        lo_half = lane < HEAD_DIM
        qm, dom = [], []
        for pr in range(npairs):
            cols = slice(LANES * pr, LANES * (pr + 1))
            q = q_ref[:, cols] * jnp.asarray(scale, BF16)
            zero = jnp.zeros_like(q)
            qm += [jnp.where(lo_half, q, zero), jnp.where(lo_half, zero, q)]
            dout = do_ref[:, cols].astype(F32)
            dom += [jnp.where(lo_half, dout, 0.0), jnp.where(lo_half, 0.0, dout)]
        row = lax.broadcasted_iota(jnp.int32, (bq, bk), 0)
        col = lax.broadcasted_iota(jnp.int32, (bq, bk), 1)
        tneg = _tri(bk, False, -1.0)
        tfwd = _tri(bk, True, 1.0)

        @pl.when(i == 0)
        def _():
            dk_acc[...] = jnp.zeros_like(dk_acc)
            dv_acc[...] = jnp.zeros_like(dv_acc)

        dq_acc[...] = jnp.zeros_like(dq_acc)
        e_ref[...] = jnp.zeros_like(e_ref)

        def block(j, diag, valid=None):
            ks = pl.multiple_of(j * bk, bk)
            kj = [k_ref[pl.ds(ks, bk), LANES * pr:LANES * (pr + 1)] for pr in range(npairs)]
            vj = [v_ref[pl.ds(ks, bk), LANES * pr:LANES * (pr + 1)] for pr in range(npairs)]
            if diag:
                before = (j * bk + col) < (i * bq + row)
            z = [_dot_nt(qm[h], kj[h // 2]) for h in heads]
            er = [jnp.exp(jnp.sum(jnp.where(lane == j + HEAD_DIM * (h % 2), rs_ref[h // 2], 0.0), axis=-1,
                                  keepdims=True)) for h in heads]
            if valid is not None:
                er = [er[h] * valid.astype(F32) for h in heads]
            dos = [(dom[h] * er[h]).astype(BF16) for h in heads]
            dw = [_dot_nt(dos[h], vj[h // 2]) for h in heads]
            psp = [_neg_softplus_parts(z[h]) for h in heads]
            sp = [psp[h][1] for h in heads]
            if diag:
                sp = [jnp.where(before, sp[h], 0.0) for h in heads]
            cin = [jnp.dot(_split_cat(sp[h], SCAN_PASSES), tneg, preferred_element_type=F32) for h in heads]
            w = [jnp.exp(z[h] + cin[h]) for h in heads]
            if diag:
                w = [jnp.where(before, w[h], 0.0) for h in heads]
            e =[dw[h] * w[h] for h in heads]
            eincl = [jnp.dot(_split_cat(e[h], SCAN_PASSES), tfwd, preferred_element_type=F32) + e_ref[h]
                     for h in heads]
            dz = []
            for h in heads:
                p = psp[h][0]
                beta = jnp.where(z[h] >= 0.0, 1.0, p) / (1.0 + p)
                d = e[h] - beta * eincl[h]
                dz.append((jnp.where(before, d, 0.0) if diag else d).astype(BF16))
            wb = [w[h].astype(BF16) for h in heads]
            for h in heads:
                e_ref[h] = eincl[h][:, bk - 1:bk]
                dq_acc[h] += jnp.dot(dz[h], kj[h // 2], preferred_element_type=F32)
            for pr in range(npairs):
                cols = slice(LANES * pr, LANES * (pr + 1))
                h0, h1 = 2 * pr, 2 * pr + 1
                dk_acc[pl.ds(ks, bk), cols] += _dot_tn(dz[h0], qm[h0]) + _dot_tn(dz[h1], qm[h1])
                dv_acc[pl.ds(ks, bk), cols] += _dot_tn(wb[h0], dos[h0]) + _dot_tn(wb[h1], dos[h1])

        def loop_body(j, carry):
            block(j, False)
            return carry

        block_of_lane = lane & (HEAD_DIM - 1)
        live = jnp.max(rs_ref[...], axis=0) > DEAD
        first_live = jnp.min(jnp.where(live, block_of_lane, nq * ratio))
        last = jnp.maximum(i * ratio - 1, 0)
        lax.fori_loop(jnp.minimum(first_live, last), last, loop_body, 0)
        block(last, False, valid=i > 0)
        for t in range(ratio):
            block(i * ratio + t, True)
        for pr in range(npairs):
            dq = jnp.where(lo_half, dq_acc[2 * pr], dq_acc[2 * pr + 1]) * scale
            dq_ref[:, LANES * pr:LANES * (pr + 1)] = dq.astype(BF16)

        @pl.when(i == nq - 1)
        def _():
            dk_ref[...] = dk_acc[...].astype(BF16)
            dv_ref[...] = dv_acc[...].astype(BF16)

        @pl.when(step == n_steps * nq - 1)
        def _():
            for fin in finish:
                fin()

    any_spec = pl.BlockSpec(memory_space=pl.ANY)
    return pl.pallas_call(
        body, name="attn_bwd", grid=(n_steps, nq),
        in_specs=[pl.BlockSpec((bq, width), lambda p, i: (i, n_steps * COL_Q + p)),
                  pl.BlockSpec((s, width), lambda p, i: (0, n_steps * COL_K + p)),
                  pl.BlockSpec((s, width), lambda p, i: (0, n_steps * COL_V + p)),
                  pl.BlockSpec((bq, width), lambda p, i: (i, p)),
                  pl.BlockSpec((npairs, bq, LANES), lambda p, i: (p, i, 0))] + [any_spec] * n_g,
        out_specs=[pl.BlockSpec((bq, width), lambda p, i: (i, p)),
                   pl.BlockSpec((s, width), lambda p, i: (0, p)),
                   pl.BlockSpec((s, width), lambda p, i: (0, p))] + [any_spec] * n_g,
        out_shape=[jax.ShapeDtypeStruct((s, D_BRANCH), BF16)] * 3
        + [jax.ShapeDtypeStruct((N_DEV,) + _GRAD_PIECE_SHAPES[a], BF16) for a, _ in plan],
        scratch_shapes=[pltpu.VMEM((s, width), F32), pltpu.VMEM((s, width), F32),
                        pltpu.VMEM((2 * npairs, bq, LANES), F32), pltpu.VMEM((2 * npairs, bq, 1), F32)]
        + [sh for a, chips in plan for sh in _presum_scratch(a, chips)],
        compiler_params=pltpu.CompilerParams(dimension_semantics=("arbitrary", "arbitrary"),
                                             vmem_limit_bytes=VMEM_LIMIT, has_side_effects=True),
    )(proj, proj, proj, do, rsave, *grads)


def _group_avg_matrix():
    a = lax.broadcasted_iota(jnp.int32, (LANES, LANES), 0) >> GROUP_SHIFT
    b = lax.broadcasted_iota(jnp.int32, (LANES, LANES), 1) >> GROUP_SHIFT
    return jnp.where(a == b, 1.0 / GROUP_DIM, 0.0).astype(BF16)


def _group_mean(a, avg):
    parts = [_split_dot(a[:, LANES * k:LANES * (k + 1)], avg, 2) for k in range(D_BRANCH // LANES)]
    return jnp.concatenate(parts, axis=1)


def _sgu_forward_parts(ub, vb, ln_g, ln_b, avg):
    ug, dug = _gelu_and_grad(ub)
    vg, dvg = _gelu_and_grad(vb)
    mu = _group_mean(vg, avg)
    d = vg - mu
    var = _group_mean(d * d, avg)
    rstd = lax.rsqrt(var + EPS)
    vhat = d * rstd
    vn = vhat * ln_g + ln_b
    return ug, dug, dvg, rstd, vhat, vn


def _sgu_mix(w_ref, src_bf16, n_chunks):
    lane = lax.broadcasted_iota(jnp.int32, (SGU_CHUNK, LANES), 1)
    lo_half = lane < GROUP_DIM
    rows = []
    for n in range(n_chunks):
        slabs = []
        for a in range(D_BRANCH // LANES):
            blk = src_bf16[SGU_CHUNK * n:SGU_CHUNK * (n + 1), LANES * a:LANES * (a + 1)]
            zero = jnp.zeros_like(blk)
            m0 = jnp.dot(w_ref[2 * a], jnp.where(lo_half, blk, zero), preferred_element_type=F32)
            m1 = jnp.dot(w_ref[2 * a + 1], jnp.where(lo_half, zero, blk), preferred_element_type=F32)
            slabs.append(m0 + m1)
        rows.append(jnp.concatenate(slabs, axis=1))
    return jnp.concatenate(rows, axis=0)


def _sgu_fwd(proj, ln_g, ln_b, w_mask, bias_full):
    s = proj.shape[0]
    tm = min(512, s)
    n_chunks = tm // SGU_CHUNK

    def body(ub_ref, vb_ref, zb_ref, g_ref, b_ref, w_ref, bias_ref, yb_ref):
        avg = _group_avg_matrix()
        ug, _, _, _, _, vn = _sgu_forward_parts(ub_ref[...].astype(F32), vb_ref[...].astype(F32),
                                                g_ref[...], b_ref[...], avg)
        mixed = _sgu_mix(w_ref, vn.astype(BF16), n_chunks) + jnp.concatenate([bias_ref[...]] * n_chunks, axis=0)
        zb = zb_ref[...].astype(F32)
        yb_ref[...] = (ug * mixed * (zb * _sigmoid(zb))).astype(BF16)

    col = lambda c: pl.BlockSpec((tm, D_BRANCH), lambda i: (i, c))
    full = lambda shape: pl.BlockSpec(shape, lambda i: (0,) * len(shape))
    return pl.pallas_call(
        body, name="sgu_fwd", grid=(s // tm,),
        in_specs=[col(COL_UB), col(COL_VB), col(COL_ZB), full((1, D_BRANCH)), full((1, D_BRANCH)),
                  full((N_GROUPS, SGU_CHUNK, SGU_CHUNK)), full((SGU_CHUNK, D_BRANCH))],
        out_specs=pl.BlockSpec((tm, D_BRANCH), lambda i: (i, 0)),
        out_shape=jax.ShapeDtypeStruct((s, D_BRANCH), BF16),
        compiler_params=_cparams(("parallel",)),
    )(proj, proj, proj, ln_g, ln_b, w_mask, bias_full)


def _sgu_bwd(proj, dyb, ln_g, ln_b, w_mask, w_mask_t, bias_full):
    s = proj.shape[0]
    tm = min(512, s)
    n_chunks = tm // SGU_CHUNK
    n_steps = s // tm

    def body(ub_ref, vb_ref, zb_ref, dyb_ref, g_ref, b_ref, w_ref, wt_ref, bias_ref,
             dsgu_ref, dw_ref, db_ref, dg_ref, dbeta_ref, dmix_acc):
        i = pl.program_id(0)

        @pl.when(i == 0)
        def _():
            dw_ref[...] = jnp.zeros_like(dw_ref)
            dg_ref[...] = jnp.zeros_like(dg_ref)
            dbeta_ref[...] = jnp.zeros_like(dbeta_ref)
            dmix_acc[...] = jnp.zeros_like(dmix_acc)

        avg = _group_avg_matrix()
        ln_gv = g_ref[...]
        ug, dug, dvg, rstd, vhat, vn = _sgu_forward_parts(ub_ref[...].astype(F32), vb_ref[...].astype(F32),
                                                          ln_gv, b_ref[...], avg)
        vnb = vn.astype(BF16)
        mixed = _sgu_mix(w_ref, vnb, n_chunks) + jnp.concatenate([bias_ref[...]] * n_chunks, axis=0)
        zb = zb_ref[...].astype(F32)
        sg = _sigmoid(zb)
        sz = zb * sg
        dsz = sg * (1.0 + zb * (1.0 - sg))
        dy = dyb_ref[...].astype(F32)
        dmixed = dy * ug * sz
        du = dy * mixed * sz * dug
        dzb = dy * ug * mixed * dsz
        dmb = dmixed.astype(BF16)
        dvn = _sgu_mix(wt_ref, dmb, n_chunks)

        lane = lax.broadcasted_iota(jnp.int32, (SGU_CHUNK, LANES), 1)
        lo_half = lane < GROUP_DIM
        dm_sum = None
        for n in range(n_chunks):
            rows = slice(SGU_CHUNK * n, SGU_CHUNK * (n + 1))
            dm_sum = dmixed[rows] if dm_sum is None else dm_sum + dmixed[rows]
            for a in range(D_BRANCH // LANES):
                cols = slice(LANES * a, LANES * (a + 1))
                dblk = dmb[rows, cols]
                vblk = vnb[rows, cols]
                zero = jnp.zeros_like(dblk)
                dw_ref[2 * a] += _dot_nt(jnp.where(lo_half, dblk, zero), vblk)
                dw_ref[2 * a + 1] += _dot_nt(jnp.where(lo_half, zero, dblk), vblk)
        dmix_acc[...] += dm_sum

        dg_ref[...] += jnp.sum(dvn * vhat, axis=0, keepdims=True)
        dbeta_ref[...] += jnp.sum(dvn, axis=0, keepdims=True)
        dvh = dvn * ln_gv
        m1 = _group_mean(dvh, avg)
        m2 = _group_mean(dvh * vhat, avg)
        dv = rstd * (dvh - m1 - vhat * m2) * dvg
        dsgu_ref[:, 0:D_BRANCH] = du.astype(BF16)
        dsgu_ref[:, D_BRANCH:2 * D_BRANCH] = dv.astype(BF16)
        dsgu_ref[:, 2 * D_BRANCH:3 * D_BRANCH] = dzb.astype(BF16)

        @pl.when(i == n_steps - 1)
        def _():
            pos = lax.broadcasted_iota(jnp.int32, (SGU_CHUNK, SGU_CHUNK), 0) >> GROUP_SHIFT
            src = lax.broadcasted_iota(jnp.int32, (SGU_CHUNK, SGU_CHUNK), 1) >> GROUP_SHIFT
            keep = src <= pos
            for g in range(N_GROUPS):
                dw_ref[g] = jnp.where(keep, dw_ref[g], 0.0)
            grp = lax.broadcasted_iota(jnp.int32, (D_BRANCH, LANES), 0) >> GROUP_SHIFT
            sel = (grp == lax.broadcasted_iota(jnp.int32, (D_BRANCH, LANES), 1)).astype(BF16)
            db_ref[...] = _split_dot(dmix_acc[...], sel, 3)

    col = lambda c: pl.BlockSpec((tm, D_BRANCH), lambda i: (i, c))
    full = lambda shape: pl.BlockSpec(shape, lambda i: (0,) * len(shape))
    return pl.pallas_call(
        body, name="sgu_bwd", grid=(n_steps,),
        in_specs=[col(COL_UB), col(COL_VB), col(COL_ZB), pl.BlockSpec((tm, D_BRANCH), lambda i: (i, 0)),
                  full((1, D_BRANCH)), full((1, D_BRANCH)),
                  full((N_GROUPS, SGU_CHUNK, SGU_CHUNK)), full((N_GROUPS, SGU_CHUNK, SGU_CHUNK)),
                  full((SGU_CHUNK, D_BRANCH))],
        out_specs=[pl.BlockSpec((tm, 3 * D_BRANCH), lambda i: (i, 0)),
                   full((N_GROUPS, SGU_CHUNK, SGU_CHUNK)), full((SGU_CHUNK, LANES)),
                   full((1, D_BRANCH)), full((1, D_BRANCH))],
        out_shape=[jax.ShapeDtypeStruct((s, 3 * D_BRANCH), BF16),
                   jax.ShapeDtypeStruct((N_GROUPS, SGU_CHUNK, SGU_CHUNK), F32),
                   jax.ShapeDtypeStruct((SGU_CHUNK, LANES), F32),
                   jax.ShapeDtypeStruct((1, D_BRANCH), F32), jax.ShapeDtypeStruct((1, D_BRANCH), F32)],
        scratch_shapes=[pltpu.VMEM((SGU_CHUNK, D_BRANCH), F32)],
        compiler_params=_cparams(("arbitrary",)),
    )(proj, proj, proj, dyb, ln_g, ln_b, w_mask, w_mask_t, bias_full)


def _mid(proj, ya, yb, o, x, target, final_g, w_up_a, w_up_b, w_out):
    s = x.shape[0]
    tm = min(256, s)
    n_steps = s // tm
    half = D_MODEL // 2

    def body(ya_ref, yb_ref, o_ref, za_ref, ga0_ref, ga1_ref, gb0_ref, gb1_ref, x_ref, t_ref, gf_ref,
             wa_ref, wb_ref, wo_ref,
             dzg_ref, do_ref, dyb_ref, dx2_ref, gwo_ref, gwa_ref, gwb_ref, loss_ref, dgf_ref,
             acc_o, acc_a, acc_b):
        i = pl.program_id(0)

        @pl.when(i == 0)
        def _():
            acc_o[...] = jnp.zeros_like(acc_o)
            acc_a[...] = jnp.zeros_like(acc_a)
            acc_b[...] = jnp.zeros_like(acc_b)
            loss_ref[...] = jnp.zeros_like(loss_ref)
            dgf_ref[...] = jnp.zeros_like(dgf_ref)

        ya_v = ya_ref[...]
        yb_v = yb_ref[...]
        pa = jnp.dot(ya_v, wa_ref[...], preferred_element_type=F32)
        pb = jnp.dot(yb_v, wb_ref[...], preferred_element_type=F32)
        sa = _sigmoid(jnp.concatenate([ga0_ref[...], ga1_ref[...]], axis=1).astype(F32))
        sb = _sigmoid(jnp.concatenate([gb0_ref[...], gb1_ref[...]], axis=1).astype(F32))
        merged = (sa * pa + sb * pb).astype(BF16)
        x2 = x_ref[...] + jnp.dot(merged, wo_ref[...], preferred_element_type=F32)
        r2 = lax.rsqrt(jnp.mean(x2 * x2, axis=-1, keepdims=True) + EPS)
        xh = x2 * r2
        gf = gf_ref[...]
        diff = xh * gf - t_ref[...]
        loss_ref[...] += 0.5 * jnp.sum(jnp.mean(diff * diff, axis=-1, keepdims=True))
        dy = diff * (1.0 / D_MODEL)
        dgf_ref[...] += jnp.sum(dy * xh, axis=0, keepdims=True)
        dyg = dy * gf
        dx2 = r2 * (dyg - xh * jnp.mean(dyg * xh, axis=-1, keepdims=True))
        dx2_ref[...] = dx2
        dx2b = dx2.astype(BF16)
        dmerged = _dot_nt(dx2b, wo_ref[...])
        acc_o[...] += _dot_tn(merged, dx2b)
        dpa = dmerged * sa
        dpb = dmerged * sb
        dzg_ref[:, D_BRANCH:D_BRANCH + D_MODEL] = (dpa * pa * (1.0 - sa)).astype(BF16)
        dzg_ref[:, D_BRANCH + D_MODEL:D_BRANCH + 2 * D_MODEL] = (dpb * pb * (1.0 - sb)).astype(BF16)
        dpab = dpa.astype(BF16)
        dpbb = dpb.astype(BF16)
        acc_a[...] += _dot_tn(ya_v, dpab)
        acc_b[...] += _dot_tn(yb_v, dpbb)
        dya = _dot_nt(dpab, wa_ref[...])
        dyb_ref[...] = _dot_nt(dpbb, wb_ref[...]).astype(BF16)
        za = za_ref[...].astype(F32)
        sg = _sigmoid(za)
        do_ref[...] = (dya * (za * sg)).astype(BF16)
        dzg_ref[:, 0:D_BRANCH] = (dya * o_ref[...].astype(F32) * (sg * (1.0 + za * (1.0 - sg)))).astype(BF16)

        @pl.when(i == n_steps - 1)
        def _():
            gwo_ref[...] = acc_o[...].astype(BF16)
            gwa_ref[...] = acc_a[...].astype(BF16)
            gwb_ref[...] = acc_b[...].astype(BF16)

    tok = lambda w: pl.BlockSpec((tm, w), lambda i: (i, 0))
    col = lambda c: pl.BlockSpec((tm, half), lambda i: (i, c))
    full = lambda shape: pl.BlockSpec(shape, lambda i: (0,) * len(shape))
    return pl.pallas_call(
        body, name="mid", grid=(n_steps,),
        in_specs=[tok(D_BRANCH), tok(D_BRANCH), tok(D_BRANCH), col(COL_ZA), col(COL_GA), col(COL_GA + 1),
                  col(COL_GB), col(COL_GB + 1), tok(D_MODEL), tok(D_MODEL), full((1, D_MODEL)),
                  full((D_BRANCH, D_MODEL)), full((D_BRANCH, D_MODEL)), full((D_MODEL, D_MODEL))],
        out_specs=[tok(D_BRANCH + 2 * D_MODEL), tok(D_BRANCH), tok(D_BRANCH), tok(D_MODEL),
                   full((D_MODEL, D_MODEL)), full((D_BRANCH, D_MODEL)), full((D_BRANCH, D_MODEL)),
                   full((8, LANES)), full((1, D_MODEL))],
        out_shape=[jax.ShapeDtypeStruct((s, D_BRANCH + 2 * D_MODEL), BF16),
                   jax.ShapeDtypeStruct((s, D_BRANCH), BF16), jax.ShapeDtypeStruct((s, D_BRANCH), BF16),
                   jax.ShapeDtypeStruct((s, D_MODEL), F32),
                   jax.ShapeDtypeStruct((D_MODEL, D_MODEL), BF16),
                   jax.ShapeDtypeStruct((D_BRANCH, D_MODEL), BF16), jax.ShapeDtypeStruct((D_BRANCH, D_MODEL), BF16),
                   jax.ShapeDtypeStruct((8, LANES), F32), jax.ShapeDtypeStruct((1, D_MODEL), F32)],
        scratch_shapes=[pltpu.VMEM((D_MODEL, D_MODEL), F32), pltpu.VMEM((D_BRANCH, D_MODEL), F32),
                        pltpu.VMEM((D_BRANCH, D_MODEL), F32)],
        compiler_params=_cparams(("arbitrary",)),
    )(ya, yb, o, proj, proj, proj, proj, proj, x, target, final_g, w_up_a, w_up_b, w_out)


def _dwin_early(ht, first, tile_of_first, second, tile_of_second):
    s = ht.shape[1]
    n1 = first.shape[1] // D_BRANCH
    n2 = second.shape[1] // D_BRANCH

    def body(ht_ref, a_ref, b_ref, out_ref):
        j = pl.program_id(0)

        @pl.when(j < n1)
        def _():
            out_ref[...] = jnp.dot(ht_ref[...], a_ref[...], preferred_element_type=F32).astype(BF16)

        @pl.when(j >= n1)
        def _():
            out_ref[...] = jnp.dot(ht_ref[...], b_ref[...], preferred_element_type=F32).astype(BF16)

    return pl.pallas_call(
        body, name="dwin_early", grid=(n1 + n2,),
        in_specs=[pl.BlockSpec((D_MODEL, s), lambda j: (0, 0)),
                  pl.BlockSpec((s, D_BRANCH), lambda j: (0, jnp.minimum(j, n1 - 1))),
                  pl.BlockSpec((s, D_BRANCH), lambda j: (0, jnp.maximum(j - n1, 0)))],
        out_specs=pl.BlockSpec((D_MODEL, D_BRANCH),
                               lambda j: (0, jnp.where(j < n1, tile_of_first(j), tile_of_second(j - n1)))),
        out_shape=jax.ShapeDtypeStruct((D_MODEL, D_IN), BF16),
        compiler_params=_cparams(("arbitrary",)),
    )(ht, first, second)


def _dwin_pieces(ht, pieces, first_tile, prev):
    s = ht.shape[1]
    n = len(pieces)

    def body(ht_ref, *refs):
        srcs = refs[:n]
        out_ref, buf, sems = refs[n + 1:]
        j = pl.program_id(0)

        @pl.when(j == 0)
        def _():
            for k in range(n):
                pltpu.make_async_copy(srcs[k], buf.at[k], sems.at[k]).start()

        for k in range(n):
            @pl.when(j == k)
            def _():
                pltpu.make_async_copy(srcs[k], buf.at[k], sems.at[k]).wait()

        out_ref[...] = jnp.dot(ht_ref[...], buf[j], preferred_element_type=F32).astype(BF16)

    any_spec = pl.BlockSpec(memory_space=pl.ANY)
    return pl.pallas_call(
        body, name="dwin_pieces", grid=(n,),
        in_specs=[pl.BlockSpec((D_MODEL, s), lambda j: (0, 0))] + [any_spec] * (n + 1),
        out_specs=pl.BlockSpec((D_MODEL, D_BRANCH), lambda j: (0, first_tile + j)),
        out_shape=jax.ShapeDtypeStruct((D_MODEL, D_IN), BF16),
        scratch_shapes=[pltpu.VMEM((n, s, D_BRANCH), BF16), pltpu.SemaphoreType.DMA((n,))],
        input_output_aliases={n + 1: 0},
        compiler_params=_cparams(("arbitrary",)),
    )(ht, *pieces, prev)


def _dh_dx(pieces, w_in, x, norm_g, dx2, g_in):
    s = x.shape[0]
    tm = min(512, s)
    n_steps = s // tm
    arrays = []
    for arr, _, _, _ in pieces:
        if not any(arr is a for a in arrays):
            arrays.append(arr)
    n_arr = len(arrays)
    plan = [([k for k, a in enumerate(arrays) if a is arr][0], wcol, off, width) for arr, wcol, off, width in pieces]

    def body(*refs):
        p_refs = refs[:n_arr]
        w_ref, x_ref, g_ref, dx2_ref, gin_ref, dx_ref, dg_ref, late_ref = refs[n_arr:n_arr + 8]
        step = pl.program_id(0)
        finish = _presum_program(step == 0, step == min(1, n_steps - 1), step == n_steps - 1, 0, _LATE_CHIPS,
                                 gin_ref, late_ref, refs[n_arr + 8:])

        @pl.when(step == 0)
        def _():
            dg_ref[...] = jnp.zeros_like(dg_ref)

        dh = None
        for k, wcol, off, width in plan:
            d = _dot_nt(p_refs[k][:, off:off + width], w_ref[:, wcol:wcol + width])
            dh = d if dh is None else dh + d
        xf = x_ref[...]
        r = lax.rsqrt(jnp.mean(xf * xf, axis=-1, keepdims=True) + EPS)
        xh = xf * r
        dg_ref[...] += jnp.sum(dh * xh, axis=0, keepdims=True)
        dhg = dh * g_ref[...]
        dx_ref[...] = r * (dhg - xh * jnp.mean(dhg * xh, axis=-1, keepdims=True)) + dx2_ref[...]

        @pl.when(step == n_steps - 1)
        def _():
            finish()

    tok = lambda w: pl.BlockSpec((tm, w), lambda i: (i, 0))
    full = lambda shape: pl.BlockSpec(shape, lambda i: (0,) * len(shape))
    any_spec = pl.BlockSpec(memory_space=pl.ANY)
    return pl.pallas_call(
        body, name="dh_dx", grid=(n_steps,),
        in_specs=[tok(a.shape[1]) for a in arrays] + [full((D_MODEL, D_IN)), tok(D_MODEL), full((1, D_MODEL)),
                                                      tok(D_MODEL), any_spec],
        out_specs=[tok(D_MODEL), full((1, D_MODEL)), any_spec],
        out_shape=[jax.ShapeDtypeStruct((s, D_MODEL), F32), jax.ShapeDtypeStruct((1, D_MODEL), F32),
                   jax.ShapeDtypeStruct((N_DEV,) + _GRAD_PIECE_SHAPES[0], BF16)],
        scratch_shapes=_presum_scratch(0, _LATE_CHIPS),
        compiler_params=pltpu.CompilerParams(dimension_semantics=("arbitrary",), vmem_limit_bytes=VMEM_LIMIT,
                                             has_side_effects=True),
    )(*arrays, w_in, x, norm_g, dx2, g_in)


ADAM_GRID = 8


def _adamw(sets):
    n = len(sets)
    c1 = 1.0 - ADAM_B1 ** ADAM_STEP
    c2 = 1.0 - ADAM_B2 ** ADAM_STEP

    def body(*refs):
        for k in range(n):
            w_ref, g_ref, m_ref, v_ref = refs[4 * k:4 * k + 4]
            g_out_ref, d_ref, nm_ref, nv_ref = refs[4 * n + 4 * k:4 * n + 4 * k + 4]
            gv = g_ref[...]
            g_out_ref[...] = gv
            nm = ADAM_B1 * m_ref[...] + (1.0 - ADAM_B1) * gv
            nv = ADAM_B2 * v_ref[...] + (1.0 - ADAM_B2) * (gv * gv)
            d_ref[...] = -ADAM_LR * ((nm / c1) / (jnp.sqrt(nv / c2) + ADAM_EPS) + ADAM_WD * w_ref[...])
            nm_ref[...] = nm
            nv_ref[...] = nv

    in_specs, out_specs, shapes, args = [], [], [], []
    for w, g, m, v, g_tile0 in sets:
        rows, cols = w.shape
        assert rows % (8 * ADAM_GRID) == 0, (rows, cols)
        spec = pl.BlockSpec((rows // ADAM_GRID, cols), lambda i: (i, 0))
        g_spec = pl.BlockSpec((rows // ADAM_GRID, cols), lambda i, t0=g_tile0: (t0 + i, 0))
        in_specs += [spec, g_spec, spec, spec]
        out_specs += [spec] * 4
        shapes += [jax.ShapeDtypeStruct((rows, cols), F32)] * 4
        args += [w, g, m, v]
    outs = pl.pallas_call(
        body, name="adamw", grid=(ADAM_GRID,),
        in_specs=in_specs, out_specs=out_specs, out_shape=shapes,
        compiler_params=_cparams(("parallel",)),
    )(*args)
    return [outs[4 * k:4 * k + 4] for k in range(n)]


def _reduce_grads_tail(grads, g_small, early_slots, late_in_slots):
    n_big = len(grads)
    n_arr = n_big + 1
    shard_shapes = [(2 * r, w) for r, w in _GRAD_PIECE_SHAPES]
    small_piece = (SMALL_PIECE, LANES)

    def body(*refs):
        src = refs[:n_arr]
        early = refs[n_arr:n_arr + n_big]
        late_in = refs[n_arr + n_big]
        n_in = n_arr + n_big + 1
        out = refs[n_in:n_in + n_arr]
        slots = refs[n_in + n_arr:n_in + 2 * n_arr]
        sums = refs[n_in + 2 * n_arr:n_in + 3 * n_arr]
        send1, recv1, send2, recv2, local_sems = refs[n_in + 3 * n_arr:]
        x, y, c = _place()
        me = 4 * x + 2 * y + c

        def piece_of(a, dev):
            return src[a].at[dev] if a == n_big else _grad_piece(src[a], a, dev)

        def late(a, dst_dev, src_dev):
            return pltpu.make_async_remote_copy(
                src_ref=piece_of(a, dst_dev), dst_ref=slots[a].at[src_dev],
                send_sem=send1.at[n_arr * dst_dev + a], recv_sem=recv1.at[n_arr * src_dev + a],
                device_id=_dev_id(dst_dev), device_id_type=MESH)

        def late_arrays(dev):
            return (n_big,)

        def load(a, dev, received):
            return pltpu.make_async_copy(received.at[dev], slots[a].at[dev], local_sems.at[n_arr * dev + a])

        def own(a, dev):
            return pltpu.make_async_copy(piece_of(a, dev), slots[a].at[dev], local_sems.at[n_arr * dev + a])

        for dev in range(N_DEV):
            @pl.when(me == dev)
            def _():
                received = [early[0] if dev in _EARLY_IN_DEVS else late_in] + list(early[1:])
                for a in range(n_arr):
                    own(a, dev).start()
                sources = sorted([dev] + _presum_sources(dev))
                for peer in range(N_DEV):
                    if peer != dev:
                        for a in late_arrays(peer):
                            late(a, peer, dev).start()
                        for a in range(n_big):
                            if peer in sources:
                                load(a, peer, received[a]).start()
                for a in range(n_arr):
                    own(a, dev).wait()
                for peer in range(N_DEV):
                    if peer != dev:
                        for a in late_arrays(dev):
                            late(a, dev, peer).wait_recv()
                        for a in range(n_big):
                            if peer in sources:
                                load(a, peer, received[a]).wait()
                for a in range(n_arr):
                    rows = slots[a].shape[1]
                    step = 64 if rows % 64 == 0 else 8
                    used = sources if a < n_big else list(range(N_DEV))

                    def add_rows(t, carry, a=a, step=step, used=used):
                        r0 = pl.multiple_of(t * step, step)
                        total = slots[a][used[0], pl.ds(r0, step), :].astype(F32)
                        for src_dev in used[1:]:
                            total = total + slots[a][src_dev, pl.ds(r0, step), :].astype(F32)
                        sums[a][pl.ds(r0, step), :] = total
                        return carry

                    lax.fori_loop(0, rows // step, add_rows, 0)

        shares = []
        keeps = []
        for a in range(n_big):
            r, w = _GRAD_PIECE_SHAPES[a]
            dst = out[a].at[pl.ds(pl.multiple_of(c * r, 8), r), :]
            cp = pltpu.make_async_remote_copy(src_ref=sums[a], dst_ref=dst, send_sem=send2.at[a], recv_sem=recv2.at[a],
                                              device_id=(x, y, 1 - c), device_id_type=MESH)
            cp.start()
            shares.append(cp)
            kp = pltpu.make_async_copy(sums[a], dst, local_sems.at[N_DEV * n_arr + a])
            kp.start()
            keeps.append(kp)
        kp = pltpu.make_async_copy(sums[n_big], out[n_big].at[me], local_sems.at[N_DEV * n_arr + n_big])
        kp.start()
        keeps.append(kp)

        def small_share(dst_dev, src_dev):
            return pltpu.make_async_remote_copy(src_ref=sums[n_big], dst_ref=out[n_big].at[src_dev],
                                                send_sem=send2.at[n_big + dst_dev], recv_sem=recv2.at[n_big + src_dev],
                                                device_id=_dev_id(dst_dev), device_id_type=MESH)

        for dev in range(N_DEV):
            @pl.when(me != dev)
            def _():
                small_share(dev, me).start()
        for a in range(n_big):
            r, w = _GRAD_PIECE_SHAPES[a]
            other = out[a].at[pl.ds(pl.multiple_of((1 - c) * r, 8), r), :]
            pltpu.make_async_remote_copy(src_ref=sums[a], dst_ref=other, send_sem=send2.at[a], recv_sem=recv2.at[a],
                                         device_id=(x, y, 1 - c), device_id_type=MESH).wait_recv()
        for dev in range(N_DEV):
            @pl.when(me != dev)
            def _():
                small_share(dev, dev).wait_recv()
                small_share(dev, me).wait_send()
                for a in late_arrays(dev):
                    late(a, dev, me).wait_send()
        for cp in shares:
            cp.wait_send()
        for kp in keeps:
            kp.wait()

    any_spec = pl.BlockSpec(memory_space=pl.ANY)
    return pl.pallas_call(
        body, name="reduce_grads_tail",
        in_specs=[any_spec] * (n_arr + n_big + 1), out_specs=[any_spec] * n_arr,
        out_shape=[jax.ShapeDtypeStruct(sh, F32) for sh in shard_shapes]
        + [jax.ShapeDtypeStruct((N_DEV,) + small_piece, F32)],
        scratch_shapes=[pltpu.VMEM((N_DEV,) + sh, BF16) for sh in _GRAD_PIECE_SHAPES]
        + [pltpu.VMEM((N_DEV,) + small_piece, F32)]
        + [pltpu.VMEM(sh, F32) for sh in _GRAD_PIECE_SHAPES] + [pltpu.VMEM(small_piece, F32)]
        + [pltpu.SemaphoreType.DMA((N_DEV * n_arr,)), pltpu.SemaphoreType.DMA((N_DEV * n_arr,)),
           pltpu.SemaphoreType.DMA((n_big + N_DEV,)), pltpu.SemaphoreType.DMA((n_big + N_DEV,)),
           pltpu.SemaphoreType.DMA((N_DEV * n_arr + n_arr,))],
        compiler_params=pltpu.CompilerParams(vmem_limit_bytes=VMEM_LIMIT, has_side_effects=True),
    )(*grads, g_small, *early_slots, late_in_slots)


W_SPATIAL_ROWS = N_GROUPS * SGU_CHUNK
_REST_PARTS = ("norm_g", "sgu_ln_g", "sgu_ln_b", "b_spatial", "final_norm_g")
REST_ROWS = SMALL_ROWS - W_SPATIAL_ROWS
_LOSS_ROW = W_SPATIAL_ROWS + 8 * len(_REST_PARTS)


def _pack_rest(parts, loss_tile=None):
    rows = []
    for name in _REST_PARTS:
        a = parts[name].reshape(-1, LANES).astype(F32)
        rows.append(jnp.pad(a, ((0, 8 - a.shape[0]), (0, 0))))
    rows.append(jnp.zeros((8, LANES), F32) if loss_tile is None else loss_tile)
    rows.append(jnp.zeros((REST_ROWS - 8 * len(rows), LANES), F32))
    return jnp.concatenate(rows, axis=0)


def _pack_small(parts, loss_tile):
    return jnp.concatenate([parts["w_spatial"].reshape(W_SPATIAL_ROWS, LANES), _pack_rest(parts, loss_tile)], axis=0)


def _unpack_rest(packed, shapes):
    out = {}
    for k, name in enumerate(_REST_PARTS):
        n = math.prod(shapes[name])
        out[name] = packed[8 * k:8 * k + n // LANES].reshape(shapes[name])
    return out


def _local_step(proj, ht, x, target, norm_g, w_in, sgu_ln_g, sgu_ln_b, w_spatial, b_spatial, w_up_a, w_up_b, w_out,
                final_norm_g, bq, bk):
    pos = jnp.arange(SGU_CHUNK)
    keep = (pos[None, :] // SGU_SUBCHUNK) <= (pos[:, None] // SGU_SUBCHUNK)
    w_mask = jnp.where(keep[None], w_spatial, 0.0).astype(BF16)
    w_mask_t = jnp.swapaxes(w_mask, 1, 2)
    bias_full = jnp.repeat(b_spatial.T, GROUP_DIM, axis=1)
    ln_g = sgu_ln_g.reshape(1, D_BRANCH)
    ln_b = sgu_ln_b.reshape(1, D_BRANCH)
    final_g = final_norm_g.reshape(1, D_MODEL)

    o, ya, rsave = _attn_fwd(proj, bq, bk, ATTN_PAIRS)
    yb = _sgu_fwd(proj, ln_g, ln_b, w_mask, bias_full)
    dzg, do, dyb, dx2, g_out, g_up_a, g_up_b, loss_acc, d_final = _mid(
        proj, ya, yb, o, x, target, final_g, w_up_a, w_up_b, w_out)
    dsgu, d_wsp, d_bsp, d_lng, d_lnb = _sgu_bwd(proj, dyb, ln_g, ln_b, w_mask, w_mask_t, bias_full)
    g_in = _dwin_early(ht, dzg, lambda j: jnp.where(j == 0, COL_ZA, COL_GA - 1 + j), dsgu, lambda j: COL_UB + j)
    dq, dk, dv, *early_slots = _attn_bwd(proj, do, rsave, bq, bk, ATTN_PAIRS, (g_in, g_up_a, g_up_b, g_out))
    g_in = _dwin_pieces(ht, (dq, dk, dv), COL_Q, g_in)
    pieces = [(dq, COL_Q * D_BRANCH, 0, D_BRANCH), (dk, COL_K * D_BRANCH, 0, D_BRANCH),
              (dv, COL_V * D_BRANCH, 0, D_BRANCH), (dzg, COL_ZA * D_BRANCH, 0, D_BRANCH),
              (dsgu, COL_UB * D_BRANCH, 0, 3 * D_BRANCH), (dzg, COL_GA * D_BRANCH, D_BRANCH, 2 * D_MODEL)]
    dx, d_norm, late_in_slots = _dh_dx(pieces, w_in, x, norm_g, dx2, g_in)
    small = {"norm_g": d_norm, "sgu_ln_g": d_lng, "sgu_ln_b": d_lnb, "w_spatial": d_wsp,
             "b_spatial": d_bsp[:, :N_GROUPS].T, "final_norm_g": d_final}
    return loss_acc, dx, (g_in, g_up_a, g_up_b, g_out), small, early_slots, late_in_slots


def kernel(x, norm_g, w_in, sgu_ln_g, sgu_ln_b, w_spatial, b_spatial, w_up_a, w_up_b, w_out, final_norm_g, loss_target, m_norm_g, m_w_in, m_sgu_ln_g, m_sgu_ln_b, m_w_spatial, m_b_spatial, m_w_up_a, m_w_up_b, m_w_out, m_final_norm_g, v_norm_g, v_w_in, v_sgu_ln_g, v_sgu_ln_b, v_w_spatial, v_b_spatial, v_w_up_a, v_w_up_b, v_w_out, v_final_norm_g):
    big_names = ("w_in", "w_up_a", "w_up_b", "w_out")
    names = ("norm_g", "w_in", "sgu_ln_g", "sgu_ln_b", "w_spatial", "b_spatial", "w_up_a", "w_up_b", "w_out",
             "final_norm_g")
    w = dict(norm_g=norm_g, w_in=w_in, sgu_ln_g=sgu_ln_g, sgu_ln_b=sgu_ln_b, w_spatial=w_spatial,
             b_spatial=b_spatial, w_up_a=w_up_a, w_up_b=w_up_b, w_out=w_out, final_norm_g=final_norm_g)
    m = dict(norm_g=m_norm_g, w_in=m_w_in, sgu_ln_g=m_sgu_ln_g, sgu_ln_b=m_sgu_ln_b, w_spatial=m_w_spatial,
             b_spatial=m_b_spatial, w_up_a=m_w_up_a, w_up_b=m_w_up_b, w_out=m_w_out, final_norm_g=m_final_norm_g)
    v = dict(norm_g=v_norm_g, w_in=v_w_in, sgu_ln_g=v_sgu_ln_g, sgu_ln_b=v_sgu_ln_b, w_spatial=v_w_spatial,
             b_spatial=v_b_spatial, w_up_a=v_w_up_a, w_up_b=v_w_up_b, w_out=v_w_out, final_norm_g=v_final_norm_g)
    shapes = {n: w[n].shape for n in names}
    flat2d = lambda a: a.reshape(a.shape[-2:])

    proj, ht, *full = _in_proj_gather(x[0], norm_g, *[flat2d(w[n]) for n in big_names])
    loss, dx, big_grads, small, early_slots, late_in_slots = _local_step(
        proj, ht, x[0], loss_target[0], norm_g, full[0], sgu_ln_g[0], sgu_ln_b[0], w_spatial[0], b_spatial[0],
        full[1], full[2], full[3], final_norm_g, ATTN_Q_BLOCK, ATTN_K_BLOCK)
    packed = _pack_small(small, loss).reshape(N_DEV, SMALL_PIECE, LANES)
    red = _reduce_grads_tail(big_grads, packed, early_slots, late_in_slots)

    grads, deltas, new_m, new_v = {}, {}, {}, {}
    g_small = red[4].reshape(SMALL_ROWS, LANES)
    rows2d = lambda a: a.reshape(W_SPATIAL_ROWS, LANES)
    sets = [(flat2d(w[n]), g, flat2d(m[n]), flat2d(v[n]), 0) for n, g in zip(big_names, red[:4])]
    sets.append((rows2d(w_spatial), g_small, rows2d(m_w_spatial), rows2d(v_w_spatial), 0))
    sets.append((_pack_rest(w), g_small, _pack_rest(m), _pack_rest(v), W_SPATIAL_ROWS * ADAM_GRID // REST_ROWS))
    updated = _adamw(sets)
    for n, (g, d, nm, nv) in zip(big_names + ("w_spatial",), updated[:5]):
        grads[n], deltas[n], new_m[n], new_v[n] = (a.reshape(shapes[n]) for a in (g, d, nm, nv))
    g_rest, d, nm, nv = updated[5]
    for src, dst in ((g_rest, grads), (d, deltas), (nm, new_m), (nv, new_v)):
        dst.update(_unpack_rest(src, shapes))

    return (g_rest[_LOSS_ROW - W_SPATIAL_ROWS, 0], dx[None], *[grads[n] for n in names], *[deltas[n] for n in names],
            *[new_m[n] for n in names], *[new_v[n] for n in names])
```

```python
import math

import jax
import jax.numpy as jnp
from jax import lax
from jax.experimental import pallas as pl
from jax.experimental.pallas import tpu as pltpu

F32 = jnp.float32
BF16 = jnp.bfloat16

D_MODEL = 1024
N_HEADS = 8
HEAD_DIM = 64
D_BRANCH = 512
D_IN = 4 * D_BRANCH + 3 * D_BRANCH + 2 * D_MODEL
N_GROUPS = 8
GROUP_DIM = 64
SGU_CHUNK = 128
SGU_SUBCHUNK = 64
GROUP_SHIFT = 6
EPS = 1e-6
LANES = 128
ATTN_Q_BLOCK = 256
ATTN_K_BLOCK = 256
DEAD = -110.0
SKIPPED = -1e30
SCAN_PASSES = 1
ATTN_PAIRS = 2
N_CHIPS = 4
N_DEV = 8
MESH = pl.DeviceIdType.MESH

ADAM_LR = 0.001
ADAM_B1 = 0.9
ADAM_B2 = 0.999
ADAM_EPS = 1e-08
ADAM_WD = 0.01
ADAM_STEP = 10

COL_Q, COL_K, COL_V, COL_ZA, COL_UB, COL_VB, COL_ZB, COL_GA, COL_GB = 0, 1, 2, 3, 4, 5, 6, 7, 9

VMEM_LIMIT = 56 * 1024 * 1024

SMALL_ROWS = 1088
SMALL_PIECE = SMALL_ROWS // N_DEV


def _cparams(sem=None):
    return pltpu.CompilerParams(dimension_semantics=sem, vmem_limit_bytes=VMEM_LIMIT)


def _aligned(v, m):
    return v if isinstance(v, int) else pl.multiple_of(v, m)


def _sigmoid(x):
    return 1.0 / (1.0 + jnp.exp(-x))


def _gelu_and_grad(x):
    k = math.sqrt(2.0 / math.pi)
    x2 = x * x
    inner = k * (x + 0.044715 * x * x2)
    th = jnp.tanh(inner)
    g = 0.5 * x * (1.0 + th)
    dg = 0.5 * (1.0 + th) + 0.5 * x * (1.0 - th * th) * (k * (1.0 + 3.0 * 0.044715 * x2))
    return g, dg


def _split_dot(a, b_bf16, passes):
    out = None
    rem = a
    for _ in range(passes):
        part = rem.astype(BF16)
        d = jnp.dot(part, b_bf16, preferred_element_type=F32)
        out = d if out is None else out + d
        rem = rem - part.astype(F32)
    return out


def _dot_nt(a, b):
    return lax.dot_general(a, b, (((1,), (1,)), ((), ())), preferred_element_type=F32)


def _dot_tn(a, b):
    return lax.dot_general(a, b, (((0,), (0,)), ((), ())), preferred_element_type=F32)


def _place():
    x, y, c = lax.axis_index("x"), lax.axis_index("y"), lax.axis_index("c")
    return x, y, c


def _in_proj_gather(x, norm_g, w_in, w_up_a, w_up_b, w_out):
    s = x.shape[0]
    tm = min(1024, s)
    nt = s // tm
    shards = (w_in, w_up_a, w_up_b, w_out)
    n_arr = len(shards)
    col_sharded = (True, True, True, False)
    full_shapes = ((D_MODEL, D_IN), (D_BRANCH, D_MODEL), (D_BRANCH, D_MODEL), (D_MODEL, D_MODEL))
    w_shard = w_in.shape[1]
    half_rows = D_MODEL // 2
    stage_rows = 256

    def body(order_ref, x_ref, g_ref, *refs):
        src = refs[:n_arr]
        proj_ref, ht_ref = refs[n_arr:n_arr + 2]
        out = refs[n_arr + 2:2 * n_arr + 2]
        wsc, h_scr, stage = refs[2 * n_arr + 2:2 * n_arr + 5]
        small_stage = refs[2 * n_arr + 5:2 * n_arr + 8]
        small_cast = refs[2 * n_arr + 8:2 * n_arr + 11]
        send_sems, recv_sems, local_sems = refs[2 * n_arr + 11:]
        k = pl.program_id(0)
        i = pl.program_id(1)
        x_, y_, c = _place()
        chip = 2 * x_ + y_
        sibling = (x_, y_, 1 - c)
        others = [(x_, 1 - y_), (1 - x_, y_), (1 - x_, 1 - y_)]

        def region(a, chip_idx, half):
            if a == 0:
                return wsc.at[chip_idx, pl.ds(_aligned(half * half_rows, 16), half_rows), :]
            r, w = shards[a].shape
            hr = r // 2
            if col_sharded[a]:
                return out[a].at[pl.ds(_aligned(half * hr, 16), hr), pl.ds(_aligned(chip_idx * w, LANES), w)]
            return out[a].at[pl.ds(_aligned(chip_idx * r + half * hr, 16), hr), :]

        def remote(kk, a, chip_idx, half, to, own):
            s_ref = region(a, chip_idx, half)
            if own and a > 0:
                hr = shards[a].shape[0] // 2
                s_ref = small_cast[a - 1].at[pl.ds(_aligned(half * hr, 16), hr), :]
            return pltpu.make_async_remote_copy(src_ref=s_ref, dst_ref=region(a, chip_idx, half),
                                                send_sem=send_sems.at[kk], recv_sem=recv_sems.at[kk],
                                                device_id=to, device_id_type=MESH)

        def keep_whole(kk, chip_idx):
            return pltpu.make_async_copy(wsc.at[chip_idx],
                                         out[0].at[:, pl.ds(_aligned(chip_idx * w_shard, LANES), w_shard)],
                                         local_sems.at[kk])

        def small_stores():
            cps = []
            for a in range(1, n_arr):
                hr = shards[a].shape[0] // 2
                for half in range(2):
                    cps.append(pltpu.make_async_copy(small_cast[a - 1].at[pl.ds(half * hr, hr), :],
                                                     region(a, chip, half), local_sems.at[4 + 2 * (a - 1) + half]))
            return cps

        def arrive_and_pass(j):
            ochip = chip ^ j
            for a in range(n_arr):
                kk = n_arr * (j - 1) + a
                remote(kk, a, ochip, c, sibling, False).wait_recv()
                remote(3 * n_arr + kk, a, ochip, c, sibling, False).start()

        def from_sibling(j, a):
            remote(3 * n_arr + n_arr * (j - 1) + a, a, chip ^ j, 1 - c, sibling, False).wait_recv()

        @pl.when((k == 0) & (i == 0))
        def _():
            def cast_rows(half):
                for t in range(half_rows // stage_rows):
                    r0 = pl.multiple_of(half * half_rows + t * stage_rows, stage_rows)
                    pltpu.sync_copy(src[0].at[pl.ds(r0, stage_rows), :], stage)
                    wsc[chip, pl.ds(r0, stage_rows), :] = stage[...].astype(BF16)

            cast_rows(c)
            for j in (1, 2):
                remote(n_arr * (j - 1), 0, chip, c, (*others[j - 1], c), True).start()
            cast_rows(1 - c)
            for a in range(1, n_arr):
                pltpu.sync_copy(src[a], small_stage[a - 1])
                small_cast[a - 1][...] = small_stage[a - 1][...].astype(BF16)
            for j in (1, 2):
                for a in range(1, n_arr):
                    remote(n_arr * (j - 1) + a, a, chip, c, (*others[j - 1], c), True).start()
            keep_whole(0, chip).start()
            for cp in small_stores():
                cp.start()

        @pl.when((k == 1) & (i == 0))
        def _():
            for j in (1, 2):
                remote(n_arr * (j - 1), 0, chip, c, (*others[j - 1], c), True).wait_send()
            for a in range(n_arr):
                remote(n_arr * 2 + a, a, chip, c, (*others[2], c), True).start()
            arrive_and_pass(1)
            arrive_and_pass(2)
            from_sibling(1, 0)
            keep_whole(1, chip ^ 1).start()

        @pl.when((k == 2) & (i == 0))
        def _():
            from_sibling(2, 0)
            keep_whole(2, chip ^ 2).start()
            arrive_and_pass(3)

        @pl.when((k == 3) & (i == 0))
        def _():
            from_sibling(3, 0)
            keep_whole(3, chip ^ 3).start()

        @pl.when(k == 0)
        def _():
            xf = x_ref[...]
            r = lax.rsqrt(jnp.mean(xf * xf, axis=-1, keepdims=True) + EPS)
            h = xf * r * g_ref[...]
            h_scr[i] = h.astype(BF16)
            ht_ref[...] = h.T.astype(BF16)

        proj_ref[...] = jnp.dot(h_scr[i], wsc[order_ref[k]], preferred_element_type=F32).astype(BF16)

        @pl.when((k == 3) & (i == nt - 1))
        def _():
            for j in (1, 2, 3):
                for a in range(1, n_arr):
                    from_sibling(j, a)
            for j in (1, 2, 3):
                for a in range(n_arr):
                    kk = n_arr * (j - 1) + a
                    if a > 0 or j == 3:
                        remote(kk, a, chip, c, (*others[j - 1], c), True).wait_send()
                    remote(3 * n_arr + kk, a, chip ^ j, c, sibling, False).wait_send()
            for kk in range(4):
                keep_whole(kk, chip ^ kk).wait()
            for cp in small_stores():
                cp.wait()

    any_spec = pl.BlockSpec(memory_space=pl.ANY)
    tile = lambda kk, ii: jnp.where(kk == 0, ii, nt - 1)
    grid_spec = pltpu.PrefetchScalarGridSpec(
        num_scalar_prefetch=1, grid=(N_CHIPS, nt),
        in_specs=[pl.BlockSpec((tm, D_MODEL), lambda kk, ii, order: (tile(kk, ii), 0)),
                  pl.BlockSpec((1, D_MODEL), lambda kk, ii, order: (0, 0))] + [any_spec] * n_arr,
        out_specs=[pl.BlockSpec((tm, w_shard), lambda kk, ii, order: (ii, order[kk])),
                   pl.BlockSpec((D_MODEL, tm), lambda kk, ii, order: (0, tile(kk, ii)))] + [any_spec] * n_arr,
        scratch_shapes=[pltpu.VMEM((N_CHIPS, D_MODEL, w_shard), BF16), pltpu.VMEM((nt, tm, D_MODEL), BF16),
                        pltpu.VMEM((stage_rows, w_shard), F32)]
        + [pltpu.VMEM(a.shape, F32) for a in shards[1:]] + [pltpu.VMEM(a.shape, BF16) for a in shards[1:]]
        + [pltpu.SemaphoreType.DMA((6 * n_arr,)), pltpu.SemaphoreType.DMA((6 * n_arr,)),
           pltpu.SemaphoreType.DMA((4 + 2 * (n_arr - 1),))])
    x_, y_, _ = _place()
    order = (2 * x_ + y_) ^ jnp.arange(N_CHIPS, dtype=jnp.int32)
    return pl.pallas_call(
        body, name="in_proj_gather", grid_spec=grid_spec,
        out_shape=[jax.ShapeDtypeStruct((s, D_IN), BF16), jax.ShapeDtypeStruct((D_MODEL, s), BF16)]
        + [jax.ShapeDtypeStruct(sh, BF16) for sh in full_shapes],
        compiler_params=pltpu.CompilerParams(dimension_semantics=("arbitrary", "arbitrary"),
                                             vmem_limit_bytes=VMEM_LIMIT, has_side_effects=True),
    )(order, x, norm_g, *shards)


def _neg_softplus_parts(z):
    zb = z.astype(BF16)
    p = jnp.exp(-jnp.abs(zb))
    return p, jnp.maximum(zb, jnp.zeros_like(zb)) + jnp.log(1.0 + p)


def _split_cat(a, passes):
    parts = []
    rem = a
    for k in range(passes):
        part = rem.astype(BF16)
        parts.append(part)
        if k + 1 < passes:
            rem = rem - part.astype(F32)
    return parts[0] if passes == 1 else jnp.concatenate(parts, axis=1)


def _tri(blk, upper, sign):
    row = lax.broadcasted_iota(jnp.int32, (blk, blk), 0)
    col = lax.broadcasted_iota(jnp.int32, (blk, blk), 1)
    keep = (row <= col) if upper else (row >= col)
    t = jnp.where(keep, sign, 0.0).astype(BF16)
    return t if SCAN_PASSES == 1 else jnp.concatenate([t] * SCAN_PASSES, axis=0)


def _attn_fwd(proj, bq, bk, npairs):
    s = proj.shape[0]
    nq = s // bq
    ratio = bq // bk
    scale = HEAD_DIM ** -0.5
    heads = tuple(range(2 * npairs))
    width = LANES * npairs

    def body(q_ref, k_ref, v_ref, za_ref, tneg_ref, o_ref, ya_ref, rs_ref, acc_ref, r_ref):
        i = pl.program_id(1)
        lane = lax.broadcasted_iota(jnp.int32, (bq, LANES), 1)
        lo_half = lane < HEAD_DIM
        qm = []
        for pr in range(npairs):
            q = q_ref[:, LANES * pr:LANES * (pr + 1)] * jnp.asarray(scale, BF16)
            zero = jnp.zeros_like(q)
            qm += [jnp.where(lo_half, q, zero), jnp.where(lo_half, zero, q)]
        row = lax.broadcasted_iota(jnp.int32, (bq, bk), 0)
        col = lax.broadcasted_iota(jnp.int32, (bq, bk), 1)
        tneg = tneg_ref[...]
        acc_ref[...] = jnp.zeros_like(acc_ref)
        r_ref[...] = jnp.zeros_like(r_ref)
        rs_ref[...] = jnp.full_like(rs_ref, SKIPPED)

        def scores(j):
            ks = pl.multiple_of(j * bk, bk)
            return [_dot_nt(qm[h], k_ref[pl.ds(ks, bk), LANES * (h // 2):LANES * (h // 2 + 1)]) for h in heads]

        def block(j, diag, valid=None):
            ks = pl.multiple_of(j * bk, bk)
            vj = [v_ref[pl.ds(ks, bk), LANES * pr:LANES * (pr + 1)] for pr in range(npairs)]
            if diag:
                before = (j * bk + col) < (i * bq + row)
            z = scores(j)
            sp = [_neg_softplus_parts(z[h])[1] for h in heads]
            if diag:
                sp = [jnp.where(before, sp[h], 0.0) for h in heads]
            cin = [jnp.dot(_split_cat(sp[h], SCAN_PASSES), tneg, preferred_element_type=F32) for h in heads]
            w = [jnp.exp(z[h] + cin[h]) for h in heads]
            if diag:
                w = [jnp.where(before, w[h], 0.0) for h in heads]
            pv = [jnp.dot(w[h].astype(BF16), vj[h // 2], preferred_element_type=F32) for h in heads]
            r = [r_ref[h] for h in heads]
            keep = 1.0 if valid is None else valid.astype(F32)
            for h in heads:
                acc_ref[h] += pv[h] * (jnp.exp(r[h]) * keep)
                r_ref[h] = r[h] + cin[h][:, 0:1] * keep
            for pr in range(npairs):
                hit = [lane == j, lane == j + HEAD_DIM]
                if valid is not None:
                    hit = [m & valid for m in hit]
                rs_ref[pr] = jnp.where(hit[0], r[2 * pr], jnp.where(hit[1], r[2 * pr + 1], rs_ref[pr]))

        for t in range(ratio):
            block(i * ratio + ratio - 1 - t, True)
        block(jnp.maximum(i * ratio - 1, 0), False, valid=i > 0)

        def alive(carry):
            jj, r_max = carry
            return (jj < i * ratio - 1) & (r_max > DEAD)

        def loop_body(carry):
            jj, _ = carry
            block(i * ratio - 2 - jj, False)
            return jj + 1, jnp.max(r_ref[...])

        lax.while_loop(alive, loop_body, (0, jnp.max(r_ref[...])))
        for pr in range(npairs):
            cols = slice(LANES * pr, LANES * (pr + 1))
            o = jnp.where(lo_half, acc_ref[2 * pr], acc_ref[2 * pr + 1])
            o_ref[:, cols] = o.astype(BF16)
            za = za_ref[:, cols].astype(F32)
            ya_ref[:, cols] = (o * (za * _sigmoid(za))).astype(BF16)

    n_steps = N_HEADS // (2 * npairs)
    return pl.pallas_call(
        body, name="attn_fwd", grid=(n_steps, nq),
        in_specs=[pl.BlockSpec((bq, width), lambda p, i: (i, n_steps * COL_Q + p)),
                  pl.BlockSpec((s, width), lambda p, i: (0, n_steps * COL_K + p)),
                  pl.BlockSpec((s, width), lambda p, i: (0, n_steps * COL_V + p)),
                  pl.BlockSpec((bq, width), lambda p, i: (i, n_steps * COL_ZA + p)),
                  pl.BlockSpec((SCAN_PASSES * bk, bk), lambda p, i: (0, 0))],
        out_specs=[pl.BlockSpec((bq, width), lambda p, i: (i, p)),
                   pl.BlockSpec((bq, width), lambda p, i: (i, p)),
                   pl.BlockSpec((npairs, bq, LANES), lambda p, i: (p, i, 0))],
        out_shape=[jax.ShapeDtypeStruct((s, D_BRANCH), BF16), jax.ShapeDtypeStruct((s, D_BRANCH), BF16),
                   jax.ShapeDtypeStruct((N_HEADS // 2, s, LANES), F32)],
        scratch_shapes=[pltpu.VMEM((2 * npairs, bq, LANES), F32), pltpu.VMEM((2 * npairs, bq, 1), F32)],
        compiler_params=_cparams(("parallel", "parallel")),
    )(proj, proj, proj, proj, _tri(bk, False, -1.0))


_GRAD_COL_SHARDED = (True, True, True, False)
_GRAD_FULL_SHAPES = ((D_MODEL, D_IN), (D_BRANCH, D_MODEL), (D_BRANCH, D_MODEL), (D_MODEL, D_MODEL))
_GRAD_PIECE_SHAPES = tuple((r // 2, w // N_CHIPS) if cs else (r // (2 * N_CHIPS), w)
                           for (r, w), cs in zip(_GRAD_FULL_SHAPES, _GRAD_COL_SHARDED))
_EARLY_IN_DEVS = (4, 5, 6, 7)
_LATE_IN_DEVS = (0, 1, 2, 3)
_LATE_CHIPS = (0, 1)
_EARLY_CHIPS = (2, 3)
_ALL_CHIPS = (0, 1, 2, 3)


def _grad_piece(ref, a, dev):
    r, w = _GRAD_PIECE_SHAPES[a]
    if _GRAD_COL_SHARDED[a]:
        return ref.at[pl.ds((dev % 2) * r, r), pl.ds((dev // 2) * w, w)]
    return ref.at[pl.ds(dev * r, r), :]


def _dev_id(dev):
    return (dev // 4, (dev // 2) % 2, dev % 2)


def _me():
    return 4 * lax.axis_index("x") + 2 * lax.axis_index("y") + lax.axis_index("c")


def _presum_copy(src, dst, send_sem, recv_sem, to_dev):
    return pltpu.make_async_remote_copy(src_ref=src, dst_ref=dst, send_sem=send_sem, recv_sem=recv_sem,
                                        device_id=_dev_id(to_dev), device_id_type=MESH)


def _presum_hand_off(dev, a, dest_chips, g_ref, slots, pair, send_sems, recv_sems):
    chip, core = dev // 2, dev % 2
    cps = []
    for k, q in enumerate(dest_chips):
        piece = _grad_piece(g_ref, a, 2 * q + 1 - core)
        if q == chip:
            cps.append(_presum_copy(piece, slots.at[dev], send_sems.at[N_DEV + k], recv_sems.at[dev], dev ^ 1))
        else:
            cps.append(_presum_copy(piece, pair.at[k], send_sems.at[N_DEV + k], recv_sems.at[N_DEV + k], dev ^ 1))
    return cps


def _presum_sends(dev, a, dest_chips, slots, sums, send_sems, recv_sems):
    chip, core = dev // 2, dev % 2
    return [_presum_copy(sums.at[k], slots.at[dev], send_sems.at[2 * q + core], recv_sems.at[dev], 2 * q + core)
            for k, q in enumerate(dest_chips) if q != chip]


def _presum_loads(dev, a, dest_chips, g_ref, stage, load_sems):
    chip, core = dev // 2, dev % 2
    return [pltpu.make_async_copy(_grad_piece(g_ref, a, 2 * q + core), stage.at[k], load_sems.at[k])
            for k, q in enumerate(dest_chips) if q != chip]


def _presum_send(dev, a, dest_chips, g_ref, slots, pair, stage, sums, send_sems, recv_sems, load_sems):
    chip, core = dev // 2, dev % 2
    hand = _presum_hand_off(dev, a, dest_chips, g_ref, slots, pair, send_sems, recv_sems)
    for cp in _presum_loads(dev, a, dest_chips, g_ref, stage, load_sems):
        cp.wait()
    for k, q in enumerate(dest_chips):
        if q != chip:
            hand[k].wait_recv()
            sums[k] = (stage[k].astype(F32) + pair[k].astype(F32)).astype(BF16)
    for cp in _presum_sends(dev, a, dest_chips, slots, sums, send_sems, recv_sems):
        cp.start()


def _presum_wait(dev, a, dest_chips, g_ref, slots, pair, sums, send_sems, recv_sems):
    chip, core = dev // 2, dev % 2
    for cp in _presum_hand_off(dev, a, dest_chips, g_ref, slots, pair, send_sems, recv_sems):
        cp.wait_send()
    for cp in _presum_sends(dev, a, dest_chips, slots, sums, send_sems, recv_sems):
        cp.wait_send()
    if chip in dest_chips:
        for src_dev in _presum_sources(dev):
            _presum_copy(sums.at[0], slots.at[src_dev], send_sems.at[src_dev], recv_sems.at[src_dev], src_dev).wait_recv()


def _presum_sources(dev):
    return [dev ^ 1] + [2 * r + dev % 2 for r in range(N_CHIPS) if r != dev // 2]


def _presum_scratch(a, dest_chips):
    n = len(dest_chips)
    piece = _GRAD_PIECE_SHAPES[a]
    return [pltpu.VMEM((n,) + piece, BF16), pltpu.VMEM((n,) + piece, BF16), pltpu.VMEM((n,) + piece, BF16),
            pltpu.SemaphoreType.DMA((N_DEV + n,)), pltpu.SemaphoreType.DMA((N_DEV + n,)),
            pltpu.SemaphoreType.DMA((n,))]


PRESUM_SCRATCH = 6


def _presum_program(first, second, last, a, dest_chips, g_ref, slots, scratch):
    pair, stage, sums, send_sems, recv_sems, load_sems = scratch
    me = _me()

    @pl.when(first)
    def _():
        for dev in range(N_DEV):
            @pl.when(me == dev)
            def _():
                for cp in _presum_hand_off(dev, a, dest_chips, g_ref, slots, pair, send_sems, recv_sems):
                    cp.start()
                for cp in _presum_loads(dev, a, dest_chips, g_ref, stage, load_sems):
                    cp.start()

    @pl.when(second)
    def _():
        for dev in range(N_DEV):
            @pl.when(me == dev)
            def _():
                _presum_send(dev, a, dest_chips, g_ref, slots, pair, stage, sums, send_sems, recv_sems, load_sems)

    def finish():
        for dev in range(N_DEV):
            @pl.when(me == dev)
            def _():
                _presum_wait(dev, a, dest_chips, g_ref, slots, pair, sums, send_sems, recv_sems)

    return finish


def _attn_bwd(proj, do, rsave, bq, bk, npairs, grads):
    plan = ((0, _EARLY_CHIPS), (1, _ALL_CHIPS), (2, _ALL_CHIPS), (3, _ALL_CHIPS))
    s = proj.shape[0]
    nq = s // bq
    ratio = bq // bk
    scale = HEAD_DIM ** -0.5
    heads = tuple(range(2 * npairs))
    width = LANES * npairs

    n_steps = N_HEADS // (2 * npairs)
    n_g = len(grads)

    def body(q_ref, k_ref, v_ref, do_ref, rs_ref, tneg_ref, tfwd_ref, *refs):
        g_src = refs[:n_g]
        dq_ref, dk_ref, dv_ref = refs[n_g:n_g + 3]
        g_slots = refs[n_g + 3:2 * n_g + 3]
        dk_acc, dv_acc, dq_acc, e_ref = refs[2 * n_g + 3:2 * n_g + 7]
        i = pl.program_id(1)
        step = pl.program_id(0) * nq + i
        finish = [_presum_program(step == 0, step == 1, step == n_steps * nq - 1, a, chips, g_src[pos], g_slots[pos],
                                  refs[2 * n_g + 7 + PRESUM_SCRATCH * pos:2 * n_g + 7 + PRESUM_SCRATCH * (pos + 1)])
                  for pos, (a, chips) in enumerate(plan)]

        lane = lax.broadcasted_iota(jnp.int32, (bq, LANES), 1)
        lo_half = lane < HEAD_DIM
        qm, dom = [], []
        for pr in range(npairs):
            cols = slice(LANES * pr, LANES * (pr + 1))
            q = q_ref[:, cols] * jnp.asarray(scale, BF16)
            zero = jnp.zeros_like(q)
            qm += [jnp.where(lo_half, q, zero), jnp.where(lo_half, zero, q)]
            dout = do_ref[:, cols].astype(F32)
            dom += [jnp.where(lo_half, dout, 0.0), jnp.where(lo_half, 0.0, dout)]
        row = lax.broadcasted_iota(jnp.int32, (bq, bk), 0)
        col = lax.broadcasted_iota(jnp.int32, (bq, bk), 1)
        tneg = tneg_ref[...]
        tfwd = tfwd_ref[...]

        @pl.when(i == 0)
        def _():
            dk_acc[...] = jnp.zeros_like(dk_acc)
            dv_acc[...] = jnp.zeros_like(dv_acc)

        dq_acc[...] = jnp.zeros_like(dq_acc)
        e_ref[...] = jnp.zeros_like(e_ref)

        def block(j, diag, valid=None):
            ks = pl.multiple_of(j * bk, bk)
            kj = [k_ref[pl.ds(ks, bk), LANES * pr:LANES * (pr + 1)] for pr in range(npairs)]
            vj = [v_ref[pl.ds(ks, bk), LANES * pr:LANES * (pr + 1)] for pr in range(npairs)]
            if diag:
                before = (j * bk + col) < (i * bq + row)
            z = [_dot_nt(qm[h], kj[h // 2]) for h in heads]
            er = [jnp.exp(jnp.sum(jnp.where(lane == j + HEAD_DIM * (h % 2), rs_ref[h // 2], 0.0), axis=-1,
                                  keepdims=True)) for h in heads]
            if valid is not None:
                er = [er[h] * valid.astype(F32) for h in heads]
            dos = [(dom[h] * er[h]).astype(BF16) for h in heads]
            dw = [_dot_nt(dos[h], vj[h // 2]) for h in heads]
            psp = [_neg_softplus_parts(z[h]) for h in heads]
            sp = [psp[h][1] for h in heads]
            if diag:
                sp = [jnp.where(before, sp[h], 0.0) for h in heads]
            cin = [jnp.dot(_split_cat(sp[h], SCAN_PASSES), tneg, preferred_element_type=F32) for h in heads]
            w = [jnp.exp(z[h] + cin[h]) for h in heads]
            if diag:
                w = [jnp.where(before, w[h], 0.0) for h in heads]
            e =[dw[h] * w[h] for h in heads]
            eincl = [jnp.dot(_split_cat(e[h], SCAN_PASSES), tfwd, preferred_element_type=F32) + e_ref[h]
                     for h in heads]
            dz = []
            for h in heads:
                p = psp[h][0]
                beta = jnp.where(z[h] >= 0.0, 1.0, p) / (1.0 + p)
                d = e[h] - beta * eincl[h]
                dz.append((jnp.where(before, d, 0.0) if diag else d).astype(BF16))
            wb = [w[h].astype(BF16) for h in heads]
            for h in heads:
                e_ref[h] = eincl[h][:, bk - 1:bk]
                dq_acc[h] += jnp.dot(dz[h], kj[h // 2], preferred_element_type=F32)
            for pr in range(npairs):
                cols = slice(LANES * pr, LANES * (pr + 1))
                h0, h1 = 2 * pr, 2 * pr + 1
                dk_acc[pl.ds(ks, bk), cols] += _dot_tn(dz[h0], qm[h0]) + _dot_tn(dz[h1], qm[h1])
                dv_acc[pl.ds(ks, bk), cols] += _dot_tn(wb[h0], dos[h0]) + _dot_tn(wb[h1], dos[h1])

        def loop_body(j, carry):
            block(j, False)
            return carry

        block_of_lane = lane & (HEAD_DIM - 1)
        live = jnp.max(rs_ref[...], axis=0) > DEAD
        first_live = jnp.min(jnp.where(live, block_of_lane, nq * ratio))
        last = jnp.maximum(i * ratio - 1, 0)
        lax.fori_loop(jnp.minimum(first_live, last), last, loop_body, 0)
        block(last, False, valid=i > 0)
        for t in range(ratio):
            block(i * ratio + t, True)
        for pr in range(npairs):
            dq = jnp.where(lo_half, dq_acc[2 * pr], dq_acc[2 * pr + 1]) * scale
            dq_ref[:, LANES * pr:LANES * (pr + 1)] = dq.astype(BF16)

        @pl.when(i == nq - 1)
        def _():
            dk_ref[...] = dk_acc[...].astype(BF16)
            dv_ref[...] = dv_acc[...].astype(BF16)

        @pl.when(step == n_steps * nq - 1)
        def _():
            for fin in finish:
                fin()

    any_spec = pl.BlockSpec(memory_space=pl.ANY)
    return pl.pallas_call(
        body, name="attn_bwd", grid=(n_steps, nq),
        in_specs=[pl.BlockSpec((bq, width), lambda p, i: (i, n_steps * COL_Q + p)),
                  pl.BlockSpec((s, width), lambda p, i: (0, n_steps * COL_K + p)),
                  pl.BlockSpec((s, width), lambda p, i: (0, n_steps * COL_V + p)),
                  pl.BlockSpec((bq, width), lambda p, i: (i, p)),
                  pl.BlockSpec((npairs, bq, LANES), lambda p, i: (p, i, 0)),
                  pl.BlockSpec((SCAN_PASSES * bk, bk), lambda p, i: (0, 0)),
                  pl.BlockSpec((SCAN_PASSES * bk, bk), lambda p, i: (0, 0))] + [any_spec] * n_g,
        out_specs=[pl.BlockSpec((bq, width), lambda p, i: (i, p)),
                   pl.BlockSpec((s, width), lambda p, i: (0, p)),
                   pl.BlockSpec((s, width), lambda p, i: (0, p))] + [any_spec] * n_g,
        out_shape=[jax.ShapeDtypeStruct((s, D_BRANCH), BF16)] * 3
        + [jax.ShapeDtypeStruct((N_DEV,) + _GRAD_PIECE_SHAPES[a], BF16) for a, _ in plan],
        scratch_shapes=[pltpu.VMEM((s, width), F32), pltpu.VMEM((s, width), F32),
                        pltpu.VMEM((2 * npairs, bq, LANES), F32), pltpu.VMEM((2 * npairs, bq, 1), F32)]
        + [sh for a, chips in plan for sh in _presum_scratch(a, chips)],
        compiler_params=pltpu.CompilerParams(dimension_semantics=("arbitrary", "arbitrary"),
                                             vmem_limit_bytes=VMEM_LIMIT, has_side_effects=True),
    )(proj, proj, proj, do, rsave, _tri(bk, False, -1.0), _tri(bk, True, 1.0), *grads)


def _group_avg_matrix():
    a = lax.broadcasted_iota(jnp.int32, (LANES, LANES), 0) >> GROUP_SHIFT
    b = lax.broadcasted_iota(jnp.int32, (LANES, LANES), 1) >> GROUP_SHIFT
    return jnp.where(a == b, 1.0 / GROUP_DIM, 0.0).astype(BF16)


def _group_mean(a, avg):
    parts = [_split_dot(a[:, LANES * k:LANES * (k + 1)], avg, 2) for k in range(D_BRANCH // LANES)]
    return jnp.concatenate(parts, axis=1)


def _sgu_forward_parts(ub, vb, ln_g, ln_b, avg):
    ug, dug = _gelu_and_grad(ub)
    vg, dvg = _gelu_and_grad(vb)
    mu = _group_mean(vg, avg)
    d = vg - mu
    var = _group_mean(d * d, avg)
    rstd = lax.rsqrt(var + EPS)
    vhat = d * rstd
    vn = vhat * ln_g + ln_b
    return ug, dug, dvg, rstd, vhat, vn


def _sgu_mix(w_ref, src_bf16, n_chunks):
    lane = lax.broadcasted_iota(jnp.int32, (SGU_CHUNK, LANES), 1)
    lo_half = lane < GROUP_DIM
    rows = []
    for n in range(n_chunks):
        slabs = []
        for a in range(D_BRANCH // LANES):
            blk = src_bf16[SGU_CHUNK * n:SGU_CHUNK * (n + 1), LANES * a:LANES * (a + 1)]
            zero = jnp.zeros_like(blk)
            m0 = jnp.dot(w_ref[2 * a], jnp.where(lo_half, blk, zero), preferred_element_type=F32)
            m1 = jnp.dot(w_ref[2 * a + 1], jnp.where(lo_half, zero, blk), preferred_element_type=F32)
            slabs.append(m0 + m1)
        rows.append(jnp.concatenate(slabs, axis=1))
    return jnp.concatenate(rows, axis=0)


def _sgu_fwd(proj, ln_g, ln_b, w_mask, bias_full):
    s = proj.shape[0]
    tm = min(512, s)
    n_chunks = tm // SGU_CHUNK

    def body(ub_ref, vb_ref, zb_ref, g_ref, b_ref, w_ref, bias_ref, yb_ref):
        avg = _group_avg_matrix()
        ug, _, _, _, _, vn = _sgu_forward_parts(ub_ref[...].astype(F32), vb_ref[...].astype(F32),
                                                g_ref[...], b_ref[...], avg)
        mixed = _sgu_mix(w_ref, vn.astype(BF16), n_chunks) + jnp.concatenate([bias_ref[...]] * n_chunks, axis=0)
        zb = zb_ref[...].astype(F32)
        yb_ref[...] = (ug * mixed * (zb * _sigmoid(zb))).astype(BF16)

    col = lambda c: pl.BlockSpec((tm, D_BRANCH), lambda i: (i, c))
    full = lambda shape: pl.BlockSpec(shape, lambda i: (0,) * len(shape))
    return pl.pallas_call(
        body, name="sgu_fwd", grid=(s // tm,),
        in_specs=[col(COL_UB), col(COL_VB), col(COL_ZB), full((1, D_BRANCH)), full((1, D_BRANCH)),
                  full((N_GROUPS, SGU_CHUNK, SGU_CHUNK)), full((SGU_CHUNK, D_BRANCH))],
        out_specs=pl.BlockSpec((tm, D_BRANCH), lambda i: (i, 0)),
        out_shape=jax.ShapeDtypeStruct((s, D_BRANCH), BF16),
        compiler_params=_cparams(("parallel",)),
    )(proj, proj, proj, ln_g, ln_b, w_mask, bias_full)


def _sgu_bwd(proj, dyb, ln_g, ln_b, w_mask, w_mask_t, bias_full):
    s = proj.shape[0]
    tm = min(512, s)
    n_chunks = tm // SGU_CHUNK
    n_steps = s // tm

    def body(ub_ref, vb_ref, zb_ref, dyb_ref, g_ref, b_ref, w_ref, wt_ref, bias_ref,
             dsgu_ref, dw_ref, db_ref, dg_ref, dbeta_ref, dmix_acc):
        i = pl.program_id(0)

        @pl.when(i == 0)
        def _():
            dw_ref[...] = jnp.zeros_like(dw_ref)
            dg_ref[...] = jnp.zeros_like(dg_ref)
            dbeta_ref[...] = jnp.zeros_like(dbeta_ref)
            dmix_acc[...] = jnp.zeros_like(dmix_acc)

        avg = _group_avg_matrix()
        ln_gv = g_ref[...]
        ug, dug, dvg, rstd, vhat, vn = _sgu_forward_parts(ub_ref[...].astype(F32), vb_ref[...].astype(F32),
                                                          ln_gv, b_ref[...], avg)
        vnb = vn.astype(BF16)
        mixed = _sgu_mix(w_ref, vnb, n_chunks) + jnp.concatenate([bias_ref[...]] * n_chunks, axis=0)
        zb = zb_ref[...].astype(F32)
        sg = _sigmoid(zb)
        sz = zb * sg
        dsz = sg * (1.0 + zb * (1.0 - sg))
        dy = dyb_ref[...].astype(F32)
        dmixed = dy * ug * sz
        du = dy * mixed * sz * dug
        dzb = dy * ug * mixed * dsz
        dmb = dmixed.astype(BF16)
        dvn = _sgu_mix(wt_ref, dmb, n_chunks)

        lane = lax.broadcasted_iota(jnp.int32, (SGU_CHUNK, LANES), 1)
        lo_half = lane < GROUP_DIM
        dm_sum = None
        for n in range(n_chunks):
            rows = slice(SGU_CHUNK * n, SGU_CHUNK * (n + 1))
            dm_sum = dmixed[rows] if dm_sum is None else dm_sum + dmixed[rows]
            for a in range(D_BRANCH // LANES):
                cols = slice(LANES * a, LANES * (a + 1))
                dblk = dmb[rows, cols]
                vblk = vnb[rows, cols]
                zero = jnp.zeros_like(dblk)
                dw_ref[2 * a] += _dot_nt(jnp.where(lo_half, dblk, zero), vblk)
                dw_ref[2 * a + 1] += _dot_nt(jnp.where(lo_half, zero, dblk), vblk)
        dmix_acc[...] += dm_sum

        dg_ref[...] += jnp.sum(dvn * vhat, axis=0, keepdims=True)
        dbeta_ref[...] += jnp.sum(dvn, axis=0, keepdims=True)
        dvh = dvn * ln_gv
        m1 = _group_mean(dvh, avg)
        m2 = _group_mean(dvh * vhat, avg)
        dv = rstd * (dvh - m1 - vhat * m2) * dvg
        dsgu_ref[:, 0:D_BRANCH] = du.astype(BF16)
        dsgu_ref[:, D_BRANCH:2 * D_BRANCH] = dv.astype(BF16)
        dsgu_ref[:, 2 * D_BRANCH:3 * D_BRANCH] = dzb.astype(BF16)

        @pl.when(i == n_steps - 1)
        def _():
            pos = lax.broadcasted_iota(jnp.int32, (SGU_CHUNK, SGU_CHUNK), 0) >> GROUP_SHIFT
            src = lax.broadcasted_iota(jnp.int32, (SGU_CHUNK, SGU_CHUNK), 1) >> GROUP_SHIFT
            keep = src <= pos
            for g in range(N_GROUPS):
                dw_ref[g] = jnp.where(keep, dw_ref[g], 0.0)
            grp = lax.broadcasted_iota(jnp.int32, (D_BRANCH, LANES), 0) >> GROUP_SHIFT
            sel = (grp == lax.broadcasted_iota(jnp.int32, (D_BRANCH, LANES), 1)).astype(BF16)
            db_ref[...] = _split_dot(dmix_acc[...], sel, 3)

    col = lambda c: pl.BlockSpec((tm, D_BRANCH), lambda i: (i, c))
    full = lambda shape: pl.BlockSpec(shape, lambda i: (0,) * len(shape))
    return pl.pallas_call(
        body, name="sgu_bwd", grid=(n_steps,),
        in_specs=[col(COL_UB), col(COL_VB), col(COL_ZB), pl.BlockSpec((tm, D_BRANCH), lambda i: (i, 0)),
                  full((1, D_BRANCH)), full((1, D_BRANCH)),
                  full((N_GROUPS, SGU_CHUNK, SGU_CHUNK)), full((N_GROUPS, SGU_CHUNK, SGU_CHUNK)),
                  full((SGU_CHUNK, D_BRANCH))],
        out_specs=[pl.BlockSpec((tm, 3 * D_BRANCH), lambda i: (i, 0)),
                   full((N_GROUPS, SGU_CHUNK, SGU_CHUNK)), full((SGU_CHUNK, LANES)),
                   full((1, D_BRANCH)), full((1, D_BRANCH))],
        out_shape=[jax.ShapeDtypeStruct((s, 3 * D_BRANCH), BF16),
                   jax.ShapeDtypeStruct((N_GROUPS, SGU_CHUNK, SGU_CHUNK), F32),
                   jax.ShapeDtypeStruct((SGU_CHUNK, LANES), F32),
                   jax.ShapeDtypeStruct((1, D_BRANCH), F32), jax.ShapeDtypeStruct((1, D_BRANCH), F32)],
        scratch_shapes=[pltpu.VMEM((SGU_CHUNK, D_BRANCH), F32)],
        compiler_params=_cparams(("arbitrary",)),
    )(proj, proj, proj, dyb, ln_g, ln_b, w_mask, w_mask_t, bias_full)


def _mid(proj, ya, yb, o, x, target, final_g, w_up_a, w_up_b, w_out):
    s = x.shape[0]
    tm = min(256, s)
    n_steps = s // tm
    half = D_MODEL // 2

    def body(ya_ref, yb_ref, o_ref, za_ref, ga0_ref, ga1_ref, gb0_ref, gb1_ref, x_ref, t_ref, gf_ref,
             wa_ref, wb_ref, wo_ref,
             dzg_ref, do_ref, dyb_ref, dx2_ref, gwo_ref, gwa_ref, gwb_ref, loss_ref, dgf_ref,
             acc_o, acc_a, acc_b):
        i = pl.program_id(0)

        @pl.when(i == 0)
        def _():
            acc_o[...] = jnp.zeros_like(acc_o)
            acc_a[...] = jnp.zeros_like(acc_a)
            acc_b[...] = jnp.zeros_like(acc_b)
            loss_ref[...] = jnp.zeros_like(loss_ref)
            dgf_ref[...] = jnp.zeros_like(dgf_ref)

        ya_v = ya_ref[...]
        yb_v = yb_ref[...]
        pa = jnp.dot(ya_v, wa_ref[...], preferred_element_type=F32)
        pb = jnp.dot(yb_v, wb_ref[...], preferred_element_type=F32)
        sa = _sigmoid(jnp.concatenate([ga0_ref[...], ga1_ref[...]], axis=1).astype(F32))
        sb = _sigmoid(jnp.concatenate([gb0_ref[...], gb1_ref[...]], axis=1).astype(F32))
        merged = (sa * pa + sb * pb).astype(BF16)
        x2 = x_ref[...] + jnp.dot(merged, wo_ref[...], preferred_element_type=F32)
        r2 = lax.rsqrt(jnp.mean(x2 * x2, axis=-1, keepdims=True) + EPS)
        xh = x2 * r2
        gf = gf_ref[...]
        diff = xh * gf - t_ref[...]
        loss_ref[...] += 0.5 * jnp.sum(jnp.mean(diff * diff, axis=-1, keepdims=True))
        dy = diff * (1.0 / D_MODEL)
        dgf_ref[...] += jnp.sum(dy * xh, axis=0, keepdims=True)
        dyg = dy * gf
        dx2 = r2 * (dyg - xh * jnp.mean(dyg * xh, axis=-1, keepdims=True))
        dx2_ref[...] = dx2
        dx2b = dx2.astype(BF16)
        dmerged = _dot_nt(dx2b, wo_ref[...])
        acc_o[...] += _dot_tn(merged, dx2b)
        dpa = dmerged * sa
        dpb = dmerged * sb
        dzg_ref[:, D_BRANCH:D_BRANCH + D_MODEL] = (dpa * pa * (1.0 - sa)).astype(BF16)
        dzg_ref[:, D_BRANCH + D_MODEL:D_BRANCH + 2 * D_MODEL] = (dpb * pb * (1.0 - sb)).astype(BF16)
        dpab = dpa.astype(BF16)
        dpbb = dpb.astype(BF16)
        acc_a[...] += _dot_tn(ya_v, dpab)
        acc_b[...] += _dot_tn(yb_v, dpbb)
        dya = _dot_nt(dpab, wa_ref[...])
        dyb_ref[...] = _dot_nt(dpbb, wb_ref[...]).astype(BF16)
        za = za_ref[...].astype(F32)
        sg = _sigmoid(za)
        do_ref[...] = (dya * (za * sg)).astype(BF16)
        dzg_ref[:, 0:D_BRANCH] = (dya * o_ref[...].astype(F32) * (sg * (1.0 + za * (1.0 - sg)))).astype(BF16)

        @pl.when(i == n_steps - 1)
        def _():
            gwo_ref[...] = acc_o[...].astype(BF16)
            gwa_ref[...] = acc_a[...].astype(BF16)
            gwb_ref[...] = acc_b[...].astype(BF16)

    tok = lambda w: pl.BlockSpec((tm, w), lambda i: (i, 0))
    col = lambda c: pl.BlockSpec((tm, half), lambda i: (i, c))
    full = lambda shape: pl.BlockSpec(shape, lambda i: (0,) * len(shape))
    return pl.pallas_call(
        body, name="mid", grid=(n_steps,),
        in_specs=[tok(D_BRANCH), tok(D_BRANCH), tok(D_BRANCH), col(COL_ZA), col(COL_GA), col(COL_GA + 1),
                  col(COL_GB), col(COL_GB + 1), tok(D_MODEL), tok(D_MODEL), full((1, D_MODEL)),
                  full((D_BRANCH, D_MODEL)), full((D_BRANCH, D_MODEL)), full((D_MODEL, D_MODEL))],
        out_specs=[tok(D_BRANCH + 2 * D_MODEL), tok(D_BRANCH), tok(D_BRANCH), tok(D_MODEL),
                   full((D_MODEL, D_MODEL)), full((D_BRANCH, D_MODEL)), full((D_BRANCH, D_MODEL)),
                   full((8, LANES)), full((1, D_MODEL))],
        out_shape=[jax.ShapeDtypeStruct((s, D_BRANCH + 2 * D_MODEL), BF16),
                   jax.ShapeDtypeStruct((s, D_BRANCH), BF16), jax.ShapeDtypeStruct((s, D_BRANCH), BF16),
                   jax.ShapeDtypeStruct((s, D_MODEL), F32),
                   jax.ShapeDtypeStruct((D_MODEL, D_MODEL), BF16),
                   jax.ShapeDtypeStruct((D_BRANCH, D_MODEL), BF16), jax.ShapeDtypeStruct((D_BRANCH, D_MODEL), BF16),
                   jax.ShapeDtypeStruct((8, LANES), F32), jax.ShapeDtypeStruct((1, D_MODEL), F32)],
        scratch_shapes=[pltpu.VMEM((D_MODEL, D_MODEL), F32), pltpu.VMEM((D_BRANCH, D_MODEL), F32),
                        pltpu.VMEM((D_BRANCH, D_MODEL), F32)],
        compiler_params=_cparams(("arbitrary",)),
    )(ya, yb, o, proj, proj, proj, proj, proj, x, target, final_g, w_up_a, w_up_b, w_out)


def _dwin_early(ht, first, tile_of_first, second, tile_of_second):
    s = ht.shape[1]
    n1 = first.shape[1] // D_BRANCH
    n2 = second.shape[1] // D_BRANCH

    def body(ht_ref, a_ref, b_ref, out_ref):
        j = pl.program_id(0)

        @pl.when(j < n1)
        def _():
            out_ref[...] = jnp.dot(ht_ref[...], a_ref[...], preferred_element_type=F32).astype(BF16)

        @pl.when(j >= n1)
        def _():
            out_ref[...] = jnp.dot(ht_ref[...], b_ref[...], preferred_element_type=F32).astype(BF16)

    return pl.pallas_call(
        body, name="dwin_early", grid=(n1 + n2,),
        in_specs=[pl.BlockSpec((D_MODEL, s), lambda j: (0, 0)),
                  pl.BlockSpec((s, D_BRANCH), lambda j: (0, jnp.minimum(j, n1 - 1))),
                  pl.BlockSpec((s, D_BRANCH), lambda j: (0, jnp.maximum(j - n1, 0)))],
        out_specs=pl.BlockSpec((D_MODEL, D_BRANCH),
                               lambda j: (0, jnp.where(j < n1, tile_of_first(j), tile_of_second(j - n1)))),
        out_shape=jax.ShapeDtypeStruct((D_MODEL, D_IN), BF16),
        compiler_params=_cparams(("arbitrary",)),
    )(ht, first, second)


def _dwin_pieces(ht, pieces, first_tile, prev):
    s = ht.shape[1]
    n = len(pieces)

    def body(ht_ref, *refs):
        srcs = refs[:n]
        out_ref, buf, sems = refs[n + 1:]
        j = pl.program_id(0)

        @pl.when(j == 0)
        def _():
            for k in range(n):
                pltpu.make_async_copy(srcs[k], buf.at[k], sems.at[k]).start()

        for k in range(n):
            @pl.when(j == k)
            def _():
                pltpu.make_async_copy(srcs[k], buf.at[k], sems.at[k]).wait()

        out_ref[...] = jnp.dot(ht_ref[...], buf[j], preferred_element_type=F32).astype(BF16)

    any_spec = pl.BlockSpec(memory_space=pl.ANY)
    return pl.pallas_call(
        body, name="dwin_pieces", grid=(n,),
        in_specs=[pl.BlockSpec((D_MODEL, s), lambda j: (0, 0))] + [any_spec] * (n + 1),
        out_specs=pl.BlockSpec((D_MODEL, D_BRANCH), lambda j: (0, first_tile + j)),
        out_shape=jax.ShapeDtypeStruct((D_MODEL, D_IN), BF16),
        scratch_shapes=[pltpu.VMEM((n, s, D_BRANCH), BF16), pltpu.SemaphoreType.DMA((n,))],
        input_output_aliases={n + 1: 0},
        compiler_params=_cparams(("arbitrary",)),
    )(ht, *pieces, prev)


def _dh_dx(pieces, w_in, x, norm_g, dx2, g_in):
    s = x.shape[0]
    tm = min(256, s)
    n_steps = s // tm
    arrays = []
    for arr, _, _, _ in pieces:
        if not any(arr is a for a in arrays):
            arrays.append(arr)
    n_arr = len(arrays)
    plan = [([k for k, a in enumerate(arrays) if a is arr][0], wcol, off, width) for arr, wcol, off, width in pieces]

    def body(*refs):
        p_refs = refs[:n_arr]
        w_ref, x_ref, g_ref, dx2_ref, gin_ref, dx_ref, dg_ref, late_ref = refs[n_arr:n_arr + 8]
        step = pl.program_id(0)
        finish = _presum_program(step == 0, step == min(1, n_steps - 1), step == n_steps - 1, 0, _LATE_CHIPS,
                                 gin_ref, late_ref, refs[n_arr + 8:])

        @pl.when(step == 0)
        def _():
            dg_ref[...] = jnp.zeros_like(dg_ref)

        dh = None
        for k, wcol, off, width in plan:
            d = _dot_nt(p_refs[k][:, off:off + width], w_ref[:, wcol:wcol + width])
            dh = d if dh is None else dh + d
        xf = x_ref[...]
        r = lax.rsqrt(jnp.mean(xf * xf, axis=-1, keepdims=True) + EPS)
        xh = xf * r
        dg_ref[...] += jnp.sum(dh * xh, axis=0, keepdims=True)
        dhg = dh * g_ref[...]
        dx_ref[...] = r * (dhg - xh * jnp.mean(dhg * xh, axis=-1, keepdims=True)) + dx2_ref[...]

        @pl.when(step == n_steps - 1)
        def _():
            finish()

    tok = lambda w: pl.BlockSpec((tm, w), lambda i: (i, 0))
    full = lambda shape: pl.BlockSpec(shape, lambda i: (0,) * len(shape))
    any_spec = pl.BlockSpec(memory_space=pl.ANY)
    return pl.pallas_call(
        body, name="dh_dx", grid=(n_steps,),
        in_specs=[tok(a.shape[1]) for a in arrays] + [full((D_MODEL, D_IN)), tok(D_MODEL), full((1, D_MODEL)),
                                                      tok(D_MODEL), any_spec],
        out_specs=[tok(D_MODEL), full((1, D_MODEL)), any_spec],
        out_shape=[jax.ShapeDtypeStruct((s, D_MODEL), F32), jax.ShapeDtypeStruct((1, D_MODEL), F32),
                   jax.ShapeDtypeStruct((N_DEV,) + _GRAD_PIECE_SHAPES[0], BF16)],
        scratch_shapes=_presum_scratch(0, _LATE_CHIPS),
        compiler_params=pltpu.CompilerParams(dimension_semantics=("arbitrary",), vmem_limit_bytes=VMEM_LIMIT,
                                             has_side_effects=True),
    )(*arrays, w_in, x, norm_g, dx2, g_in)


ADAM_GRID = 8


def _adamw(sets):
    n = len(sets)
    c1 = 1.0 - ADAM_B1 ** ADAM_STEP
    c2 = 1.0 - ADAM_B2 ** ADAM_STEP

    def body(*refs):
        for k in range(n):
            w_ref, g_ref, m_ref, v_ref = refs[4 * k:4 * k + 4]
            g_out_ref, d_ref, nm_ref, nv_ref = refs[4 * n + 4 * k:4 * n + 4 * k + 4]
            gv = g_ref[...]
            g_out_ref[...] = gv
            nm = ADAM_B1 * m_ref[...] + (1.0 - ADAM_B1) * gv
            nv = ADAM_B2 * v_ref[...] + (1.0 - ADAM_B2) * (gv * gv)
            d_ref[...] = -ADAM_LR * ((nm / c1) / (jnp.sqrt(nv / c2) + ADAM_EPS) + ADAM_WD * w_ref[...])
            nm_ref[...] = nm
            nv_ref[...] = nv

    in_specs, out_specs, shapes, args = [], [], [], []
    for w, g, m, v, g_tile0 in sets:
        rows, cols = w.shape
        assert rows % (8 * ADAM_GRID) == 0, (rows, cols)
        spec = pl.BlockSpec((rows // ADAM_GRID, cols), lambda i: (i, 0))
        g_spec = pl.BlockSpec((rows // ADAM_GRID, cols), lambda i, t0=g_tile0: (t0 + i, 0))
        in_specs += [spec, g_spec, spec, spec]
        out_specs += [spec] * 4
        shapes += [jax.ShapeDtypeStruct((rows, cols), F32)] * 4
        args += [w, g, m, v]
    outs = pl.pallas_call(
        body, name="adamw", grid=(ADAM_GRID,),
        in_specs=in_specs, out_specs=out_specs, out_shape=shapes,
        compiler_params=_cparams(("parallel",)),
    )(*args)
    return [outs[4 * k:4 * k + 4] for k in range(n)]


def _reduce_grads_tail(grads, g_small, early_slots, late_in_slots):
    n_big = len(grads)
    n_arr = n_big + 1
    shard_shapes = [(2 * r, w) for r, w in _GRAD_PIECE_SHAPES]
    small_piece = (SMALL_PIECE, LANES)

    def body(*refs):
        src = refs[:n_arr]
        early = refs[n_arr:n_arr + n_big]
        late_in = refs[n_arr + n_big]
        n_in = n_arr + n_big + 1
        out = refs[n_in:n_in + n_arr]
        slots = refs[n_in + n_arr:n_in + 2 * n_arr]
        sums = refs[n_in + 2 * n_arr:n_in + 3 * n_arr]
        send1, recv1, send2, recv2, local_sems = refs[n_in + 3 * n_arr:]
        x, y, c = _place()
        me = 4 * x + 2 * y + c

        def piece_of(a, dev):
            return src[a].at[dev] if a == n_big else _grad_piece(src[a], a, dev)

        def late(a, dst_dev, src_dev):
            return pltpu.make_async_remote_copy(
                src_ref=piece_of(a, dst_dev), dst_ref=slots[a].at[src_dev],
                send_sem=send1.at[n_arr * dst_dev + a], recv_sem=recv1.at[n_arr * src_dev + a],
                device_id=_dev_id(dst_dev), device_id_type=MESH)

        def late_arrays(dev):
            return (n_big,)

        def load(a, dev, received):
            return pltpu.make_async_copy(received.at[dev], slots[a].at[dev], local_sems.at[n_arr * dev + a])

        def own(a, dev):
            return pltpu.make_async_copy(piece_of(a, dev), slots[a].at[dev], local_sems.at[n_arr * dev + a])

        for dev in range(N_DEV):
            @pl.when(me == dev)
            def _():
                received = [early[0] if dev in _EARLY_IN_DEVS else late_in] + list(early[1:])
                for a in range(n_arr):
                    own(a, dev).start()
                sources = sorted([dev] + _presum_sources(dev))
                for peer in range(N_DEV):
                    if peer != dev:
                        for a in late_arrays(peer):
                            late(a, peer, dev).start()
                        for a in range(n_big):
                            if peer in sources:
                                load(a, peer, received[a]).start()
                for a in range(n_arr):
                    own(a, dev).wait()
                for peer in range(N_DEV):
                    if peer != dev:
                        for a in late_arrays(dev):
                            late(a, dev, peer).wait_recv()
                        for a in range(n_big):
                            if peer in sources:
                                load(a, peer, received[a]).wait()
                for a in range(n_arr):
                    rows = slots[a].shape[1]
                    step = 64 if rows % 64 == 0 else 8
                    used = sources if a < n_big else list(range(N_DEV))

                    def add_rows(t, carry, a=a, step=step, used=used):
                        r0 = pl.multiple_of(t * step, step)
                        total = slots[a][used[0], pl.ds(r0, step), :].astype(F32)
                        for src_dev in used[1:]:
                            total = total + slots[a][src_dev, pl.ds(r0, step), :].astype(F32)
                        sums[a][pl.ds(r0, step), :] = total
                        return carry

                    lax.fori_loop(0, rows // step, add_rows, 0)

        shares = []
        keeps = []
        for a in range(n_big):
            r, w = _GRAD_PIECE_SHAPES[a]
            dst = out[a].at[pl.ds(pl.multiple_of(c * r, 8), r), :]
            cp = pltpu.make_async_remote_copy(src_ref=sums[a], dst_ref=dst, send_sem=send2.at[a], recv_sem=recv2.at[a],
                                              device_id=(x, y, 1 - c), device_id_type=MESH)
            cp.start()
            shares.append(cp)
            kp = pltpu.make_async_copy(sums[a], dst, local_sems.at[N_DEV * n_arr + a])
            kp.start()
            keeps.append(kp)
        kp = pltpu.make_async_copy(sums[n_big], out[n_big].at[me], local_sems.at[N_DEV * n_arr + n_big])
        kp.start()
        keeps.append(kp)

        def small_share(dst_dev, src_dev):
            return pltpu.make_async_remote_copy(src_ref=sums[n_big], dst_ref=out[n_big].at[src_dev],
                                                send_sem=send2.at[n_big + dst_dev], recv_sem=recv2.at[n_big + src_dev],
                                                device_id=_dev_id(dst_dev), device_id_type=MESH)

        for dev in range(N_DEV):
            @pl.when(me != dev)
            def _():
                small_share(dev, me).start()
        for a in range(n_big):
            r, w = _GRAD_PIECE_SHAPES[a]
            other = out[a].at[pl.ds(pl.multiple_of((1 - c) * r, 8), r), :]
            pltpu.make_async_remote_copy(src_ref=sums[a], dst_ref=other, send_sem=send2.at[a], recv_sem=recv2.at[a],
                                         device_id=(x, y, 1 - c), device_id_type=MESH).wait_recv()
        for dev in range(N_DEV):
            @pl.when(me != dev)
            def _():
                small_share(dev, dev).wait_recv()
                small_share(dev, me).wait_send()
                for a in late_arrays(dev):
                    late(a, dev, me).wait_send()
        for cp in shares:
            cp.wait_send()
        for kp in keeps:
            kp.wait()

    any_spec = pl.BlockSpec(memory_space=pl.ANY)
    return pl.pallas_call(
        body, name="reduce_grads_tail",
        in_specs=[any_spec] * (n_arr + n_big + 1), out_specs=[any_spec] * n_arr,
        out_shape=[jax.ShapeDtypeStruct(sh, F32) for sh in shard_shapes]
        + [jax.ShapeDtypeStruct((N_DEV,) + small_piece, F32)],
        scratch_shapes=[pltpu.VMEM((N_DEV,) + sh, BF16) for sh in _GRAD_PIECE_SHAPES]
        + [pltpu.VMEM((N_DEV,) + small_piece, F32)]
        + [pltpu.VMEM(sh, F32) for sh in _GRAD_PIECE_SHAPES] + [pltpu.VMEM(small_piece, F32)]
        + [pltpu.SemaphoreType.DMA((N_DEV * n_arr,)), pltpu.SemaphoreType.DMA((N_DEV * n_arr,)),
           pltpu.SemaphoreType.DMA((n_big + N_DEV,)), pltpu.SemaphoreType.DMA((n_big + N_DEV,)),
           pltpu.SemaphoreType.DMA((N_DEV * n_arr + n_arr,))],
        compiler_params=pltpu.CompilerParams(vmem_limit_bytes=VMEM_LIMIT, has_side_effects=True),
    )(*grads, g_small, *early_slots, late_in_slots)


W_SPATIAL_ROWS = N_GROUPS * SGU_CHUNK
_REST_PARTS = ("norm_g", "sgu_ln_g", "sgu_ln_b", "b_spatial", "final_norm_g")
REST_ROWS = SMALL_ROWS - W_SPATIAL_ROWS
_LOSS_ROW = W_SPATIAL_ROWS + 8 * len(_REST_PARTS)


def _pack_rest(parts, loss_tile=None):
    rows = []
    for name in _REST_PARTS:
        a = parts[name].reshape(-1, LANES).astype(F32)
        rows.append(jnp.pad(a, ((0, 8 - a.shape[0]), (0, 0))))
    rows.append(jnp.zeros((8, LANES), F32) if loss_tile is None else loss_tile)
    rows.append(jnp.zeros((REST_ROWS - 8 * len(rows), LANES), F32))
    return jnp.concatenate(rows, axis=0)


def _pack_small(parts, loss_tile):
    return jnp.concatenate([parts["w_spatial"].reshape(W_SPATIAL_ROWS, LANES), _pack_rest(parts, loss_tile)], axis=0)


def _unpack_rest(packed, shapes):
    out = {}
    for k, name in enumerate(_REST_PARTS):
        n = math.prod(shapes[name])
        out[name] = packed[8 * k:8 * k + n // LANES].reshape(shapes[name])
    return out


def _local_step(proj, ht, x, target, norm_g, w_in, sgu_ln_g, sgu_ln_b, w_spatial, b_spatial, w_up_a, w_up_b, w_out,
                final_norm_g, bq, bk):
    pos = jnp.arange(SGU_CHUNK)
    keep = (pos[None, :] // SGU_SUBCHUNK) <= (pos[:, None] // SGU_SUBCHUNK)
    w_mask = jnp.where(keep[None], w_spatial, 0.0).astype(BF16)
    w_mask_t = jnp.swapaxes(w_mask, 1, 2)
    bias_full = jnp.repeat(b_spatial.T, GROUP_DIM, axis=1)
    ln_g = sgu_ln_g.reshape(1, D_BRANCH)
    ln_b = sgu_ln_b.reshape(1, D_BRANCH)
    final_g = final_norm_g.reshape(1, D_MODEL)

    o, ya, rsave = _attn_fwd(proj, bq, bk, ATTN_PAIRS)
    yb = _sgu_fwd(proj, ln_g, ln_b, w_mask, bias_full)
    dzg, do, dyb, dx2, g_out, g_up_a, g_up_b, loss_acc, d_final = _mid(
        proj, ya, yb, o, x, target, final_g, w_up_a, w_up_b, w_out)
    dsgu, d_wsp, d_bsp, d_lng, d_lnb = _sgu_bwd(proj, dyb, ln_g, ln_b, w_mask, w_mask_t, bias_full)
    g_in = _dwin_early(ht, dzg, lambda j: jnp.where(j == 0, COL_ZA, COL_GA - 1 + j), dsgu, lambda j: COL_UB + j)
    dq, dk, dv, *early_slots = _attn_bwd(proj, do, rsave, bq, bk, ATTN_PAIRS, (g_in, g_up_a, g_up_b, g_out))
    g_in = _dwin_pieces(ht, (dq, dk, dv), COL_Q, g_in)
    pieces = [(dq, COL_Q * D_BRANCH, 0, D_BRANCH), (dk, COL_K * D_BRANCH, 0, D_BRANCH),
              (dv, COL_V * D_BRANCH, 0, D_BRANCH), (dzg, COL_ZA * D_BRANCH, 0, D_BRANCH),
              (dsgu, COL_UB * D_BRANCH, 0, 3 * D_BRANCH), (dzg, COL_GA * D_BRANCH, D_BRANCH, 2 * D_MODEL)]
    dx, d_norm, late_in_slots = _dh_dx(pieces, w_in, x, norm_g, dx2, g_in)
    small = {"norm_g": d_norm, "sgu_ln_g": d_lng, "sgu_ln_b": d_lnb, "w_spatial": d_wsp,
             "b_spatial": d_bsp[:, :N_GROUPS].T, "final_norm_g": d_final}
    return loss_acc, dx, (g_in, g_up_a, g_up_b, g_out), small, early_slots, late_in_slots


def kernel(x, norm_g, w_in, sgu_ln_g, sgu_ln_b, w_spatial, b_spatial, w_up_a, w_up_b, w_out, final_norm_g, loss_target, m_norm_g, m_w_in, m_sgu_ln_g, m_sgu_ln_b, m_w_spatial, m_b_spatial, m_w_up_a, m_w_up_b, m_w_out, m_final_norm_g, v_norm_g, v_w_in, v_sgu_ln_g, v_sgu_ln_b, v_w_spatial, v_b_spatial, v_w_up_a, v_w_up_b, v_w_out, v_final_norm_g):
    big_names = ("w_in", "w_up_a", "w_up_b", "w_out")
    names = ("norm_g", "w_in", "sgu_ln_g", "sgu_ln_b", "w_spatial", "b_spatial", "w_up_a", "w_up_b", "w_out",
             "final_norm_g")
    w = dict(norm_g=norm_g, w_in=w_in, sgu_ln_g=sgu_ln_g, sgu_ln_b=sgu_ln_b, w_spatial=w_spatial,
             b_spatial=b_spatial, w_up_a=w_up_a, w_up_b=w_up_b, w_out=w_out, final_norm_g=final_norm_g)
    m = dict(norm_g=m_norm_g, w_in=m_w_in, sgu_ln_g=m_sgu_ln_g, sgu_ln_b=m_sgu_ln_b, w_spatial=m_w_spatial,
             b_spatial=m_b_spatial, w_up_a=m_w_up_a, w_up_b=m_w_up_b, w_out=m_w_out, final_norm_g=m_final_norm_g)
    v = dict(norm_g=v_norm_g, w_in=v_w_in, sgu_ln_g=v_sgu_ln_g, sgu_ln_b=v_sgu_ln_b, w_spatial=v_w_spatial,
             b_spatial=v_b_spatial, w_up_a=v_w_up_a, w_up_b=v_w_up_b, w_out=v_w_out, final_norm_g=v_final_norm_g)
    shapes = {n: w[n].shape for n in names}
    flat2d = lambda a: a.reshape(a.shape[-2:])

    proj, ht, *full = _in_proj_gather(x[0], norm_g, *[flat2d(w[n]) for n in big_names])
    loss, dx, big_grads, small, early_slots, late_in_slots = _local_step(
        proj, ht, x[0], loss_target[0], norm_g, full[0], sgu_ln_g[0], sgu_ln_b[0], w_spatial[0], b_spatial[0],
        full[1], full[2], full[3], final_norm_g, ATTN_Q_BLOCK, ATTN_K_BLOCK)
    packed = _pack_small(small, loss).reshape(N_DEV, SMALL_PIECE, LANES)
    red = _reduce_grads_tail(big_grads, packed, early_slots, late_in_slots)

    grads, deltas, new_m, new_v = {}, {}, {}, {}
    g_small = red[4].reshape(SMALL_ROWS, LANES)
    rows2d = lambda a: a.reshape(W_SPATIAL_ROWS, LANES)
    sets = [(flat2d(w[n]), g, flat2d(m[n]), flat2d(v[n]), 0) for n, g in zip(big_names, red[:4])]
    sets.append((rows2d(w_spatial), g_small, rows2d(m_w_spatial), rows2d(v_w_spatial), 0))
    sets.append((_pack_rest(w), g_small, _pack_rest(m), _pack_rest(v), W_SPATIAL_ROWS * ADAM_GRID // REST_ROWS))
    updated = _adamw(sets)
    for n, (g, d, nm, nv) in zip(big_names + ("w_spatial",), updated[:5]):
        grads[n], deltas[n], new_m[n], new_v[n] = (a.reshape(shapes[n]) for a in (g, d, nm, nv))
    g_rest, d, nm, nv = updated[5]
    for src, dst in ((g_rest, grads), (d, deltas), (nm, new_m), (nv, new_v)):
        dst.update(_unpack_rest(src, shapes))

    return (g_rest[_LOSS_ROW - W_SPATIAL_ROWS, 0], dx[None], *[grads[n] for n in names], *[deltas[n] for n in names],
            *[new_m[n] for n in names], *[new_v[n] for n in names])
```

```python
import math

import jax
import jax.numpy as jnp
from jax import lax
from jax.experimental import pallas as pl
from jax.experimental.pallas import tpu as pltpu

F32 = jnp.float32
BF16 = jnp.bfloat16

D_MODEL = 1024
N_HEADS = 8
HEAD_DIM = 64
D_BRANCH = 512
D_IN = 4 * D_BRANCH + 3 * D_BRANCH + 2 * D_MODEL
N_GROUPS = 8
GROUP_DIM = 64
SGU_CHUNK = 128
SGU_SUBCHUNK = 64
GROUP_SHIFT = 6
EPS = 1e-6
LANES = 128
ATTN_Q_BLOCK = 256
ATTN_K_BLOCK = 256
DEAD = -110.0
SKIPPED = -1e30
SCAN_PASSES = 1
ATTN_PAIRS = 2
N_CHIPS = 4
N_DEV = 8
MESH = pl.DeviceIdType.MESH

ADAM_LR = 0.001
ADAM_B1 = 0.9
ADAM_B2 = 0.999
ADAM_EPS = 1e-08
ADAM_WD = 0.01
ADAM_STEP = 10

COL_Q, COL_K, COL_V, COL_ZA, COL_UB, COL_VB, COL_ZB, COL_GA, COL_GB = 0, 1, 2, 3, 4, 5, 6, 7, 9

VMEM_LIMIT = 56 * 1024 * 1024

SMALL_ROWS = 1088
SMALL_PIECE = SMALL_ROWS // N_DEV


def _cparams(sem=None):
    return pltpu.CompilerParams(dimension_semantics=sem, vmem_limit_bytes=VMEM_LIMIT)


def _aligned(v, m):
    return v if isinstance(v, int) else pl.multiple_of(v, m)


def _sigmoid(x):
    return 1.0 / (1.0 + jnp.exp(-x))


def _gelu_and_grad(x):
    k = math.sqrt(2.0 / math.pi)
    x2 = x * x
    inner = k * (x + 0.044715 * x * x2)
    th = jnp.tanh(inner)
    g = 0.5 * x * (1.0 + th)
    dg = 0.5 * (1.0 + th) + 0.5 * x * (1.0 - th * th) * (k * (1.0 + 3.0 * 0.044715 * x2))
    return g, dg


def _split_dot(a, b_bf16, passes):
    out = None
    rem = a
    for _ in range(passes):
        part = rem.astype(BF16)
        d = jnp.dot(part, b_bf16, preferred_element_type=F32)
        out = d if out is None else out + d
        rem = rem - part.astype(F32)
    return out


def _dot_nt(a, b):
    return lax.dot_general(a, b, (((1,), (1,)), ((), ())), preferred_element_type=F32)


def _dot_tn(a, b):
    return lax.dot_general(a, b, (((0,), (0,)), ((), ())), preferred_element_type=F32)


def _place():
    x, y, c = lax.axis_index("x"), lax.axis_index("y"), lax.axis_index("c")
    return x, y, c


def _in_proj_gather(x, norm_g, w_in, w_up_a, w_up_b, w_out):
    s = x.shape[0]
    tm = min(1024, s)
    nt = s // tm
    shards = (w_in, w_up_a, w_up_b, w_out)
    n_arr = len(shards)
    col_sharded = (True, True, True, False)
    full_shapes = ((D_MODEL, D_IN), (D_BRANCH, D_MODEL), (D_BRANCH, D_MODEL), (D_MODEL, D_MODEL))
    w_shard = w_in.shape[1]
    half_rows = D_MODEL // 2
    stage_rows = 256

    def body(order_ref, x_ref, g_ref, *refs):
        src = refs[:n_arr]
        proj_ref, ht_ref = refs[n_arr:n_arr + 2]
        out = refs[n_arr + 2:2 * n_arr + 2]
        wsc, h_scr, stage = refs[2 * n_arr + 2:2 * n_arr + 5]
        small_stage = refs[2 * n_arr + 5:2 * n_arr + 8]
        small_cast = refs[2 * n_arr + 8:2 * n_arr + 11]
        send_sems, recv_sems, local_sems = refs[2 * n_arr + 11:]
        k = pl.program_id(0)
        i = pl.program_id(1)
        x_, y_, c = _place()
        chip = 2 * x_ + y_
        sibling = (x_, y_, 1 - c)
        others = [(x_, 1 - y_), (1 - x_, y_), (1 - x_, 1 - y_)]

        def region(a, chip_idx, half):
            if a == 0:
                return wsc.at[chip_idx, pl.ds(_aligned(half * half_rows, 16), half_rows), :]
            r, w = shards[a].shape
            hr = r // 2
            if col_sharded[a]:
                return out[a].at[pl.ds(_aligned(half * hr, 16), hr), pl.ds(_aligned(chip_idx * w, LANES), w)]
            return out[a].at[pl.ds(_aligned(chip_idx * r + half * hr, 16), hr), :]

        def remote(kk, a, chip_idx, half, to, own):
            s_ref = region(a, chip_idx, half)
            if own and a > 0:
                hr = shards[a].shape[0] // 2
                s_ref = small_cast[a - 1].at[pl.ds(_aligned(half * hr, 16), hr), :]
            return pltpu.make_async_remote_copy(src_ref=s_ref, dst_ref=region(a, chip_idx, half),
                                                send_sem=send_sems.at[kk], recv_sem=recv_sems.at[kk],
                                                device_id=to, device_id_type=MESH)

        def keep_whole(kk, chip_idx):
            return pltpu.make_async_copy(wsc.at[chip_idx],
                                         out[0].at[:, pl.ds(_aligned(chip_idx * w_shard, LANES), w_shard)],
                                         local_sems.at[kk])

        def small_stores():
            cps = []
            for a in range(1, n_arr):
                hr = shards[a].shape[0] // 2
                for half in range(2):
                    cps.append(pltpu.make_async_copy(small_cast[a - 1].at[pl.ds(half * hr, hr), :],
                                                     region(a, chip, half), local_sems.at[4 + 2 * (a - 1) + half]))
            return cps

        def arrive_and_pass(j):
            ochip = chip ^ j
            for a in range(n_arr):
                kk = n_arr * (j - 1) + a
                remote(kk, a, ochip, c, sibling, False).wait_recv()
                remote(3 * n_arr + kk, a, ochip, c, sibling, False).start()

        def from_sibling(j, a):
            remote(3 * n_arr + n_arr * (j - 1) + a, a, chip ^ j, 1 - c, sibling, False).wait_recv()

        @pl.when((k == 0) & (i == 0))
        def _():
            def cast_rows(half):
                for t in range(half_rows // stage_rows):
                    r0 = pl.multiple_of(half * half_rows + t * stage_rows, stage_rows)
                    pltpu.sync_copy(src[0].at[pl.ds(r0, stage_rows), :], stage)
                    wsc[chip, pl.ds(r0, stage_rows), :] = stage[...].astype(BF16)

            cast_rows(c)
            for j in (1, 2):
                remote(n_arr * (j - 1), 0, chip, c, (*others[j - 1], c), True).start()
            cast_rows(1 - c)
            for a in range(1, n_arr):
                pltpu.sync_copy(src[a], small_stage[a - 1])
                small_cast[a - 1][...] = small_stage[a - 1][...].astype(BF16)
            for j in (1, 2):
                for a in range(1, n_arr):
                    remote(n_arr * (j - 1) + a, a, chip, c, (*others[j - 1], c), True).start()
            keep_whole(0, chip).start()
            for cp in small_stores():
                cp.start()

        @pl.when((k == 1) & (i == 0))
        def _():
            for j in (1, 2):
                remote(n_arr * (j - 1), 0, chip, c, (*others[j - 1], c), True).wait_send()
            for a in range(n_arr):
                remote(n_arr * 2 + a, a, chip, c, (*others[2], c), True).start()
            arrive_and_pass(1)
            from_sibling(1, 0)
            keep_whole(1, chip ^ 1).start()

        @pl.when((k == 1) & (i == min(1, nt - 1)))
        def _():
            arrive_and_pass(2)

        @pl.when((k == 2) & (i == 0))
        def _():
            from_sibling(2, 0)
            keep_whole(2, chip ^ 2).start()
            arrive_and_pass(3)

        @pl.when((k == 3) & (i == 0))
        def _():
            from_sibling(3, 0)
            keep_whole(3, chip ^ 3).start()

        @pl.when(k == 0)
        def _():
            xf = x_ref[...]
            r = lax.rsqrt(jnp.mean(xf * xf, axis=-1, keepdims=True) + EPS)
            h = xf * r * g_ref[...]
            h_scr[i] = h.astype(BF16)
            ht_ref[...] = h.T.astype(BF16)

        proj_ref[...] = jnp.dot(h_scr[i], wsc[order_ref[k]], preferred_element_type=F32).astype(BF16)

        @pl.when((k == 3) & (i == nt - 1))
        def _():
            for j in (1, 2, 3):
                for a in range(1, n_arr):
                    from_sibling(j, a)
            for j in (1, 2, 3):
                for a in range(n_arr):
                    kk = n_arr * (j - 1) + a
                    if a > 0 or j == 3:
                        remote(kk, a, chip, c, (*others[j - 1], c), True).wait_send()
                    remote(3 * n_arr + kk, a, chip ^ j, c, sibling, False).wait_send()
            for kk in range(4):
                keep_whole(kk, chip ^ kk).wait()
            for cp in small_stores():
                cp.wait()

    any_spec = pl.BlockSpec(memory_space=pl.ANY)
    tile = lambda kk, ii: jnp.where(kk == 0, ii, nt - 1)
    grid_spec = pltpu.PrefetchScalarGridSpec(
        num_scalar_prefetch=1, grid=(N_CHIPS, nt),
        in_specs=[pl.BlockSpec((tm, D_MODEL), lambda kk, ii, order: (tile(kk, ii), 0)),
                  pl.BlockSpec((1, D_MODEL), lambda kk, ii, order: (0, 0))] + [any_spec] * n_arr,
        out_specs=[pl.BlockSpec((tm, w_shard), lambda kk, ii, order: (ii, order[kk])),
                   pl.BlockSpec((D_MODEL, tm), lambda kk, ii, order: (0, tile(kk, ii)))] + [any_spec] * n_arr,
        scratch_shapes=[pltpu.VMEM((N_CHIPS, D_MODEL, w_shard), BF16), pltpu.VMEM((nt, tm, D_MODEL), BF16),
                        pltpu.VMEM((stage_rows, w_shard), F32)]
        + [pltpu.VMEM(a.shape, F32) for a in shards[1:]] + [pltpu.VMEM(a.shape, BF16) for a in shards[1:]]
        + [pltpu.SemaphoreType.DMA((6 * n_arr,)), pltpu.SemaphoreType.DMA((6 * n_arr,)),
           pltpu.SemaphoreType.DMA((4 + 2 * (n_arr - 1),))])
    x_, y_, _ = _place()
    order = (2 * x_ + y_) ^ jnp.arange(N_CHIPS, dtype=jnp.int32)
    return pl.pallas_call(
        body, name="in_proj_gather", grid_spec=grid_spec,
        out_shape=[jax.ShapeDtypeStruct((s, D_IN), BF16), jax.ShapeDtypeStruct((D_MODEL, s), BF16)]
        + [jax.ShapeDtypeStruct(sh, BF16) for sh in full_shapes],
        compiler_params=pltpu.CompilerParams(dimension_semantics=("arbitrary", "arbitrary"),
                                             vmem_limit_bytes=VMEM_LIMIT, has_side_effects=True),
    )(order, x, norm_g, *shards)


def _neg_softplus_parts(z):
    zb = z.astype(BF16)
    p = jnp.exp(-jnp.abs(zb))
    return p, jnp.maximum(zb, jnp.zeros_like(zb)) + jnp.log(1.0 + p)


def _split_cat(a, passes):
    parts = []
    rem = a
    for k in range(passes):
        part = rem.astype(BF16)
        parts.append(part)
        if k + 1 < passes:
            rem = rem - part.astype(F32)
    return parts[0] if passes == 1 else jnp.concatenate(parts, axis=1)


def _tri(blk, upper, sign):
    row = lax.broadcasted_iota(jnp.int32, (blk, blk), 0)
    col = lax.broadcasted_iota(jnp.int32, (blk, blk), 1)
    keep = (row <= col) if upper else (row >= col)
    t = jnp.where(keep, sign, 0.0).astype(BF16)
    return t if SCAN_PASSES == 1 else jnp.concatenate([t] * SCAN_PASSES, axis=0)


def _attn_fwd(proj, bq, bk, npairs):
    s = proj.shape[0]
    nq = s // bq
    ratio = bq // bk
    scale = HEAD_DIM ** -0.5
    heads = tuple(range(2 * npairs))
    width = LANES * npairs

    def body(q_ref, k_ref, v_ref, za_ref, o_ref, ya_ref, rs_ref, acc_ref, r_ref):
        i = pl.program_id(1)
        lane = lax.broadcasted_iota(jnp.int32, (bq, LANES), 1)
        lo_half = lane < HEAD_DIM
        qm = []
        for pr in range(npairs):
            q = q_ref[:, LANES * pr:LANES * (pr + 1)] * jnp.asarray(scale, BF16)
            zero = jnp.zeros_like(q)
            qm += [jnp.where(lo_half, q, zero), jnp.where(lo_half, zero, q)]
        row = lax.broadcasted_iota(jnp.int32, (bq, bk), 0)
        col = lax.broadcasted_iota(jnp.int32, (bq, bk), 1)
        tneg = _tri(bk, False, -1.0)
        acc_ref[...] = jnp.zeros_like(acc_ref)
        r_ref[...] = jnp.zeros_like(r_ref)
        rs_ref[...] = jnp.full_like(rs_ref, SKIPPED)

        def scores(j):
            ks = pl.multiple_of(j * bk, bk)
            return [_dot_nt(qm[h], k_ref[pl.ds(ks, bk), LANES * (h // 2):LANES * (h // 2 + 1)]) for h in heads]

        def block(j, diag, valid=None):
            ks = pl.multiple_of(j * bk, bk)
            vj = [v_ref[pl.ds(ks, bk), LANES * pr:LANES * (pr + 1)] for pr in range(npairs)]
            if diag:
                before = (j * bk + col) < (i * bq + row)
            z = scores(j)
            sp = [_neg_softplus_parts(z[h])[1] for h in heads]
            if diag:
                sp = [jnp.where(before, sp[h], 0.0) for h in heads]
            cin = [jnp.dot(_split_cat(sp[h], SCAN_PASSES), tneg, preferred_element_type=F32) for h in heads]
            w = [jnp.exp(z[h] + cin[h]) for h in heads]
            if diag:
                w = [jnp.where(before, w[h], 0.0) for h in heads]
            pv = [jnp.dot(w[h].astype(BF16), vj[h // 2], preferred_element_type=F32) for h in heads]
            r = [r_ref[h] for h in heads]
            keep = 1.0 if valid is None else valid.astype(F32)
            for h in heads:
                acc_ref[h] += pv[h] * (jnp.exp(r[h]) * keep)
                r_ref[h] = r[h] + cin[h][:, 0:1] * keep
            for pr in range(npairs):
                hit = [lane == j, lane == j + HEAD_DIM]
                if valid is not None:
                    hit = [m & valid for m in hit]
                rs_ref[pr] = jnp.where(hit[0], r[2 * pr], jnp.where(hit[1], r[2 * pr + 1], rs_ref[pr]))

        for t in range(ratio):
            block(i * ratio + ratio - 1 - t, True)
        block(jnp.maximum(i * ratio - 1, 0), False, valid=i > 0)

        def alive(carry):
            jj, r_max = carry
            return (jj < i * ratio - 1) & (r_max > DEAD)

        def loop_body(carry):
            jj, _ = carry
            block(i * ratio - 2 - jj, False)
            return jj + 1, jnp.max(r_ref[...])

        lax.while_loop(alive, loop_body, (0, jnp.max(r_ref[...])))
        for pr in range(npairs):
            cols = slice(LANES * pr, LANES * (pr + 1))
            o = jnp.where(lo_half, acc_ref[2 * pr], acc_ref[2 * pr + 1])
            o_ref[:, cols] = o.astype(BF16)
            za = za_ref[:, cols].astype(F32)
            ya_ref[:, cols] = (o * (za * _sigmoid(za))).astype(BF16)

    n_steps = N_HEADS // (2 * npairs)
    return pl.pallas_call(
        body, name="attn_fwd", grid=(n_steps, nq),
        in_specs=[pl.BlockSpec((bq, width), lambda p, i: (i, n_steps * COL_Q + p)),
                  pl.BlockSpec((s, width), lambda p, i: (0, n_steps * COL_K + p)),
                  pl.BlockSpec((s, width), lambda p, i: (0, n_steps * COL_V + p)),
                  pl.BlockSpec((bq, width), lambda p, i: (i, n_steps * COL_ZA + p))],
        out_specs=[pl.BlockSpec((bq, width), lambda p, i: (i, p)),
                   pl.BlockSpec((bq, width), lambda p, i: (i, p)),
                   pl.BlockSpec((npairs, bq, LANES), lambda p, i: (p, i, 0))],
        out_shape=[jax.ShapeDtypeStruct((s, D_BRANCH), BF16), jax.ShapeDtypeStruct((s, D_BRANCH), BF16),
                   jax.ShapeDtypeStruct((N_HEADS // 2, s, LANES), F32)],
        scratch_shapes=[pltpu.VMEM((2 * npairs, bq, LANES), F32), pltpu.VMEM((2 * npairs, bq, 1), F32)],
        compiler_params=_cparams(("parallel", "parallel")),
    )(proj, proj, proj, proj)


_GRAD_COL_SHARDED = (True, True, True, False)
_GRAD_FULL_SHAPES = ((D_MODEL, D_IN), (D_BRANCH, D_MODEL), (D_BRANCH, D_MODEL), (D_MODEL, D_MODEL))
_GRAD_PIECE_SHAPES = tuple((r // 2, w // N_CHIPS) if cs else (r // (2 * N_CHIPS), w)
                           for (r, w), cs in zip(_GRAD_FULL_SHAPES, _GRAD_COL_SHARDED))
_EARLY_IN_DEVS = (4, 5, 6, 7)
_LATE_IN_DEVS = (0, 1, 2, 3)
_LATE_CHIPS = (0, 1)
_EARLY_CHIPS = (2, 3)
_ALL_CHIPS = (0, 1, 2, 3)


def _grad_piece(ref, a, dev):
    r, w = _GRAD_PIECE_SHAPES[a]
    if _GRAD_COL_SHARDED[a]:
        return ref.at[pl.ds((dev % 2) * r, r), pl.ds((dev // 2) * w, w)]
    return ref.at[pl.ds(dev * r, r), :]


def _dev_id(dev):
    return (dev // 4, (dev // 2) % 2, dev % 2)


def _me():
    return 4 * lax.axis_index("x") + 2 * lax.axis_index("y") + lax.axis_index("c")


def _presum_copy(src, dst, send_sem, recv_sem, to_dev):
    return pltpu.make_async_remote_copy(src_ref=src, dst_ref=dst, send_sem=send_sem, recv_sem=recv_sem,
                                        device_id=_dev_id(to_dev), device_id_type=MESH)


def _presum_hand_off(dev, a, dest_chips, g_ref, slots, pair, send_sems, recv_sems):
    chip, core = dev // 2, dev % 2
    cps = []
    for k, q in enumerate(dest_chips):
        piece = _grad_piece(g_ref, a, 2 * q + 1 - core)
        if q == chip:
            cps.append(_presum_copy(piece, slots.at[dev], send_sems.at[N_DEV + k], recv_sems.at[dev], dev ^ 1))
        else:
            cps.append(_presum_copy(piece, pair.at[k], send_sems.at[N_DEV + k], recv_sems.at[N_DEV + k], dev ^ 1))
    return cps


def _presum_sends(dev, a, dest_chips, slots, sums, send_sems, recv_sems):
    chip, core = dev // 2, dev % 2
    return [_presum_copy(sums.at[k], slots.at[dev], send_sems.at[2 * q + core], recv_sems.at[dev], 2 * q + core)
            for k, q in enumerate(dest_chips) if q != chip]


def _presum_loads(dev, a, dest_chips, g_ref, stage, load_sems):
    chip, core = dev // 2, dev % 2
    return [pltpu.make_async_copy(_grad_piece(g_ref, a, 2 * q + core), stage.at[k], load_sems.at[k])
            for k, q in enumerate(dest_chips) if q != chip]


def _presum_send(dev, a, dest_chips, g_ref, slots, pair, stage, sums, send_sems, recv_sems, load_sems):
    chip, core = dev // 2, dev % 2
    hand = _presum_hand_off(dev, a, dest_chips, g_ref, slots, pair, send_sems, recv_sems)
    for cp in _presum_loads(dev, a, dest_chips, g_ref, stage, load_sems):
        cp.wait()
    for k, q in enumerate(dest_chips):
        if q != chip:
            hand[k].wait_recv()
            sums[k] = (stage[k].astype(F32) + pair[k].astype(F32)).astype(BF16)
    for cp in _presum_sends(dev, a, dest_chips, slots, sums, send_sems, recv_sems):
        cp.start()


def _presum_wait(dev, a, dest_chips, g_ref, slots, pair, sums, send_sems, recv_sems):
    chip, core = dev // 2, dev % 2
    for cp in _presum_hand_off(dev, a, dest_chips, g_ref, slots, pair, send_sems, recv_sems):
        cp.wait_send()
    for cp in _presum_sends(dev, a, dest_chips, slots, sums, send_sems, recv_sems):
        cp.wait_send()
    if chip in dest_chips:
        for src_dev in _presum_sources(dev):
            _presum_copy(sums.at[0], slots.at[src_dev], send_sems.at[src_dev], recv_sems.at[src_dev], src_dev).wait_recv()


def _presum_sources(dev):
    return [dev ^ 1] + [2 * r + dev % 2 for r in range(N_CHIPS) if r != dev // 2]


def _presum_scratch(a, dest_chips):
    n = len(dest_chips)
    piece = _GRAD_PIECE_SHAPES[a]
    return [pltpu.VMEM((n,) + piece, BF16), pltpu.VMEM((n,) + piece, BF16), pltpu.VMEM((n,) + piece, BF16),
            pltpu.SemaphoreType.DMA((N_DEV + n,)), pltpu.SemaphoreType.DMA((N_DEV + n,)),
            pltpu.SemaphoreType.DMA((n,))]


PRESUM_SCRATCH = 6


def _presum_program(first, second, last, a, dest_chips, g_ref, slots, scratch):
    pair, stage, sums, send_sems, recv_sems, load_sems = scratch
    me = _me()

    @pl.when(first)
    def _():
        for dev in range(N_DEV):
            @pl.when(me == dev)
            def _():
                for cp in _presum_hand_off(dev, a, dest_chips, g_ref, slots, pair, send_sems, recv_sems):
                    cp.start()
                for cp in _presum_loads(dev, a, dest_chips, g_ref, stage, load_sems):
                    cp.start()

    @pl.when(second)
    def _():
        for dev in range(N_DEV):
            @pl.when(me == dev)
            def _():
                _presum_send(dev, a, dest_chips, g_ref, slots, pair, stage, sums, send_sems, recv_sems, load_sems)

    def finish():
        for dev in range(N_DEV):
            @pl.when(me == dev)
            def _():
                _presum_wait(dev, a, dest_chips, g_ref, slots, pair, sums, send_sems, recv_sems)

    return finish


def _attn_bwd(proj, do, rsave, bq, bk, npairs, grads):
    plan = ((0, _EARLY_CHIPS), (1, _ALL_CHIPS), (2, _ALL_CHIPS), (3, _ALL_CHIPS))
    s = proj.shape[0]
    nq = s // bq
    ratio = bq // bk
    scale = HEAD_DIM ** -0.5
    heads = tuple(range(2 * npairs))
    width = LANES * npairs

    n_steps = N_HEADS // (2 * npairs)
    n_g = len(grads)

    def body(q_ref, k_ref, v_ref, do_ref, rs_ref, *refs):
        g_src = refs[:n_g]
        dq_ref, dk_ref, dv_ref = refs[n_g:n_g + 3]
        g_slots = refs[n_g + 3:2 * n_g + 3]
        dk_acc, dv_acc, dq_acc, e_ref = refs[2 * n_g + 3:2 * n_g + 7]
        i = pl.program_id(1)
        step = pl.program_id(0) * nq + i
        finish = [_presum_program(step == 0, step == 1, step == n_steps * nq - 1, a, chips, g_src[pos], g_slots[pos],
                                  refs[2 * n_g + 7 + PRESUM_SCRATCH * pos:2 * n_g + 7 + PRESUM_SCRATCH * (pos + 1)])
                  for pos, (a, chips) in enumerate(plan)]

        lane = lax.broadcasted_iota(jnp.int32, (bq, LANES), 1)
        lo_half = lane < HEAD_DIM
        qm, dom = [], []
        for pr in range(npairs):
            cols = slice(LANES * pr, LANES * (pr + 1))
            q = q_ref[:, cols] * jnp.asarray(scale, BF16)
            zero = jnp.zeros_like(q)
            qm += [jnp.where(lo_half, q, zero), jnp.where(lo_half, zero, q)]
            dout = do_ref[:, cols].astype(F32)
            dom += [jnp.where(lo_half, dout, 0.0), jnp.where(lo_half, 0.0, dout)]
        row = lax.broadcasted_iota(jnp.int32, (bq, bk), 0)
        col = lax.broadcasted_iota(jnp.int32, (bq, bk), 1)
        tneg = _tri(bk, False, -1.0)
        tfwd = _tri(bk, True, 1.0)

        @pl.when(i == 0)
        def _():
            dk_acc[...] = jnp.zeros_like(dk_acc)
            dv_acc[...] = jnp.zeros_like(dv_acc)

        dq_acc[...] = jnp.zeros_like(dq_acc)
        e_ref[...] = jnp.zeros_like(e_ref)

        def block(j, diag, valid=None):
            ks = pl.multiple_of(j * bk, bk)
            kj = [k_ref[pl.ds(ks, bk), LANES * pr:LANES * (pr + 1)] for pr in range(npairs)]
            vj = [v_ref[pl.ds(ks, bk), LANES * pr:LANES * (pr + 1)] for pr in range(npairs)]
            if diag:
                before = (j * bk + col) < (i * bq + row)
            z = [_dot_nt(qm[h], kj[h // 2]) for h in heads]
            er = [jnp.exp(jnp.sum(jnp.where(lane == j + HEAD_DIM * (h % 2), rs_ref[h // 2], 0.0), axis=-1,
                                  keepdims=True)) for h in heads]
            if valid is not None:
                er = [er[h] * valid.astype(F32) for h in heads]
            dos = [(dom[h] * er[h]).astype(BF16) for h in heads]
            dw = [_dot_nt(dos[h], vj[h // 2]) for h in heads]
            psp = [_neg_softplus_parts(z[h]) for h in heads]
            sp = [psp[h][1] for h in heads]
            if diag:
                sp = [jnp.where(before, sp[h], 0.0) for h in heads]
            cin = [jnp.dot(_split_cat(sp[h], SCAN_PASSES), tneg, preferred_element_type=F32) for h in heads]
            w = [jnp.exp(z[h] + cin[h]) for h in heads]
            if diag:
                w = [jnp.where(before, w[h], 0.0) for h in heads]
            e =[dw[h] * w[h] for h in heads]
            eincl = [jnp.dot(_split_cat(e[h], SCAN_PASSES), tfwd, preferred_element_type=F32) + e_ref[h]
                     for h in heads]
            dz = []
            for h in heads:
                p = psp[h][0]
                beta = jnp.where(z[h] >= 0.0, 1.0, p) / (1.0 + p)
                d = e[h] - beta * eincl[h]
                dz.append((jnp.where(before, d, 0.0) if diag else d).astype(BF16))
            wb = [w[h].astype(BF16) for h in heads]
            for h in heads:
                e_ref[h] = eincl[h][:, bk - 1:bk]
                dq_acc[h] += jnp.dot(dz[h], kj[h // 2], preferred_element_type=F32)
            for pr in range(npairs):
                cols = slice(LANES * pr, LANES * (pr + 1))
                h0, h1 = 2 * pr, 2 * pr + 1
                dk_acc[pl.ds(ks, bk), cols] += _dot_tn(dz[h0], qm[h0]) + _dot_tn(dz[h1], qm[h1])
                dv_acc[pl.ds(ks, bk), cols] += _dot_tn(wb[h0], dos[h0]) + _dot_tn(wb[h1], dos[h1])

        def loop_body(j, carry):
            block(j, False)
            return carry

        block_of_lane = lane & (HEAD_DIM - 1)
        live = jnp.max(rs_ref[...], axis=0) > DEAD
        first_live = jnp.min(jnp.where(live, block_of_lane, nq * ratio))
        last = jnp.maximum(i * ratio - 1, 0)
        lax.fori_loop(jnp.minimum(first_live, last), last, loop_body, 0)
        block(last, False, valid=i > 0)
        for t in range(ratio):
            block(i * ratio + t, True)
        for pr in range(npairs):
            dq = jnp.where(lo_half, dq_acc[2 * pr], dq_acc[2 * pr + 1]) * scale
            dq_ref[:, LANES * pr:LANES * (pr + 1)] = dq.astype(BF16)

        @pl.when(i == nq - 1)
        def _():
            dk_ref[...] = dk_acc[...].astype(BF16)
            dv_ref[...] = dv_acc[...].astype(BF16)

        @pl.when(step == n_steps * nq - 1)
        def _():
            for fin in finish:
                fin()

    any_spec = pl.BlockSpec(memory_space=pl.ANY)
    return pl.pallas_call(
        body, name="attn_bwd", grid=(n_steps, nq),
        in_specs=[pl.BlockSpec((bq, width), lambda p, i: (i, n_steps * COL_Q + p)),
                  pl.BlockSpec((s, width), lambda p, i: (0, n_steps * COL_K + p)),
                  pl.BlockSpec((s, width), lambda p, i: (0, n_steps * COL_V + p)),
                  pl.BlockSpec((bq, width), lambda p, i: (i, p)),
                  pl.BlockSpec((npairs, bq, LANES), lambda p, i: (p, i, 0))] + [any_spec] * n_g,
        out_specs=[pl.BlockSpec((bq, width), lambda p, i: (i, p)),
                   pl.BlockSpec((s, width), lambda p, i: (0, p)),
                   pl.BlockSpec((s, width), lambda p, i: (0, p))] + [any_spec] * n_g,
        out_shape=[jax.ShapeDtypeStruct((s, D_BRANCH), BF16)] * 3
        + [jax.ShapeDtypeStruct((N_DEV,) + _GRAD_PIECE_SHAPES[a], BF16) for a, _ in plan],
        scratch_shapes=[pltpu.VMEM((s, width), F32), pltpu.VMEM((s, width), F32),
                        pltpu.VMEM((2 * npairs, bq, LANES), F32), pltpu.VMEM((2 * npairs, bq, 1), F32)]
        + [sh for a, chips in plan for sh in _presum_scratch(a, chips)],
        compiler_params=pltpu.CompilerParams(dimension_semantics=("arbitrary", "arbitrary"),
                                             vmem_limit_bytes=VMEM_LIMIT, has_side_effects=True),
    )(proj, proj, proj, do, rsave, *grads)


def _group_avg_matrix():
    a = lax.broadcasted_iota(jnp.int32, (LANES, LANES), 0) >> GROUP_SHIFT
    b = lax.broadcasted_iota(jnp.int32, (LANES, LANES), 1) >> GROUP_SHIFT
    return jnp.where(a == b, 1.0 / GROUP_DIM, 0.0).astype(BF16)


def _group_mean(a, avg):
    parts = [_split_dot(a[:, LANES * k:LANES * (k + 1)], avg, 2) for k in range(D_BRANCH // LANES)]
    return jnp.concatenate(parts, axis=1)


def _sgu_forward_parts(ub, vb, ln_g, ln_b, avg):
    ug, dug = _gelu_and_grad(ub)
    vg, dvg = _gelu_and_grad(vb)
    mu = _group_mean(vg, avg)
    d = vg - mu
    var = _group_mean(d * d, avg)
    rstd = lax.rsqrt(var + EPS)
    vhat = d * rstd
    vn = vhat * ln_g + ln_b
    return ug, dug, dvg, rstd, vhat, vn


def _sgu_mix(w_ref, src_bf16, n_chunks):
    lane = lax.broadcasted_iota(jnp.int32, (SGU_CHUNK, LANES), 1)
    lo_half = lane < GROUP_DIM
    rows = []
    for n in range(n_chunks):
        slabs = []
        for a in range(D_BRANCH // LANES):
            blk = src_bf16[SGU_CHUNK * n:SGU_CHUNK * (n + 1), LANES * a:LANES * (a + 1)]
            zero = jnp.zeros_like(blk)
            m0 = jnp.dot(w_ref[2 * a], jnp.where(lo_half, blk, zero), preferred_element_type=F32)
            m1 = jnp.dot(w_ref[2 * a + 1], jnp.where(lo_half, zero, blk), preferred_element_type=F32)
            slabs.append(m0 + m1)
        rows.append(jnp.concatenate(slabs, axis=1))
    return jnp.concatenate(rows, axis=0)


def _sgu_fwd(proj, ln_g, ln_b, w_mask, bias_full):
    s = proj.shape[0]
    tm = min(512, s)
    n_chunks = tm // SGU_CHUNK

    def body(ub_ref, vb_ref, zb_ref, g_ref, b_ref, w_ref, bias_ref, yb_ref):
        avg = _group_avg_matrix()
        ug, _, _, _, _, vn = _sgu_forward_parts(ub_ref[...].astype(F32), vb_ref[...].astype(F32),
                                                g_ref[...], b_ref[...], avg)
        mixed = _sgu_mix(w_ref, vn.astype(BF16), n_chunks) + jnp.concatenate([bias_ref[...]] * n_chunks, axis=0)
        zb = zb_ref[...].astype(F32)
        yb_ref[...] = (ug * mixed * (zb * _sigmoid(zb))).astype(BF16)

    col = lambda c: pl.BlockSpec((tm, D_BRANCH), lambda i: (i, c))
    full = lambda shape: pl.BlockSpec(shape, lambda i: (0,) * len(shape))
    return pl.pallas_call(
        body, name="sgu_fwd", grid=(s // tm,),
        in_specs=[col(COL_UB), col(COL_VB), col(COL_ZB), full((1, D_BRANCH)), full((1, D_BRANCH)),
                  full((N_GROUPS, SGU_CHUNK, SGU_CHUNK)), full((SGU_CHUNK, D_BRANCH))],
        out_specs=pl.BlockSpec((tm, D_BRANCH), lambda i: (i, 0)),
        out_shape=jax.ShapeDtypeStruct((s, D_BRANCH), BF16),
        compiler_params=_cparams(("parallel",)),
    )(proj, proj, proj, ln_g, ln_b, w_mask, bias_full)


def _sgu_bwd(proj, dyb, ln_g, ln_b, w_mask, w_mask_t, bias_full):
    s = proj.shape[0]
    tm = min(512, s)
    n_chunks = tm // SGU_CHUNK
    n_steps = s // tm

    def body(ub_ref, vb_ref, zb_ref, dyb_ref, g_ref, b_ref, w_ref, wt_ref, bias_ref,
             dsgu_ref, dw_ref, db_ref, dg_ref, dbeta_ref, dmix_acc):
        i = pl.program_id(0)

        @pl.when(i == 0)
        def _():
            dw_ref[...] = jnp.zeros_like(dw_ref)
            dg_ref[...] = jnp.zeros_like(dg_ref)
            dbeta_ref[...] = jnp.zeros_like(dbeta_ref)
            dmix_acc[...] = jnp.zeros_like(dmix_acc)

        avg = _group_avg_matrix()
        ln_gv = g_ref[...]
        ug, dug, dvg, rstd, vhat, vn = _sgu_forward_parts(ub_ref[...].astype(F32), vb_ref[...].astype(F32),
                                                          ln_gv, b_ref[...], avg)
        vnb = vn.astype(BF16)
        mixed = _sgu_mix(w_ref, vnb, n_chunks) + jnp.concatenate([bias_ref[...]] * n_chunks, axis=0)
        zb = zb_ref[...].astype(F32)
        sg = _sigmoid(zb)
        sz = zb * sg
        dsz = sg * (1.0 + zb * (1.0 - sg))
        dy = dyb_ref[...].astype(F32)
        dmixed = dy * ug * sz
        du = dy * mixed * sz * dug
        dzb = dy * ug * mixed * dsz
        dmb = dmixed.astype(BF16)
        dvn = _sgu_mix(wt_ref, dmb, n_chunks)

        lane = lax.broadcasted_iota(jnp.int32, (SGU_CHUNK, LANES), 1)
        lo_half = lane < GROUP_DIM
        dm_sum = None
        for n in range(n_chunks):
            rows = slice(SGU_CHUNK * n, SGU_CHUNK * (n + 1))
            dm_sum = dmixed[rows] if dm_sum is None else dm_sum + dmixed[rows]
            for a in range(D_BRANCH // LANES):
                cols = slice(LANES * a, LANES * (a + 1))
                dblk = dmb[rows, cols]
                vblk = vnb[rows, cols]
                zero = jnp.zeros_like(dblk)
                dw_ref[2 * a] += _dot_nt(jnp.where(lo_half, dblk, zero), vblk)
                dw_ref[2 * a + 1] += _dot_nt(jnp.where(lo_half, zero, dblk), vblk)
        dmix_acc[...] += dm_sum

        dg_ref[...] += jnp.sum(dvn * vhat, axis=0, keepdims=True)
        dbeta_ref[...] += jnp.sum(dvn, axis=0, keepdims=True)
        dvh = dvn * ln_gv
        m1 = _group_mean(dvh, avg)
        m2 = _group_mean(dvh * vhat, avg)
        dv = rstd * (dvh - m1 - vhat * m2) * dvg
        dsgu_ref[:, 0:D_BRANCH] = du.astype(BF16)
        dsgu_ref[:, D_BRANCH:2 * D_BRANCH] = dv.astype(BF16)
        dsgu_ref[:, 2 * D_BRANCH:3 * D_BRANCH] = dzb.astype(BF16)

        @pl.when(i == n_steps - 1)
        def _():
            pos = lax.broadcasted_iota(jnp.int32, (SGU_CHUNK, SGU_CHUNK), 0) >> GROUP_SHIFT
            src = lax.broadcasted_iota(jnp.int32, (SGU_CHUNK, SGU_CHUNK), 1) >> GROUP_SHIFT
            keep = src <= pos
            for g in range(N_GROUPS):
                dw_ref[g] = jnp.where(keep, dw_ref[g], 0.0)
            grp = lax.broadcasted_iota(jnp.int32, (D_BRANCH, LANES), 0) >> GROUP_SHIFT
            sel = (grp == lax.broadcasted_iota(jnp.int32, (D_BRANCH, LANES), 1)).astype(BF16)
            db_ref[...] = _split_dot(dmix_acc[...], sel, 3)

    col = lambda c: pl.BlockSpec((tm, D_BRANCH), lambda i: (i, c))
    full = lambda shape: pl.BlockSpec(shape, lambda i: (0,) * len(shape))
    return pl.pallas_call(
        body, name="sgu_bwd", grid=(n_steps,),
        in_specs=[col(COL_UB), col(COL_VB), col(COL_ZB), pl.BlockSpec((tm, D_BRANCH), lambda i: (i, 0)),
                  full((1, D_BRANCH)), full((1, D_BRANCH)),
                  full((N_GROUPS, SGU_CHUNK, SGU_CHUNK)), full((N_GROUPS, SGU_CHUNK, SGU_CHUNK)),
                  full((SGU_CHUNK, D_BRANCH))],
        out_specs=[pl.BlockSpec((tm, 3 * D_BRANCH), lambda i: (i, 0)),
                   full((N_GROUPS, SGU_CHUNK, SGU_CHUNK)), full((SGU_CHUNK, LANES)),
                   full((1, D_BRANCH)), full((1, D_BRANCH))],
        out_shape=[jax.ShapeDtypeStruct((s, 3 * D_BRANCH), BF16),
                   jax.ShapeDtypeStruct((N_GROUPS, SGU_CHUNK, SGU_CHUNK), F32),
                   jax.ShapeDtypeStruct((SGU_CHUNK, LANES), F32),
                   jax.ShapeDtypeStruct((1, D_BRANCH), F32), jax.ShapeDtypeStruct((1, D_BRANCH), F32)],
        scratch_shapes=[pltpu.VMEM((SGU_CHUNK, D_BRANCH), F32)],
        compiler_params=_cparams(("arbitrary",)),
    )(proj, proj, proj, dyb, ln_g, ln_b, w_mask, w_mask_t, bias_full)


def _mid(proj, ya, yb, o, x, target, final_g, w_up_a, w_up_b, w_out):
    s = x.shape[0]
    tm = min(256, s)
    n_steps = s // tm
    half = D_MODEL // 2

    def body(ya_ref, yb_ref, o_ref, za_ref, ga0_ref, ga1_ref, gb0_ref, gb1_ref, x_ref, t_ref, gf_ref,
             wa_ref, wb_ref, wo_ref,
             dzg_ref, do_ref, dyb_ref, dx2_ref, gwo_ref, gwa_ref, gwb_ref, loss_ref, dgf_ref,
             acc_o, acc_a, acc_b):
        i = pl.program_id(0)

        @pl.when(i == 0)
        def _():
            acc_o[...] = jnp.zeros_like(acc_o)
            acc_a[...] = jnp.zeros_like(acc_a)
            acc_b[...] = jnp.zeros_like(acc_b)
            loss_ref[...] = jnp.zeros_like(loss_ref)
            dgf_ref[...] = jnp.zeros_like(dgf_ref)

        ya_v = ya_ref[...]
        yb_v = yb_ref[...]
        pa = jnp.dot(ya_v, wa_ref[...], preferred_element_type=F32)
        pb = jnp.dot(yb_v, wb_ref[...], preferred_element_type=F32)
        sa = _sigmoid(jnp.concatenate([ga0_ref[...], ga1_ref[...]], axis=1).astype(F32))
        sb = _sigmoid(jnp.concatenate([gb0_ref[...], gb1_ref[...]], axis=1).astype(F32))
        merged = (sa * pa + sb * pb).astype(BF16)
        x2 = x_ref[...] + jnp.dot(merged, wo_ref[...], preferred_element_type=F32)
        r2 = lax.rsqrt(jnp.mean(x2 * x2, axis=-1, keepdims=True) + EPS)
        xh = x2 * r2
        gf = gf_ref[...]
        diff = xh * gf - t_ref[...]
        loss_ref[...] += 0.5 * jnp.sum(jnp.mean(diff * diff, axis=-1, keepdims=True))
        dy = diff * (1.0 / D_MODEL)
        dgf_ref[...] += jnp.sum(dy * xh, axis=0, keepdims=True)
        dyg = dy * gf
        dx2 = r2 * (dyg - xh * jnp.mean(dyg * xh, axis=-1, keepdims=True))
        dx2_ref[...] = dx2
        dx2b = dx2.astype(BF16)
        dmerged = _dot_nt(dx2b, wo_ref[...])
        acc_o[...] += _dot_tn(merged, dx2b)
        dpa = dmerged * sa
        dpb = dmerged * sb
        dzg_ref[:, D_BRANCH:D_BRANCH + D_MODEL] = (dpa * pa * (1.0 - sa)).astype(BF16)
        dzg_ref[:, D_BRANCH + D_MODEL:D_BRANCH + 2 * D_MODEL] = (dpb * pb * (1.0 - sb)).astype(BF16)
        dpab = dpa.astype(BF16)
        dpbb = dpb.astype(BF16)
        acc_a[...] += _dot_tn(ya_v, dpab)
        acc_b[...] += _dot_tn(yb_v, dpbb)
        dya = _dot_nt(dpab, wa_ref[...])
        dyb_ref[...] = _dot_nt(dpbb, wb_ref[...]).astype(BF16)
        za = za_ref[...].astype(F32)
        sg = _sigmoid(za)
        do_ref[...] = (dya * (za * sg)).astype(BF16)
        dzg_ref[:, 0:D_BRANCH] = (dya * o_ref[...].astype(F32) * (sg * (1.0 + za * (1.0 - sg)))).astype(BF16)

        @pl.when(i == n_steps - 1)
        def _():
            gwo_ref[...] = acc_o[...].astype(BF16)
            gwa_ref[...] = acc_a[...].astype(BF16)
            gwb_ref[...] = acc_b[...].astype(BF16)

    tok = lambda w: pl.BlockSpec((tm, w), lambda i: (i, 0))
    col = lambda c: pl.BlockSpec((tm, half), lambda i: (i, c))
    full = lambda shape: pl.BlockSpec(shape, lambda i: (0,) * len(shape))
    return pl.pallas_call(
        body, name="mid", grid=(n_steps,),
        in_specs=[tok(D_BRANCH), tok(D_BRANCH), tok(D_BRANCH), col(COL_ZA), col(COL_GA), col(COL_GA + 1),
                  col(COL_GB), col(COL_GB + 1), tok(D_MODEL), tok(D_MODEL), full((1, D_MODEL)),
                  full((D_BRANCH, D_MODEL)), full((D_BRANCH, D_MODEL)), full((D_MODEL, D_MODEL))],
        out_specs=[tok(D_BRANCH + 2 * D_MODEL), tok(D_BRANCH), tok(D_BRANCH), tok(D_MODEL),
                   full((D_MODEL, D_MODEL)), full((D_BRANCH, D_MODEL)), full((D_BRANCH, D_MODEL)),
                   full((8, LANES)), full((1, D_MODEL))],
        out_shape=[jax.ShapeDtypeStruct((s, D_BRANCH + 2 * D_MODEL), BF16),
                   jax.ShapeDtypeStruct((s, D_BRANCH), BF16), jax.ShapeDtypeStruct((s, D_BRANCH), BF16),
                   jax.ShapeDtypeStruct((s, D_MODEL), F32),
                   jax.ShapeDtypeStruct((D_MODEL, D_MODEL), BF16),
                   jax.ShapeDtypeStruct((D_BRANCH, D_MODEL), BF16), jax.ShapeDtypeStruct((D_BRANCH, D_MODEL), BF16),
                   jax.ShapeDtypeStruct((8, LANES), F32), jax.ShapeDtypeStruct((1, D_MODEL), F32)],
        scratch_shapes=[pltpu.VMEM((D_MODEL, D_MODEL), F32), pltpu.VMEM((D_BRANCH, D_MODEL), F32),
                        pltpu.VMEM((D_BRANCH, D_MODEL), F32)],
        compiler_params=_cparams(("arbitrary",)),
    )(ya, yb, o, proj, proj, proj, proj, proj, x, target, final_g, w_up_a, w_up_b, w_out)


def _dwin_early(ht, first, tile_of_first, second, tile_of_second):
    s = ht.shape[1]
    n1 = first.shape[1] // D_BRANCH
    n2 = second.shape[1] // D_BRANCH

    def body(ht_ref, a_ref, b_ref, out_ref):
        j = pl.program_id(0)

        @pl.when(j < n1)
        def _():
            out_ref[...] = jnp.dot(ht_ref[...], a_ref[...], preferred_element_type=F32).astype(BF16)

        @pl.when(j >= n1)
        def _():
            out_ref[...] = jnp.dot(ht_ref[...], b_ref[...], preferred_element_type=F32).astype(BF16)

    return pl.pallas_call(
        body, name="dwin_early", grid=(n1 + n2,),
        in_specs=[pl.BlockSpec((D_MODEL, s), lambda j: (0, 0)),
                  pl.BlockSpec((s, D_BRANCH), lambda j: (0, jnp.minimum(j, n1 - 1))),
                  pl.BlockSpec((s, D_BRANCH), lambda j: (0, jnp.maximum(j - n1, 0)))],
        out_specs=pl.BlockSpec((D_MODEL, D_BRANCH),
                               lambda j: (0, jnp.where(j < n1, tile_of_first(j), tile_of_second(j - n1)))),
        out_shape=jax.ShapeDtypeStruct((D_MODEL, D_IN), BF16),
        compiler_params=_cparams(("arbitrary",)),
    )(ht, first, second)


def _dwin_pieces(ht, pieces, first_tile, prev):
    s = ht.shape[1]
    n = len(pieces)

    def body(ht_ref, *refs):
        srcs = refs[:n]
        out_ref, buf, sems = refs[n + 1:]
        j = pl.program_id(0)

        @pl.when(j == 0)
        def _():
            for k in range(n):
                pltpu.make_async_copy(srcs[k], buf.at[k], sems.at[k]).start()

        for k in range(n):
            @pl.when(j == k)
            def _():
                pltpu.make_async_copy(srcs[k], buf.at[k], sems.at[k]).wait()

        out_ref[...] = jnp.dot(ht_ref[...], buf[j], preferred_element_type=F32).astype(BF16)

    any_spec = pl.BlockSpec(memory_space=pl.ANY)
    return pl.pallas_call(
        body, name="dwin_pieces", grid=(n,),
        in_specs=[pl.BlockSpec((D_MODEL, s), lambda j: (0, 0))] + [any_spec] * (n + 1),
        out_specs=pl.BlockSpec((D_MODEL, D_BRANCH), lambda j: (0, first_tile + j)),
        out_shape=jax.ShapeDtypeStruct((D_MODEL, D_IN), BF16),
        scratch_shapes=[pltpu.VMEM((n, s, D_BRANCH), BF16), pltpu.SemaphoreType.DMA((n,))],
        input_output_aliases={n + 1: 0},
        compiler_params=_cparams(("arbitrary",)),
    )(ht, *pieces, prev)


def _dh_dx(pieces, w_in, x, norm_g, dx2, g_in):
    s = x.shape[0]
    tm = min(256, s)
    n_steps = s // tm
    arrays = []
    for arr, _, _, _ in pieces:
        if not any(arr is a for a in arrays):
            arrays.append(arr)
    n_arr = len(arrays)
    plan = [([k for k, a in enumerate(arrays) if a is arr][0], wcol, off, width) for arr, wcol, off, width in pieces]

    def body(*refs):
        p_refs = refs[:n_arr]
        w_ref, x_ref, g_ref, dx2_ref, gin_ref, dx_ref, dg_ref, late_ref = refs[n_arr:n_arr + 8]
        step = pl.program_id(0)
        finish = _presum_program(step == 0, step == min(1, n_steps - 1), step == n_steps - 1, 0, _LATE_CHIPS,
                                 gin_ref, late_ref, refs[n_arr + 8:])

        @pl.when(step == 0)
        def _():
            dg_ref[...] = jnp.zeros_like(dg_ref)

        dh = None
        for k, wcol, off, width in plan:
            d = _dot_nt(p_refs[k][:, off:off + width], w_ref[:, wcol:wcol + width])
            dh = d if dh is None else dh + d
        xf = x_ref[...]
        r = lax.rsqrt(jnp.mean(xf * xf, axis=-1, keepdims=True) + EPS)
        xh = xf * r
        dg_ref[...] += jnp.sum(dh * xh, axis=0, keepdims=True)
        dhg = dh * g_ref[...]
        dx_ref[...] = r * (dhg - xh * jnp.mean(dhg * xh, axis=-1, keepdims=True)) + dx2_ref[...]

        @pl.when(step == n_steps - 1)
        def _():
            finish()

    tok = lambda w: pl.BlockSpec((tm, w), lambda i: (i, 0))
    full = lambda shape: pl.BlockSpec(shape, lambda i: (0,) * len(shape))
    any_spec = pl.BlockSpec(memory_space=pl.ANY)
    return pl.pallas_call(
        body, name="dh_dx", grid=(n_steps,),
        in_specs=[tok(a.shape[1]) for a in arrays] + [full((D_MODEL, D_IN)), tok(D_MODEL), full((1, D_MODEL)),
                                                      tok(D_MODEL), any_spec],
        out_specs=[tok(D_MODEL), full((1, D_MODEL)), any_spec],
        out_shape=[jax.ShapeDtypeStruct((s, D_MODEL), F32), jax.ShapeDtypeStruct((1, D_MODEL), F32),
                   jax.ShapeDtypeStruct((N_DEV,) + _GRAD_PIECE_SHAPES[0], BF16)],
        scratch_shapes=_presum_scratch(0, _LATE_CHIPS),
        compiler_params=pltpu.CompilerParams(dimension_semantics=("arbitrary",), vmem_limit_bytes=VMEM_LIMIT,
                                             has_side_effects=True),
    )(*arrays, w_in, x, norm_g, dx2, g_in)


ADAM_GRID = 8


def _adamw(sets):
    n = len(sets)
    c1 = 1.0 - ADAM_B1 ** ADAM_STEP
    c2 = 1.0 - ADAM_B2 ** ADAM_STEP

    def body(*refs):
        for k in range(n):
            w_ref, g_ref, m_ref, v_ref = refs[4 * k:4 * k + 4]
            g_out_ref, d_ref, nm_ref, nv_ref = refs[4 * n + 4 * k:4 * n + 4 * k + 4]
            gv = g_ref[...]
            g_out_ref[...] = gv
            nm = ADAM_B1 * m_ref[...] + (1.0 - ADAM_B1) * gv
            nv = ADAM_B2 * v_ref[...] + (1.0 - ADAM_B2) * (gv * gv)
            d_ref[...] = -ADAM_LR * ((nm / c1) / (jnp.sqrt(nv / c2) + ADAM_EPS) + ADAM_WD * w_ref[...])
            nm_ref[...] = nm
            nv_ref[...] = nv

    in_specs, out_specs, shapes, args = [], [], [], []
    for w, g, m, v, g_tile0 in sets:
        rows, cols = w.shape
        assert rows % (8 * ADAM_GRID) == 0, (rows, cols)
        spec = pl.BlockSpec((rows // ADAM_GRID, cols), lambda i: (i, 0))
        g_spec = pl.BlockSpec((rows // ADAM_GRID, cols), lambda i, t0=g_tile0: (t0 + i, 0))
        in_specs += [spec, g_spec, spec, spec]
        out_specs += [spec] * 4
        shapes += [jax.ShapeDtypeStruct((rows, cols), F32)] * 4
        args += [w, g, m, v]
    outs = pl.pallas_call(
        body, name="adamw", grid=(ADAM_GRID,),
        in_specs=in_specs, out_specs=out_specs, out_shape=shapes,
        compiler_params=_cparams(("parallel",)),
    )(*args)
    return [outs[4 * k:4 * k + 4] for k in range(n)]


def _reduce_grads_tail(grads, g_small, early_slots, late_in_slots):
    n_big = len(grads)
    n_arr = n_big + 1
    shard_shapes = [(2 * r, w) for r, w in _GRAD_PIECE_SHAPES]
    small_piece = (SMALL_PIECE, LANES)

    def body(*refs):
        src = refs[:n_arr]
        early = refs[n_arr:n_arr + n_big]
        late_in = refs[n_arr + n_big]
        n_in = n_arr + n_big + 1
        out = refs[n_in:n_in + n_arr]
        slots = refs[n_in + n_arr:n_in + 2 * n_arr]
        sums = refs[n_in + 2 * n_arr:n_in + 3 * n_arr]
        send1, recv1, send2, recv2, local_sems = refs[n_in + 3 * n_arr:]
        x, y, c = _place()
        me = 4 * x + 2 * y + c

        def piece_of(a, dev):
            return src[a].at[dev] if a == n_big else _grad_piece(src[a], a, dev)

        def late(a, dst_dev, src_dev):
            return pltpu.make_async_remote_copy(
                src_ref=piece_of(a, dst_dev), dst_ref=slots[a].at[src_dev],
                send_sem=send1.at[n_arr * dst_dev + a], recv_sem=recv1.at[n_arr * src_dev + a],
                device_id=_dev_id(dst_dev), device_id_type=MESH)

        def late_arrays(dev):
            return (n_big,)

        def load(a, dev, received):
            return pltpu.make_async_copy(received.at[dev], slots[a].at[dev], local_sems.at[n_arr * dev + a])

        def own(a, dev):
            return pltpu.make_async_copy(piece_of(a, dev), slots[a].at[dev], local_sems.at[n_arr * dev + a])

        for dev in range(N_DEV):
            @pl.when(me == dev)
            def _():
                received = [early[0] if dev in _EARLY_IN_DEVS else late_in] + list(early[1:])
                for a in range(n_arr):
                    own(a, dev).start()
                sources = sorted([dev] + _presum_sources(dev))
                for peer in range(N_DEV):
                    if peer != dev:
                        for a in late_arrays(peer):
                            late(a, peer, dev).start()
                        for a in range(n_big):
                            if peer in sources:
                                load(a, peer, received[a]).start()
                for a in range(n_arr):
                    own(a, dev).wait()
                for peer in range(N_DEV):
                    if peer != dev:
                        for a in late_arrays(dev):
                            late(a, dev, peer).wait_recv()
                        for a in range(n_big):
                            if peer in sources:
                                load(a, peer, received[a]).wait()
                for a in range(n_arr):
                    rows = slots[a].shape[1]
                    step = 64 if rows % 64 == 0 else 8
                    used = sources if a < n_big else list(range(N_DEV))

                    def add_rows(t, carry, a=a, step=step, used=used):
                        r0 = pl.multiple_of(t * step, step)
                        total = slots[a][used[0], pl.ds(r0, step), :].astype(F32)
                        for src_dev in used[1:]:
                            total = total + slots[a][src_dev, pl.ds(r0, step), :].astype(F32)
                        sums[a][pl.ds(r0, step), :] = total
                        return carry

                    lax.fori_loop(0, rows // step, add_rows, 0)

        shares = []
        keeps = []
        for a in range(n_big):
            r, w = _GRAD_PIECE_SHAPES[a]
            dst = out[a].at[pl.ds(pl.multiple_of(c * r, 8), r), :]
            cp = pltpu.make_async_remote_copy(src_ref=sums[a], dst_ref=dst, send_sem=send2.at[a], recv_sem=recv2.at[a],
                                              device_id=(x, y, 1 - c), device_id_type=MESH)
            cp.start()
            shares.append(cp)
            kp = pltpu.make_async_copy(sums[a], dst, local_sems.at[N_DEV * n_arr + a])
            kp.start()
            keeps.append(kp)
        kp = pltpu.make_async_copy(sums[n_big], out[n_big].at[me], local_sems.at[N_DEV * n_arr + n_big])
        kp.start()
        keeps.append(kp)

        def small_share(dst_dev, src_dev):
            return pltpu.make_async_remote_copy(src_ref=sums[n_big], dst_ref=out[n_big].at[src_dev],
                                                send_sem=send2.at[n_big + dst_dev], recv_sem=recv2.at[n_big + src_dev],
                                                device_id=_dev_id(dst_dev), device_id_type=MESH)

        for dev in range(N_DEV):
            @pl.when(me != dev)
            def _():
                small_share(dev, me).start()
        for a in range(n_big):
            r, w = _GRAD_PIECE_SHAPES[a]
            other = out[a].at[pl.ds(pl.multiple_of((1 - c) * r, 8), r), :]
            pltpu.make_async_remote_copy(src_ref=sums[a], dst_ref=other, send_sem=send2.at[a], recv_sem=recv2.at[a],
                                         device_id=(x, y, 1 - c), device_id_type=MESH).wait_recv()
        for dev in range(N_DEV):
            @pl.when(me != dev)
            def _():
                small_share(dev, dev).wait_recv()
                small_share(dev, me).wait_send()
                for a in late_arrays(dev):
                    late(a, dev, me).wait_send()
        for cp in shares:
            cp.wait_send()
        for kp in keeps:
            kp.wait()

    any_spec = pl.BlockSpec(memory_space=pl.ANY)
    return pl.pallas_call(
        body, name="reduce_grads_tail",
        in_specs=[any_spec] * (n_arr + n_big + 1), out_specs=[any_spec] * n_arr,
        out_shape=[jax.ShapeDtypeStruct(sh, F32) for sh in shard_shapes]
        + [jax.ShapeDtypeStruct((N_DEV,) + small_piece, F32)],
        scratch_shapes=[pltpu.VMEM((N_DEV,) + sh, BF16) for sh in _GRAD_PIECE_SHAPES]
        + [pltpu.VMEM((N_DEV,) + small_piece, F32)]
        + [pltpu.VMEM(sh, F32) for sh in _GRAD_PIECE_SHAPES] + [pltpu.VMEM(small_piece, F32)]
        + [pltpu.SemaphoreType.DMA((N_DEV * n_arr,)), pltpu.SemaphoreType.DMA((N_DEV * n_arr,)),
           pltpu.SemaphoreType.DMA((n_big + N_DEV,)), pltpu.SemaphoreType.DMA((n_big + N_DEV,)),
           pltpu.SemaphoreType.DMA((N_DEV * n_arr + n_arr,))],
        compiler_params=pltpu.CompilerParams(vmem_limit_bytes=VMEM_LIMIT, has_side_effects=True),
    )(*grads, g_small, *early_slots, late_in_slots)


W_SPATIAL_ROWS = N_GROUPS * SGU_CHUNK
_REST_PARTS = ("norm_g", "sgu_ln_g", "sgu_ln_b", "b_spatial", "final_norm_g")
REST_ROWS = SMALL_ROWS - W_SPATIAL_ROWS
_LOSS_ROW = W_SPATIAL_ROWS + 8 * len(_REST_PARTS)


def _pack_rest(parts, loss_tile=None):
    rows = []
    for name in _REST_PARTS:
        a = parts[name].reshape(-1, LANES).astype(F32)
        rows.append(jnp.pad(a, ((0, 8 - a.shape[0]), (0, 0))))
    rows.append(jnp.zeros((8, LANES), F32) if loss_tile is None else loss_tile)
    rows.append(jnp.zeros((REST_ROWS - 8 * len(rows), LANES), F32))
    return jnp.concatenate(rows, axis=0)


def _pack_small(parts, loss_tile):
    return jnp.concatenate([parts["w_spatial"].reshape(W_SPATIAL_ROWS, LANES), _pack_rest(parts, loss_tile)], axis=0)


def _unpack_rest(packed, shapes):
    out = {}
    for k, name in enumerate(_REST_PARTS):
        n = math.prod(shapes[name])
        out[name] = packed[8 * k:8 * k + n // LANES].reshape(shapes[name])
    return out


def _local_step(proj, ht, x, target, norm_g, w_in, sgu_ln_g, sgu_ln_b, w_spatial, b_spatial, w_up_a, w_up_b, w_out,
                final_norm_g, bq, bk):
    pos = jnp.arange(SGU_CHUNK)
    keep = (pos[None, :] // SGU_SUBCHUNK) <= (pos[:, None] // SGU_SUBCHUNK)
    w_mask = jnp.where(keep[None], w_spatial, 0.0).astype(BF16)
    w_mask_t = jnp.swapaxes(w_mask, 1, 2)
    bias_full = jnp.repeat(b_spatial.T, GROUP_DIM, axis=1)
    ln_g = sgu_ln_g.reshape(1, D_BRANCH)
    ln_b = sgu_ln_b.reshape(1, D_BRANCH)
    final_g = final_norm_g.reshape(1, D_MODEL)

    o, ya, rsave = _attn_fwd(proj, bq, bk, 2 * ATTN_PAIRS)
    yb = _sgu_fwd(proj, ln_g, ln_b, w_mask, bias_full)
    dzg, do, dyb, dx2, g_out, g_up_a, g_up_b, loss_acc, d_final = _mid(
        proj, ya, yb, o, x, target, final_g, w_up_a, w_up_b, w_out)
    dsgu, d_wsp, d_bsp, d_lng, d_lnb = _sgu_bwd(proj, dyb, ln_g, ln_b, w_mask, w_mask_t, bias_full)
    g_in = _dwin_early(ht, dzg, lambda j: jnp.where(j == 0, COL_ZA, COL_GA - 1 + j), dsgu, lambda j: COL_UB + j)
    dq, dk, dv, *early_slots = _attn_bwd(proj, do, rsave, bq, bk, ATTN_PAIRS, (g_in, g_up_a, g_up_b, g_out))
    g_in = _dwin_pieces(ht, (dq, dk, dv), COL_Q, g_in)
    pieces = [(dq, COL_Q * D_BRANCH, 0, D_BRANCH), (dk, COL_K * D_BRANCH, 0, D_BRANCH),
              (dv, COL_V * D_BRANCH, 0, D_BRANCH), (dzg, COL_ZA * D_BRANCH, 0, D_BRANCH),
              (dsgu, COL_UB * D_BRANCH, 0, 3 * D_BRANCH), (dzg, COL_GA * D_BRANCH, D_BRANCH, 2 * D_MODEL)]
    dx, d_norm, late_in_slots = _dh_dx(pieces, w_in, x, norm_g, dx2, g_in)
    small = {"norm_g": d_norm, "sgu_ln_g": d_lng, "sgu_ln_b": d_lnb, "w_spatial": d_wsp,
             "b_spatial": d_bsp[:, :N_GROUPS].T, "final_norm_g": d_final}
    return loss_acc, dx, (g_in, g_up_a, g_up_b, g_out), small, early_slots, late_in_slots


def kernel(x, norm_g, w_in, sgu_ln_g, sgu_ln_b, w_spatial, b_spatial, w_up_a, w_up_b, w_out, final_norm_g, loss_target, m_norm_g, m_w_in, m_sgu_ln_g, m_sgu_ln_b, m_w_spatial, m_b_spatial, m_w_up_a, m_w_up_b, m_w_out, m_final_norm_g, v_norm_g, v_w_in, v_sgu_ln_g, v_sgu_ln_b, v_w_spatial, v_b_spatial, v_w_up_a, v_w_up_b, v_w_out, v_final_norm_g):
    big_names = ("w_in", "w_up_a", "w_up_b", "w_out")
    names = ("norm_g", "w_in", "sgu_ln_g", "sgu_ln_b", "w_spatial", "b_spatial", "w_up_a", "w_up_b", "w_out",
             "final_norm_g")
    w = dict(norm_g=norm_g, w_in=w_in, sgu_ln_g=sgu_ln_g, sgu_ln_b=sgu_ln_b, w_spatial=w_spatial,
             b_spatial=b_spatial, w_up_a=w_up_a, w_up_b=w_up_b, w_out=w_out, final_norm_g=final_norm_g)
    m = dict(norm_g=m_norm_g, w_in=m_w_in, sgu_ln_g=m_sgu_ln_g, sgu_ln_b=m_sgu_ln_b, w_spatial=m_w_spatial,
             b_spatial=m_b_spatial, w_up_a=m_w_up_a, w_up_b=m_w_up_b, w_out=m_w_out, final_norm_g=m_final_norm_g)
    v = dict(norm_g=v_norm_g, w_in=v_w_in, sgu_ln_g=v_sgu_ln_g, sgu_ln_b=v_sgu_ln_b, w_spatial=v_w_spatial,
             b_spatial=v_b_spatial, w_up_a=v_w_up_a, w_up_b=v_w_up_b, w_out=v_w_out, final_norm_g=v_final_norm_g)
    shapes = {n: w[n].shape for n in names}
    flat2d = lambda a: a.reshape(a.shape[-2:])

    proj, ht, *full = _in_proj_gather(x[0], norm_g, *[flat2d(w[n]) for n in big_names])
    loss, dx, big_grads, small, early_slots, late_in_slots = _local_step(
        proj, ht, x[0], loss_target[0], norm_g, full[0], sgu_ln_g[0], sgu_ln_b[0], w_spatial[0], b_spatial[0],
        full[1], full[2], full[3], final_norm_g, ATTN_Q_BLOCK, ATTN_K_BLOCK)
    packed = _pack_small(small, loss).reshape(N_DEV, SMALL_PIECE, LANES)
    red = _reduce_grads_tail(big_grads, packed, early_slots, late_in_slots)

    grads, deltas, new_m, new_v = {}, {}, {}, {}
    g_small = red[4].reshape(SMALL_ROWS, LANES)
    rows2d = lambda a: a.reshape(W_SPATIAL_ROWS, LANES)
    sets = [(flat2d(w[n]), g, flat2d(m[n]), flat2d(v[n]), 0) for n, g in zip(big_names, red[:4])]
    sets.append((rows2d(w_spatial), g_small, rows2d(m_w_spatial), rows2d(v_w_spatial), 0))
    sets.append((_pack_rest(w), g_small, _pack_rest(m), _pack_rest(v), W_SPATIAL_ROWS * ADAM_GRID // REST_ROWS))
    updated = _adamw(sets)
    for n, (g, d, nm, nv) in zip(big_names + ("w_spatial",), updated[:5]):
        grads[n], deltas[n], new_m[n], new_v[n] = (a.reshape(shapes[n]) for a in (g, d, nm, nv))
    g_rest, d, nm, nv = updated[5]
    for src, dst in ((g_rest, grads), (d, deltas), (nm, new_m), (nv, new_v)):
        dst.update(_unpack_rest(src, shapes))

    return (g_rest[_LOSS_ROW - W_SPATIAL_ROWS, 0], dx[None], *[grads[n] for n in names], *[deltas[n] for n in names],
            *[new_m[n] for n in names], *[new_v[n] for n in names])
```

```python
import math

import jax
import jax.numpy as jnp
from jax import lax
from jax.experimental import pallas as pl
from jax.experimental.pallas import tpu as pltpu

F32 = jnp.float32
BF16 = jnp.bfloat16

D_MODEL = 1024
N_HEADS = 8
HEAD_DIM = 64
D_BRANCH = 512
D_IN = 4 * D_BRANCH + 3 * D_BRANCH + 2 * D_MODEL
N_GROUPS = 8
GROUP_DIM = 64
SGU_CHUNK = 128
SGU_SUBCHUNK = 64
GROUP_SHIFT = 6
EPS = 1e-6
LANES = 128
ATTN_Q_BLOCK = 256
ATTN_K_BLOCK = 256
DEAD = -110.0
SKIPPED = -1e30
SCAN_PASSES = 1
ATTN_PAIRS = 2
N_CHIPS = 4
N_DEV = 8
MESH = pl.DeviceIdType.MESH

ADAM_LR = 0.001
ADAM_B1 = 0.9
ADAM_B2 = 0.999
ADAM_EPS = 1e-08
ADAM_WD = 0.01
ADAM_STEP = 10

COL_Q, COL_K, COL_V, COL_ZA, COL_UB, COL_VB, COL_ZB, COL_GA, COL_GB = 0, 1, 2, 3, 4, 5, 6, 7, 9

VMEM_LIMIT = 56 * 1024 * 1024

SMALL_ROWS = 1088
SMALL_PIECE = SMALL_ROWS // N_DEV


def _cparams(sem=None):
    return pltpu.CompilerParams(dimension_semantics=sem, vmem_limit_bytes=VMEM_LIMIT)


def _aligned(v, m):
    return v if isinstance(v, int) else pl.multiple_of(v, m)


def _sigmoid(x):
    return 1.0 / (1.0 + jnp.exp(-x))


def _gelu_and_grad(x):
    k = math.sqrt(2.0 / math.pi)
    x2 = x * x
    inner = k * (x + 0.044715 * x * x2)
    th = jnp.tanh(inner)
    g = 0.5 * x * (1.0 + th)
    dg = 0.5 * (1.0 + th) + 0.5 * x * (1.0 - th * th) * (k * (1.0 + 3.0 * 0.044715 * x2))
    return g, dg


def _split_dot(a, b_bf16, passes):
    out = None
    rem = a
    for _ in range(passes):
        part = rem.astype(BF16)
        d = jnp.dot(part, b_bf16, preferred_element_type=F32)
        out = d if out is None else out + d
        rem = rem - part.astype(F32)
    return out


def _dot_nt(a, b):
    return lax.dot_general(a, b, (((1,), (1,)), ((), ())), preferred_element_type=F32)


def _dot_tn(a, b):
    return lax.dot_general(a, b, (((0,), (0,)), ((), ())), preferred_element_type=F32)


def _place():
    x, y, c = lax.axis_index("x"), lax.axis_index("y"), lax.axis_index("c")
    return x, y, c


def _in_proj_gather(x, norm_g, w_in, w_up_a, w_up_b, w_out):
    s = x.shape[0]
    tm = min(1024, s)
    nt = s // tm
    shards = (w_in, w_up_a, w_up_b, w_out)
    n_arr = len(shards)
    col_sharded = (True, True, True, False)
    full_shapes = ((D_MODEL, D_IN), (D_BRANCH, D_MODEL), (D_BRANCH, D_MODEL), (D_MODEL, D_MODEL))
    w_shard = w_in.shape[1]
    half_rows = D_MODEL // 2
    stage_rows = 256

    def body(order_ref, x_ref, g_ref, *refs):
        src = refs[:n_arr]
        proj_ref, ht_ref = refs[n_arr:n_arr + 2]
        out = refs[n_arr + 2:2 * n_arr + 2]
        wsc, h_scr, stage = refs[2 * n_arr + 2:2 * n_arr + 5]
        small_stage = refs[2 * n_arr + 5:2 * n_arr + 8]
        small_cast = refs[2 * n_arr + 8:2 * n_arr + 11]
        send_sems, recv_sems, local_sems = refs[2 * n_arr + 11:]
        k = pl.program_id(0)
        i = pl.program_id(1)
        x_, y_, c = _place()
        chip = 2 * x_ + y_
        sibling = (x_, y_, 1 - c)
        others = [(x_, 1 - y_), (1 - x_, y_), (1 - x_, 1 - y_)]

        def region(a, chip_idx, half):
            if a == 0:
                return wsc.at[chip_idx, pl.ds(_aligned(half * half_rows, 16), half_rows), :]
            r, w = shards[a].shape
            hr = r // 2
            if col_sharded[a]:
                return out[a].at[pl.ds(_aligned(half * hr, 16), hr), pl.ds(_aligned(chip_idx * w, LANES), w)]
            return out[a].at[pl.ds(_aligned(chip_idx * r + half * hr, 16), hr), :]

        def remote(kk, a, chip_idx, half, to, own):
            s_ref = region(a, chip_idx, half)
            if own and a > 0:
                hr = shards[a].shape[0] // 2
                s_ref = small_cast[a - 1].at[pl.ds(_aligned(half * hr, 16), hr), :]
            return pltpu.make_async_remote_copy(src_ref=s_ref, dst_ref=region(a, chip_idx, half),
                                                send_sem=send_sems.at[kk], recv_sem=recv_sems.at[kk],
                                                device_id=to, device_id_type=MESH)

        def keep_whole(kk, chip_idx):
            return pltpu.make_async_copy(wsc.at[chip_idx],
                                         out[0].at[:, pl.ds(_aligned(chip_idx * w_shard, LANES), w_shard)],
                                         local_sems.at[kk])

        def small_stores():
            cps = []
            for a in range(1, n_arr):
                hr = shards[a].shape[0] // 2
                for half in range(2):
                    cps.append(pltpu.make_async_copy(small_cast[a - 1].at[pl.ds(half * hr, hr), :],
                                                     region(a, chip, half), local_sems.at[4 + 2 * (a - 1) + half]))
            return cps

        def arrive_and_pass(j):
            ochip = chip ^ j
            for a in range(n_arr):
                kk = n_arr * (j - 1) + a
                remote(kk, a, ochip, c, sibling, False).wait_recv()
                remote(3 * n_arr + kk, a, ochip, c, sibling, False).start()

        def from_sibling(j, a):
            remote(3 * n_arr + n_arr * (j - 1) + a, a, chip ^ j, 1 - c, sibling, False).wait_recv()

        @pl.when((k == 0) & (i == 0))
        def _():
            def cast_rows(half):
                for t in range(half_rows // stage_rows):
                    r0 = pl.multiple_of(half * half_rows + t * stage_rows, stage_rows)
                    pltpu.sync_copy(src[0].at[pl.ds(r0, stage_rows), :], stage)
                    wsc[chip, pl.ds(r0, stage_rows), :] = stage[...].astype(BF16)

            cast_rows(c)
            for j in (1, 2):
                remote(n_arr * (j - 1), 0, chip, c, (*others[j - 1], c), True).start()
            cast_rows(1 - c)
            for a in range(1, n_arr):
                pltpu.sync_copy(src[a], small_stage[a - 1])
                small_cast[a - 1][...] = small_stage[a - 1][...].astype(BF16)
            for j in (1, 2):
                for a in range(1, n_arr):
                    remote(n_arr * (j - 1) + a, a, chip, c, (*others[j - 1], c), True).start()
            keep_whole(0, chip).start()
            for cp in small_stores():
                cp.start()

        @pl.when((k == 1) & (i == 0))
        def _():
            for j in (1, 2):
                remote(n_arr * (j - 1), 0, chip, c, (*others[j - 1], c), True).wait_send()
            for a in range(n_arr):
                remote(n_arr * 2 + a, a, chip, c, (*others[2], c), True).start()
            arrive_and_pass(1)
            from_sibling(1, 0)
            keep_whole(1, chip ^ 1).start()

        @pl.when((k == 1) & (i == min(1, nt - 1)))
        def _():
            arrive_and_pass(2)

        @pl.when((k == 2) & (i == 0))
        def _():
            from_sibling(2, 0)
            keep_whole(2, chip ^ 2).start()
            arrive_and_pass(3)

        @pl.when((k == 3) & (i == 0))
        def _():
            from_sibling(3, 0)
            keep_whole(3, chip ^ 3).start()

        @pl.when(k == 0)
        def _():
            xf = x_ref[...]
            r = lax.rsqrt(jnp.mean(xf * xf, axis=-1, keepdims=True) + EPS)
            h = xf * r * g_ref[...]
            h_scr[i] = h.astype(BF16)
            ht_ref[...] = h.T.astype(BF16)

        proj_ref[...] = jnp.dot(h_scr[i], wsc[order_ref[k]], preferred_element_type=F32).astype(BF16)

        @pl.when((k == 3) & (i == nt - 1))
        def _():
            for j in (1, 2, 3):
                for a in range(1, n_arr):
                    from_sibling(j, a)
            for j in (1, 2, 3):
                for a in range(n_arr):
                    kk = n_arr * (j - 1) + a
                    if a > 0 or j == 3:
                        remote(kk, a, chip, c, (*others[j - 1], c), True).wait_send()
                    remote(3 * n_arr + kk, a, chip ^ j, c, sibling, False).wait_send()
            for kk in range(4):
                keep_whole(kk, chip ^ kk).wait()
            for cp in small_stores():
                cp.wait()

    any_spec = pl.BlockSpec(memory_space=pl.ANY)
    tile = lambda kk, ii: jnp.where(kk == 0, ii, nt - 1)
    grid_spec = pltpu.PrefetchScalarGridSpec(
        num_scalar_prefetch=1, grid=(N_CHIPS, nt),
        in_specs=[pl.BlockSpec((tm, D_MODEL), lambda kk, ii, order: (tile(kk, ii), 0)),
                  pl.BlockSpec((1, D_MODEL), lambda kk, ii, order: (0, 0))] + [any_spec] * n_arr,
        out_specs=[pl.BlockSpec((tm, w_shard), lambda kk, ii, order: (ii, order[kk])),
                   pl.BlockSpec((D_MODEL, tm), lambda kk, ii, order: (0, tile(kk, ii)))] + [any_spec] * n_arr,
        scratch_shapes=[pltpu.VMEM((N_CHIPS, D_MODEL, w_shard), BF16), pltpu.VMEM((nt, tm, D_MODEL), BF16),
                        pltpu.VMEM((stage_rows, w_shard), F32)]
        + [pltpu.VMEM(a.shape, F32) for a in shards[1:]] + [pltpu.VMEM(a.shape, BF16) for a in shards[1:]]
        + [pltpu.SemaphoreType.DMA((6 * n_arr,)), pltpu.SemaphoreType.DMA((6 * n_arr,)),
           pltpu.SemaphoreType.DMA((4 + 2 * (n_arr - 1),))])
    x_, y_, _ = _place()
    order = (2 * x_ + y_) ^ jnp.arange(N_CHIPS, dtype=jnp.int32)
    return pl.pallas_call(
        body, name="in_proj_gather", grid_spec=grid_spec,
        out_shape=[jax.ShapeDtypeStruct((s, D_IN), BF16), jax.ShapeDtypeStruct((D_MODEL, s), BF16)]
        + [jax.ShapeDtypeStruct(sh, BF16) for sh in full_shapes],
        compiler_params=pltpu.CompilerParams(dimension_semantics=("arbitrary", "arbitrary"),
                                             vmem_limit_bytes=VMEM_LIMIT, has_side_effects=True),
    )(order, x, norm_g, *shards)


def _neg_softplus_parts(z):
    zb = z.astype(BF16)
    p = jnp.exp(-jnp.abs(zb))
    return p, jnp.maximum(zb, jnp.zeros_like(zb)) + jnp.log(1.0 + p)


def _split_cat(a, passes):
    parts = []
    rem = a
    for k in range(passes):
        part = rem.astype(BF16)
        parts.append(part)
        if k + 1 < passes:
            rem = rem - part.astype(F32)
    return parts[0] if passes == 1 else jnp.concatenate(parts, axis=1)


def _tri(blk, upper, sign):
    row = lax.broadcasted_iota(jnp.int32, (blk, blk), 0)
    col = lax.broadcasted_iota(jnp.int32, (blk, blk), 1)
    keep = (row <= col) if upper else (row >= col)
    t = jnp.where(keep, sign, 0.0).astype(BF16)
    return t if SCAN_PASSES == 1 else jnp.concatenate([t] * SCAN_PASSES, axis=0)


def _attn_fwd(proj, bq, bk, npairs):
    s = proj.shape[0]
    nq = s // bq
    ratio = bq // bk
    scale = HEAD_DIM ** -0.5
    heads = tuple(range(2 * npairs))
    width = LANES * npairs

    def body(q_ref, k_ref, v_ref, za_ref, o_ref, ya_ref, rs_ref, acc_ref, r_ref):
        i = pl.program_id(1)
        lane = lax.broadcasted_iota(jnp.int32, (bq, LANES), 1)
        lo_half = lane < HEAD_DIM
        qm = []
        for pr in range(npairs):
            q = q_ref[:, LANES * pr:LANES * (pr + 1)] * jnp.asarray(scale, BF16)
            zero = jnp.zeros_like(q)
            qm += [jnp.where(lo_half, q, zero), jnp.where(lo_half, zero, q)]
        row = lax.broadcasted_iota(jnp.int32, (bq, bk), 0)
        col = lax.broadcasted_iota(jnp.int32, (bq, bk), 1)
        tneg = _tri(bk, False, -1.0)
        acc_ref[...] = jnp.zeros_like(acc_ref)
        r_ref[...] = jnp.zeros_like(r_ref)
        rs_ref[...] = jnp.full_like(rs_ref, SKIPPED)

        def scores(j):
            ks = pl.multiple_of(j * bk, bk)
            return [_dot_nt(qm[h], k_ref[pl.ds(ks, bk), LANES * (h // 2):LANES * (h // 2 + 1)]) for h in heads]

        def block(j, diag, valid=None):
            ks = pl.multiple_of(j * bk, bk)
            vj = [v_ref[pl.ds(ks, bk), LANES * pr:LANES * (pr + 1)] for pr in range(npairs)]
            if diag:
                before = (j * bk + col) < (i * bq + row)
            z = scores(j)
            sp = [_neg_softplus_parts(z[h])[1] for h in heads]
            if diag:
                sp = [jnp.where(before, sp[h], 0.0) for h in heads]
            cin = [jnp.dot(_split_cat(sp[h], SCAN_PASSES), tneg, preferred_element_type=F32) for h in heads]
            w = [jnp.exp(z[h] + cin[h]) for h in heads]
            if diag:
                w = [jnp.where(before, w[h], 0.0) for h in heads]
            pv = [jnp.dot(w[h].astype(BF16), vj[h // 2], preferred_element_type=F32) for h in heads]
            r = [r_ref[h] for h in heads]
            keep = 1.0 if valid is None else valid.astype(F32)
            for h in heads:
                acc_ref[h] += pv[h] * (jnp.exp(r[h]) * keep)
                r_ref[h] = r[h] + cin[h][:, 0:1] * keep
            for pr in range(npairs):
                hit = [lane == j, lane == j + HEAD_DIM]
                if valid is not None:
                    hit = [m & valid for m in hit]
                rs_ref[pr] = jnp.where(hit[0], r[2 * pr], jnp.where(hit[1], r[2 * pr + 1], rs_ref[pr]))

        for t in range(ratio):
            block(i * ratio + ratio - 1 - t, True)
        block(jnp.maximum(i * ratio - 1, 0), False, valid=i > 0)

        def alive(carry):
            jj, r_max = carry
            return (jj < i * ratio - 1) & (r_max > DEAD)

        def loop_body(carry):
            jj, _ = carry
            block(i * ratio - 2 - jj, False)
            return jj + 1, jnp.max(r_ref[...])

        lax.while_loop(alive, loop_body, (0, jnp.max(r_ref[...])))
        for pr in range(npairs):
            cols = slice(LANES * pr, LANES * (pr + 1))
            o = jnp.where(lo_half, acc_ref[2 * pr], acc_ref[2 * pr + 1])
            o_ref[:, cols] = o.astype(BF16)
            za = za_ref[:, cols].astype(F32)
            ya_ref[:, cols] = (o * (za * _sigmoid(za))).astype(BF16)

    n_steps = N_HEADS // (2 * npairs)
    return pl.pallas_call(
        body, name="attn_fwd", grid=(n_steps, nq),
        in_specs=[pl.BlockSpec((bq, width), lambda p, i: (i, n_steps * COL_Q + p)),
                  pl.BlockSpec((s, width), lambda p, i: (0, n_steps * COL_K + p)),
                  pl.BlockSpec((s, width), lambda p, i: (0, n_steps * COL_V + p)),
                  pl.BlockSpec((bq, width), lambda p, i: (i, n_steps * COL_ZA + p))],
        out_specs=[pl.BlockSpec((bq, width), lambda p, i: (i, p)),
                   pl.BlockSpec((bq, width), lambda p, i: (i, p)),
                   pl.BlockSpec((npairs, bq, LANES), lambda p, i: (p, i, 0))],
        out_shape=[jax.ShapeDtypeStruct((s, D_BRANCH), BF16), jax.ShapeDtypeStruct((s, D_BRANCH), BF16),
                   jax.ShapeDtypeStruct((N_HEADS // 2, s, LANES), F32)],
        scratch_shapes=[pltpu.VMEM((2 * npairs, bq, LANES), F32), pltpu.VMEM((2 * npairs, bq, 1), F32)],
        compiler_params=_cparams(("parallel", "parallel")),
    )(proj, proj, proj, proj)


_GRAD_COL_SHARDED = (True, True, True, False)
_GRAD_FULL_SHAPES = ((D_MODEL, D_IN), (D_BRANCH, D_MODEL), (D_BRANCH, D_MODEL), (D_MODEL, D_MODEL))
_GRAD_PIECE_SHAPES = tuple((r // 2, w // N_CHIPS) if cs else (r // (2 * N_CHIPS), w)
                           for (r, w), cs in zip(_GRAD_FULL_SHAPES, _GRAD_COL_SHARDED))
_EARLY_IN_DEVS = (4, 5, 6, 7)
_LATE_IN_DEVS = (0, 1, 2, 3)
_LATE_CHIPS = (0, 1)
_EARLY_CHIPS = (2, 3)
_ALL_CHIPS = (0, 1, 2, 3)


def _grad_piece(ref, a, dev):
    r, w = _GRAD_PIECE_SHAPES[a]
    if _GRAD_COL_SHARDED[a]:
        return ref.at[pl.ds((dev % 2) * r, r), pl.ds((dev // 2) * w, w)]
    return ref.at[pl.ds(dev * r, r), :]


def _dev_id(dev):
    return (dev // 4, (dev // 2) % 2, dev % 2)


def _me():
    return 4 * lax.axis_index("x") + 2 * lax.axis_index("y") + lax.axis_index("c")


def _presum_copy(src, dst, send_sem, recv_sem, to_dev):
    return pltpu.make_async_remote_copy(src_ref=src, dst_ref=dst, send_sem=send_sem, recv_sem=recv_sem,
                                        device_id=_dev_id(to_dev), device_id_type=MESH)


def _presum_hand_off(dev, a, dest_chips, g_ref, slots, pair, send_sems, recv_sems):
    chip, core = dev // 2, dev % 2
    cps = []
    for k, q in enumerate(dest_chips):
        piece = _grad_piece(g_ref, a, 2 * q + 1 - core)
        if q == chip:
            cps.append(_presum_copy(piece, slots.at[dev], send_sems.at[N_DEV + k], recv_sems.at[dev], dev ^ 1))
        else:
            cps.append(_presum_copy(piece, pair.at[k], send_sems.at[N_DEV + k], recv_sems.at[N_DEV + k], dev ^ 1))
    return cps


def _presum_sends(dev, a, dest_chips, slots, sums, send_sems, recv_sems):
    chip, core = dev // 2, dev % 2
    return [_presum_copy(sums.at[k], slots.at[dev], send_sems.at[2 * q + core], recv_sems.at[dev], 2 * q + core)
            for k, q in enumerate(dest_chips) if q != chip]


def _presum_loads(dev, a, dest_chips, g_ref, stage, load_sems):
    chip, core = dev // 2, dev % 2
    return [pltpu.make_async_copy(_grad_piece(g_ref, a, 2 * q + core), stage.at[k], load_sems.at[k])
            for k, q in enumerate(dest_chips) if q != chip]


def _presum_send(dev, a, dest_chips, g_ref, slots, pair, stage, sums, send_sems, recv_sems, load_sems):
    chip, core = dev // 2, dev % 2
    hand = _presum_hand_off(dev, a, dest_chips, g_ref, slots, pair, send_sems, recv_sems)
    for cp in _presum_loads(dev, a, dest_chips, g_ref, stage, load_sems):
        cp.wait()
    for k, q in enumerate(dest_chips):
        if q != chip:
            hand[k].wait_recv()
            sums[k] = (stage[k].astype(F32) + pair[k].astype(F32)).astype(BF16)
    for cp in _presum_sends(dev, a, dest_chips, slots, sums, send_sems, recv_sems):
        cp.start()


def _presum_wait(dev, a, dest_chips, g_ref, slots, pair, sums, send_sems, recv_sems):
    chip, core = dev // 2, dev % 2
    for cp in _presum_hand_off(dev, a, dest_chips, g_ref, slots, pair, send_sems, recv_sems):
        cp.wait_send()
    for cp in _presum_sends(dev, a, dest_chips, slots, sums, send_sems, recv_sems):
        cp.wait_send()
    if chip in dest_chips:
        for src_dev in _presum_sources(dev):
            _presum_copy(sums.at[0], slots.at[src_dev], send_sems.at[src_dev], recv_sems.at[src_dev], src_dev).wait_recv()


def _presum_sources(dev):
    return [dev ^ 1] + [2 * r + dev % 2 for r in range(N_CHIPS) if r != dev // 2]


def _presum_scratch(a, dest_chips):
    n = len(dest_chips)
    piece = _GRAD_PIECE_SHAPES[a]
    return [pltpu.VMEM((n,) + piece, BF16), pltpu.VMEM((n,) + piece, BF16), pltpu.VMEM((n,) + piece, BF16),
            pltpu.SemaphoreType.DMA((N_DEV + n,)), pltpu.SemaphoreType.DMA((N_DEV + n,)),
            pltpu.SemaphoreType.DMA((n,))]


PRESUM_SCRATCH = 6


def _presum_program(first, second, last, a, dest_chips, g_ref, slots, scratch):
    pair, stage, sums, send_sems, recv_sems, load_sems = scratch
    me = _me()

    @pl.when(first)
    def _():
        for dev in range(N_DEV):
            @pl.when(me == dev)
            def _():
                for cp in _presum_hand_off(dev, a, dest_chips, g_ref, slots, pair, send_sems, recv_sems):
                    cp.start()
                for cp in _presum_loads(dev, a, dest_chips, g_ref, stage, load_sems):
                    cp.start()

    @pl.when(second)
    def _():
        for dev in range(N_DEV):
            @pl.when(me == dev)
            def _():
                _presum_send(dev, a, dest_chips, g_ref, slots, pair, stage, sums, send_sems, recv_sems, load_sems)

    def finish():
        for dev in range(N_DEV):
            @pl.when(me == dev)
            def _():
                _presum_wait(dev, a, dest_chips, g_ref, slots, pair, sums, send_sems, recv_sems)

    return finish


def _attn_bwd(proj, do, rsave, bq, bk, npairs, grads):
    plan = ((0, _EARLY_CHIPS), (1, _ALL_CHIPS), (2, _ALL_CHIPS), (3, _ALL_CHIPS))
    s = proj.shape[0]
    nq = s // bq
    ratio = bq // bk
    scale = HEAD_DIM ** -0.5
    heads = tuple(range(2 * npairs))
    width = LANES * npairs

    n_steps = N_HEADS // (2 * npairs)
    n_g = len(grads)

    def body(q_ref, k_ref, v_ref, do_ref, rs_ref, *refs):
        g_src = refs[:n_g]
        dq_ref, dk_ref, dv_ref = refs[n_g:n_g + 3]
        g_slots = refs[n_g + 3:2 * n_g + 3]
        dk_acc, dv_acc, dq_acc, e_ref = refs[2 * n_g + 3:2 * n_g + 7]
        i = pl.program_id(1)
        step = pl.program_id(0) * nq + i
        finish = [_presum_program(step == 0, step == 1, step == n_steps * nq - 1, a, chips, g_src[pos], g_slots[pos],
                                  refs[2 * n_g + 7 + PRESUM_SCRATCH * pos:2 * n_g + 7 + PRESUM_SCRATCH * (pos + 1)])
                  for pos, (a, chips) in enumerate(plan)]

        lane = lax.broadcasted_iota(jnp.int32, (bq, LANES), 1)
        lo_half = lane < HEAD_DIM
        qm, dom = [], []
        for pr in range(npairs):
            cols = slice(LANES * pr, LANES * (pr + 1))
            q = q_ref[:, cols] * jnp.asarray(scale, BF16)
            zero = jnp.zeros_like(q)
            qm += [jnp.where(lo_half, q, zero), jnp.where(lo_half, zero, q)]
            dout = do_ref[:, cols].astype(F32)
            dom += [jnp.where(lo_half, dout, 0.0), jnp.where(lo_half, 0.0, dout)]
        row = lax.broadcasted_iota(jnp.int32, (bq, bk), 0)
        col = lax.broadcasted_iota(jnp.int32, (bq, bk), 1)
        tneg = _tri(bk, False, -1.0)
        tfwd = _tri(bk, True, 1.0)

        @pl.when(i == 0)
        def _():
            dk_acc[...] = jnp.zeros_like(dk_acc)
            dv_acc[...] = jnp.zeros_like(dv_acc)

        dq_acc[...] = jnp.zeros_like(dq_acc)
        e_ref[...] = jnp.zeros_like(e_ref)

        def block(j, diag, valid=None):
            ks = pl.multiple_of(j * bk, bk)
            kj = [k_ref[pl.ds(ks, bk), LANES * pr:LANES * (pr + 1)] for pr in range(npairs)]
            vj = [v_ref[pl.ds(ks, bk), LANES * pr:LANES * (pr + 1)] for pr in range(npairs)]
            if diag:
                before = (j * bk + col) < (i * bq + row)
            z = [_dot_nt(qm[h], kj[h // 2]) for h in heads]
            er = [jnp.exp(jnp.sum(jnp.where(lane == j + HEAD_DIM * (h % 2), rs_ref[h // 2], 0.0), axis=-1,
                                  keepdims=True)) for h in heads]
            if valid is not None:
                er = [er[h] * valid.astype(F32) for h in heads]
            dos = [(dom[h] * er[h]).astype(BF16) for h in heads]
            dw = [_dot_nt(dos[h], vj[h // 2]) for h in heads]
            psp = [_neg_softplus_parts(z[h]) for h in heads]
            sp = [psp[h][1] for h in heads]
            if diag:
                sp = [jnp.where(before, sp[h], 0.0) for h in heads]
            cin = [jnp.dot(_split_cat(sp[h], SCAN_PASSES), tneg, preferred_element_type=F32) for h in heads]
            w = [jnp.exp(z[h] + cin[h]) for h in heads]
            if diag:
                w = [jnp.where(before, w[h], 0.0) for h in heads]
            e =[dw[h] * w[h] for h in heads]
            eincl = [jnp.dot(_split_cat(e[h], SCAN_PASSES), tfwd, preferred_element_type=F32) + e_ref[h]
                     for h in heads]
            dz = []
            for h in heads:
                p = psp[h][0]
                beta = jnp.where(z[h] >= 0.0, 1.0, p) / (1.0 + p)
                d = e[h] - beta * eincl[h]
                dz.append((jnp.where(before, d, 0.0) if diag else d).astype(BF16))
            wb = [w[h].astype(BF16) for h in heads]
            for h in heads:
                e_ref[h] = eincl[h][:, bk - 1:bk]
                dq_acc[h] += jnp.dot(dz[h], kj[h // 2], preferred_element_type=F32)
            for pr in range(npairs):
                cols = slice(LANES * pr, LANES * (pr + 1))
                h0, h1 = 2 * pr, 2 * pr + 1
                dk_acc[pl.ds(ks, bk), cols] += _dot_tn(dz[h0], qm[h0]) + _dot_tn(dz[h1], qm[h1])
                dv_acc[pl.ds(ks, bk), cols] += _dot_tn(wb[h0], dos[h0]) + _dot_tn(wb[h1], dos[h1])

        def loop_body(j, carry):
            block(j, False)
            return carry

        block_of_lane = lane & (HEAD_DIM - 1)
        live = jnp.max(rs_ref[...], axis=0) > DEAD
        first_live = jnp.min(jnp.where(live, block_of_lane, nq * ratio))
        last = jnp.maximum(i * ratio - 1, 0)
        lax.fori_loop(jnp.minimum(first_live, last), last, loop_body, 0)
        block(last, False, valid=i > 0)
        for t in range(ratio):
            block(i * ratio + t, True)
        for pr in range(npairs):
            dq = jnp.where(lo_half, dq_acc[2 * pr], dq_acc[2 * pr + 1]) * scale
            dq_ref[:, LANES * pr:LANES * (pr + 1)] = dq.astype(BF16)

        @pl.when(i == nq - 1)
        def _():
            dk_ref[...] = dk_acc[...].astype(BF16)
            dv_ref[...] = dv_acc[...].astype(BF16)

        @pl.when(step == n_steps * nq - 1)
        def _():
            for fin in finish:
                fin()

    any_spec = pl.BlockSpec(memory_space=pl.ANY)
    return pl.pallas_call(
        body, name="attn_bwd", grid=(n_steps, nq),
        in_specs=[pl.BlockSpec((bq, width), lambda p, i: (i, n_steps * COL_Q + p)),
                  pl.BlockSpec((s, width), lambda p, i: (0, n_steps * COL_K + p)),
                  pl.BlockSpec((s, width), lambda p, i: (0, n_steps * COL_V + p)),
                  pl.BlockSpec((bq, width), lambda p, i: (i, p)),
                  pl.BlockSpec((npairs, bq, LANES), lambda p, i: (p, i, 0))] + [any_spec] * n_g,
        out_specs=[pl.BlockSpec((bq, width), lambda p, i: (i, p)),
                   pl.BlockSpec((s, width), lambda p, i: (0, p)),
                   pl.BlockSpec((s, width), lambda p, i: (0, p))] + [any_spec] * n_g,
        out_shape=[jax.ShapeDtypeStruct((s, D_BRANCH), BF16)] * 3
        + [jax.ShapeDtypeStruct((N_DEV,) + _GRAD_PIECE_SHAPES[a], BF16) for a, _ in plan],
        scratch_shapes=[pltpu.VMEM((s, width), F32), pltpu.VMEM((s, width), F32),
                        pltpu.VMEM((2 * npairs, bq, LANES), F32), pltpu.VMEM((2 * npairs, bq, 1), F32)]
        + [sh for a, chips in plan for sh in _presum_scratch(a, chips)],
        compiler_params=pltpu.CompilerParams(dimension_semantics=("arbitrary", "arbitrary"),
                                             vmem_limit_bytes=VMEM_LIMIT, has_side_effects=True),
    )(proj, proj, proj, do, rsave, *grads)


def _group_avg_matrix():
    a = lax.broadcasted_iota(jnp.int32, (LANES, LANES), 0) >> GROUP_SHIFT
    b = lax.broadcasted_iota(jnp.int32, (LANES, LANES), 1) >> GROUP_SHIFT
    return jnp.where(a == b, 1.0 / GROUP_DIM, 0.0).astype(BF16)


def _group_mean(a, avg):
    parts = [_split_dot(a[:, LANES * k:LANES * (k + 1)], avg, 2) for k in range(D_BRANCH // LANES)]
    return jnp.concatenate(parts, axis=1)


def _sgu_forward_parts(ub, vb, ln_g, ln_b, avg):
    ug, dug = _gelu_and_grad(ub)
    vg, dvg = _gelu_and_grad(vb)
    mu = _group_mean(vg, avg)
    d = vg - mu
    var = _group_mean(d * d, avg)
    rstd = lax.rsqrt(var + EPS)
    vhat = d * rstd
    vn = vhat * ln_g + ln_b
    return ug, dug, dvg, rstd, vhat, vn


def _sgu_mix(w_ref, src_bf16, n_chunks):
    lane = lax.broadcasted_iota(jnp.int32, (SGU_CHUNK, LANES), 1)
    lo_half = lane < GROUP_DIM
    rows = []
    for n in range(n_chunks):
        slabs = []
        for a in range(D_BRANCH // LANES):
            blk = src_bf16[SGU_CHUNK * n:SGU_CHUNK * (n + 1), LANES * a:LANES * (a + 1)]
            zero = jnp.zeros_like(blk)
            m0 = jnp.dot(w_ref[2 * a], jnp.where(lo_half, blk, zero), preferred_element_type=F32)
            m1 = jnp.dot(w_ref[2 * a + 1], jnp.where(lo_half, zero, blk), preferred_element_type=F32)
            slabs.append(m0 + m1)
        rows.append(jnp.concatenate(slabs, axis=1))
    return jnp.concatenate(rows, axis=0)


def _sgu_fwd(proj, ln_g, ln_b, w_mask, bias_full):
    s = proj.shape[0]
    tm = min(512, s)
    n_chunks = tm // SGU_CHUNK

    def body(ub_ref, vb_ref, zb_ref, g_ref, b_ref, w_ref, bias_ref, yb_ref):
        avg = _group_avg_matrix()
        ug, _, _, _, _, vn = _sgu_forward_parts(ub_ref[...].astype(F32), vb_ref[...].astype(F32),
                                                g_ref[...], b_ref[...], avg)
        mixed = _sgu_mix(w_ref, vn.astype(BF16), n_chunks) + jnp.concatenate([bias_ref[...]] * n_chunks, axis=0)
        zb = zb_ref[...].astype(F32)
        yb_ref[...] = (ug * mixed * (zb * _sigmoid(zb))).astype(BF16)

    col = lambda c: pl.BlockSpec((tm, D_BRANCH), lambda i: (i, c))
    full = lambda shape: pl.BlockSpec(shape, lambda i: (0,) * len(shape))
    return pl.pallas_call(
        body, name="sgu_fwd", grid=(s // tm,),
        in_specs=[col(COL_UB), col(COL_VB), col(COL_ZB), full((1, D_BRANCH)), full((1, D_BRANCH)),
                  full((N_GROUPS, SGU_CHUNK, SGU_CHUNK)), full((SGU_CHUNK, D_BRANCH))],
        out_specs=pl.BlockSpec((tm, D_BRANCH), lambda i: (i, 0)),
        out_shape=jax.ShapeDtypeStruct((s, D_BRANCH), BF16),
        compiler_params=_cparams(("parallel",)),
    )(proj, proj, proj, ln_g, ln_b, w_mask, bias_full)


def _sgu_bwd(proj, dyb, ln_g, ln_b, w_mask, w_mask_t, bias_full):
    s = proj.shape[0]
    tm = min(512, s)
    n_chunks = tm // SGU_CHUNK
    n_steps = s // tm

    def body(ub_ref, vb_ref, zb_ref, dyb_ref, g_ref, b_ref, w_ref, wt_ref, bias_ref,
             dsgu_ref, dw_ref, db_ref, dg_ref, dbeta_ref, dmix_acc):
        i = pl.program_id(0)

        @pl.when(i == 0)
        def _():
            dw_ref[...] = jnp.zeros_like(dw_ref)
            dg_ref[...] = jnp.zeros_like(dg_ref)
            dbeta_ref[...] = jnp.zeros_like(dbeta_ref)
            dmix_acc[...] = jnp.zeros_like(dmix_acc)

        avg = _group_avg_matrix()
        ln_gv = g_ref[...]
        ug, dug, dvg, rstd, vhat, vn = _sgu_forward_parts(ub_ref[...].astype(F32), vb_ref[...].astype(F32),
                                                          ln_gv, b_ref[...], avg)
        vnb = vn.astype(BF16)
        mixed = _sgu_mix(w_ref, vnb, n_chunks) + jnp.concatenate([bias_ref[...]] * n_chunks, axis=0)
        zb = zb_ref[...].astype(F32)
        sg = _sigmoid(zb)
        sz = zb * sg
        dsz = sg * (1.0 + zb * (1.0 - sg))
        dy = dyb_ref[...].astype(F32)
        dmixed = dy * ug * sz
        du = dy * mixed * sz * dug
        dzb = dy * ug * mixed * dsz
        dmb = dmixed.astype(BF16)
        dvn = _sgu_mix(wt_ref, dmb, n_chunks)

        lane = lax.broadcasted_iota(jnp.int32, (SGU_CHUNK, LANES), 1)
        lo_half = lane < GROUP_DIM
        dm_sum = None
        for n in range(n_chunks):
            rows = slice(SGU_CHUNK * n, SGU_CHUNK * (n + 1))
            dm_sum = dmixed[rows] if dm_sum is None else dm_sum + dmixed[rows]
            for a in range(D_BRANCH // LANES):
                cols = slice(LANES * a, LANES * (a + 1))
                dblk = dmb[rows, cols]
                vblk = vnb[rows, cols]
                zero = jnp.zeros_like(dblk)
                dw_ref[2 * a] += _dot_nt(jnp.where(lo_half, dblk, zero), vblk)
                dw_ref[2 * a + 1] += _dot_nt(jnp.where(lo_half, zero, dblk), vblk)
        dmix_acc[...] += dm_sum

        dg_ref[...] += jnp.sum(dvn * vhat, axis=0, keepdims=True)
        dbeta_ref[...] += jnp.sum(dvn, axis=0, keepdims=True)
        dvh = dvn * ln_gv
        m1 = _group_mean(dvh, avg)
        m2 = _group_mean(dvh * vhat, avg)
        dv = rstd * (dvh - m1 - vhat * m2) * dvg
        dsgu_ref[:, 0:D_BRANCH] = du.astype(BF16)
        dsgu_ref[:, D_BRANCH:2 * D_BRANCH] = dv.astype(BF16)
        dsgu_ref[:, 2 * D_BRANCH:3 * D_BRANCH] = dzb.astype(BF16)

        @pl.when(i == n_steps - 1)
        def _():
            pos = lax.broadcasted_iota(jnp.int32, (SGU_CHUNK, SGU_CHUNK), 0) >> GROUP_SHIFT
            src = lax.broadcasted_iota(jnp.int32, (SGU_CHUNK, SGU_CHUNK), 1) >> GROUP_SHIFT
            keep = src <= pos
            for g in range(N_GROUPS):
                dw_ref[g] = jnp.where(keep, dw_ref[g], 0.0)
            grp = lax.broadcasted_iota(jnp.int32, (D_BRANCH, LANES), 0) >> GROUP_SHIFT
            sel = (grp == lax.broadcasted_iota(jnp.int32, (D_BRANCH, LANES), 1)).astype(BF16)
            db_ref[...] = _split_dot(dmix_acc[...], sel, 3)

    col = lambda c: pl.BlockSpec((tm, D_BRANCH), lambda i: (i, c))
    full = lambda shape: pl.BlockSpec(shape, lambda i: (0,) * len(shape))
    return pl.pallas_call(
        body, name="sgu_bwd", grid=(n_steps,),
        in_specs=[col(COL_UB), col(COL_VB), col(COL_ZB), pl.BlockSpec((tm, D_BRANCH), lambda i: (i, 0)),
                  full((1, D_BRANCH)), full((1, D_BRANCH)),
                  full((N_GROUPS, SGU_CHUNK, SGU_CHUNK)), full((N_GROUPS, SGU_CHUNK, SGU_CHUNK)),
                  full((SGU_CHUNK, D_BRANCH))],
        out_specs=[pl.BlockSpec((tm, 3 * D_BRANCH), lambda i: (i, 0)),
                   full((N_GROUPS, SGU_CHUNK, SGU_CHUNK)), full((SGU_CHUNK, LANES)),
                   full((1, D_BRANCH)), full((1, D_BRANCH))],
        out_shape=[jax.ShapeDtypeStruct((s, 3 * D_BRANCH), BF16),
                   jax.ShapeDtypeStruct((N_GROUPS, SGU_CHUNK, SGU_CHUNK), F32),
                   jax.ShapeDtypeStruct((SGU_CHUNK, LANES), F32),
                   jax.ShapeDtypeStruct((1, D_BRANCH), F32), jax.ShapeDtypeStruct((1, D_BRANCH), F32)],
        scratch_shapes=[pltpu.VMEM((SGU_CHUNK, D_BRANCH), F32)],
        compiler_params=_cparams(("arbitrary",)),
    )(proj, proj, proj, dyb, ln_g, ln_b, w_mask, w_mask_t, bias_full)


def _mid(proj, ya, yb, o, x, target, final_g, w_up_a, w_up_b, w_out):
    s = x.shape[0]
    tm = min(256, s)
    n_steps = s // tm
    half = D_MODEL // 2

    def body(ya_ref, yb_ref, o_ref, za_ref, ga0_ref, ga1_ref, gb0_ref, gb1_ref, x_ref, t_ref, gf_ref,
             wa_ref, wb_ref, wo_ref,
             dzg_ref, do_ref, dyb_ref, dx2_ref, gwo_ref, gwa_ref, gwb_ref, loss_ref, dgf_ref,
             acc_o, acc_a, acc_b):
        i = pl.program_id(0)

        @pl.when(i == 0)
        def _():
            acc_o[...] = jnp.zeros_like(acc_o)
            acc_a[...] = jnp.zeros_like(acc_a)
            acc_b[...] = jnp.zeros_like(acc_b)
            loss_ref[...] = jnp.zeros_like(loss_ref)
            dgf_ref[...] = jnp.zeros_like(dgf_ref)

        ya_v = ya_ref[...]
        yb_v = yb_ref[...]
        pa = jnp.dot(ya_v, wa_ref[...], preferred_element_type=F32)
        pb = jnp.dot(yb_v, wb_ref[...], preferred_element_type=F32)
        sa = _sigmoid(jnp.concatenate([ga0_ref[...], ga1_ref[...]], axis=1).astype(F32))
        sb = _sigmoid(jnp.concatenate([gb0_ref[...], gb1_ref[...]], axis=1).astype(F32))
        merged = (sa * pa + sb * pb).astype(BF16)
        x2 = x_ref[...] + jnp.dot(merged, wo_ref[...], preferred_element_type=F32)
        r2 = lax.rsqrt(jnp.mean(x2 * x2, axis=-1, keepdims=True) + EPS)
        xh = x2 * r2
        gf = gf_ref[...]
        diff = xh * gf - t_ref[...]
        loss_ref[...] += 0.5 * jnp.sum(jnp.mean(diff * diff, axis=-1, keepdims=True))
        dy = diff * (1.0 / D_MODEL)
        dgf_ref[...] += jnp.sum(dy * xh, axis=0, keepdims=True)
        dyg = dy * gf
        dx2 = r2 * (dyg - xh * jnp.mean(dyg * xh, axis=-1, keepdims=True))
        dx2_ref[...] = dx2
        dx2b = dx2.astype(BF16)
        dmerged = _dot_nt(dx2b, wo_ref[...])
        acc_o[...] += _dot_tn(merged, dx2b)
        dpa = dmerged * sa
        dpb = dmerged * sb
        dzg_ref[:, D_BRANCH:D_BRANCH + D_MODEL] = (dpa * pa * (1.0 - sa)).astype(BF16)
        dzg_ref[:, D_BRANCH + D_MODEL:D_BRANCH + 2 * D_MODEL] = (dpb * pb * (1.0 - sb)).astype(BF16)
        dpab = dpa.astype(BF16)
        dpbb = dpb.astype(BF16)
        acc_a[...] += _dot_tn(ya_v, dpab)
        acc_b[...] += _dot_tn(yb_v, dpbb)
        dya = _dot_nt(dpab, wa_ref[...])
        dyb_ref[...] = _dot_nt(dpbb, wb_ref[...]).astype(BF16)
        za = za_ref[...].astype(F32)
        sg = _sigmoid(za)
        do_ref[...] = (dya * (za * sg)).astype(BF16)
        dzg_ref[:, 0:D_BRANCH] = (dya * o_ref[...].astype(F32) * (sg * (1.0 + za * (1.0 - sg)))).astype(BF16)

        @pl.when(i == n_steps - 1)
        def _():
            gwo_ref[...] = acc_o[...].astype(BF16)
            gwa_ref[...] = acc_a[...].astype(BF16)
            gwb_ref[...] = acc_b[...].astype(BF16)

    tok = lambda w: pl.BlockSpec((tm, w), lambda i: (i, 0))
    col = lambda c: pl.BlockSpec((tm, half), lambda i: (i, c))
    full = lambda shape: pl.BlockSpec(shape, lambda i: (0,) * len(shape))
    return pl.pallas_call(
        body, name="mid", grid=(n_steps,),
        in_specs=[tok(D_BRANCH), tok(D_BRANCH), tok(D_BRANCH), col(COL_ZA), col(COL_GA), col(COL_GA + 1),
                  col(COL_GB), col(COL_GB + 1), tok(D_MODEL), tok(D_MODEL), full((1, D_MODEL)),
                  full((D_BRANCH, D_MODEL)), full((D_BRANCH, D_MODEL)), full((D_MODEL, D_MODEL))],
        out_specs=[tok(D_BRANCH + 2 * D_MODEL), tok(D_BRANCH), tok(D_BRANCH), tok(D_MODEL),
                   full((D_MODEL, D_MODEL)), full((D_BRANCH, D_MODEL)), full((D_BRANCH, D_MODEL)),
                   full((8, LANES)), full((1, D_MODEL))],
        out_shape=[jax.ShapeDtypeStruct((s, D_BRANCH + 2 * D_MODEL), BF16),
                   jax.ShapeDtypeStruct((s, D_BRANCH), BF16), jax.ShapeDtypeStruct((s, D_BRANCH), BF16),
                   jax.ShapeDtypeStruct((s, D_MODEL), F32),
                   jax.ShapeDtypeStruct((D_MODEL, D_MODEL), BF16),
                   jax.ShapeDtypeStruct((D_BRANCH, D_MODEL), BF16), jax.ShapeDtypeStruct((D_BRANCH, D_MODEL), BF16),
                   jax.ShapeDtypeStruct((8, LANES), F32), jax.ShapeDtypeStruct((1, D_MODEL), F32)],
        scratch_shapes=[pltpu.VMEM((D_MODEL, D_MODEL), F32), pltpu.VMEM((D_BRANCH, D_MODEL), F32),
                        pltpu.VMEM((D_BRANCH, D_MODEL), F32)],
        compiler_params=_cparams(("arbitrary",)),
    )(ya, yb, o, proj, proj, proj, proj, proj, x, target, final_g, w_up_a, w_up_b, w_out)


def _dwin_early(ht, first, tile_of_first, second, tile_of_second):
    s = ht.shape[1]
    n1 = first.shape[1] // D_BRANCH
    n2 = second.shape[1] // D_BRANCH

    def body(ht_ref, a_ref, b_ref, out_ref):
        j = pl.program_id(0)

        @pl.when(j < n1)
        def _():
            out_ref[...] = jnp.dot(ht_ref[...], a_ref[...], preferred_element_type=F32).astype(BF16)

        @pl.when(j >= n1)
        def _():
            out_ref[...] = jnp.dot(ht_ref[...], b_ref[...], preferred_element_type=F32).astype(BF16)

    return pl.pallas_call(
        body, name="dwin_early", grid=(n1 + n2,),
        in_specs=[pl.BlockSpec((D_MODEL, s), lambda j: (0, 0)),
                  pl.BlockSpec((s, D_BRANCH), lambda j: (0, jnp.minimum(j, n1 - 1))),
                  pl.BlockSpec((s, D_BRANCH), lambda j: (0, jnp.maximum(j - n1, 0)))],
        out_specs=pl.BlockSpec((D_MODEL, D_BRANCH),
                               lambda j: (0, jnp.where(j < n1, tile_of_first(j), tile_of_second(j - n1)))),
        out_shape=jax.ShapeDtypeStruct((D_MODEL, D_IN), BF16),
        compiler_params=_cparams(("arbitrary",)),
    )(ht, first, second)


def _dwin_pieces(ht, pieces, first_tile, prev):
    s = ht.shape[1]
    n = len(pieces)

    def body(ht_ref, *refs):
        srcs = refs[:n]
        out_ref, buf, sems = refs[n + 1:]
        j = pl.program_id(0)

        @pl.when(j == 0)
        def _():
            for k in range(n):
                pltpu.make_async_copy(srcs[k], buf.at[k], sems.at[k]).start()

        for k in range(n):
            @pl.when(j == k)
            def _():
                pltpu.make_async_copy(srcs[k], buf.at[k], sems.at[k]).wait()

        out_ref[...] = jnp.dot(ht_ref[...], buf[j], preferred_element_type=F32).astype(BF16)

    any_spec = pl.BlockSpec(memory_space=pl.ANY)
    return pl.pallas_call(
        body, name="dwin_pieces", grid=(n,),
        in_specs=[pl.BlockSpec((D_MODEL, s), lambda j: (0, 0))] + [any_spec] * (n + 1),
        out_specs=pl.BlockSpec((D_MODEL, D_BRANCH), lambda j: (0, first_tile + j)),
        out_shape=jax.ShapeDtypeStruct((D_MODEL, D_IN), BF16),
        scratch_shapes=[pltpu.VMEM((n, s, D_BRANCH), BF16), pltpu.SemaphoreType.DMA((n,))],
        input_output_aliases={n + 1: 0},
        compiler_params=_cparams(("arbitrary",)),
    )(ht, *pieces, prev)


def _dh_dx(pieces, w_in, x, norm_g, dx2, g_in):
    s = x.shape[0]
    tm = min(256, s)
    n_steps = s // tm
    arrays = []
    for arr, _, _, _ in pieces:
        if not any(arr is a for a in arrays):
            arrays.append(arr)
    n_arr = len(arrays)
    plan = [([k for k, a in enumerate(arrays) if a is arr][0], wcol, off, width) for arr, wcol, off, width in pieces]

    def body(*refs):
        p_refs = refs[:n_arr]
        w_ref, x_ref, g_ref, dx2_ref, gin_ref, dx_ref, dg_ref, late_ref = refs[n_arr:n_arr + 8]
        step = pl.program_id(0)
        finish = _presum_program(step == 0, step == min(1, n_steps - 1), step == n_steps - 1, 0, _LATE_CHIPS,
                                 gin_ref, late_ref, refs[n_arr + 8:])

        @pl.when(step == 0)
        def _():
            dg_ref[...] = jnp.zeros_like(dg_ref)

        dh = None
        for k, wcol, off, width in plan:
            d = _dot_nt(p_refs[k][:, off:off + width], w_ref[:, wcol:wcol + width])
            dh = d if dh is None else dh + d
        xf = x_ref[...]
        r = lax.rsqrt(jnp.mean(xf * xf, axis=-1, keepdims=True) + EPS)
        xh = xf * r
        dg_ref[...] += jnp.sum(dh * xh, axis=0, keepdims=True)
        dhg = dh * g_ref[...]
        dx_ref[...] = r * (dhg - xh * jnp.mean(dhg * xh, axis=-1, keepdims=True)) + dx2_ref[...]

        @pl.when(step == n_steps - 1)
        def _():
            finish()

    tok = lambda w: pl.BlockSpec((tm, w), lambda i: (i, 0))
    full = lambda shape: pl.BlockSpec(shape, lambda i: (0,) * len(shape))
    any_spec = pl.BlockSpec(memory_space=pl.ANY)
    return pl.pallas_call(
        body, name="dh_dx", grid=(n_steps,),
        in_specs=[tok(a.shape[1]) for a in arrays] + [full((D_MODEL, D_IN)), tok(D_MODEL), full((1, D_MODEL)),
                                                      tok(D_MODEL), any_spec],
        out_specs=[tok(D_MODEL), full((1, D_MODEL)), any_spec],
        out_shape=[jax.ShapeDtypeStruct((s, D_MODEL), F32), jax.ShapeDtypeStruct((1, D_MODEL), F32),
                   jax.ShapeDtypeStruct((N_DEV,) + _GRAD_PIECE_SHAPES[0], BF16)],
        scratch_shapes=_presum_scratch(0, _LATE_CHIPS),
        compiler_params=pltpu.CompilerParams(dimension_semantics=("arbitrary",), vmem_limit_bytes=VMEM_LIMIT,
                                             has_side_effects=True),
    )(*arrays, w_in, x, norm_g, dx2, g_in)


ADAM_GRID = 8


def _adamw(sets):
    n = len(sets)
    c1 = 1.0 - ADAM_B1 ** ADAM_STEP
    c2 = 1.0 - ADAM_B2 ** ADAM_STEP

    def body(*refs):
        for k in range(n):
            w_ref, g_ref, m_ref, v_ref = refs[4 * k:4 * k + 4]
            g_out_ref, d_ref, nm_ref, nv_ref = refs[4 * n + 4 * k:4 * n + 4 * k + 4]
            gv = g_ref[...]
            g_out_ref[...] = gv
            nm = ADAM_B1 * m_ref[...] + (1.0 - ADAM_B1) * gv
            nv = ADAM_B2 * v_ref[...] + (1.0 - ADAM_B2) * (gv * gv)
            d_ref[...] = -ADAM_LR * ((nm / c1) / (jnp.sqrt(nv / c2) + ADAM_EPS) + ADAM_WD * w_ref[...])
            nm_ref[...] = nm
            nv_ref[...] = nv

    in_specs, out_specs, shapes, args = [], [], [], []
    for w, g, m, v, g_tile0 in sets:
        rows, cols = w.shape
        assert rows % (8 * ADAM_GRID) == 0, (rows, cols)
        spec = pl.BlockSpec((rows // ADAM_GRID, cols), lambda i: (i, 0))
        g_spec = pl.BlockSpec((rows // ADAM_GRID, cols), lambda i, t0=g_tile0: (t0 + i, 0))
        in_specs += [spec, g_spec, spec, spec]
        out_specs += [spec] * 4
        shapes += [jax.ShapeDtypeStruct((rows, cols), F32)] * 4
        args += [w, g, m, v]
    outs = pl.pallas_call(
        body, name="adamw", grid=(ADAM_GRID,),
        in_specs=in_specs, out_specs=out_specs, out_shape=shapes,
        compiler_params=_cparams(("parallel",)),
    )(*args)
    return [outs[4 * k:4 * k + 4] for k in range(n)]


def _reduce_grads_tail(grads, g_small, early_slots, late_in_slots):
    n_big = len(grads)
    n_arr = n_big + 1
    shard_shapes = [(2 * r, w) for r, w in _GRAD_PIECE_SHAPES]
    small_piece = (SMALL_PIECE, LANES)

    def body(*refs):
        src = refs[:n_arr]
        early = refs[n_arr:n_arr + n_big]
        late_in = refs[n_arr + n_big]
        n_in = n_arr + n_big + 1
        out = refs[n_in:n_in + n_arr]
        slots = refs[n_in + n_arr:n_in + 2 * n_arr]
        sums = refs[n_in + 2 * n_arr:n_in + 3 * n_arr]
        send1, recv1, send2, recv2, local_sems = refs[n_in + 3 * n_arr:]
        x, y, c = _place()
        me = 4 * x + 2 * y + c

        def piece_of(a, dev):
            return src[a].at[dev] if a == n_big else _grad_piece(src[a], a, dev)

        def late(a, dst_dev, src_dev):
            return pltpu.make_async_remote_copy(
                src_ref=piece_of(a, dst_dev), dst_ref=slots[a].at[src_dev],
                send_sem=send1.at[n_arr * dst_dev + a], recv_sem=recv1.at[n_arr * src_dev + a],
                device_id=_dev_id(dst_dev), device_id_type=MESH)

        def late_arrays(dev):
            return (n_big,)

        def load(a, dev, received):
            return pltpu.make_async_copy(received.at[dev], slots[a].at[dev], local_sems.at[n_arr * dev + a])

        def own(a, dev):
            return pltpu.make_async_copy(piece_of(a, dev), slots[a].at[dev], local_sems.at[n_arr * dev + a])

        def half_of(a, core):
            r, _ = _GRAD_PIECE_SHAPES[a]
            return out[a].at[pl.ds(_aligned(core * r, 8), r), :]

        def share(a, core, sibling):
            return pltpu.make_async_remote_copy(src_ref=sums[a], dst_ref=half_of(a, core), send_sem=send2.at[a],
                                                recv_sem=recv2.at[a], device_id=sibling, device_id_type=MESH)

        def keep(a, dev):
            dst = out[n_big].at[dev] if a == n_big else half_of(a, dev % 2)
            return pltpu.make_async_copy(sums[a], dst, local_sems.at[N_DEV * n_arr + a])

        def small_share(dst_dev, src_dev):
            return pltpu.make_async_remote_copy(src_ref=sums[n_big], dst_ref=out[n_big].at[src_dev],
                                                send_sem=send2.at[n_big + dst_dev], recv_sem=recv2.at[n_big + src_dev],
                                                device_id=_dev_id(dst_dev), device_id_type=MESH)

        for dev in range(N_DEV):
            @pl.when(me == dev)
            def _():
                received = [early[0] if dev in _EARLY_IN_DEVS else late_in] + list(early[1:])
                for a in range(n_arr):
                    own(a, dev).start()
                sources = sorted([dev] + _presum_sources(dev))
                for peer in range(N_DEV):
                    if peer != dev:
                        for a in late_arrays(peer):
                            late(a, peer, dev).start()
                        for a in range(n_big):
                            if peer in sources:
                                load(a, peer, received[a]).start()
                for a in range(n_arr):
                    own(a, dev).wait()
                for peer in range(N_DEV):
                    if peer != dev:
                        for a in late_arrays(dev):
                            late(a, dev, peer).wait_recv()
                        for a in range(n_big):
                            if peer in sources:
                                load(a, peer, received[a]).wait()
                for a in [n_big] + list(range(n_big)):
                    rows = slots[a].shape[1]
                    step = 64 if rows % 64 == 0 else 8
                    used = sources if a < n_big else list(range(N_DEV))

                    def add_rows(t, carry, a=a, step=step, used=used):
                        r0 = pl.multiple_of(t * step, step)
                        total = slots[a][used[0], pl.ds(r0, step), :].astype(F32)
                        for src_dev in used[1:]:
                            total = total + slots[a][src_dev, pl.ds(r0, step), :].astype(F32)
                        sums[a][pl.ds(r0, step), :] = total
                        return carry

                    lax.fori_loop(0, rows // step, add_rows, 0)
                    keep(a, dev).start()
                    if a == n_big:
                        for peer in range(N_DEV):
                            if peer != dev:
                                small_share(peer, dev).start()
                    else:
                        share(a, dev % 2, _dev_id(dev ^ 1)).start()

        for a in range(n_big):
            share(a, 1 - c, (x, y, 1 - c)).wait_recv()
        for dev in range(N_DEV):
            @pl.when(me != dev)
            def _():
                small_share(dev, dev).wait_recv()
                small_share(dev, me).wait_send()
                for a in late_arrays(dev):
                    late(a, dev, me).wait_send()
        for a in range(n_big):
            share(a, c, (x, y, 1 - c)).wait_send()
        for dev in range(N_DEV):
            @pl.when(me == dev)
            def _():
                for a in range(n_arr):
                    keep(a, dev).wait()

    any_spec = pl.BlockSpec(memory_space=pl.ANY)
    return pl.pallas_call(
        body, name="reduce_grads_tail",
        in_specs=[any_spec] * (n_arr + n_big + 1), out_specs=[any_spec] * n_arr,
        out_shape=[jax.ShapeDtypeStruct(sh, F32) for sh in shard_shapes]
        + [jax.ShapeDtypeStruct((N_DEV,) + small_piece, F32)],
        scratch_shapes=[pltpu.VMEM((N_DEV,) + sh, BF16) for sh in _GRAD_PIECE_SHAPES]
        + [pltpu.VMEM((N_DEV,) + small_piece, F32)]
        + [pltpu.VMEM(sh, F32) for sh in _GRAD_PIECE_SHAPES] + [pltpu.VMEM(small_piece, F32)]
        + [pltpu.SemaphoreType.DMA((N_DEV * n_arr,)), pltpu.SemaphoreType.DMA((N_DEV * n_arr,)),
           pltpu.SemaphoreType.DMA((n_big + N_DEV,)), pltpu.SemaphoreType.DMA((n_big + N_DEV,)),
           pltpu.SemaphoreType.DMA((N_DEV * n_arr + n_arr,))],
        compiler_params=pltpu.CompilerParams(vmem_limit_bytes=VMEM_LIMIT, has_side_effects=True),
    )(*grads, g_small, *early_slots, late_in_slots)


W_SPATIAL_ROWS = N_GROUPS * SGU_CHUNK
_REST_PARTS = ("norm_g", "sgu_ln_g", "sgu_ln_b", "b_spatial", "final_norm_g")
REST_ROWS = SMALL_ROWS - W_SPATIAL_ROWS
_LOSS_ROW = W_SPATIAL_ROWS + 8 * len(_REST_PARTS)


def _pack_rest(parts, loss_tile=None):
    rows = []
    for name in _REST_PARTS:
        a = parts[name].reshape(-1, LANES).astype(F32)
        rows.append(jnp.pad(a, ((0, 8 - a.shape[0]), (0, 0))))
    rows.append(jnp.zeros((8, LANES), F32) if loss_tile is None else loss_tile)
    rows.append(jnp.zeros((REST_ROWS - 8 * len(rows), LANES), F32))
    return jnp.concatenate(rows, axis=0)


def _pack_small(parts, loss_tile):
    return jnp.concatenate([parts["w_spatial"].reshape(W_SPATIAL_ROWS, LANES), _pack_rest(parts, loss_tile)], axis=0)


def _unpack_rest(packed, shapes):
    out = {}
    for k, name in enumerate(_REST_PARTS):
        n = math.prod(shapes[name])
        out[name] = packed[8 * k:8 * k + n // LANES].reshape(shapes[name])
    return out


def _local_step(proj, ht, x, target, norm_g, w_in, sgu_ln_g, sgu_ln_b, w_spatial, b_spatial, w_up_a, w_up_b, w_out,
                final_norm_g, bq, bk):
    pos = jnp.arange(SGU_CHUNK)
    keep = (pos[None, :] // SGU_SUBCHUNK) <= (pos[:, None] // SGU_SUBCHUNK)
    w_mask = jnp.where(keep[None], w_spatial, 0.0).astype(BF16)
    w_mask_t = jnp.swapaxes(w_mask, 1, 2)
    bias_full = jnp.repeat(b_spatial.T, GROUP_DIM, axis=1)
    ln_g = sgu_ln_g.reshape(1, D_BRANCH)
    ln_b = sgu_ln_b.reshape(1, D_BRANCH)
    final_g = final_norm_g.reshape(1, D_MODEL)

    o, ya, rsave = _attn_fwd(proj, bq, bk, 2 * ATTN_PAIRS)
    yb = _sgu_fwd(proj, ln_g, ln_b, w_mask, bias_full)
    dzg, do, dyb, dx2, g_out, g_up_a, g_up_b, loss_acc, d_final = _mid(
        proj, ya, yb, o, x, target, final_g, w_up_a, w_up_b, w_out)
    dsgu, d_wsp, d_bsp, d_lng, d_lnb = _sgu_bwd(proj, dyb, ln_g, ln_b, w_mask, w_mask_t, bias_full)
    g_in = _dwin_early(ht, dzg, lambda j: jnp.where(j == 0, COL_ZA, COL_GA - 1 + j), dsgu, lambda j: COL_UB + j)
    dq, dk, dv, *early_slots = _attn_bwd(proj, do, rsave, bq, bk, ATTN_PAIRS, (g_in, g_up_a, g_up_b, g_out))
    g_in = _dwin_pieces(ht, (dq, dk, dv), COL_Q, g_in)
    pieces = [(dq, COL_Q * D_BRANCH, 0, D_BRANCH), (dk, COL_K * D_BRANCH, 0, D_BRANCH),
              (dv, COL_V * D_BRANCH, 0, D_BRANCH), (dzg, COL_ZA * D_BRANCH, 0, D_BRANCH),
              (dsgu, COL_UB * D_BRANCH, 0, 3 * D_BRANCH), (dzg, COL_GA * D_BRANCH, D_BRANCH, 2 * D_MODEL)]
    dx, d_norm, late_in_slots = _dh_dx(pieces, w_in, x, norm_g, dx2, g_in)
    small = {"norm_g": d_norm, "sgu_ln_g": d_lng, "sgu_ln_b": d_lnb, "w_spatial": d_wsp,
             "b_spatial": d_bsp[:, :N_GROUPS].T, "final_norm_g": d_final}
    return loss_acc, dx, (g_in, g_up_a, g_up_b, g_out), small, early_slots, late_in_slots


def kernel(x, norm_g, w_in, sgu_ln_g, sgu_ln_b, w_spatial, b_spatial, w_up_a, w_up_b, w_out, final_norm_g, loss_target, m_norm_g, m_w_in, m_sgu_ln_g, m_sgu_ln_b, m_w_spatial, m_b_spatial, m_w_up_a, m_w_up_b, m_w_out, m_final_norm_g, v_norm_g, v_w_in, v_sgu_ln_g, v_sgu_ln_b, v_w_spatial, v_b_spatial, v_w_up_a, v_w_up_b, v_w_out, v_final_norm_g):
    big_names = ("w_in", "w_up_a", "w_up_b", "w_out")
    names = ("norm_g", "w_in", "sgu_ln_g", "sgu_ln_b", "w_spatial", "b_spatial", "w_up_a", "w_up_b", "w_out",
             "final_norm_g")
    w = dict(norm_g=norm_g, w_in=w_in, sgu_ln_g=sgu_ln_g, sgu_ln_b=sgu_ln_b, w_spatial=w_spatial,
             b_spatial=b_spatial, w_up_a=w_up_a, w_up_b=w_up_b, w_out=w_out, final_norm_g=final_norm_g)
    m = dict(norm_g=m_norm_g, w_in=m_w_in, sgu_ln_g=m_sgu_ln_g, sgu_ln_b=m_sgu_ln_b, w_spatial=m_w_spatial,
             b_spatial=m_b_spatial, w_up_a=m_w_up_a, w_up_b=m_w_up_b, w_out=m_w_out, final_norm_g=m_final_norm_g)
    v = dict(norm_g=v_norm_g, w_in=v_w_in, sgu_ln_g=v_sgu_ln_g, sgu_ln_b=v_sgu_ln_b, w_spatial=v_w_spatial,
             b_spatial=v_b_spatial, w_up_a=v_w_up_a, w_up_b=v_w_up_b, w_out=v_w_out, final_norm_g=v_final_norm_g)
    shapes = {n: w[n].shape for n in names}
    flat2d = lambda a: a.reshape(a.shape[-2:])

    proj, ht, *full = _in_proj_gather(x[0], norm_g, *[flat2d(w[n]) for n in big_names])
    loss, dx, big_grads, small, early_slots, late_in_slots = _local_step(
        proj, ht, x[0], loss_target[0], norm_g, full[0], sgu_ln_g[0], sgu_ln_b[0], w_spatial[0], b_spatial[0],
        full[1], full[2], full[3], final_norm_g, ATTN_Q_BLOCK, ATTN_K_BLOCK)
    packed = _pack_small(small, loss).reshape(N_DEV, SMALL_PIECE, LANES)
    red = _reduce_grads_tail(big_grads, packed, early_slots, late_in_slots)

    grads, deltas, new_m, new_v = {}, {}, {}, {}
    g_small = red[4].reshape(SMALL_ROWS, LANES)
    rows2d = lambda a: a.reshape(W_SPATIAL_ROWS, LANES)
    sets = [(flat2d(w[n]), g, flat2d(m[n]), flat2d(v[n]), 0) for n, g in zip(big_names, red[:4])]
    sets.append((rows2d(w_spatial), g_small, rows2d(m_w_spatial), rows2d(v_w_spatial), 0))
    sets.append((_pack_rest(w), g_small, _pack_rest(m), _pack_rest(v), W_SPATIAL_ROWS * ADAM_GRID // REST_ROWS))
    updated = _adamw(sets)
    for n, (g, d, nm, nv) in zip(big_names + ("w_spatial",), updated[:5]):
        grads[n], deltas[n], new_m[n], new_v[n] = (a.reshape(shapes[n]) for a in (g, d, nm, nv))
    g_rest, d, nm, nv = updated[5]
    for src, dst in ((g_rest, grads), (d, deltas), (nm, new_m), (nv, new_v)):
        dst.update(_unpack_rest(src, shapes))

    return (g_rest[_LOSS_ROW - W_SPATIAL_ROWS, 0], dx[None], *[grads[n] for n in names], *[deltas[n] for n in names],
            *[new_m[n] for n in names], *[new_v[n] for n in names])
```

```python
import math

import jax
import jax.numpy as jnp
from jax import lax
from jax.experimental import pallas as pl
from jax.experimental.pallas import tpu as pltpu

F32 = jnp.float32
BF16 = jnp.bfloat16

D_MODEL = 1024
N_HEADS = 8
HEAD_DIM = 64
D_BRANCH = 512
D_IN = 4 * D_BRANCH + 3 * D_BRANCH + 2 * D_MODEL
N_GROUPS = 8
GROUP_DIM = 64
SGU_CHUNK = 128
SGU_SUBCHUNK = 64
GROUP_SHIFT = 6
EPS = 1e-6
LANES = 128
ATTN_Q_BLOCK = 256
ATTN_K_BLOCK = 256
DEAD = -110.0
SKIPPED = -1e30
SCAN_PASSES = 1
ATTN_PAIRS = 2
N_CHIPS = 4
N_DEV = 8
MESH = pl.DeviceIdType.MESH

ADAM_LR = 0.001
ADAM_B1 = 0.9
ADAM_B2 = 0.999
ADAM_EPS = 1e-08
ADAM_WD = 0.01
ADAM_STEP = 10

COL_Q, COL_K, COL_V, COL_ZA, COL_UB, COL_VB, COL_ZB, COL_GA, COL_GB = 0, 1, 2, 3, 4, 5, 6, 7, 9

VMEM_LIMIT = 56 * 1024 * 1024

SMALL_ROWS = 1088
SMALL_PIECE = SMALL_ROWS // N_DEV


def _cparams(sem=None):
    return pltpu.CompilerParams(dimension_semantics=sem, vmem_limit_bytes=VMEM_LIMIT)


def _aligned(v, m):
    return v if isinstance(v, int) else pl.multiple_of(v, m)


def _sigmoid(x):
    return 1.0 / (1.0 + jnp.exp(-x))


def _gelu_and_grad(x):
    k = math.sqrt(2.0 / math.pi)
    x2 = x * x
    inner = k * (x + 0.044715 * x * x2)
    th = jnp.tanh(inner)
    g = 0.5 * x * (1.0 + th)
    dg = 0.5 * (1.0 + th) + 0.5 * x * (1.0 - th * th) * (k * (1.0 + 3.0 * 0.044715 * x2))
    return g, dg


def _split_dot(a, b_bf16, passes):
    out = None
    rem = a
    for _ in range(passes):
        part = rem.astype(BF16)
        d = jnp.dot(part, b_bf16, preferred_element_type=F32)
        out = d if out is None else out + d
        rem = rem - part.astype(F32)
    return out


def _dot_nt(a, b):
    return lax.dot_general(a, b, (((1,), (1,)), ((), ())), preferred_element_type=F32)


def _dot_tn(a, b):
    return lax.dot_general(a, b, (((0,), (0,)), ((), ())), preferred_element_type=F32)


def _place():
    x, y, c = lax.axis_index("x"), lax.axis_index("y"), lax.axis_index("c")
    return x, y, c


def _in_proj_gather(x, norm_g, w_in, w_up_a, w_up_b, w_out):
    s = x.shape[0]
    tm = min(1024, s)
    nt = s // tm
    shards = (w_in, w_up_a, w_up_b, w_out)
    n_arr = len(shards)
    col_sharded = (True, True, True, False)
    full_shapes = ((D_MODEL, D_IN), (D_BRANCH, D_MODEL), (D_BRANCH, D_MODEL), (D_MODEL, D_MODEL))
    w_shard = w_in.shape[1]
    half_rows = D_MODEL // 2
    stage_rows = 256

    def body(order_ref, x_ref, g_ref, *refs):
        src = refs[:n_arr]
        proj_ref, ht_ref = refs[n_arr:n_arr + 2]
        out = refs[n_arr + 2:2 * n_arr + 2]
        wsc, h_scr, stage = refs[2 * n_arr + 2:2 * n_arr + 5]
        small_stage = refs[2 * n_arr + 5:2 * n_arr + 8]
        small_cast = refs[2 * n_arr + 8:2 * n_arr + 11]
        send_sems, recv_sems, local_sems = refs[2 * n_arr + 11:]
        k = pl.program_id(0)
        i = pl.program_id(1)
        x_, y_, c = _place()
        chip = 2 * x_ + y_
        sibling = (x_, y_, 1 - c)
        others = [(x_, 1 - y_), (1 - x_, y_), (1 - x_, 1 - y_)]

        def region(a, chip_idx, half):
            if a == 0:
                return wsc.at[chip_idx, pl.ds(_aligned(half * half_rows, 16), half_rows), :]
            r, w = shards[a].shape
            hr = r // 2
            if col_sharded[a]:
                return out[a].at[pl.ds(_aligned(half * hr, 16), hr), pl.ds(_aligned(chip_idx * w, LANES), w)]
            return out[a].at[pl.ds(_aligned(chip_idx * r + half * hr, 16), hr), :]

        def remote(kk, a, chip_idx, half, to, own):
            s_ref = region(a, chip_idx, half)
            if own and a > 0:
                hr = shards[a].shape[0] // 2
                s_ref = small_cast[a - 1].at[pl.ds(_aligned(half * hr, 16), hr), :]
            return pltpu.make_async_remote_copy(src_ref=s_ref, dst_ref=region(a, chip_idx, half),
                                                send_sem=send_sems.at[kk], recv_sem=recv_sems.at[kk],
                                                device_id=to, device_id_type=MESH)

        def keep_whole(kk, chip_idx):
            return pltpu.make_async_copy(wsc.at[chip_idx],
                                         out[0].at[:, pl.ds(_aligned(chip_idx * w_shard, LANES), w_shard)],
                                         local_sems.at[kk])

        def small_stores():
            cps = []
            for a in range(1, n_arr):
                hr = shards[a].shape[0] // 2
                for half in range(2):
                    cps.append(pltpu.make_async_copy(small_cast[a - 1].at[pl.ds(half * hr, hr), :],
                                                     region(a, chip, half), local_sems.at[4 + 2 * (a - 1) + half]))
            return cps

        def arrive_and_pass(j):
            ochip = chip ^ j
            for a in range(n_arr):
                kk = n_arr * (j - 1) + a
                remote(kk, a, ochip, c, sibling, False).wait_recv()
                remote(3 * n_arr + kk, a, ochip, c, sibling, False).start()

        def from_sibling(j, a):
            remote(3 * n_arr + n_arr * (j - 1) + a, a, chip ^ j, 1 - c, sibling, False).wait_recv()

        @pl.when((k == 0) & (i == 0))
        def _():
            def cast_rows(half):
                for t in range(half_rows // stage_rows):
                    r0 = pl.multiple_of(half * half_rows + t * stage_rows, stage_rows)
                    pltpu.sync_copy(src[0].at[pl.ds(r0, stage_rows), :], stage)
                    wsc[chip, pl.ds(r0, stage_rows), :] = stage[...].astype(BF16)

            cast_rows(c)
            for j in (1, 2):
                remote(n_arr * (j - 1), 0, chip, c, (*others[j - 1], c), True).start()
            cast_rows(1 - c)
            for a in range(1, n_arr):
                pltpu.sync_copy(src[a], small_stage[a - 1])
                small_cast[a - 1][...] = small_stage[a - 1][...].astype(BF16)
            for j in (1, 2):
                for a in range(1, n_arr):
                    remote(n_arr * (j - 1) + a, a, chip, c, (*others[j - 1], c), True).start()
            keep_whole(0, chip).start()
            for cp in small_stores():
                cp.start()

        @pl.when((k == 1) & (i == 0))
        def _():
            for j in (1, 2):
                remote(n_arr * (j - 1), 0, chip, c, (*others[j - 1], c), True).wait_send()
            for a in range(n_arr):
                remote(n_arr * 2 + a, a, chip, c, (*others[2], c), True).start()
            arrive_and_pass(1)
            from_sibling(1, 0)
            keep_whole(1, chip ^ 1).start()

        @pl.when((k == 1) & (i == min(1, nt - 1)))
        def _():
            arrive_and_pass(2)

        @pl.when((k == 2) & (i == 0))
        def _():
            from_sibling(2, 0)
            keep_whole(2, chip ^ 2).start()
            arrive_and_pass(3)

        @pl.when((k == 3) & (i == 0))
        def _():
            from_sibling(3, 0)
            keep_whole(3, chip ^ 3).start()

        @pl.when(k == 0)
        def _():
            xf = x_ref[...]
            r = lax.rsqrt(jnp.mean(xf * xf, axis=-1, keepdims=True) + EPS)
            h = xf * r * g_ref[...]
            h_scr[i] = h.astype(BF16)
            ht_ref[...] = h.T.astype(BF16)

        proj_ref[...] = jnp.dot(h_scr[i], wsc[order_ref[k]], preferred_element_type=F32).astype(BF16)

        @pl.when((k == 3) & (i == nt - 1))
        def _():
            for j in (1, 2, 3):
                for a in range(1, n_arr):
                    from_sibling(j, a)
            for j in (1, 2, 3):
                for a in range(n_arr):
                    kk = n_arr * (j - 1) + a
                    if a > 0 or j == 3:
                        remote(kk, a, chip, c, (*others[j - 1], c), True).wait_send()
                    remote(3 * n_arr + kk, a, chip ^ j, c, sibling, False).wait_send()
            for kk in range(4):
                keep_whole(kk, chip ^ kk).wait()
            for cp in small_stores():
                cp.wait()

    any_spec = pl.BlockSpec(memory_space=pl.ANY)
    tile = lambda kk, ii: jnp.where(kk == 0, ii, nt - 1)
    grid_spec = pltpu.PrefetchScalarGridSpec(
        num_scalar_prefetch=1, grid=(N_CHIPS, nt),
        in_specs=[pl.BlockSpec((tm, D_MODEL), lambda kk, ii, order: (tile(kk, ii), 0)),
                  pl.BlockSpec((1, D_MODEL), lambda kk, ii, order: (0, 0))] + [any_spec] * n_arr,
        out_specs=[pl.BlockSpec((tm, w_shard), lambda kk, ii, order: (ii, order[kk])),
                   pl.BlockSpec((D_MODEL, tm), lambda kk, ii, order: (0, tile(kk, ii)))] + [any_spec] * n_arr,
        scratch_shapes=[pltpu.VMEM((N_CHIPS, D_MODEL, w_shard), BF16), pltpu.VMEM((nt, tm, D_MODEL), BF16),
                        pltpu.VMEM((stage_rows, w_shard), F32)]
        + [pltpu.VMEM(a.shape, F32) for a in shards[1:]] + [pltpu.VMEM(a.shape, BF16) for a in shards[1:]]
        + [pltpu.SemaphoreType.DMA((6 * n_arr,)), pltpu.SemaphoreType.DMA((6 * n_arr,)),
           pltpu.SemaphoreType.DMA((4 + 2 * (n_arr - 1),))])
    x_, y_, _ = _place()
    order = (2 * x_ + y_) ^ jnp.arange(N_CHIPS, dtype=jnp.int32)
    return pl.pallas_call(
        body, name="in_proj_gather", grid_spec=grid_spec,
        out_shape=[jax.ShapeDtypeStruct((s, D_IN), BF16), jax.ShapeDtypeStruct((D_MODEL, s), BF16)]
        + [jax.ShapeDtypeStruct(sh, BF16) for sh in full_shapes],
        compiler_params=pltpu.CompilerParams(dimension_semantics=("arbitrary", "arbitrary"),
                                             vmem_limit_bytes=VMEM_LIMIT, has_side_effects=True),
    )(order, x, norm_g, *shards)


def _neg_softplus_parts(z):
    zb = z.astype(BF16)
    p = jnp.exp(-jnp.abs(zb))
    return p, jnp.maximum(zb, jnp.zeros_like(zb)) + jnp.log(1.0 + p)


def _split_cat(a, passes):
    parts = []
    rem = a
    for k in range(passes):
        part = rem.astype(BF16)
        parts.append(part)
        if k + 1 < passes:
            rem = rem - part.astype(F32)
    return parts[0] if passes == 1 else jnp.concatenate(parts, axis=1)


def _tri(blk, upper, sign):
    row = lax.broadcasted_iota(jnp.int32, (blk, blk), 0)
    col = lax.broadcasted_iota(jnp.int32, (blk, blk), 1)
    keep = (row <= col) if upper else (row >= col)
    t = jnp.where(keep, sign, 0.0).astype(BF16)
    return t if SCAN_PASSES == 1 else jnp.concatenate([t] * SCAN_PASSES, axis=0)


def _attn_fwd(proj, bq, bk, npairs):
    s = proj.shape[0]
    nq = s // bq
    ratio = bq // bk
    scale = HEAD_DIM ** -0.5
    heads = tuple(range(2 * npairs))
    width = LANES * npairs

    def body(q_ref, k_ref, v_ref, za_ref, o_ref, ya_ref, rs_ref, acc_ref, r_ref):
        i = pl.program_id(1)
        lane = lax.broadcasted_iota(jnp.int32, (bq, LANES), 1)
        lo_half = lane < HEAD_DIM
        qm = []
        for pr in range(npairs):
            q = q_ref[:, LANES * pr:LANES * (pr + 1)] * jnp.asarray(scale, BF16)
            zero = jnp.zeros_like(q)
            qm += [jnp.where(lo_half, q, zero), jnp.where(lo_half, zero, q)]
        row = lax.broadcasted_iota(jnp.int32, (bq, bk), 0)
        col = lax.broadcasted_iota(jnp.int32, (bq, bk), 1)
        tneg = _tri(bk, False, -1.0)
        acc_ref[...] = jnp.zeros_like(acc_ref)
        r_ref[...] = jnp.zeros_like(r_ref)
        rs_ref[...] = jnp.full_like(rs_ref, SKIPPED)

        def scores(j):
            ks = pl.multiple_of(j * bk, bk)
            return [_dot_nt(qm[h], k_ref[pl.ds(ks, bk), LANES * (h // 2):LANES * (h // 2 + 1)]) for h in heads]

        def block(j, diag, valid=None):
            ks = pl.multiple_of(j * bk, bk)
            vj = [v_ref[pl.ds(ks, bk), LANES * pr:LANES * (pr + 1)] for pr in range(npairs)]
            if diag:
                before = (j * bk + col) < (i * bq + row)
            z = scores(j)
            sp = [_neg_softplus_parts(z[h])[1] for h in heads]
            if diag:
                sp = [jnp.where(before, sp[h], 0.0) for h in heads]
            cin = [jnp.dot(_split_cat(sp[h], SCAN_PASSES), tneg, preferred_element_type=F32) for h in heads]
            w = [jnp.exp(z[h] + cin[h]) for h in heads]
            if diag:
                w = [jnp.where(before, w[h], 0.0) for h in heads]
            pv = [jnp.dot(w[h].astype(BF16), vj[h // 2], preferred_element_type=F32) for h in heads]
            r = [r_ref[h] for h in heads]
            keep = 1.0 if valid is None else valid.astype(F32)
            for h in heads:
                acc_ref[h] += pv[h] * (jnp.exp(r[h]) * keep)
                r_ref[h] = r[h] + cin[h][:, 0:1] * keep
            for pr in range(npairs):
                hit = [lane == j, lane == j + HEAD_DIM]
                if valid is not None:
                    hit = [m & valid for m in hit]
                rs_ref[pr] = jnp.where(hit[0], r[2 * pr], jnp.where(hit[1], r[2 * pr + 1], rs_ref[pr]))

        for t in range(ratio):
            block(i * ratio + ratio - 1 - t, True)
        block(jnp.maximum(i * ratio - 1, 0), False, valid=i > 0)

        def alive(carry):
            jj, r_max = carry
            return (jj < i * ratio - 1) & (r_max > DEAD)

        def loop_body(carry):
            jj, _ = carry
            block(i * ratio - 2 - jj, False)
            return jj + 1, jnp.max(r_ref[...])

        lax.while_loop(alive, loop_body, (0, jnp.max(r_ref[...])))
        for pr in range(npairs):
            cols = slice(LANES * pr, LANES * (pr + 1))
            o = jnp.where(lo_half, acc_ref[2 * pr], acc_ref[2 * pr + 1])
            o_ref[:, cols] = o.astype(BF16)
            za = za_ref[:, cols].astype(F32)
            ya_ref[:, cols] = (o * (za * _sigmoid(za))).astype(BF16)

    n_steps = N_HEADS // (2 * npairs)
    return pl.pallas_call(
        body, name="attn_fwd", grid=(n_steps, nq),
        in_specs=[pl.BlockSpec((bq, width), lambda p, i: (i, n_steps * COL_Q + p)),
                  pl.BlockSpec((s, width), lambda p, i: (0, n_steps * COL_K + p)),
                  pl.BlockSpec((s, width), lambda p, i: (0, n_steps * COL_V + p)),
                  pl.BlockSpec((bq, width), lambda p, i: (i, n_steps * COL_ZA + p))],
        out_specs=[pl.BlockSpec((bq, width), lambda p, i: (i, p)),
                   pl.BlockSpec((bq, width), lambda p, i: (i, p)),
                   pl.BlockSpec((npairs, bq, LANES), lambda p, i: (p, i, 0))],
        out_shape=[jax.ShapeDtypeStruct((s, D_BRANCH), BF16), jax.ShapeDtypeStruct((s, D_BRANCH), BF16),
                   jax.ShapeDtypeStruct((N_HEADS // 2, s, LANES), F32)],
        scratch_shapes=[pltpu.VMEM((2 * npairs, bq, LANES), F32), pltpu.VMEM((2 * npairs, bq, 1), F32)],
        compiler_params=_cparams(("parallel", "parallel")),
    )(proj, proj, proj, proj)


_GRAD_COL_SHARDED = (True, True, True, False)
_GRAD_FULL_SHAPES = ((D_MODEL, D_IN), (D_BRANCH, D_MODEL), (D_BRANCH, D_MODEL), (D_MODEL, D_MODEL))
_GRAD_PIECE_SHAPES = tuple((r // 2, w // N_CHIPS) if cs else (r // (2 * N_CHIPS), w)
                           for (r, w), cs in zip(_GRAD_FULL_SHAPES, _GRAD_COL_SHARDED))
_EARLY_IN_DEVS = (4, 5, 6, 7)
_LATE_IN_DEVS = (0, 1, 2, 3)
_LATE_CHIPS = (0, 1)
_EARLY_CHIPS = (2, 3)
_ALL_CHIPS = (0, 1, 2, 3)


def _grad_piece(ref, a, dev):
    r, w = _GRAD_PIECE_SHAPES[a]
    if _GRAD_COL_SHARDED[a]:
        return ref.at[pl.ds((dev % 2) * r, r), pl.ds((dev // 2) * w, w)]
    return ref.at[pl.ds(dev * r, r), :]


def _dev_id(dev):
    return (dev // 4, (dev // 2) % 2, dev % 2)


def _me():
    return 4 * lax.axis_index("x") + 2 * lax.axis_index("y") + lax.axis_index("c")


def _presum_copy(src, dst, send_sem, recv_sem, to_dev):
    return pltpu.make_async_remote_copy(src_ref=src, dst_ref=dst, send_sem=send_sem, recv_sem=recv_sem,
                                        device_id=_dev_id(to_dev), device_id_type=MESH)


def _presum_hand_off(dev, a, dest_chips, g_ref, slots, pair, send_sems, recv_sems):
    chip, core = dev // 2, dev % 2
    cps = []
    for k, q in enumerate(dest_chips):
        piece = _grad_piece(g_ref, a, 2 * q + 1 - core)
        if q == chip:
            cps.append(_presum_copy(piece, slots.at[dev], send_sems.at[N_DEV + k], recv_sems.at[dev], dev ^ 1))
        else:
            cps.append(_presum_copy(piece, pair.at[k], send_sems.at[N_DEV + k], recv_sems.at[N_DEV + k], dev ^ 1))
    return cps


def _presum_sends(dev, a, dest_chips, slots, sums, send_sems, recv_sems):
    chip, core = dev // 2, dev % 2
    return [_presum_copy(sums.at[k], slots.at[dev], send_sems.at[2 * q + core], recv_sems.at[dev], 2 * q + core)
            for k, q in enumerate(dest_chips) if q != chip]


def _presum_loads(dev, a, dest_chips, g_ref, stage, load_sems):
    chip, core = dev // 2, dev % 2
    return [pltpu.make_async_copy(_grad_piece(g_ref, a, 2 * q + core), stage.at[k], load_sems.at[k])
            for k, q in enumerate(dest_chips) if q != chip]


def _presum_send(dev, a, dest_chips, g_ref, slots, pair, stage, sums, send_sems, recv_sems, load_sems):
    chip, core = dev // 2, dev % 2
    hand = _presum_hand_off(dev, a, dest_chips, g_ref, slots, pair, send_sems, recv_sems)
    for cp in _presum_loads(dev, a, dest_chips, g_ref, stage, load_sems):
        cp.wait()
    for k, q in enumerate(dest_chips):
        if q != chip:
            hand[k].wait_recv()
            sums[k] = (stage[k].astype(F32) + pair[k].astype(F32)).astype(BF16)
    for cp in _presum_sends(dev, a, dest_chips, slots, sums, send_sems, recv_sems):
        cp.start()


def _presum_wait(dev, a, dest_chips, g_ref, slots, pair, sums, send_sems, recv_sems):
    chip, core = dev // 2, dev % 2
    for cp in _presum_hand_off(dev, a, dest_chips, g_ref, slots, pair, send_sems, recv_sems):
        cp.wait_send()
    for cp in _presum_sends(dev, a, dest_chips, slots, sums, send_sems, recv_sems):
        cp.wait_send()
    if chip in dest_chips:
        for src_dev in _presum_sources(dev):
            _presum_copy(sums.at[0], slots.at[src_dev], send_sems.at[src_dev], recv_sems.at[src_dev], src_dev).wait_recv()


def _presum_sources(dev):
    return [dev ^ 1] + [2 * r + dev % 2 for r in range(N_CHIPS) if r != dev // 2]


def _presum_scratch(a, dest_chips):
    n = len(dest_chips)
    piece = _GRAD_PIECE_SHAPES[a]
    return [pltpu.VMEM((n,) + piece, BF16), pltpu.VMEM((n,) + piece, BF16), pltpu.VMEM((n,) + piece, BF16),
            pltpu.SemaphoreType.DMA((N_DEV + n,)), pltpu.SemaphoreType.DMA((N_DEV + n,)),
            pltpu.SemaphoreType.DMA((n,))]


PRESUM_SCRATCH = 6


def _presum_program(first, second, last, a, dest_chips, g_ref, slots, scratch):
    pair, stage, sums, send_sems, recv_sems, load_sems = scratch
    me = _me()

    @pl.when(first)
    def _():
        for dev in range(N_DEV):
            @pl.when(me == dev)
            def _():
                for cp in _presum_hand_off(dev, a, dest_chips, g_ref, slots, pair, send_sems, recv_sems):
                    cp.start()
                for cp in _presum_loads(dev, a, dest_chips, g_ref, stage, load_sems):
                    cp.start()

    @pl.when(second)
    def _():
        for dev in range(N_DEV):
            @pl.when(me == dev)
            def _():
                _presum_send(dev, a, dest_chips, g_ref, slots, pair, stage, sums, send_sems, recv_sems, load_sems)

    def finish():
        for dev in range(N_DEV):
            @pl.when(me == dev)
            def _():
                _presum_wait(dev, a, dest_chips, g_ref, slots, pair, sums, send_sems, recv_sems)

    return finish


def _attn_bwd(proj, do, rsave, bq, bk, npairs, grads):
    plan = ((0, _EARLY_CHIPS), (1, _ALL_CHIPS), (2, _ALL_CHIPS), (3, _ALL_CHIPS))
    s = proj.shape[0]
    nq = s // bq
    ratio = bq // bk
    scale = HEAD_DIM ** -0.5
    heads = tuple(range(2 * npairs))
    width = LANES * npairs

    n_steps = N_HEADS // (2 * npairs)
    n_g = len(grads)

    def body(q_ref, k_ref, v_ref, do_ref, rs_ref, *refs):
        g_src = refs[:n_g]
        dq_ref, dk_ref, dv_ref = refs[n_g:n_g + 3]
        g_slots = refs[n_g + 3:2 * n_g + 3]
        dk_acc, dv_acc, dq_acc, e_ref = refs[2 * n_g + 3:2 * n_g + 7]
        i = pl.program_id(1)
        step = pl.program_id(0) * nq + i
        finish = [_presum_program(step == 0, step == 1, step == n_steps * nq - 1, a, chips, g_src[pos], g_slots[pos],
                                  refs[2 * n_g + 7 + PRESUM_SCRATCH * pos:2 * n_g + 7 + PRESUM_SCRATCH * (pos + 1)])
                  for pos, (a, chips) in enumerate(plan)]

        lane = lax.broadcasted_iota(jnp.int32, (bq, LANES), 1)
        lo_half = lane < HEAD_DIM
        qm, dom = [], []
        for pr in range(npairs):
            cols = slice(LANES * pr, LANES * (pr + 1))
            q = q_ref[:, cols] * jnp.asarray(scale, BF16)
            zero = jnp.zeros_like(q)
            qm += [jnp.where(lo_half, q, zero), jnp.where(lo_half, zero, q)]
            dout = do_ref[:, cols].astype(F32)
            dom += [jnp.where(lo_half, dout, 0.0), jnp.where(lo_half, 0.0, dout)]
        row = lax.broadcasted_iota(jnp.int32, (bq, bk), 0)
        col = lax.broadcasted_iota(jnp.int32, (bq, bk), 1)
        tneg = _tri(bk, False, -1.0)
        tfwd = _tri(bk, True, 1.0)

        @pl.when(i == 0)
        def _():
            dk_acc[...] = jnp.zeros_like(dk_acc)
            dv_acc[...] = jnp.zeros_like(dv_acc)

        dq_acc[...] = jnp.zeros_like(dq_acc)
        e_ref[...] = jnp.zeros_like(e_ref)

        def block(j, diag, valid=None):
            ks = pl.multiple_of(j * bk, bk)
            kj = [k_ref[pl.ds(ks, bk), LANES * pr:LANES * (pr + 1)] for pr in range(npairs)]
            vj = [v_ref[pl.ds(ks, bk), LANES * pr:LANES * (pr + 1)] for pr in range(npairs)]
            if diag:
                before = (j * bk + col) < (i * bq + row)
            z = [_dot_nt(qm[h], kj[h // 2]) for h in heads]
            er = [jnp.exp(jnp.sum(jnp.where(lane == j + HEAD_DIM * (h % 2), rs_ref[h // 2], 0.0), axis=-1,
                                  keepdims=True)) for h in heads]
            if valid is not None:
                er = [er[h] * valid.astype(F32) for h in heads]
            dos = [(dom[h] * er[h]).astype(BF16) for h in heads]
            dw = [_dot_nt(dos[h], vj[h // 2]) for h in heads]
            psp = [_neg_softplus_parts(z[h]) for h in heads]
            sp = [psp[h][1] for h in heads]
            if diag:
                sp = [jnp.where(before, sp[h], 0.0) for h in heads]
            cin = [jnp.dot(_split_cat(sp[h], SCAN_PASSES), tneg, preferred_element_type=F32) for h in heads]
            w = [jnp.exp(z[h] + cin[h]) for h in heads]
            if diag:
                w = [jnp.where(before, w[h], 0.0) for h in heads]
            e =[dw[h] * w[h] for h in heads]
            eincl = [jnp.dot(_split_cat(e[h], SCAN_PASSES), tfwd, preferred_element_type=F32) + e_ref[h]
                     for h in heads]
            dz = []
            for h in heads:
                p = psp[h][0]
                beta = jnp.where(z[h] >= 0.0, 1.0, p) / (1.0 + p)
                d = e[h] - beta * eincl[h]
                dz.append((jnp.where(before, d, 0.0) if diag else d).astype(BF16))
            wb = [w[h].astype(BF16) for h in heads]
            for h in heads:
                e_ref[h] = eincl[h][:, bk - 1:bk]
                dq_acc[h] += jnp.dot(dz[h], kj[h // 2], preferred_element_type=F32)
            for pr in range(npairs):
                cols = slice(LANES * pr, LANES * (pr + 1))
                h0, h1 = 2 * pr, 2 * pr + 1
                dk_acc[pl.ds(ks, bk), cols] += _dot_tn(dz[h0], qm[h0]) + _dot_tn(dz[h1], qm[h1])
                dv_acc[pl.ds(ks, bk), cols] += _dot_tn(wb[h0], dos[h0]) + _dot_tn(wb[h1], dos[h1])

        def loop_body(j, carry):
            block(j, False)
            return carry

        block_of_lane = lane & (HEAD_DIM - 1)
        live = jnp.max(rs_ref[...], axis=0) > DEAD
        first_live = jnp.min(jnp.where(live, block_of_lane, nq * ratio))
        last = jnp.maximum(i * ratio - 1, 0)
        lax.fori_loop(jnp.minimum(first_live, last), last, loop_body, 0)
        block(last, False, valid=i > 0)
        for t in range(ratio):
            block(i * ratio + t, True)
        for pr in range(npairs):
            dq = jnp.where(lo_half, dq_acc[2 * pr], dq_acc[2 * pr + 1]) * scale
            dq_ref[:, LANES * pr:LANES * (pr + 1)] = dq.astype(BF16)

        @pl.when(i == nq - 1)
        def _():
            dk_ref[...] = dk_acc[...].astype(BF16)
            dv_ref[...] = dv_acc[...].astype(BF16)

        @pl.when(step == n_steps * nq - 1)
        def _():
            for fin in finish:
                fin()

    any_spec = pl.BlockSpec(memory_space=pl.ANY)
    return pl.pallas_call(
        body, name="attn_bwd", grid=(n_steps, nq),
        in_specs=[pl.BlockSpec((bq, width), lambda p, i: (i, n_steps * COL_Q + p)),
                  pl.BlockSpec((s, width), lambda p, i: (0, n_steps * COL_K + p)),
                  pl.BlockSpec((s, width), lambda p, i: (0, n_steps * COL_V + p)),
                  pl.BlockSpec((bq, width), lambda p, i: (i, p)),
                  pl.BlockSpec((npairs, bq, LANES), lambda p, i: (p, i, 0))] + [any_spec] * n_g,
        out_specs=[pl.BlockSpec((bq, width), lambda p, i: (i, p)),
                   pl.BlockSpec((s, width), lambda p, i: (0, p)),
                   pl.BlockSpec((s, width), lambda p, i: (0, p))] + [any_spec] * n_g,
        out_shape=[jax.ShapeDtypeStruct((s, D_BRANCH), BF16)] * 3
        + [jax.ShapeDtypeStruct((N_DEV,) + _GRAD_PIECE_SHAPES[a], BF16) for a, _ in plan],
        scratch_shapes=[pltpu.VMEM((s, width), F32), pltpu.VMEM((s, width), F32),
                        pltpu.VMEM((2 * npairs, bq, LANES), F32), pltpu.VMEM((2 * npairs, bq, 1), F32)]
        + [sh for a, chips in plan for sh in _presum_scratch(a, chips)],
        compiler_params=pltpu.CompilerParams(dimension_semantics=("arbitrary", "arbitrary"),
                                             vmem_limit_bytes=VMEM_LIMIT, has_side_effects=True),
    )(proj, proj, proj, do, rsave, *grads)


def _group_avg_matrix():
    a = lax.broadcasted_iota(jnp.int32, (LANES, LANES), 0) >> GROUP_SHIFT
    b = lax.broadcasted_iota(jnp.int32, (LANES, LANES), 1) >> GROUP_SHIFT
    return jnp.where(a == b, 1.0 / GROUP_DIM, 0.0).astype(BF16)


def _group_mean(a, avg):
    parts = [_split_dot(a[:, LANES * k:LANES * (k + 1)], avg, 2) for k in range(D_BRANCH // LANES)]
    return jnp.concatenate(parts, axis=1)


def _sgu_forward_parts(ub, vb, ln_g, ln_b, avg):
    ug, dug = _gelu_and_grad(ub)
    vg, dvg = _gelu_and_grad(vb)
    mu = _group_mean(vg, avg)
    d = vg - mu
    var = _group_mean(d * d, avg)
    rstd = lax.rsqrt(var + EPS)
    vhat = d * rstd
    vn = vhat * ln_g + ln_b
    return ug, dug, dvg, rstd, vhat, vn


def _sgu_mix(w_ref, src_bf16, n_chunks):
    lane = lax.broadcasted_iota(jnp.int32, (SGU_CHUNK, LANES), 1)
    lo_half = lane < GROUP_DIM
    rows = []
    for n in range(n_chunks):
        slabs = []
        for a in range(D_BRANCH // LANES):
            blk = src_bf16[SGU_CHUNK * n:SGU_CHUNK * (n + 1), LANES * a:LANES * (a + 1)]
            zero = jnp.zeros_like(blk)
            m0 = jnp.dot(w_ref[2 * a], jnp.where(lo_half, blk, zero), preferred_element_type=F32)
            m1 = jnp.dot(w_ref[2 * a + 1], jnp.where(lo_half, zero, blk), preferred_element_type=F32)
            slabs.append(m0 + m1)
        rows.append(jnp.concatenate(slabs, axis=1))
    return jnp.concatenate(rows, axis=0)


def _sgu_fwd(proj, ln_g, ln_b, w_mask, bias_full):
    s = proj.shape[0]
    tm = min(512, s)
    n_chunks = tm // SGU_CHUNK

    def body(ub_ref, vb_ref, zb_ref, g_ref, b_ref, w_ref, bias_ref, yb_ref):
        avg = _group_avg_matrix()
        ug, _, _, _, _, vn = _sgu_forward_parts(ub_ref[...].astype(F32), vb_ref[...].astype(F32),
                                                g_ref[...], b_ref[...], avg)
        mixed = _sgu_mix(w_ref, vn.astype(BF16), n_chunks) + jnp.concatenate([bias_ref[...]] * n_chunks, axis=0)
        zb = zb_ref[...].astype(F32)
        yb_ref[...] = (ug * mixed * (zb * _sigmoid(zb))).astype(BF16)

    col = lambda c: pl.BlockSpec((tm, D_BRANCH), lambda i: (i, c))
    full = lambda shape: pl.BlockSpec(shape, lambda i: (0,) * len(shape))
    return pl.pallas_call(
        body, name="sgu_fwd", grid=(s // tm,),
        in_specs=[col(COL_UB), col(COL_VB), col(COL_ZB), full((1, D_BRANCH)), full((1, D_BRANCH)),
                  full((N_GROUPS, SGU_CHUNK, SGU_CHUNK)), full((SGU_CHUNK, D_BRANCH))],
        out_specs=pl.BlockSpec((tm, D_BRANCH), lambda i: (i, 0)),
        out_shape=jax.ShapeDtypeStruct((s, D_BRANCH), BF16),
        compiler_params=_cparams(("parallel",)),
    )(proj, proj, proj, ln_g, ln_b, w_mask, bias_full)


def _sgu_bwd(proj, dyb, ln_g, ln_b, w_mask, w_mask_t, bias_full):
    s = proj.shape[0]
    tm = min(512, s)
    n_chunks = tm // SGU_CHUNK
    n_steps = s // tm

    def body(ub_ref, vb_ref, zb_ref, dyb_ref, g_ref, b_ref, w_ref, wt_ref, bias_ref,
             dsgu_ref, dw_ref, db_ref, dg_ref, dbeta_ref, dmix_acc):
        i = pl.program_id(0)

        @pl.when(i == 0)
        def _():
            dw_ref[...] = jnp.zeros_like(dw_ref)
            dg_ref[...] = jnp.zeros_like(dg_ref)
            dbeta_ref[...] = jnp.zeros_like(dbeta_ref)
            dmix_acc[...] = jnp.zeros_like(dmix_acc)

        avg = _group_avg_matrix()
        ln_gv = g_ref[...]
        ug, dug, dvg, rstd, vhat, vn = _sgu_forward_parts(ub_ref[...].astype(F32), vb_ref[...].astype(F32),
                                                          ln_gv, b_ref[...], avg)
        vnb = vn.astype(BF16)
        mixed = _sgu_mix(w_ref, vnb, n_chunks) + jnp.concatenate([bias_ref[...]] * n_chunks, axis=0)
        zb = zb_ref[...].astype(F32)
        sg = _sigmoid(zb)
        sz = zb * sg
        dsz = sg * (1.0 + zb * (1.0 - sg))
        dy = dyb_ref[...].astype(F32)
        dmixed = dy * ug * sz
        du = dy * mixed * sz * dug
        dzb = dy * ug * mixed * dsz
        dmb = dmixed.astype(BF16)
        dvn = _sgu_mix(wt_ref, dmb, n_chunks)

        lane = lax.broadcasted_iota(jnp.int32, (SGU_CHUNK, LANES), 1)
        lo_half = lane < GROUP_DIM
        dm_sum = None
        for n in range(n_chunks):
            rows = slice(SGU_CHUNK * n, SGU_CHUNK * (n + 1))
            dm_sum = dmixed[rows] if dm_sum is None else dm_sum + dmixed[rows]
            for a in range(D_BRANCH // LANES):
                cols = slice(LANES * a, LANES * (a + 1))
                dblk = dmb[rows, cols]
                vblk = vnb[rows, cols]
                zero = jnp.zeros_like(dblk)
                dw_ref[2 * a] += _dot_nt(jnp.where(lo_half, dblk, zero), vblk)
                dw_ref[2 * a + 1] += _dot_nt(jnp.where(lo_half, zero, dblk), vblk)
        dmix_acc[...] += dm_sum

        dg_ref[...] += jnp.sum(dvn * vhat, axis=0, keepdims=True)
        dbeta_ref[...] += jnp.sum(dvn, axis=0, keepdims=True)
        dvh = dvn * ln_gv
        m1 = _group_mean(dvh, avg)
        m2 = _group_mean(dvh * vhat, avg)
        dv = rstd * (dvh - m1 - vhat * m2) * dvg
        dsgu_ref[:, 0:D_BRANCH] = du.astype(BF16)
        dsgu_ref[:, D_BRANCH:2 * D_BRANCH] = dv.astype(BF16)
        dsgu_ref[:, 2 * D_BRANCH:3 * D_BRANCH] = dzb.astype(BF16)

        @pl.when(i == n_steps - 1)
        def _():
            pos = lax.broadcasted_iota(jnp.int32, (SGU_CHUNK, SGU_CHUNK), 0) >> GROUP_SHIFT
            src = lax.broadcasted_iota(jnp.int32, (SGU_CHUNK, SGU_CHUNK), 1) >> GROUP_SHIFT
            keep = src <= pos
            for g in range(N_GROUPS):
                dw_ref[g] = jnp.where(keep, dw_ref[g], 0.0)
            grp = lax.broadcasted_iota(jnp.int32, (D_BRANCH, LANES), 0) >> GROUP_SHIFT
            sel = (grp == lax.broadcasted_iota(jnp.int32, (D_BRANCH, LANES), 1)).astype(BF16)
            db_ref[...] = _split_dot(dmix_acc[...], sel, 3)

    col = lambda c: pl.BlockSpec((tm, D_BRANCH), lambda i: (i, c))
    full = lambda shape: pl.BlockSpec(shape, lambda i: (0,) * len(shape))
    return pl.pallas_call(
        body, name="sgu_bwd", grid=(n_steps,),
        in_specs=[col(COL_UB), col(COL_VB), col(COL_ZB), pl.BlockSpec((tm, D_BRANCH), lambda i: (i, 0)),
                  full((1, D_BRANCH)), full((1, D_BRANCH)),
                  full((N_GROUPS, SGU_CHUNK, SGU_CHUNK)), full((N_GROUPS, SGU_CHUNK, SGU_CHUNK)),
                  full((SGU_CHUNK, D_BRANCH))],
        out_specs=[pl.BlockSpec((tm, 3 * D_BRANCH), lambda i: (i, 0)),
                   full((N_GROUPS, SGU_CHUNK, SGU_CHUNK)), full((SGU_CHUNK, LANES)),
                   full((1, D_BRANCH)), full((1, D_BRANCH))],
        out_shape=[jax.ShapeDtypeStruct((s, 3 * D_BRANCH), BF16),
                   jax.ShapeDtypeStruct((N_GROUPS, SGU_CHUNK, SGU_CHUNK), F32),
                   jax.ShapeDtypeStruct((SGU_CHUNK, LANES), F32),
                   jax.ShapeDtypeStruct((1, D_BRANCH), F32), jax.ShapeDtypeStruct((1, D_BRANCH), F32)],
        scratch_shapes=[pltpu.VMEM((SGU_CHUNK, D_BRANCH), F32)],
        compiler_params=_cparams(("arbitrary",)),
    )(proj, proj, proj, dyb, ln_g, ln_b, w_mask, w_mask_t, bias_full)


def _mid(proj, ya, yb, o, x, target, final_g, w_up_a, w_up_b, w_out):
    s = x.shape[0]
    tm = min(256, s)
    n_steps = s // tm
    half = D_MODEL // 2

    def body(ya_ref, yb_ref, o_ref, za_ref, ga0_ref, ga1_ref, gb0_ref, gb1_ref, x_ref, t_ref, gf_ref,
             wa_ref, wb_ref, wo_ref,
             dzg_ref, do_ref, dyb_ref, dx2_ref, gwo_ref, gwa_ref, gwb_ref, loss_ref, dgf_ref,
             acc_o, acc_a, acc_b):
        i = pl.program_id(0)

        @pl.when(i == 0)
        def _():
            acc_o[...] = jnp.zeros_like(acc_o)
            acc_a[...] = jnp.zeros_like(acc_a)
            acc_b[...] = jnp.zeros_like(acc_b)
            loss_ref[...] = jnp.zeros_like(loss_ref)
            dgf_ref[...] = jnp.zeros_like(dgf_ref)

        ya_v = ya_ref[...]
        yb_v = yb_ref[...]
        pa = jnp.dot(ya_v, wa_ref[...], preferred_element_type=F32)
        pb = jnp.dot(yb_v, wb_ref[...], preferred_element_type=F32)
        sa = _sigmoid(jnp.concatenate([ga0_ref[...], ga1_ref[...]], axis=1).astype(F32))
        sb = _sigmoid(jnp.concatenate([gb0_ref[...], gb1_ref[...]], axis=1).astype(F32))
        merged = (sa * pa + sb * pb).astype(BF16)
        x2 = x_ref[...] + jnp.dot(merged, wo_ref[...], preferred_element_type=F32)
        r2 = lax.rsqrt(jnp.mean(x2 * x2, axis=-1, keepdims=True) + EPS)
        xh = x2 * r2
        gf = gf_ref[...]
        diff = xh * gf - t_ref[...]
        loss_ref[...] += 0.5 * jnp.sum(jnp.mean(diff * diff, axis=-1, keepdims=True))
        dy = diff * (1.0 / D_MODEL)
        dgf_ref[...] += jnp.sum(dy * xh, axis=0, keepdims=True)
        dyg = dy * gf
        dx2 = r2 * (dyg - xh * jnp.mean(dyg * xh, axis=-1, keepdims=True))
        dx2_ref[...] = dx2
        dx2b = dx2.astype(BF16)
        dmerged = _dot_nt(dx2b, wo_ref[...])
        acc_o[...] += _dot_tn(merged, dx2b)
        dpa = dmerged * sa
        dpb = dmerged * sb
        dzg_ref[:, D_BRANCH:D_BRANCH + D_MODEL] = (dpa * pa * (1.0 - sa)).astype(BF16)
        dzg_ref[:, D_BRANCH + D_MODEL:D_BRANCH + 2 * D_MODEL] = (dpb * pb * (1.0 - sb)).astype(BF16)
        dpab = dpa.astype(BF16)
        dpbb = dpb.astype(BF16)
        acc_a[...] += _dot_tn(ya_v, dpab)
        acc_b[...] += _dot_tn(yb_v, dpbb)
        dya = _dot_nt(dpab, wa_ref[...])
        dyb_ref[...] = _dot_nt(dpbb, wb_ref[...]).astype(BF16)
        za = za_ref[...].astype(F32)
        sg = _sigmoid(za)
        do_ref[...] = (dya * (za * sg)).astype(BF16)
        dzg_ref[:, 0:D_BRANCH] = (dya * o_ref[...].astype(F32) * (sg * (1.0 + za * (1.0 - sg)))).astype(BF16)

        @pl.when(i == n_steps - 1)
        def _():
            gwo_ref[...] = acc_o[...].astype(BF16)
            gwa_ref[...] = acc_a[...].astype(BF16)
            gwb_ref[...] = acc_b[...].astype(BF16)

    tok = lambda w: pl.BlockSpec((tm, w), lambda i: (i, 0))
    col = lambda c: pl.BlockSpec((tm, half), lambda i: (i, c))
    full = lambda shape: pl.BlockSpec(shape, lambda i: (0,) * len(shape))
    return pl.pallas_call(
        body, name="mid", grid=(n_steps,),
        in_specs=[tok(D_BRANCH), tok(D_BRANCH), tok(D_BRANCH), col(COL_ZA), col(COL_GA), col(COL_GA + 1),
                  col(COL_GB), col(COL_GB + 1), tok(D_MODEL), tok(D_MODEL), full((1, D_MODEL)),
                  full((D_BRANCH, D_MODEL)), full((D_BRANCH, D_MODEL)), full((D_MODEL, D_MODEL))],
        out_specs=[tok(D_BRANCH + 2 * D_MODEL), tok(D_BRANCH), tok(D_BRANCH), tok(D_MODEL),
                   full((D_MODEL, D_MODEL)), full((D_BRANCH, D_MODEL)), full((D_BRANCH, D_MODEL)),
                   full((8, LANES)), full((1, D_MODEL))],
        out_shape=[jax.ShapeDtypeStruct((s, D_BRANCH + 2 * D_MODEL), BF16),
                   jax.ShapeDtypeStruct((s, D_BRANCH), BF16), jax.ShapeDtypeStruct((s, D_BRANCH), BF16),
                   jax.ShapeDtypeStruct((s, D_MODEL), F32),
                   jax.ShapeDtypeStruct((D_MODEL, D_MODEL), BF16),
                   jax.ShapeDtypeStruct((D_BRANCH, D_MODEL), BF16), jax.ShapeDtypeStruct((D_BRANCH, D_MODEL), BF16),
                   jax.ShapeDtypeStruct((8, LANES), F32), jax.ShapeDtypeStruct((1, D_MODEL), F32)],
        scratch_shapes=[pltpu.VMEM((D_MODEL, D_MODEL), F32), pltpu.VMEM((D_BRANCH, D_MODEL), F32),
                        pltpu.VMEM((D_BRANCH, D_MODEL), F32)],
        compiler_params=_cparams(("arbitrary",)),
    )(ya, yb, o, proj, proj, proj, proj, proj, x, target, final_g, w_up_a, w_up_b, w_out)


def _dwin_early(ht, first, tile_of_first, second, tile_of_second):
    s = ht.shape[1]
    n1 = first.shape[1] // D_BRANCH
    n2 = second.shape[1] // D_BRANCH

    def body(ht_ref, a_ref, b_ref, out_ref):
        j = pl.program_id(0)

        @pl.when(j < n1)
        def _():
            out_ref[...] = jnp.dot(ht_ref[...], a_ref[...], preferred_element_type=F32).astype(BF16)

        @pl.when(j >= n1)
        def _():
            out_ref[...] = jnp.dot(ht_ref[...], b_ref[...], preferred_element_type=F32).astype(BF16)

    return pl.pallas_call(
        body, name="dwin_early", grid=(n1 + n2,),
        in_specs=[pl.BlockSpec((D_MODEL, s), lambda j: (0, 0)),
                  pl.BlockSpec((s, D_BRANCH), lambda j: (0, jnp.minimum(j, n1 - 1))),
                  pl.BlockSpec((s, D_BRANCH), lambda j: (0, jnp.maximum(j - n1, 0)))],
        out_specs=pl.BlockSpec((D_MODEL, D_BRANCH),
                               lambda j: (0, jnp.where(j < n1, tile_of_first(j), tile_of_second(j - n1)))),
        out_shape=jax.ShapeDtypeStruct((D_MODEL, D_IN), BF16),
        compiler_params=_cparams(("arbitrary",)),
    )(ht, first, second)


def _dwin_pieces(ht, pieces, first_tile, prev):
    s = ht.shape[1]
    n = len(pieces)

    def body(ht_ref, *refs):
        srcs = refs[:n]
        out_ref, buf, sems = refs[n + 1:]
        j = pl.program_id(0)

        @pl.when(j == 0)
        def _():
            for k in range(n):
                pltpu.make_async_copy(srcs[k], buf.at[k], sems.at[k]).start()

        for k in range(n):
            @pl.when(j == k)
            def _():
                pltpu.make_async_copy(srcs[k], buf.at[k], sems.at[k]).wait()

        out_ref[...] = jnp.dot(ht_ref[...], buf[j], preferred_element_type=F32).astype(BF16)

    any_spec = pl.BlockSpec(memory_space=pl.ANY)
    return pl.pallas_call(
        body, name="dwin_pieces", grid=(n,),
        in_specs=[pl.BlockSpec((D_MODEL, s), lambda j: (0, 0))] + [any_spec] * (n + 1),
        out_specs=pl.BlockSpec((D_MODEL, D_BRANCH), lambda j: (0, first_tile + j)),
        out_shape=jax.ShapeDtypeStruct((D_MODEL, D_IN), BF16),
        scratch_shapes=[pltpu.VMEM((n, s, D_BRANCH), BF16), pltpu.SemaphoreType.DMA((n,))],
        input_output_aliases={n + 1: 0},
        compiler_params=_cparams(("arbitrary",)),
    )(ht, *pieces, prev)


def _dh_dx(pieces, w_in, x, norm_g, dx2, g_in):
    s = x.shape[0]
    tm = min(256, s)
    n_steps = s // tm
    arrays = []
    for arr, _, _, _ in pieces:
        if not any(arr is a for a in arrays):
            arrays.append(arr)
    n_arr = len(arrays)
    plan = [([k for k, a in enumerate(arrays) if a is arr][0], wcol, off, width) for arr, wcol, off, width in pieces]

    def body(*refs):
        p_refs = refs[:n_arr]
        w_ref, x_ref, g_ref, dx2_ref, gin_ref, dx_ref, dg_ref, late_ref = refs[n_arr:n_arr + 8]
        step = pl.program_id(0)
        finish = _presum_program(step == 0, step == min(1, n_steps - 1), step == n_steps - 1, 0, _LATE_CHIPS,
                                 gin_ref, late_ref, refs[n_arr + 8:])

        @pl.when(step == 0)
        def _():
            dg_ref[...] = jnp.zeros_like(dg_ref)

        dh = None
        for k, wcol, off, width in plan:
            d = _dot_nt(p_refs[k][:, off:off + width], w_ref[:, wcol:wcol + width])
            dh = d if dh is None else dh + d
        xf = x_ref[...]
        r = lax.rsqrt(jnp.mean(xf * xf, axis=-1, keepdims=True) + EPS)
        xh = xf * r
        dg_ref[...] += jnp.sum(dh * xh, axis=0, keepdims=True)
        dhg = dh * g_ref[...]
        dx_ref[...] = r * (dhg - xh * jnp.mean(dhg * xh, axis=-1, keepdims=True)) + dx2_ref[...]

        @pl.when(step == n_steps - 1)
        def _():
            finish()

    tok = lambda w: pl.BlockSpec((tm, w), lambda i: (i, 0))
    full = lambda shape: pl.BlockSpec(shape, lambda i: (0,) * len(shape))
    any_spec = pl.BlockSpec(memory_space=pl.ANY)
    return pl.pallas_call(
        body, name="dh_dx", grid=(n_steps,),
        in_specs=[tok(a.shape[1]) for a in arrays] + [full((D_MODEL, D_IN)), tok(D_MODEL), full((1, D_MODEL)),
                                                      tok(D_MODEL), any_spec],
        out_specs=[tok(D_MODEL), full((1, D_MODEL)), any_spec],
        out_shape=[jax.ShapeDtypeStruct((s, D_MODEL), F32), jax.ShapeDtypeStruct((1, D_MODEL), F32),
                   jax.ShapeDtypeStruct((N_DEV,) + _GRAD_PIECE_SHAPES[0], BF16)],
        scratch_shapes=_presum_scratch(0, _LATE_CHIPS),
        compiler_params=pltpu.CompilerParams(dimension_semantics=("arbitrary",), vmem_limit_bytes=VMEM_LIMIT,
                                             has_side_effects=True),
    )(*arrays, w_in, x, norm_g, dx2, g_in)


ADAM_GRID = 4


def _adamw(sets):
    n = len(sets)
    c1 = 1.0 - ADAM_B1 ** ADAM_STEP
    c2 = 1.0 - ADAM_B2 ** ADAM_STEP

    def body(*refs):
        for k in range(n):
            w_ref, g_ref, m_ref, v_ref = refs[4 * k:4 * k + 4]
            g_out_ref, d_ref, nm_ref, nv_ref = refs[4 * n + 4 * k:4 * n + 4 * k + 4]
            gv = g_ref[...]
            g_out_ref[...] = gv
            nm = ADAM_B1 * m_ref[...] + (1.0 - ADAM_B1) * gv
            nv = ADAM_B2 * v_ref[...] + (1.0 - ADAM_B2) * (gv * gv)
            d_ref[...] = -ADAM_LR * ((nm / c1) / (jnp.sqrt(nv / c2) + ADAM_EPS) + ADAM_WD * w_ref[...])
            nm_ref[...] = nm
            nv_ref[...] = nv

    in_specs, out_specs, shapes, args = [], [], [], []
    for w, g, m, v, g_tile0 in sets:
        rows, cols = w.shape
        assert rows % (8 * ADAM_GRID) == 0, (rows, cols)
        spec = pl.BlockSpec((rows // ADAM_GRID, cols), lambda i: (i, 0))
        g_spec = pl.BlockSpec((rows // ADAM_GRID, cols), lambda i, t0=g_tile0: (t0 + i, 0))
        in_specs += [spec, g_spec, spec, spec]
        out_specs += [spec] * 4
        shapes += [jax.ShapeDtypeStruct((rows, cols), F32)] * 4
        args += [w, g, m, v]
    outs = pl.pallas_call(
        body, name="adamw", grid=(ADAM_GRID,),
        in_specs=in_specs, out_specs=out_specs, out_shape=shapes,
        compiler_params=_cparams(("parallel",)),
    )(*args)
    return [outs[4 * k:4 * k + 4] for k in range(n)]


def _reduce_grads_tail(grads, g_small, early_slots, late_in_slots):
    n_big = len(grads)
    n_arr = n_big + 1
    shard_shapes = [(2 * r, w) for r, w in _GRAD_PIECE_SHAPES]
    small_piece = (SMALL_PIECE, LANES)

    def body(*refs):
        src = refs[:n_arr]
        early = refs[n_arr:n_arr + n_big]
        late_in = refs[n_arr + n_big]
        n_in = n_arr + n_big + 1
        out = refs[n_in:n_in + n_arr]
        slots = refs[n_in + n_arr:n_in + 2 * n_arr]
        sums = refs[n_in + 2 * n_arr:n_in + 3 * n_arr]
        send1, recv1, send2, recv2, local_sems = refs[n_in + 3 * n_arr:]
        x, y, c = _place()
        me = 4 * x + 2 * y + c

        def piece_of(a, dev):
            return src[a].at[dev] if a == n_big else _grad_piece(src[a], a, dev)

        def late(a, dst_dev, src_dev):
            return pltpu.make_async_remote_copy(
                src_ref=piece_of(a, dst_dev), dst_ref=slots[a].at[src_dev],
                send_sem=send1.at[n_arr * dst_dev + a], recv_sem=recv1.at[n_arr * src_dev + a],
                device_id=_dev_id(dst_dev), device_id_type=MESH)

        def late_arrays(dev):
            return (n_big,)

        def load(a, dev, received):
            return pltpu.make_async_copy(received.at[dev], slots[a].at[dev], local_sems.at[n_arr * dev + a])

        def own(a, dev):
            return pltpu.make_async_copy(piece_of(a, dev), slots[a].at[dev], local_sems.at[n_arr * dev + a])

        def half_of(a, core):
            r, _ = _GRAD_PIECE_SHAPES[a]
            return out[a].at[pl.ds(_aligned(core * r, 8), r), :]

        def share(a, core, sibling):
            return pltpu.make_async_remote_copy(src_ref=sums[a], dst_ref=half_of(a, core), send_sem=send2.at[a],
                                                recv_sem=recv2.at[a], device_id=sibling, device_id_type=MESH)

        def keep(a, dev):
            dst = out[n_big].at[dev] if a == n_big else half_of(a, dev % 2)
            return pltpu.make_async_copy(sums[a], dst, local_sems.at[N_DEV * n_arr + a])

        def small_share(dst_dev, src_dev):
            return pltpu.make_async_remote_copy(src_ref=sums[n_big], dst_ref=out[n_big].at[src_dev],
                                                send_sem=send2.at[n_big + dst_dev], recv_sem=recv2.at[n_big + src_dev],
                                                device_id=_dev_id(dst_dev), device_id_type=MESH)

        for dev in range(N_DEV):
            @pl.when(me == dev)
            def _():
                received = [early[0] if dev in _EARLY_IN_DEVS else late_in] + list(early[1:])
                for a in range(n_arr):
                    own(a, dev).start()
                sources = sorted([dev] + _presum_sources(dev))
                for peer in range(N_DEV):
                    if peer != dev:
                        for a in late_arrays(peer):
                            late(a, peer, dev).start()
                        for a in range(n_big):
                            if peer in sources:
                                load(a, peer, received[a]).start()
                for a in range(n_arr):
                    own(a, dev).wait()
                for peer in range(N_DEV):
                    if peer != dev:
                        for a in late_arrays(dev):
                            late(a, dev, peer).wait_recv()
                        for a in range(n_big):
                            if peer in sources:
                                load(a, peer, received[a]).wait()
                for a in [n_big] + list(range(n_big)):
                    rows = slots[a].shape[1]
                    step = 64 if rows % 64 == 0 else 8
                    used = sources if a < n_big else list(range(N_DEV))

                    def add_rows(t, carry, a=a, step=step, used=used):
                        r0 = pl.multiple_of(t * step, step)
                        total = slots[a][used[0], pl.ds(r0, step), :].astype(F32)
                        for src_dev in used[1:]:
                            total = total + slots[a][src_dev, pl.ds(r0, step), :].astype(F32)
                        sums[a][pl.ds(r0, step), :] = total
                        return carry

                    lax.fori_loop(0, rows // step, add_rows, 0)
                    keep(a, dev).start()
                    if a == n_big:
                        for peer in range(N_DEV):
                            if peer != dev:
                                small_share(peer, dev).start()
                    else:
                        share(a, dev % 2, _dev_id(dev ^ 1)).start()

        for a in range(n_big):
            share(a, 1 - c, (x, y, 1 - c)).wait_recv()
        for dev in range(N_DEV):
            @pl.when(me != dev)
            def _():
                small_share(dev, dev).wait_recv()
                small_share(dev, me).wait_send()
                for a in late_arrays(dev):
                    late(a, dev, me).wait_send()
        for a in range(n_big):
            share(a, c, (x, y, 1 - c)).wait_send()
        for dev in range(N_DEV):
            @pl.when(me == dev)
            def _():
                for a in range(n_arr):
                    keep(a, dev).wait()

    any_spec = pl.BlockSpec(memory_space=pl.ANY)
    return pl.pallas_call(
        body, name="reduce_grads_tail",
        in_specs=[any_spec] * (n_arr + n_big + 1), out_specs=[any_spec] * n_arr,
        out_shape=[jax.ShapeDtypeStruct(sh, F32) for sh in shard_shapes]
        + [jax.ShapeDtypeStruct((N_DEV,) + small_piece, F32)],
        scratch_shapes=[pltpu.VMEM((N_DEV,) + sh, BF16) for sh in _GRAD_PIECE_SHAPES]
        + [pltpu.VMEM((N_DEV,) + small_piece, F32)]
        + [pltpu.VMEM(sh, F32) for sh in _GRAD_PIECE_SHAPES] + [pltpu.VMEM(small_piece, F32)]
        + [pltpu.SemaphoreType.DMA((N_DEV * n_arr,)), pltpu.SemaphoreType.DMA((N_DEV * n_arr,)),
           pltpu.SemaphoreType.DMA((n_big + N_DEV,)), pltpu.SemaphoreType.DMA((n_big + N_DEV,)),
           pltpu.SemaphoreType.DMA((N_DEV * n_arr + n_arr,))],
        compiler_params=pltpu.CompilerParams(vmem_limit_bytes=VMEM_LIMIT, has_side_effects=True),
    )(*grads, g_small, *early_slots, late_in_slots)


W_SPATIAL_ROWS = N_GROUPS * SGU_CHUNK
_REST_PARTS = ("norm_g", "sgu_ln_g", "sgu_ln_b", "b_spatial", "final_norm_g")
REST_ROWS = SMALL_ROWS - W_SPATIAL_ROWS
_LOSS_ROW = W_SPATIAL_ROWS + 8 * len(_REST_PARTS)


def _pack_rest(parts, loss_tile=None):
    rows = []
    for name in _REST_PARTS:
        a = parts[name].reshape(-1, LANES).astype(F32)
        rows.append(jnp.pad(a, ((0, 8 - a.shape[0]), (0, 0))))
    rows.append(jnp.zeros((8, LANES), F32) if loss_tile is None else loss_tile)
    rows.append(jnp.zeros((REST_ROWS - 8 * len(rows), LANES), F32))
    return jnp.concatenate(rows, axis=0)


def _pack_small(parts, loss_tile):
    return jnp.concatenate([parts["w_spatial"].reshape(W_SPATIAL_ROWS, LANES), _pack_rest(parts, loss_tile)], axis=0)


def _unpack_rest(packed, shapes):
    out = {}
    for k, name in enumerate(_REST_PARTS):
        n = math.prod(shapes[name])
        out[name] = packed[8 * k:8 * k + n // LANES].reshape(shapes[name])
    return out


def _local_step(proj, ht, x, target, norm_g, w_in, sgu_ln_g, sgu_ln_b, w_spatial, b_spatial, w_up_a, w_up_b, w_out,
                final_norm_g, bq, bk):
    pos = jnp.arange(SGU_CHUNK)
    keep = (pos[None, :] // SGU_SUBCHUNK) <= (pos[:, None] // SGU_SUBCHUNK)
    w_mask = jnp.where(keep[None], w_spatial, 0.0).astype(BF16)
    w_mask_t = jnp.swapaxes(w_mask, 1, 2)
    bias_full = jnp.repeat(b_spatial.T, GROUP_DIM, axis=1)
    ln_g = sgu_ln_g.reshape(1, D_BRANCH)
    ln_b = sgu_ln_b.reshape(1, D_BRANCH)
    final_g = final_norm_g.reshape(1, D_MODEL)

    o, ya, rsave = _attn_fwd(proj, bq, bk, 2 * ATTN_PAIRS)
    yb = _sgu_fwd(proj, ln_g, ln_b, w_mask, bias_full)
    dzg, do, dyb, dx2, g_out, g_up_a, g_up_b, loss_acc, d_final = _mid(
        proj, ya, yb, o, x, target, final_g, w_up_a, w_up_b, w_out)
    dsgu, d_wsp, d_bsp, d_lng, d_lnb = _sgu_bwd(proj, dyb, ln_g, ln_b, w_mask, w_mask_t, bias_full)
    g_in = _dwin_early(ht, dzg, lambda j: jnp.where(j == 0, COL_ZA, COL_GA - 1 + j), dsgu, lambda j: COL_UB + j)
    dq, dk, dv, *early_slots = _attn_bwd(proj, do, rsave, bq, bk, ATTN_PAIRS, (g_in, g_up_a, g_up_b, g_out))
    g_in = _dwin_pieces(ht, (dq, dk, dv), COL_Q, g_in)
    pieces = [(dq, COL_Q * D_BRANCH, 0, D_BRANCH), (dk, COL_K * D_BRANCH, 0, D_BRANCH),
              (dv, COL_V * D_BRANCH, 0, D_BRANCH), (dzg, COL_ZA * D_BRANCH, 0, D_BRANCH),
              (dsgu, COL_UB * D_BRANCH, 0, 3 * D_BRANCH), (dzg, COL_GA * D_BRANCH, D_BRANCH, 2 * D_MODEL)]
    dx, d_norm, late_in_slots = _dh_dx(pieces, w_in, x, norm_g, dx2, g_in)
    small = {"norm_g": d_norm, "sgu_ln_g": d_lng, "sgu_ln_b": d_lnb, "w_spatial": d_wsp,
             "b_spatial": d_bsp[:, :N_GROUPS].T, "final_norm_g": d_final}
    return loss_acc, dx, (g_in, g_up_a, g_up_b, g_out), small, early_slots, late_in_slots


def kernel(x, norm_g, w_in, sgu_ln_g, sgu_ln_b, w_spatial, b_spatial, w_up_a, w_up_b, w_out, final_norm_g, loss_target, m_norm_g, m_w_in, m_sgu_ln_g, m_sgu_ln_b, m_w_spatial, m_b_spatial, m_w_up_a, m_w_up_b, m_w_out, m_final_norm_g, v_norm_g, v_w_in, v_sgu_ln_g, v_sgu_ln_b, v_w_spatial, v_b_spatial, v_w_up_a, v_w_up_b, v_w_out, v_final_norm_g):
    big_names = ("w_in", "w_up_a", "w_up_b", "w_out")
    names = ("norm_g", "w_in", "sgu_ln_g", "sgu_ln_b", "w_spatial", "b_spatial", "w_up_a", "w_up_b", "w_out",
             "final_norm_g")
    w = dict(norm_g=norm_g, w_in=w_in, sgu_ln_g=sgu_ln_g, sgu_ln_b=sgu_ln_b, w_spatial=w_spatial,
             b_spatial=b_spatial, w_up_a=w_up_a, w_up_b=w_up_b, w_out=w_out, final_norm_g=final_norm_g)
    m = dict(norm_g=m_norm_g, w_in=m_w_in, sgu_ln_g=m_sgu_ln_g, sgu_ln_b=m_sgu_ln_b, w_spatial=m_w_spatial,
             b_spatial=m_b_spatial, w_up_a=m_w_up_a, w_up_b=m_w_up_b, w_out=m_w_out, final_norm_g=m_final_norm_g)
    v = dict(norm_g=v_norm_g, w_in=v_w_in, sgu_ln_g=v_sgu_ln_g, sgu_ln_b=v_sgu_ln_b, w_spatial=v_w_spatial,
             b_spatial=v_b_spatial, w_up_a=v_w_up_a, w_up_b=v_w_up_b, w_out=v_w_out, final_norm_g=v_final_norm_g)
    shapes = {n: w[n].shape for n in names}
    flat2d = lambda a: a.reshape(a.shape[-2:])

    proj, ht, *full = _in_proj_gather(x[0], norm_g, *[flat2d(w[n]) for n in big_names])
    loss, dx, big_grads, small, early_slots, late_in_slots = _local_step(
        proj, ht, x[0], loss_target[0], norm_g, full[0], sgu_ln_g[0], sgu_ln_b[0], w_spatial[0], b_spatial[0],
        full[1], full[2], full[3], final_norm_g, ATTN_Q_BLOCK, ATTN_K_BLOCK)
    packed = _pack_small(small, loss).reshape(N_DEV, SMALL_PIECE, LANES)
    red = _reduce_grads_tail(big_grads, packed, early_slots, late_in_slots)

    grads, deltas, new_m, new_v = {}, {}, {}, {}
    g_small = red[4].reshape(SMALL_ROWS, LANES)
    rows2d = lambda a: a.reshape(W_SPATIAL_ROWS, LANES)
    sets = [(flat2d(w[n]), g, flat2d(m[n]), flat2d(v[n]), 0) for n, g in zip(big_names, red[:4])]
    sets.append((rows2d(w_spatial), g_small, rows2d(m_w_spatial), rows2d(v_w_spatial), 0))
    sets.append((_pack_rest(w), g_small, _pack_rest(m), _pack_rest(v), W_SPATIAL_ROWS * ADAM_GRID // REST_ROWS))
    updated = _adamw(sets)
    for n, (g, d, nm, nv) in zip(big_names + ("w_spatial",), updated[:5]):
        grads[n], deltas[n], new_m[n], new_v[n] = (a.reshape(shapes[n]) for a in (g, d, nm, nv))
    g_rest, d, nm, nv = updated[5]
    for src, dst in ((g_rest, grads), (d, deltas), (nm, new_m), (nv, new_v)):
        dst.update(_unpack_rest(src, shapes))

    return (g_rest[_LOSS_ROW - W_SPATIAL_ROWS, 0], dx[None], *[grads[n] for n in names], *[deltas[n] for n in names],
            *[new_m[n] for n in names], *[new_v[n] for n in names])
```

```python
import math

import jax
import jax.numpy as jnp
from jax import lax
from jax.experimental import pallas as pl
from jax.experimental.pallas import tpu as pltpu

F32 = jnp.float32
BF16 = jnp.bfloat16

D_MODEL = 1024
N_HEADS = 8
HEAD_DIM = 64
D_BRANCH = 512
D_IN = 4 * D_BRANCH + 3 * D_BRANCH + 2 * D_MODEL
N_GROUPS = 8
GROUP_DIM = 64
SGU_CHUNK = 128
SGU_SUBCHUNK = 64
GROUP_SHIFT = 6
EPS = 1e-6
LANES = 128
ATTN_Q_BLOCK = 256
ATTN_K_BLOCK = 256
DEAD = -110.0
SKIPPED = -1e30
SCAN_PASSES = 1
ATTN_PAIRS = 2
N_CHIPS = 4
N_DEV = 8
MESH = pl.DeviceIdType.MESH

ADAM_LR = 0.001
ADAM_B1 = 0.9
ADAM_B2 = 0.999
ADAM_EPS = 1e-08
ADAM_WD = 0.01
ADAM_STEP = 10

COL_Q, COL_K, COL_V, COL_ZA, COL_UB, COL_VB, COL_ZB, COL_GA, COL_GB = 0, 1, 2, 3, 4, 5, 6, 7, 9

VMEM_LIMIT = 56 * 1024 * 1024

SMALL_ROWS = 1088
SMALL_PIECE = SMALL_ROWS // N_DEV


def _cparams(sem=None):
    return pltpu.CompilerParams(dimension_semantics=sem, vmem_limit_bytes=VMEM_LIMIT)


def _aligned(v, m):
    return v if isinstance(v, int) else pl.multiple_of(v, m)


def _sigmoid(x):
    return 1.0 / (1.0 + jnp.exp(-x))


def _gelu_and_grad(x):
    k = math.sqrt(2.0 / math.pi)
    x2 = x * x
    inner = k * (x + 0.044715 * x * x2)
    th = jnp.tanh(inner)
    g = 0.5 * x * (1.0 + th)
    dg = 0.5 * (1.0 + th) + 0.5 * x * (1.0 - th * th) * (k * (1.0 + 3.0 * 0.044715 * x2))
    return g, dg


def _split_dot(a, b_bf16, passes):
    out = None
    rem = a
    for _ in range(passes):
        part = rem.astype(BF16)
        d = jnp.dot(part, b_bf16, preferred_element_type=F32)
        out = d if out is None else out + d
        rem = rem - part.astype(F32)
    return out


def _dot_nt(a, b):
    return lax.dot_general(a, b, (((1,), (1,)), ((), ())), preferred_element_type=F32)


def _dot_tn(a, b):
    return lax.dot_general(a, b, (((0,), (0,)), ((), ())), preferred_element_type=F32)


def _place():
    x, y, c = lax.axis_index("x"), lax.axis_index("y"), lax.axis_index("c")
    return x, y, c


def _in_proj_gather(x, norm_g, w_in, w_up_a, w_up_b, w_out):
    s = x.shape[0]
    tm = min(1024, s)
    nt = s // tm
    shards = (w_in, w_up_a, w_up_b, w_out)
    n_arr = len(shards)
    col_sharded = (True, True, True, False)
    full_shapes = ((D_MODEL, D_IN), (D_BRANCH, D_MODEL), (D_BRANCH, D_MODEL), (D_MODEL, D_MODEL))
    w_shard = w_in.shape[1]
    half_rows = D_MODEL // 2
    stage_rows = 256

    def body(order_ref, x_ref, g_ref, *refs):
        src = refs[:n_arr]
        proj_ref, ht_ref = refs[n_arr:n_arr + 2]
        out = refs[n_arr + 2:2 * n_arr + 2]
        wsc, h_scr, stage = refs[2 * n_arr + 2:2 * n_arr + 5]
        small_stage = refs[2 * n_arr + 5:2 * n_arr + 8]
        small_cast = refs[2 * n_arr + 8:2 * n_arr + 11]
        send_sems, recv_sems, local_sems, stage_sems = refs[2 * n_arr + 11:]
        k = pl.program_id(0)
        i = pl.program_id(1)
        x_, y_, c = _place()
        chip = 2 * x_ + y_
        sibling = (x_, y_, 1 - c)
        others = [(x_, 1 - y_), (1 - x_, y_), (1 - x_, 1 - y_)]

        def region(a, chip_idx, half):
            if a == 0:
                return wsc.at[chip_idx, pl.ds(_aligned(half * half_rows, 16), half_rows), :]
            r, w = shards[a].shape
            hr = r // 2
            if col_sharded[a]:
                return out[a].at[pl.ds(_aligned(half * hr, 16), hr), pl.ds(_aligned(chip_idx * w, LANES), w)]
            return out[a].at[pl.ds(_aligned(chip_idx * r + half * hr, 16), hr), :]

        def remote(kk, a, chip_idx, half, to, own):
            s_ref = region(a, chip_idx, half)
            if own and a > 0:
                hr = shards[a].shape[0] // 2
                s_ref = small_cast[a - 1].at[pl.ds(_aligned(half * hr, 16), hr), :]
            return pltpu.make_async_remote_copy(src_ref=s_ref, dst_ref=region(a, chip_idx, half),
                                                send_sem=send_sems.at[kk], recv_sem=recv_sems.at[kk],
                                                device_id=to, device_id_type=MESH)

        def keep_whole(kk, chip_idx):
            return pltpu.make_async_copy(wsc.at[chip_idx],
                                         out[0].at[:, pl.ds(_aligned(chip_idx * w_shard, LANES), w_shard)],
                                         local_sems.at[kk])

        def small_stores():
            cps = []
            for a in range(1, n_arr):
                hr = shards[a].shape[0] // 2
                for half in range(2):
                    cps.append(pltpu.make_async_copy(small_cast[a - 1].at[pl.ds(half * hr, hr), :],
                                                     region(a, chip, half), local_sems.at[4 + 2 * (a - 1) + half]))
            return cps

        def arrive_and_pass(j):
            ochip = chip ^ j
            for a in range(n_arr):
                kk = n_arr * (j - 1) + a
                remote(kk, a, ochip, c, sibling, False).wait_recv()
                remote(3 * n_arr + kk, a, ochip, c, sibling, False).start()

        def from_sibling(j, a):
            remote(3 * n_arr + n_arr * (j - 1) + a, a, chip ^ j, 1 - c, sibling, False).wait_recv()

        @pl.when((k == 0) & (i == 0))
        def _():
            per_half = half_rows // stage_rows

            def rows_of(t):
                half = c if t < per_half else 1 - c
                return pl.multiple_of(half * half_rows + (t % per_half) * stage_rows, stage_rows)

            def load(t):
                return pltpu.make_async_copy(src[0].at[pl.ds(rows_of(t), stage_rows), :], stage.at[t % 2],
                                             stage_sems.at[t % 2])

            load(0).start()
            for t in range(2 * per_half):
                if t + 1 < 2 * per_half:
                    load(t + 1).start()
                load(t).wait()
                wsc[chip, pl.ds(rows_of(t), stage_rows), :] = stage[t % 2].astype(BF16)
                if t == per_half - 1:
                    for j in (1, 2):
                        remote(n_arr * (j - 1), 0, chip, c, (*others[j - 1], c), True).start()
            for a in range(1, n_arr):
                pltpu.sync_copy(src[a], small_stage[a - 1])
                small_cast[a - 1][...] = small_stage[a - 1][...].astype(BF16)
            for j in (1, 2):
                for a in range(1, n_arr):
                    remote(n_arr * (j - 1) + a, a, chip, c, (*others[j - 1], c), True).start()
            keep_whole(0, chip).start()
            for cp in small_stores():
                cp.start()

        @pl.when((k == 1) & (i == 0))
        def _():
            for j in (1, 2):
                remote(n_arr * (j - 1), 0, chip, c, (*others[j - 1], c), True).wait_send()
            for a in range(n_arr):
                remote(n_arr * 2 + a, a, chip, c, (*others[2], c), True).start()
            arrive_and_pass(1)
            from_sibling(1, 0)
            keep_whole(1, chip ^ 1).start()

        @pl.when((k == 1) & (i == min(1, nt - 1)))
        def _():
            arrive_and_pass(2)

        @pl.when((k == 2) & (i == 0))
        def _():
            from_sibling(2, 0)
            keep_whole(2, chip ^ 2).start()
            arrive_and_pass(3)

        @pl.when((k == 3) & (i == 0))
        def _():
            from_sibling(3, 0)
            keep_whole(3, chip ^ 3).start()

        @pl.when(k == 0)
        def _():
            xf = x_ref[...]
            r = lax.rsqrt(jnp.mean(xf * xf, axis=-1, keepdims=True) + EPS)
            h = xf * r * g_ref[...]
            h_scr[i] = h.astype(BF16)
            ht_ref[...] = h.T.astype(BF16)

        proj_ref[...] = jnp.dot(h_scr[i], wsc[order_ref[k]], preferred_element_type=F32).astype(BF16)

        @pl.when((k == 3) & (i == nt - 1))
        def _():
            for j in (1, 2, 3):
                for a in range(1, n_arr):
                    from_sibling(j, a)
            for j in (1, 2, 3):
                for a in range(n_arr):
                    kk = n_arr * (j - 1) + a
                    if a > 0 or j == 3:
                        remote(kk, a, chip, c, (*others[j - 1], c), True).wait_send()
                    remote(3 * n_arr + kk, a, chip ^ j, c, sibling, False).wait_send()
            for kk in range(4):
                keep_whole(kk, chip ^ kk).wait()
            for cp in small_stores():
                cp.wait()

    any_spec = pl.BlockSpec(memory_space=pl.ANY)
    tile = lambda kk, ii: jnp.where(kk == 0, ii, nt - 1)
    grid_spec = pltpu.PrefetchScalarGridSpec(
        num_scalar_prefetch=1, grid=(N_CHIPS, nt),
        in_specs=[pl.BlockSpec((tm, D_MODEL), lambda kk, ii, order: (tile(kk, ii), 0)),
                  pl.BlockSpec((1, D_MODEL), lambda kk, ii, order: (0, 0))] + [any_spec] * n_arr,
        out_specs=[pl.BlockSpec((tm, w_shard), lambda kk, ii, order: (ii, order[kk])),
                   pl.BlockSpec((D_MODEL, tm), lambda kk, ii, order: (0, tile(kk, ii)))] + [any_spec] * n_arr,
        scratch_shapes=[pltpu.VMEM((N_CHIPS, D_MODEL, w_shard), BF16), pltpu.VMEM((nt, tm, D_MODEL), BF16),
                        pltpu.VMEM((2, stage_rows, w_shard), F32)]
        + [pltpu.VMEM(a.shape, F32) for a in shards[1:]] + [pltpu.VMEM(a.shape, BF16) for a in shards[1:]]
        + [pltpu.SemaphoreType.DMA((6 * n_arr,)), pltpu.SemaphoreType.DMA((6 * n_arr,)),
           pltpu.SemaphoreType.DMA((4 + 2 * (n_arr - 1),)), pltpu.SemaphoreType.DMA((2,))])
    x_, y_, _ = _place()
    order = (2 * x_ + y_) ^ jnp.arange(N_CHIPS, dtype=jnp.int32)
    return pl.pallas_call(
        body, name="in_proj_gather", grid_spec=grid_spec,
        out_shape=[jax.ShapeDtypeStruct((s, D_IN), BF16), jax.ShapeDtypeStruct((D_MODEL, s), BF16)]
        + [jax.ShapeDtypeStruct(sh, BF16) for sh in full_shapes],
        compiler_params=pltpu.CompilerParams(dimension_semantics=("arbitrary", "arbitrary"),
                                             vmem_limit_bytes=VMEM_LIMIT, has_side_effects=True),
    )(order, x, norm_g, *shards)


def _neg_softplus_parts(z):
    zb = z.astype(BF16)
    p = jnp.exp(-jnp.abs(zb))
    return p, jnp.maximum(zb, jnp.zeros_like(zb)) + jnp.log(1.0 + p)


def _split_cat(a, passes):
    parts = []
    rem = a
    for k in range(passes):
        part = rem.astype(BF16)
        parts.append(part)
        if k + 1 < passes:
            rem = rem - part.astype(F32)
    return parts[0] if passes == 1 else jnp.concatenate(parts, axis=1)


def _tri(blk, upper, sign):
    row = lax.broadcasted_iota(jnp.int32, (blk, blk), 0)
    col = lax.broadcasted_iota(jnp.int32, (blk, blk), 1)
    keep = (row <= col) if upper else (row >= col)
    t = jnp.where(keep, sign, 0.0).astype(BF16)
    return t if SCAN_PASSES == 1 else jnp.concatenate([t] * SCAN_PASSES, axis=0)


def _attn_fwd(proj, bq, bk, npairs):
    s = proj.shape[0]
    nq = s // bq
    ratio = bq // bk
    scale = HEAD_DIM ** -0.5
    heads = tuple(range(2 * npairs))
    width = LANES * npairs

    def body(q_ref, k_ref, v_ref, za_ref, o_ref, ya_ref, rs_ref, acc_ref, r_ref):
        i = pl.program_id(1)
        lane = lax.broadcasted_iota(jnp.int32, (bq, LANES), 1)
        lo_half = lane < HEAD_DIM
        qm = []
        for pr in range(npairs):
            q = q_ref[:, LANES * pr:LANES * (pr + 1)] * jnp.asarray(scale, BF16)
            zero = jnp.zeros_like(q)
            qm += [jnp.where(lo_half, q, zero), jnp.where(lo_half, zero, q)]
        row = lax.broadcasted_iota(jnp.int32, (bq, bk), 0)
        col = lax.broadcasted_iota(jnp.int32, (bq, bk), 1)
        tneg = _tri(bk, False, -1.0)
        acc_ref[...] = jnp.zeros_like(acc_ref)
        r_ref[...] = jnp.zeros_like(r_ref)
        rs_ref[...] = jnp.full_like(rs_ref, SKIPPED)

        def scores(j):
            ks = pl.multiple_of(j * bk, bk)
            return [_dot_nt(qm[h], k_ref[pl.ds(ks, bk), LANES * (h // 2):LANES * (h // 2 + 1)]) for h in heads]

        def block(j, diag, valid=None):
            ks = pl.multiple_of(j * bk, bk)
            vj = [v_ref[pl.ds(ks, bk), LANES * pr:LANES * (pr + 1)] for pr in range(npairs)]
            if diag:
                before = (j * bk + col) < (i * bq + row)
            z = scores(j)
            sp = [_neg_softplus_parts(z[h])[1] for h in heads]
            if diag:
                sp = [jnp.where(before, sp[h], 0.0) for h in heads]
            cin = [jnp.dot(_split_cat(sp[h], SCAN_PASSES), tneg, preferred_element_type=F32) for h in heads]
            w = [jnp.exp(z[h] + cin[h]) for h in heads]
            if diag:
                w = [jnp.where(before, w[h], 0.0) for h in heads]
            pv = [jnp.dot(w[h].astype(BF16), vj[h // 2], preferred_element_type=F32) for h in heads]
            r = [r_ref[h] for h in heads]
            keep = 1.0 if valid is None else valid.astype(F32)
            for h in heads:
                acc_ref[h] += pv[h] * (jnp.exp(r[h]) * keep)
                r_ref[h] = r[h] + cin[h][:, 0:1] * keep
            for pr in range(npairs):
                hit = [lane == j, lane == j + HEAD_DIM]
                if valid is not None:
                    hit = [m & valid for m in hit]
                rs_ref[pr] = jnp.where(hit[0], r[2 * pr], jnp.where(hit[1], r[2 * pr + 1], rs_ref[pr]))

        for t in range(ratio):
            block(i * ratio + ratio - 1 - t, True)
        block(jnp.maximum(i * ratio - 1, 0), False, valid=i > 0)

        def alive(carry):
            jj, r_max = carry
            return (jj < i * ratio - 1) & (r_max > DEAD)

        def loop_body(carry):
            jj, _ = carry
            block(i * ratio - 2 - jj, False)
            return jj + 1, jnp.max(r_ref[...])

        lax.while_loop(alive, loop_body, (0, jnp.max(r_ref[...])))
        for pr in range(npairs):
            cols = slice(LANES * pr, LANES * (pr + 1))
            o = jnp.where(lo_half, acc_ref[2 * pr], acc_ref[2 * pr + 1])
            o_ref[:, cols] = o.astype(BF16)
            za = za_ref[:, cols].astype(F32)
            ya_ref[:, cols] = (o * (za * _sigmoid(za))).astype(BF16)

    n_steps = N_HEADS // (2 * npairs)
    return pl.pallas_call(
        body, name="attn_fwd", grid=(n_steps, nq),
        in_specs=[pl.BlockSpec((bq, width), lambda p, i: (i, n_steps * COL_Q + p)),
                  pl.BlockSpec((s, width), lambda p, i: (0, n_steps * COL_K + p)),
                  pl.BlockSpec((s, width), lambda p, i: (0, n_steps * COL_V + p)),
                  pl.BlockSpec((bq, width), lambda p, i: (i, n_steps * COL_ZA + p))],
        out_specs=[pl.BlockSpec((bq, width), lambda p, i: (i, p)),
                   pl.BlockSpec((bq, width), lambda p, i: (i, p)),
                   pl.BlockSpec((npairs, bq, LANES), lambda p, i: (p, i, 0))],
        out_shape=[jax.ShapeDtypeStruct((s, D_BRANCH), BF16), jax.ShapeDtypeStruct((s, D_BRANCH), BF16),
                   jax.ShapeDtypeStruct((N_HEADS // 2, s, LANES), F32)],
        scratch_shapes=[pltpu.VMEM((2 * npairs, bq, LANES), F32), pltpu.VMEM((2 * npairs, bq, 1), F32)],
        compiler_params=_cparams(("parallel", "parallel")),
    )(proj, proj, proj, proj)


_GRAD_COL_SHARDED = (True, True, True, False)
_GRAD_FULL_SHAPES = ((D_MODEL, D_IN), (D_BRANCH, D_MODEL), (D_BRANCH, D_MODEL), (D_MODEL, D_MODEL))
_GRAD_PIECE_SHAPES = tuple((r // 2, w // N_CHIPS) if cs else (r // (2 * N_CHIPS), w)
                           for (r, w), cs in zip(_GRAD_FULL_SHAPES, _GRAD_COL_SHARDED))
_EARLY_IN_DEVS = (4, 5, 6, 7)
_LATE_IN_DEVS = (0, 1, 2, 3)
_LATE_CHIPS = (0, 1)
_EARLY_CHIPS = (2, 3)
_ALL_CHIPS = (0, 1, 2, 3)


def _grad_piece(ref, a, dev):
    r, w = _GRAD_PIECE_SHAPES[a]
    if _GRAD_COL_SHARDED[a]:
        return ref.at[pl.ds((dev % 2) * r, r), pl.ds((dev // 2) * w, w)]
    return ref.at[pl.ds(dev * r, r), :]


def _dev_id(dev):
    return (dev // 4, (dev // 2) % 2, dev % 2)


def _me():
    return 4 * lax.axis_index("x") + 2 * lax.axis_index("y") + lax.axis_index("c")


def _presum_copy(src, dst, send_sem, recv_sem, to_dev):
    return pltpu.make_async_remote_copy(src_ref=src, dst_ref=dst, send_sem=send_sem, recv_sem=recv_sem,
                                        device_id=_dev_id(to_dev), device_id_type=MESH)


def _presum_hand_off(dev, a, dest_chips, g_ref, slots, pair, send_sems, recv_sems):
    chip, core = dev // 2, dev % 2
    cps = []
    for k, q in enumerate(dest_chips):
        piece = _grad_piece(g_ref, a, 2 * q + 1 - core)
        if q == chip:
            cps.append(_presum_copy(piece, slots.at[dev], send_sems.at[N_DEV + k], recv_sems.at[dev], dev ^ 1))
        else:
            cps.append(_presum_copy(piece, pair.at[k], send_sems.at[N_DEV + k], recv_sems.at[N_DEV + k], dev ^ 1))
    return cps


def _presum_sends(dev, a, dest_chips, slots, sums, send_sems, recv_sems):
    chip, core = dev // 2, dev % 2
    return [_presum_copy(sums.at[k], slots.at[dev], send_sems.at[2 * q + core], recv_sems.at[dev], 2 * q + core)
            for k, q in enumerate(dest_chips) if q != chip]


def _presum_loads(dev, a, dest_chips, g_ref, stage, load_sems):
    chip, core = dev // 2, dev % 2
    return [pltpu.make_async_copy(_grad_piece(g_ref, a, 2 * q + core), stage.at[k], load_sems.at[k])
            for k, q in enumerate(dest_chips) if q != chip]


def _presum_send(dev, a, dest_chips, g_ref, slots, pair, stage, sums, send_sems, recv_sems, load_sems):
    chip, core = dev // 2, dev % 2
    hand = _presum_hand_off(dev, a, dest_chips, g_ref, slots, pair, send_sems, recv_sems)
    for cp in _presum_loads(dev, a, dest_chips, g_ref, stage, load_sems):
        cp.wait()
    for k, q in enumerate(dest_chips):
        if q != chip:
            hand[k].wait_recv()
            sums[k] = (stage[k].astype(F32) + pair[k].astype(F32)).astype(BF16)
    for cp in _presum_sends(dev, a, dest_chips, slots, sums, send_sems, recv_sems):
        cp.start()


def _presum_wait(dev, a, dest_chips, g_ref, slots, pair, sums, send_sems, recv_sems):
    chip, core = dev // 2, dev % 2
    for cp in _presum_hand_off(dev, a, dest_chips, g_ref, slots, pair, send_sems, recv_sems):
        cp.wait_send()
    for cp in _presum_sends(dev, a, dest_chips, slots, sums, send_sems, recv_sems):
        cp.wait_send()
    if chip in dest_chips:
        for src_dev in _presum_sources(dev):
            _presum_copy(sums.at[0], slots.at[src_dev], send_sems.at[src_dev], recv_sems.at[src_dev], src_dev).wait_recv()


def _presum_sources(dev):
    return [dev ^ 1] + [2 * r + dev % 2 for r in range(N_CHIPS) if r != dev // 2]


def _presum_scratch(a, dest_chips):
    n = len(dest_chips)
    piece = _GRAD_PIECE_SHAPES[a]
    return [pltpu.VMEM((n,) + piece, BF16), pltpu.VMEM((n,) + piece, BF16), pltpu.VMEM((n,) + piece, BF16),
            pltpu.SemaphoreType.DMA((N_DEV + n,)), pltpu.SemaphoreType.DMA((N_DEV + n,)),
            pltpu.SemaphoreType.DMA((n,))]


PRESUM_SCRATCH = 6


def _presum_program(first, second, last, a, dest_chips, g_ref, slots, scratch):
    pair, stage, sums, send_sems, recv_sems, load_sems = scratch
    me = _me()

    @pl.when(first)
    def _():
        for dev in range(N_DEV):
            @pl.when(me == dev)
            def _():
                for cp in _presum_hand_off(dev, a, dest_chips, g_ref, slots, pair, send_sems, recv_sems):
                    cp.start()
                for cp in _presum_loads(dev, a, dest_chips, g_ref, stage, load_sems):
                    cp.start()

    @pl.when(second)
    def _():
        for dev in range(N_DEV):
            @pl.when(me == dev)
            def _():
                _presum_send(dev, a, dest_chips, g_ref, slots, pair, stage, sums, send_sems, recv_sems, load_sems)

    def finish():
        for dev in range(N_DEV):
            @pl.when(me == dev)
            def _():
                _presum_wait(dev, a, dest_chips, g_ref, slots, pair, sums, send_sems, recv_sems)

    return finish


def _attn_bwd(proj, do, rsave, bq, bk, npairs, grads):
    plan = ((0, _EARLY_CHIPS), (1, _ALL_CHIPS), (2, _ALL_CHIPS), (3, _ALL_CHIPS))
    s = proj.shape[0]
    nq = s // bq
    ratio = bq // bk
    scale = HEAD_DIM ** -0.5
    heads = tuple(range(2 * npairs))
    width = LANES * npairs

    n_steps = N_HEADS // (2 * npairs)
    n_g = len(grads)

    def body(q_ref, k_ref, v_ref, do_ref, rs_ref, *refs):
        g_src = refs[:n_g]
        dq_ref, dk_ref, dv_ref = refs[n_g:n_g + 3]
        g_slots = refs[n_g + 3:2 * n_g + 3]
        dk_acc, dv_acc, dq_acc, e_ref = refs[2 * n_g + 3:2 * n_g + 7]
        i = pl.program_id(1)
        step = pl.program_id(0) * nq + i
        finish = [_presum_program(step == 0, step == 1, step == n_steps * nq - 1, a, chips, g_src[pos], g_slots[pos],
                                  refs[2 * n_g + 7 + PRESUM_SCRATCH * pos:2 * n_g + 7 + PRESUM_SCRATCH * (pos + 1)])
                  for pos, (a, chips) in enumerate(plan)]

        lane = lax.broadcasted_iota(jnp.int32, (bq, LANES), 1)
        lo_half = lane < HEAD_DIM
        qm, dom = [], []
        for pr in range(npairs):
            cols = slice(LANES * pr, LANES * (pr + 1))
            q = q_ref[:, cols] * jnp.asarray(scale, BF16)
            zero = jnp.zeros_like(q)
            qm += [jnp.where(lo_half, q, zero), jnp.where(lo_half, zero, q)]
            dout = do_ref[:, cols].astype(F32)
            dom += [jnp.where(lo_half, dout, 0.0), jnp.where(lo_half, 0.0, dout)]
        row = lax.broadcasted_iota(jnp.int32, (bq, bk), 0)
        col = lax.broadcasted_iota(jnp.int32, (bq, bk), 1)
        tneg = _tri(bk, False, -1.0)
        tfwd = _tri(bk, True, 1.0)

        @pl.when(i == 0)
        def _():
            dk_acc[...] = jnp.zeros_like(dk_acc)
            dv_acc[...] = jnp.zeros_like(dv_acc)

        dq_acc[...] = jnp.zeros_like(dq_acc)
        e_ref[...] = jnp.zeros_like(e_ref)

        def block(j, diag, valid=None):
            ks = pl.multiple_of(j * bk, bk)
            kj = [k_ref[pl.ds(ks, bk), LANES * pr:LANES * (pr + 1)] for pr in range(npairs)]
            vj = [v_ref[pl.ds(ks, bk), LANES * pr:LANES * (pr + 1)] for pr in range(npairs)]
            if diag:
                before = (j * bk + col) < (i * bq + row)
            z = [_dot_nt(qm[h], kj[h // 2]) for h in heads]
            er = [jnp.exp(jnp.sum(jnp.where(lane == j + HEAD_DIM * (h % 2), rs_ref[h // 2], 0.0), axis=-1,
                                  keepdims=True)) for h in heads]
            if valid is not None:
                er = [er[h] * valid.astype(F32) for h in heads]
            dos = [(dom[h] * er[h]).astype(BF16) for h in heads]
            dw = [_dot_nt(dos[h], vj[h // 2]) for h in heads]
            psp = [_neg_softplus_parts(z[h]) for h in heads]
            sp = [psp[h][1] for h in heads]
            if diag:
                sp = [jnp.where(before, sp[h], 0.0) for h in heads]
            cin = [jnp.dot(_split_cat(sp[h], SCAN_PASSES), tneg, preferred_element_type=F32) for h in heads]
            w = [jnp.exp(z[h] + cin[h]) for h in heads]
            if diag:
                w = [jnp.where(before, w[h], 0.0) for h in heads]
            e =[dw[h] * w[h] for h in heads]
            eincl = [jnp.dot(_split_cat(e[h], SCAN_PASSES), tfwd, preferred_element_type=F32) + e_ref[h]
                     for h in heads]
            dz = []
            for h in heads:
                p = psp[h][0]
                beta = jnp.where(z[h] >= 0.0, 1.0, p) / (1.0 + p)
                d = e[h] - beta * eincl[h]
                dz.append((jnp.where(before, d, 0.0) if diag else d).astype(BF16))
            wb = [w[h].astype(BF16) for h in heads]
            for h in heads:
                e_ref[h] = eincl[h][:, bk - 1:bk]
                dq_acc[h] += jnp.dot(dz[h], kj[h // 2], preferred_element_type=F32)
            for pr in range(npairs):
                cols = slice(LANES * pr, LANES * (pr + 1))
                h0, h1 = 2 * pr, 2 * pr + 1
                dk_acc[pl.ds(ks, bk), cols] += _dot_tn(dz[h0], qm[h0]) + _dot_tn(dz[h1], qm[h1])
                dv_acc[pl.ds(ks, bk), cols] += _dot_tn(wb[h0], dos[h0]) + _dot_tn(wb[h1], dos[h1])

        def loop_body(j, carry):
            block(j, False)
            return carry

        block_of_lane = lane & (HEAD_DIM - 1)
        live = jnp.max(rs_ref[...], axis=0) > DEAD
        first_live = jnp.min(jnp.where(live, block_of_lane, nq * ratio))
        last = jnp.maximum(i * ratio - 1, 0)
        lax.fori_loop(jnp.minimum(first_live, last), last, loop_body, 0)
        block(last, False, valid=i > 0)
        for t in range(ratio):
            block(i * ratio + t, True)
        for pr in range(npairs):
            dq = jnp.where(lo_half, dq_acc[2 * pr], dq_acc[2 * pr + 1]) * scale
            dq_ref[:, LANES * pr:LANES * (pr + 1)] = dq.astype(BF16)

        @pl.when(i == nq - 1)
        def _():
            dk_ref[...] = dk_acc[...].astype(BF16)
            dv_ref[...] = dv_acc[...].astype(BF16)

        @pl.when(step == n_steps * nq - 1)
        def _():
            for fin in finish:
                fin()

    any_spec = pl.BlockSpec(memory_space=pl.ANY)
    return pl.pallas_call(
        body, name="attn_bwd", grid=(n_steps, nq),
        in_specs=[pl.BlockSpec((bq, width), lambda p, i: (i, n_steps * COL_Q + p)),
                  pl.BlockSpec((s, width), lambda p, i: (0, n_steps * COL_K + p)),
                  pl.BlockSpec((s, width), lambda p, i: (0, n_steps * COL_V + p)),
                  pl.BlockSpec((bq, width), lambda p, i: (i, p)),
                  pl.BlockSpec((npairs, bq, LANES), lambda p, i: (p, i, 0))] + [any_spec] * n_g,
        out_specs=[pl.BlockSpec((bq, width), lambda p, i: (i, p)),
                   pl.BlockSpec((s, width), lambda p, i: (0, p)),
                   pl.BlockSpec((s, width), lambda p, i: (0, p))] + [any_spec] * n_g,
        out_shape=[jax.ShapeDtypeStruct((s, D_BRANCH), BF16)] * 3
        + [jax.ShapeDtypeStruct((N_DEV,) + _GRAD_PIECE_SHAPES[a], BF16) for a, _ in plan],
        scratch_shapes=[pltpu.VMEM((s, width), F32), pltpu.VMEM((s, width), F32),
                        pltpu.VMEM((2 * npairs, bq, LANES), F32), pltpu.VMEM((2 * npairs, bq, 1), F32)]
        + [sh for a, chips in plan for sh in _presum_scratch(a, chips)],
        compiler_params=pltpu.CompilerParams(dimension_semantics=("arbitrary", "arbitrary"),
                                             vmem_limit_bytes=VMEM_LIMIT, has_side_effects=True),
    )(proj, proj, proj, do, rsave, *grads)


def _group_avg_matrix():
    a = lax.broadcasted_iota(jnp.int32, (LANES, LANES), 0) >> GROUP_SHIFT
    b = lax.broadcasted_iota(jnp.int32, (LANES, LANES), 1) >> GROUP_SHIFT
    return jnp.where(a == b, 1.0 / GROUP_DIM, 0.0).astype(BF16)


def _group_mean(a, avg):
    parts = [_split_dot(a[:, LANES * k:LANES * (k + 1)], avg, 2) for k in range(D_BRANCH // LANES)]
    return jnp.concatenate(parts, axis=1)


def _sgu_forward_parts(ub, vb, ln_g, ln_b, avg):
    ug, dug = _gelu_and_grad(ub)
    vg, dvg = _gelu_and_grad(vb)
    mu = _group_mean(vg, avg)
    d = vg - mu
    var = _group_mean(d * d, avg)
    rstd = lax.rsqrt(var + EPS)
    vhat = d * rstd
    vn = vhat * ln_g + ln_b
    return ug, dug, dvg, rstd, vhat, vn


def _sgu_mix(w_ref, src_bf16, n_chunks):
    lane = lax.broadcasted_iota(jnp.int32, (SGU_CHUNK, LANES), 1)
    lo_half = lane < GROUP_DIM
    rows = []
    for n in range(n_chunks):
        slabs = []
        for a in range(D_BRANCH // LANES):
            blk = src_bf16[SGU_CHUNK * n:SGU_CHUNK * (n + 1), LANES * a:LANES * (a + 1)]
            zero = jnp.zeros_like(blk)
            m0 = jnp.dot(w_ref[2 * a], jnp.where(lo_half, blk, zero), preferred_element_type=F32)
            m1 = jnp.dot(w_ref[2 * a + 1], jnp.where(lo_half, zero, blk), preferred_element_type=F32)
            slabs.append(m0 + m1)
        rows.append(jnp.concatenate(slabs, axis=1))
    return jnp.concatenate(rows, axis=0)


def _sgu_fwd(proj, ln_g, ln_b, w_mask, bias_full):
    s = proj.shape[0]
    tm = min(512, s)
    n_chunks = tm // SGU_CHUNK

    def body(ub_ref, vb_ref, zb_ref, g_ref, b_ref, w_ref, bias_ref, yb_ref):
        avg = _group_avg_matrix()
        ug, _, _, _, _, vn = _sgu_forward_parts(ub_ref[...].astype(F32), vb_ref[...].astype(F32),
                                                g_ref[...], b_ref[...], avg)
        mixed = _sgu_mix(w_ref, vn.astype(BF16), n_chunks) + jnp.concatenate([bias_ref[...]] * n_chunks, axis=0)
        zb = zb_ref[...].astype(F32)
        yb_ref[...] = (ug * mixed * (zb * _sigmoid(zb))).astype(BF16)

    col = lambda c: pl.BlockSpec((tm, D_BRANCH), lambda i: (i, c))
    full = lambda shape: pl.BlockSpec(shape, lambda i: (0,) * len(shape))
    return pl.pallas_call(
        body, name="sgu_fwd", grid=(s // tm,),
        in_specs=[col(COL_UB), col(COL_VB), col(COL_ZB), full((1, D_BRANCH)), full((1, D_BRANCH)),
                  full((N_GROUPS, SGU_CHUNK, SGU_CHUNK)), full((SGU_CHUNK, D_BRANCH))],
        out_specs=pl.BlockSpec((tm, D_BRANCH), lambda i: (i, 0)),
        out_shape=jax.ShapeDtypeStruct((s, D_BRANCH), BF16),
        compiler_params=_cparams(("parallel",)),
    )(proj, proj, proj, ln_g, ln_b, w_mask, bias_full)


def _sgu_bwd(proj, dyb, ln_g, ln_b, w_mask, w_mask_t, bias_full):
    s = proj.shape[0]
    tm = min(512, s)
    n_chunks = tm // SGU_CHUNK
    n_steps = s // tm

    def body(ub_ref, vb_ref, zb_ref, dyb_ref, g_ref, b_ref, w_ref, wt_ref, bias_ref,
             dsgu_ref, dw_ref, db_ref, dg_ref, dbeta_ref, dmix_acc):
        i = pl.program_id(0)

        @pl.when(i == 0)
        def _():
            dw_ref[...] = jnp.zeros_like(dw_ref)
            dg_ref[...] = jnp.zeros_like(dg_ref)
            dbeta_ref[...] = jnp.zeros_like(dbeta_ref)
            dmix_acc[...] = jnp.zeros_like(dmix_acc)

        avg = _group_avg_matrix()
        ln_gv = g_ref[...]
        ug, dug, dvg, rstd, vhat, vn = _sgu_forward_parts(ub_ref[...].astype(F32), vb_ref[...].astype(F32),
                                                          ln_gv, b_ref[...], avg)
        vnb = vn.astype(BF16)
        mixed = _sgu_mix(w_ref, vnb, n_chunks) + jnp.concatenate([bias_ref[...]] * n_chunks, axis=0)
        zb = zb_ref[...].astype(F32)
        sg = _sigmoid(zb)
        sz = zb * sg
        dsz = sg * (1.0 + zb * (1.0 - sg))
        dy = dyb_ref[...].astype(F32)
        dmixed = dy * ug * sz
        du = dy * mixed * sz * dug
        dzb = dy * ug * mixed * dsz
        dmb = dmixed.astype(BF16)
        dvn = _sgu_mix(wt_ref, dmb, n_chunks)

        lane = lax.broadcasted_iota(jnp.int32, (SGU_CHUNK, LANES), 1)
        lo_half = lane < GROUP_DIM
        dm_sum = None
        for n in range(n_chunks):
            rows = slice(SGU_CHUNK * n, SGU_CHUNK * (n + 1))
            dm_sum = dmixed[rows] if dm_sum is None else dm_sum + dmixed[rows]
            for a in range(D_BRANCH // LANES):
                cols = slice(LANES * a, LANES * (a + 1))
                dblk = dmb[rows, cols]
                vblk = vnb[rows, cols]
                zero = jnp.zeros_like(dblk)
                dw_ref[2 * a] += _dot_nt(jnp.where(lo_half, dblk, zero), vblk)
                dw_ref[2 * a + 1] += _dot_nt(jnp.where(lo_half, zero, dblk), vblk)
        dmix_acc[...] += dm_sum

        dg_ref[...] += jnp.sum(dvn * vhat, axis=0, keepdims=True)
        dbeta_ref[...] += jnp.sum(dvn, axis=0, keepdims=True)
        dvh = dvn * ln_gv
        m1 = _group_mean(dvh, avg)
        m2 = _group_mean(dvh * vhat, avg)
        dv = rstd * (dvh - m1 - vhat * m2) * dvg
        dsgu_ref[:, 0:D_BRANCH] = du.astype(BF16)
        dsgu_ref[:, D_BRANCH:2 * D_BRANCH] = dv.astype(BF16)
        dsgu_ref[:, 2 * D_BRANCH:3 * D_BRANCH] = dzb.astype(BF16)

        @pl.when(i == n_steps - 1)
        def _():
            pos = lax.broadcasted_iota(jnp.int32, (SGU_CHUNK, SGU_CHUNK), 0) >> GROUP_SHIFT
            src = lax.broadcasted_iota(jnp.int32, (SGU_CHUNK, SGU_CHUNK), 1) >> GROUP_SHIFT
            keep = src <= pos
            for g in range(N_GROUPS):
                dw_ref[g] = jnp.where(keep, dw_ref[g], 0.0)
            grp = lax.broadcasted_iota(jnp.int32, (D_BRANCH, LANES), 0) >> GROUP_SHIFT
            sel = (grp == lax.broadcasted_iota(jnp.int32, (D_BRANCH, LANES), 1)).astype(BF16)
            db_ref[...] = _split_dot(dmix_acc[...], sel, 3)

    col = lambda c: pl.BlockSpec((tm, D_BRANCH), lambda i: (i, c))
    full = lambda shape: pl.BlockSpec(shape, lambda i: (0,) * len(shape))
    return pl.pallas_call(
        body, name="sgu_bwd", grid=(n_steps,),
        in_specs=[col(COL_UB), col(COL_VB), col(COL_ZB), pl.BlockSpec((tm, D_BRANCH), lambda i: (i, 0)),
                  full((1, D_BRANCH)), full((1, D_BRANCH)),
                  full((N_GROUPS, SGU_CHUNK, SGU_CHUNK)), full((N_GROUPS, SGU_CHUNK, SGU_CHUNK)),
                  full((SGU_CHUNK, D_BRANCH))],
        out_specs=[pl.BlockSpec((tm, 3 * D_BRANCH), lambda i: (i, 0)),
                   full((N_GROUPS, SGU_CHUNK, SGU_CHUNK)), full((SGU_CHUNK, LANES)),
                   full((1, D_BRANCH)), full((1, D_BRANCH))],
        out_shape=[jax.ShapeDtypeStruct((s, 3 * D_BRANCH), BF16),
                   jax.ShapeDtypeStruct((N_GROUPS, SGU_CHUNK, SGU_CHUNK), F32),
                   jax.ShapeDtypeStruct((SGU_CHUNK, LANES), F32),
                   jax.ShapeDtypeStruct((1, D_BRANCH), F32), jax.ShapeDtypeStruct((1, D_BRANCH), F32)],
        scratch_shapes=[pltpu.VMEM((SGU_CHUNK, D_BRANCH), F32)],
        compiler_params=_cparams(("arbitrary",)),
    )(proj, proj, proj, dyb, ln_g, ln_b, w_mask, w_mask_t, bias_full)


def _mid(proj, ya, yb, o, x, target, final_g, w_up_a, w_up_b, w_out):
    s = x.shape[0]
    tm = min(256, s)
    n_steps = s // tm
    half = D_MODEL // 2

    def body(ya_ref, yb_ref, o_ref, za_ref, ga0_ref, ga1_ref, gb0_ref, gb1_ref, x_ref, t_ref, gf_ref,
             wa_ref, wb_ref, wo_ref,
             dzg_ref, do_ref, dyb_ref, dx2_ref, gwo_ref, gwa_ref, gwb_ref, loss_ref, dgf_ref,
             acc_o, acc_a, acc_b):
        i = pl.program_id(0)

        @pl.when(i == 0)
        def _():
            acc_o[...] = jnp.zeros_like(acc_o)
            acc_a[...] = jnp.zeros_like(acc_a)
            acc_b[...] = jnp.zeros_like(acc_b)
            loss_ref[...] = jnp.zeros_like(loss_ref)
            dgf_ref[...] = jnp.zeros_like(dgf_ref)

        ya_v = ya_ref[...]
        yb_v = yb_ref[...]
        pa = jnp.dot(ya_v, wa_ref[...], preferred_element_type=F32)
        pb = jnp.dot(yb_v, wb_ref[...], preferred_element_type=F32)
        sa = _sigmoid(jnp.concatenate([ga0_ref[...], ga1_ref[...]], axis=1).astype(F32))
        sb = _sigmoid(jnp.concatenate([gb0_ref[...], gb1_ref[...]], axis=1).astype(F32))
        merged = (sa * pa + sb * pb).astype(BF16)
        x2 = x_ref[...] + jnp.dot(merged, wo_ref[...], preferred_element_type=F32)
        r2 = lax.rsqrt(jnp.mean(x2 * x2, axis=-1, keepdims=True) + EPS)
        xh = x2 * r2
        gf = gf_ref[...]
        diff = xh * gf - t_ref[...]
        loss_ref[...] += 0.5 * jnp.sum(jnp.mean(diff * diff, axis=-1, keepdims=True))
        dy = diff * (1.0 / D_MODEL)
        dgf_ref[...] += jnp.sum(dy * xh, axis=0, keepdims=True)
        dyg = dy * gf
        dx2 = r2 * (dyg - xh * jnp.mean(dyg * xh, axis=-1, keepdims=True))
        dx2_ref[...] = dx2
        dx2b = dx2.astype(BF16)
        dmerged = _dot_nt(dx2b, wo_ref[...])
        acc_o[...] += _dot_tn(merged, dx2b)
        dpa = dmerged * sa
        dpb = dmerged * sb
        dzg_ref[:, D_BRANCH:D_BRANCH + D_MODEL] = (dpa * pa * (1.0 - sa)).astype(BF16)
        dzg_ref[:, D_BRANCH + D_MODEL:D_BRANCH + 2 * D_MODEL] = (dpb * pb * (1.0 - sb)).astype(BF16)
        dpab = dpa.astype(BF16)
        dpbb = dpb.astype(BF16)
        acc_a[...] += _dot_tn(ya_v, dpab)
        acc_b[...] += _dot_tn(yb_v, dpbb)
        dya = _dot_nt(dpab, wa_ref[...])
        dyb_ref[...] = _dot_nt(dpbb, wb_ref[...]).astype(BF16)
        za = za_ref[...].astype(F32)
        sg = _sigmoid(za)
        do_ref[...] = (dya * (za * sg)).astype(BF16)
        dzg_ref[:, 0:D_BRANCH] = (dya * o_ref[...].astype(F32) * (sg * (1.0 + za * (1.0 - sg)))).astype(BF16)

        @pl.when(i == n_steps - 1)
        def _():
            gwo_ref[...] = acc_o[...].astype(BF16)
            gwa_ref[...] = acc_a[...].astype(BF16)
            gwb_ref[...] = acc_b[...].astype(BF16)

    tok = lambda w: pl.BlockSpec((tm, w), lambda i: (i, 0))
    col = lambda c: pl.BlockSpec((tm, half), lambda i: (i, c))
    full = lambda shape: pl.BlockSpec(shape, lambda i: (0,) * len(shape))
    return pl.pallas_call(
        body, name="mid", grid=(n_steps,),
        in_specs=[tok(D_BRANCH), tok(D_BRANCH), tok(D_BRANCH), col(COL_ZA), col(COL_GA), col(COL_GA + 1),
                  col(COL_GB), col(COL_GB + 1), tok(D_MODEL), tok(D_MODEL), full((1, D_MODEL)),
                  full((D_BRANCH, D_MODEL)), full((D_BRANCH, D_MODEL)), full((D_MODEL, D_MODEL))],
        out_specs=[tok(D_BRANCH + 2 * D_MODEL), tok(D_BRANCH), tok(D_BRANCH), tok(D_MODEL),
                   full((D_MODEL, D_MODEL)), full((D_BRANCH, D_MODEL)), full((D_BRANCH, D_MODEL)),
                   full((8, LANES)), full((1, D_MODEL))],
        out_shape=[jax.ShapeDtypeStruct((s, D_BRANCH + 2 * D_MODEL), BF16),
                   jax.ShapeDtypeStruct((s, D_BRANCH), BF16), jax.ShapeDtypeStruct((s, D_BRANCH), BF16),
                   jax.ShapeDtypeStruct((s, D_MODEL), F32),
                   jax.ShapeDtypeStruct((D_MODEL, D_MODEL), BF16),
                   jax.ShapeDtypeStruct((D_BRANCH, D_MODEL), BF16), jax.ShapeDtypeStruct((D_BRANCH, D_MODEL), BF16),
                   jax.ShapeDtypeStruct((8, LANES), F32), jax.ShapeDtypeStruct((1, D_MODEL), F32)],
        scratch_shapes=[pltpu.VMEM((D_MODEL, D_MODEL), F32), pltpu.VMEM((D_BRANCH, D_MODEL), F32),
                        pltpu.VMEM((D_BRANCH, D_MODEL), F32)],
        compiler_params=_cparams(("arbitrary",)),
    )(ya, yb, o, proj, proj, proj, proj, proj, x, target, final_g, w_up_a, w_up_b, w_out)


def _dwin_early(ht, first, tile_of_first, second, tile_of_second):
    s = ht.shape[1]
    n1 = first.shape[1] // D_BRANCH
    n2 = second.shape[1] // D_BRANCH

    def body(ht_ref, a_ref, b_ref, out_ref):
        j = pl.program_id(0)

        @pl.when(j < n1)
        def _():
            out_ref[...] = jnp.dot(ht_ref[...], a_ref[...], preferred_element_type=F32).astype(BF16)

        @pl.when(j >= n1)
        def _():
            out_ref[...] = jnp.dot(ht_ref[...], b_ref[...], preferred_element_type=F32).astype(BF16)

    return pl.pallas_call(
        body, name="dwin_early", grid=(n1 + n2,),
        in_specs=[pl.BlockSpec((D_MODEL, s), lambda j: (0, 0)),
                  pl.BlockSpec((s, D_BRANCH), lambda j: (0, jnp.minimum(j, n1 - 1))),
                  pl.BlockSpec((s, D_BRANCH), lambda j: (0, jnp.maximum(j - n1, 0)))],
        out_specs=pl.BlockSpec((D_MODEL, D_BRANCH),
                               lambda j: (0, jnp.where(j < n1, tile_of_first(j), tile_of_second(j - n1)))),
        out_shape=jax.ShapeDtypeStruct((D_MODEL, D_IN), BF16),
        compiler_params=_cparams(("arbitrary",)),
    )(ht, first, second)


def _dwin_pieces(ht, pieces, first_tile, prev):
    s = ht.shape[1]
    n = len(pieces)

    def body(ht_ref, *refs):
        srcs = refs[:n]
        out_ref, buf, sems = refs[n + 1:]
        j = pl.program_id(0)

        @pl.when(j == 0)
        def _():
            for k in range(n):
                pltpu.make_async_copy(srcs[k], buf.at[k], sems.at[k]).start()

        for k in range(n):
            @pl.when(j == k)
            def _():
                pltpu.make_async_copy(srcs[k], buf.at[k], sems.at[k]).wait()

        out_ref[...] = jnp.dot(ht_ref[...], buf[j], preferred_element_type=F32).astype(BF16)

    any_spec = pl.BlockSpec(memory_space=pl.ANY)
    return pl.pallas_call(
        body, name="dwin_pieces", grid=(n,),
        in_specs=[pl.BlockSpec((D_MODEL, s), lambda j: (0, 0))] + [any_spec] * (n + 1),
        out_specs=pl.BlockSpec((D_MODEL, D_BRANCH), lambda j: (0, first_tile + j)),
        out_shape=jax.ShapeDtypeStruct((D_MODEL, D_IN), BF16),
        scratch_shapes=[pltpu.VMEM((n, s, D_BRANCH), BF16), pltpu.SemaphoreType.DMA((n,))],
        input_output_aliases={n + 1: 0},
        compiler_params=_cparams(("arbitrary",)),
    )(ht, *pieces, prev)


def _dh_dx(pieces, w_in, x, norm_g, dx2, g_in):
    s = x.shape[0]
    tm = min(256, s)
    n_steps = s // tm
    arrays = []
    for arr, _, _, _ in pieces:
        if not any(arr is a for a in arrays):
            arrays.append(arr)
    n_arr = len(arrays)
    plan = [([k for k, a in enumerate(arrays) if a is arr][0], wcol, off, width) for arr, wcol, off, width in pieces]

    def body(*refs):
        p_refs = refs[:n_arr]
        w_ref, x_ref, g_ref, dx2_ref, gin_ref, dx_ref, dg_ref, late_ref = refs[n_arr:n_arr + 8]
        step = pl.program_id(0)
        finish = _presum_program(step == 0, step == min(1, n_steps - 1), step == n_steps - 1, 0, _LATE_CHIPS,
                                 gin_ref, late_ref, refs[n_arr + 8:])

        @pl.when(step == 0)
        def _():
            dg_ref[...] = jnp.zeros_like(dg_ref)

        dh = None
        for k, wcol, off, width in plan:
            d = _dot_nt(p_refs[k][:, off:off + width], w_ref[:, wcol:wcol + width])
            dh = d if dh is None else dh + d
        xf = x_ref[...]
        r = lax.rsqrt(jnp.mean(xf * xf, axis=-1, keepdims=True) + EPS)
        xh = xf * r
        dg_ref[...] += jnp.sum(dh * xh, axis=0, keepdims=True)
        dhg = dh * g_ref[...]
        dx_ref[...] = r * (dhg - xh * jnp.mean(dhg * xh, axis=-1, keepdims=True)) + dx2_ref[...]

        @pl.when(step == n_steps - 1)
        def _():
            finish()

    tok = lambda w: pl.BlockSpec((tm, w), lambda i: (i, 0))
    full = lambda shape: pl.BlockSpec(shape, lambda i: (0,) * len(shape))
    any_spec = pl.BlockSpec(memory_space=pl.ANY)
    return pl.pallas_call(
        body, name="dh_dx", grid=(n_steps,),
        in_specs=[tok(a.shape[1]) for a in arrays] + [full((D_MODEL, D_IN)), tok(D_MODEL), full((1, D_MODEL)),
                                                      tok(D_MODEL), any_spec],
        out_specs=[tok(D_MODEL), full((1, D_MODEL)), any_spec],
        out_shape=[jax.ShapeDtypeStruct((s, D_MODEL), F32), jax.ShapeDtypeStruct((1, D_MODEL), F32),
                   jax.ShapeDtypeStruct((N_DEV,) + _GRAD_PIECE_SHAPES[0], BF16)],
        scratch_shapes=_presum_scratch(0, _LATE_CHIPS),
        compiler_params=pltpu.CompilerParams(dimension_semantics=("arbitrary",), vmem_limit_bytes=VMEM_LIMIT,
                                             has_side_effects=True),
    )(*arrays, w_in, x, norm_g, dx2, g_in)


ADAM_GRID = 8


def _adamw(sets):
    n = len(sets)
    c1 = 1.0 - ADAM_B1 ** ADAM_STEP
    c2 = 1.0 - ADAM_B2 ** ADAM_STEP

    def body(*refs):
        for k in range(n):
            w_ref, g_ref, m_ref, v_ref = refs[4 * k:4 * k + 4]
            g_out_ref, d_ref, nm_ref, nv_ref = refs[4 * n + 4 * k:4 * n + 4 * k + 4]
            gv = g_ref[...]
            g_out_ref[...] = gv
            nm = ADAM_B1 * m_ref[...] + (1.0 - ADAM_B1) * gv
            nv = ADAM_B2 * v_ref[...] + (1.0 - ADAM_B2) * (gv * gv)
            d_ref[...] = -ADAM_LR * ((nm / c1) / (jnp.sqrt(nv / c2) + ADAM_EPS) + ADAM_WD * w_ref[...])
            nm_ref[...] = nm
            nv_ref[...] = nv

    in_specs, out_specs, shapes, args = [], [], [], []
    for w, g, m, v, g_tile0 in sets:
        rows, cols = w.shape
        assert rows % (8 * ADAM_GRID) == 0, (rows, cols)
        spec = pl.BlockSpec((rows // ADAM_GRID, cols), lambda i: (i, 0))
        g_spec = pl.BlockSpec((rows // ADAM_GRID, cols), lambda i, t0=g_tile0: (t0 + i, 0))
        in_specs += [spec, g_spec, spec, spec]
        out_specs += [spec] * 4
        shapes += [jax.ShapeDtypeStruct((rows, cols), F32)] * 4
        args += [w, g, m, v]
    outs = pl.pallas_call(
        body, name="adamw", grid=(ADAM_GRID,),
        in_specs=in_specs, out_specs=out_specs, out_shape=shapes,
        compiler_params=_cparams(("parallel",)),
    )(*args)
    return [outs[4 * k:4 * k + 4] for k in range(n)]


def _reduce_grads_tail(grads, g_small, early_slots, late_in_slots):
    n_big = len(grads)
    n_arr = n_big + 1
    shard_shapes = [(2 * r, w) for r, w in _GRAD_PIECE_SHAPES]
    small_piece = (SMALL_PIECE, LANES)

    def body(*refs):
        src = refs[:n_arr]
        early = refs[n_arr:n_arr + n_big]
        late_in = refs[n_arr + n_big]
        n_in = n_arr + n_big + 1
        out = refs[n_in:n_in + n_arr]
        slots = refs[n_in + n_arr:n_in + 2 * n_arr]
        sums = refs[n_in + 2 * n_arr:n_in + 3 * n_arr]
        send1, recv1, send2, recv2, local_sems = refs[n_in + 3 * n_arr:]
        x, y, c = _place()
        me = 4 * x + 2 * y + c

        def piece_of(a, dev):
            return src[a].at[dev] if a == n_big else _grad_piece(src[a], a, dev)

        def late(a, dst_dev, src_dev):
            return pltpu.make_async_remote_copy(
                src_ref=piece_of(a, dst_dev), dst_ref=slots[a].at[src_dev],
                send_sem=send1.at[n_arr * dst_dev + a], recv_sem=recv1.at[n_arr * src_dev + a],
                device_id=_dev_id(dst_dev), device_id_type=MESH)

        def late_arrays(dev):
            return (n_big,)

        def load(a, dev, received):
            return pltpu.make_async_copy(received.at[dev], slots[a].at[dev], local_sems.at[n_arr * dev + a])

        def own(a, dev):
            return pltpu.make_async_copy(piece_of(a, dev), slots[a].at[dev], local_sems.at[n_arr * dev + a])

        def half_of(a, core):
            r, _ = _GRAD_PIECE_SHAPES[a]
            return out[a].at[pl.ds(_aligned(core * r, 8), r), :]

        def share(a, core, sibling):
            return pltpu.make_async_remote_copy(src_ref=sums[a], dst_ref=half_of(a, core), send_sem=send2.at[a],
                                                recv_sem=recv2.at[a], device_id=sibling, device_id_type=MESH)

        def keep(a, dev):
            dst = out[n_big].at[dev] if a == n_big else half_of(a, dev % 2)
            return pltpu.make_async_copy(sums[a], dst, local_sems.at[N_DEV * n_arr + a])

        def small_share(dst_dev, src_dev):
            return pltpu.make_async_remote_copy(src_ref=sums[n_big], dst_ref=out[n_big].at[src_dev],
                                                send_sem=send2.at[n_big + dst_dev], recv_sem=recv2.at[n_big + src_dev],
                                                device_id=_dev_id(dst_dev), device_id_type=MESH)

        for dev in range(N_DEV):
            @pl.when(me == dev)
            def _():
                received = [early[0] if dev in _EARLY_IN_DEVS else late_in] + list(early[1:])
                for a in range(n_arr):
                    own(a, dev).start()
                sources = sorted([dev] + _presum_sources(dev))
                for peer in range(N_DEV):
                    if peer != dev:
                        for a in late_arrays(peer):
                            late(a, peer, dev).start()
                        for a in range(n_big):
                            if peer in sources:
                                load(a, peer, received[a]).start()
                for a in range(n_arr):
                    own(a, dev).wait()
                for peer in range(N_DEV):
                    if peer != dev:
                        for a in late_arrays(dev):
                            late(a, dev, peer).wait_recv()
                        for a in range(n_big):
                            if peer in sources:
                                load(a, peer, received[a]).wait()
                for a in [n_big] + list(range(n_big)):
                    rows = slots[a].shape[1]
                    step = 64 if rows % 64 == 0 else 8
                    used = sources if a < n_big else list(range(N_DEV))

                    def add_rows(t, carry, a=a, step=step, used=used):
                        r0 = pl.multiple_of(t * step, step)
                        total = slots[a][used[0], pl.ds(r0, step), :].astype(F32)
                        for src_dev in used[1:]:
                            total = total + slots[a][src_dev, pl.ds(r0, step), :].astype(F32)
                        sums[a][pl.ds(r0, step), :] = total
                        return carry

                    lax.fori_loop(0, rows // step, add_rows, 0)
                    keep(a, dev).start()
                    if a == n_big:
                        for peer in range(N_DEV):
                            if peer != dev:
                                small_share(peer, dev).start()
                    else:
                        share(a, dev % 2, _dev_id(dev ^ 1)).start()

        for a in range(n_big):
            share(a, 1 - c, (x, y, 1 - c)).wait_recv()
        for dev in range(N_DEV):
            @pl.when(me != dev)
            def _():
                small_share(dev, dev).wait_recv()
                small_share(dev, me).wait_send()
                for a in late_arrays(dev):
                    late(a, dev, me).wait_send()
        for a in range(n_big):
            share(a, c, (x, y, 1 - c)).wait_send()
        for dev in range(N_DEV):
            @pl.when(me == dev)
            def _():
                for a in range(n_arr):
                    keep(a, dev).wait()

    any_spec = pl.BlockSpec(memory_space=pl.ANY)
    return pl.pallas_call(
        body, name="reduce_grads_tail",
        in_specs=[any_spec] * (n_arr + n_big + 1), out_specs=[any_spec] * n_arr,
        out_shape=[jax.ShapeDtypeStruct(sh, F32) for sh in shard_shapes]
        + [jax.ShapeDtypeStruct((N_DEV,) + small_piece, F32)],
        scratch_shapes=[pltpu.VMEM((N_DEV,) + sh, BF16) for sh in _GRAD_PIECE_SHAPES]
        + [pltpu.VMEM((N_DEV,) + small_piece, F32)]
        + [pltpu.VMEM(sh, F32) for sh in _GRAD_PIECE_SHAPES] + [pltpu.VMEM(small_piece, F32)]
        + [pltpu.SemaphoreType.DMA((N_DEV * n_arr,)), pltpu.SemaphoreType.DMA((N_DEV * n_arr,)),
           pltpu.SemaphoreType.DMA((n_big + N_DEV,)), pltpu.SemaphoreType.DMA((n_big + N_DEV,)),
           pltpu.SemaphoreType.DMA((N_DEV * n_arr + n_arr,))],
        compiler_params=pltpu.CompilerParams(vmem_limit_bytes=VMEM_LIMIT, has_side_effects=True),
    )(*grads, g_small, *early_slots, late_in_slots)


W_SPATIAL_ROWS = N_GROUPS * SGU_CHUNK
_REST_PARTS = ("norm_g", "sgu_ln_g", "sgu_ln_b", "b_spatial", "final_norm_g")
REST_ROWS = SMALL_ROWS - W_SPATIAL_ROWS
_LOSS_ROW = W_SPATIAL_ROWS + 8 * len(_REST_PARTS)


def _pack_rest(parts, loss_tile=None):
    rows = []
    for name in _REST_PARTS:
        a = parts[name].reshape(-1, LANES).astype(F32)
        rows.append(jnp.pad(a, ((0, 8 - a.shape[0]), (0, 0))))
    rows.append(jnp.zeros((8, LANES), F32) if loss_tile is None else loss_tile)
    rows.append(jnp.zeros((REST_ROWS - 8 * len(rows), LANES), F32))
    return jnp.concatenate(rows, axis=0)


def _pack_small(parts, loss_tile):
    return jnp.concatenate([parts["w_spatial"].reshape(W_SPATIAL_ROWS, LANES), _pack_rest(parts, loss_tile)], axis=0)


def _unpack_rest(packed, shapes):
    out = {}
    for k, name in enumerate(_REST_PARTS):
        n = math.prod(shapes[name])
        out[name] = packed[8 * k:8 * k + n // LANES].reshape(shapes[name])
    return out


def _local_step(proj, ht, x, target, norm_g, w_in, sgu_ln_g, sgu_ln_b, w_spatial, b_spatial, w_up_a, w_up_b, w_out,
                final_norm_g, bq, bk):
    pos = jnp.arange(SGU_CHUNK)
    keep = (pos[None, :] // SGU_SUBCHUNK) <= (pos[:, None] // SGU_SUBCHUNK)
    w_mask = jnp.where(keep[None], w_spatial, 0.0).astype(BF16)
    w_mask_t = jnp.swapaxes(w_mask, 1, 2)
    bias_full = jnp.repeat(b_spatial.T, GROUP_DIM, axis=1)
    ln_g = sgu_ln_g.reshape(1, D_BRANCH)
    ln_b = sgu_ln_b.reshape(1, D_BRANCH)
    final_g = final_norm_g.reshape(1, D_MODEL)

    o, ya, rsave = _attn_fwd(proj, bq, bk, 2 * ATTN_PAIRS)
    yb = _sgu_fwd(proj, ln_g, ln_b, w_mask, bias_full)
    dzg, do, dyb, dx2, g_out, g_up_a, g_up_b, loss_acc, d_final = _mid(
        proj, ya, yb, o, x, target, final_g, w_up_a, w_up_b, w_out)
    dsgu, d_wsp, d_bsp, d_lng, d_lnb = _sgu_bwd(proj, dyb, ln_g, ln_b, w_mask, w_mask_t, bias_full)
    g_in = _dwin_early(ht, dzg, lambda j: jnp.where(j == 0, COL_ZA, COL_GA - 1 + j), dsgu, lambda j: COL_UB + j)
    dq, dk, dv, *early_slots = _attn_bwd(proj, do, rsave, bq, bk, ATTN_PAIRS, (g_in, g_up_a, g_up_b, g_out))
    g_in = _dwin_pieces(ht, (dq, dk, dv), COL_Q, g_in)
    pieces = [(dq, COL_Q * D_BRANCH, 0, D_BRANCH), (dk, COL_K * D_BRANCH, 0, D_BRANCH),
              (dv, COL_V * D_BRANCH, 0, D_BRANCH), (dzg, COL_ZA * D_BRANCH, 0, D_BRANCH),
              (dsgu, COL_UB * D_BRANCH, 0, 3 * D_BRANCH), (dzg, COL_GA * D_BRANCH, D_BRANCH, 2 * D_MODEL)]
    dx, d_norm, late_in_slots = _dh_dx(pieces, w_in, x, norm_g, dx2, g_in)
    small = {"norm_g": d_norm, "sgu_ln_g": d_lng, "sgu_ln_b": d_lnb, "w_spatial": d_wsp,
             "b_spatial": d_bsp[:, :N_GROUPS].T, "final_norm_g": d_final}
    return loss_acc, dx, (g_in, g_up_a, g_up_b, g_out), small, early_slots, late_in_slots


def kernel(x, norm_g, w_in, sgu_ln_g, sgu_ln_b, w_spatial, b_spatial, w_up_a, w_up_b, w_out, final_norm_g, loss_target, m_norm_g, m_w_in, m_sgu_ln_g, m_sgu_ln_b, m_w_spatial, m_b_spatial, m_w_up_a, m_w_up_b, m_w_out, m_final_norm_g, v_norm_g, v_w_in, v_sgu_ln_g, v_sgu_ln_b, v_w_spatial, v_b_spatial, v_w_up_a, v_w_up_b, v_w_out, v_final_norm_g):
    big_names = ("w_in", "w_up_a", "w_up_b", "w_out")
    names = ("norm_g", "w_in", "sgu_ln_g", "sgu_ln_b", "w_spatial", "b_spatial", "w_up_a", "w_up_b", "w_out",
             "final_norm_g")
    w = dict(norm_g=norm_g, w_in=w_in, sgu_ln_g=sgu_ln_g, sgu_ln_b=sgu_ln_b, w_spatial=w_spatial,
             b_spatial=b_spatial, w_up_a=w_up_a, w_up_b=w_up_b, w_out=w_out, final_norm_g=final_norm_g)
    m = dict(norm_g=m_norm_g, w_in=m_w_in, sgu_ln_g=m_sgu_ln_g, sgu_ln_b=m_sgu_ln_b, w_spatial=m_w_spatial,
             b_spatial=m_b_spatial, w_up_a=m_w_up_a, w_up_b=m_w_up_b, w_out=m_w_out, final_norm_g=m_final_norm_g)
    v = dict(norm_g=v_norm_g, w_in=v_w_in, sgu_ln_g=v_sgu_ln_g, sgu_ln_b=v_sgu_ln_b, w_spatial=v_w_spatial,
             b_spatial=v_b_spatial, w_up_a=v_w_up_a, w_up_b=v_w_up_b, w_out=v_w_out, final_norm_g=v_final_norm_g)
    shapes = {n: w[n].shape for n in names}
    flat2d = lambda a: a.reshape(a.shape[-2:])

    proj, ht, *full = _in_proj_gather(x[0], norm_g, *[flat2d(w[n]) for n in big_names])
    loss, dx, big_grads, small, early_slots, late_in_slots = _local_step(
        proj, ht, x[0], loss_target[0], norm_g, full[0], sgu_ln_g[0], sgu_ln_b[0], w_spatial[0], b_spatial[0],
        full[1], full[2], full[3], final_norm_g, ATTN_Q_BLOCK, ATTN_K_BLOCK)
    packed = _pack_small(small, loss).reshape(N_DEV, SMALL_PIECE, LANES)
    red = _reduce_grads_tail(big_grads, packed, early_slots, late_in_slots)

    grads, deltas, new_m, new_v = {}, {}, {}, {}
    g_small = red[4].reshape(SMALL_ROWS, LANES)
    rows2d = lambda a: a.reshape(W_SPATIAL_ROWS, LANES)
    sets = [(flat2d(w[n]), g, flat2d(m[n]), flat2d(v[n]), 0) for n, g in zip(big_names, red[:4])]
    sets.append((rows2d(w_spatial), g_small, rows2d(m_w_spatial), rows2d(v_w_spatial), 0))
    sets.append((_pack_rest(w), g_small, _pack_rest(m), _pack_rest(v), W_SPATIAL_ROWS * ADAM_GRID // REST_ROWS))
    updated = _adamw(sets)
    for n, (g, d, nm, nv) in zip(big_names + ("w_spatial",), updated[:5]):
        grads[n], deltas[n], new_m[n], new_v[n] = (a.reshape(shapes[n]) for a in (g, d, nm, nv))
    g_rest, d, nm, nv = updated[5]
    for src, dst in ((g_rest, grads), (d, deltas), (nm, new_m), (nv, new_v)):
        dst.update(_unpack_rest(src, shapes))

    return (g_rest[_LOSS_ROW - W_SPATIAL_ROWS, 0], dx[None], *[grads[n] for n in names], *[deltas[n] for n in names],
            *[new_m[n] for n in names], *[new_v[n] for n in names])
```
